```python
import jax, jax.numpy as jnp
from jax import lax
import numpy as np

D_MODEL = 2048
BATCH = 8
SEQ = 4096
DEPTH = 2

HEAD_DIM = 128
ROPE_THETA = 10000.0
GRID_W = 64
Q_BLOCK = 128
EPS = 1e-6
NEG = -1e30

A_HEADS = 4
A_Q_RANK = 512
A_KV_RANK = 512
A_NOPE = 128
A_ROPE = 64
A_V = 128
B_HEADS = 6
B_PATTERNS = ((128, 1), (512, 4), (2048, 16))
B_BLOCK = 64
C_HEADS = 6
C_KV_HEADS = 2
C_GROUP = C_HEADS // C_KV_HEADS

A_WIDTH = A_HEADS * A_V
B_WIDTH = B_HEADS * HEAD_DIM
C_WIDTH = C_HEADS * HEAD_DIM
MIX_WIDTH = A_WIDTH + B_WIDTH + C_WIDTH

IN_A = A_Q_RANK + A_KV_RANK + A_ROPE
IN_B = 3 * B_WIDTH
IN_C = C_WIDTH + 2 * C_KV_HEADS * HEAD_DIM
IN_WIDTH = IN_A + IN_B + IN_C

D_FF = -(-8 * D_MODEL // (3 * 256)) * 256

kernel_name = "hymba_mla_dilated_axial_gqa_encoder"


def rms_norm(x, g):
    xf = x.astype(jnp.float32)
    y = xf * lax.rsqrt(jnp.mean(xf * xf, axis=-1, keepdims=True) + EPS)
    return (y * g.astype(jnp.float32)).astype(x.dtype)


def rope_angles(pos, dim):
    inv = ROPE_THETA ** (-jnp.arange(0, dim, 2, dtype=jnp.float32) / dim)
    return pos.astype(jnp.float32)[:, None] * inv[None, :]


def apply_rope(x, ang):
    cos = jnp.cos(ang)[:, None, :]
    sin = jnp.sin(ang)[:, None, :]
    x1, x2 = jnp.split(x.astype(jnp.float32), 2, axis=-1)
    out = jnp.concatenate([x1 * cos - x2 * sin, x2 * cos + x1 * sin], axis=-1)
    return out.astype(x.dtype)


def dense_block_attention(q, k, v, scale):
    B, S, Hkv, G, Dk = q.shape
    nb = S // Q_BLOCK
    qb = jnp.moveaxis(q.reshape(B, nb, Q_BLOCK, Hkv, G, Dk), 1, 0)

    def attend(qblk):
        s = jnp.einsum('bqhgd,bkhd->bhgqk', qblk, k, preferred_element_type=jnp.float32) * scale
        p = jax.nn.softmax(s, axis=-1).astype(v.dtype)
        return jnp.einsum('bhgqk,bkhd->bqhgd', p, v)

    o = lax.map(attend, qb)
    return jnp.moveaxis(o, 0, 1).reshape(B, S, Hkv, G, v.shape[-1])


def dilated_pattern(q, k, v, window, dilation):
    B, S, H, D = q.shape
    half = window // (2 * dilation)
    L = S // dilation
    nb = -(-L // B_BLOCK)
    Lp = nb * B_BLOCK

    def to_res(t):
        return jnp.moveaxis(t.reshape(B, L, dilation, H, D), 2, 1)

    qr = jnp.pad(to_res(q), ((0, 0), (0, 0), (0, Lp - L), (0, 0), (0, 0)))
    kpad = ((0, 0), (0, 0), (B_BLOCK, Lp - L + B_BLOCK), (0, 0), (0, 0))
    kr = jnp.pad(to_res(k), kpad).reshape(B, dilation, nb + 2, B_BLOCK, H, D)
    vr = jnp.pad(to_res(v), kpad).reshape(B, dilation, nb + 2, B_BLOCK, H, D)
    qb = qr.reshape(B, dilation, nb, B_BLOCK, H, D)
    kb = jnp.concatenate([kr[:, :, :-2], kr[:, :, 1:-1], kr[:, :, 2:]], axis=3)
    vb = jnp.concatenate([vr[:, :, :-2], vr[:, :, 1:-1], vr[:, :, 2:]], axis=3)

    qi = jnp.arange(nb)[:, None] * B_BLOCK + jnp.arange(B_BLOCK)[None, :]
    kj = jnp.arange(nb)[:, None] * B_BLOCK - B_BLOCK + jnp.arange(3 * B_BLOCK)[None, :]
    rel = kj[:, None, :] - qi[:, :, None]
    mask = (jnp.abs(rel) <= half) & (kj[:, None, :] >= 0) & (kj[:, None, :] < L)

    s = jnp.einsum('brnqhd,brnkhd->brnhqk', qb, kb, preferred_element_type=jnp.float32) * (D ** -0.5)
    s = jnp.where(mask[:, None, :, :], s, NEG)
    m = jnp.max(s, axis=-1, keepdims=True)
    e = jnp.exp(s - m)
    den = jnp.sum(e, axis=-1)
    o = jnp.einsum('brnhqk,brnkhd->brnqhd', e.astype(v.dtype), vb, preferred_element_type=jnp.float32)
    o = o / jnp.moveaxis(den, -1, -2)[..., None]
    lse = jnp.moveaxis(m[..., 0] + jnp.log(den), -1, -2)

    def from_res(t):
        t = t.reshape((B, dilation, Lp) + t.shape[4:])[:, :, :L]
        return jnp.moveaxis(t, 1, 2).reshape((B, S) + t.shape[3:])

    return from_res(o), from_res(lse)


def mixer_mla(pa, q_norm, w_uq, kv_norm, w_ukv, ang_a):
    B, S, _ = pa.shape
    c_q, c_kv, k_rope = jnp.split(pa, [A_Q_RANK, A_Q_RANK + A_KV_RANK], axis=-1)
    q = (rms_norm(c_q, q_norm) @ w_uq).reshape(B, S, A_HEADS, A_NOPE + A_ROPE)
    q_nope, q_rope = jnp.split(q, [A_NOPE], axis=-1)
    q_rope = apply_rope(q_rope, ang_a)
    k_rope = apply_rope(k_rope[:, :, None, :], ang_a)
    kv = (rms_norm(c_kv, kv_norm) @ w_ukv).reshape(B, S, A_HEADS, A_NOPE + A_V)
    k_nope, v = jnp.split(kv, [A_NOPE], axis=-1)
    q_full = jnp.concatenate([q_nope, q_rope], axis=-1)[:, :, :, None, :]
    k_full = jnp.concatenate([k_nope, jnp.broadcast_to(k_rope, (B, S, A_HEADS, A_ROPE))], axis=-1)
    o = dense_block_attention(q_full, k_full, v, (A_NOPE + A_ROPE) ** -0.5)
    return o.reshape(B, S, A_WIDTH)


def mixer_dilated(pb, ang_1d):
    B, S, _ = pb.shape
    q, k, v = [t.reshape(B, S, B_HEADS, HEAD_DIM) for t in jnp.split(pb, 3, axis=-1)]
    q = apply_rope(q, ang_1d)
    k = apply_rope(k, ang_1d)
    outs, lses = [], []
    for window, dilation in B_PATTERNS:
        o, lse = dilated_pattern(q, k, v, window, dilation)
        outs.append(o)
        lses.append(lse)
    w = jax.nn.softmax(jnp.stack(lses, axis=0), axis=0)
    o = jnp.sum(w[..., None] * jnp.stack(outs, axis=0), axis=0)
    return o.astype(pb.dtype).reshape(B, S, B_WIDTH)


def mixer_axial_gqa(pc, q_norm, k_norm, ang_row, ang_col):
    B, S, _ = pc.shape
    q, k, v = jnp.split(pc, [C_WIDTH, C_WIDTH + C_KV_HEADS * HEAD_DIM], axis=-1)
    q = rms_norm(q.reshape(B, S, C_HEADS, HEAD_DIM), q_norm)
    k = rms_norm(k.reshape(B, S, C_KV_HEADS, HEAD_DIM), k_norm)
    v = v.reshape(B, S, C_KV_HEADS, HEAD_DIM)
    hd = HEAD_DIM // 2

    def axial(t):
        return jnp.concatenate([apply_rope(t[..., :hd], ang_row), apply_rope(t[..., hd:], ang_col)], axis=-1)

    q = axial(q).reshape(B, S, C_KV_HEADS, C_GROUP, HEAD_DIM)
    k = axial(k)
    o = dense_block_attention(q, k, v, HEAD_DIM ** -0.5)
    return o.reshape(B, S, C_WIDTH)


def _fwd_setup_inputs(seed: int = 0) -> dict:
    key = jax.random.key(seed)
    ks = jax.random.split(key, 20)
    f32 = jnp.float32

    def nrm(k, shape, scale):
        return jax.random.normal(k, shape, f32) * scale

    def gain(k, shape):
        return 1.0 + 0.02 * jax.random.normal(k, shape, f32)

    return {
        "x": jax.random.normal(ks[0], (BATCH, SEQ, D_MODEL), f32),
        "attn_norm": gain(ks[1], (DEPTH, D_MODEL)),
        "w_in": nrm(ks[2], (DEPTH, D_MODEL, IN_WIDTH), D_MODEL ** -0.5),
        "a_q_norm": gain(ks[3], (DEPTH, A_Q_RANK)),
        "a_w_uq": nrm(ks[4], (DEPTH, A_Q_RANK, A_HEADS * (A_NOPE + A_ROPE)), A_Q_RANK ** -0.5),
        "a_kv_norm": gain(ks[5], (DEPTH, A_KV_RANK)),
        "a_w_ukv": nrm(ks[6], (DEPTH, A_KV_RANK, A_HEADS * (A_NOPE + A_V)), A_KV_RANK ** -0.5),
        "c_q_norm": gain(ks[7], (DEPTH, HEAD_DIM)),
        "c_k_norm": gain(ks[8], (DEPTH, HEAD_DIM)),
        "out_norm": gain(ks[9], (DEPTH, MIX_WIDTH)),
        "w_out": nrm(ks[10], (DEPTH, MIX_WIDTH, D_MODEL), MIX_WIDTH ** -0.5),
        "ffn_norm": gain(ks[11], (DEPTH, D_MODEL)),
        "w_gate": nrm(ks[12], (DEPTH, D_MODEL, D_FF), D_MODEL ** -0.5),
        "w_up": nrm(ks[13], (DEPTH, D_MODEL, D_FF), D_MODEL ** -0.5),
        "w_down": nrm(ks[14], (DEPTH, D_FF, D_MODEL), D_FF ** -0.5),
        "final_norm": gain(ks[15], (D_MODEL,)),
    }


def _fwd_reference(x, attn_norm, w_in, a_q_norm, a_w_uq, a_kv_norm, a_w_ukv, c_q_norm, c_k_norm,
              out_norm, w_out, ffn_norm, w_gate, w_up, w_down, final_norm):
    B, S, _ = x.shape
    rows = S // GRID_W
    pos = jnp.arange(S, dtype=jnp.int32)
    row = jnp.repeat(jnp.arange(rows, dtype=jnp.int32), GRID_W)
    col = jnp.tile(jnp.arange(GRID_W, dtype=jnp.int32), rows)
    ang_1d = rope_angles(pos, HEAD_DIM)
    ang_a = rope_angles(pos, A_ROPE)
    ang_row = rope_angles(row, HEAD_DIM // 2)
    ang_col = rope_angles(col, HEAD_DIM // 2)

    for l in range(DEPTH):
        h = rms_norm(x, attn_norm[l])
        proj = jnp.einsum('bsd,de->bse', h, w_in[l])
        pa, pb, pc = jnp.split(proj, [IN_A, IN_A + IN_B], axis=-1)
        ya = mixer_mla(pa, a_q_norm[l], a_w_uq[l], a_kv_norm[l], a_w_ukv[l], ang_a)
        yb = mixer_dilated(pb, ang_1d)
        yc = mixer_axial_gqa(pc, c_q_norm[l], c_k_norm[l], ang_row, ang_col)
        g = out_norm[l]
        y = jnp.concatenate([
            rms_norm(ya, g[:A_WIDTH]),
            rms_norm(yb, g[A_WIDTH:A_WIDTH + B_WIDTH]),
            rms_norm(yc, g[A_WIDTH + B_WIDTH:]),
        ], axis=-1).astype(x.dtype)
        x = x + jnp.einsum('bse,ed->bsd', y, w_out[l])
        h = rms_norm(x, ffn_norm[l])
        ff = jax.nn.silu(h @ w_gate[l]) * (h @ w_up[l])
        x = x + ff @ w_down[l]
    return rms_norm(x, final_norm)


import jax as _jax
import jax.numpy as _jnp

TWIN_FORMAT = 'train_step'
FWD_PARAMS = ['x', 'attn_norm', 'w_in', 'a_q_norm', 'a_w_uq', 'a_kv_norm', 'a_w_ukv', 'c_q_norm', 'c_k_norm', 'out_norm', 'w_out', 'ffn_norm', 'w_gate', 'w_up', 'w_down', 'final_norm']
TWIN_WEIGHTS = ['attn_norm', 'w_in', 'a_q_norm', 'a_w_uq', 'a_kv_norm', 'a_w_ukv', 'c_q_norm', 'c_k_norm', 'out_norm', 'w_out', 'ffn_norm', 'w_gate', 'w_up', 'w_down', 'final_norm']
TWIN_DIFF_INPUT = 'x'
TWIN_INPUTS = ['x', 'attn_norm', 'w_in', 'a_q_norm', 'a_w_uq', 'a_kv_norm', 'a_w_ukv', 'c_q_norm', 'c_k_norm', 'out_norm', 'w_out', 'ffn_norm', 'w_gate', 'w_up', 'w_down', 'final_norm', 'loss_target', 'm_attn_norm', 'm_w_in', 'm_a_q_norm', 'm_a_w_uq', 'm_a_kv_norm', 'm_a_w_ukv', 'm_c_q_norm', 'm_c_k_norm', 'm_out_norm', 'm_w_out', 'm_ffn_norm', 'm_w_gate', 'm_w_up', 'm_w_down', 'm_final_norm', 'v_attn_norm', 'v_w_in', 'v_a_q_norm', 'v_a_w_uq', 'v_a_kv_norm', 'v_a_w_ukv', 'v_c_q_norm', 'v_c_k_norm', 'v_out_norm', 'v_w_out', 'v_ffn_norm', 'v_w_gate', 'v_w_up', 'v_w_down', 'v_final_norm']
TWIN_OUTPUTS = ['loss', 'grad_x', 'grad_attn_norm', 'grad_w_in', 'grad_a_q_norm', 'grad_a_w_uq', 'grad_a_kv_norm', 'grad_a_w_ukv', 'grad_c_q_norm', 'grad_c_k_norm', 'grad_out_norm', 'grad_w_out', 'grad_ffn_norm', 'grad_w_gate', 'grad_w_up', 'grad_w_down', 'grad_final_norm', 'delta_attn_norm', 'delta_w_in', 'delta_a_q_norm', 'delta_a_w_uq', 'delta_a_kv_norm', 'delta_a_w_ukv', 'delta_c_q_norm', 'delta_c_k_norm', 'delta_out_norm', 'delta_w_out', 'delta_ffn_norm', 'delta_w_gate', 'delta_w_up', 'delta_w_down', 'delta_final_norm', 'new_m_attn_norm', 'new_m_w_in', 'new_m_a_q_norm', 'new_m_a_w_uq', 'new_m_a_kv_norm', 'new_m_a_w_ukv', 'new_m_c_q_norm', 'new_m_c_k_norm', 'new_m_out_norm', 'new_m_w_out', 'new_m_ffn_norm', 'new_m_w_gate', 'new_m_w_up', 'new_m_w_down', 'new_m_final_norm', 'new_v_attn_norm', 'new_v_w_in', 'new_v_a_q_norm', 'new_v_a_w_uq', 'new_v_a_kv_norm', 'new_v_a_w_ukv', 'new_v_c_q_norm', 'new_v_c_k_norm', 'new_v_out_norm', 'new_v_w_out', 'new_v_ffn_norm', 'new_v_w_gate', 'new_v_w_up', 'new_v_w_down', 'new_v_final_norm']
TWIN_LEAF_KINDS = {'loss': 'loss', 'grad_x': 'grad_x', 'grad_attn_norm': 'grad_w', 'grad_w_in': 'grad_w', 'grad_a_q_norm': 'grad_w', 'grad_a_w_uq': 'grad_w', 'grad_a_kv_norm': 'grad_w', 'grad_a_w_ukv': 'grad_w', 'grad_c_q_norm': 'grad_w', 'grad_c_k_norm': 'grad_w', 'grad_out_norm': 'grad_w', 'grad_w_out': 'grad_w', 'grad_ffn_norm': 'grad_w', 'grad_w_gate': 'grad_w', 'grad_w_up': 'grad_w', 'grad_w_down': 'grad_w', 'grad_final_norm': 'grad_w', 'delta_attn_norm': 'delta_w', 'delta_w_in': 'delta_w', 'delta_a_q_norm': 'delta_w', 'delta_a_w_uq': 'delta_w', 'delta_a_kv_norm': 'delta_w', 'delta_a_w_ukv': 'delta_w', 'delta_c_q_norm': 'delta_w', 'delta_c_k_norm': 'delta_w', 'delta_out_norm': 'delta_w', 'delta_w_out': 'delta_w', 'delta_ffn_norm': 'delta_w', 'delta_w_gate': 'delta_w', 'delta_w_up': 'delta_w', 'delta_w_down': 'delta_w', 'delta_final_norm': 'delta_w', 'new_m_attn_norm': 'new_m', 'new_m_w_in': 'new_m', 'new_m_a_q_norm': 'new_m', 'new_m_a_w_uq': 'new_m', 'new_m_a_kv_norm': 'new_m', 'new_m_a_w_ukv': 'new_m', 'new_m_c_q_norm': 'new_m', 'new_m_c_k_norm': 'new_m', 'new_m_out_norm': 'new_m', 'new_m_w_out': 'new_m', 'new_m_ffn_norm': 'new_m', 'new_m_w_gate': 'new_m', 'new_m_w_up': 'new_m', 'new_m_w_down': 'new_m', 'new_m_final_norm': 'new_m', 'new_v_attn_norm': 'new_v', 'new_v_w_in': 'new_v', 'new_v_a_q_norm': 'new_v', 'new_v_a_w_uq': 'new_v', 'new_v_a_kv_norm': 'new_v', 'new_v_a_w_ukv': 'new_v', 'new_v_c_q_norm': 'new_v', 'new_v_c_k_norm': 'new_v', 'new_v_out_norm': 'new_v', 'new_v_w_out': 'new_v', 'new_v_ffn_norm': 'new_v', 'new_v_w_gate': 'new_v', 'new_v_w_up': 'new_v', 'new_v_w_down': 'new_v', 'new_v_final_norm': 'new_v'}


def _forward(args):
    return _fwd_reference(*[args[k] for k in FWD_PARAMS])


def _output_shape():
    def fwd():
        inp = _fwd_setup_inputs(0)
        return _fwd_reference(*[inp[k] for k in FWD_PARAMS])
    out = _jax.eval_shape(fwd)
    return out.shape, out.dtype

N_MICROBATCH = 1
ADAM_LR = 0.001
ADAM_B1 = 0.9
ADAM_B2 = 0.999
ADAM_EPS = 1e-08
ADAM_WD = 0.01
ADAM_STEP = 10
PER_EXAMPLE_BATCH_AXIS = {'x': 0, 'loss_target': 0}
SHARED_INPUTS = []
_WEIGHT_DTYPES = {'attn_norm': _jnp.float32, 'w_in': _jnp.float32, 'a_q_norm': _jnp.float32, 'a_w_uq': _jnp.float32, 'a_kv_norm': _jnp.float32, 'a_w_ukv': _jnp.float32, 'c_q_norm': _jnp.float32, 'c_k_norm': _jnp.float32, 'out_norm': _jnp.float32, 'w_out': _jnp.float32, 'ffn_norm': _jnp.float32, 'w_gate': _jnp.float32, 'w_up': _jnp.float32, 'w_down': _jnp.float32, 'final_norm': _jnp.float32}
MOMENT_SCALE = {'attn_norm': 9.583380e-02, 'w_in': 6.429427e-02, 'a_q_norm': 4.421088e-02, 'a_w_uq': 3.563635e-02, 'a_kv_norm': 9.703243e-02, 'a_w_ukv': 6.459683e-02, 'c_q_norm': 1.108976e-01, 'c_k_norm': 1.052426e-01, 'out_norm': 8.187569e-02, 'w_out': 8.160103e-02, 'ffn_norm': 4.400257e-02, 'w_gate': 1.875032e-02, 'w_up': 1.878526e-02, 'w_down': 3.107361e-02, 'final_norm': 1.614845e+01}


def _to_microbatches(a, axis):
    t = _jnp.moveaxis(a, axis, 0)
    t = t.reshape((N_MICROBATCH, t.shape[0] // N_MICROBATCH) + t.shape[1:])
    return _jnp.moveaxis(t, 1, axis + 1)


def setup_inputs(seed: int = 0) -> dict:
    inp = _fwd_setup_inputs(seed)
    key = _jax.random.fold_in(_jax.random.key(seed), 7919)
    shape, _ = _output_shape()
    out = dict(inp)
    out["loss_target"] = _jax.random.normal(_jax.random.fold_in(key, 0), shape, _jnp.float32)
    for i, name in enumerate(TWIN_WEIGHTS):
        w = inp[name].astype(_jnp.float32)
        if MOMENT_SCALE is None:
            s = _jnp.sqrt(_jnp.mean(_jnp.square(w)) + 1e-30)
        else:
            s = MOMENT_SCALE[name]
        km, kv = _jax.random.split(_jax.random.fold_in(key, i + 1))
        out[name] = w
        out["m_" + name] = s * _jax.random.normal(km, w.shape, _jnp.float32)
        out["v_" + name] = (s * s) * _jax.random.uniform(kv, w.shape, _jnp.float32, 0.5, 1.5)
    if N_MICROBATCH > 1:
        for name, axis in PER_EXAMPLE_BATCH_AXIS.items():
            out[name] = _to_microbatches(out[name], axis)
    return {'x': out['x'], 'attn_norm': out['attn_norm'], 'w_in': out['w_in'], 'a_q_norm': out['a_q_norm'], 'a_w_uq': out['a_w_uq'], 'a_kv_norm': out['a_kv_norm'], 'a_w_ukv': out['a_w_ukv'], 'c_q_norm': out['c_q_norm'], 'c_k_norm': out['c_k_norm'], 'out_norm': out['out_norm'], 'w_out': out['w_out'], 'ffn_norm': out['ffn_norm'], 'w_gate': out['w_gate'], 'w_up': out['w_up'], 'w_down': out['w_down'], 'final_norm': out['final_norm'], 'loss_target': out['loss_target'], 'm_attn_norm': out['m_attn_norm'], 'm_w_in': out['m_w_in'], 'm_a_q_norm': out['m_a_q_norm'], 'm_a_w_uq': out['m_a_w_uq'], 'm_a_kv_norm': out['m_a_kv_norm'], 'm_a_w_ukv': out['m_a_w_ukv'], 'm_c_q_norm': out['m_c_q_norm'], 'm_c_k_norm': out['m_c_k_norm'], 'm_out_norm': out['m_out_norm'], 'm_w_out': out['m_w_out'], 'm_ffn_norm': out['m_ffn_norm'], 'm_w_gate': out['m_w_gate'], 'm_w_up': out['m_w_up'], 'm_w_down': out['m_w_down'], 'm_final_norm': out['m_final_norm'], 'v_attn_norm': out['v_attn_norm'], 'v_w_in': out['v_w_in'], 'v_a_q_norm': out['v_a_q_norm'], 'v_a_w_uq': out['v_a_w_uq'], 'v_a_kv_norm': out['v_a_kv_norm'], 'v_a_w_ukv': out['v_a_w_ukv'], 'v_c_q_norm': out['v_c_q_norm'], 'v_c_k_norm': out['v_c_k_norm'], 'v_out_norm': out['v_out_norm'], 'v_w_out': out['v_w_out'], 'v_ffn_norm': out['v_ffn_norm'], 'v_w_gate': out['v_w_gate'], 'v_w_up': out['v_w_up'], 'v_w_down': out['v_w_down'], 'v_final_norm': out['v_final_norm']}


def _loss(weights, diff, rest, loss_target):
    with _jax.named_scope("forward"):
        args = {**rest, TWIN_DIFF_INPUT: diff, **{k: w.astype(_WEIGHT_DTYPES[k]) for k, w in weights.items()}}
        y = _forward(args)
    with _jax.named_scope("loss_head"):
        err = _jnp.square(y.astype(_jnp.float32) - loss_target)
        return 0.5 * _jnp.sum(_jnp.mean(err, axis=-1)) if err.ndim else 0.5 * err


def _adamw(w, g, m, v):
    m = ADAM_B1 * m + (1.0 - ADAM_B1) * g
    v = ADAM_B2 * v + (1.0 - ADAM_B2) * _jnp.square(g)
    m_hat = m / (1.0 - ADAM_B1 ** ADAM_STEP)
    v_hat = v / (1.0 - ADAM_B2 ** ADAM_STEP)
    delta = -ADAM_LR * (m_hat / (_jnp.sqrt(v_hat) + ADAM_EPS) + ADAM_WD * w)
    return delta, m, v


def reference(x, attn_norm, w_in, a_q_norm, a_w_uq, a_kv_norm, a_w_ukv, c_q_norm, c_k_norm, out_norm, w_out, ffn_norm, w_gate, w_up, w_down, final_norm, loss_target, m_attn_norm, m_w_in, m_a_q_norm, m_a_w_uq, m_a_kv_norm, m_a_w_ukv, m_c_q_norm, m_c_k_norm, m_out_norm, m_w_out, m_ffn_norm, m_w_gate, m_w_up, m_w_down, m_final_norm, v_attn_norm, v_w_in, v_a_q_norm, v_a_w_uq, v_a_kv_norm, v_a_w_ukv, v_c_q_norm, v_c_k_norm, v_out_norm, v_w_out, v_ffn_norm, v_w_gate, v_w_up, v_w_down, v_final_norm):
    given = dict(x=x, attn_norm=attn_norm, w_in=w_in, a_q_norm=a_q_norm, a_w_uq=a_w_uq, a_kv_norm=a_kv_norm, a_w_ukv=a_w_ukv, c_q_norm=c_q_norm, c_k_norm=c_k_norm, out_norm=out_norm, w_out=w_out, ffn_norm=ffn_norm, w_gate=w_gate, w_up=w_up, w_down=w_down, final_norm=final_norm, loss_target=loss_target, m_attn_norm=m_attn_norm, m_w_in=m_w_in, m_a_q_norm=m_a_q_norm, m_a_w_uq=m_a_w_uq, m_a_kv_norm=m_a_kv_norm, m_a_w_ukv=m_a_w_ukv, m_c_q_norm=m_c_q_norm, m_c_k_norm=m_c_k_norm, m_out_norm=m_out_norm, m_w_out=m_w_out, m_ffn_norm=m_ffn_norm, m_w_gate=m_w_gate, m_w_up=m_w_up, m_w_down=m_w_down, m_final_norm=m_final_norm, v_attn_norm=v_attn_norm, v_w_in=v_w_in, v_a_q_norm=v_a_q_norm, v_a_w_uq=v_a_w_uq, v_a_kv_norm=v_a_kv_norm, v_a_w_ukv=v_a_w_ukv, v_c_q_norm=v_c_q_norm, v_c_k_norm=v_c_k_norm, v_out_norm=v_out_norm, v_w_out=v_w_out, v_ffn_norm=v_ffn_norm, v_w_gate=v_w_gate, v_w_up=v_w_up, v_w_down=v_w_down, v_final_norm=v_final_norm)
    weights = {n: given[n] for n in TWIN_WEIGHTS}
    shared = {n: given[n] for n in SHARED_INPUTS}
    per_example = {n: given[n] for n in ['x']}
    grad_fn = _jax.value_and_grad(_loss, argnums=(0, 1))

    def one_microbatch(ex, loss_target):
        ex = dict(ex)
        diff = ex.pop(TWIN_DIFF_INPUT)
        return grad_fn(weights, diff, {**shared, **ex}, loss_target)

    if N_MICROBATCH == 1:
        loss, (grad_w, grad_x) = one_microbatch(per_example, given["loss_target"])
    else:
        def body(carry, xs):
            loss_sum, grad_sum = carry
            l_k, (gw_k, gx_k) = one_microbatch(xs[0], xs[1])
            with _jax.named_scope("update"):
                return (loss_sum + l_k, _jax.tree.map(_jnp.add, grad_sum, gw_k)), gx_k

        init = (_jnp.zeros((), _jnp.float32), _jax.tree.map(_jnp.zeros_like, weights))
        (loss, grad_w), grad_x = _jax.lax.scan(body, init, (per_example, given["loss_target"]))
    with _jax.named_scope("update"):
        delta_w, new_m, new_v = {}, {}, {}
        for n in TWIN_WEIGHTS:
            delta_w[n], new_m[n], new_v[n] = _adamw(weights[n], grad_w[n], given["m_" + n], given["v_" + n])
    return (loss, grad_x, *[grad_w[n] for n in TWIN_WEIGHTS], *[delta_w[n] for n in TWIN_WEIGHTS],
            *[new_m[n] for n in TWIN_WEIGHTS], *[new_v[n] for n in TWIN_WEIGHTS])
```

```python
import functools
import math

import jax
import jax.numpy as jnp
import numpy as np
from jax import lax
from jax.experimental import pallas as pl
from jax.experimental.pallas import tpu as pltpu

F32 = jnp.float32
BF16 = jnp.bfloat16
MESH = pl.DeviceIdType.MESH

HEAD_DIM = 128
ROPE_THETA = 10000.0
GRID_W = 64
EPS = 1e-6
NEG = -1e30
A_HEADS, A_Q_RANK, A_KV_RANK, A_NOPE, A_ROPE, A_V = 4, 512, 512, 128, 64, 128
B_HEADS = 6
B_PATTERNS = ((128, 1), (512, 4), (2048, 16))
C_HEADS, C_KV_HEADS = 6, 2
C_GROUP = C_HEADS // C_KV_HEADS
A_WIDTH, B_WIDTH, C_WIDTH = A_HEADS * A_V, B_HEADS * HEAD_DIM, C_HEADS * HEAD_DIM
IN_A = A_Q_RANK + A_KV_RANK + A_ROPE
IN_B = 3 * B_WIDTH
IN_C = C_WIDTH + 2 * C_KV_HEADS * HEAD_DIM
ADAM_LR, ADAM_B1, ADAM_B2, ADAM_EPS, ADAM_WD, ADAM_STEP = 0.001, 0.9, 0.999, 1e-08, 0.01, 10

LANE = 128
SUBLANE = 8
VMEM_BYTES_V7X = 64 * 1024 * 1024
VMEM_LIMIT_CAP = VMEM_BYTES_V7X - 8 * 1024 * 1024
N_CHIPS = 4
N_DEV = 8

A_PAD = 12 * LANE
PB_CQ, PB_CKV, PB_KR = 0, 4, 8
PB_BQ, PB_BK, PB_BV = 12, 18, 24
PB_CQH, PB_CKH, PB_CVH = 30, 36, 38
NP = 40 * LANE
A_QK = 2 * LANE


def _pick(n, cap, mult=LANE):
    if n <= cap:
        return n
    t = cap - cap % mult
    while t >= mult:
        if n % t == 0:
            return t
        t -= mult
    return n


def _rows_for(width_bytes, n_rows, target=2 * 1024 * 1024):
    return _pick(n_rows, max(SUBLANE, target // max(width_bytes, 1)), SUBLANE)


def _params(est_bytes):
    limit = int(min(max(est_bytes + (4 << 20), 32 << 20), VMEM_LIMIT_CAP))
    return pltpu.CompilerParams(vmem_limit_bytes=limit)


def _isz(x):
    return jnp.dtype(x.dtype).itemsize


_DIMS = {"nn": (((1,), (0,)), ((), ())), "nt": (((1,), (1,)), ((), ())), "tn": (((0,), (0,)), ((), ()))}


def _matmul(a, b, *, mode, out_dtype, name, add=None, tm=512, tn=512):
    if mode == "tn":
        (k, m), (k2, n) = a.shape, b.shape
    elif mode == "nt":
        (m, k), (n, k2) = a.shape, b.shape
    else:
        (m, k), (k2, n) = a.shape, b.shape
    assert k == k2, (a.shape, b.shape, mode)
    tm, tn = _pick(m, tm), _pick(n, tn)
    a_spec = pl.BlockSpec((k, tm), lambda i, j: (0, i)) if mode == "tn" else pl.BlockSpec((tm, k), lambda i, j: (i, 0))
    b_spec = pl.BlockSpec((tn, k), lambda i, j: (j, 0)) if mode == "nt" else pl.BlockSpec((k, tn), lambda i, j: (0, j))
    o_spec = pl.BlockSpec((tm, tn), lambda i, j: (i, j))
    dims = _DIMS[mode]

    def body(*refs):
        if add is None:
            a_ref, b_ref, o_ref = refs
        else:
            a_ref, b_ref, add_ref, o_ref = refs
        acc = lax.dot_general(a_ref[...].astype(BF16), b_ref[...].astype(BF16), dims, preferred_element_type=F32)
        if add is not None:
            acc = acc + add_ref[...].astype(F32)
        o_ref[...] = acc.astype(out_dtype)

    ins, specs = [a, b], [a_spec, b_spec]
    if add is not None:
        ins.append(add)
        specs.append(o_spec)
    est = 2 * (tm * k * _isz(a) + tn * k * _isz(b) + tm * tn * (jnp.dtype(out_dtype).itemsize + (4 if add is not None else 0)))
    est += (tm + tn) * k * 2 + 2 * tm * tn * 4
    return pl.pallas_call(
        body, name=name, grid=(m // tm, n // tn), in_specs=specs, out_specs=o_spec,
        out_shape=jax.ShapeDtypeStruct((m, n), out_dtype), compiler_params=_params(est),
    )(*ins)


def _norm_fwd(x, gain, *, wb, cb, nb, shared_gain, out_dtype, name):
    s = x.shape[0]
    ts = _rows_for(wb * 4, s)

    def body(x_ref, g_ref, o_ref):
        xv = x_ref[...].astype(F32)
        r = lax.rsqrt(jnp.mean(xv * xv, axis=1, keepdims=True) + EPS)
        o_ref[...] = ((xv * r) * g_ref[...]).astype(out_dtype)

    return pl.pallas_call(
        body, name=name, grid=(nb, s // ts),
        in_specs=[pl.BlockSpec((ts, wb), lambda n, i: (i, cb + n)),
                  pl.BlockSpec((1, wb), (lambda n, i: (0, 0)) if shared_gain else (lambda n, i: (0, n)))],
        out_specs=pl.BlockSpec((ts, wb), lambda n, i: (i, n)),
        out_shape=jax.ShapeDtypeStruct((s, nb * wb), out_dtype), compiler_params=_params(6 * ts * wb * 4),
    )(x, gain)


def _norm_bwd(x, gain, dy, *, wb, cb, nb, shared_gain, out_dtype, name, dy_cb=0, add=None):
    s = x.shape[0]
    ts = _rows_for(wb * 4, s, target=1024 * 1024)
    gw = wb if shared_gain else nb * wb

    def body(*refs):
        if add is None:
            x_ref, g_ref, dy_ref, dx_ref, dg_ref = refs
        else:
            x_ref, g_ref, dy_ref, add_ref, dx_ref, dg_ref = refs
        n, i = pl.program_id(0), pl.program_id(1)
        xv = x_ref[...].astype(F32)
        dyv = dy_ref[...].astype(F32)
        r = lax.rsqrt(jnp.mean(xv * xv, axis=1, keepdims=True) + EPS)
        xh = xv * r
        dyg = dyv * g_ref[...]
        dx = r * (dyg - xh * jnp.mean(dyg * xh, axis=1, keepdims=True))
        if add is not None:
            dx = dx + add_ref[...]
        dx_ref[...] = dx.astype(out_dtype)
        first = jnp.logical_and(n == 0, i == 0) if shared_gain else (i == 0)

        @pl.when(first)
        def _():
            dg_ref[...] = jnp.zeros_like(dg_ref)

        dg_ref[...] += jnp.sum(dyv * xh, axis=0, keepdims=True)

    ins = [x, gain, dy]
    specs = [pl.BlockSpec((ts, wb), lambda n, i: (i, cb + n)),
             pl.BlockSpec((1, wb), (lambda n, i: (0, 0)) if shared_gain else (lambda n, i: (0, n))),
             pl.BlockSpec((ts, wb), lambda n, i: (i, dy_cb + n))]
    if add is not None:
        ins.append(add)
        specs.append(pl.BlockSpec((ts, wb), lambda n, i: (i, n)))
    return pl.pallas_call(
        body, name=name, grid=(nb, s // ts), in_specs=specs,
        out_specs=[pl.BlockSpec((ts, wb), lambda n, i: (i, n)),
                   pl.BlockSpec((1, wb), (lambda n, i: (0, 0)) if shared_gain else (lambda n, i: (0, n)))],
        out_shape=[jax.ShapeDtypeStruct((s, nb * wb), out_dtype), jax.ShapeDtypeStruct((1, gw), F32)],
        compiler_params=_params(12 * ts * wb * 4),
    )(*ins)


def _swap_halves(x, half):
    if 2 * half == LANE:
        return pltpu.roll(x, half, axis=1)
    lane = lax.broadcasted_iota(jnp.int32, x.shape, 1)
    first = jnp.bitwise_and(lane, 2 * half - 1) < half
    return jnp.where(first, pltpu.roll(x, LANE - half, axis=1), pltpu.roll(x, half, axis=1))


def _rope(x, cos_t, sin_t, *, tw, cb, nb, half, sign, out_dtype, name):
    s = x.shape[0]
    ts = _rows_for(tw * 4, s)

    def body(x_ref, c_ref, s_ref, o_ref):
        for q in range(tw // LANE):
            sl = slice(q * LANE, (q + 1) * LANE)
            xv = x_ref[:, sl].astype(F32)
            sv = s_ref[:, sl]
            if sign < 0:
                sv = -sv
            o_ref[:, sl] = (xv * c_ref[:, sl] + _swap_halves(xv, half) * sv).astype(out_dtype)

    return pl.pallas_call(
        body, name=name, grid=(nb, s // ts),
        in_specs=[pl.BlockSpec((ts, tw), lambda n, i: (i, cb + n)),
                  pl.BlockSpec((ts, tw), lambda n, i: (i, 0)),
                  pl.BlockSpec((ts, tw), lambda n, i: (i, 0))],
        out_specs=pl.BlockSpec((ts, tw), lambda n, i: (i, n)),
        out_shape=jax.ShapeDtypeStruct((s, nb * tw), out_dtype), compiler_params=_params(10 * ts * tw * 4),
    )(x, cos_t, sin_t)


def _group_sum(x, *, n_out, g, src, out_dtype, name):
    s = x.shape[0]
    ts = _rows_for(LANE * 4, s)

    def body(*refs):
        acc = refs[0][...].astype(F32)
        for r in refs[1:-1]:
            acc = acc + r[...].astype(F32)
        refs[-1][...] = acc.astype(out_dtype)

    return pl.pallas_call(
        body, name=name, grid=(n_out, s // ts),
        in_specs=[pl.BlockSpec((ts, LANE), functools.partial(lambda n, i, j: (i, src(n, j)), j=j)) for j in range(g)],
        out_specs=pl.BlockSpec((ts, LANE), lambda n, i: (i, n)),
        out_shape=jax.ShapeDtypeStruct((s, n_out * LANE), out_dtype), compiler_params=_params(4 * g * ts * LANE * 4),
    )(*([x] * g))


def _flash_fwd(q, k, v, *, hkv, g, dqk, q_cb, k_cb, v_cb, v_step, scale, t, bias, name):
    s = q.shape[0]
    n = s // t
    hq = hkv * g
    banded = bias is not None
    if banded:
        nkb = bias.shape[0]
        wb = (nkb - 1) // 2
    else:
        nkb, wb = n, 0

    def kv_blk(i, j):
        return jnp.clip(i + j - wb, 0, n - 1) if banded else j

    def body(*refs):
        if banded:
            q_ref, k_ref, v_ref, b_ref, o_ref, lse_ref, m_s, l_s, acc_s = refs
        else:
            q_ref, k_ref, v_ref, o_ref, lse_ref, m_s, l_s, acc_s = refs
        i, j = pl.program_id(1), pl.program_id(2)

        @pl.when(j == 0)
        def _():
            m_s[...] = jnp.full_like(m_s, NEG)
            l_s[...] = jnp.zeros_like(l_s)
            acc_s[...] = jnp.zeros_like(acc_s)

        def step():
            sc = lax.dot_general(q_ref[...].astype(BF16), k_ref[...].astype(BF16), _DIMS["nt"],
                                 preferred_element_type=F32) * scale
            if banded:
                sc = sc + b_ref[0]
            m_prev = m_s[...]
            m_new = jnp.maximum(m_prev, jnp.max(sc, axis=1, keepdims=True))
            alpha = jnp.exp(m_prev - m_new)
            p = jnp.exp(sc - m_new)
            l_s[...] = alpha * l_s[...] + jnp.sum(p, axis=1, keepdims=True)
            acc_s[...] = alpha * acc_s[...] + jnp.dot(p.astype(BF16), v_ref[...].astype(BF16), preferred_element_type=F32)
            m_s[...] = m_new

        if banded:
            pl.when(jnp.logical_and(i + j - wb >= 0, i + j - wb < n))(step)
        else:
            step()

        @pl.when(j == nkb - 1)
        def _():
            o_ref[...] = acc_s[...] / l_s[...]
            lse_ref[0] = jnp.broadcast_to(m_s[...] + jnp.log(l_s[...]), (t, LANE))

    ins = [q, k, v]
    specs = [pl.BlockSpec((t, dqk), lambda h, i, j: (i, q_cb + h)),
             pl.BlockSpec((t, dqk), lambda h, i, j: (kv_blk(i, j), k_cb + h // g)),
             pl.BlockSpec((t, LANE), lambda h, i, j: (kv_blk(i, j), v_cb + v_step * (h // g)))]
    if banded:
        ins.append(bias)
        specs.append(pl.BlockSpec((1, t, t), lambda h, i, j: (j, 0, 0)))
    est = 2 * (2 * t * dqk * 4 + t * LANE * 4 * 3 + t * t * 4) + 8 * t * t * 4
    return pl.pallas_call(
        body, name=name, grid=(hq, n, nkb), in_specs=specs,
        out_specs=[pl.BlockSpec((t, LANE), lambda h, i, j: (i, h)),
                   pl.BlockSpec((1, t, LANE), lambda h, i, j: (h, i, 0))],
        out_shape=[jax.ShapeDtypeStruct((s, hq * LANE), F32), jax.ShapeDtypeStruct((hq, s, LANE), F32)],
        scratch_shapes=[pltpu.VMEM((t, 1), F32), pltpu.VMEM((t, 1), F32), pltpu.VMEM((t, LANE), F32)],
        compiler_params=_params(est),
    )(*ins)


def _flash_bwd(q, k, v, o, do, lse, *, hkv, g, dqk, q_cb, k_cb, v_cb, v_step, scale, t, bias, dv_dtype, name):
    s = q.shape[0]
    n = s // t
    hq = hkv * g
    banded = bias is not None
    if banded:
        nkb = bias.shape[0]
        wb = (nkb - 1) // 2
    else:
        nkb, wb = n, 0

    def q_blk(kj, tt):
        return jnp.clip(kj + tt - wb, 0, n - 1) if banded else tt

    def body(*refs):
        if banded:
            q_ref, k_ref, v_ref, o_ref, do_ref, lse_ref, b_ref, dq_ref, dk_ref, dv_ref, dk_s, dv_s = refs
        else:
            q_ref, k_ref, v_ref, o_ref, do_ref, lse_ref, dq_ref, dk_ref, dv_ref, dk_s, dv_s = refs
        kj, tt = pl.program_id(1), pl.program_id(2)

        @pl.when(jnp.logical_and(kj == 0, tt == 0))
        def _():
            dq_ref[...] = jnp.zeros_like(dq_ref)

        @pl.when(tt == 0)
        def _():
            dk_s[...] = jnp.zeros_like(dk_s)
            dv_s[...] = jnp.zeros_like(dv_s)

        def step():
            qv = q_ref[...].astype(BF16)
            kv = k_ref[...].astype(BF16)
            dof = do_ref[...].astype(F32)
            dov = dof.astype(BF16)
            sc = lax.dot_general(qv, kv, _DIMS["nt"], preferred_element_type=F32) * scale
            if banded:
                sc = sc + b_ref[0]
            p = jnp.exp(sc - lse_ref[0, :, 0:1])
            dp = lax.dot_general(dov, v_ref[...].astype(BF16), _DIMS["nt"], preferred_element_type=F32)
            delta = jnp.sum(dof * o_ref[...], axis=1, keepdims=True)
            ds = (p * (dp - delta) * scale).astype(BF16)
            dv_s[...] += lax.dot_general(p.astype(BF16), dov, _DIMS["tn"], preferred_element_type=F32)
            dk_s[...] += lax.dot_general(ds, qv, _DIMS["tn"], preferred_element_type=F32)
            rows = pl.ds(pl.multiple_of(q_blk(kj, tt) * t, t), t)
            dq_ref[rows, :] += jnp.dot(ds, kv, preferred_element_type=F32)

        if banded:
            pl.when(jnp.logical_and(kj + tt - wb >= 0, kj + tt - wb < n))(step)
        else:
            step()

        @pl.when(tt == nkb - 1)
        def _():
            dk_ref[...] = dk_s[...]
            dv_ref[...] = dv_s[...].astype(dv_dtype)

    ins = [q, k, v, o, do, lse]
    specs = [pl.BlockSpec((t, dqk), lambda h, kj, tt: (q_blk(kj, tt), q_cb + h)),
             pl.BlockSpec((t, dqk), lambda h, kj, tt: (kj, k_cb + h // g)),
             pl.BlockSpec((t, LANE), lambda h, kj, tt: (kj, v_cb + v_step * (h // g))),
             pl.BlockSpec((t, LANE), lambda h, kj, tt: (q_blk(kj, tt), h)),
             pl.BlockSpec((t, LANE), lambda h, kj, tt: (q_blk(kj, tt), h)),
             pl.BlockSpec((1, t, LANE), lambda h, kj, tt: (h, q_blk(kj, tt), 0))]
    if banded:
        ins.append(bias)
        specs.append(pl.BlockSpec((1, t, t), lambda h, kj, tt: (jnp.clip(2 * wb - tt, 0, nkb - 1), 0, 0)))
    est = 2 * (3 * t * dqk * 4 + 5 * t * LANE * 4 + t * t * 4 + s * dqk * 4) + 2 * t * dqk * 4 + 10 * t * t * 4
    return pl.pallas_call(
        body, name=name, grid=(hq, n, nkb), in_specs=specs,
        out_specs=[pl.BlockSpec((s, dqk), lambda h, kj, tt: (0, h)),
                   pl.BlockSpec((t, dqk), lambda h, kj, tt: (kj, h)),
                   pl.BlockSpec((t, LANE), lambda h, kj, tt: (kj, h))],
        out_shape=[jax.ShapeDtypeStruct((s, hq * dqk), F32), jax.ShapeDtypeStruct((s, hq * dqk), F32),
                   jax.ShapeDtypeStruct((s, hq * LANE), dv_dtype)],
        scratch_shapes=[pltpu.VMEM((t, dqk), F32), pltpu.VMEM((t, LANE), F32)],
        compiler_params=_params(est),
    )(*ins)


def _swiglu_fwd(gate, up, *, name):
    s, f = gate.shape
    ts, tf = _pick(s, 512, SUBLANE), _pick(f, 1024)

    def body(g_ref, u_ref, o_ref):
        gv = g_ref[...]
        o_ref[...] = (gv / (1.0 + jnp.exp(-gv)) * u_ref[...]).astype(BF16)

    spec = pl.BlockSpec((ts, tf), lambda i, j: (i, j))
    return pl.pallas_call(
        body, name=name, grid=(s // ts, f // tf), in_specs=[spec, spec], out_specs=spec,
        out_shape=jax.ShapeDtypeStruct((s, f), BF16), compiler_params=_params(8 * ts * tf * 4),
    )(gate, up)


def _swiglu_bwd(gate, up, dact, *, name):
    s, f = gate.shape
    ts, tf = _pick(s, 512, SUBLANE), _pick(f, 1024)

    def body(g_ref, u_ref, d_ref, dg_ref, du_ref):
        gv, uv, dv = g_ref[...], u_ref[...], d_ref[...]
        sig = 1.0 / (1.0 + jnp.exp(-gv))
        dg_ref[...] = (dv * uv * (sig * (1.0 + gv * (1.0 - sig)))).astype(BF16)
        du_ref[...] = (dv * (gv * sig)).astype(BF16)

    spec = pl.BlockSpec((ts, tf), lambda i, j: (i, j))
    return pl.pallas_call(
        body, name=name, grid=(s // ts, f // tf), in_specs=[spec, spec, spec], out_specs=[spec, spec],
        out_shape=[jax.ShapeDtypeStruct((s, f), BF16)] * 2, compiler_params=_params(12 * ts * tf * 4),
    )(gate, up, dact)


def _final_loss(x, gain, target, *, name):
    s, d = x.shape
    ts = _rows_for(d * 4, s, target=1024 * 1024)

    def body(x_ref, g_ref, t_ref, dx_ref, dg_ref, loss_ref):
        i = pl.program_id(0)
        xv = x_ref[...]
        gv = g_ref[...]
        r = lax.rsqrt(jnp.mean(xv * xv, axis=1, keepdims=True) + EPS)
        xh = xv * r
        err = xh * gv - t_ref[...]
        dy = err / d
        dyg = dy * gv
        dx_ref[...] = r * (dyg - xh * jnp.mean(dyg * xh, axis=1, keepdims=True))

        @pl.when(i == 0)
        def _():
            dg_ref[...] = jnp.zeros_like(dg_ref)
            loss_ref[...] = jnp.zeros_like(loss_ref)

        dg_ref[...] += jnp.sum(dy * xh, axis=0, keepdims=True)
        part = jnp.sum(jnp.mean(err * err, axis=1, keepdims=True), axis=0, keepdims=True)
        loss_ref[...] += jnp.broadcast_to(0.5 * part, (1, LANE))

    row = pl.BlockSpec((ts, d), lambda i: (i, 0))
    return pl.pallas_call(
        body, name=name, grid=(s // ts,),
        in_specs=[row, pl.BlockSpec((1, d), lambda i: (0, 0)), row],
        out_specs=[row, pl.BlockSpec((1, d), lambda i: (0, 0)), pl.BlockSpec((1, LANE), lambda i: (0, 0))],
        out_shape=[jax.ShapeDtypeStruct((s, d), F32), jax.ShapeDtypeStruct((1, d), F32),
                   jax.ShapeDtypeStruct((1, LANE), F32)],
        compiler_params=_params(12 * ts * d * 4),
    )(x, gain, target)


def _cast_bf16(x2d, *, name):
    r, c = x2d.shape
    tr = _rows_for(c * 4, r)

    def body(x_ref, o_ref):
        o_ref[...] = x_ref[...].astype(BF16)

    spec = pl.BlockSpec((tr, c), lambda i: (i, 0))
    return pl.pallas_call(body, name=name, grid=(r // tr,), in_specs=[spec], out_specs=spec,
                          out_shape=jax.ShapeDtypeStruct((r, c), BF16), compiler_params=_params(6 * tr * c * 4))(x2d)


def _sum_chips(parts, *, name):
    _, r, c = parts.shape
    tr = _rows_for(c * 4, r, target=1024 * 1024)

    def body(p_ref, o_ref):
        acc = p_ref[0].astype(F32)
        for j in range(1, N_CHIPS):
            acc = acc + p_ref[j].astype(F32)
        o_ref[...] = acc

    return pl.pallas_call(
        body, name=name, grid=(r // tr,),
        in_specs=[pl.BlockSpec((N_CHIPS, tr, c), lambda i: (0, i, 0))], out_specs=pl.BlockSpec((tr, c), lambda i: (i, 0)),
        out_shape=jax.ShapeDtypeStruct((r, c), F32), compiler_params=_params(8 * tr * c * 4),
    )(parts)


def _adamw_math(w, g, m, v):
    m2 = ADAM_B1 * m + (1.0 - ADAM_B1) * g
    v2 = ADAM_B2 * v + (1.0 - ADAM_B2) * (g * g)
    m_hat = m2 / (1.0 - ADAM_B1 ** ADAM_STEP)
    v_hat = v2 / (1.0 - ADAM_B2 ** ADAM_STEP)
    delta = -ADAM_LR * (m_hat / (jnp.sqrt(v_hat) + ADAM_EPS) + ADAM_WD * w)
    return delta, m2, v2


def _adamw(g_a, g_b, w, m, v, *, name):
    r, c = w.shape
    tr = _rows_for(c * 4, r, target=512 * 1024)

    def body(a_ref, b_ref, w_ref, m_ref, v_ref, g_out, d_out, m_out, v_out):
        gv = a_ref[...] + b_ref[...]
        delta, m2, v2 = _adamw_math(w_ref[...], gv, m_ref[...], v_ref[...])
        g_out[...] = gv
        d_out[...] = delta
        m_out[...] = m2
        v_out[...] = v2

    spec = pl.BlockSpec((tr, c), lambda i: (i, 0))
    return pl.pallas_call(
        body, name=name, grid=(r // tr,), in_specs=[spec] * 5, out_specs=[spec] * 4,
        out_shape=[jax.ShapeDtypeStruct((r, c), F32)] * 4, compiler_params=_params(22 * tr * c * 4),
    )(g_a, g_b, w, m, v)


def _small_adamw(g_all, w, m, v, *, name):
    r, c = w.shape

    def body(ga_ref, w_ref, m_ref, v_ref, g_out, d_out, m_out, v_out):
        gv = ga_ref[0]
        for j in range(1, N_DEV):
            gv = gv + ga_ref[j]
        delta, m2, v2 = _adamw_math(w_ref[...], gv, m_ref[...], v_ref[...])
        g_out[...] = gv
        d_out[...] = delta
        m_out[...] = m2
        v_out[...] = v2

    return pl.pallas_call(body, name=name, out_shape=[jax.ShapeDtypeStruct((r, c), F32)] * 4)(g_all, w, m, v)


_ANY = pl.BlockSpec(memory_space=pl.ANY)


def _chip_exchange(srcs, *, per_dest, name):
    n = len(srcs)
    out_shape = [jax.ShapeDtypeStruct(((N_CHIPS,) + a.shape[1:]) if per_dest else ((N_CHIPS,) + a.shape), a.dtype)
                 for a in srcs]

    def body(*refs):
        src, out = refs[:n], refs[n:2 * n]
        send_sems, recv_sems, local_sems = refs[2 * n:]
        x, y, c = lax.axis_index("x"), lax.axis_index("y"), lax.axis_index("c")
        me = 2 * x + y
        peers = [(1 - x, y), (x, 1 - y), (1 - x, 1 - y)]

        def remote(k, p, dst_slot):
            px, py = peers[p]
            s_ref = src[k].at[2 * px + py] if per_dest else src[k]
            return pltpu.make_async_remote_copy(
                src_ref=s_ref, dst_ref=out[k].at[dst_slot], send_sem=send_sems.at[k, p], recv_sem=recv_sems.at[k, p],
                device_id=(px, py, c), device_id_type=MESH)

        local = [pltpu.make_async_copy(src[k].at[me] if per_dest else src[k], out[k].at[me], local_sems.at[k])
                 for k in range(n)]
        sends = [remote(k, p, me) for k in range(n) for p in range(3)]
        for cp in local + sends:
            cp.start()
        for k in range(n):
            for p in range(3):
                remote(k, p, 2 * peers[p][0] + peers[p][1]).wait_recv()
        for cp in sends:
            cp.wait_send()
        for cp in local:
            cp.wait()

    return pl.pallas_call(
        body, name=name, in_specs=[_ANY] * n, out_specs=[_ANY] * n, out_shape=out_shape,
        scratch_shapes=[pltpu.SemaphoreType.DMA((n, 3)), pltpu.SemaphoreType.DMA((n, 3)), pltpu.SemaphoreType.DMA((n,))],
    )(*srcs)


def _sibling_exchange(srcs, *, name):
    n = len(srcs)

    def body(*refs):
        src, out = refs[:n], refs[n:2 * n]
        send_sems, recv_sems = refs[2 * n:]
        sibling = (lax.axis_index("x"), lax.axis_index("y"), 1 - lax.axis_index("c"))
        copies = [pltpu.make_async_remote_copy(src_ref=src[k], dst_ref=out[k], send_sem=send_sems.at[k],
                                               recv_sem=recv_sems.at[k], device_id=sibling, device_id_type=MESH)
                  for k in range(n)]
        for cp in copies:
            cp.start()
        for cp in copies:
            cp.wait_recv()
        for cp in copies:
            cp.wait_send()

    return pl.pallas_call(
        body, name=name, in_specs=[_ANY] * n, out_specs=[_ANY] * n,
        out_shape=[jax.ShapeDtypeStruct(a.shape, a.dtype) for a in srcs],
        scratch_shapes=[pltpu.SemaphoreType.DMA((n,)), pltpu.SemaphoreType.DMA((n,))],
    )(*srcs)


def _all_gather_small(block, *, name):
    m_per, ncol = block.shape

    def body(x_ref, out_ref, send_sems, recv_sems, local_sem):
        x, y, c = lax.axis_index("x"), lax.axis_index("y"), lax.axis_index("c")
        me, sibling = (x, y, c), (x, y, 1 - c)
        chips = [(1 - x, y), (x, 1 - y), (1 - x, 1 - y)]

        def rows(px, py, pc):
            return out_ref.at[pl.ds((4 * px + 2 * py + pc) * m_per, m_per), :]

        def copy(k, blk, to, src=None):
            return pltpu.make_async_remote_copy(
                src_ref=rows(*blk) if src is None else src, dst_ref=rows(*blk),
                send_sem=send_sems.at[k], recv_sem=recv_sems.at[k], device_id=to, device_id_type=MESH)

        mine = pltpu.make_async_copy(x_ref, rows(*me), local_sem)
        mine.start()
        first = [copy(0, me, sibling, src=x_ref)]
        first += [copy(1 + j, me, (*chip, c), src=x_ref) for j, chip in enumerate(chips)]
        for cp in first:
            cp.start()
        passed = [copy(4 + j, (*chip, c), sibling) for j, chip in enumerate(chips)]
        for j, chip in enumerate(chips):
            copy(1 + j, (*chip, c), me).wait_recv()
            passed[j].start()
        copy(0, sibling, me).wait_recv()
        for j, chip in enumerate(chips):
            copy(4 + j, (*chip, 1 - c), me).wait_recv()
        for cp in first + passed:
            cp.wait_send()
        mine.wait()

    return pl.pallas_call(
        body, name=name, out_shape=jax.ShapeDtypeStruct((N_DEV * m_per, ncol), block.dtype),
        in_specs=[pl.BlockSpec(memory_space=pltpu.VMEM)], out_specs=pl.BlockSpec(memory_space=pltpu.VMEM),
        scratch_shapes=[pltpu.SemaphoreType.DMA((7,)), pltpu.SemaphoreType.DMA((7,)), pltpu.SemaphoreType.DMA],
    )(block)


def _rope_angles(pos, dim):
    inv = ROPE_THETA ** (-jnp.arange(0, dim, 2, dtype=F32) / dim)
    return pos.astype(F32)[:, None] * inv[None, :]


def _rope_tables(s):
    pos = jnp.arange(s, dtype=jnp.int32)
    rows = s // GRID_W
    row = jnp.repeat(jnp.arange(rows, dtype=jnp.int32), GRID_W)
    col = jnp.tile(jnp.arange(GRID_W, dtype=jnp.int32), rows)
    a1 = _rope_angles(pos, HEAD_DIM)
    aa = _rope_angles(pos, A_ROPE)
    ar = _rope_angles(row, HEAD_DIM // 2)
    ac = _rope_angles(col, HEAD_DIM // 2)
    one = jnp.ones((s, LANE), F32)
    zero = jnp.zeros((s, LANE), F32)
    pad = LANE - A_ROPE
    cos_a = jnp.concatenate([one, jnp.cos(aa), jnp.cos(aa), jnp.ones((s, pad), F32)], axis=1)
    sin_a = jnp.concatenate([zero, -jnp.sin(aa), jnp.sin(aa), jnp.zeros((s, pad), F32)], axis=1)
    cos_b = jnp.concatenate([jnp.cos(a1), jnp.cos(a1)], axis=1)
    sin_b = jnp.concatenate([-jnp.sin(a1), jnp.sin(a1)], axis=1)
    cos_c = jnp.concatenate([jnp.cos(ar), jnp.cos(ar), jnp.cos(ac), jnp.cos(ac)], axis=1)
    sin_c = jnp.concatenate([-jnp.sin(ar), jnp.sin(ar), -jnp.sin(ac), jnp.sin(ac)], axis=1)
    return (cos_a, sin_a), (cos_b, sin_b), (cos_c, sin_c)


def _band_bias(t, n):
    reach = max((w // (2 * d)) * d for w, d in B_PATTERNS)
    wb = min(-(-reach // t), n - 1)
    j = jnp.arange(2 * wb + 1, dtype=jnp.int32)[:, None, None]
    r = jnp.arange(t, dtype=jnp.int32)[None, :, None]
    c = jnp.arange(t, dtype=jnp.int32)[None, None, :]
    rel = (j - wb) * t + c - r
    mult = jnp.zeros(rel.shape, F32)
    for w, d in B_PATTERNS:
        mult = mult + jnp.logical_and(rel % d == 0, jnp.abs(rel) <= (w // (2 * d)) * d).astype(F32)
    return jnp.where(mult > 0, jnp.log(jnp.maximum(mult, 1.0)), NEG)


_BIG = ("w_in", "a_w_uq", "a_w_ukv", "w_out", "w_gate", "w_up", "w_down")
_SMALL = ("attn_norm", "a_q_norm", "a_kv_norm", "c_q_norm", "c_k_norm", "out_norm", "ffn_norm", "final_norm")
_WEIGHTS = ("attn_norm", "w_in", "a_q_norm", "a_w_uq", "a_kv_norm", "a_w_ukv", "c_q_norm", "c_k_norm", "out_norm",
            "w_out", "ffn_norm", "w_gate", "w_up", "w_down", "final_norm")


def _assemble_weights(gw, l):
    def cols(name):
        a = gw[name][:, l]
        return jnp.transpose(a, (1, 0, 2)).reshape(a.shape[1], N_CHIPS * a.shape[2])

    def rows(name):
        a = gw[name][:, l]
        return a.reshape(N_CHIPS * a.shape[1], a.shape[2])

    w_in = cols("w_in")
    d = w_in.shape[0]
    w_all = jnp.concatenate([w_in[:, :IN_A], jnp.zeros((d, A_PAD - IN_A), BF16), w_in[:, IN_A:]], axis=1)
    uq = cols("a_w_uq").reshape(A_Q_RANK, A_HEADS, A_NOPE + A_ROPE)
    uq = jnp.pad(uq, ((0, 0), (0, 0), (0, A_QK - A_NOPE - A_ROPE))).reshape(A_Q_RANK, A_HEADS * A_QK)
    return dict(w_all=w_all, uq=uq, ukv=cols("a_w_ukv"), w_out=rows("w_out"), w_gate=cols("w_gate"),
                w_up=cols("w_up"), w_down=rows("w_down"))


def _split_grads(gl):
    def cols(a):
        return jnp.transpose(a.reshape(a.shape[0], N_CHIPS, a.shape[1] // N_CHIPS), (1, 0, 2))

    def rows(a):
        return a.reshape(N_CHIPS, a.shape[0] // N_CHIPS, a.shape[1])

    w_all = gl["w_all"]
    w_in = jnp.concatenate([w_all[:, :IN_A], w_all[:, A_PAD:]], axis=1)
    uq = gl["uq"].reshape(A_Q_RANK, A_HEADS, A_QK)[:, :, :A_NOPE + A_ROPE].reshape(A_Q_RANK, A_HEADS * (A_NOPE + A_ROPE))
    return dict(w_in=cols(w_in), a_w_uq=cols(uq), a_w_ukv=cols(gl["ukv"]), w_out=rows(gl["w_out"]),
                w_gate=cols(gl["w_gate"]), w_up=cols(gl["w_up"]), w_down=rows(gl["w_down"]))


def _layer_fwd(x, wl, sm, tabs, bias, t):
    s = x.shape[0]
    (cos_a, sin_a), (cos_b, sin_b), (cos_c, sin_c) = tabs
    h = _norm_fwd(x, sm["attn_norm"], wb=x.shape[1], cb=0, nb=1, shared_gain=True, out_dtype=BF16, name="attn_norm_fwd")
    p = _matmul(h, wl["w_all"], mode="nn", out_dtype=F32, name="in_proj", tm=1024, tn=640)
    cq_n = _norm_fwd(p, sm["a_q_norm"], wb=A_Q_RANK, cb=0, nb=1, shared_gain=True, out_dtype=BF16, name="a_q_norm_fwd")
    ckv_n = _norm_fwd(p, sm["a_kv_norm"], wb=A_KV_RANK, cb=1, nb=1, shared_gain=True, out_dtype=BF16, name="a_kv_norm_fwd")
    qa_raw = _matmul(cq_n, wl["uq"], mode="nn", out_dtype=F32, name="a_uq", tm=1024, tn=1024)
    kv = _matmul(ckv_n, wl["ukv"], mode="nn", out_dtype=F32, name="a_ukv", tm=1024, tn=1024)
    k_nope = kv.reshape(s, A_HEADS, 2, LANE)[:, :, 0]
    k_rope = jnp.broadcast_to(p[:, PB_KR * LANE:(PB_KR + 1) * LANE][:, None, :], (s, A_HEADS, LANE))
    ka_raw = jnp.stack([k_nope, k_rope], axis=2).reshape(s, A_HEADS * A_QK)
    qa = _rope(qa_raw, cos_a, sin_a, tw=A_QK, cb=0, nb=A_HEADS, half=A_ROPE // 2, sign=1, out_dtype=BF16, name="a_rope_q")
    ka = _rope(ka_raw, cos_a, sin_a, tw=A_QK, cb=0, nb=A_HEADS, half=A_ROPE // 2, sign=1, out_dtype=BF16, name="a_rope_k")
    oa, lse_a = _flash_fwd(qa, ka, kv, hkv=A_HEADS, g=1, dqk=A_QK, q_cb=0, k_cb=0, v_cb=1, v_step=2,
                           scale=(A_NOPE + A_ROPE) ** -0.5, t=t, bias=None, name="a_flash_fwd")
    qb = _rope(p, cos_b, sin_b, tw=LANE, cb=PB_BQ, nb=B_HEADS, half=HEAD_DIM // 2, sign=1, out_dtype=BF16, name="b_rope_q")
    kb = _rope(p, cos_b, sin_b, tw=LANE, cb=PB_BK, nb=B_HEADS, half=HEAD_DIM // 2, sign=1, out_dtype=BF16, name="b_rope_k")
    ob, lse_b = _flash_fwd(qb, kb, p, hkv=B_HEADS, g=1, dqk=LANE, q_cb=0, k_cb=0, v_cb=PB_BV, v_step=1,
                           scale=HEAD_DIM ** -0.5, t=t, bias=bias, name="b_flash_fwd")
    qn = _norm_fwd(p, sm["c_q_norm"], wb=LANE, cb=PB_CQH, nb=C_HEADS, shared_gain=True, out_dtype=F32, name="c_q_norm_fwd")
    kn = _norm_fwd(p, sm["c_k_norm"], wb=LANE, cb=PB_CKH, nb=C_KV_HEADS, shared_gain=True, out_dtype=F32, name="c_k_norm_fwd")
    qc = _rope(qn, cos_c, sin_c, tw=LANE, cb=0, nb=C_HEADS, half=HEAD_DIM // 4, sign=1, out_dtype=BF16, name="c_rope_q")
    kc = _rope(kn, cos_c, sin_c, tw=LANE, cb=0, nb=C_KV_HEADS, half=HEAD_DIM // 4, sign=1, out_dtype=BF16, name="c_rope_k")
    oc, lse_c = _flash_fwd(qc, kc, p, hkv=C_KV_HEADS, g=C_GROUP, dqk=LANE, q_cb=0, k_cb=0, v_cb=PB_CVH, v_step=1,
                           scale=HEAD_DIM ** -0.5, t=t, bias=None, name="c_flash_fwd")
    g_out = sm["out_norm"]
    ga, gb, gc = g_out[:, :A_WIDTH], g_out[:, A_WIDTH:A_WIDTH + B_WIDTH], g_out[:, A_WIDTH + B_WIDTH:]
    ya = _norm_fwd(oa, ga, wb=A_WIDTH, cb=0, nb=1, shared_gain=True, out_dtype=BF16, name="out_norm_a_fwd")
    yb = _norm_fwd(ob, gb, wb=B_WIDTH, cb=0, nb=1, shared_gain=True, out_dtype=BF16, name="out_norm_b_fwd")
    yc = _norm_fwd(oc, gc, wb=C_WIDTH, cb=0, nb=1, shared_gain=True, out_dtype=BF16, name="out_norm_c_fwd")
    y = jnp.concatenate([ya, yb, yc], axis=1)
    x1 = _matmul(y, wl["w_out"], mode="nn", out_dtype=F32, name="out_proj", add=x, tm=1024, tn=512)
    h2 = _norm_fwd(x1, sm["ffn_norm"], wb=x.shape[1], cb=0, nb=1, shared_gain=True, out_dtype=BF16, name="ffn_norm_fwd")
    gate = _matmul(h2, wl["w_gate"], mode="nn", out_dtype=F32, name="ffn_gate", tm=1024, tn=512)
    up = _matmul(h2, wl["w_up"], mode="nn", out_dtype=F32, name="ffn_up", tm=1024, tn=512)
    act = _swiglu_fwd(gate, up, name="swiglu_fwd")
    x2 = _matmul(act, wl["w_down"], mode="nn", out_dtype=F32, name="ffn_down", add=x1, tm=512, tn=512)
    saved = dict(x=x, h=h, p=p, cq_n=cq_n, ckv_n=ckv_n, kv=kv, qa=qa, ka=ka, oa=oa, lse_a=lse_a, qb=qb, kb=kb, ob=ob,
                 lse_b=lse_b, qc=qc, kc=kc, oc=oc, lse_c=lse_c, y=y, x1=x1, h2=h2, gate=gate, up=up, act=act)
    return x2, saved


def _layer_bwd(dx2, sv, wl, sm, tabs, bias, t):
    s, d = dx2.shape
    (cos_a, sin_a), (cos_b, sin_b), (cos_c, sin_c) = tabs
    gw, gs = {}, {}
    dact = _matmul(dx2, wl["w_down"], mode="nt", out_dtype=F32, name="ffn_down_dx", tm=512, tn=512)
    gw["w_down"] = _matmul(sv["act"], dx2, mode="tn", out_dtype=BF16, name="ffn_down_dw", tm=512, tn=512)
    dgate, dup = _swiglu_bwd(sv["gate"], sv["up"], dact, name="swiglu_bwd")
    dh2 = _matmul(dgate, wl["w_gate"], mode="nt", out_dtype=F32, name="ffn_gate_dx", tm=512, tn=512)
    dh2 = _matmul(dup, wl["w_up"], mode="nt", out_dtype=F32, name="ffn_up_dx", add=dh2, tm=512, tn=512)
    gw["w_gate"] = _matmul(sv["h2"], dgate, mode="tn", out_dtype=BF16, name="ffn_gate_dw", tm=512, tn=512)
    gw["w_up"] = _matmul(sv["h2"], dup, mode="tn", out_dtype=BF16, name="ffn_up_dw", tm=512, tn=512)
    dx1, gs["ffn_norm"] = _norm_bwd(sv["x1"], sm["ffn_norm"], dh2, wb=d, cb=0, nb=1, shared_gain=True, out_dtype=F32,
                                    name="ffn_norm_bwd", add=dx2)
    dy = _matmul(dx1, wl["w_out"], mode="nt", out_dtype=F32, name="out_proj_dx", tm=512, tn=512)
    gw["w_out"] = _matmul(sv["y"], dx1, mode="tn", out_dtype=BF16, name="out_proj_dw", tm=512, tn=512)
    g_out = sm["out_norm"]
    ga, gb, gc = g_out[:, :A_WIDTH], g_out[:, A_WIDTH:A_WIDTH + B_WIDTH], g_out[:, A_WIDTH + B_WIDTH:]
    dya, dyb, dyc = dy[:, :A_WIDTH], dy[:, A_WIDTH:A_WIDTH + B_WIDTH], dy[:, A_WIDTH + B_WIDTH:]
    doa, dga = _norm_bwd(sv["oa"], ga, dya, wb=A_WIDTH, cb=0, nb=1, shared_gain=True, out_dtype=F32, name="out_norm_a_bwd")
    dob, dgb = _norm_bwd(sv["ob"], gb, dyb, wb=B_WIDTH, cb=0, nb=1, shared_gain=True, out_dtype=F32, name="out_norm_b_bwd")
    doc, dgc = _norm_bwd(sv["oc"], gc, dyc, wb=C_WIDTH, cb=0, nb=1, shared_gain=True, out_dtype=F32, name="out_norm_c_bwd")
    gs["out_norm"] = jnp.concatenate([dga, dgb, dgc], axis=1)
    p = sv["p"]
    dqc, dkc_q, dvc_q = _flash_bwd(sv["qc"], sv["kc"], p, sv["oc"], doc, sv["lse_c"], hkv=C_KV_HEADS, g=C_GROUP,
                                   dqk=LANE, q_cb=0, k_cb=0, v_cb=PB_CVH, v_step=1, scale=HEAD_DIM ** -0.5, t=t,
                                   bias=None, dv_dtype=F32, name="c_flash_bwd")
    dkc = _group_sum(dkc_q, n_out=C_KV_HEADS, g=C_GROUP, src=lambda n, j: C_GROUP * n + j, out_dtype=F32, name="c_dk_group_sum")
    dvc = _group_sum(dvc_q, n_out=C_KV_HEADS, g=C_GROUP, src=lambda n, j: C_GROUP * n + j, out_dtype=BF16, name="c_dv_group_sum")
    dqn = _rope(dqc, cos_c, sin_c, tw=LANE, cb=0, nb=C_HEADS, half=HEAD_DIM // 4, sign=-1, out_dtype=F32, name="c_rope_q_bwd")
    dkn = _rope(dkc, cos_c, sin_c, tw=LANE, cb=0, nb=C_KV_HEADS, half=HEAD_DIM // 4, sign=-1, out_dtype=F32, name="c_rope_k_bwd")
    dpcq, gs["c_q_norm"] = _norm_bwd(p, sm["c_q_norm"], dqn, wb=LANE, cb=PB_CQH, nb=C_HEADS, shared_gain=True,
                                     out_dtype=BF16, name="c_q_norm_bwd")
    dpck, gs["c_k_norm"] = _norm_bwd(p, sm["c_k_norm"], dkn, wb=LANE, cb=PB_CKH, nb=C_KV_HEADS, shared_gain=True,
                                     out_dtype=BF16, name="c_k_norm_bwd")
    dqb, dkb, dvb = _flash_bwd(sv["qb"], sv["kb"], p, sv["ob"], dob, sv["lse_b"], hkv=B_HEADS, g=1, dqk=LANE, q_cb=0,
                               k_cb=0, v_cb=PB_BV, v_step=1, scale=HEAD_DIM ** -0.5, t=t, bias=bias, dv_dtype=BF16,
                               name="b_flash_bwd")
    dpbq = _rope(dqb, cos_b, sin_b, tw=LANE, cb=0, nb=B_HEADS, half=HEAD_DIM // 2, sign=-1, out_dtype=BF16, name="b_rope_q_bwd")
    dpbk = _rope(dkb, cos_b, sin_b, tw=LANE, cb=0, nb=B_HEADS, half=HEAD_DIM // 2, sign=-1, out_dtype=BF16, name="b_rope_k_bwd")
    dqa, dka, dva = _flash_bwd(sv["qa"], sv["ka"], sv["kv"], sv["oa"], doa, sv["lse_a"], hkv=A_HEADS, g=1, dqk=A_QK,
                               q_cb=0, k_cb=0, v_cb=1, v_step=2, scale=(A_NOPE + A_ROPE) ** -0.5, t=t, bias=None,
                               dv_dtype=BF16, name="a_flash_bwd")
    dqa_raw = _rope(dqa, cos_a, sin_a, tw=A_QK, cb=0, nb=A_HEADS, half=A_ROPE // 2, sign=-1, out_dtype=BF16, name="a_rope_q_bwd")
    dka_raw = _rope(dka, cos_a, sin_a, tw=A_QK, cb=0, nb=A_HEADS, half=A_ROPE // 2, sign=-1, out_dtype=BF16, name="a_rope_k_bwd")
    dkr = _group_sum(dka_raw, n_out=1, g=A_HEADS, src=lambda n, j: 2 * j + 1, out_dtype=BF16, name="a_k_rope_sum")
    dkv = jnp.stack([dka_raw.reshape(s, A_HEADS, 2, LANE)[:, :, 0], dva.reshape(s, A_HEADS, LANE)], axis=2)
    dkv = dkv.reshape(s, A_HEADS * 2 * LANE)
    dckv_n = _matmul(dkv, wl["ukv"], mode="nt", out_dtype=F32, name="a_ukv_dx", tm=1024, tn=512)
    gw["ukv"] = _matmul(sv["ckv_n"], dkv, mode="tn", out_dtype=BF16, name="a_ukv_dw", tm=512, tn=1024)
    dcq_n = _matmul(dqa_raw, wl["uq"], mode="nt", out_dtype=F32, name="a_uq_dx", tm=1024, tn=512)
    gw["uq"] = _matmul(sv["cq_n"], dqa_raw, mode="tn", out_dtype=BF16, name="a_uq_dw", tm=512, tn=1024)
    dcq, gs["a_q_norm"] = _norm_bwd(p, sm["a_q_norm"], dcq_n, wb=A_Q_RANK, cb=0, nb=1, shared_gain=True, out_dtype=BF16,
                                    name="a_q_norm_bwd")
    dckv, gs["a_kv_norm"] = _norm_bwd(p, sm["a_kv_norm"], dckv_n, wb=A_KV_RANK, cb=1, nb=1, shared_gain=True,
                                      out_dtype=BF16, name="a_kv_norm_bwd")
    dp = jnp.concatenate([dcq, dckv, dkr, jnp.zeros((s, A_PAD - (PB_KR + 1) * LANE), BF16), dpbq, dpbk, dvb, dpcq, dpck,
                          dvc], axis=1)
    dh = _matmul(dp, wl["w_all"], mode="nt", out_dtype=F32, name="in_proj_dx", tm=512, tn=512)
    gw["w_all"] = _matmul(sv["h"], dp, mode="tn", out_dtype=BF16, name="in_proj_dw", tm=512, tn=640)
    dx, gs["attn_norm"] = _norm_bwd(sv["x"], sm["attn_norm"], dh, wb=d, cb=0, nb=1, shared_gain=True, out_dtype=F32,
                                    name="attn_norm_bwd", add=dx1)
    return dx, gw, gs


def _pack_small(vals):
    flat = jnp.concatenate([vals[n].reshape(-1).astype(F32) for n in _SMALL])
    tile = SUBLANE * LANE
    padded = -(-flat.shape[0] // tile) * tile
    return jnp.pad(flat, (0, padded - flat.shape[0])).reshape(padded // LANE, LANE)


def _unpack_small(packed, like):
    flat = packed.reshape(-1)
    out, off = {}, 0
    for n in _SMALL:
        size = math.prod(like[n].shape)
        out[n] = flat[off:off + size].reshape(like[n].shape)
        off += size
    return out


def kernel(x, attn_norm, w_in, a_q_norm, a_w_uq, a_kv_norm, a_w_ukv, c_q_norm, c_k_norm, out_norm, w_out, ffn_norm, w_gate, w_up, w_down, final_norm, loss_target, m_attn_norm, m_w_in, m_a_q_norm, m_a_w_uq, m_a_kv_norm, m_a_w_ukv, m_c_q_norm, m_c_k_norm, m_out_norm, m_w_out, m_ffn_norm, m_w_gate, m_w_up, m_w_down, m_final_norm, v_attn_norm, v_w_in, v_a_q_norm, v_a_w_uq, v_a_kv_norm, v_a_w_ukv, v_c_q_norm, v_c_k_norm, v_out_norm, v_w_out, v_ffn_norm, v_w_gate, v_w_up, v_w_down, v_final_norm):
    w = dict(attn_norm=attn_norm, w_in=w_in, a_q_norm=a_q_norm, a_w_uq=a_w_uq, a_kv_norm=a_kv_norm, a_w_ukv=a_w_ukv,
             c_q_norm=c_q_norm, c_k_norm=c_k_norm, out_norm=out_norm, w_out=w_out, ffn_norm=ffn_norm, w_gate=w_gate,
             w_up=w_up, w_down=w_down, final_norm=final_norm)
    m = dict(attn_norm=m_attn_norm, w_in=m_w_in, a_q_norm=m_a_q_norm, a_w_uq=m_a_w_uq, a_kv_norm=m_a_kv_norm,
             a_w_ukv=m_a_w_ukv, c_q_norm=m_c_q_norm, c_k_norm=m_c_k_norm, out_norm=m_out_norm, w_out=m_w_out,
             ffn_norm=m_ffn_norm, w_gate=m_w_gate, w_up=m_w_up, w_down=m_w_down, final_norm=m_final_norm)
    v = dict(attn_norm=v_attn_norm, w_in=v_w_in, a_q_norm=v_a_q_norm, a_w_uq=v_a_w_uq, a_kv_norm=v_a_kv_norm,
             a_w_ukv=v_a_w_ukv, c_q_norm=v_c_q_norm, c_k_norm=v_c_k_norm, out_norm=v_out_norm, w_out=v_w_out,
             ffn_norm=v_ffn_norm, w_gate=v_w_gate, w_up=v_w_up, w_down=v_w_down, final_norm=v_final_norm)
    _, s, d = x.shape
    depth = attn_norm.shape[0]
    t = _pick(s, 512)

    shards = [_cast_bf16(w[n].reshape(-1, w[n].shape[-1]), name="cast_" + n).reshape(w[n].shape) for n in _BIG]
    gathered = dict(zip(_BIG, _chip_exchange(shards, per_dest=False, name="gather_weights")))

    tabs = _rope_tables(s)
    bias = _band_bias(t, s // t)

    xs = x.reshape(s, d)
    saved, wls, sms = [], [], []
    for l in range(depth):
        wl = _assemble_weights(gathered, l)
        sm = {n: w[n][l][None, :] for n in _SMALL if n != "final_norm"}
        xs, sv = _layer_fwd(xs, wl, sm, tabs, bias, t)
        saved.append(sv)
        wls.append(wl)
        sms.append(sm)
    dx, g_final, loss_row = _final_loss(xs, final_norm[None, :], loss_target.reshape(s, d), name="final_loss")
    loss = lax.psum(loss_row[0, 0], ("x", "y", "c"))

    gw_layers, gs_layers = [None] * depth, [None] * depth
    for l in reversed(range(depth)):
        dx, gw_layers[l], gs_layers[l] = _layer_bwd(dx, saved[l], wls[l], sms[l], tabs, bias, t)
    grad_x = dx.reshape(x.shape)

    split = [_split_grads(g) for g in gw_layers]
    to_send = [jnp.stack([split[l][n] for l in range(depth)], axis=1) for n in _BIG]
    landed = _chip_exchange(to_send, per_dest=True, name="scatter_grads")
    sums = [_sum_chips(a.reshape(N_CHIPS, -1, a.shape[-1]), name="sum_" + n) for n, a in zip(_BIG, landed)]
    sib = _sibling_exchange(sums, name="swap_core_sums")
    grads, deltas, new_m, new_v = {}, {}, {}, {}
    for n, mine, other in zip(_BIG, sums, sib):
        shp = w[n].shape
        two_d = (-1, shp[-1])
        res = _adamw(mine, other, w[n].reshape(two_d), m[n].reshape(two_d), v[n].reshape(two_d), name="adamw_" + n)
        grads[n], deltas[n], new_m[n], new_v[n] = [r.reshape(shp) for r in res]

    gsm = {n: jnp.stack([gs_layers[l][n][0] for l in range(depth)]) for n in _SMALL if n != "final_norm"}
    gsm["final_norm"] = g_final[0]
    packed = _pack_small(gsm)
    everyone = _all_gather_small(packed, name="gather_gain_grads").reshape(N_DEV, packed.shape[0], LANE)
    res = _small_adamw(everyone, _pack_small(w), _pack_small(m), _pack_small(v), name="adamw_gains")
    for dst, r in zip((grads, deltas, new_m, new_v), res):
        dst.update(_unpack_small(r, w))

    return (loss, grad_x, *[grads[n] for n in _WEIGHTS], *[deltas[n] for n in _WEIGHTS],
            *[new_m[n] for n in _WEIGHTS], *[new_v[n] for n in _WEIGHTS])
```

```python
import functools
import math

import jax
import jax.numpy as jnp
import numpy as np
from jax import lax
from jax.experimental import pallas as pl
from jax.experimental.pallas import tpu as pltpu

F32 = jnp.float32
BF16 = jnp.bfloat16
MESH = pl.DeviceIdType.MESH

HEAD_DIM = 128
ROPE_THETA = 10000.0
GRID_W = 64
EPS = 1e-6
NEG = -1e30
A_HEADS, A_Q_RANK, A_KV_RANK, A_NOPE, A_ROPE, A_V = 4, 512, 512, 128, 64, 128
B_HEADS = 6
B_PATTERNS = ((128, 1), (512, 4), (2048, 16))
C_HEADS, C_KV_HEADS = 6, 2
C_GROUP = C_HEADS // C_KV_HEADS
A_WIDTH, B_WIDTH, C_WIDTH = A_HEADS * A_V, B_HEADS * HEAD_DIM, C_HEADS * HEAD_DIM
IN_A = A_Q_RANK + A_KV_RANK + A_ROPE
IN_B = 3 * B_WIDTH
IN_C = C_WIDTH + 2 * C_KV_HEADS * HEAD_DIM
ADAM_LR, ADAM_B1, ADAM_B2, ADAM_EPS, ADAM_WD, ADAM_STEP = 0.001, 0.9, 0.999, 1e-08, 0.01, 10

LANE = 128
SUBLANE = 8
VMEM_BYTES_V7X = 64 * 1024 * 1024
VMEM_LIMIT_CAP = VMEM_BYTES_V7X - 8 * 1024 * 1024
N_CHIPS = 4
N_DEV = 8

A_PAD = 12 * LANE
PB_CQ, PB_CKV, PB_KR = 0, 4, 8
PB_BQ, PB_BK, PB_BV = 12, 18, 24
PB_CQH, PB_CKH, PB_CVH = 30, 36, 38
NP = 40 * LANE
A_QK = 2 * LANE


def _pick(n, cap, mult=LANE):
    if n <= cap:
        return n
    t = cap - cap % mult
    while t >= mult:
        if n % t == 0:
            return t
        t -= mult
    return n


def _rows_for(width_bytes, n_rows, target=2 * 1024 * 1024):
    return _pick(n_rows, max(SUBLANE, target // max(width_bytes, 1)), SUBLANE)


def _params(est_bytes):
    limit = int(min(max(est_bytes + (4 << 20), 32 << 20), VMEM_LIMIT_CAP))
    return pltpu.CompilerParams(vmem_limit_bytes=limit)


def _isz(x):
    return jnp.dtype(x.dtype).itemsize


_DIMS = {"nn": (((1,), (0,)), ((), ())), "nt": (((1,), (1,)), ((), ())), "tn": (((0,), (0,)), ((), ()))}


def _matmul(a, b, *, mode, out_dtype, name, add=None, tm=512, tn=512):
    if mode == "tn":
        (k, m), (k2, n) = a.shape, b.shape
    elif mode == "nt":
        (m, k), (n, k2) = a.shape, b.shape
    else:
        (m, k), (k2, n) = a.shape, b.shape
    assert k == k2, (a.shape, b.shape, mode)
    tm, tn = _pick(m, tm), _pick(n, tn)
    a_spec = pl.BlockSpec((k, tm), lambda i, j: (0, i)) if mode == "tn" else pl.BlockSpec((tm, k), lambda i, j: (i, 0))
    b_spec = pl.BlockSpec((tn, k), lambda i, j: (j, 0)) if mode == "nt" else pl.BlockSpec((k, tn), lambda i, j: (0, j))
    o_spec = pl.BlockSpec((tm, tn), lambda i, j: (i, j))
    dims = _DIMS[mode]

    def body(*refs):
        if add is None:
            a_ref, b_ref, o_ref = refs
        else:
            a_ref, b_ref, add_ref, o_ref = refs
        acc = lax.dot_general(a_ref[...].astype(BF16), b_ref[...].astype(BF16), dims, preferred_element_type=F32)
        if add is not None:
            acc = acc + add_ref[...].astype(F32)
        o_ref[...] = acc.astype(out_dtype)

    ins, specs = [a, b], [a_spec, b_spec]
    if add is not None:
        ins.append(add)
        specs.append(o_spec)
    est = 2 * (tm * k * _isz(a) + tn * k * _isz(b) + tm * tn * (jnp.dtype(out_dtype).itemsize + (4 if add is not None else 0)))
    est += (tm + tn) * k * 2 + 2 * tm * tn * 4
    return pl.pallas_call(
        body, name=name, grid=(m // tm, n // tn), in_specs=specs, out_specs=o_spec,
        out_shape=jax.ShapeDtypeStruct((m, n), out_dtype), compiler_params=_params(est),
    )(*ins)


def _norm_fwd(x, gain, *, wb, cb, nb, shared_gain, out_dtype, name):
    s = x.shape[0]
    ts = _rows_for(wb * 4, s)

    def body(x_ref, g_ref, o_ref):
        xv = x_ref[...].astype(F32)
        r = lax.rsqrt(jnp.mean(xv * xv, axis=1, keepdims=True) + EPS)
        o_ref[...] = ((xv * r) * g_ref[...]).astype(out_dtype)

    return pl.pallas_call(
        body, name=name, grid=(nb, s // ts),
        in_specs=[pl.BlockSpec((ts, wb), lambda n, i: (i, cb + n)),
                  pl.BlockSpec((1, wb), (lambda n, i: (0, 0)) if shared_gain else (lambda n, i: (0, n)))],
        out_specs=pl.BlockSpec((ts, wb), lambda n, i: (i, n)),
        out_shape=jax.ShapeDtypeStruct((s, nb * wb), out_dtype), compiler_params=_params(6 * ts * wb * 4),
    )(x, gain)


def _norm_bwd(x, gain, dy, *, wb, cb, nb, shared_gain, out_dtype, name, dy_cb=0, add=None):
    s = x.shape[0]
    ts = _rows_for(wb * 4, s, target=1024 * 1024)
    gw = wb if shared_gain else nb * wb

    def body(*refs):
        if add is None:
            x_ref, g_ref, dy_ref, dx_ref, dg_ref = refs
        else:
            x_ref, g_ref, dy_ref, add_ref, dx_ref, dg_ref = refs
        n, i = pl.program_id(0), pl.program_id(1)
        xv = x_ref[...].astype(F32)
        dyv = dy_ref[...].astype(F32)
        r = lax.rsqrt(jnp.mean(xv * xv, axis=1, keepdims=True) + EPS)
        xh = xv * r
        dyg = dyv * g_ref[...]
        dx = r * (dyg - xh * jnp.mean(dyg * xh, axis=1, keepdims=True))
        if add is not None:
            dx = dx + add_ref[...]
        dx_ref[...] = dx.astype(out_dtype)
        first = jnp.logical_and(n == 0, i == 0) if shared_gain else (i == 0)

        @pl.when(first)
        def _():
            dg_ref[...] = jnp.zeros_like(dg_ref)

        dg_ref[...] += jnp.sum(dyv * xh, axis=0, keepdims=True)

    ins = [x, gain, dy]
    specs = [pl.BlockSpec((ts, wb), lambda n, i: (i, cb + n)),
             pl.BlockSpec((1, wb), (lambda n, i: (0, 0)) if shared_gain else (lambda n, i: (0, n))),
             pl.BlockSpec((ts, wb), lambda n, i: (i, dy_cb + n))]
    if add is not None:
        ins.append(add)
        specs.append(pl.BlockSpec((ts, wb), lambda n, i: (i, n)))
    return pl.pallas_call(
        body, name=name, grid=(nb, s // ts), in_specs=specs,
        out_specs=[pl.BlockSpec((ts, wb), lambda n, i: (i, n)),
                   pl.BlockSpec((1, wb), (lambda n, i: (0, 0)) if shared_gain else (lambda n, i: (0, n)))],
        out_shape=[jax.ShapeDtypeStruct((s, nb * wb), out_dtype), jax.ShapeDtypeStruct((1, gw), F32)],
        compiler_params=_params(12 * ts * wb * 4),
    )(*ins)


def _swap_halves(x, half):
    if 2 * half == LANE:
        return pltpu.roll(x, half, axis=1)
    lane = lax.broadcasted_iota(jnp.int32, x.shape, 1)
    first = jnp.bitwise_and(lane, 2 * half - 1) < half
    return jnp.where(first, pltpu.roll(x, LANE - half, axis=1), pltpu.roll(x, half, axis=1))


def _rope(x, cos_t, sin_t, *, tw, cb, nb, half, sign, out_dtype, name):
    s = x.shape[0]
    ts = _rows_for(tw * 4, s)

    def body(x_ref, c_ref, s_ref, o_ref):
        for q in range(tw // LANE):
            sl = slice(q * LANE, (q + 1) * LANE)
            xv = x_ref[:, sl].astype(F32)
            sv = s_ref[:, sl]
            if sign < 0:
                sv = -sv
            o_ref[:, sl] = (xv * c_ref[:, sl] + _swap_halves(xv, half) * sv).astype(out_dtype)

    return pl.pallas_call(
        body, name=name, grid=(nb, s // ts),
        in_specs=[pl.BlockSpec((ts, tw), lambda n, i: (i, cb + n)),
                  pl.BlockSpec((ts, tw), lambda n, i: (i, 0)),
                  pl.BlockSpec((ts, tw), lambda n, i: (i, 0))],
        out_specs=pl.BlockSpec((ts, tw), lambda n, i: (i, n)),
        out_shape=jax.ShapeDtypeStruct((s, nb * tw), out_dtype), compiler_params=_params(10 * ts * tw * 4),
    )(x, cos_t, sin_t)


def _group_sum(x, *, n_out, g, src, out_dtype, name):
    s = x.shape[0]
    ts = _rows_for(LANE * 4, s)

    def body(*refs):
        acc = refs[0][...].astype(F32)
        for r in refs[1:-1]:
            acc = acc + r[...].astype(F32)
        refs[-1][...] = acc.astype(out_dtype)

    return pl.pallas_call(
        body, name=name, grid=(n_out, s // ts),
        in_specs=[pl.BlockSpec((ts, LANE), functools.partial(lambda n, i, j: (i, src(n, j)), j=j)) for j in range(g)],
        out_specs=pl.BlockSpec((ts, LANE), lambda n, i: (i, n)),
        out_shape=jax.ShapeDtypeStruct((s, n_out * LANE), out_dtype), compiler_params=_params(4 * g * ts * LANE * 4),
    )(*([x] * g))


def _flash_fwd(q, k, v, *, hkv, g, dqk, q_cb, k_cb, v_cb, v_step, scale, t, bias, name):
    s = q.shape[0]
    n = s // t
    hq = hkv * g
    banded = bias is not None
    if banded:
        nkb = bias.shape[0]
        wb = (nkb - 1) // 2
    else:
        nkb, wb = n, 0

    def kv_blk(i, j):
        return jnp.clip(i + j - wb, 0, n - 1) if banded else j

    def body(*refs):
        if banded:
            q_ref, k_ref, v_ref, b_ref, o_ref, lse_ref, m_s, l_s, acc_s = refs
        else:
            q_ref, k_ref, v_ref, o_ref, lse_ref, m_s, l_s, acc_s = refs
        i, j = pl.program_id(1), pl.program_id(2)

        @pl.when(j == 0)
        def _():
            m_s[...] = jnp.full_like(m_s, NEG)
            l_s[...] = jnp.zeros_like(l_s)
            acc_s[...] = jnp.zeros_like(acc_s)

        def step():
            sc = lax.dot_general(q_ref[...].astype(BF16), k_ref[...].astype(BF16), _DIMS["nt"],
                                 preferred_element_type=F32) * scale
            if banded:
                sc = sc + b_ref[0]
            m_prev = m_s[...]
            m_new = jnp.maximum(m_prev, jnp.max(sc, axis=1, keepdims=True))
            alpha = jnp.exp(m_prev - m_new)
            p = jnp.exp(sc - m_new)
            l_s[...] = alpha * l_s[...] + jnp.sum(p, axis=1, keepdims=True)
            acc_s[...] = alpha * acc_s[...] + jnp.dot(p.astype(BF16), v_ref[...].astype(BF16), preferred_element_type=F32)
            m_s[...] = m_new

        if banded:
            pl.when(jnp.logical_and(i + j - wb >= 0, i + j - wb < n))(step)
        else:
            step()

        @pl.when(j == nkb - 1)
        def _():
            o_ref[...] = acc_s[...] / l_s[...]
            lse_ref[0] = jnp.broadcast_to(m_s[...] + jnp.log(l_s[...]), (t, LANE))

    ins = [q, k, v]
    specs = [pl.BlockSpec((t, dqk), lambda h, i, j: (i, q_cb + h)),
             pl.BlockSpec((t, dqk), lambda h, i, j: (kv_blk(i, j), k_cb + h // g)),
             pl.BlockSpec((t, LANE), lambda h, i, j: (kv_blk(i, j), v_cb + v_step * (h // g)))]
    if banded:
        ins.append(bias)
        specs.append(pl.BlockSpec((1, t, t), lambda h, i, j: (j, 0, 0)))
    est = 2 * (2 * t * dqk * 4 + t * LANE * 4 * 3 + t * t * 4) + 8 * t * t * 4
    return pl.pallas_call(
        body, name=name, grid=(hq, n, nkb), in_specs=specs,
        out_specs=[pl.BlockSpec((t, LANE), lambda h, i, j: (i, h)),
                   pl.BlockSpec((1, t, LANE), lambda h, i, j: (h, i, 0))],
        out_shape=[jax.ShapeDtypeStruct((s, hq * LANE), F32), jax.ShapeDtypeStruct((hq, s, LANE), F32)],
        scratch_shapes=[pltpu.VMEM((t, 1), F32), pltpu.VMEM((t, 1), F32), pltpu.VMEM((t, LANE), F32)],
        compiler_params=_params(est),
    )(*ins)


def _flash_bwd(q, k, v, o, do, lse, *, hkv, g, dqk, q_cb, k_cb, v_cb, v_step, scale, t, bias, dv_dtype, name):
    s = q.shape[0]
    n = s // t
    hq = hkv * g
    banded = bias is not None
    if banded:
        nkb = bias.shape[0]
        wb = (nkb - 1) // 2
    else:
        nkb, wb = n, 0

    def q_blk(kj, tt):
        return jnp.clip(kj + tt - wb, 0, n - 1) if banded else tt

    def body(*refs):
        if banded:
            q_ref, k_ref, v_ref, o_ref, do_ref, lse_ref, b_ref, dq_ref, dk_ref, dv_ref, dk_s, dv_s = refs
        else:
            q_ref, k_ref, v_ref, o_ref, do_ref, lse_ref, dq_ref, dk_ref, dv_ref, dk_s, dv_s = refs
        kj, tt = pl.program_id(1), pl.program_id(2)

        @pl.when(jnp.logical_and(kj == 0, tt == 0))
        def _():
            dq_ref[...] = jnp.zeros_like(dq_ref)

        @pl.when(tt == 0)
        def _():
            dk_s[...] = jnp.zeros_like(dk_s)
            dv_s[...] = jnp.zeros_like(dv_s)

        def step():
            qv = q_ref[...].astype(BF16)
            kv = k_ref[...].astype(BF16)
            dof = do_ref[...].astype(F32)
            dov = dof.astype(BF16)
            sc = lax.dot_general(qv, kv, _DIMS["nt"], preferred_element_type=F32) * scale
            if banded:
                sc = sc + b_ref[0]
            p = jnp.exp(sc - lse_ref[0, :, 0:1])
            dp = lax.dot_general(dov, v_ref[...].astype(BF16), _DIMS["nt"], preferred_element_type=F32)
            delta = jnp.sum(dof * o_ref[...], axis=1, keepdims=True)
            ds = (p * (dp - delta) * scale).astype(BF16)
            dv_s[...] += lax.dot_general(p.astype(BF16), dov, _DIMS["tn"], preferred_element_type=F32)
            dk_s[...] += lax.dot_general(ds, qv, _DIMS["tn"], preferred_element_type=F32)
            rows = pl.ds(pl.multiple_of(q_blk(kj, tt) * t, t), t)
            dq_ref[rows, :] += jnp.dot(ds, kv, preferred_element_type=F32)

        if banded:
            pl.when(jnp.logical_and(kj + tt - wb >= 0, kj + tt - wb < n))(step)
        else:
            step()

        @pl.when(tt == nkb - 1)
        def _():
            dk_ref[...] = dk_s[...]
            dv_ref[...] = dv_s[...].astype(dv_dtype)

    ins = [q, k, v, o, do, lse]
    specs = [pl.BlockSpec((t, dqk), lambda h, kj, tt: (q_blk(kj, tt), q_cb + h)),
             pl.BlockSpec((t, dqk), lambda h, kj, tt: (kj, k_cb + h // g)),
             pl.BlockSpec((t, LANE), lambda h, kj, tt: (kj, v_cb + v_step * (h // g))),
             pl.BlockSpec((t, LANE), lambda h, kj, tt: (q_blk(kj, tt), h)),
             pl.BlockSpec((t, LANE), lambda h, kj, tt: (q_blk(kj, tt), h)),
             pl.BlockSpec((1, t, LANE), lambda h, kj, tt: (h, q_blk(kj, tt), 0))]
    if banded:
        ins.append(bias)
        specs.append(pl.BlockSpec((1, t, t), lambda h, kj, tt: (jnp.clip(2 * wb - tt, 0, nkb - 1), 0, 0)))
    est = 2 * (3 * t * dqk * 4 + 5 * t * LANE * 4 + t * t * 4 + s * dqk * 4) + 2 * t * dqk * 4 + 10 * t * t * 4
    return pl.pallas_call(
        body, name=name, grid=(hq, n, nkb), in_specs=specs,
        out_specs=[pl.BlockSpec((s, dqk), lambda h, kj, tt: (0, h)),
                   pl.BlockSpec((t, dqk), lambda h, kj, tt: (kj, h)),
                   pl.BlockSpec((t, LANE), lambda h, kj, tt: (kj, h))],
        out_shape=[jax.ShapeDtypeStruct((s, hq * dqk), F32), jax.ShapeDtypeStruct((s, hq * dqk), F32),
                   jax.ShapeDtypeStruct((s, hq * LANE), dv_dtype)],
        scratch_shapes=[pltpu.VMEM((t, dqk), F32), pltpu.VMEM((t, LANE), F32)],
        compiler_params=_params(est),
    )(*ins)


def _swiglu_fwd(gate, up, *, name):
    s, f = gate.shape
    ts, tf = _pick(s, 512, SUBLANE), _pick(f, 1024)

    def body(g_ref, u_ref, o_ref):
        gv = g_ref[...]
        o_ref[...] = (gv / (1.0 + jnp.exp(-gv)) * u_ref[...]).astype(BF16)

    spec = pl.BlockSpec((ts, tf), lambda i, j: (i, j))
    return pl.pallas_call(
        body, name=name, grid=(s // ts, f // tf), in_specs=[spec, spec], out_specs=spec,
        out_shape=jax.ShapeDtypeStruct((s, f), BF16), compiler_params=_params(8 * ts * tf * 4),
    )(gate, up)


def _swiglu_bwd(gate, up, dact, *, name):
    s, f = gate.shape
    ts, tf = _pick(s, 512, SUBLANE), _pick(f, 1024)

    def body(g_ref, u_ref, d_ref, dg_ref, du_ref):
        gv, uv, dv = g_ref[...], u_ref[...], d_ref[...]
        sig = 1.0 / (1.0 + jnp.exp(-gv))
        dg_ref[...] = (dv * uv * (sig * (1.0 + gv * (1.0 - sig)))).astype(BF16)
        du_ref[...] = (dv * (gv * sig)).astype(BF16)

    spec = pl.BlockSpec((ts, tf), lambda i, j: (i, j))
    return pl.pallas_call(
        body, name=name, grid=(s // ts, f // tf), in_specs=[spec, spec, spec], out_specs=[spec, spec],
        out_shape=[jax.ShapeDtypeStruct((s, f), BF16)] * 2, compiler_params=_params(12 * ts * tf * 4),
    )(gate, up, dact)


def _final_loss(x, gain, target, *, name):
    s, d = x.shape
    ts = _rows_for(d * 4, s, target=1024 * 1024)

    def body(x_ref, g_ref, t_ref, dx_ref, dg_ref, loss_ref):
        i = pl.program_id(0)
        xv = x_ref[...]
        gv = g_ref[...]
        r = lax.rsqrt(jnp.mean(xv * xv, axis=1, keepdims=True) + EPS)
        xh = xv * r
        err = xh * gv - t_ref[...]
        dy = err / d
        dyg = dy * gv
        dx_ref[...] = r * (dyg - xh * jnp.mean(dyg * xh, axis=1, keepdims=True))

        @pl.when(i == 0)
        def _():
            dg_ref[...] = jnp.zeros_like(dg_ref)
            loss_ref[...] = jnp.zeros_like(loss_ref)

        dg_ref[...] += jnp.sum(dy * xh, axis=0, keepdims=True)
        part = jnp.sum(jnp.mean(err * err, axis=1, keepdims=True), axis=0, keepdims=True)
        loss_ref[...] += jnp.broadcast_to(0.5 * part, (1, LANE))

    row = pl.BlockSpec((ts, d), lambda i: (i, 0))
    return pl.pallas_call(
        body, name=name, grid=(s // ts,),
        in_specs=[row, pl.BlockSpec((1, d), lambda i: (0, 0)), row],
        out_specs=[row, pl.BlockSpec((1, d), lambda i: (0, 0)), pl.BlockSpec((1, LANE), lambda i: (0, 0))],
        out_shape=[jax.ShapeDtypeStruct((s, d), F32), jax.ShapeDtypeStruct((1, d), F32),
                   jax.ShapeDtypeStruct((1, LANE), F32)],
        compiler_params=_params(12 * ts * d * 4),
    )(x, gain, target)


def _cast_to_slot(x2d, me, *, layer, rows, name):
    c = x2d.shape[1]
    tr = _rows_for(c * 4, rows)
    nt = rows // tr

    def body(me_ref, x_ref, o_ref):
        o_ref[...] = x_ref[...].astype(BF16)

    return pl.pallas_call(
        body, name=name,
        grid_spec=pltpu.PrefetchScalarGridSpec(
            num_scalar_prefetch=1, grid=(nt,),
            in_specs=[pl.BlockSpec((tr, c), lambda i, me_ref: (layer * nt + i, 0))],
            out_specs=pl.BlockSpec((None, tr, c), lambda i, me_ref: (me_ref[0], i, 0))),
        out_shape=jax.ShapeDtypeStruct((N_CHIPS, rows, c), BF16), compiler_params=_params(6 * tr * c * 4),
    )(me, x2d)


def _sum_parts(srcs, lands, me, *, name):
    depth = len(srcs)
    _, r, c = srcs[0].shape
    tr = _rows_for(c * 4, r, target=1024 * 1024)
    nt = r // tr

    def body(me_ref, *refs):
        o_ref = refs[-1]
        l = pl.program_id(0)
        for k in range(depth):
            @pl.when(l == k)
            def _(k=k):
                acc = refs[k][...].astype(F32)
                for p in range(3):
                    acc = acc + refs[depth + k][p].astype(F32)
                o_ref[...] = acc

    def rows_of(k):
        return lambda l, i, me_ref: jnp.where(l == k, i, jnp.where(l < k, 0, nt - 1))

    in_specs = [pl.BlockSpec((None, tr, c), functools.partial(lambda l, i, me_ref, f: (me_ref[0], f(l, i, me_ref), 0), f=rows_of(k)))
                for k in range(depth)]
    in_specs += [pl.BlockSpec((3, tr, c), functools.partial(lambda l, i, me_ref, f: (0, f(l, i, me_ref), 0), f=rows_of(k)))
                 for k in range(depth)]
    return pl.pallas_call(
        body, name=name,
        grid_spec=pltpu.PrefetchScalarGridSpec(
            num_scalar_prefetch=1, grid=(depth, nt), in_specs=in_specs,
            out_specs=pl.BlockSpec((tr, c), lambda l, i, me_ref: (l * nt + i, 0))),
        out_shape=jax.ShapeDtypeStruct((depth * r, c), F32), compiler_params=_params(depth * 10 * tr * c * 4),
    )(me, *srcs, *lands)


def _adamw_math(w, g, m, v):
    m2 = ADAM_B1 * m + (1.0 - ADAM_B1) * g
    v2 = ADAM_B2 * v + (1.0 - ADAM_B2) * (g * g)
    m_hat = m2 / (1.0 - ADAM_B1 ** ADAM_STEP)
    v_hat = v2 / (1.0 - ADAM_B2 ** ADAM_STEP)
    delta = -ADAM_LR * (m_hat / (jnp.sqrt(v_hat) + ADAM_EPS) + ADAM_WD * w)
    return delta, m2, v2


def _adamw(g_a, g_b, w, m, v, *, name):
    r, c = w.shape
    tr = _rows_for(c * 4, r, target=512 * 1024)

    def body(a_ref, b_ref, w_ref, m_ref, v_ref, g_out, d_out, m_out, v_out):
        gv = a_ref[...] + b_ref[...]
        delta, m2, v2 = _adamw_math(w_ref[...], gv, m_ref[...], v_ref[...])
        g_out[...] = gv
        d_out[...] = delta
        m_out[...] = m2
        v_out[...] = v2

    spec = pl.BlockSpec((tr, c), lambda i: (i, 0))
    return pl.pallas_call(
        body, name=name, grid=(r // tr,), in_specs=[spec] * 5, out_specs=[spec] * 4,
        out_shape=[jax.ShapeDtypeStruct((r, c), F32)] * 4, compiler_params=_params(22 * tr * c * 4),
    )(g_a, g_b, w, m, v)


def _small_adamw(g_all, w, m, v, *, name):
    r, c = w.shape

    def body(ga_ref, w_ref, m_ref, v_ref, g_out, d_out, m_out, v_out):
        gv = ga_ref[0]
        for j in range(1, N_DEV):
            gv = gv + ga_ref[j]
        delta, m2, v2 = _adamw_math(w_ref[...], gv, m_ref[...], v_ref[...])
        g_out[...] = gv
        d_out[...] = delta
        m_out[...] = m2
        v_out[...] = v2

    return pl.pallas_call(body, name=name, out_shape=[jax.ShapeDtypeStruct((r, c), F32)] * 4)(g_all, w, m, v)


_ANY = pl.BlockSpec(memory_space=pl.ANY)


_HBM = pl.BlockSpec(memory_space=pltpu.HBM)
_SEM = pl.BlockSpec(memory_space=pltpu.SEMAPHORE)
_EFFECT = pltpu.SideEffectType.DATAFLOW_SIDE_EFFECTING


def _peer_chips():
    x, y = lax.axis_index("x"), lax.axis_index("y")
    return 2 * x + y, [(1 - x, y), (x, 1 - y), (1 - x, 1 - y)]


def _exchange_copy(srcs, lands, send_sems, recv_sems, k, p, scatter):
    me, peers = _peer_chips()
    px, py = peers[p]
    return pltpu.make_async_remote_copy(
        src_ref=srcs[k].at[2 * px + py] if scatter else srcs[k].at[me],
        dst_ref=lands[k].at[p] if scatter else lands[k].at[me],
        send_sem=send_sems.at[3 * k + p], recv_sem=recv_sems.at[3 * k + p],
        device_id=(px, py, lax.axis_index("c")), device_id_type=MESH)


def _exchange_start(srcs, lands, after, *, name):
    scatter = lands is not None
    n = len(srcs)
    bufs = list(srcs) + (list(lands) if scatter else [])
    nb = len(bufs)

    def body(*refs):
        buf_refs, send_sems, recv_sems = refs[:nb], refs[nb + 1], refs[nb + 2]
        token = refs[-1]
        s_refs = buf_refs[:n]
        l_refs = buf_refs[n:] if scatter else s_refs
        for k in range(n):
            for p in range(3):
                _exchange_copy(s_refs, l_refs, send_sems, recv_sems, k, p, scatter).start()
        token[...] = jnp.zeros_like(token)

    out = pl.pallas_call(
        body, name=name,
        out_shape=(pltpu.SemaphoreType.DMA((3 * n,)), pltpu.SemaphoreType.DMA((3 * n,)),
                   *[pltpu.HBM(b.shape, b.dtype) for b in bufs], jax.ShapeDtypeStruct((SUBLANE, LANE), F32)),
        in_specs=[_HBM] * nb + [_ANY],
        out_specs=(_SEM, _SEM, *[_HBM] * nb, pl.BlockSpec(memory_space=pltpu.VMEM)),
        input_output_aliases={i: 2 + i for i in range(nb)},
        compiler_params=pltpu.CompilerParams(has_side_effects=_EFFECT),
    )(*[pltpu.with_memory_space_constraint(b, pltpu.HBM) for b in bufs], after)
    send_sems, recv_sems = out[0], out[1]
    thru = out[2:2 + nb]
    return send_sems, recv_sems, list(thru[:n]), (list(thru[n:]) if scatter else None), out[-1]


def _exchange_wait(send_sems, recv_sems, srcs, lands, after, *, name):
    scatter = lands is not None
    n = len(srcs)
    bufs = list(srcs) + (list(lands) if scatter else [])
    nb = len(bufs)

    def body(*refs):
        buf_refs, send_sems_ref, recv_sems_ref = refs[:nb], refs[nb], refs[nb + 1]
        s_refs = buf_refs[:n]
        l_refs = buf_refs[n:] if scatter else s_refs
        for k in range(n):
            for p in range(3):
                cp = _exchange_copy(s_refs, l_refs, send_sems_ref, recv_sems_ref, k, p, scatter)
                cp.wait_send()
                cp.wait_recv()

    out = pl.pallas_call(
        body, name=name, out_shape=tuple(pltpu.HBM(b.shape, b.dtype) for b in bufs),
        in_specs=[_HBM] * nb + [_SEM, _SEM, _ANY], out_specs=tuple([_HBM] * nb),
        input_output_aliases={i: i for i in range(nb)},
        compiler_params=pltpu.CompilerParams(has_side_effects=_EFFECT),
    )(*bufs, send_sems, recv_sems, after)
    return list(out)


def _sibling_exchange(srcs, *, name):
    n = len(srcs)

    def body(*refs):
        src, out = refs[:n], refs[n:2 * n]
        send_sems, recv_sems = refs[2 * n:]
        sibling = (lax.axis_index("x"), lax.axis_index("y"), 1 - lax.axis_index("c"))
        copies = [pltpu.make_async_remote_copy(src_ref=src[k], dst_ref=out[k], send_sem=send_sems.at[k],
                                               recv_sem=recv_sems.at[k], device_id=sibling, device_id_type=MESH)
                  for k in range(n)]
        for cp in copies:
            cp.start()
        for cp in copies:
            cp.wait_recv()
        for cp in copies:
            cp.wait_send()

    return pl.pallas_call(
        body, name=name, in_specs=[_ANY] * n, out_specs=[_ANY] * n,
        out_shape=[jax.ShapeDtypeStruct(a.shape, a.dtype) for a in srcs],
        scratch_shapes=[pltpu.SemaphoreType.DMA((n,)), pltpu.SemaphoreType.DMA((n,))],
    )(*srcs)


def _all_gather_small(block, *, name):
    m_per, ncol = block.shape

    def body(x_ref, out_ref, send_sems, recv_sems, local_sem):
        x, y, c = lax.axis_index("x"), lax.axis_index("y"), lax.axis_index("c")
        me, sibling = (x, y, c), (x, y, 1 - c)
        chips = [(1 - x, y), (x, 1 - y), (1 - x, 1 - y)]

        def rows(px, py, pc):
            return out_ref.at[pl.ds((4 * px + 2 * py + pc) * m_per, m_per), :]

        def copy(k, blk, to, src=None):
            return pltpu.make_async_remote_copy(
                src_ref=rows(*blk) if src is None else src, dst_ref=rows(*blk),
                send_sem=send_sems.at[k], recv_sem=recv_sems.at[k], device_id=to, device_id_type=MESH)

        mine = pltpu.make_async_copy(x_ref, rows(*me), local_sem)
        mine.start()
        first = [copy(0, me, sibling, src=x_ref)]
        first += [copy(1 + j, me, (*chip, c), src=x_ref) for j, chip in enumerate(chips)]
        for cp in first:
            cp.start()
        passed = [copy(4 + j, (*chip, c), sibling) for j, chip in enumerate(chips)]
        for j, chip in enumerate(chips):
            copy(1 + j, (*chip, c), me).wait_recv()
            passed[j].start()
        copy(0, sibling, me).wait_recv()
        for j, chip in enumerate(chips):
            copy(4 + j, (*chip, 1 - c), me).wait_recv()
        for cp in first + passed:
            cp.wait_send()
        mine.wait()

    return pl.pallas_call(
        body, name=name, out_shape=jax.ShapeDtypeStruct((N_DEV * m_per, ncol), block.dtype),
        in_specs=[pl.BlockSpec(memory_space=pltpu.VMEM)], out_specs=pl.BlockSpec(memory_space=pltpu.VMEM),
        scratch_shapes=[pltpu.SemaphoreType.DMA((7,)), pltpu.SemaphoreType.DMA((7,)), pltpu.SemaphoreType.DMA],
    )(block)


def _rope_angles(pos, dim):
    inv = ROPE_THETA ** (-jnp.arange(0, dim, 2, dtype=F32) / dim)
    return pos.astype(F32)[:, None] * inv[None, :]


def _rope_tables(s):
    pos = jnp.arange(s, dtype=jnp.int32)
    rows = s // GRID_W
    row = jnp.repeat(jnp.arange(rows, dtype=jnp.int32), GRID_W)
    col = jnp.tile(jnp.arange(GRID_W, dtype=jnp.int32), rows)
    a1 = _rope_angles(pos, HEAD_DIM)
    aa = _rope_angles(pos, A_ROPE)
    ar = _rope_angles(row, HEAD_DIM // 2)
    ac = _rope_angles(col, HEAD_DIM // 2)
    one = jnp.ones((s, LANE), F32)
    zero = jnp.zeros((s, LANE), F32)
    pad = LANE - A_ROPE
    cos_a = jnp.concatenate([one, jnp.cos(aa), jnp.cos(aa), jnp.ones((s, pad), F32)], axis=1)
    sin_a = jnp.concatenate([zero, -jnp.sin(aa), jnp.sin(aa), jnp.zeros((s, pad), F32)], axis=1)
    cos_b = jnp.concatenate([jnp.cos(a1), jnp.cos(a1)], axis=1)
    sin_b = jnp.concatenate([-jnp.sin(a1), jnp.sin(a1)], axis=1)
    cos_c = jnp.concatenate([jnp.cos(ar), jnp.cos(ar), jnp.cos(ac), jnp.cos(ac)], axis=1)
    sin_c = jnp.concatenate([-jnp.sin(ar), jnp.sin(ar), -jnp.sin(ac), jnp.sin(ac)], axis=1)
    return (cos_a, sin_a), (cos_b, sin_b), (cos_c, sin_c)


def _band_bias(t, n):
    reach = max((w // (2 * d)) * d for w, d in B_PATTERNS)
    wb = min(-(-reach // t), n - 1)
    j = jnp.arange(2 * wb + 1, dtype=jnp.int32)[:, None, None]
    r = jnp.arange(t, dtype=jnp.int32)[None, :, None]
    c = jnp.arange(t, dtype=jnp.int32)[None, None, :]
    rel = (j - wb) * t + c - r
    mult = jnp.zeros(rel.shape, F32)
    for w, d in B_PATTERNS:
        mult = mult + jnp.logical_and(rel % d == 0, jnp.abs(rel) <= (w // (2 * d)) * d).astype(F32)
    return jnp.where(mult > 0, jnp.log(jnp.maximum(mult, 1.0)), NEG)


_BIG = ("w_in", "a_w_uq", "a_w_ukv", "w_out", "w_gate", "w_up", "w_down")
_SMALL = ("attn_norm", "a_q_norm", "a_kv_norm", "c_q_norm", "c_k_norm", "out_norm", "ffn_norm", "final_norm")
_WEIGHTS = ("attn_norm", "w_in", "a_q_norm", "a_w_uq", "a_kv_norm", "a_w_ukv", "c_q_norm", "c_k_norm", "out_norm",
            "w_out", "ffn_norm", "w_gate", "w_up", "w_down", "final_norm")


_ATTN = ("w_in", "a_w_uq", "a_w_ukv", "w_out")
_FFN = ("w_gate", "w_up", "w_down")


def _from_cols(a):
    return jnp.transpose(a, (1, 0, 2)).reshape(a.shape[1], N_CHIPS * a.shape[2])


def _from_rows(a):
    return a.reshape(N_CHIPS * a.shape[1], a.shape[2])


def _to_cols(a):
    return jnp.transpose(a.reshape(a.shape[0], N_CHIPS, a.shape[1] // N_CHIPS), (1, 0, 2))


def _to_rows(a):
    return a.reshape(N_CHIPS, a.shape[0] // N_CHIPS, a.shape[1])


def _assemble_attn(gw):
    w_in, uq, ukv, w_out = _from_cols(gw[0]), _from_cols(gw[1]), _from_cols(gw[2]), _from_rows(gw[3])
    d = w_in.shape[0]
    w_all = jnp.concatenate([w_in[:, :IN_A], jnp.zeros((d, A_PAD - IN_A), BF16), w_in[:, IN_A:]], axis=1)
    uq = uq.reshape(A_Q_RANK, A_HEADS, A_NOPE + A_ROPE)
    uq = jnp.pad(uq, ((0, 0), (0, 0), (0, A_QK - A_NOPE - A_ROPE))).reshape(A_Q_RANK, A_HEADS * A_QK)
    return dict(w_all=w_all, uq=uq, ukv=ukv, w_out=w_out)


def _assemble_ffn(gw):
    return dict(w_gate=_from_cols(gw[0]), w_up=_from_cols(gw[1]), w_down=_from_rows(gw[2]))


def _split_attn_grads(gl):
    w_all = gl["w_all"]
    w_in = jnp.concatenate([w_all[:, :IN_A], w_all[:, A_PAD:]], axis=1)
    uq = gl["uq"].reshape(A_Q_RANK, A_HEADS, A_QK)[:, :, :A_NOPE + A_ROPE].reshape(A_Q_RANK, A_HEADS * (A_NOPE + A_ROPE))
    return [_to_cols(w_in), _to_cols(uq), _to_cols(gl["ukv"]), _to_rows(gl["w_out"])]


def _split_ffn_grads(gl):
    return [_to_cols(gl["w_gate"]), _to_cols(gl["w_up"]), _to_rows(gl["w_down"])]


def _tie(a, token):
    return a + token[0:1, 0:1]


def _layer_fwd(x, wl, ffn_weights, sm, tabs, bias, t):
    s = x.shape[0]
    (cos_a, sin_a), (cos_b, sin_b), (cos_c, sin_c) = tabs
    h = _norm_fwd(x, sm["attn_norm"], wb=x.shape[1], cb=0, nb=1, shared_gain=True, out_dtype=BF16, name="attn_norm_fwd")
    p = _matmul(h, wl["w_all"], mode="nn", out_dtype=F32, name="in_proj", tm=1024, tn=640)
    cq_n = _norm_fwd(p, sm["a_q_norm"], wb=A_Q_RANK, cb=0, nb=1, shared_gain=True, out_dtype=BF16, name="a_q_norm_fwd")
    ckv_n = _norm_fwd(p, sm["a_kv_norm"], wb=A_KV_RANK, cb=1, nb=1, shared_gain=True, out_dtype=BF16, name="a_kv_norm_fwd")
    qa_raw = _matmul(cq_n, wl["uq"], mode="nn", out_dtype=F32, name="a_uq", tm=1024, tn=1024)
    kv = _matmul(ckv_n, wl["ukv"], mode="nn", out_dtype=F32, name="a_ukv", tm=1024, tn=1024)
    k_nope = kv.reshape(s, A_HEADS, 2, LANE)[:, :, 0]
    k_rope = jnp.broadcast_to(p[:, PB_KR * LANE:(PB_KR + 1) * LANE][:, None, :], (s, A_HEADS, LANE))
    ka_raw = jnp.stack([k_nope, k_rope], axis=2).reshape(s, A_HEADS * A_QK)
    qa = _rope(qa_raw, cos_a, sin_a, tw=A_QK, cb=0, nb=A_HEADS, half=A_ROPE // 2, sign=1, out_dtype=BF16, name="a_rope_q")
    ka = _rope(ka_raw, cos_a, sin_a, tw=A_QK, cb=0, nb=A_HEADS, half=A_ROPE // 2, sign=1, out_dtype=BF16, name="a_rope_k")
    oa, lse_a = _flash_fwd(qa, ka, kv, hkv=A_HEADS, g=1, dqk=A_QK, q_cb=0, k_cb=0, v_cb=1, v_step=2,
                           scale=(A_NOPE + A_ROPE) ** -0.5, t=t, bias=None, name="a_flash_fwd")
    qb = _rope(p, cos_b, sin_b, tw=LANE, cb=PB_BQ, nb=B_HEADS, half=HEAD_DIM // 2, sign=1, out_dtype=BF16, name="b_rope_q")
    kb = _rope(p, cos_b, sin_b, tw=LANE, cb=PB_BK, nb=B_HEADS, half=HEAD_DIM // 2, sign=1, out_dtype=BF16, name="b_rope_k")
    ob, lse_b = _flash_fwd(qb, kb, p, hkv=B_HEADS, g=1, dqk=LANE, q_cb=0, k_cb=0, v_cb=PB_BV, v_step=1,
                           scale=HEAD_DIM ** -0.5, t=t, bias=bias, name="b_flash_fwd")
    qn = _norm_fwd(p, sm["c_q_norm"], wb=LANE, cb=PB_CQH, nb=C_HEADS, shared_gain=True, out_dtype=F32, name="c_q_norm_fwd")
    kn = _norm_fwd(p, sm["c_k_norm"], wb=LANE, cb=PB_CKH, nb=C_KV_HEADS, shared_gain=True, out_dtype=F32, name="c_k_norm_fwd")
    qc = _rope(qn, cos_c, sin_c, tw=LANE, cb=0, nb=C_HEADS, half=HEAD_DIM // 4, sign=1, out_dtype=BF16, name="c_rope_q")
    kc = _rope(kn, cos_c, sin_c, tw=LANE, cb=0, nb=C_KV_HEADS, half=HEAD_DIM // 4, sign=1, out_dtype=BF16, name="c_rope_k")
    oc, lse_c = _flash_fwd(qc, kc, p, hkv=C_KV_HEADS, g=C_GROUP, dqk=LANE, q_cb=0, k_cb=0, v_cb=PB_CVH, v_step=1,
                           scale=HEAD_DIM ** -0.5, t=t, bias=None, name="c_flash_fwd")
    g_out = sm["out_norm"]
    ga, gb, gc = g_out[:, :A_WIDTH], g_out[:, A_WIDTH:A_WIDTH + B_WIDTH], g_out[:, A_WIDTH + B_WIDTH:]
    ya = _norm_fwd(oa, ga, wb=A_WIDTH, cb=0, nb=1, shared_gain=True, out_dtype=BF16, name="out_norm_a_fwd")
    yb = _norm_fwd(ob, gb, wb=B_WIDTH, cb=0, nb=1, shared_gain=True, out_dtype=BF16, name="out_norm_b_fwd")
    yc = _norm_fwd(oc, gc, wb=C_WIDTH, cb=0, nb=1, shared_gain=True, out_dtype=BF16, name="out_norm_c_fwd")
    y = jnp.concatenate([ya, yb, yc], axis=1)
    x1 = _matmul(y, wl["w_out"], mode="nn", out_dtype=F32, name="out_proj", add=x, tm=1024, tn=512)
    wl = {**wl, **ffn_weights(x1)}
    h2 = _norm_fwd(x1, sm["ffn_norm"], wb=x.shape[1], cb=0, nb=1, shared_gain=True, out_dtype=BF16, name="ffn_norm_fwd")
    gate = _matmul(h2, wl["w_gate"], mode="nn", out_dtype=F32, name="ffn_gate", tm=1024, tn=512)
    up = _matmul(h2, wl["w_up"], mode="nn", out_dtype=F32, name="ffn_up", tm=1024, tn=512)
    act = _swiglu_fwd(gate, up, name="swiglu_fwd")
    x2 = _matmul(act, wl["w_down"], mode="nn", out_dtype=F32, name="ffn_down", add=x1, tm=512, tn=512)
    saved = dict(x=x, h=h, p=p, cq_n=cq_n, ckv_n=ckv_n, kv=kv, qa=qa, ka=ka, oa=oa, lse_a=lse_a, qb=qb, kb=kb, ob=ob,
                 lse_b=lse_b, qc=qc, kc=kc, oc=oc, lse_c=lse_c, y=y, x1=x1, h2=h2, gate=gate, up=up, act=act)
    return x2, saved, wl


def _layer_bwd(dx2, sv, wl, sm, tabs, bias, t, send_ffn, send_attn):
    s, d = dx2.shape
    (cos_a, sin_a), (cos_b, sin_b), (cos_c, sin_c) = tabs
    gw, gs = {}, {}
    dact = _matmul(dx2, wl["w_down"], mode="nt", out_dtype=F32, name="ffn_down_dx", tm=512, tn=512)
    gw["w_down"] = _matmul(sv["act"], dx2, mode="tn", out_dtype=BF16, name="ffn_down_dw", tm=512, tn=512)
    dgate, dup = _swiglu_bwd(sv["gate"], sv["up"], dact, name="swiglu_bwd")
    dh2 = _matmul(dgate, wl["w_gate"], mode="nt", out_dtype=F32, name="ffn_gate_dx", tm=512, tn=512)
    dh2 = _matmul(dup, wl["w_up"], mode="nt", out_dtype=F32, name="ffn_up_dx", add=dh2, tm=512, tn=512)
    gw["w_gate"] = _matmul(sv["h2"], dgate, mode="tn", out_dtype=BF16, name="ffn_gate_dw", tm=512, tn=512)
    gw["w_up"] = _matmul(sv["h2"], dup, mode="tn", out_dtype=BF16, name="ffn_up_dw", tm=512, tn=512)
    token = send_ffn(gw)
    dx1, gs["ffn_norm"] = _norm_bwd(sv["x1"], sm["ffn_norm"], dh2, wb=d, cb=0, nb=1, shared_gain=True, out_dtype=F32,
                                    name="ffn_norm_bwd", add=dx2)
    dy = _matmul(dx1, wl["w_out"], mode="nt", out_dtype=F32, name="out_proj_dx", tm=512, tn=512)
    gw["w_out"] = _matmul(sv["y"], dx1, mode="tn", out_dtype=BF16, name="out_proj_dw", tm=512, tn=512)
    g_out = _tie(sm["out_norm"], token)
    ga, gb, gc = g_out[:, :A_WIDTH], g_out[:, A_WIDTH:A_WIDTH + B_WIDTH], g_out[:, A_WIDTH + B_WIDTH:]
    dya, dyb, dyc = dy[:, :A_WIDTH], dy[:, A_WIDTH:A_WIDTH + B_WIDTH], dy[:, A_WIDTH + B_WIDTH:]
    doa, dga = _norm_bwd(sv["oa"], ga, dya, wb=A_WIDTH, cb=0, nb=1, shared_gain=True, out_dtype=F32, name="out_norm_a_bwd")
    dob, dgb = _norm_bwd(sv["ob"], gb, dyb, wb=B_WIDTH, cb=0, nb=1, shared_gain=True, out_dtype=F32, name="out_norm_b_bwd")
    doc, dgc = _norm_bwd(sv["oc"], gc, dyc, wb=C_WIDTH, cb=0, nb=1, shared_gain=True, out_dtype=F32, name="out_norm_c_bwd")
    gs["out_norm"] = jnp.concatenate([dga, dgb, dgc], axis=1)
    p = sv["p"]
    dqc, dkc_q, dvc_q = _flash_bwd(sv["qc"], sv["kc"], p, sv["oc"], doc, sv["lse_c"], hkv=C_KV_HEADS, g=C_GROUP,
                                   dqk=LANE, q_cb=0, k_cb=0, v_cb=PB_CVH, v_step=1, scale=HEAD_DIM ** -0.5, t=t,
                                   bias=None, dv_dtype=F32, name="c_flash_bwd")
    dkc = _group_sum(dkc_q, n_out=C_KV_HEADS, g=C_GROUP, src=lambda n, j: C_GROUP * n + j, out_dtype=F32, name="c_dk_group_sum")
    dvc = _group_sum(dvc_q, n_out=C_KV_HEADS, g=C_GROUP, src=lambda n, j: C_GROUP * n + j, out_dtype=BF16, name="c_dv_group_sum")
    dqn = _rope(dqc, cos_c, sin_c, tw=LANE, cb=0, nb=C_HEADS, half=HEAD_DIM // 4, sign=-1, out_dtype=F32, name="c_rope_q_bwd")
    dkn = _rope(dkc, cos_c, sin_c, tw=LANE, cb=0, nb=C_KV_HEADS, half=HEAD_DIM // 4, sign=-1, out_dtype=F32, name="c_rope_k_bwd")
    dpcq, gs["c_q_norm"] = _norm_bwd(p, sm["c_q_norm"], dqn, wb=LANE, cb=PB_CQH, nb=C_HEADS, shared_gain=True,
                                     out_dtype=BF16, name="c_q_norm_bwd")
    dpck, gs["c_k_norm"] = _norm_bwd(p, sm["c_k_norm"], dkn, wb=LANE, cb=PB_CKH, nb=C_KV_HEADS, shared_gain=True,
                                     out_dtype=BF16, name="c_k_norm_bwd")
    dqb, dkb, dvb = _flash_bwd(sv["qb"], sv["kb"], p, sv["ob"], dob, sv["lse_b"], hkv=B_HEADS, g=1, dqk=LANE, q_cb=0,
                               k_cb=0, v_cb=PB_BV, v_step=1, scale=HEAD_DIM ** -0.5, t=t, bias=bias, dv_dtype=BF16,
                               name="b_flash_bwd")
    dpbq = _rope(dqb, cos_b, sin_b, tw=LANE, cb=0, nb=B_HEADS, half=HEAD_DIM // 2, sign=-1, out_dtype=BF16, name="b_rope_q_bwd")
    dpbk = _rope(dkb, cos_b, sin_b, tw=LANE, cb=0, nb=B_HEADS, half=HEAD_DIM // 2, sign=-1, out_dtype=BF16, name="b_rope_k_bwd")
    dqa, dka, dva = _flash_bwd(sv["qa"], sv["ka"], sv["kv"], sv["oa"], doa, sv["lse_a"], hkv=A_HEADS, g=1, dqk=A_QK,
                               q_cb=0, k_cb=0, v_cb=1, v_step=2, scale=(A_NOPE + A_ROPE) ** -0.5, t=t, bias=None,
                               dv_dtype=BF16, name="a_flash_bwd")
    dqa_raw = _rope(dqa, cos_a, sin_a, tw=A_QK, cb=0, nb=A_HEADS, half=A_ROPE // 2, sign=-1, out_dtype=BF16, name="a_rope_q_bwd")
    dka_raw = _rope(dka, cos_a, sin_a, tw=A_QK, cb=0, nb=A_HEADS, half=A_ROPE // 2, sign=-1, out_dtype=BF16, name="a_rope_k_bwd")
    dkr = _group_sum(dka_raw, n_out=1, g=A_HEADS, src=lambda n, j: 2 * j + 1, out_dtype=BF16, name="a_k_rope_sum")
    dkv = jnp.stack([dka_raw.reshape(s, A_HEADS, 2, LANE)[:, :, 0], dva.reshape(s, A_HEADS, LANE)], axis=2)
    dkv = dkv.reshape(s, A_HEADS * 2 * LANE)
    dckv_n = _matmul(dkv, wl["ukv"], mode="nt", out_dtype=F32, name="a_ukv_dx", tm=1024, tn=512)
    gw["ukv"] = _matmul(sv["ckv_n"], dkv, mode="tn", out_dtype=BF16, name="a_ukv_dw", tm=512, tn=1024)
    dcq_n = _matmul(dqa_raw, wl["uq"], mode="nt", out_dtype=F32, name="a_uq_dx", tm=1024, tn=512)
    gw["uq"] = _matmul(sv["cq_n"], dqa_raw, mode="tn", out_dtype=BF16, name="a_uq_dw", tm=512, tn=1024)
    dcq, gs["a_q_norm"] = _norm_bwd(p, sm["a_q_norm"], dcq_n, wb=A_Q_RANK, cb=0, nb=1, shared_gain=True, out_dtype=BF16,
                                    name="a_q_norm_bwd")
    dckv, gs["a_kv_norm"] = _norm_bwd(p, sm["a_kv_norm"], dckv_n, wb=A_KV_RANK, cb=1, nb=1, shared_gain=True,
                                      out_dtype=BF16, name="a_kv_norm_bwd")
    dp = jnp.concatenate([dcq, dckv, dkr, jnp.zeros((s, A_PAD - (PB_KR + 1) * LANE), BF16), dpbq, dpbk, dvb, dpcq, dpck,
                          dvc], axis=1)
    dh = _matmul(dp, wl["w_all"], mode="nt", out_dtype=F32, name="in_proj_dx", tm=512, tn=512)
    gw["w_all"] = _matmul(sv["h"], dp, mode="tn", out_dtype=BF16, name="in_proj_dw", tm=512, tn=640)
    token = send_attn(gw)
    dx, gs["attn_norm"] = _norm_bwd(sv["x"], _tie(sm["attn_norm"], token), dh, wb=d, cb=0, nb=1, shared_gain=True,
                                    out_dtype=F32, name="attn_norm_bwd", add=dx1)
    return dx, gs, token


def _pack_small(vals):
    flat = jnp.concatenate([vals[n].reshape(-1).astype(F32) for n in _SMALL])
    tile = SUBLANE * LANE
    padded = -(-flat.shape[0] // tile) * tile
    return jnp.pad(flat, (0, padded - flat.shape[0])).reshape(padded // LANE, LANE)


def _unpack_small(packed, like):
    flat = packed.reshape(-1)
    out, off = {}, 0
    for n in _SMALL:
        size = math.prod(like[n].shape)
        out[n] = flat[off:off + size].reshape(like[n].shape)
        off += size
    return out


def kernel(x, attn_norm, w_in, a_q_norm, a_w_uq, a_kv_norm, a_w_ukv, c_q_norm, c_k_norm, out_norm, w_out, ffn_norm, w_gate, w_up, w_down, final_norm, loss_target, m_attn_norm, m_w_in, m_a_q_norm, m_a_w_uq, m_a_kv_norm, m_a_w_ukv, m_c_q_norm, m_c_k_norm, m_out_norm, m_w_out, m_ffn_norm, m_w_gate, m_w_up, m_w_down, m_final_norm, v_attn_norm, v_w_in, v_a_q_norm, v_a_w_uq, v_a_kv_norm, v_a_w_ukv, v_c_q_norm, v_c_k_norm, v_out_norm, v_w_out, v_ffn_norm, v_w_gate, v_w_up, v_w_down, v_final_norm):
    w = dict(attn_norm=attn_norm, w_in=w_in, a_q_norm=a_q_norm, a_w_uq=a_w_uq, a_kv_norm=a_kv_norm, a_w_ukv=a_w_ukv,
             c_q_norm=c_q_norm, c_k_norm=c_k_norm, out_norm=out_norm, w_out=w_out, ffn_norm=ffn_norm, w_gate=w_gate,
             w_up=w_up, w_down=w_down, final_norm=final_norm)
    m = dict(attn_norm=m_attn_norm, w_in=m_w_in, a_q_norm=m_a_q_norm, a_w_uq=m_a_w_uq, a_kv_norm=m_a_kv_norm,
             a_w_ukv=m_a_w_ukv, c_q_norm=m_c_q_norm, c_k_norm=m_c_k_norm, out_norm=m_out_norm, w_out=m_w_out,
             ffn_norm=m_ffn_norm, w_gate=m_w_gate, w_up=m_w_up, w_down=m_w_down, final_norm=m_final_norm)
    v = dict(attn_norm=v_attn_norm, w_in=v_w_in, a_q_norm=v_a_q_norm, a_w_uq=v_a_w_uq, a_kv_norm=v_a_kv_norm,
             a_w_ukv=v_a_w_ukv, c_q_norm=v_c_q_norm, c_k_norm=v_c_k_norm, out_norm=v_out_norm, w_out=v_w_out,
             ffn_norm=v_ffn_norm, w_gate=v_w_gate, w_up=v_w_up, w_down=v_w_down, final_norm=v_final_norm)
    _, s, d = x.shape
    depth = attn_norm.shape[0]
    t = _pick(s, 512)

    me = (2 * lax.axis_index("x") + lax.axis_index("y")).astype(jnp.int32).reshape(1)

    gathers, after = {}, me
    for l in range(depth):
        for group, names in (("attn", _ATTN), ("ffn", _FFN)):
            bufs = [_cast_to_slot(w[n].reshape(-1, w[n].shape[-1]), me, layer=l, rows=w[n].shape[1], name=f"cast_{n}")
                    for n in names]
            send_sems, recv_sems, bufs, _, after = _exchange_start(bufs, None, after, name=f"gather_start_{group}{l}")
            gathers[group, l] = (send_sems, recv_sems, bufs)
    all_started = after

    def gathered(group, l, after):
        send_sems, recv_sems, bufs = gathers[group, l]
        return _exchange_wait(send_sems, recv_sems, bufs, None, after, name=f"gather_wait_{group}{l}")

    tabs = _rope_tables(s)
    bias = _band_bias(t, s // t)

    xs = x.reshape(s, d)
    saved, wls, sms = [], [], []
    for l in range(depth):
        wl = _assemble_attn(gathered("attn", l, all_started if l == 0 else xs))
        sm = {n: w[n][l][None, :] for n in _SMALL if n != "final_norm"}
        xs, sv, wl = _layer_fwd(xs, wl, lambda after, l=l: _assemble_ffn(gathered("ffn", l, after)), sm, tabs, bias, t)
        saved.append(sv)
        wls.append(wl)
        sms.append(sm)
    dx, g_final, loss_row = _final_loss(xs, final_norm[None, :], loss_target.reshape(s, d), name="final_loss")
    loss = lax.psum(loss_row[0, 0], ("x", "y", "c"))

    sends = {}

    def send(group, l, srcs, after):
        lands = [lax.empty((3,) + a.shape[1:], BF16) for a in srcs]
        send_sems, recv_sems, srcs, lands, token = _exchange_start(srcs, lands, after, name=f"scatter_start_{group}{l}")
        sends[group, l] = (send_sems, recv_sems, srcs, lands)
        return token

    gs_layers, token = [None] * depth, all_started
    for l in reversed(range(depth)):
        dx, gs_layers[l], token = _layer_bwd(
            dx, saved[l], wls[l], sms[l], tabs, bias, t,
            lambda gw, l=l, tk=token: send("ffn", l, _split_ffn_grads(gw), tk),
            lambda gw, l=l: send("attn", l, _split_attn_grads(gw), dx))
    grad_x = dx.reshape(x.shape)

    srcs, lands = {}, {}
    for (group, l), (send_sems, recv_sems, s_bufs, l_bufs) in sends.items():
        got = _exchange_wait(send_sems, recv_sems, s_bufs, l_bufs, token, name=f"scatter_wait_{group}{l}")
        for k, n in enumerate(_ATTN if group == "attn" else _FFN):
            srcs[n, l], lands[n, l] = got[k], got[len(s_bufs) + k]
    sums = [_sum_parts([srcs[n, l] for l in range(depth)], [lands[n, l] for l in range(depth)], me, name="sum_" + n)
            for n in _BIG]
    sib = _sibling_exchange(sums, name="swap_core_sums")
    grads, deltas, new_m, new_v = {}, {}, {}, {}
    for n, mine, other in zip(_BIG, sums, sib):
        shp = w[n].shape
        two_d = (-1, shp[-1])
        res = _adamw(mine, other, w[n].reshape(two_d), m[n].reshape(two_d), v[n].reshape(two_d), name="adamw_" + n)
        grads[n], deltas[n], new_m[n], new_v[n] = [r.reshape(shp) for r in res]

    gsm = {n: jnp.stack([gs_layers[l][n][0] for l in range(depth)]) for n in _SMALL if n != "final_norm"}
    gsm["final_norm"] = g_final[0]
    packed = _pack_small(gsm)
    everyone = _all_gather_small(packed, name="gather_gain_grads").reshape(N_DEV, packed.shape[0], LANE)
    res = _small_adamw(everyone, _pack_small(w), _pack_small(m), _pack_small(v), name="adamw_gains")
    for dst, r in zip((grads, deltas, new_m, new_v), res):
        dst.update(_unpack_small(r, w))

    return (loss, grad_x, *[grads[n] for n in _WEIGHTS], *[deltas[n] for n in _WEIGHTS],
            *[new_m[n] for n in _WEIGHTS], *[new_v[n] for n in _WEIGHTS])
```

```python
import functools
import math

import jax
import jax.numpy as jnp
import numpy as np
from jax import lax
from jax.experimental import pallas as pl
from jax.experimental.pallas import tpu as pltpu

F32 = jnp.float32
BF16 = jnp.bfloat16
MESH = pl.DeviceIdType.MESH

HEAD_DIM = 128
ROPE_THETA = 10000.0
GRID_W = 64
EPS = 1e-6
NEG = -1e30
A_HEADS, A_Q_RANK, A_KV_RANK, A_NOPE, A_ROPE, A_V = 4, 512, 512, 128, 64, 128
B_HEADS = 6
B_PATTERNS = ((128, 1), (512, 4), (2048, 16))
C_HEADS, C_KV_HEADS = 6, 2
C_GROUP = C_HEADS // C_KV_HEADS
A_WIDTH, B_WIDTH, C_WIDTH = A_HEADS * A_V, B_HEADS * HEAD_DIM, C_HEADS * HEAD_DIM
IN_A = A_Q_RANK + A_KV_RANK + A_ROPE
IN_B = 3 * B_WIDTH
IN_C = C_WIDTH + 2 * C_KV_HEADS * HEAD_DIM
ADAM_LR, ADAM_B1, ADAM_B2, ADAM_EPS, ADAM_WD, ADAM_STEP = 0.001, 0.9, 0.999, 1e-08, 0.01, 10

LANE = 128
SUBLANE = 8
VMEM_BYTES_V7X = 64 * 1024 * 1024
VMEM_LIMIT_CAP = VMEM_BYTES_V7X - 8 * 1024 * 1024
N_CHIPS = 4
N_DEV = 8

A_PAD = 12 * LANE
PB_CQ, PB_CKV, PB_KR = 0, 4, 8
PB_BQ, PB_BK, PB_BV = 12, 18, 24
PB_CQH, PB_CKH, PB_CVH = 30, 36, 38
NP = 40 * LANE
A_QK = 2 * LANE


def _pick(n, cap, mult=LANE):
    if n <= cap:
        return n
    t = cap - cap % mult
    while t >= mult:
        if n % t == 0:
            return t
        t -= mult
    return n


def _rows_for(width_bytes, n_rows, target=2 * 1024 * 1024):
    return _pick(n_rows, max(SUBLANE, target // max(width_bytes, 1)), SUBLANE)


def _params(est_bytes):
    limit = int(min(max(est_bytes + (4 << 20), 32 << 20), VMEM_LIMIT_CAP))
    return pltpu.CompilerParams(vmem_limit_bytes=limit)


def _isz(x):
    return jnp.dtype(x.dtype).itemsize


_DIMS = {"nn": (((1,), (0,)), ((), ())), "nt": (((1,), (1,)), ((), ())), "tn": (((0,), (0,)), ((), ()))}


def _matmul(a, b, *, mode, out_dtype, name, add=None, tm=512, tn=512):
    if mode == "tn":
        (k, m), (k2, n) = a.shape, b.shape
    elif mode == "nt":
        (m, k), (n, k2) = a.shape, b.shape
    else:
        (m, k), (k2, n) = a.shape, b.shape
    assert k == k2, (a.shape, b.shape, mode)
    tm, tn = _pick(m, tm), _pick(n, tn)
    a_spec = pl.BlockSpec((k, tm), lambda i, j: (0, i)) if mode == "tn" else pl.BlockSpec((tm, k), lambda i, j: (i, 0))
    b_spec = pl.BlockSpec((tn, k), lambda i, j: (j, 0)) if mode == "nt" else pl.BlockSpec((k, tn), lambda i, j: (0, j))
    o_spec = pl.BlockSpec((tm, tn), lambda i, j: (i, j))
    dims = _DIMS[mode]

    def body(*refs):
        if add is None:
            a_ref, b_ref, o_ref = refs
        else:
            a_ref, b_ref, add_ref, o_ref = refs
        acc = lax.dot_general(a_ref[...].astype(BF16), b_ref[...].astype(BF16), dims, preferred_element_type=F32)
        if add is not None:
            acc = acc + add_ref[...].astype(F32)
        o_ref[...] = acc.astype(out_dtype)

    ins, specs = [a, b], [a_spec, b_spec]
    if add is not None:
        ins.append(add)
        specs.append(o_spec)
    est = 2 * (tm * k * _isz(a) + tn * k * _isz(b) + tm * tn * (jnp.dtype(out_dtype).itemsize + (4 if add is not None else 0)))
    est += (tm + tn) * k * 2 + 2 * tm * tn * 4
    return pl.pallas_call(
        body, name=name, grid=(m // tm, n // tn), in_specs=specs, out_specs=o_spec,
        out_shape=jax.ShapeDtypeStruct((m, n), out_dtype), compiler_params=_params(est),
    )(*ins)


def _norm_fwd(x, gain, *, wb, cb, nb, shared_gain, out_dtype, name):
    s = x.shape[0]
    ts = _rows_for(wb * 4, s)

    def body(x_ref, g_ref, o_ref):
        xv = x_ref[...].astype(F32)
        r = lax.rsqrt(jnp.mean(xv * xv, axis=1, keepdims=True) + EPS)
        o_ref[...] = ((xv * r) * g_ref[...]).astype(out_dtype)

    return pl.pallas_call(
        body, name=name, grid=(nb, s // ts),
        in_specs=[pl.BlockSpec((ts, wb), lambda n, i: (i, cb + n)),
                  pl.BlockSpec((1, wb), (lambda n, i: (0, 0)) if shared_gain else (lambda n, i: (0, n)))],
        out_specs=pl.BlockSpec((ts, wb), lambda n, i: (i, n)),
        out_shape=jax.ShapeDtypeStruct((s, nb * wb), out_dtype), compiler_params=_params(6 * ts * wb * 4),
    )(x, gain)


def _norm_bwd(x, gain, dy, *, wb, cb, nb, shared_gain, out_dtype, name, dy_cb=0, add=None):
    s = x.shape[0]
    ts = _rows_for(wb * 4, s, target=1024 * 1024)
    gw = wb if shared_gain else nb * wb

    def body(*refs):
        if add is None:
            x_ref, g_ref, dy_ref, dx_ref, dg_ref = refs
        else:
            x_ref, g_ref, dy_ref, add_ref, dx_ref, dg_ref = refs
        n, i = pl.program_id(0), pl.program_id(1)
        xv = x_ref[...].astype(F32)
        dyv = dy_ref[...].astype(F32)
        r = lax.rsqrt(jnp.mean(xv * xv, axis=1, keepdims=True) + EPS)
        xh = xv * r
        dyg = dyv * g_ref[...]
        dx = r * (dyg - xh * jnp.mean(dyg * xh, axis=1, keepdims=True))
        if add is not None:
            dx = dx + add_ref[...]
        dx_ref[...] = dx.astype(out_dtype)
        first = jnp.logical_and(n == 0, i == 0) if shared_gain else (i == 0)

        @pl.when(first)
        def _():
            dg_ref[...] = jnp.zeros_like(dg_ref)

        dg_ref[...] += jnp.sum(dyv * xh, axis=0, keepdims=True)

    ins = [x, gain, dy]
    specs = [pl.BlockSpec((ts, wb), lambda n, i: (i, cb + n)),
             pl.BlockSpec((1, wb), (lambda n, i: (0, 0)) if shared_gain else (lambda n, i: (0, n))),
             pl.BlockSpec((ts, wb), lambda n, i: (i, dy_cb + n))]
    if add is not None:
        ins.append(add)
        specs.append(pl.BlockSpec((ts, wb), lambda n, i: (i, n)))
    return pl.pallas_call(
        body, name=name, grid=(nb, s // ts), in_specs=specs,
        out_specs=[pl.BlockSpec((ts, wb), lambda n, i: (i, n)),
                   pl.BlockSpec((1, wb), (lambda n, i: (0, 0)) if shared_gain else (lambda n, i: (0, n)))],
        out_shape=[jax.ShapeDtypeStruct((s, nb * wb), out_dtype), jax.ShapeDtypeStruct((1, gw), F32)],
        compiler_params=_params(12 * ts * wb * 4),
    )(*ins)


def _swap_halves(x, half):
    if 2 * half == LANE:
        return pltpu.roll(x, half, axis=1)
    lane = lax.broadcasted_iota(jnp.int32, x.shape, 1)
    first = jnp.bitwise_and(lane, 2 * half - 1) < half
    return jnp.where(first, pltpu.roll(x, LANE - half, axis=1), pltpu.roll(x, half, axis=1))


def _rope(x, cos_t, sin_t, *, tw, cb, nb, half, sign, out_dtype, name):
    s = x.shape[0]
    ts = _rows_for(tw * 4, s)

    def body(x_ref, c_ref, s_ref, o_ref):
        for q in range(tw // LANE):
            sl = slice(q * LANE, (q + 1) * LANE)
            xv = x_ref[:, sl].astype(F32)
            sv = s_ref[:, sl]
            if sign < 0:
                sv = -sv
            o_ref[:, sl] = (xv * c_ref[:, sl] + _swap_halves(xv, half) * sv).astype(out_dtype)

    return pl.pallas_call(
        body, name=name, grid=(nb, s // ts),
        in_specs=[pl.BlockSpec((ts, tw), lambda n, i: (i, cb + n)),
                  pl.BlockSpec((ts, tw), lambda n, i: (i, 0)),
                  pl.BlockSpec((ts, tw), lambda n, i: (i, 0))],
        out_specs=pl.BlockSpec((ts, tw), lambda n, i: (i, n)),
        out_shape=jax.ShapeDtypeStruct((s, nb * tw), out_dtype), compiler_params=_params(10 * ts * tw * 4),
    )(x, cos_t, sin_t)


def _cast_cols(x, *, cb, nb, name):
    s = x.shape[0]
    ts = _rows_for(LANE * 4, s)

    def body(x_ref, o_ref):
        o_ref[...] = x_ref[...].astype(BF16)

    return pl.pallas_call(
        body, name=name, grid=(nb, s // ts), in_specs=[pl.BlockSpec((ts, LANE), lambda n, i: (i, cb + n))],
        out_specs=pl.BlockSpec((ts, LANE), lambda n, i: (i, n)),
        out_shape=jax.ShapeDtypeStruct((s, nb * LANE), BF16), compiler_params=_params(4 * ts * LANE * 4),
    )(x)


def _group_sum(x, *, n_out, g, src, out_dtype, name):
    s = x.shape[0]
    ts = _rows_for(LANE * 4, s)

    def body(*refs):
        acc = refs[0][...].astype(F32)
        for r in refs[1:-1]:
            acc = acc + r[...].astype(F32)
        refs[-1][...] = acc.astype(out_dtype)

    return pl.pallas_call(
        body, name=name, grid=(n_out, s // ts),
        in_specs=[pl.BlockSpec((ts, LANE), functools.partial(lambda n, i, j: (i, src(n, j)), j=j)) for j in range(g)],
        out_specs=pl.BlockSpec((ts, LANE), lambda n, i: (i, n)),
        out_shape=jax.ShapeDtypeStruct((s, n_out * LANE), out_dtype), compiler_params=_params(4 * g * ts * LANE * 4),
    )(*([x] * g))


LOG2E = 1.4426950408889634
ATTN_ROW_CHUNK = 256


def _attn_window(i, tq, s, band):
    w, r = band
    start = jnp.clip(i * tq - r, 0, s - w)
    return pl.multiple_of(start, tq), pl.multiple_of((w - tq) - (i * tq - start), LANE)


def _flash_fwd(q, k, v, table, *, hkv, g, dqk, q_cb, k_cb, v_cb, v_step, scale, tq, band, name):
    s = q.shape[0]
    n = s // tq
    hq = hkv * g
    rc = min(tq, ATTN_ROW_CHUNK)
    w = s if band is None else band[0]

    def body(*refs):
        if band is None:
            q_ref, k_ref, v_ref, o_ref, lse_ref = refs
            kw, vw = k_ref[...], v_ref[...]
        else:
            q_ref, k_ref, v_ref, t_ref, o_ref, lse_ref = refs
            start, u = _attn_window(pl.program_id(1), tq, s, band)
            kw, vw = k_ref[pl.ds(start, w), :], v_ref[pl.ds(start, w), :]
        for c in range(tq // rc):
            rows = slice(c * rc, (c + 1) * rc)
            sc = lax.dot_general(q_ref[rows, :], kw, _DIMS["nt"], preferred_element_type=F32) * (scale * LOG2E)
            if band is not None:
                sc = sc + t_ref[rows, pl.ds(u, w)]
            m = jnp.max(sc, axis=1, keepdims=True)
            p = jnp.exp2(sc - m)
            l = jnp.sum(p, axis=1, keepdims=True)
            o_ref[rows, :] = jnp.dot(p.astype(BF16), vw, preferred_element_type=F32) / l
            lse_ref[0, rows, :] = jnp.broadcast_to(m + jnp.log2(l), (rc, LANE))

    ins = [q, k, v]
    specs = [pl.BlockSpec((tq, dqk), lambda h, i: (i, q_cb + h)),
             pl.BlockSpec((s, dqk), lambda h, i: (0, k_cb + h // g)),
             pl.BlockSpec((s, LANE), lambda h, i: (0, v_cb + v_step * (h // g)))]
    if band is not None:
        ins.append(table)
        specs.append(pl.BlockSpec(table.shape, lambda h, i: (0, 0)))
    est = 4 * s * (dqk + LANE) + 6 * rc * w * 4 + 8 * tq * LANE * 4 + (0 if band is None else 2 * table.size * 4)
    return pl.pallas_call(
        body, name=name, grid=(hq, n), in_specs=specs,
        out_specs=[pl.BlockSpec((tq, LANE), lambda h, i: (i, h)), pl.BlockSpec((1, tq, LANE), lambda h, i: (h, i, 0))],
        out_shape=[jax.ShapeDtypeStruct((s, hq * LANE), F32), jax.ShapeDtypeStruct((hq, s, LANE), F32)],
        compiler_params=_params(est),
    )(*ins)


def _flash_bwd(q, k, v, o, do, lse, table, *, hkv, g, dqk, q_cb, k_cb, v_cb, v_step, scale, tq, band, name):
    s = q.shape[0]
    n = s // tq
    hq = hkv * g
    rc = min(tq, ATTN_ROW_CHUNK)
    w = s if band is None else band[0]

    def body(*refs):
        if band is None:
            q_ref, k_ref, v_ref, o_ref, do_ref, lse_ref, dq_ref, dk_ref, dv_ref = refs
            keys = slice(None)
        else:
            q_ref, k_ref, v_ref, o_ref, do_ref, lse_ref, t_ref, dq_ref, dk_ref, dv_ref = refs
            start, u = _attn_window(pl.program_id(1), tq, s, band)
            keys = pl.ds(start, w)
        h, i = pl.program_id(0), pl.program_id(1)

        @pl.when(jnp.logical_and(h % g == 0, i == 0))
        def _():
            dk_ref[...] = jnp.zeros_like(dk_ref)
            dv_ref[...] = jnp.zeros_like(dv_ref)

        kw, vw = k_ref[keys, :], v_ref[keys, :]
        for c in range(tq // rc):
            rows = slice(c * rc, (c + 1) * rc)
            qv = q_ref[rows, :]
            dof = do_ref[rows, :]
            dov = dof.astype(BF16)
            sc = lax.dot_general(qv, kw, _DIMS["nt"], preferred_element_type=F32) * (scale * LOG2E)
            if band is not None:
                sc = sc + t_ref[rows, pl.ds(u, w)]
            p = jnp.exp2(sc - lse_ref[0, rows, 0:1])
            dp = lax.dot_general(dov, vw, _DIMS["nt"], preferred_element_type=F32)
            delta = jnp.sum(dof * o_ref[rows, :], axis=1, keepdims=True)
            ds = (p * (dp - delta) * scale).astype(BF16)
            dv_ref[keys, :] += lax.dot_general(p.astype(BF16), dov, _DIMS["tn"], preferred_element_type=F32)
            dk_ref[keys, :] += lax.dot_general(ds, qv, _DIMS["tn"], preferred_element_type=F32)
            dq_ref[rows, :] = jnp.dot(ds, kw, preferred_element_type=F32)

    ins = [q, k, v, o, do, lse]
    specs = [pl.BlockSpec((tq, dqk), lambda h, i: (i, q_cb + h)),
             pl.BlockSpec((s, dqk), lambda h, i: (0, k_cb + h // g)),
             pl.BlockSpec((s, LANE), lambda h, i: (0, v_cb + v_step * (h // g))),
             pl.BlockSpec((tq, LANE), lambda h, i: (i, h)),
             pl.BlockSpec((tq, LANE), lambda h, i: (i, h)),
             pl.BlockSpec((1, tq, LANE), lambda h, i: (h, i, 0))]
    if band is not None:
        ins.append(table)
        specs.append(pl.BlockSpec(table.shape, lambda h, i: (0, 0)))
    est = (4 + 8) * s * (dqk + LANE) + 10 * rc * w * 4 + 12 * tq * LANE * 4 + (0 if band is None else 2 * table.size * 4)
    return pl.pallas_call(
        body, name=name, grid=(hq, n), in_specs=specs,
        out_specs=[pl.BlockSpec((tq, dqk), lambda h, i: (i, h)),
                   pl.BlockSpec((s, dqk), lambda h, i: (0, h // g)),
                   pl.BlockSpec((s, LANE), lambda h, i: (0, h // g))],
        out_shape=[jax.ShapeDtypeStruct((s, hq * dqk), F32), jax.ShapeDtypeStruct((s, hkv * dqk), F32),
                   jax.ShapeDtypeStruct((s, hkv * LANE), F32)],
        compiler_params=_params(est),
    )(*ins)


def _swiglu_fwd(gate, up, *, name):
    s, f = gate.shape
    ts, tf = _pick(s, 512, SUBLANE), _pick(f, 1024)

    def body(g_ref, u_ref, o_ref):
        gv = g_ref[...]
        o_ref[...] = (gv / (1.0 + jnp.exp(-gv)) * u_ref[...]).astype(BF16)

    spec = pl.BlockSpec((ts, tf), lambda i, j: (i, j))
    return pl.pallas_call(
        body, name=name, grid=(s // ts, f // tf), in_specs=[spec, spec], out_specs=spec,
        out_shape=jax.ShapeDtypeStruct((s, f), BF16), compiler_params=_params(8 * ts * tf * 4),
    )(gate, up)


def _swiglu_bwd(gate, up, dact, *, name):
    s, f = gate.shape
    ts, tf = _pick(s, 512, SUBLANE), _pick(f, 1024)

    def body(g_ref, u_ref, d_ref, dg_ref, du_ref):
        gv, uv, dv = g_ref[...], u_ref[...], d_ref[...]
        sig = 1.0 / (1.0 + jnp.exp(-gv))
        dg_ref[...] = (dv * uv * (sig * (1.0 + gv * (1.0 - sig)))).astype(BF16)
        du_ref[...] = (dv * (gv * sig)).astype(BF16)

    spec = pl.BlockSpec((ts, tf), lambda i, j: (i, j))
    return pl.pallas_call(
        body, name=name, grid=(s // ts, f // tf), in_specs=[spec, spec, spec], out_specs=[spec, spec],
        out_shape=[jax.ShapeDtypeStruct((s, f), BF16)] * 2, compiler_params=_params(12 * ts * tf * 4),
    )(gate, up, dact)


def _final_loss(x, gain, target, *, name):
    s, d = x.shape
    ts = _rows_for(d * 4, s, target=1024 * 1024)

    def body(x_ref, g_ref, t_ref, dx_ref, dg_ref, loss_ref):
        i = pl.program_id(0)
        xv = x_ref[...]
        gv = g_ref[...]
        r = lax.rsqrt(jnp.mean(xv * xv, axis=1, keepdims=True) + EPS)
        xh = xv * r
        err = xh * gv - t_ref[...]
        dy = err / d
        dyg = dy * gv
        dx_ref[...] = r * (dyg - xh * jnp.mean(dyg * xh, axis=1, keepdims=True))

        @pl.when(i == 0)
        def _():
            dg_ref[...] = jnp.zeros_like(dg_ref)
            loss_ref[...] = jnp.zeros_like(loss_ref)

        dg_ref[...] += jnp.sum(dy * xh, axis=0, keepdims=True)
        part = jnp.sum(jnp.mean(err * err, axis=1, keepdims=True), axis=0, keepdims=True)
        loss_ref[...] += jnp.broadcast_to(0.5 * part, (1, LANE))

    row = pl.BlockSpec((ts, d), lambda i: (i, 0))
    return pl.pallas_call(
        body, name=name, grid=(s // ts,),
        in_specs=[row, pl.BlockSpec((1, d), lambda i: (0, 0)), row],
        out_specs=[row, pl.BlockSpec((1, d), lambda i: (0, 0)), pl.BlockSpec((1, LANE), lambda i: (0, 0))],
        out_shape=[jax.ShapeDtypeStruct((s, d), F32), jax.ShapeDtypeStruct((1, d), F32),
                   jax.ShapeDtypeStruct((1, LANE), F32)],
        compiler_params=_params(12 * ts * d * 4),
    )(x, gain, target)


def _cast_to_slot(x2d, me, *, layer, rows, name):
    c = x2d.shape[1]
    tr = _rows_for(c * 4, rows)
    nt = rows // tr

    def body(me_ref, x_ref, o_ref):
        o_ref[...] = x_ref[...].astype(BF16)

    return pl.pallas_call(
        body, name=name,
        grid_spec=pltpu.PrefetchScalarGridSpec(
            num_scalar_prefetch=1, grid=(nt,),
            in_specs=[pl.BlockSpec((tr, c), lambda i, me_ref: (layer * nt + i, 0))],
            out_specs=pl.BlockSpec((None, tr, c), lambda i, me_ref: (me_ref[0], i, 0))),
        out_shape=jax.ShapeDtypeStruct((N_CHIPS, rows, c), BF16), compiler_params=_params(6 * tr * c * 4),
    )(me, x2d)


def _sum_parts(srcs, lands, me, *, name):
    depth = len(srcs)
    _, r, c = srcs[0].shape
    tr = _rows_for(c * 4, r, target=1024 * 1024)
    nt = r // tr

    def body(me_ref, *refs):
        o_ref = refs[-1]
        l = pl.program_id(0)
        for k in range(depth):
            @pl.when(l == k)
            def _(k=k):
                acc = refs[k][...].astype(F32)
                for p in range(3):
                    acc = acc + refs[depth + k][p].astype(F32)
                o_ref[...] = acc

    def rows_of(k):
        return lambda l, i, me_ref: jnp.where(l == k, i, jnp.where(l < k, 0, nt - 1))

    in_specs = [pl.BlockSpec((None, tr, c), functools.partial(lambda l, i, me_ref, f: (me_ref[0], f(l, i, me_ref), 0), f=rows_of(k)))
                for k in range(depth)]
    in_specs += [pl.BlockSpec((3, tr, c), functools.partial(lambda l, i, me_ref, f: (0, f(l, i, me_ref), 0), f=rows_of(k)))
                 for k in range(depth)]
    return pl.pallas_call(
        body, name=name,
        grid_spec=pltpu.PrefetchScalarGridSpec(
            num_scalar_prefetch=1, grid=(depth, nt), in_specs=in_specs,
            out_specs=pl.BlockSpec((tr, c), lambda l, i, me_ref: (l * nt + i, 0))),
        out_shape=jax.ShapeDtypeStruct((depth * r, c), F32), compiler_params=_params(depth * 10 * tr * c * 4),
    )(me, *srcs, *lands)


def _adamw_math(w, g, m, v):
    m2 = ADAM_B1 * m + (1.0 - ADAM_B1) * g
    v2 = ADAM_B2 * v + (1.0 - ADAM_B2) * (g * g)
    m_hat = m2 / (1.0 - ADAM_B1 ** ADAM_STEP)
    v_hat = v2 / (1.0 - ADAM_B2 ** ADAM_STEP)
    delta = -ADAM_LR * (m_hat / (jnp.sqrt(v_hat) + ADAM_EPS) + ADAM_WD * w)
    return delta, m2, v2


def _adamw(g_a, g_b, w, m, v, *, name):
    r, c = w.shape
    tr = _rows_for(c * 4, r, target=512 * 1024)

    def body(a_ref, b_ref, w_ref, m_ref, v_ref, g_out, d_out, m_out, v_out):
        gv = a_ref[...] + b_ref[...]
        delta, m2, v2 = _adamw_math(w_ref[...], gv, m_ref[...], v_ref[...])
        g_out[...] = gv
        d_out[...] = delta
        m_out[...] = m2
        v_out[...] = v2

    spec = pl.BlockSpec((tr, c), lambda i: (i, 0))
    return pl.pallas_call(
        body, name=name, grid=(r // tr,), in_specs=[spec] * 5, out_specs=[spec] * 4,
        out_shape=[jax.ShapeDtypeStruct((r, c), F32)] * 4, compiler_params=_params(22 * tr * c * 4),
    )(g_a, g_b, w, m, v)


def _small_adamw(g_all, w, m, v, *, name):
    r, c = w.shape

    def body(ga_ref, w_ref, m_ref, v_ref, g_out, d_out, m_out, v_out):
        gv = ga_ref[0]
        for j in range(1, N_DEV):
            gv = gv + ga_ref[j]
        delta, m2, v2 = _adamw_math(w_ref[...], gv, m_ref[...], v_ref[...])
        g_out[...] = gv
        d_out[...] = delta
        m_out[...] = m2
        v_out[...] = v2

    return pl.pallas_call(body, name=name, out_shape=[jax.ShapeDtypeStruct((r, c), F32)] * 4)(g_all, w, m, v)


_ANY = pl.BlockSpec(memory_space=pl.ANY)


_HBM = pl.BlockSpec(memory_space=pltpu.HBM)
_SEM = pl.BlockSpec(memory_space=pltpu.SEMAPHORE)
_EFFECT = pltpu.SideEffectType.DATAFLOW_SIDE_EFFECTING


def _peer_chips():
    x, y = lax.axis_index("x"), lax.axis_index("y")
    return 2 * x + y, [(1 - x, y), (x, 1 - y), (1 - x, 1 - y)]


def _exchange_copy(srcs, lands, send_sems, recv_sems, k, p, scatter):
    me, peers = _peer_chips()
    px, py = peers[p]
    return pltpu.make_async_remote_copy(
        src_ref=srcs[k].at[2 * px + py] if scatter else srcs[k].at[me],
        dst_ref=lands[k].at[p] if scatter else lands[k].at[me],
        send_sem=send_sems.at[3 * k + p], recv_sem=recv_sems.at[3 * k + p],
        device_id=(px, py, lax.axis_index("c")), device_id_type=MESH)


def _exchange_start(srcs, lands, after, *, name):
    scatter = lands is not None
    n = len(srcs)
    bufs = list(srcs) + (list(lands) if scatter else [])
    nb = len(bufs)

    def body(*refs):
        buf_refs, send_sems, recv_sems = refs[:nb], refs[nb + 1], refs[nb + 2]
        token = refs[-1]
        s_refs = buf_refs[:n]
        l_refs = buf_refs[n:] if scatter else s_refs
        for k in range(n):
            for p in range(3):
                _exchange_copy(s_refs, l_refs, send_sems, recv_sems, k, p, scatter).start()
        token[...] = jnp.zeros_like(token)

    out = pl.pallas_call(
        body, name=name,
        out_shape=(pltpu.SemaphoreType.DMA((3 * n,)), pltpu.SemaphoreType.DMA((3 * n,)),
                   *[pltpu.HBM(b.shape, b.dtype) for b in bufs], jax.ShapeDtypeStruct((SUBLANE, LANE), F32)),
        in_specs=[_HBM] * nb + [_ANY],
        out_specs=(_SEM, _SEM, *[_HBM] * nb, pl.BlockSpec(memory_space=pltpu.VMEM)),
        input_output_aliases={i: 2 + i for i in range(nb)},
        compiler_params=pltpu.CompilerParams(has_side_effects=_EFFECT),
    )(*[pltpu.with_memory_space_constraint(b, pltpu.HBM) for b in bufs], after)
    send_sems, recv_sems = out[0], out[1]
    thru = out[2:2 + nb]
    return send_sems, recv_sems, list(thru[:n]), (list(thru[n:]) if scatter else None), out[-1]


def _exchange_wait(send_sems, recv_sems, srcs, lands, after, *, name):
    scatter = lands is not None
    n = len(srcs)
    bufs = list(srcs) + (list(lands) if scatter else [])
    nb = len(bufs)

    def body(*refs):
        buf_refs, send_sems_ref, recv_sems_ref = refs[:nb], refs[nb], refs[nb + 1]
        s_refs = buf_refs[:n]
        l_refs = buf_refs[n:] if scatter else s_refs
        for k in range(n):
            for p in range(3):
                cp = _exchange_copy(s_refs, l_refs, send_sems_ref, recv_sems_ref, k, p, scatter)
                cp.wait_send()
                cp.wait_recv()

    out = pl.pallas_call(
        body, name=name, out_shape=tuple(pltpu.HBM(b.shape, b.dtype) for b in bufs),
        in_specs=[_HBM] * nb + [_SEM, _SEM, _ANY], out_specs=tuple([_HBM] * nb),
        input_output_aliases={i: i for i in range(nb)},
        compiler_params=pltpu.CompilerParams(has_side_effects=_EFFECT),
    )(*bufs, send_sems, recv_sems, after)
    return list(out)


def _sibling_exchange(srcs, *, name):
    n = len(srcs)

    def body(*refs):
        src, out = refs[:n], refs[n:2 * n]
        send_sems, recv_sems = refs[2 * n:]
        sibling = (lax.axis_index("x"), lax.axis_index("y"), 1 - lax.axis_index("c"))
        copies = [pltpu.make_async_remote_copy(src_ref=src[k], dst_ref=out[k], send_sem=send_sems.at[k],
                                               recv_sem=recv_sems.at[k], device_id=sibling, device_id_type=MESH)
                  for k in range(n)]
        for cp in copies:
            cp.start()
        for cp in copies:
            cp.wait_recv()
        for cp in copies:
            cp.wait_send()

    return pl.pallas_call(
        body, name=name, in_specs=[_ANY] * n, out_specs=[_ANY] * n,
        out_shape=[jax.ShapeDtypeStruct(a.shape, a.dtype) for a in srcs],
        scratch_shapes=[pltpu.SemaphoreType.DMA((n,)), pltpu.SemaphoreType.DMA((n,))],
    )(*srcs)


def _all_gather_small(block, *, name):
    m_per, ncol = block.shape

    def body(x_ref, out_ref, send_sems, recv_sems, local_sem):
        x, y, c = lax.axis_index("x"), lax.axis_index("y"), lax.axis_index("c")
        me, sibling = (x, y, c), (x, y, 1 - c)
        chips = [(1 - x, y), (x, 1 - y), (1 - x, 1 - y)]

        def rows(px, py, pc):
            return out_ref.at[pl.ds((4 * px + 2 * py + pc) * m_per, m_per), :]

        def copy(k, blk, to, src=None):
            return pltpu.make_async_remote_copy(
                src_ref=rows(*blk) if src is None else src, dst_ref=rows(*blk),
                send_sem=send_sems.at[k], recv_sem=recv_sems.at[k], device_id=to, device_id_type=MESH)

        mine = pltpu.make_async_copy(x_ref, rows(*me), local_sem)
        mine.start()
        first = [copy(0, me, sibling, src=x_ref)]
        first += [copy(1 + j, me, (*chip, c), src=x_ref) for j, chip in enumerate(chips)]
        for cp in first:
            cp.start()
        passed = [copy(4 + j, (*chip, c), sibling) for j, chip in enumerate(chips)]
        for j, chip in enumerate(chips):
            copy(1 + j, (*chip, c), me).wait_recv()
            passed[j].start()
        copy(0, sibling, me).wait_recv()
        for j, chip in enumerate(chips):
            copy(4 + j, (*chip, 1 - c), me).wait_recv()
        for cp in first + passed:
            cp.wait_send()
        mine.wait()

    return pl.pallas_call(
        body, name=name, out_shape=jax.ShapeDtypeStruct((N_DEV * m_per, ncol), block.dtype),
        in_specs=[pl.BlockSpec(memory_space=pltpu.VMEM)], out_specs=pl.BlockSpec(memory_space=pltpu.VMEM),
        scratch_shapes=[pltpu.SemaphoreType.DMA((7,)), pltpu.SemaphoreType.DMA((7,)), pltpu.SemaphoreType.DMA],
    )(block)


def _rope_angles(pos, dim):
    inv = ROPE_THETA ** (-jnp.arange(0, dim, 2, dtype=F32) / dim)
    return pos.astype(F32)[:, None] * inv[None, :]


def _rope_tables(s):
    pos = jnp.arange(s, dtype=jnp.int32)
    rows = s // GRID_W
    row = jnp.repeat(jnp.arange(rows, dtype=jnp.int32), GRID_W)
    col = jnp.tile(jnp.arange(GRID_W, dtype=jnp.int32), rows)
    a1 = _rope_angles(pos, HEAD_DIM)
    aa = _rope_angles(pos, A_ROPE)
    ar = _rope_angles(row, HEAD_DIM // 2)
    ac = _rope_angles(col, HEAD_DIM // 2)
    one = jnp.ones((s, LANE), F32)
    zero = jnp.zeros((s, LANE), F32)
    pad = LANE - A_ROPE
    cos_a = jnp.concatenate([one, jnp.cos(aa), jnp.cos(aa), jnp.ones((s, pad), F32)], axis=1)
    sin_a = jnp.concatenate([zero, -jnp.sin(aa), jnp.sin(aa), jnp.zeros((s, pad), F32)], axis=1)
    cos_b = jnp.concatenate([jnp.cos(a1), jnp.cos(a1)], axis=1)
    sin_b = jnp.concatenate([-jnp.sin(a1), jnp.sin(a1)], axis=1)
    cos_c = jnp.concatenate([jnp.cos(ar), jnp.cos(ar), jnp.cos(ac), jnp.cos(ac)], axis=1)
    sin_c = jnp.concatenate([-jnp.sin(ar), jnp.sin(ar), -jnp.sin(ac), jnp.sin(ac)], axis=1)
    return (cos_a, sin_a), (cos_b, sin_b), (cos_c, sin_c)


def _band_table(tq, s):
    reach = max((win // (2 * d)) * d for win, d in B_PATTERNS)
    r = -(-reach // tq) * tq
    w = min(s, tq + 2 * r)
    j = jnp.arange(tq, dtype=jnp.int32)[:, None]
    x = jnp.arange(2 * w - tq, dtype=jnp.int32)[None, :]
    rel = x - (w - tq) - j
    mult = jnp.zeros(rel.shape, F32)
    for win, d in B_PATTERNS:
        mult = mult + jnp.logical_and(rel % d == 0, jnp.abs(rel) <= (win // (2 * d)) * d).astype(F32)
    return jnp.where(mult > 0, jnp.log2(jnp.maximum(mult, 1.0)), NEG), (w, r)


_BIG = ("w_in", "a_w_uq", "a_w_ukv", "w_out", "w_gate", "w_up", "w_down")
_SMALL = ("attn_norm", "a_q_norm", "a_kv_norm", "c_q_norm", "c_k_norm", "out_norm", "ffn_norm", "final_norm")
_WEIGHTS = ("attn_norm", "w_in", "a_q_norm", "a_w_uq", "a_kv_norm", "a_w_ukv", "c_q_norm", "c_k_norm", "out_norm",
            "w_out", "ffn_norm", "w_gate", "w_up", "w_down", "final_norm")


_ATTN = ("w_in", "a_w_uq", "a_w_ukv", "w_out")
_FFN = ("w_gate", "w_up", "w_down")


def _from_cols(a):
    return jnp.transpose(a, (1, 0, 2)).reshape(a.shape[1], N_CHIPS * a.shape[2])


def _from_rows(a):
    return a.reshape(N_CHIPS * a.shape[1], a.shape[2])


def _to_cols(a):
    return jnp.transpose(a.reshape(a.shape[0], N_CHIPS, a.shape[1] // N_CHIPS), (1, 0, 2))


def _to_rows(a):
    return a.reshape(N_CHIPS, a.shape[0] // N_CHIPS, a.shape[1])


def _assemble_attn(gw):
    w_in, uq, ukv, w_out = _from_cols(gw[0]), _from_cols(gw[1]), _from_cols(gw[2]), _from_rows(gw[3])
    d = w_in.shape[0]
    w_all = jnp.concatenate([w_in[:, :IN_A], jnp.zeros((d, A_PAD - IN_A), BF16), w_in[:, IN_A:]], axis=1)
    uq = uq.reshape(A_Q_RANK, A_HEADS, A_NOPE + A_ROPE)
    uq = jnp.pad(uq, ((0, 0), (0, 0), (0, A_QK - A_NOPE - A_ROPE))).reshape(A_Q_RANK, A_HEADS * A_QK)
    return dict(w_all=w_all, uq=uq, ukv=ukv, w_out=w_out)


def _assemble_ffn(gw):
    return dict(w_gate=_from_cols(gw[0]), w_up=_from_cols(gw[1]), w_down=_from_rows(gw[2]))


def _split_attn_grads(gl):
    w_all = gl["w_all"]
    w_in = jnp.concatenate([w_all[:, :IN_A], w_all[:, A_PAD:]], axis=1)
    uq = gl["uq"].reshape(A_Q_RANK, A_HEADS, A_QK)[:, :, :A_NOPE + A_ROPE].reshape(A_Q_RANK, A_HEADS * (A_NOPE + A_ROPE))
    return [_to_cols(w_in), _to_cols(uq), _to_cols(gl["ukv"]), _to_rows(gl["w_out"])]


def _split_ffn_grads(gl):
    return [_to_cols(gl["w_gate"]), _to_cols(gl["w_up"]), _to_rows(gl["w_down"])]


def _tie(a, token):
    return a + token[0:1, 0:1]


def _layer_fwd(x, wl, ffn_weights, sm, tabs, bias, t):
    s = x.shape[0]
    (cos_a, sin_a), (cos_b, sin_b), (cos_c, sin_c) = tabs
    h = _norm_fwd(x, sm["attn_norm"], wb=x.shape[1], cb=0, nb=1, shared_gain=True, out_dtype=BF16, name="attn_norm_fwd")
    p = _matmul(h, wl["w_all"], mode="nn", out_dtype=F32, name="in_proj", tm=1024, tn=640)
    cq_n = _norm_fwd(p, sm["a_q_norm"], wb=A_Q_RANK, cb=0, nb=1, shared_gain=True, out_dtype=BF16, name="a_q_norm_fwd")
    ckv_n = _norm_fwd(p, sm["a_kv_norm"], wb=A_KV_RANK, cb=1, nb=1, shared_gain=True, out_dtype=BF16, name="a_kv_norm_fwd")
    qa_raw = _matmul(cq_n, wl["uq"], mode="nn", out_dtype=F32, name="a_uq", tm=1024, tn=1024)
    kv = _matmul(ckv_n, wl["ukv"], mode="nn", out_dtype=BF16, name="a_ukv", tm=1024, tn=1024)
    k_nope = kv.reshape(s, A_HEADS, 2, LANE)[:, :, 0].astype(F32)
    k_rope = jnp.broadcast_to(p[:, PB_KR * LANE:(PB_KR + 1) * LANE][:, None, :], (s, A_HEADS, LANE))
    ka_raw = jnp.stack([k_nope, k_rope], axis=2).reshape(s, A_HEADS * A_QK)
    qa = _rope(qa_raw, cos_a, sin_a, tw=A_QK, cb=0, nb=A_HEADS, half=A_ROPE // 2, sign=1, out_dtype=BF16, name="a_rope_q")
    ka = _rope(ka_raw, cos_a, sin_a, tw=A_QK, cb=0, nb=A_HEADS, half=A_ROPE // 2, sign=1, out_dtype=BF16, name="a_rope_k")
    oa, lse_a = _flash_fwd(qa, ka, kv, None, hkv=A_HEADS, g=1, dqk=A_QK, q_cb=0, k_cb=0, v_cb=1, v_step=2,
                           scale=(A_NOPE + A_ROPE) ** -0.5, tq=t, band=None, name="a_flash_fwd")
    table, band = bias
    qb = _rope(p, cos_b, sin_b, tw=LANE, cb=PB_BQ, nb=B_HEADS, half=HEAD_DIM // 2, sign=1, out_dtype=BF16, name="b_rope_q")
    kb = _rope(p, cos_b, sin_b, tw=LANE, cb=PB_BK, nb=B_HEADS, half=HEAD_DIM // 2, sign=1, out_dtype=BF16, name="b_rope_k")
    vb = _cast_cols(p, cb=PB_BV, nb=B_HEADS, name="b_cast_v")
    ob, lse_b = _flash_fwd(qb, kb, vb, table, hkv=B_HEADS, g=1, dqk=LANE, q_cb=0, k_cb=0, v_cb=0, v_step=1,
                           scale=HEAD_DIM ** -0.5, tq=t, band=band, name="b_flash_fwd")
    qn = _norm_fwd(p, sm["c_q_norm"], wb=LANE, cb=PB_CQH, nb=C_HEADS, shared_gain=True, out_dtype=F32, name="c_q_norm_fwd")
    kn = _norm_fwd(p, sm["c_k_norm"], wb=LANE, cb=PB_CKH, nb=C_KV_HEADS, shared_gain=True, out_dtype=F32, name="c_k_norm_fwd")
    qc = _rope(qn, cos_c, sin_c, tw=LANE, cb=0, nb=C_HEADS, half=HEAD_DIM // 4, sign=1, out_dtype=BF16, name="c_rope_q")
    kc = _rope(kn, cos_c, sin_c, tw=LANE, cb=0, nb=C_KV_HEADS, half=HEAD_DIM // 4, sign=1, out_dtype=BF16, name="c_rope_k")
    vc = _cast_cols(p, cb=PB_CVH, nb=C_KV_HEADS, name="c_cast_v")
    oc, lse_c = _flash_fwd(qc, kc, vc, None, hkv=C_KV_HEADS, g=C_GROUP, dqk=LANE, q_cb=0, k_cb=0, v_cb=0, v_step=1,
                           scale=HEAD_DIM ** -0.5, tq=t, band=None, name="c_flash_fwd")
    g_out = sm["out_norm"]
    ga, gb, gc = g_out[:, :A_WIDTH], g_out[:, A_WIDTH:A_WIDTH + B_WIDTH], g_out[:, A_WIDTH + B_WIDTH:]
    ya = _norm_fwd(oa, ga, wb=A_WIDTH, cb=0, nb=1, shared_gain=True, out_dtype=BF16, name="out_norm_a_fwd")
    yb = _norm_fwd(ob, gb, wb=B_WIDTH, cb=0, nb=1, shared_gain=True, out_dtype=BF16, name="out_norm_b_fwd")
    yc = _norm_fwd(oc, gc, wb=C_WIDTH, cb=0, nb=1, shared_gain=True, out_dtype=BF16, name="out_norm_c_fwd")
    y = jnp.concatenate([ya, yb, yc], axis=1)
    x1 = _matmul(y, wl["w_out"], mode="nn", out_dtype=F32, name="out_proj", add=x, tm=1024, tn=512)
    wl = {**wl, **ffn_weights(x1)}
    h2 = _norm_fwd(x1, sm["ffn_norm"], wb=x.shape[1], cb=0, nb=1, shared_gain=True, out_dtype=BF16, name="ffn_norm_fwd")
    gate = _matmul(h2, wl["w_gate"], mode="nn", out_dtype=F32, name="ffn_gate", tm=1024, tn=512)
    up = _matmul(h2, wl["w_up"], mode="nn", out_dtype=F32, name="ffn_up", tm=1024, tn=512)
    act = _swiglu_fwd(gate, up, name="swiglu_fwd")
    x2 = _matmul(act, wl["w_down"], mode="nn", out_dtype=F32, name="ffn_down", add=x1, tm=512, tn=512)
    saved = dict(x=x, h=h, p=p, cq_n=cq_n, ckv_n=ckv_n, kv=kv, qa=qa, ka=ka, oa=oa, lse_a=lse_a, qb=qb, kb=kb, vb=vb, ob=ob,
                 lse_b=lse_b, qc=qc, kc=kc, vc=vc, oc=oc, lse_c=lse_c, y=y, x1=x1, h2=h2, gate=gate, up=up, act=act)
    return x2, saved, wl


def _layer_bwd(dx2, sv, wl, sm, tabs, bias, t, send_ffn, send_attn):
    s, d = dx2.shape
    (cos_a, sin_a), (cos_b, sin_b), (cos_c, sin_c) = tabs
    gw, gs = {}, {}
    dact = _matmul(dx2, wl["w_down"], mode="nt", out_dtype=F32, name="ffn_down_dx", tm=512, tn=512)
    gw["w_down"] = _matmul(sv["act"], dx2, mode="tn", out_dtype=BF16, name="ffn_down_dw", tm=512, tn=512)
    dgate, dup = _swiglu_bwd(sv["gate"], sv["up"], dact, name="swiglu_bwd")
    dh2 = _matmul(dgate, wl["w_gate"], mode="nt", out_dtype=F32, name="ffn_gate_dx", tm=512, tn=512)
    dh2 = _matmul(dup, wl["w_up"], mode="nt", out_dtype=F32, name="ffn_up_dx", add=dh2, tm=512, tn=512)
    gw["w_gate"] = _matmul(sv["h2"], dgate, mode="tn", out_dtype=BF16, name="ffn_gate_dw", tm=512, tn=512)
    gw["w_up"] = _matmul(sv["h2"], dup, mode="tn", out_dtype=BF16, name="ffn_up_dw", tm=512, tn=512)
    token = send_ffn(gw)
    dx1, gs["ffn_norm"] = _norm_bwd(sv["x1"], sm["ffn_norm"], dh2, wb=d, cb=0, nb=1, shared_gain=True, out_dtype=F32,
                                    name="ffn_norm_bwd", add=dx2)
    dy = _matmul(dx1, wl["w_out"], mode="nt", out_dtype=F32, name="out_proj_dx", tm=512, tn=512)
    gw["w_out"] = _matmul(sv["y"], dx1, mode="tn", out_dtype=BF16, name="out_proj_dw", tm=512, tn=512)
    g_out = _tie(sm["out_norm"], token)
    ga, gb, gc = g_out[:, :A_WIDTH], g_out[:, A_WIDTH:A_WIDTH + B_WIDTH], g_out[:, A_WIDTH + B_WIDTH:]
    dya, dyb, dyc = dy[:, :A_WIDTH], dy[:, A_WIDTH:A_WIDTH + B_WIDTH], dy[:, A_WIDTH + B_WIDTH:]
    doa, dga = _norm_bwd(sv["oa"], ga, dya, wb=A_WIDTH, cb=0, nb=1, shared_gain=True, out_dtype=F32, name="out_norm_a_bwd")
    dob, dgb = _norm_bwd(sv["ob"], gb, dyb, wb=B_WIDTH, cb=0, nb=1, shared_gain=True, out_dtype=F32, name="out_norm_b_bwd")
    doc, dgc = _norm_bwd(sv["oc"], gc, dyc, wb=C_WIDTH, cb=0, nb=1, shared_gain=True, out_dtype=F32, name="out_norm_c_bwd")
    gs["out_norm"] = jnp.concatenate([dga, dgb, dgc], axis=1)
    p = sv["p"]
    dqc, dkc, dvc = _flash_bwd(sv["qc"], sv["kc"], sv["vc"], sv["oc"], doc, sv["lse_c"], None, hkv=C_KV_HEADS,
                               g=C_GROUP, dqk=LANE, q_cb=0, k_cb=0, v_cb=0, v_step=1, scale=HEAD_DIM ** -0.5, tq=t,
                               band=None, name="c_flash_bwd")
    dqn = _rope(dqc, cos_c, sin_c, tw=LANE, cb=0, nb=C_HEADS, half=HEAD_DIM // 4, sign=-1, out_dtype=F32, name="c_rope_q_bwd")
    dkn = _rope(dkc, cos_c, sin_c, tw=LANE, cb=0, nb=C_KV_HEADS, half=HEAD_DIM // 4, sign=-1, out_dtype=F32, name="c_rope_k_bwd")
    dpcq, gs["c_q_norm"] = _norm_bwd(p, sm["c_q_norm"], dqn, wb=LANE, cb=PB_CQH, nb=C_HEADS, shared_gain=True,
                                     out_dtype=BF16, name="c_q_norm_bwd")
    dpck, gs["c_k_norm"] = _norm_bwd(p, sm["c_k_norm"], dkn, wb=LANE, cb=PB_CKH, nb=C_KV_HEADS, shared_gain=True,
                                     out_dtype=BF16, name="c_k_norm_bwd")
    table, band = bias
    dqb, dkb, dvb = _flash_bwd(sv["qb"], sv["kb"], sv["vb"], sv["ob"], dob, sv["lse_b"], table, hkv=B_HEADS, g=1,
                               dqk=LANE, q_cb=0, k_cb=0, v_cb=0, v_step=1, scale=HEAD_DIM ** -0.5, tq=t, band=band,
                               name="b_flash_bwd")
    dpbq = _rope(dqb, cos_b, sin_b, tw=LANE, cb=0, nb=B_HEADS, half=HEAD_DIM // 2, sign=-1, out_dtype=BF16, name="b_rope_q_bwd")
    dpbk = _rope(dkb, cos_b, sin_b, tw=LANE, cb=0, nb=B_HEADS, half=HEAD_DIM // 2, sign=-1, out_dtype=BF16, name="b_rope_k_bwd")
    dqa, dka, dva = _flash_bwd(sv["qa"], sv["ka"], sv["kv"], sv["oa"], doa, sv["lse_a"], None, hkv=A_HEADS, g=1,
                               dqk=A_QK, q_cb=0, k_cb=0, v_cb=1, v_step=2, scale=(A_NOPE + A_ROPE) ** -0.5, tq=t,
                               band=None, name="a_flash_bwd")
    dqa_raw = _rope(dqa, cos_a, sin_a, tw=A_QK, cb=0, nb=A_HEADS, half=A_ROPE // 2, sign=-1, out_dtype=BF16, name="a_rope_q_bwd")
    dka_raw = _rope(dka, cos_a, sin_a, tw=A_QK, cb=0, nb=A_HEADS, half=A_ROPE // 2, sign=-1, out_dtype=BF16, name="a_rope_k_bwd")
    dkr = _group_sum(dka_raw, n_out=1, g=A_HEADS, src=lambda n, j: 2 * j + 1, out_dtype=BF16, name="a_k_rope_sum")
    dkv = jnp.stack([dka_raw.reshape(s, A_HEADS, 2, LANE)[:, :, 0], dva.reshape(s, A_HEADS, LANE).astype(BF16)], axis=2)
    dkv = dkv.reshape(s, A_HEADS * 2 * LANE)
    dckv_n = _matmul(dkv, wl["ukv"], mode="nt", out_dtype=F32, name="a_ukv_dx", tm=1024, tn=512)
    gw["ukv"] = _matmul(sv["ckv_n"], dkv, mode="tn", out_dtype=BF16, name="a_ukv_dw", tm=512, tn=1024)
    dcq_n = _matmul(dqa_raw, wl["uq"], mode="nt", out_dtype=F32, name="a_uq_dx", tm=1024, tn=512)
    gw["uq"] = _matmul(sv["cq_n"], dqa_raw, mode="tn", out_dtype=BF16, name="a_uq_dw", tm=512, tn=1024)
    dcq, gs["a_q_norm"] = _norm_bwd(p, sm["a_q_norm"], dcq_n, wb=A_Q_RANK, cb=0, nb=1, shared_gain=True, out_dtype=BF16,
                                    name="a_q_norm_bwd")
    dckv, gs["a_kv_norm"] = _norm_bwd(p, sm["a_kv_norm"], dckv_n, wb=A_KV_RANK, cb=1, nb=1, shared_gain=True,
                                      out_dtype=BF16, name="a_kv_norm_bwd")
    dp = jnp.concatenate([dcq, dckv, dkr, jnp.zeros((s, A_PAD - (PB_KR + 1) * LANE), BF16), dpbq, dpbk,
                          dvb.astype(BF16), dpcq, dpck, dvc.astype(BF16)], axis=1)
    dh = _matmul(dp, wl["w_all"], mode="nt", out_dtype=F32, name="in_proj_dx", tm=512, tn=512)
    gw["w_all"] = _matmul(sv["h"], dp, mode="tn", out_dtype=BF16, name="in_proj_dw", tm=512, tn=640)
    token = send_attn(gw)
    dx, gs["attn_norm"] = _norm_bwd(sv["x"], _tie(sm["attn_norm"], token), dh, wb=d, cb=0, nb=1, shared_gain=True,
                                    out_dtype=F32, name="attn_norm_bwd", add=dx1)
    return dx, gs, token


def _pack_small(vals):
    flat = jnp.concatenate([vals[n].reshape(-1).astype(F32) for n in _SMALL])
    tile = SUBLANE * LANE
    padded = -(-flat.shape[0] // tile) * tile
    return jnp.pad(flat, (0, padded - flat.shape[0])).reshape(padded // LANE, LANE)


def _unpack_small(packed, like):
    flat = packed.reshape(-1)
    out, off = {}, 0
    for n in _SMALL:
        size = math.prod(like[n].shape)
        out[n] = flat[off:off + size].reshape(like[n].shape)
        off += size
    return out


def kernel(x, attn_norm, w_in, a_q_norm, a_w_uq, a_kv_norm, a_w_ukv, c_q_norm, c_k_norm, out_norm, w_out, ffn_norm, w_gate, w_up, w_down, final_norm, loss_target, m_attn_norm, m_w_in, m_a_q_norm, m_a_w_uq, m_a_kv_norm, m_a_w_ukv, m_c_q_norm, m_c_k_norm, m_out_norm, m_w_out, m_ffn_norm, m_w_gate, m_w_up, m_w_down, m_final_norm, v_attn_norm, v_w_in, v_a_q_norm, v_a_w_uq, v_a_kv_norm, v_a_w_ukv, v_c_q_norm, v_c_k_norm, v_out_norm, v_w_out, v_ffn_norm, v_w_gate, v_w_up, v_w_down, v_final_norm):
    w = dict(attn_norm=attn_norm, w_in=w_in, a_q_norm=a_q_norm, a_w_uq=a_w_uq, a_kv_norm=a_kv_norm, a_w_ukv=a_w_ukv,
             c_q_norm=c_q_norm, c_k_norm=c_k_norm, out_norm=out_norm, w_out=w_out, ffn_norm=ffn_norm, w_gate=w_gate,
             w_up=w_up, w_down=w_down, final_norm=final_norm)
    m = dict(attn_norm=m_attn_norm, w_in=m_w_in, a_q_norm=m_a_q_norm, a_w_uq=m_a_w_uq, a_kv_norm=m_a_kv_norm,
             a_w_ukv=m_a_w_ukv, c_q_norm=m_c_q_norm, c_k_norm=m_c_k_norm, out_norm=m_out_norm, w_out=m_w_out,
             ffn_norm=m_ffn_norm, w_gate=m_w_gate, w_up=m_w_up, w_down=m_w_down, final_norm=m_final_norm)
    v = dict(attn_norm=v_attn_norm, w_in=v_w_in, a_q_norm=v_a_q_norm, a_w_uq=v_a_w_uq, a_kv_norm=v_a_kv_norm,
             a_w_ukv=v_a_w_ukv, c_q_norm=v_c_q_norm, c_k_norm=v_c_k_norm, out_norm=v_out_norm, w_out=v_w_out,
             ffn_norm=v_ffn_norm, w_gate=v_w_gate, w_up=v_w_up, w_down=v_w_down, final_norm=v_final_norm)
    _, s, d = x.shape
    depth = attn_norm.shape[0]
    t = _pick(s, 512)

    me = (2 * lax.axis_index("x") + lax.axis_index("y")).astype(jnp.int32).reshape(1)

    gathers, after = {}, me
    for l in range(depth):
        for group, names in (("attn", _ATTN), ("ffn", _FFN)):
            bufs = [_cast_to_slot(w[n].reshape(-1, w[n].shape[-1]), me, layer=l, rows=w[n].shape[1], name=f"cast_{n}")
                    for n in names]
            send_sems, recv_sems, bufs, _, after = _exchange_start(bufs, None, after, name=f"gather_start_{group}{l}")
            gathers[group, l] = (send_sems, recv_sems, bufs)
    all_started = after

    def gathered(group, l, after):
        send_sems, recv_sems, bufs = gathers[group, l]
        return _exchange_wait(send_sems, recv_sems, bufs, None, after, name=f"gather_wait_{group}{l}")

    tabs = _rope_tables(s)
    bias = _band_table(t, s)

    xs = x.reshape(s, d)
    saved, wls, sms = [], [], []
    for l in range(depth):
        wl = _assemble_attn(gathered("attn", l, all_started if l == 0 else xs))
        sm = {n: w[n][l][None, :] for n in _SMALL if n != "final_norm"}
        xs, sv, wl = _layer_fwd(xs, wl, lambda after, l=l: _assemble_ffn(gathered("ffn", l, after)), sm, tabs, bias, t)
        saved.append(sv)
        wls.append(wl)
        sms.append(sm)
    dx, g_final, loss_row = _final_loss(xs, final_norm[None, :], loss_target.reshape(s, d), name="final_loss")
    loss = lax.psum(loss_row[0, 0], ("x", "y", "c"))

    sends = {}

    def send(group, l, srcs, after):
        lands = [lax.empty((3,) + a.shape[1:], BF16) for a in srcs]
        send_sems, recv_sems, srcs, lands, token = _exchange_start(srcs, lands, after, name=f"scatter_start_{group}{l}")
        sends[group, l] = (send_sems, recv_sems, srcs, lands)
        return token

    gs_layers, token = [None] * depth, all_started
    for l in reversed(range(depth)):
        dx, gs_layers[l], token = _layer_bwd(
            dx, saved[l], wls[l], sms[l], tabs, bias, t,
            lambda gw, l=l, tk=token: send("ffn", l, _split_ffn_grads(gw), tk),
            lambda gw, l=l: send("attn", l, _split_attn_grads(gw), dx))
    grad_x = dx.reshape(x.shape)

    srcs, lands = {}, {}
    for (group, l), (send_sems, recv_sems, s_bufs, l_bufs) in sends.items():
        got = _exchange_wait(send_sems, recv_sems, s_bufs, l_bufs, token, name=f"scatter_wait_{group}{l}")
        for k, n in enumerate(_ATTN if group == "attn" else _FFN):
            srcs[n, l], lands[n, l] = got[k], got[len(s_bufs) + k]
    sums = [_sum_parts([srcs[n, l] for l in range(depth)], [lands[n, l] for l in range(depth)], me, name="sum_" + n)
            for n in _BIG]
    sib = _sibling_exchange(sums, name="swap_core_sums")
    grads, deltas, new_m, new_v = {}, {}, {}, {}
    for n, mine, other in zip(_BIG, sums, sib):
        shp = w[n].shape
        two_d = (-1, shp[-1])
        res = _adamw(mine, other, w[n].reshape(two_d), m[n].reshape(two_d), v[n].reshape(two_d), name="adamw_" + n)
        grads[n], deltas[n], new_m[n], new_v[n] = [r.reshape(shp) for r in res]

    gsm = {n: jnp.stack([gs_layers[l][n][0] for l in range(depth)]) for n in _SMALL if n != "final_norm"}
    gsm["final_norm"] = g_final[0]
    packed = _pack_small(gsm)
    everyone = _all_gather_small(packed, name="gather_gain_grads").reshape(N_DEV, packed.shape[0], LANE)
    res = _small_adamw(everyone, _pack_small(w), _pack_small(m), _pack_small(v), name="adamw_gains")
    for dst, r in zip((grads, deltas, new_m, new_v), res):
        dst.update(_unpack_small(r, w))

    return (loss, grad_x, *[grads[n] for n in _WEIGHTS], *[deltas[n] for n in _WEIGHTS],
            *[new_m[n] for n in _WEIGHTS], *[new_v[n] for n in _WEIGHTS])
```

```python
import functools
import math

import jax
import jax.numpy as jnp
import numpy as np
from jax import lax
from jax.experimental import pallas as pl
from jax.experimental.pallas import tpu as pltpu

F32 = jnp.float32
BF16 = jnp.bfloat16
MESH = pl.DeviceIdType.MESH

HEAD_DIM = 128
ROPE_THETA = 10000.0
GRID_W = 64
EPS = 1e-6
NEG = -1e30
A_HEADS, A_Q_RANK, A_KV_RANK, A_NOPE, A_ROPE, A_V = 4, 512, 512, 128, 64, 128
B_HEADS = 6
B_PATTERNS = ((128, 1), (512, 4), (2048, 16))
C_HEADS, C_KV_HEADS = 6, 2
C_GROUP = C_HEADS // C_KV_HEADS
A_WIDTH, B_WIDTH, C_WIDTH = A_HEADS * A_V, B_HEADS * HEAD_DIM, C_HEADS * HEAD_DIM
IN_A = A_Q_RANK + A_KV_RANK + A_ROPE
IN_B = 3 * B_WIDTH
IN_C = C_WIDTH + 2 * C_KV_HEADS * HEAD_DIM
ADAM_LR, ADAM_B1, ADAM_B2, ADAM_EPS, ADAM_WD, ADAM_STEP = 0.001, 0.9, 0.999, 1e-08, 0.01, 10

LANE = 128
SUBLANE = 8
VMEM_BYTES_V7X = 64 * 1024 * 1024
VMEM_LIMIT_CAP = VMEM_BYTES_V7X - 8 * 1024 * 1024
N_CHIPS = 4
N_DEV = 8

A_PAD = 12 * LANE
PB_CQ, PB_CKV, PB_KR = 0, 4, 8
PB_BQ, PB_BK, PB_BV = 12, 18, 24
PB_CQH, PB_CKH, PB_CVH = 30, 36, 38
NP = 40 * LANE
A_QK = 2 * LANE


def _pick(n, cap, mult=LANE):
    if n <= cap:
        return n
    t = cap - cap % mult
    while t >= mult:
        if n % t == 0:
            return t
        t -= mult
    return n


def _rows_for(width_bytes, n_rows, target=2 * 1024 * 1024):
    return _pick(n_rows, max(SUBLANE, target // max(width_bytes, 1)), SUBLANE)


def _params(est_bytes):
    limit = int(min(max(est_bytes + (4 << 20), 32 << 20), VMEM_LIMIT_CAP))
    return pltpu.CompilerParams(vmem_limit_bytes=limit)


def _isz(x):
    return jnp.dtype(x.dtype).itemsize


_DIMS = {"nn": (((1,), (0,)), ((), ())), "nt": (((1,), (1,)), ((), ())), "tn": (((0,), (0,)), ((), ()))}


def _matmul(a, b, *, mode, out_dtype, name, add=None, tm=512, tn=512, col_shards=False):
    if mode == "tn":
        (k, m), (k2, n) = a.shape, b.shape
    elif mode == "nt":
        (m, k), (n, k2) = a.shape, b.shape
    else:
        (m, k), (k2, n) = a.shape, b.shape
    assert k == k2, (a.shape, b.shape, mode)
    tm, tn = _pick(m, tm), (n // N_CHIPS if col_shards else _pick(n, tn))
    a_spec = pl.BlockSpec((k, tm), lambda i, j: (0, i)) if mode == "tn" else pl.BlockSpec((tm, k), lambda i, j: (i, 0))
    b_spec = pl.BlockSpec((tn, k), lambda i, j: (j, 0)) if mode == "nt" else pl.BlockSpec((k, tn), lambda i, j: (0, j))
    o_spec = pl.BlockSpec((None, tm, tn), lambda i, j: (j, i, 0)) if col_shards else pl.BlockSpec((tm, tn), lambda i, j: (i, j))
    dims = _DIMS[mode]

    def body(*refs):
        if add is None:
            a_ref, b_ref, o_ref = refs
        else:
            a_ref, b_ref, add_ref, o_ref = refs
        acc = lax.dot_general(a_ref[...].astype(BF16), b_ref[...].astype(BF16), dims, preferred_element_type=F32)
        if add is not None:
            acc = acc + add_ref[...].astype(F32)
        o_ref[...] = acc.astype(out_dtype)

    ins, specs = [a, b], [a_spec, b_spec]
    if add is not None:
        ins.append(add)
        specs.append(o_spec)
    est = 2 * (tm * k * _isz(a) + tn * k * _isz(b) + tm * tn * (jnp.dtype(out_dtype).itemsize + (4 if add is not None else 0)))
    est += (tm + tn) * k * 2 + 2 * tm * tn * 4
    return pl.pallas_call(
        body, name=name, grid=(m // tm, n // tn), in_specs=specs, out_specs=o_spec,
        out_shape=jax.ShapeDtypeStruct((N_CHIPS, m, tn) if col_shards else (m, n), out_dtype),
        compiler_params=_params(est),
    )(*ins)


def _ffn_up(h, wg, wu, *, name):
    s, d = h.shape
    _, _, c = wg.shape
    tm = _pick(s, 512, SUBLANE)

    def body(h_ref, wg_ref, wu_ref, g_ref, u_ref, a_ref):
        hv = h_ref[...]
        gv = jnp.dot(hv, wg_ref[...], preferred_element_type=F32)
        uv = jnp.dot(hv, wu_ref[...], preferred_element_type=F32)
        g_ref[...] = gv
        u_ref[...] = uv
        a_ref[...] = (gv / (1.0 + jnp.exp(-gv)) * uv).astype(BF16)

    w_spec = pl.BlockSpec((None, d, c), lambda j, i: (j, 0, 0))
    o_spec = pl.BlockSpec((tm, c), lambda j, i: (i, j))
    est = 2 * (tm * d * 2 + 2 * d * c * 2 + tm * c * 10) + 4 * tm * c * 4
    return pl.pallas_call(
        body, name=name, grid=(N_CHIPS, s // tm), in_specs=[pl.BlockSpec((tm, d), lambda j, i: (i, 0)), w_spec, w_spec],
        out_specs=[o_spec, o_spec, o_spec],
        out_shape=[jax.ShapeDtypeStruct((s, N_CHIPS * c), F32), jax.ShapeDtypeStruct((s, N_CHIPS * c), F32),
                   jax.ShapeDtypeStruct((s, N_CHIPS * c), BF16)],
        compiler_params=_params(est),
    )(h, wg, wu)


def _ffn_down_dx(dx, w_down, gate, up, *, name):
    s, d = dx.shape
    f = w_down.shape[0]
    tm, tn = _pick(s, 512, SUBLANE), _pick(f, 512)

    def body(dx_ref, w_ref, g_ref, u_ref, dg_ref, du_ref):
        dact = lax.dot_general(dx_ref[...], w_ref[...], _DIMS["nt"], preferred_element_type=F32)
        gv, uv = g_ref[...], u_ref[...]
        sig = 1.0 / (1.0 + jnp.exp(-gv))
        dg_ref[...] = (dact * uv * (sig * (1.0 + gv * (1.0 - sig)))).astype(BF16)
        du_ref[...] = (dact * (gv * sig)).astype(BF16)

    t_spec = pl.BlockSpec((tm, tn), lambda i, j: (i, j))
    est = 2 * (tm * d * 2 + tn * d * 2 + tm * tn * 12) + 6 * tm * tn * 4
    return pl.pallas_call(
        body, name=name, grid=(s // tm, f // tn),
        in_specs=[pl.BlockSpec((tm, d), lambda i, j: (i, 0)), pl.BlockSpec((tn, d), lambda i, j: (j, 0)), t_spec, t_spec],
        out_specs=[t_spec, t_spec], out_shape=[jax.ShapeDtypeStruct((s, f), BF16)] * 2, compiler_params=_params(est),
    )(dx, w_down, gate, up)


def _ffn_up_dx(dgate, dup, wg, wu, *, name):
    s, f = dgate.shape
    _, d, c = wg.shape
    tm, tn = _pick(s, 1024, SUBLANE), _pick(d, 1024)
    nk = 2 * N_CHIPS

    def body(dg_ref, du_ref, wg_ref, wu_ref, o_ref, acc):
        kk = pl.program_id(2)

        @pl.when(kk == 0)
        def _():
            acc[...] = jnp.zeros_like(acc)

        @pl.when(kk < N_CHIPS)
        def _():
            acc[...] += lax.dot_general(dg_ref[...], wg_ref[...], _DIMS["nt"], preferred_element_type=F32)

        @pl.when(kk >= N_CHIPS)
        def _():
            acc[...] += lax.dot_general(du_ref[...], wu_ref[...], _DIMS["nt"], preferred_element_type=F32)

        @pl.when(kk == nk - 1)
        def _():
            o_ref[...] = acc[...]

    last = N_CHIPS - 1
    est = 2 * (2 * tm * c * 2 + 2 * tn * c * 2 + tm * tn * 4) + 2 * tm * tn * 4
    return pl.pallas_call(
        body, name=name, grid=(s // tm, d // tn, nk),
        in_specs=[pl.BlockSpec((tm, c), lambda i, j, kk: (i, jnp.minimum(kk, last))),
                  pl.BlockSpec((tm, c), lambda i, j, kk: (i, jnp.maximum(kk - N_CHIPS, 0))),
                  pl.BlockSpec((None, tn, c), lambda i, j, kk: (jnp.minimum(kk, last), j, 0)),
                  pl.BlockSpec((None, tn, c), lambda i, j, kk: (jnp.maximum(kk - N_CHIPS, 0), j, 0))],
        out_specs=pl.BlockSpec((tm, tn), lambda i, j, kk: (i, j)),
        out_shape=jax.ShapeDtypeStruct((s, d), F32), scratch_shapes=[pltpu.VMEM((tm, tn), F32)],
        compiler_params=_params(est),
    )(dgate, dup, wg, wu)


def _norm_fwd(x, gain, *, wb, cb, nb, shared_gain, out_dtype, name):
    s = x.shape[0]
    ts = _rows_for(wb * 4, s)

    def body(x_ref, g_ref, o_ref):
        xv = x_ref[...].astype(F32)
        r = lax.rsqrt(jnp.mean(xv * xv, axis=1, keepdims=True) + EPS)
        o_ref[...] = ((xv * r) * g_ref[...]).astype(out_dtype)

    return pl.pallas_call(
        body, name=name, grid=(nb, s // ts),
        in_specs=[pl.BlockSpec((ts, wb), lambda n, i: (i, cb + n)),
                  pl.BlockSpec((1, wb), (lambda n, i: (0, 0)) if shared_gain else (lambda n, i: (0, n)))],
        out_specs=pl.BlockSpec((ts, wb), lambda n, i: (i, n)),
        out_shape=jax.ShapeDtypeStruct((s, nb * wb), out_dtype), compiler_params=_params(6 * ts * wb * 4),
    )(x, gain)


def _norm_bwd(x, gain, dy, *, wb, cb, nb, shared_gain, out_dtype, name, dy_cb=0, add=None, bf16_copy=False):
    s = x.shape[0]
    ts = _rows_for(wb * 4, s, target=1024 * 1024)
    gw = wb if shared_gain else nb * wb

    def body(*refs):
        refs = list(refs)
        dxb_ref = refs.pop() if bf16_copy else None
        if add is None:
            x_ref, g_ref, dy_ref, dx_ref, dg_ref = refs
        else:
            x_ref, g_ref, dy_ref, add_ref, dx_ref, dg_ref = refs
        n, i = pl.program_id(0), pl.program_id(1)
        xv = x_ref[...].astype(F32)
        dyv = dy_ref[...].astype(F32)
        r = lax.rsqrt(jnp.mean(xv * xv, axis=1, keepdims=True) + EPS)
        xh = xv * r
        dyg = dyv * g_ref[...]
        dx = r * (dyg - xh * jnp.mean(dyg * xh, axis=1, keepdims=True))
        if add is not None:
            dx = dx + add_ref[...]
        dx_ref[...] = dx.astype(out_dtype)
        if bf16_copy:
            dxb_ref[...] = dx.astype(BF16)
        first = jnp.logical_and(n == 0, i == 0) if shared_gain else (i == 0)

        @pl.when(first)
        def _():
            dg_ref[...] = jnp.zeros_like(dg_ref)

        dg_ref[...] += jnp.sum(dyv * xh, axis=0, keepdims=True)

    ins = [x, gain, dy]
    specs = [pl.BlockSpec((ts, wb), lambda n, i: (i, cb + n)),
             pl.BlockSpec((1, wb), (lambda n, i: (0, 0)) if shared_gain else (lambda n, i: (0, n))),
             pl.BlockSpec((ts, wb), lambda n, i: (i, dy_cb + n))]
    if add is not None:
        ins.append(add)
        specs.append(pl.BlockSpec((ts, wb), lambda n, i: (i, n)))
    out_specs = [pl.BlockSpec((ts, wb), lambda n, i: (i, n)),
                 pl.BlockSpec((1, wb), (lambda n, i: (0, 0)) if shared_gain else (lambda n, i: (0, n)))]
    out_shape = [jax.ShapeDtypeStruct((s, nb * wb), out_dtype), jax.ShapeDtypeStruct((1, gw), F32)]
    if bf16_copy:
        out_specs.append(out_specs[0])
        out_shape.append(jax.ShapeDtypeStruct((s, nb * wb), BF16))
    return pl.pallas_call(
        body, name=name, grid=(nb, s // ts), in_specs=specs, out_specs=out_specs, out_shape=out_shape,
        compiler_params=_params(14 * ts * wb * 4),
    )(*ins)


def _swap_halves(x, half):
    if 2 * half == LANE:
        return pltpu.roll(x, half, axis=1)
    lane = lax.broadcasted_iota(jnp.int32, x.shape, 1)
    first = jnp.bitwise_and(lane, 2 * half - 1) < half
    return jnp.where(first, pltpu.roll(x, LANE - half, axis=1), pltpu.roll(x, half, axis=1))


def _rope(x, cos_t, sin_t, *, tw, cb, nb, half, sign, out_dtype, name):
    s = x.shape[0]
    ts = _rows_for(tw * 4, s)

    def body(x_ref, c_ref, s_ref, o_ref):
        for q in range(tw // LANE):
            sl = slice(q * LANE, (q + 1) * LANE)
            xv = x_ref[:, sl].astype(F32)
            sv = s_ref[:, sl]
            if sign < 0:
                sv = -sv
            o_ref[:, sl] = (xv * c_ref[:, sl] + _swap_halves(xv, half) * sv).astype(out_dtype)

    return pl.pallas_call(
        body, name=name, grid=(nb, s // ts),
        in_specs=[pl.BlockSpec((ts, tw), lambda n, i: (i, cb + n)),
                  pl.BlockSpec((ts, tw), lambda n, i: (i, 0)),
                  pl.BlockSpec((ts, tw), lambda n, i: (i, 0))],
        out_specs=pl.BlockSpec((ts, tw), lambda n, i: (i, n)),
        out_shape=jax.ShapeDtypeStruct((s, nb * tw), out_dtype), compiler_params=_params(10 * ts * tw * 4),
    )(x, cos_t, sin_t)


def _cast_cols(x, *, cb, nb, name):
    s = x.shape[0]
    ts = _rows_for(LANE * 4, s)

    def body(x_ref, o_ref):
        o_ref[...] = x_ref[...].astype(BF16)

    return pl.pallas_call(
        body, name=name, grid=(nb, s // ts), in_specs=[pl.BlockSpec((ts, LANE), lambda n, i: (i, cb + n))],
        out_specs=pl.BlockSpec((ts, LANE), lambda n, i: (i, n)),
        out_shape=jax.ShapeDtypeStruct((s, nb * LANE), BF16), compiler_params=_params(4 * ts * LANE * 4),
    )(x)


def _group_sum(x, *, n_out, g, src, out_dtype, name):
    s = x.shape[0]
    ts = _rows_for(LANE * 4, s)

    def body(*refs):
        acc = refs[0][...].astype(F32)
        for r in refs[1:-1]:
            acc = acc + r[...].astype(F32)
        refs[-1][...] = acc.astype(out_dtype)

    return pl.pallas_call(
        body, name=name, grid=(n_out, s // ts),
        in_specs=[pl.BlockSpec((ts, LANE), functools.partial(lambda n, i, j: (i, src(n, j)), j=j)) for j in range(g)],
        out_specs=pl.BlockSpec((ts, LANE), lambda n, i: (i, n)),
        out_shape=jax.ShapeDtypeStruct((s, n_out * LANE), out_dtype), compiler_params=_params(4 * g * ts * LANE * 4),
    )(*([x] * g))


LOG2E = 1.4426950408889634
ATTN_ROW_CHUNK = 256


def _attn_window(i, tq, s, band):
    w, r = band
    start = jnp.clip(i * tq - r, 0, s - w)
    return pl.multiple_of(start, tq), pl.multiple_of((w - tq) - (i * tq - start), LANE)


def _flash_fwd(q, k, v, table, *, hkv, g, dqk, q_cb, k_cb, v_cb, v_step, scale, tq, band, name):
    s = q.shape[0]
    n = s // tq
    hq = hkv * g
    rc = min(tq, ATTN_ROW_CHUNK)
    w = s if band is None else band[0]

    def body(*refs):
        if band is None:
            q_ref, k_ref, v_ref, o_ref, lse_ref = refs
            kw, vw = k_ref[...], v_ref[...]
        else:
            q_ref, k_ref, v_ref, t_ref, o_ref, lse_ref = refs
            start, u = _attn_window(pl.program_id(1), tq, s, band)
            kw, vw = k_ref[pl.ds(start, w), :], v_ref[pl.ds(start, w), :]
        for c in range(tq // rc):
            rows = slice(c * rc, (c + 1) * rc)
            sc = lax.dot_general(q_ref[rows, :], kw, _DIMS["nt"], preferred_element_type=F32) * (scale * LOG2E)
            if band is not None:
                sc = sc + t_ref[rows, pl.ds(u, w)]
            m = jnp.max(sc, axis=1, keepdims=True)
            p = jnp.exp2(sc - m)
            l = jnp.sum(p, axis=1, keepdims=True)
            o_ref[rows, :] = jnp.dot(p.astype(BF16), vw, preferred_element_type=F32) / l
            lse_ref[0, rows, :] = jnp.broadcast_to(m + jnp.log2(l), (rc, LANE))

    ins = [q, k, v]
    specs = [pl.BlockSpec((tq, dqk), lambda h, i: (i, q_cb + h)),
             pl.BlockSpec((s, dqk), lambda h, i: (0, k_cb + h // g)),
             pl.BlockSpec((s, LANE), lambda h, i: (0, v_cb + v_step * (h // g)))]
    if band is not None:
        ins.append(table)
        specs.append(pl.BlockSpec(table.shape, lambda h, i: (0, 0)))
    est = 4 * s * (dqk + LANE) + 6 * rc * w * 4 + 8 * tq * LANE * 4 + (0 if band is None else 2 * table.size * 4)
    return pl.pallas_call(
        body, name=name, grid=(hq, n), in_specs=specs,
        out_specs=[pl.BlockSpec((tq, LANE), lambda h, i: (i, h)), pl.BlockSpec((1, tq, LANE), lambda h, i: (h, i, 0))],
        out_shape=[jax.ShapeDtypeStruct((s, hq * LANE), F32), jax.ShapeDtypeStruct((hq, s, LANE), F32)],
        compiler_params=_params(est),
    )(*ins)


def _flash_bwd(q, k, v, o, do, lse, table, *, hkv, g, dqk, q_cb, k_cb, v_cb, v_step, scale, tq, band, name):
    s = q.shape[0]
    n = s // tq
    hq = hkv * g
    rc = min(tq, ATTN_ROW_CHUNK)
    w = s if band is None else band[0]

    def body(*refs):
        if band is None:
            q_ref, k_ref, v_ref, o_ref, do_ref, lse_ref, dq_ref, dk_ref, dv_ref = refs
            keys = slice(None)
        else:
            q_ref, k_ref, v_ref, o_ref, do_ref, lse_ref, t_ref, dq_ref, dk_ref, dv_ref = refs
            start, u = _attn_window(pl.program_id(1), tq, s, band)
            keys = pl.ds(start, w)
        h, i = pl.program_id(0), pl.program_id(1)

        @pl.when(jnp.logical_and(h % g == 0, i == 0))
        def _():
            dk_ref[...] = jnp.zeros_like(dk_ref)
            dv_ref[...] = jnp.zeros_like(dv_ref)

        kw, vw = k_ref[keys, :], v_ref[keys, :]
        for c in range(tq // rc):
            rows = slice(c * rc, (c + 1) * rc)
            qv = q_ref[rows, :]
            dof = do_ref[rows, :]
            dov = dof.astype(BF16)
            sc = lax.dot_general(qv, kw, _DIMS["nt"], preferred_element_type=F32) * (scale * LOG2E)
            if band is not None:
                sc = sc + t_ref[rows, pl.ds(u, w)]
            p = jnp.exp2(sc - lse_ref[0, rows, 0:1])
            dp = lax.dot_general(dov, vw, _DIMS["nt"], preferred_element_type=F32)
            delta = jnp.sum(dof * o_ref[rows, :], axis=1, keepdims=True)
            ds = (p * (dp - delta) * scale).astype(BF16)
            dv_ref[keys, :] += lax.dot_general(p.astype(BF16), dov, _DIMS["tn"], preferred_element_type=F32)
            dk_ref[keys, :] += lax.dot_general(ds, qv, _DIMS["tn"], preferred_element_type=F32)
            dq_ref[rows, :] = jnp.dot(ds, kw, preferred_element_type=F32)

    ins = [q, k, v, o, do, lse]
    specs = [pl.BlockSpec((tq, dqk), lambda h, i: (i, q_cb + h)),
             pl.BlockSpec((s, dqk), lambda h, i: (0, k_cb + h // g)),
             pl.BlockSpec((s, LANE), lambda h, i: (0, v_cb + v_step * (h // g))),
             pl.BlockSpec((tq, LANE), lambda h, i: (i, h)),
             pl.BlockSpec((tq, LANE), lambda h, i: (i, h)),
             pl.BlockSpec((1, tq, LANE), lambda h, i: (h, i, 0))]
    if band is not None:
        ins.append(table)
        specs.append(pl.BlockSpec(table.shape, lambda h, i: (0, 0)))
    est = (4 + 8) * s * (dqk + LANE) + 10 * rc * w * 4 + 12 * tq * LANE * 4 + (0 if band is None else 2 * table.size * 4)
    return pl.pallas_call(
        body, name=name, grid=(hq, n), in_specs=specs,
        out_specs=[pl.BlockSpec((tq, dqk), lambda h, i: (i, h)),
                   pl.BlockSpec((s, dqk), lambda h, i: (0, h // g)),
                   pl.BlockSpec((s, LANE), lambda h, i: (0, h // g))],
        out_shape=[jax.ShapeDtypeStruct((s, hq * dqk), F32), jax.ShapeDtypeStruct((s, hkv * dqk), F32),
                   jax.ShapeDtypeStruct((s, hkv * LANE), F32)],
        compiler_params=_params(est),
    )(*ins)


def _final_loss(x, gain, target, *, name):
    s, d = x.shape
    ts = _rows_for(d * 4, s, target=1024 * 1024)

    def body(x_ref, g_ref, t_ref, dx_ref, dg_ref, loss_ref, dxb_ref):
        i = pl.program_id(0)
        xv = x_ref[...]
        gv = g_ref[...]
        r = lax.rsqrt(jnp.mean(xv * xv, axis=1, keepdims=True) + EPS)
        xh = xv * r
        err = xh * gv - t_ref[...]
        dy = err / d
        dyg = dy * gv
        dx = r * (dyg - xh * jnp.mean(dyg * xh, axis=1, keepdims=True))
        dx_ref[...] = dx
        dxb_ref[...] = dx.astype(BF16)

        @pl.when(i == 0)
        def _():
            dg_ref[...] = jnp.zeros_like(dg_ref)
            loss_ref[...] = jnp.zeros_like(loss_ref)

        dg_ref[...] += jnp.sum(dy * xh, axis=0, keepdims=True)
        part = jnp.sum(jnp.mean(err * err, axis=1, keepdims=True), axis=0, keepdims=True)
        loss_ref[...] += jnp.broadcast_to(0.5 * part, (1, LANE))

    row = pl.BlockSpec((ts, d), lambda i: (i, 0))
    return pl.pallas_call(
        body, name=name, grid=(s // ts,),
        in_specs=[row, pl.BlockSpec((1, d), lambda i: (0, 0)), row],
        out_specs=[row, pl.BlockSpec((1, d), lambda i: (0, 0)), pl.BlockSpec((1, LANE), lambda i: (0, 0)), row],
        out_shape=[jax.ShapeDtypeStruct((s, d), F32), jax.ShapeDtypeStruct((1, d), F32),
                   jax.ShapeDtypeStruct((1, LANE), F32), jax.ShapeDtypeStruct((s, d), BF16)],
        compiler_params=_params(14 * ts * d * 4),
    )(x, gain, target)


def _cast_to_slot(x2d, me, *, layer, rows, name):
    c = x2d.shape[1]
    tr = _rows_for(c * 4, rows)
    nt = rows // tr

    def body(me_ref, x_ref, o_ref):
        o_ref[...] = x_ref[...].astype(BF16)

    return pl.pallas_call(
        body, name=name,
        grid_spec=pltpu.PrefetchScalarGridSpec(
            num_scalar_prefetch=1, grid=(nt,),
            in_specs=[pl.BlockSpec((tr, c), lambda i, me_ref: (layer * nt + i, 0))],
            out_specs=pl.BlockSpec((None, tr, c), lambda i, me_ref: (me_ref[0], i, 0))),
        out_shape=jax.ShapeDtypeStruct((N_CHIPS, rows, c), BF16), compiler_params=_params(6 * tr * c * 4),
    )(me, x2d)


def _sum_parts(srcs, lands, me, *, name):
    depth = len(srcs)
    _, r, c = srcs[0].shape
    tr = _rows_for(c * 4, r, target=1024 * 1024)
    nt = r // tr

    def body(me_ref, *refs):
        o_ref = refs[-1]
        l = pl.program_id(0)
        for k in range(depth):
            @pl.when(l == k)
            def _(k=k):
                acc = refs[k][...].astype(F32)
                for p in range(3):
                    acc = acc + refs[depth + k][p].astype(F32)
                o_ref[...] = acc

    def rows_of(k):
        return lambda l, i, me_ref: jnp.where(l == k, i, jnp.where(l < k, 0, nt - 1))

    in_specs = [pl.BlockSpec((None, tr, c), functools.partial(lambda l, i, me_ref, f: (me_ref[0], f(l, i, me_ref), 0), f=rows_of(k)))
                for k in range(depth)]
    in_specs += [pl.BlockSpec((3, tr, c), functools.partial(lambda l, i, me_ref, f: (0, f(l, i, me_ref), 0), f=rows_of(k)))
                 for k in range(depth)]
    return pl.pallas_call(
        body, name=name,
        grid_spec=pltpu.PrefetchScalarGridSpec(
            num_scalar_prefetch=1, grid=(depth, nt), in_specs=in_specs,
            out_specs=pl.BlockSpec((tr, c), lambda l, i, me_ref: (l * nt + i, 0))),
        out_shape=jax.ShapeDtypeStruct((depth * r, c), F32), compiler_params=_params(depth * 10 * tr * c * 4),
    )(me, *srcs, *lands)


def _adamw_math(w, g, m, v):
    m2 = ADAM_B1 * m + (1.0 - ADAM_B1) * g
    v2 = ADAM_B2 * v + (1.0 - ADAM_B2) * (g * g)
    m_hat = m2 / (1.0 - ADAM_B1 ** ADAM_STEP)
    v_hat = v2 / (1.0 - ADAM_B2 ** ADAM_STEP)
    delta = -ADAM_LR * (m_hat / (jnp.sqrt(v_hat) + ADAM_EPS) + ADAM_WD * w)
    return delta, m2, v2


def _adamw(g_a, g_b, w, m, v, *, name):
    r, c = w.shape
    tr = _rows_for(c * 4, r, target=512 * 1024)

    def body(a_ref, b_ref, w_ref, m_ref, v_ref, g_out, d_out, m_out, v_out):
        gv = a_ref[...] + b_ref[...]
        delta, m2, v2 = _adamw_math(w_ref[...], gv, m_ref[...], v_ref[...])
        g_out[...] = gv
        d_out[...] = delta
        m_out[...] = m2
        v_out[...] = v2

    spec = pl.BlockSpec((tr, c), lambda i: (i, 0))
    return pl.pallas_call(
        body, name=name, grid=(r // tr,), in_specs=[spec] * 5, out_specs=[spec] * 4,
        out_shape=[jax.ShapeDtypeStruct((r, c), F32)] * 4, compiler_params=_params(22 * tr * c * 4),
    )(g_a, g_b, w, m, v)


def _small_adamw(g_all, w, m, v, *, name):
    r, c = w.shape

    def body(ga_ref, w_ref, m_ref, v_ref, g_out, d_out, m_out, v_out):
        gv = ga_ref[0]
        for j in range(1, N_DEV):
            gv = gv + ga_ref[j]
        delta, m2, v2 = _adamw_math(w_ref[...], gv, m_ref[...], v_ref[...])
        g_out[...] = gv
        d_out[...] = delta
        m_out[...] = m2
        v_out[...] = v2

    return pl.pallas_call(body, name=name, out_shape=[jax.ShapeDtypeStruct((r, c), F32)] * 4)(g_all, w, m, v)


_ANY = pl.BlockSpec(memory_space=pl.ANY)


_HBM = pl.BlockSpec(memory_space=pltpu.HBM)
_SEM = pl.BlockSpec(memory_space=pltpu.SEMAPHORE)
_EFFECT = pltpu.SideEffectType.DATAFLOW_SIDE_EFFECTING


def _peer_chips():
    x, y = lax.axis_index("x"), lax.axis_index("y")
    return 2 * x + y, [(1 - x, y), (x, 1 - y), (1 - x, 1 - y)]


def _exchange_copy(srcs, lands, send_sems, recv_sems, k, p, scatter):
    me, peers = _peer_chips()
    px, py = peers[p]
    return pltpu.make_async_remote_copy(
        src_ref=srcs[k].at[2 * px + py] if scatter else srcs[k].at[me],
        dst_ref=lands[k].at[p] if scatter else lands[k].at[me],
        send_sem=send_sems.at[3 * k + p], recv_sem=recv_sems.at[3 * k + p],
        device_id=(px, py, lax.axis_index("c")), device_id_type=MESH)


def _exchange_start(srcs, lands, after, *, name):
    scatter = lands is not None
    n = len(srcs)
    bufs = list(srcs) + (list(lands) if scatter else [])
    nb = len(bufs)

    def body(*refs):
        buf_refs, send_sems, recv_sems = refs[:nb], refs[nb + 1], refs[nb + 2]
        token = refs[-1]
        s_refs = buf_refs[:n]
        l_refs = buf_refs[n:] if scatter else s_refs
        for k in range(n):
            for p in range(3):
                _exchange_copy(s_refs, l_refs, send_sems, recv_sems, k, p, scatter).start()
        token[...] = jnp.zeros_like(token)

    out = pl.pallas_call(
        body, name=name,
        out_shape=(pltpu.SemaphoreType.DMA((3 * n,)), pltpu.SemaphoreType.DMA((3 * n,)),
                   *[pltpu.HBM(b.shape, b.dtype) for b in bufs], jax.ShapeDtypeStruct((SUBLANE, LANE), F32)),
        in_specs=[_HBM] * nb + [_ANY],
        out_specs=(_SEM, _SEM, *[_HBM] * nb, pl.BlockSpec(memory_space=pltpu.VMEM)),
        input_output_aliases={i: 2 + i for i in range(nb)},
        compiler_params=pltpu.CompilerParams(has_side_effects=_EFFECT),
    )(*[pltpu.with_memory_space_constraint(b, pltpu.HBM) for b in bufs], after)
    send_sems, recv_sems = out[0], out[1]
    thru = out[2:2 + nb]
    return send_sems, recv_sems, list(thru[:n]), (list(thru[n:]) if scatter else None), out[-1]


def _exchange_wait(send_sems, recv_sems, srcs, lands, after, *, name):
    scatter = lands is not None
    n = len(srcs)
    bufs = list(srcs) + (list(lands) if scatter else [])
    nb = len(bufs)

    def body(*refs):
        buf_refs, send_sems_ref, recv_sems_ref = refs[:nb], refs[nb], refs[nb + 1]
        s_refs = buf_refs[:n]
        l_refs = buf_refs[n:] if scatter else s_refs
        for k in range(n):
            for p in range(3):
                cp = _exchange_copy(s_refs, l_refs, send_sems_ref, recv_sems_ref, k, p, scatter)
                cp.wait_send()
                cp.wait_recv()

    out = pl.pallas_call(
        body, name=name, out_shape=tuple(pltpu.HBM(b.shape, b.dtype) for b in bufs),
        in_specs=[_HBM] * nb + [_SEM, _SEM, _ANY], out_specs=tuple([_HBM] * nb),
        input_output_aliases={i: i for i in range(nb)},
        compiler_params=pltpu.CompilerParams(has_side_effects=_EFFECT),
    )(*bufs, send_sems, recv_sems, after)
    return list(out)


def _sibling_exchange(srcs, *, name):
    n = len(srcs)

    def body(*refs):
        src, out = refs[:n], refs[n:2 * n]
        send_sems, recv_sems = refs[2 * n:]
        sibling = (lax.axis_index("x"), lax.axis_index("y"), 1 - lax.axis_index("c"))
        copies = [pltpu.make_async_remote_copy(src_ref=src[k], dst_ref=out[k], send_sem=send_sems.at[k],
                                               recv_sem=recv_sems.at[k], device_id=sibling, device_id_type=MESH)
                  for k in range(n)]
        for cp in copies:
            cp.start()
        for cp in copies:
            cp.wait_recv()
        for cp in copies:
            cp.wait_send()

    return pl.pallas_call(
        body, name=name, in_specs=[_ANY] * n, out_specs=[_ANY] * n,
        out_shape=[jax.ShapeDtypeStruct(a.shape, a.dtype) for a in srcs],
        scratch_shapes=[pltpu.SemaphoreType.DMA((n,)), pltpu.SemaphoreType.DMA((n,))],
    )(*srcs)


def _all_gather_small(block, *, name):
    m_per, ncol = block.shape

    def body(x_ref, out_ref, send_sems, recv_sems, local_sem):
        x, y, c = lax.axis_index("x"), lax.axis_index("y"), lax.axis_index("c")
        me, sibling = (x, y, c), (x, y, 1 - c)
        chips = [(1 - x, y), (x, 1 - y), (1 - x, 1 - y)]

        def rows(px, py, pc):
            return out_ref.at[pl.ds((4 * px + 2 * py + pc) * m_per, m_per), :]

        def copy(k, blk, to, src=None):
            return pltpu.make_async_remote_copy(
                src_ref=rows(*blk) if src is None else src, dst_ref=rows(*blk),
                send_sem=send_sems.at[k], recv_sem=recv_sems.at[k], device_id=to, device_id_type=MESH)

        mine = pltpu.make_async_copy(x_ref, rows(*me), local_sem)
        mine.start()
        first = [copy(0, me, sibling, src=x_ref)]
        first += [copy(1 + j, me, (*chip, c), src=x_ref) for j, chip in enumerate(chips)]
        for cp in first:
            cp.start()
        passed = [copy(4 + j, (*chip, c), sibling) for j, chip in enumerate(chips)]
        for j, chip in enumerate(chips):
            copy(1 + j, (*chip, c), me).wait_recv()
            passed[j].start()
        copy(0, sibling, me).wait_recv()
        for j, chip in enumerate(chips):
            copy(4 + j, (*chip, 1 - c), me).wait_recv()
        for cp in first + passed:
            cp.wait_send()
        mine.wait()

    return pl.pallas_call(
        body, name=name, out_shape=jax.ShapeDtypeStruct((N_DEV * m_per, ncol), block.dtype),
        in_specs=[pl.BlockSpec(memory_space=pltpu.VMEM)], out_specs=pl.BlockSpec(memory_space=pltpu.VMEM),
        scratch_shapes=[pltpu.SemaphoreType.DMA((7,)), pltpu.SemaphoreType.DMA((7,)), pltpu.SemaphoreType.DMA],
    )(block)


def _rope_angles(pos, dim):
    inv = ROPE_THETA ** (-jnp.arange(0, dim, 2, dtype=F32) / dim)
    return pos.astype(F32)[:, None] * inv[None, :]


def _rope_tables(s):
    pos = jnp.arange(s, dtype=jnp.int32)
    rows = s // GRID_W
    row = jnp.repeat(jnp.arange(rows, dtype=jnp.int32), GRID_W)
    col = jnp.tile(jnp.arange(GRID_W, dtype=jnp.int32), rows)
    a1 = _rope_angles(pos, HEAD_DIM)
    aa = _rope_angles(pos, A_ROPE)
    ar = _rope_angles(row, HEAD_DIM // 2)
    ac = _rope_angles(col, HEAD_DIM // 2)
    one = jnp.ones((s, LANE), F32)
    zero = jnp.zeros((s, LANE), F32)
    pad = LANE - A_ROPE
    cos_a = jnp.concatenate([one, jnp.cos(aa), jnp.cos(aa), jnp.ones((s, pad), F32)], axis=1)
    sin_a = jnp.concatenate([zero, -jnp.sin(aa), jnp.sin(aa), jnp.zeros((s, pad), F32)], axis=1)
    cos_b = jnp.concatenate([jnp.cos(a1), jnp.cos(a1)], axis=1)
    sin_b = jnp.concatenate([-jnp.sin(a1), jnp.sin(a1)], axis=1)
    cos_c = jnp.concatenate([jnp.cos(ar), jnp.cos(ar), jnp.cos(ac), jnp.cos(ac)], axis=1)
    sin_c = jnp.concatenate([-jnp.sin(ar), jnp.sin(ar), -jnp.sin(ac), jnp.sin(ac)], axis=1)
    return (cos_a, sin_a), (cos_b, sin_b), (cos_c, sin_c)


def _band_table(tq, s):
    reach = max((win // (2 * d)) * d for win, d in B_PATTERNS)
    r = -(-reach // tq) * tq
    w = min(s, tq + 2 * r)
    j = jnp.arange(tq, dtype=jnp.int32)[:, None]
    x = jnp.arange(2 * w - tq, dtype=jnp.int32)[None, :]
    rel = x - (w - tq) - j
    mult = jnp.zeros(rel.shape, F32)
    for win, d in B_PATTERNS:
        mult = mult + jnp.logical_and(rel % d == 0, jnp.abs(rel) <= (win // (2 * d)) * d).astype(F32)
    return jnp.where(mult > 0, jnp.log2(jnp.maximum(mult, 1.0)), NEG), (w, r)


_BIG = ("w_in", "a_w_uq", "a_w_ukv", "w_out", "w_gate", "w_up", "w_down")
_SMALL = ("attn_norm", "a_q_norm", "a_kv_norm", "c_q_norm", "c_k_norm", "out_norm", "ffn_norm", "final_norm")
_WEIGHTS = ("attn_norm", "w_in", "a_q_norm", "a_w_uq", "a_kv_norm", "a_w_ukv", "c_q_norm", "c_k_norm", "out_norm",
            "w_out", "ffn_norm", "w_gate", "w_up", "w_down", "final_norm")


_ATTN = ("w_in", "a_w_uq", "a_w_ukv", "w_out")
_FFN = ("w_gate", "w_up", "w_down")


def _from_cols(a):
    return jnp.transpose(a, (1, 0, 2)).reshape(a.shape[1], N_CHIPS * a.shape[2])


def _from_rows(a):
    return a.reshape(N_CHIPS * a.shape[1], a.shape[2])


def _to_cols(a):
    return jnp.transpose(a.reshape(a.shape[0], N_CHIPS, a.shape[1] // N_CHIPS), (1, 0, 2))


def _to_rows(a):
    return a.reshape(N_CHIPS, a.shape[0] // N_CHIPS, a.shape[1])


def _assemble_attn(gw):
    w_in, uq, ukv, w_out = _from_cols(gw[0]), _from_cols(gw[1]), _from_cols(gw[2]), _from_rows(gw[3])
    d = w_in.shape[0]
    w_all = jnp.concatenate([w_in[:, :IN_A], jnp.zeros((d, A_PAD - IN_A), BF16), w_in[:, IN_A:]], axis=1)
    uq = uq.reshape(A_Q_RANK, A_HEADS, A_NOPE + A_ROPE)
    uq = jnp.pad(uq, ((0, 0), (0, 0), (0, A_QK - A_NOPE - A_ROPE))).reshape(A_Q_RANK, A_HEADS * A_QK)
    return dict(w_all=w_all, uq=uq, ukv=ukv, w_out=w_out)


def _assemble_ffn(gw):
    return dict(w_gate=gw[0], w_up=gw[1], w_down=_from_rows(gw[2]))


def _split_attn_grads(gl):
    w_all = gl["w_all"]
    w_in = jnp.concatenate([w_all[:, :IN_A], w_all[:, A_PAD:]], axis=1)
    uq = gl["uq"].reshape(A_Q_RANK, A_HEADS, A_QK)[:, :, :A_NOPE + A_ROPE].reshape(A_Q_RANK, A_HEADS * (A_NOPE + A_ROPE))
    return [_to_cols(w_in), _to_cols(uq), _to_cols(gl["ukv"]), _to_rows(gl["w_out"])]


def _split_ffn_grads(gl):
    return [gl["w_gate"], gl["w_up"], _to_rows(gl["w_down"])]


def _tie(a, token):
    return a + token[0:1, 0:1]


def _layer_fwd(x, wl, ffn_weights, sm, tabs, bias, t):
    s = x.shape[0]
    (cos_a, sin_a), (cos_b, sin_b), (cos_c, sin_c) = tabs
    h = _norm_fwd(x, sm["attn_norm"], wb=x.shape[1], cb=0, nb=1, shared_gain=True, out_dtype=BF16, name="attn_norm_fwd")
    p = _matmul(h, wl["w_all"], mode="nn", out_dtype=F32, name="in_proj", tm=1024, tn=640)
    cq_n = _norm_fwd(p, sm["a_q_norm"], wb=A_Q_RANK, cb=0, nb=1, shared_gain=True, out_dtype=BF16, name="a_q_norm_fwd")
    ckv_n = _norm_fwd(p, sm["a_kv_norm"], wb=A_KV_RANK, cb=1, nb=1, shared_gain=True, out_dtype=BF16, name="a_kv_norm_fwd")
    qa_raw = _matmul(cq_n, wl["uq"], mode="nn", out_dtype=F32, name="a_uq", tm=1024, tn=1024)
    kv = _matmul(ckv_n, wl["ukv"], mode="nn", out_dtype=BF16, name="a_ukv", tm=1024, tn=1024)
    k_nope = kv.reshape(s, A_HEADS, 2, LANE)[:, :, 0].astype(F32)
    k_rope = jnp.broadcast_to(p[:, PB_KR * LANE:(PB_KR + 1) * LANE][:, None, :], (s, A_HEADS, LANE))
    ka_raw = jnp.stack([k_nope, k_rope], axis=2).reshape(s, A_HEADS * A_QK)
    qa = _rope(qa_raw, cos_a, sin_a, tw=A_QK, cb=0, nb=A_HEADS, half=A_ROPE // 2, sign=1, out_dtype=BF16, name="a_rope_q")
    ka = _rope(ka_raw, cos_a, sin_a, tw=A_QK, cb=0, nb=A_HEADS, half=A_ROPE // 2, sign=1, out_dtype=BF16, name="a_rope_k")
    oa, lse_a = _flash_fwd(qa, ka, kv, None, hkv=A_HEADS, g=1, dqk=A_QK, q_cb=0, k_cb=0, v_cb=1, v_step=2,
                           scale=(A_NOPE + A_ROPE) ** -0.5, tq=t, band=None, name="a_flash_fwd")
    table, band = bias
    qb = _rope(p, cos_b, sin_b, tw=LANE, cb=PB_BQ, nb=B_HEADS, half=HEAD_DIM // 2, sign=1, out_dtype=BF16, name="b_rope_q")
    kb = _rope(p, cos_b, sin_b, tw=LANE, cb=PB_BK, nb=B_HEADS, half=HEAD_DIM // 2, sign=1, out_dtype=BF16, name="b_rope_k")
    vb = _cast_cols(p, cb=PB_BV, nb=B_HEADS, name="b_cast_v")
    ob, lse_b = _flash_fwd(qb, kb, vb, table, hkv=B_HEADS, g=1, dqk=LANE, q_cb=0, k_cb=0, v_cb=0, v_step=1,
                           scale=HEAD_DIM ** -0.5, tq=t, band=band, name="b_flash_fwd")
    qn = _norm_fwd(p, sm["c_q_norm"], wb=LANE, cb=PB_CQH, nb=C_HEADS, shared_gain=True, out_dtype=F32, name="c_q_norm_fwd")
    kn = _norm_fwd(p, sm["c_k_norm"], wb=LANE, cb=PB_CKH, nb=C_KV_HEADS, shared_gain=True, out_dtype=F32, name="c_k_norm_fwd")
    qc = _rope(qn, cos_c, sin_c, tw=LANE, cb=0, nb=C_HEADS, half=HEAD_DIM // 4, sign=1, out_dtype=BF16, name="c_rope_q")
    kc = _rope(kn, cos_c, sin_c, tw=LANE, cb=0, nb=C_KV_HEADS, half=HEAD_DIM // 4, sign=1, out_dtype=BF16, name="c_rope_k")
    vc = _cast_cols(p, cb=PB_CVH, nb=C_KV_HEADS, name="c_cast_v")
    oc, lse_c = _flash_fwd(qc, kc, vc, None, hkv=C_KV_HEADS, g=C_GROUP, dqk=LANE, q_cb=0, k_cb=0, v_cb=0, v_step=1,
                           scale=HEAD_DIM ** -0.5, tq=t, band=None, name="c_flash_fwd")
    g_out = sm["out_norm"]
    ga, gb, gc = g_out[:, :A_WIDTH], g_out[:, A_WIDTH:A_WIDTH + B_WIDTH], g_out[:, A_WIDTH + B_WIDTH:]
    ya = _norm_fwd(oa, ga, wb=A_WIDTH, cb=0, nb=1, shared_gain=True, out_dtype=BF16, name="out_norm_a_fwd")
    yb = _norm_fwd(ob, gb, wb=B_WIDTH, cb=0, nb=1, shared_gain=True, out_dtype=BF16, name="out_norm_b_fwd")
    yc = _norm_fwd(oc, gc, wb=C_WIDTH, cb=0, nb=1, shared_gain=True, out_dtype=BF16, name="out_norm_c_fwd")
    y = jnp.concatenate([ya, yb, yc], axis=1)
    x1 = _matmul(y, wl["w_out"], mode="nn", out_dtype=F32, name="out_proj", add=x, tm=1024, tn=512)
    wl = {**wl, **ffn_weights(x1)}
    h2 = _norm_fwd(x1, sm["ffn_norm"], wb=x.shape[1], cb=0, nb=1, shared_gain=True, out_dtype=BF16, name="ffn_norm_fwd")
    gate, up, act = _ffn_up(h2, wl["w_gate"], wl["w_up"], name="ffn_up")
    x2 =_matmul(act, wl["w_down"], mode="nn", out_dtype=F32, name="ffn_down", add=x1, tm=512, tn=512)
    saved = dict(x=x, h=h, p=p, cq_n=cq_n, ckv_n=ckv_n, kv=kv, qa=qa, ka=ka, oa=oa, lse_a=lse_a, qb=qb, kb=kb, vb=vb, ob=ob,
                 lse_b=lse_b, qc=qc, kc=kc, vc=vc, oc=oc, lse_c=lse_c, y=y, x1=x1, h2=h2, gate=gate, up=up, act=act)
    return x2, saved, wl


def _layer_bwd(dx2, dx2b, sv, wl, sm, tabs, bias, t, send_ffn, send_attn):
    s, d = dx2.shape
    (cos_a, sin_a), (cos_b, sin_b), (cos_c, sin_c) = tabs
    gw, gs = {}, {}
    dgate, dup = _ffn_down_dx(dx2b, wl["w_down"], sv["gate"], sv["up"], name="ffn_down_dx")
    gw["w_down"] = _matmul(sv["act"], dx2b, mode="tn", out_dtype=BF16, name="ffn_down_dw", tm=512, tn=512)
    dh2 = _ffn_up_dx(dgate, dup, wl["w_gate"], wl["w_up"], name="ffn_up_dx")
    gw["w_gate"] = _matmul(sv["h2"], dgate, mode="tn", out_dtype=BF16, name="ffn_gate_dw", tm=512, col_shards=True)
    gw["w_up"] = _matmul(sv["h2"], dup, mode="tn", out_dtype=BF16, name="ffn_up_dw", tm=512, col_shards=True)
    token = send_ffn(gw)
    dx1, gs["ffn_norm"], dx1b = _norm_bwd(sv["x1"], sm["ffn_norm"], dh2, wb=d, cb=0, nb=1, shared_gain=True,
                                          out_dtype=F32, name="ffn_norm_bwd", add=dx2, bf16_copy=True)
    dy = _matmul(dx1b, wl["w_out"], mode="nt", out_dtype=F32, name="out_proj_dx", tm=512, tn=512)
    gw["w_out"] = _matmul(sv["y"], dx1b, mode="tn", out_dtype=BF16, name="out_proj_dw", tm=512, tn=512)
    g_out = _tie(sm["out_norm"], token)
    ga, gb, gc = g_out[:, :A_WIDTH], g_out[:, A_WIDTH:A_WIDTH + B_WIDTH], g_out[:, A_WIDTH + B_WIDTH:]
    dya, dyb, dyc = dy[:, :A_WIDTH], dy[:, A_WIDTH:A_WIDTH + B_WIDTH], dy[:, A_WIDTH + B_WIDTH:]
    doa, dga = _norm_bwd(sv["oa"], ga, dya, wb=A_WIDTH, cb=0, nb=1, shared_gain=True, out_dtype=F32, name="out_norm_a_bwd")
    dob, dgb = _norm_bwd(sv["ob"], gb, dyb, wb=B_WIDTH, cb=0, nb=1, shared_gain=True, out_dtype=F32, name="out_norm_b_bwd")
    doc, dgc = _norm_bwd(sv["oc"], gc, dyc, wb=C_WIDTH, cb=0, nb=1, shared_gain=True, out_dtype=F32, name="out_norm_c_bwd")
    gs["out_norm"] = jnp.concatenate([dga, dgb, dgc], axis=1)
    p = sv["p"]
    dqc, dkc, dvc = _flash_bwd(sv["qc"], sv["kc"], sv["vc"], sv["oc"], doc, sv["lse_c"], None, hkv=C_KV_HEADS,
                               g=C_GROUP, dqk=LANE, q_cb=0, k_cb=0, v_cb=0, v_step=1, scale=HEAD_DIM ** -0.5, tq=t,
                               band=None, name="c_flash_bwd")
    dqn = _rope(dqc, cos_c, sin_c, tw=LANE, cb=0, nb=C_HEADS, half=HEAD_DIM // 4, sign=-1, out_dtype=F32, name="c_rope_q_bwd")
    dkn = _rope(dkc, cos_c, sin_c, tw=LANE, cb=0, nb=C_KV_HEADS, half=HEAD_DIM // 4, sign=-1, out_dtype=F32, name="c_rope_k_bwd")
    dpcq, gs["c_q_norm"] = _norm_bwd(p, sm["c_q_norm"], dqn, wb=LANE, cb=PB_CQH, nb=C_HEADS, shared_gain=True,
                                     out_dtype=BF16, name="c_q_norm_bwd")
    dpck, gs["c_k_norm"] = _norm_bwd(p, sm["c_k_norm"], dkn, wb=LANE, cb=PB_CKH, nb=C_KV_HEADS, shared_gain=True,
                                     out_dtype=BF16, name="c_k_norm_bwd")
    table, band = bias
    dqb, dkb, dvb = _flash_bwd(sv["qb"], sv["kb"], sv["vb"], sv["ob"], dob, sv["lse_b"], table, hkv=B_HEADS, g=1,
                               dqk=LANE, q_cb=0, k_cb=0, v_cb=0, v_step=1, scale=HEAD_DIM ** -0.5, tq=t, band=band,
                               name="b_flash_bwd")
    dpbq = _rope(dqb, cos_b, sin_b, tw=LANE, cb=0, nb=B_HEADS, half=HEAD_DIM // 2, sign=-1, out_dtype=BF16, name="b_rope_q_bwd")
    dpbk = _rope(dkb, cos_b, sin_b, tw=LANE, cb=0, nb=B_HEADS, half=HEAD_DIM // 2, sign=-1, out_dtype=BF16, name="b_rope_k_bwd")
    dqa, dka, dva = _flash_bwd(sv["qa"], sv["ka"], sv["kv"], sv["oa"], doa, sv["lse_a"], None, hkv=A_HEADS, g=1,
                               dqk=A_QK, q_cb=0, k_cb=0, v_cb=1, v_step=2, scale=(A_NOPE + A_ROPE) ** -0.5, tq=t,
                               band=None, name="a_flash_bwd")
    dqa_raw = _rope(dqa, cos_a, sin_a, tw=A_QK, cb=0, nb=A_HEADS, half=A_ROPE // 2, sign=-1, out_dtype=BF16, name="a_rope_q_bwd")
    dka_raw = _rope(dka, cos_a, sin_a, tw=A_QK, cb=0, nb=A_HEADS, half=A_ROPE // 2, sign=-1, out_dtype=BF16, name="a_rope_k_bwd")
    dkr = _group_sum(dka_raw, n_out=1, g=A_HEADS, src=lambda n, j: 2 * j + 1, out_dtype=BF16, name="a_k_rope_sum")
    dkv = jnp.stack([dka_raw.reshape(s, A_HEADS, 2, LANE)[:, :, 0], dva.reshape(s, A_HEADS, LANE).astype(BF16)], axis=2)
    dkv = dkv.reshape(s, A_HEADS * 2 * LANE)
    dckv_n = _matmul(dkv, wl["ukv"], mode="nt", out_dtype=F32, name="a_ukv_dx", tm=1024, tn=512)
    gw["ukv"] = _matmul(sv["ckv_n"], dkv, mode="tn", out_dtype=BF16, name="a_ukv_dw", tm=512, tn=1024)
    dcq_n = _matmul(dqa_raw, wl["uq"], mode="nt", out_dtype=F32, name="a_uq_dx", tm=1024, tn=512)
    gw["uq"] = _matmul(sv["cq_n"], dqa_raw, mode="tn", out_dtype=BF16, name="a_uq_dw", tm=512, tn=1024)
    dcq, gs["a_q_norm"] = _norm_bwd(p, sm["a_q_norm"], dcq_n, wb=A_Q_RANK, cb=0, nb=1, shared_gain=True, out_dtype=BF16,
                                    name="a_q_norm_bwd")
    dckv, gs["a_kv_norm"] = _norm_bwd(p, sm["a_kv_norm"], dckv_n, wb=A_KV_RANK, cb=1, nb=1, shared_gain=True,
                                      out_dtype=BF16, name="a_kv_norm_bwd")
    dp = jnp.concatenate([dcq, dckv, dkr, jnp.zeros((s, A_PAD - (PB_KR + 1) * LANE), BF16), dpbq, dpbk,
                          dvb.astype(BF16), dpcq, dpck, dvc.astype(BF16)], axis=1)
    dh = _matmul(dp, wl["w_all"], mode="nt", out_dtype=F32, name="in_proj_dx", tm=512, tn=512)
    gw["w_all"] = _matmul(sv["h"], dp, mode="tn", out_dtype=BF16, name="in_proj_dw", tm=512, tn=640)
    token = send_attn(gw)
    dx, gs["attn_norm"], dxb = _norm_bwd(sv["x"], _tie(sm["attn_norm"], token), dh, wb=d, cb=0, nb=1, shared_gain=True,
                                         out_dtype=F32, name="attn_norm_bwd", add=dx1, bf16_copy=True)
    return dx, dxb, gs, token


def _pack_small(vals):
    flat = jnp.concatenate([vals[n].reshape(-1).astype(F32) for n in _SMALL])
    tile = SUBLANE * LANE
    padded = -(-flat.shape[0] // tile) * tile
    return jnp.pad(flat, (0, padded - flat.shape[0])).reshape(padded // LANE, LANE)


def _unpack_small(packed, like):
    flat = packed.reshape(-1)
    out, off = {}, 0
    for n in _SMALL:
        size = math.prod(like[n].shape)
        out[n] = flat[off:off + size].reshape(like[n].shape)
        off += size
    return out


def kernel(x, attn_norm, w_in, a_q_norm, a_w_uq, a_kv_norm, a_w_ukv, c_q_norm, c_k_norm, out_norm, w_out, ffn_norm, w_gate, w_up, w_down, final_norm, loss_target, m_attn_norm, m_w_in, m_a_q_norm, m_a_w_uq, m_a_kv_norm, m_a_w_ukv, m_c_q_norm, m_c_k_norm, m_out_norm, m_w_out, m_ffn_norm, m_w_gate, m_w_up, m_w_down, m_final_norm, v_attn_norm, v_w_in, v_a_q_norm, v_a_w_uq, v_a_kv_norm, v_a_w_ukv, v_c_q_norm, v_c_k_norm, v_out_norm, v_w_out, v_ffn_norm, v_w_gate, v_w_up, v_w_down, v_final_norm):
    w = dict(attn_norm=attn_norm, w_in=w_in, a_q_norm=a_q_norm, a_w_uq=a_w_uq, a_kv_norm=a_kv_norm, a_w_ukv=a_w_ukv,
             c_q_norm=c_q_norm, c_k_norm=c_k_norm, out_norm=out_norm, w_out=w_out, ffn_norm=ffn_norm, w_gate=w_gate,
             w_up=w_up, w_down=w_down, final_norm=final_norm)
    m = dict(attn_norm=m_attn_norm, w_in=m_w_in, a_q_norm=m_a_q_norm, a_w_uq=m_a_w_uq, a_kv_norm=m_a_kv_norm,
             a_w_ukv=m_a_w_ukv, c_q_norm=m_c_q_norm, c_k_norm=m_c_k_norm, out_norm=m_out_norm, w_out=m_w_out,
             ffn_norm=m_ffn_norm, w_gate=m_w_gate, w_up=m_w_up, w_down=m_w_down, final_norm=m_final_norm)
    v = dict(attn_norm=v_attn_norm, w_in=v_w_in, a_q_norm=v_a_q_norm, a_w_uq=v_a_w_uq, a_kv_norm=v_a_kv_norm,
             a_w_ukv=v_a_w_ukv, c_q_norm=v_c_q_norm, c_k_norm=v_c_k_norm, out_norm=v_out_norm, w_out=v_w_out,
             ffn_norm=v_ffn_norm, w_gate=v_w_gate, w_up=v_w_up, w_down=v_w_down, final_norm=v_final_norm)
    _, s, d = x.shape
    depth = attn_norm.shape[0]
    t = _pick(s, 512)

    me = (2 * lax.axis_index("x") + lax.axis_index("y")).astype(jnp.int32).reshape(1)

    gathers, after = {}, me
    for l in range(depth):
        for group, names in (("attn", _ATTN), ("ffn", _FFN)):
            bufs = [_cast_to_slot(w[n].reshape(-1, w[n].shape[-1]), me, layer=l, rows=w[n].shape[1], name=f"cast_{n}")
                    for n in names]
            send_sems, recv_sems, bufs, _, after = _exchange_start(bufs, None, after, name=f"gather_start_{group}{l}")
            gathers[group, l] = (send_sems, recv_sems, bufs)
    all_started = after

    def gathered(group, l, after):
        send_sems, recv_sems, bufs = gathers[group, l]
        return _exchange_wait(send_sems, recv_sems, bufs, None, after, name=f"gather_wait_{group}{l}")

    tabs = _rope_tables(s)
    bias = _band_table(t, s)

    xs = x.reshape(s, d)
    saved, wls, sms = [], [], []
    for l in range(depth):
        wl = _assemble_attn(gathered("attn", l, all_started if l == 0 else xs))
        sm = {n: w[n][l][None, :] for n in _SMALL if n != "final_norm"}
        xs, sv, wl = _layer_fwd(xs, wl, lambda after, l=l: _assemble_ffn(gathered("ffn", l, after)), sm, tabs, bias, t)
        saved.append(sv)
        wls.append(wl)
        sms.append(sm)
    dx, g_final, loss_row, dxb = _final_loss(xs, final_norm[None, :], loss_target.reshape(s, d), name="final_loss")
    loss = lax.psum(loss_row[0, 0], ("x", "y", "c"))

    sends = {}

    def send(group, l, srcs, after):
        lands = [lax.empty((3,) + a.shape[1:], BF16) for a in srcs]
        send_sems, recv_sems, srcs, lands, token = _exchange_start(srcs, lands, after, name=f"scatter_start_{group}{l}")
        sends[group, l] = (send_sems, recv_sems, srcs, lands)
        return token

    gs_layers, token = [None] * depth, all_started
    for l in reversed(range(depth)):
        dx, dxb, gs_layers[l], token = _layer_bwd(
            dx, dxb, saved[l], wls[l], sms[l], tabs, bias, t,
            lambda gw, l=l, tk=token: send("ffn", l, _split_ffn_grads(gw), tk),
            lambda gw, l=l: send("attn", l, _split_attn_grads(gw), dx))
    grad_x = dx.reshape(x.shape)

    srcs, lands = {}, {}
    for (group, l), (send_sems, recv_sems, s_bufs, l_bufs) in sends.items():
        got = _exchange_wait(send_sems, recv_sems, s_bufs, l_bufs, token, name=f"scatter_wait_{group}{l}")
        for k, n in enumerate(_ATTN if group == "attn" else _FFN):
            srcs[n, l], lands[n, l] = got[k], got[len(s_bufs) + k]
    sums = [_sum_parts([srcs[n, l] for l in range(depth)], [lands[n, l] for l in range(depth)], me, name="sum_" + n)
            for n in _BIG]
    sib = _sibling_exchange(sums, name="swap_core_sums")
    grads, deltas, new_m, new_v = {}, {}, {}, {}
    for n, mine, other in zip(_BIG, sums, sib):
        shp = w[n].shape
        two_d = (-1, shp[-1])
        res = _adamw(mine, other, w[n].reshape(two_d), m[n].reshape(two_d), v[n].reshape(two_d), name="adamw_" + n)
        grads[n], deltas[n], new_m[n], new_v[n] = [r.reshape(shp) for r in res]

    gsm = {n: jnp.stack([gs_layers[l][n][0] for l in range(depth)]) for n in _SMALL if n != "final_norm"}
    gsm["final_norm"] = g_final[0]
    packed = _pack_small(gsm)
    everyone = _all_gather_small(packed, name="gather_gain_grads").reshape(N_DEV, packed.shape[0], LANE)
    res = _small_adamw(everyone, _pack_small(w), _pack_small(m), _pack_small(v), name="adamw_gains")
    for dst, r in zip((grads, deltas, new_m, new_v), res):
        dst.update(_unpack_small(r, w))

    return (loss, grad_x, *[grads[n] for n in _WEIGHTS], *[deltas[n] for n in _WEIGHTS],
            *[new_m[n] for n in _WEIGHTS], *[new_v[n] for n in _WEIGHTS])
```

```python
import functools
import math

import jax
import jax.numpy as jnp
import numpy as np
from jax import lax
from jax.experimental import pallas as pl
from jax.experimental.pallas import tpu as pltpu

F32 = jnp.float32
BF16 = jnp.bfloat16
MESH = pl.DeviceIdType.MESH

HEAD_DIM = 128
ROPE_THETA = 10000.0
GRID_W = 64
EPS = 1e-6
NEG = -1e30
A_HEADS, A_Q_RANK, A_KV_RANK, A_NOPE, A_ROPE, A_V = 4, 512, 512, 128, 64, 128
B_HEADS = 6
B_PATTERNS = ((128, 1), (512, 4), (2048, 16))
C_HEADS, C_KV_HEADS = 6, 2
C_GROUP = C_HEADS // C_KV_HEADS
A_WIDTH, B_WIDTH, C_WIDTH = A_HEADS * A_V, B_HEADS * HEAD_DIM, C_HEADS * HEAD_DIM
IN_A = A_Q_RANK + A_KV_RANK + A_ROPE
IN_B = 3 * B_WIDTH
IN_C = C_WIDTH + 2 * C_KV_HEADS * HEAD_DIM
ADAM_LR, ADAM_B1, ADAM_B2, ADAM_EPS, ADAM_WD, ADAM_STEP = 0.001, 0.9, 0.999, 1e-08, 0.01, 10

LANE = 128
SUBLANE = 8
VMEM_BYTES_V7X = 64 * 1024 * 1024
VMEM_LIMIT_CAP = VMEM_BYTES_V7X - 8 * 1024 * 1024
N_CHIPS = 4
N_DEV = 8

A_PAD = 12 * LANE
PB_CQ, PB_CKV, PB_KR = 0, 4, 8
PB_BQ, PB_BK, PB_BV = 12, 18, 24
PB_CQH, PB_CKH, PB_CVH = 30, 36, 38
NP = 40 * LANE
A_QK = 2 * LANE


def _pick(n, cap, mult=LANE):
    if n <= cap:
        return n
    t = cap - cap % mult
    while t >= mult:
        if n % t == 0:
            return t
        t -= mult
    return n


def _rows_for(width_bytes, n_rows, target=2 * 1024 * 1024):
    return _pick(n_rows, max(SUBLANE, target // max(width_bytes, 1)), SUBLANE)


def _params(est_bytes):
    limit = int(min(max(est_bytes + (4 << 20), 32 << 20), VMEM_LIMIT_CAP))
    return pltpu.CompilerParams(vmem_limit_bytes=limit)


def _isz(x):
    return jnp.dtype(x.dtype).itemsize


_DIMS = {"nn": (((1,), (0,)), ((), ())), "nt": (((1,), (1,)), ((), ())), "tn": (((0,), (0,)), ((), ()))}


def _matmul(a, b, *, mode, out_dtype, name, add=None, tm=512, tn=512, col_shards=False, after=None):
    if mode == "tn":
        (k, m), (k2, n) = a.shape, b.shape
    elif mode == "nt":
        (m, k), (n, k2) = a.shape, b.shape
    else:
        (m, k), (k2, n) = a.shape, b.shape
    assert k == k2, (a.shape, b.shape, mode)
    tm, tn = _pick(m, tm), (n // N_CHIPS if col_shards else _pick(n, tn))
    a_spec = pl.BlockSpec((k, tm), lambda i, j: (0, i)) if mode == "tn" else pl.BlockSpec((tm, k), lambda i, j: (i, 0))
    b_spec = pl.BlockSpec((tn, k), lambda i, j: (j, 0)) if mode == "nt" else pl.BlockSpec((k, tn), lambda i, j: (0, j))
    o_spec = pl.BlockSpec((None, tm, tn), lambda i, j: (j, i, 0)) if col_shards else pl.BlockSpec((tm, tn), lambda i, j: (i, j))
    dims = _DIMS[mode]

    def body(*refs):
        a_ref, b_ref, o_ref = refs[0], refs[1], refs[-1]
        acc = lax.dot_general(a_ref[...].astype(BF16), b_ref[...].astype(BF16), dims, preferred_element_type=F32)
        if add is not None:
            acc = acc + refs[2][...].astype(F32)
        o_ref[...] = acc.astype(out_dtype)

    ins, specs = [a, b], [a_spec, b_spec]
    if add is not None:
        ins.append(add)
        specs.append(o_spec)
    if after is not None:
        ins.append(after)
        specs.append(pl.BlockSpec(memory_space=pl.ANY))
    est = 2 * (tm * k * _isz(a) + tn * k * _isz(b) + tm * tn * (jnp.dtype(out_dtype).itemsize + (4 if add is not None else 0)))
    est += (tm + tn) * k * 2 + 2 * tm * tn * 4
    return pl.pallas_call(
        body, name=name, grid=(m // tm, n // tn), in_specs=specs, out_specs=o_spec,
        out_shape=jax.ShapeDtypeStruct((N_CHIPS, m, tn) if col_shards else (m, n), out_dtype),
        compiler_params=_params(est),
    )(*ins)


def _ffn_up(h, wg, wu, *, name):
    s, d = h.shape
    _, _, c = wg.shape
    tm = _pick(s, 512, SUBLANE)

    def body(h_ref, wg_ref, wu_ref, g_ref, u_ref, a_ref):
        hv = h_ref[...]
        gv = jnp.dot(hv, wg_ref[...], preferred_element_type=F32)
        uv = jnp.dot(hv, wu_ref[...], preferred_element_type=F32)
        g_ref[...] = gv
        u_ref[...] = uv
        a_ref[...] = (gv / (1.0 + jnp.exp(-gv)) * uv).astype(BF16)

    w_spec = pl.BlockSpec((None, d, c), lambda j, i: (j, 0, 0))
    o_spec = pl.BlockSpec((tm, c), lambda j, i: (i, j))
    est = 2 * (tm * d * 2 + 2 * d * c * 2 + tm * c * 10) + 4 * tm * c * 4
    return pl.pallas_call(
        body, name=name, grid=(N_CHIPS, s // tm), in_specs=[pl.BlockSpec((tm, d), lambda j, i: (i, 0)), w_spec, w_spec],
        out_specs=[o_spec, o_spec, o_spec],
        out_shape=[jax.ShapeDtypeStruct((s, N_CHIPS * c), F32), jax.ShapeDtypeStruct((s, N_CHIPS * c), F32),
                   jax.ShapeDtypeStruct((s, N_CHIPS * c), BF16)],
        compiler_params=_params(est),
    )(h, wg, wu)


def _ffn_down_dx(dx, w_down, gate, up, *, name):
    s, d = dx.shape
    f = w_down.shape[0]
    tm, tn = _pick(s, 512, SUBLANE), _pick(f, 512)

    def body(dx_ref, w_ref, g_ref, u_ref, dg_ref, du_ref):
        dact = lax.dot_general(dx_ref[...], w_ref[...], _DIMS["nt"], preferred_element_type=F32)
        gv, uv = g_ref[...], u_ref[...]
        sig = 1.0 / (1.0 + jnp.exp(-gv))
        dg_ref[...] = (dact * uv * (sig * (1.0 + gv * (1.0 - sig)))).astype(BF16)
        du_ref[...] = (dact * (gv * sig)).astype(BF16)

    t_spec = pl.BlockSpec((tm, tn), lambda i, j: (i, j))
    est = 2 * (tm * d * 2 + tn * d * 2 + tm * tn * 12) + 6 * tm * tn * 4
    return pl.pallas_call(
        body, name=name, grid=(s // tm, f // tn),
        in_specs=[pl.BlockSpec((tm, d), lambda i, j: (i, 0)), pl.BlockSpec((tn, d), lambda i, j: (j, 0)), t_spec, t_spec],
        out_specs=[t_spec, t_spec], out_shape=[jax.ShapeDtypeStruct((s, f), BF16)] * 2, compiler_params=_params(est),
    )(dx, w_down, gate, up)


def _ffn_up_dx(dgate, dup, wg, wu, *, name):
    s, f = dgate.shape
    _, d, c = wg.shape
    tm, tn = _pick(s, 1024, SUBLANE), _pick(d, 1024)
    nk = 2 * N_CHIPS

    def body(dg_ref, du_ref, wg_ref, wu_ref, o_ref, acc):
        kk = pl.program_id(2)

        @pl.when(kk == 0)
        def _():
            acc[...] = jnp.zeros_like(acc)

        @pl.when(kk < N_CHIPS)
        def _():
            acc[...] += lax.dot_general(dg_ref[...], wg_ref[...], _DIMS["nt"], preferred_element_type=F32)

        @pl.when(kk >= N_CHIPS)
        def _():
            acc[...] += lax.dot_general(du_ref[...], wu_ref[...], _DIMS["nt"], preferred_element_type=F32)

        @pl.when(kk == nk - 1)
        def _():
            o_ref[...] = acc[...]

    last = N_CHIPS - 1
    est = 2 * (2 * tm * c * 2 + 2 * tn * c * 2 + tm * tn * 4) + 2 * tm * tn * 4
    return pl.pallas_call(
        body, name=name, grid=(s // tm, d // tn, nk),
        in_specs=[pl.BlockSpec((tm, c), lambda i, j, kk: (i, jnp.minimum(kk, last))),
                  pl.BlockSpec((tm, c), lambda i, j, kk: (i, jnp.maximum(kk - N_CHIPS, 0))),
                  pl.BlockSpec((None, tn, c), lambda i, j, kk: (jnp.minimum(kk, last), j, 0)),
                  pl.BlockSpec((None, tn, c), lambda i, j, kk: (jnp.maximum(kk - N_CHIPS, 0), j, 0))],
        out_specs=pl.BlockSpec((tm, tn), lambda i, j, kk: (i, j)),
        out_shape=jax.ShapeDtypeStruct((s, d), F32), scratch_shapes=[pltpu.VMEM((tm, tn), F32)],
        compiler_params=_params(est),
    )(dgate, dup, wg, wu)


def _norm_fwd(x, gain, *, wb, cb, nb, shared_gain, out_dtype, name):
    s = x.shape[0]
    ts = _rows_for(wb * 4, s)

    def body(x_ref, g_ref, o_ref):
        xv = x_ref[...].astype(F32)
        r = lax.rsqrt(jnp.mean(xv * xv, axis=1, keepdims=True) + EPS)
        o_ref[...] = ((xv * r) * g_ref[...]).astype(out_dtype)

    return pl.pallas_call(
        body, name=name, grid=(nb, s // ts),
        in_specs=[pl.BlockSpec((ts, wb), lambda n, i: (i, cb + n)),
                  pl.BlockSpec((1, wb), (lambda n, i: (0, 0)) if shared_gain else (lambda n, i: (0, n)))],
        out_specs=pl.BlockSpec((ts, wb), lambda n, i: (i, n)),
        out_shape=jax.ShapeDtypeStruct((s, nb * wb), out_dtype), compiler_params=_params(6 * ts * wb * 4),
    )(x, gain)


def _norm_bwd(x, gain, dy, *, wb, cb, nb, shared_gain, out_dtype, name, dy_cb=0, add=None, bf16_copy=False):
    s = x.shape[0]
    ts = _rows_for(wb * 4, s, target=1024 * 1024)
    gw = wb if shared_gain else nb * wb

    def body(*refs):
        refs = list(refs)
        dxb_ref = refs.pop() if bf16_copy else None
        if add is None:
            x_ref, g_ref, dy_ref, dx_ref, dg_ref = refs
        else:
            x_ref, g_ref, dy_ref, add_ref, dx_ref, dg_ref = refs
        n, i = pl.program_id(0), pl.program_id(1)
        xv = x_ref[...].astype(F32)
        dyv = dy_ref[...].astype(F32)
        r = lax.rsqrt(jnp.mean(xv * xv, axis=1, keepdims=True) + EPS)
        xh = xv * r
        dyg = dyv * g_ref[...]
        dx = r * (dyg - xh * jnp.mean(dyg * xh, axis=1, keepdims=True))
        if add is not None:
            dx = dx + add_ref[...]
        dx_ref[...] = dx.astype(out_dtype)
        if bf16_copy:
            dxb_ref[...] = dx.astype(BF16)
        first = jnp.logical_and(n == 0, i == 0) if shared_gain else (i == 0)

        @pl.when(first)
        def _():
            dg_ref[...] = jnp.zeros_like(dg_ref)

        dg_ref[...] += jnp.sum(dyv * xh, axis=0, keepdims=True)

    ins = [x, gain, dy]
    specs = [pl.BlockSpec((ts, wb), lambda n, i: (i, cb + n)),
             pl.BlockSpec((1, wb), (lambda n, i: (0, 0)) if shared_gain else (lambda n, i: (0, n))),
             pl.BlockSpec((ts, wb), lambda n, i: (i, dy_cb + n))]
    if add is not None:
        ins.append(add)
        specs.append(pl.BlockSpec((ts, wb), lambda n, i: (i, n)))
    out_specs = [pl.BlockSpec((ts, wb), lambda n, i: (i, n)),
                 pl.BlockSpec((1, wb), (lambda n, i: (0, 0)) if shared_gain else (lambda n, i: (0, n)))]
    out_shape = [jax.ShapeDtypeStruct((s, nb * wb), out_dtype), jax.ShapeDtypeStruct((1, gw), F32)]
    if bf16_copy:
        out_specs.append(out_specs[0])
        out_shape.append(jax.ShapeDtypeStruct((s, nb * wb), BF16))
    return pl.pallas_call(
        body, name=name, grid=(nb, s // ts), in_specs=specs, out_specs=out_specs, out_shape=out_shape,
        compiler_params=_params(14 * ts * wb * 4),
    )(*ins)


def _swap_halves(x, half):
    if 2 * half == LANE:
        return pltpu.roll(x, half, axis=1)
    lane = lax.broadcasted_iota(jnp.int32, x.shape, 1)
    first = jnp.bitwise_and(lane, 2 * half - 1) < half
    return jnp.where(first, pltpu.roll(x, LANE - half, axis=1), pltpu.roll(x, half, axis=1))


def _rope(x, cos_t, sin_t, *, tw, cb, nb, half, sign, out_dtype, name):
    s = x.shape[0]
    ts = _rows_for(tw * 4, s)

    def body(x_ref, c_ref, s_ref, o_ref):
        for q in range(tw // LANE):
            sl = slice(q * LANE, (q + 1) * LANE)
            xv = x_ref[:, sl].astype(F32)
            sv = s_ref[:, sl]
            if sign < 0:
                sv = -sv
            o_ref[:, sl] = (xv * c_ref[:, sl] + _swap_halves(xv, half) * sv).astype(out_dtype)

    return pl.pallas_call(
        body, name=name, grid=(nb, s // ts),
        in_specs=[pl.BlockSpec((ts, tw), lambda n, i: (i, cb + n)),
                  pl.BlockSpec((ts, tw), lambda n, i: (i, 0)),
                  pl.BlockSpec((ts, tw), lambda n, i: (i, 0))],
        out_specs=pl.BlockSpec((ts, tw), lambda n, i: (i, n)),
        out_shape=jax.ShapeDtypeStruct((s, nb * tw), out_dtype), compiler_params=_params(10 * ts * tw * 4),
    )(x, cos_t, sin_t)


def _cast_cols(x, *, cb, nb, name):
    s = x.shape[0]
    ts = _rows_for(LANE * 4, s)

    def body(x_ref, o_ref):
        o_ref[...] = x_ref[...].astype(BF16)

    return pl.pallas_call(
        body, name=name, grid=(nb, s // ts), in_specs=[pl.BlockSpec((ts, LANE), lambda n, i: (i, cb + n))],
        out_specs=pl.BlockSpec((ts, LANE), lambda n, i: (i, n)),
        out_shape=jax.ShapeDtypeStruct((s, nb * LANE), BF16), compiler_params=_params(4 * ts * LANE * 4),
    )(x)


def _group_sum(x, *, n_out, g, src, out_dtype, name):
    s = x.shape[0]
    ts = _rows_for(LANE * 4, s)

    def body(*refs):
        acc = refs[0][...].astype(F32)
        for r in refs[1:-1]:
            acc = acc + r[...].astype(F32)
        refs[-1][...] = acc.astype(out_dtype)

    return pl.pallas_call(
        body, name=name, grid=(n_out, s // ts),
        in_specs=[pl.BlockSpec((ts, LANE), functools.partial(lambda n, i, j: (i, src(n, j)), j=j)) for j in range(g)],
        out_specs=pl.BlockSpec((ts, LANE), lambda n, i: (i, n)),
        out_shape=jax.ShapeDtypeStruct((s, n_out * LANE), out_dtype), compiler_params=_params(4 * g * ts * LANE * 4),
    )(*([x] * g))


LOG2E = 1.4426950408889634
ATTN_ROW_CHUNK = 256


def _attn_window(i, tq, s, band):
    w, r = band
    start = jnp.clip(i * tq - r, 0, s - w)
    return pl.multiple_of(start, tq), pl.multiple_of((w - tq) - (i * tq - start), LANE)


def _flash_fwd(q, k, v, table, *, hkv, g, dqk, q_cb, k_cb, v_cb, v_step, scale, tq, band, name):
    s = q.shape[0]
    n = s // tq
    hq = hkv * g
    rc = min(tq, ATTN_ROW_CHUNK)
    w = s if band is None else band[0]

    def body(*refs):
        if band is None:
            q_ref, k_ref, v_ref, o_ref, lse_ref = refs
            kw, vw = k_ref[...], v_ref[...]
        else:
            q_ref, k_ref, v_ref, t_ref, o_ref, lse_ref = refs
            start, u = _attn_window(pl.program_id(1), tq, s, band)
            kw, vw = k_ref[pl.ds(start, w), :], v_ref[pl.ds(start, w), :]
        for c in range(tq // rc):
            rows = slice(c * rc, (c + 1) * rc)
            sc = lax.dot_general(q_ref[rows, :], kw, _DIMS["nt"], preferred_element_type=F32) * (scale * LOG2E)
            if band is not None:
                sc = sc + t_ref[rows, pl.ds(u, w)]
            m = jnp.max(sc, axis=1, keepdims=True)
            p = jnp.exp2(sc - m)
            l = jnp.sum(p, axis=1, keepdims=True)
            o_ref[rows, :] = jnp.dot(p.astype(BF16), vw, preferred_element_type=F32) / l
            lse_ref[0, rows, :] = jnp.broadcast_to(m + jnp.log2(l), (rc, LANE))

    ins = [q, k, v]
    specs = [pl.BlockSpec((tq, dqk), lambda h, i: (i, q_cb + h)),
             pl.BlockSpec((s, dqk), lambda h, i: (0, k_cb + h // g)),
             pl.BlockSpec((s, LANE), lambda h, i: (0, v_cb + v_step * (h // g)))]
    if band is not None:
        ins.append(table)
        specs.append(pl.BlockSpec(table.shape, lambda h, i: (0, 0)))
    est = 4 * s * (dqk + LANE) + 6 * rc * w * 4 + 8 * tq * LANE * 4 + (0 if band is None else 2 * table.size * 4)
    return pl.pallas_call(
        body, name=name, grid=(hq, n), in_specs=specs,
        out_specs=[pl.BlockSpec((tq, LANE), lambda h, i: (i, h)), pl.BlockSpec((1, tq, LANE), lambda h, i: (h, i, 0))],
        out_shape=[jax.ShapeDtypeStruct((s, hq * LANE), F32), jax.ShapeDtypeStruct((hq, s, LANE), F32)],
        compiler_params=_params(est),
    )(*ins)


def _flash_bwd(q, k, v, o, do, lse, table, *, hkv, g, dqk, q_cb, k_cb, v_cb, v_step, scale, tq, band, name):
    s = q.shape[0]
    n = s // tq
    hq = hkv * g
    rc = min(tq, ATTN_ROW_CHUNK)
    w = s if band is None else band[0]

    def body(*refs):
        if band is None:
            q_ref, k_ref, v_ref, o_ref, do_ref, lse_ref, dq_ref, dk_ref, dv_ref = refs
            keys = slice(None)
        else:
            q_ref, k_ref, v_ref, o_ref, do_ref, lse_ref, t_ref, dq_ref, dk_ref, dv_ref = refs
            start, u = _attn_window(pl.program_id(1), tq, s, band)
            keys = pl.ds(start, w)
        h, i = pl.program_id(0), pl.program_id(1)

        @pl.when(jnp.logical_and(h % g == 0, i == 0))
        def _():
            dk_ref[...] = jnp.zeros_like(dk_ref)
            dv_ref[...] = jnp.zeros_like(dv_ref)

        kw, vw = k_ref[keys, :], v_ref[keys, :]
        for c in range(tq // rc):
            rows = slice(c * rc, (c + 1) * rc)
            qv = q_ref[rows, :]
            dof = do_ref[rows, :]
            dov = dof.astype(BF16)
            sc = lax.dot_general(qv, kw, _DIMS["nt"], preferred_element_type=F32) * (scale * LOG2E)
            if band is not None:
                sc = sc + t_ref[rows, pl.ds(u, w)]
            p = jnp.exp2(sc - lse_ref[0, rows, 0:1])
            dp = lax.dot_general(dov, vw, _DIMS["nt"], preferred_element_type=F32)
            delta = jnp.sum(dof * o_ref[rows, :], axis=1, keepdims=True)
            ds = (p * (dp - delta) * scale).astype(BF16)
            dv_ref[keys, :] += lax.dot_general(p.astype(BF16), dov, _DIMS["tn"], preferred_element_type=F32)
            dk_ref[keys, :] += lax.dot_general(ds, qv, _DIMS["tn"], preferred_element_type=F32)
            dq_ref[rows, :] = jnp.dot(ds, kw, preferred_element_type=F32)

    ins = [q, k, v, o, do, lse]
    specs = [pl.BlockSpec((tq, dqk), lambda h, i: (i, q_cb + h)),
             pl.BlockSpec((s, dqk), lambda h, i: (0, k_cb + h // g)),
             pl.BlockSpec((s, LANE), lambda h, i: (0, v_cb + v_step * (h // g))),
             pl.BlockSpec((tq, LANE), lambda h, i: (i, h)),
             pl.BlockSpec((tq, LANE), lambda h, i: (i, h)),
             pl.BlockSpec((1, tq, LANE), lambda h, i: (h, i, 0))]
    if band is not None:
        ins.append(table)
        specs.append(pl.BlockSpec(table.shape, lambda h, i: (0, 0)))
    est = (4 + 8) * s * (dqk + LANE) + 10 * rc * w * 4 + 12 * tq * LANE * 4 + (0 if band is None else 2 * table.size * 4)
    return pl.pallas_call(
        body, name=name, grid=(hq, n), in_specs=specs,
        out_specs=[pl.BlockSpec((tq, dqk), lambda h, i: (i, h)),
                   pl.BlockSpec((s, dqk), lambda h, i: (0, h // g)),
                   pl.BlockSpec((s, LANE), lambda h, i: (0, h // g))],
        out_shape=[jax.ShapeDtypeStruct((s, hq * dqk), F32), jax.ShapeDtypeStruct((s, hkv * dqk), F32),
                   jax.ShapeDtypeStruct((s, hkv * LANE), F32)],
        compiler_params=_params(est),
    )(*ins)


def _final_loss(x, gain, target, *, name):
    s, d = x.shape
    ts = _rows_for(d * 4, s, target=1024 * 1024)

    def body(x_ref, g_ref, t_ref, dx_ref, dg_ref, loss_ref, dxb_ref):
        i = pl.program_id(0)
        xv = x_ref[...]
        gv = g_ref[...]
        r = lax.rsqrt(jnp.mean(xv * xv, axis=1, keepdims=True) + EPS)
        xh = xv * r
        err = xh * gv - t_ref[...]
        dy = err / d
        dyg = dy * gv
        dx = r * (dyg - xh * jnp.mean(dyg * xh, axis=1, keepdims=True))
        dx_ref[...] = dx
        dxb_ref[...] = dx.astype(BF16)

        @pl.when(i == 0)
        def _():
            dg_ref[...] = jnp.zeros_like(dg_ref)
            loss_ref[...] = jnp.zeros_like(loss_ref)

        dg_ref[...] += jnp.sum(dy * xh, axis=0, keepdims=True)
        part = jnp.sum(jnp.mean(err * err, axis=1, keepdims=True), axis=0, keepdims=True)
        loss_ref[...] += jnp.broadcast_to(0.5 * part, (1, LANE))

    row = pl.BlockSpec((ts, d), lambda i: (i, 0))
    return pl.pallas_call(
        body, name=name, grid=(s // ts,),
        in_specs=[row, pl.BlockSpec((1, d), lambda i: (0, 0)), row],
        out_specs=[row, pl.BlockSpec((1, d), lambda i: (0, 0)), pl.BlockSpec((1, LANE), lambda i: (0, 0)), row],
        out_shape=[jax.ShapeDtypeStruct((s, d), F32), jax.ShapeDtypeStruct((1, d), F32),
                   jax.ShapeDtypeStruct((1, LANE), F32), jax.ShapeDtypeStruct((s, d), BF16)],
        compiler_params=_params(14 * ts * d * 4),
    )(x, gain, target)


def _cast_to_slot(x2d, me, *, layer, rows, name):
    c = x2d.shape[1]
    tr = _rows_for(c * 4, rows)
    nt = rows // tr

    def body(me_ref, x_ref, o_ref):
        o_ref[...] = x_ref[...].astype(BF16)

    return pl.pallas_call(
        body, name=name,
        grid_spec=pltpu.PrefetchScalarGridSpec(
            num_scalar_prefetch=1, grid=(nt,),
            in_specs=[pl.BlockSpec((tr, c), lambda i, me_ref: (layer * nt + i, 0))],
            out_specs=pl.BlockSpec((None, tr, c), lambda i, me_ref: (me_ref[0], i, 0))),
        out_shape=jax.ShapeDtypeStruct((N_CHIPS, rows, c), BF16), compiler_params=_params(6 * tr * c * 4),
    )(me, x2d)


def _sum_parts(srcs, lands, me, *, name):
    depth = len(srcs)
    _, r, c = srcs[0].shape
    tr = _rows_for(c * 4, r, target=1024 * 1024)
    nt = r // tr

    def body(me_ref, *refs):
        o_ref = refs[-1]
        l = pl.program_id(0)
        for k in range(depth):
            @pl.when(l == k)
            def _(k=k):
                acc = refs[k][...].astype(F32)
                for p in range(3):
                    acc = acc + refs[depth + k][p].astype(F32)
                o_ref[...] = acc

    def rows_of(k):
        return lambda l, i, me_ref: jnp.where(l == k, i, jnp.where(l < k, 0, nt - 1))

    in_specs = [pl.BlockSpec((None, tr, c), functools.partial(lambda l, i, me_ref, f: (me_ref[0], f(l, i, me_ref), 0), f=rows_of(k)))
                for k in range(depth)]
    in_specs += [pl.BlockSpec((3, tr, c), functools.partial(lambda l, i, me_ref, f: (0, f(l, i, me_ref), 0), f=rows_of(k)))
                 for k in range(depth)]
    return pl.pallas_call(
        body, name=name,
        grid_spec=pltpu.PrefetchScalarGridSpec(
            num_scalar_prefetch=1, grid=(depth, nt), in_specs=in_specs,
            out_specs=pl.BlockSpec((tr, c), lambda l, i, me_ref: (l * nt + i, 0))),
        out_shape=jax.ShapeDtypeStruct((depth * r, c), F32), compiler_params=_params(depth * 10 * tr * c * 4),
    )(me, *srcs, *lands)


def _adamw_math(w, g, m, v):
    m2 = ADAM_B1 * m + (1.0 - ADAM_B1) * g
    v2 = ADAM_B2 * v + (1.0 - ADAM_B2) * (g * g)
    m_hat = m2 / (1.0 - ADAM_B1 ** ADAM_STEP)
    v_hat = v2 / (1.0 - ADAM_B2 ** ADAM_STEP)
    delta = -ADAM_LR * (m_hat / (jnp.sqrt(v_hat) + ADAM_EPS) + ADAM_WD * w)
    return delta, m2, v2


def _adamw(g_a, g_b, w, m, v, *, name):
    r, c = w.shape
    tr = _rows_for(c * 4, r, target=512 * 1024)

    def body(a_ref, b_ref, w_ref, m_ref, v_ref, g_out, d_out, m_out, v_out):
        gv = a_ref[...] + b_ref[...]
        delta, m2, v2 = _adamw_math(w_ref[...], gv, m_ref[...], v_ref[...])
        g_out[...] = gv
        d_out[...] = delta
        m_out[...] = m2
        v_out[...] = v2

    spec = pl.BlockSpec((tr, c), lambda i: (i, 0))
    return pl.pallas_call(
        body, name=name, grid=(r // tr,), in_specs=[spec] * 5, out_specs=[spec] * 4,
        out_shape=[jax.ShapeDtypeStruct((r, c), F32)] * 4, compiler_params=_params(22 * tr * c * 4),
    )(g_a, g_b, w, m, v)


def _small_adamw(g_all, w, m, v, *, name):
    r, c = w.shape

    def body(ga_ref, w_ref, m_ref, v_ref, g_out, d_out, m_out, v_out):
        gv = ga_ref[0]
        for j in range(1, N_DEV):
            gv = gv + ga_ref[j]
        delta, m2, v2 = _adamw_math(w_ref[...], gv, m_ref[...], v_ref[...])
        g_out[...] = gv
        d_out[...] = delta
        m_out[...] = m2
        v_out[...] = v2

    return pl.pallas_call(body, name=name, out_shape=[jax.ShapeDtypeStruct((r, c), F32)] * 4)(g_all, w, m, v)


_ANY = pl.BlockSpec(memory_space=pl.ANY)


_HBM = pl.BlockSpec(memory_space=pltpu.HBM)
_SEM = pl.BlockSpec(memory_space=pltpu.SEMAPHORE)
_EFFECT = pltpu.SideEffectType.DATAFLOW_SIDE_EFFECTING


def _peer_chips():
    x, y = lax.axis_index("x"), lax.axis_index("y")
    return 2 * x + y, [(1 - x, y), (x, 1 - y), (1 - x, 1 - y)]


def _exchange_copy(srcs, lands, send_sems, recv_sems, k, p, kind):
    c = lax.axis_index("c")
    if kind == "swap":
        return pltpu.make_async_remote_copy(
            src_ref=srcs[k], dst_ref=lands[k], send_sem=send_sems.at[k], recv_sem=recv_sems.at[k],
            device_id=(lax.axis_index("x"), lax.axis_index("y"), 1 - c), device_id_type=MESH)
    me, peers = _peer_chips()
    px, py = peers[p]
    return pltpu.make_async_remote_copy(
        src_ref=srcs[k].at[2 * px + py] if kind == "scatter" else srcs[k].at[me],
        dst_ref=lands[k].at[p] if kind == "scatter" else lands[k].at[me],
        send_sem=send_sems.at[3 * k + p], recv_sem=recv_sems.at[3 * k + p],
        device_id=(px, py, c), device_id_type=MESH)


def _exchange_start(srcs, lands, after, *, kind, name):
    n = len(srcs)
    npeer = 1 if kind == "swap" else 3
    bufs = list(srcs) + (list(lands) if lands is not None else [])
    nb = len(bufs)

    def body(*refs):
        buf_refs, send_sems, recv_sems = refs[:nb], refs[nb + 1], refs[nb + 2]
        token = refs[-1]
        s_refs = buf_refs[:n]
        l_refs = buf_refs[n:] if lands is not None else s_refs
        for k in range(n):
            for p in range(npeer):
                _exchange_copy(s_refs, l_refs, send_sems, recv_sems, k, p, kind).start()
        token[...] = jnp.zeros_like(token)

    out = pl.pallas_call(
        body, name=name,
        out_shape=(pltpu.SemaphoreType.DMA((npeer * n,)), pltpu.SemaphoreType.DMA((npeer * n,)),
                   *[pltpu.HBM(b.shape, b.dtype) for b in bufs], jax.ShapeDtypeStruct((SUBLANE, LANE), F32)),
        in_specs=[_HBM] * nb + [_ANY],
        out_specs=(_SEM, _SEM, *[_HBM] * nb, pl.BlockSpec(memory_space=pltpu.VMEM)),
        input_output_aliases={i: 2 + i for i in range(nb)},
        compiler_params=pltpu.CompilerParams(has_side_effects=_EFFECT),
    )(*[pltpu.with_memory_space_constraint(b, pltpu.HBM) for b in bufs], after)
    send_sems, recv_sems = out[0], out[1]
    thru = out[2:2 + nb]
    return send_sems, recv_sems, list(thru[:n]), (list(thru[n:]) if lands is not None else None), out[-1]


def _exchange_wait(send_sems, recv_sems, srcs, lands, after, *, kind, name):
    n = len(srcs)
    npeer = 1 if kind == "swap" else 3
    bufs = list(srcs) + (list(lands) if lands is not None else [])
    nb = len(bufs)

    def body(*refs):
        buf_refs, send_sems_ref, recv_sems_ref = refs[:nb], refs[nb], refs[nb + 1]
        s_refs = buf_refs[:n]
        l_refs = buf_refs[n:] if lands is not None else s_refs
        for k in range(n):
            for p in range(npeer):
                cp = _exchange_copy(s_refs, l_refs, send_sems_ref, recv_sems_ref, k, p, kind)
                cp.wait_send()
                cp.wait_recv()

    out = pl.pallas_call(
        body, name=name, out_shape=tuple(pltpu.HBM(b.shape, b.dtype) for b in bufs),
        in_specs=[_HBM] * nb + [_SEM, _SEM, _ANY], out_specs=tuple([_HBM] * nb),
        input_output_aliases={i: i for i in range(nb)},
        compiler_params=pltpu.CompilerParams(has_side_effects=_EFFECT),
    )(*bufs, send_sems, recv_sems, after)
    return list(out)


def _sibling_exchange(srcs, *, name):
    n = len(srcs)

    def body(*refs):
        src, out = refs[:n], refs[n:2 * n]
        send_sems, recv_sems = refs[2 * n:]
        sibling = (lax.axis_index("x"), lax.axis_index("y"), 1 - lax.axis_index("c"))
        copies = [pltpu.make_async_remote_copy(src_ref=src[k], dst_ref=out[k], send_sem=send_sems.at[k],
                                               recv_sem=recv_sems.at[k], device_id=sibling, device_id_type=MESH)
                  for k in range(n)]
        for cp in copies:
            cp.start()
        for cp in copies:
            cp.wait_recv()
        for cp in copies:
            cp.wait_send()

    return pl.pallas_call(
        body, name=name, in_specs=[_ANY] * n, out_specs=[_ANY] * n,
        out_shape=[jax.ShapeDtypeStruct(a.shape, a.dtype) for a in srcs],
        scratch_shapes=[pltpu.SemaphoreType.DMA((n,)), pltpu.SemaphoreType.DMA((n,))],
    )(*srcs)


def _all_gather_small(block, *, name):
    m_per, ncol = block.shape

    def body(x_ref, out_ref, send_sems, recv_sems, local_sem):
        x, y, c = lax.axis_index("x"), lax.axis_index("y"), lax.axis_index("c")
        me, sibling = (x, y, c), (x, y, 1 - c)
        chips = [(1 - x, y), (x, 1 - y), (1 - x, 1 - y)]

        def rows(px, py, pc):
            return out_ref.at[pl.ds((4 * px + 2 * py + pc) * m_per, m_per), :]

        def copy(k, blk, to, src=None):
            return pltpu.make_async_remote_copy(
                src_ref=rows(*blk) if src is None else src, dst_ref=rows(*blk),
                send_sem=send_sems.at[k], recv_sem=recv_sems.at[k], device_id=to, device_id_type=MESH)

        mine = pltpu.make_async_copy(x_ref, rows(*me), local_sem)
        mine.start()
        first = [copy(0, me, sibling, src=x_ref)]
        first += [copy(1 + j, me, (*chip, c), src=x_ref) for j, chip in enumerate(chips)]
        for cp in first:
            cp.start()
        passed = [copy(4 + j, (*chip, c), sibling) for j, chip in enumerate(chips)]
        for j, chip in enumerate(chips):
            copy(1 + j, (*chip, c), me).wait_recv()
            passed[j].start()
        copy(0, sibling, me).wait_recv()
        for j, chip in enumerate(chips):
            copy(4 + j, (*chip, 1 - c), me).wait_recv()
        for cp in first + passed:
            cp.wait_send()
        mine.wait()

    return pl.pallas_call(
        body, name=name, out_shape=jax.ShapeDtypeStruct((N_DEV * m_per, ncol), block.dtype),
        in_specs=[pl.BlockSpec(memory_space=pltpu.VMEM)], out_specs=pl.BlockSpec(memory_space=pltpu.VMEM),
        scratch_shapes=[pltpu.SemaphoreType.DMA((7,)), pltpu.SemaphoreType.DMA((7,)), pltpu.SemaphoreType.DMA],
    )(block)


def _rope_angles(pos, dim):
    inv = ROPE_THETA ** (-jnp.arange(0, dim, 2, dtype=F32) / dim)
    return pos.astype(F32)[:, None] * inv[None, :]


def _rope_tables(s):
    pos = jnp.arange(s, dtype=jnp.int32)
    rows = s // GRID_W
    row = jnp.repeat(jnp.arange(rows, dtype=jnp.int32), GRID_W)
    col = jnp.tile(jnp.arange(GRID_W, dtype=jnp.int32), rows)
    a1 = _rope_angles(pos, HEAD_DIM)
    aa = _rope_angles(pos, A_ROPE)
    ar = _rope_angles(row, HEAD_DIM // 2)
    ac = _rope_angles(col, HEAD_DIM // 2)
    one = jnp.ones((s, LANE), F32)
    zero = jnp.zeros((s, LANE), F32)
    pad = LANE - A_ROPE
    cos_a = jnp.concatenate([one, jnp.cos(aa), jnp.cos(aa), jnp.ones((s, pad), F32)], axis=1)
    sin_a = jnp.concatenate([zero, -jnp.sin(aa), jnp.sin(aa), jnp.zeros((s, pad), F32)], axis=1)
    cos_b = jnp.concatenate([jnp.cos(a1), jnp.cos(a1)], axis=1)
    sin_b = jnp.concatenate([-jnp.sin(a1), jnp.sin(a1)], axis=1)
    cos_c = jnp.concatenate([jnp.cos(ar), jnp.cos(ar), jnp.cos(ac), jnp.cos(ac)], axis=1)
    sin_c = jnp.concatenate([-jnp.sin(ar), jnp.sin(ar), -jnp.sin(ac), jnp.sin(ac)], axis=1)
    return (cos_a, sin_a), (cos_b, sin_b), (cos_c, sin_c)


def _band_table(tq, s):
    reach = max((win // (2 * d)) * d for win, d in B_PATTERNS)
    r = -(-reach // tq) * tq
    w = min(s, tq + 2 * r)
    j = jnp.arange(tq, dtype=jnp.int32)[:, None]
    x = jnp.arange(2 * w - tq, dtype=jnp.int32)[None, :]
    rel = x - (w - tq) - j
    mult = jnp.zeros(rel.shape, F32)
    for win, d in B_PATTERNS:
        mult = mult + jnp.logical_and(rel % d == 0, jnp.abs(rel) <= (win // (2 * d)) * d).astype(F32)
    return jnp.where(mult > 0, jnp.log2(jnp.maximum(mult, 1.0)), NEG), (w, r)


_BIG = ("w_in", "a_w_uq", "a_w_ukv", "w_out", "w_gate", "w_up", "w_down")
_SMALL = ("attn_norm", "a_q_norm", "a_kv_norm", "c_q_norm", "c_k_norm", "out_norm", "ffn_norm", "final_norm")
_WEIGHTS = ("attn_norm", "w_in", "a_q_norm", "a_w_uq", "a_kv_norm", "a_w_ukv", "c_q_norm", "c_k_norm", "out_norm",
            "w_out", "ffn_norm", "w_gate", "w_up", "w_down", "final_norm")


_ATTN = ("w_in", "a_w_uq", "a_w_ukv")
_FFN = ("w_out", "w_gate", "w_up", "w_down")


def _from_cols(a):
    return jnp.transpose(a, (1, 0, 2)).reshape(a.shape[1], N_CHIPS * a.shape[2])


def _from_rows(a):
    return a.reshape(N_CHIPS * a.shape[1], a.shape[2])


def _to_cols(a):
    return jnp.transpose(a.reshape(a.shape[0], N_CHIPS, a.shape[1] // N_CHIPS), (1, 0, 2))


def _to_rows(a):
    return a.reshape(N_CHIPS, a.shape[0] // N_CHIPS, a.shape[1])


def _assemble_attn(gw):
    w_in, uq, ukv = _from_cols(gw[0]), _from_cols(gw[1]), _from_cols(gw[2])
    d = w_in.shape[0]
    w_all = jnp.concatenate([w_in[:, :IN_A], jnp.zeros((d, A_PAD - IN_A), BF16), w_in[:, IN_A:]], axis=1)
    uq = uq.reshape(A_Q_RANK, A_HEADS, A_NOPE + A_ROPE)
    uq = jnp.pad(uq, ((0, 0), (0, 0), (0, A_QK - A_NOPE - A_ROPE))).reshape(A_Q_RANK, A_HEADS * A_QK)
    return dict(w_all=w_all, uq=uq, ukv=ukv)


def _assemble_ffn(gw):
    return dict(w_out=_from_rows(gw[0]), w_gate=gw[1], w_up=gw[2], w_down=_from_rows(gw[3]))


def _split_attn_grads(gl):
    w_all = gl["w_all"]
    w_in = jnp.concatenate([w_all[:, :IN_A], w_all[:, A_PAD:]], axis=1)
    uq = gl["uq"].reshape(A_Q_RANK, A_HEADS, A_QK)[:, :, :A_NOPE + A_ROPE].reshape(A_Q_RANK, A_HEADS * (A_NOPE + A_ROPE))
    return [_to_cols(w_in), _to_cols(uq), _to_cols(gl["ukv"])]


def _split_ffn_grads(gl):
    return [_to_rows(gl["w_out"]), gl["w_gate"], gl["w_up"], _to_rows(gl["w_down"])]


def _tie(a, token):
    return a + token[0:1, 0:1]


def _layer_fwd(x, wl, ffn_weights, sm, tabs, bias, t):
    s = x.shape[0]
    (cos_a, sin_a), (cos_b, sin_b), (cos_c, sin_c) = tabs
    h = _norm_fwd(x, sm["attn_norm"], wb=x.shape[1], cb=0, nb=1, shared_gain=True, out_dtype=BF16, name="attn_norm_fwd")
    p = _matmul(h, wl["w_all"], mode="nn", out_dtype=F32, name="in_proj", tm=1024, tn=640)
    cq_n = _norm_fwd(p, sm["a_q_norm"], wb=A_Q_RANK, cb=0, nb=1, shared_gain=True, out_dtype=BF16, name="a_q_norm_fwd")
    ckv_n = _norm_fwd(p, sm["a_kv_norm"], wb=A_KV_RANK, cb=1, nb=1, shared_gain=True, out_dtype=BF16, name="a_kv_norm_fwd")
    qa_raw = _matmul(cq_n, wl["uq"], mode="nn", out_dtype=F32, name="a_uq", tm=1024, tn=1024)
    kv = _matmul(ckv_n, wl["ukv"], mode="nn", out_dtype=BF16, name="a_ukv", tm=1024, tn=1024)
    k_nope = kv.reshape(s, A_HEADS, 2, LANE)[:, :, 0].astype(F32)
    k_rope = jnp.broadcast_to(p[:, PB_KR * LANE:(PB_KR + 1) * LANE][:, None, :], (s, A_HEADS, LANE))
    ka_raw = jnp.stack([k_nope, k_rope], axis=2).reshape(s, A_HEADS * A_QK)
    qa = _rope(qa_raw, cos_a, sin_a, tw=A_QK, cb=0, nb=A_HEADS, half=A_ROPE // 2, sign=1, out_dtype=BF16, name="a_rope_q")
    ka = _rope(ka_raw, cos_a, sin_a, tw=A_QK, cb=0, nb=A_HEADS, half=A_ROPE // 2, sign=1, out_dtype=BF16, name="a_rope_k")
    oa, lse_a = _flash_fwd(qa, ka, kv, None, hkv=A_HEADS, g=1, dqk=A_QK, q_cb=0, k_cb=0, v_cb=1, v_step=2,
                           scale=(A_NOPE + A_ROPE) ** -0.5, tq=t, band=None, name="a_flash_fwd")
    table, band = bias
    qb = _rope(p, cos_b, sin_b, tw=LANE, cb=PB_BQ, nb=B_HEADS, half=HEAD_DIM // 2, sign=1, out_dtype=BF16, name="b_rope_q")
    kb = _rope(p, cos_b, sin_b, tw=LANE, cb=PB_BK, nb=B_HEADS, half=HEAD_DIM // 2, sign=1, out_dtype=BF16, name="b_rope_k")
    vb = _cast_cols(p, cb=PB_BV, nb=B_HEADS, name="b_cast_v")
    ob, lse_b = _flash_fwd(qb, kb, vb, table, hkv=B_HEADS, g=1, dqk=LANE, q_cb=0, k_cb=0, v_cb=0, v_step=1,
                           scale=HEAD_DIM ** -0.5, tq=t, band=band, name="b_flash_fwd")
    qn = _norm_fwd(p, sm["c_q_norm"], wb=LANE, cb=PB_CQH, nb=C_HEADS, shared_gain=True, out_dtype=F32, name="c_q_norm_fwd")
    kn = _norm_fwd(p, sm["c_k_norm"], wb=LANE, cb=PB_CKH, nb=C_KV_HEADS, shared_gain=True, out_dtype=F32, name="c_k_norm_fwd")
    qc = _rope(qn, cos_c, sin_c, tw=LANE, cb=0, nb=C_HEADS, half=HEAD_DIM // 4, sign=1, out_dtype=BF16, name="c_rope_q")
    kc = _rope(kn, cos_c, sin_c, tw=LANE, cb=0, nb=C_KV_HEADS, half=HEAD_DIM // 4, sign=1, out_dtype=BF16, name="c_rope_k")
    vc = _cast_cols(p, cb=PB_CVH, nb=C_KV_HEADS, name="c_cast_v")
    oc, lse_c = _flash_fwd(qc, kc, vc, None, hkv=C_KV_HEADS, g=C_GROUP, dqk=LANE, q_cb=0, k_cb=0, v_cb=0, v_step=1,
                           scale=HEAD_DIM ** -0.5, tq=t, band=None, name="c_flash_fwd")
    g_out = sm["out_norm"]
    ga, gb, gc = g_out[:, :A_WIDTH], g_out[:, A_WIDTH:A_WIDTH + B_WIDTH], g_out[:, A_WIDTH + B_WIDTH:]
    ya = _norm_fwd(oa, ga, wb=A_WIDTH, cb=0, nb=1, shared_gain=True, out_dtype=BF16, name="out_norm_a_fwd")
    yb = _norm_fwd(ob, gb, wb=B_WIDTH, cb=0, nb=1, shared_gain=True, out_dtype=BF16, name="out_norm_b_fwd")
    yc = _norm_fwd(oc, gc, wb=C_WIDTH, cb=0, nb=1, shared_gain=True, out_dtype=BF16, name="out_norm_c_fwd")
    y = jnp.concatenate([ya, yb, yc], axis=1)
    wl = {**wl, **ffn_weights(y)}
    x1 = _matmul(y, wl["w_out"], mode="nn", out_dtype=F32, name="out_proj", add=x, tm=1024, tn=512)
    h2 = _norm_fwd(x1, sm["ffn_norm"], wb=x.shape[1], cb=0, nb=1, shared_gain=True, out_dtype=BF16, name="ffn_norm_fwd")
    gate, up, act = _ffn_up(h2, wl["w_gate"], wl["w_up"], name="ffn_up")
    x2 =_matmul(act, wl["w_down"], mode="nn", out_dtype=F32, name="ffn_down", add=x1, tm=512, tn=512)
    saved = dict(x=x, h=h, p=p, cq_n=cq_n, ckv_n=ckv_n, kv=kv, qa=qa, ka=ka, oa=oa, lse_a=lse_a, qb=qb, kb=kb, vb=vb, ob=ob,
                 lse_b=lse_b, qc=qc, kc=kc, vc=vc, oc=oc, lse_c=lse_c, y=y, x1=x1, h2=h2, gate=gate, up=up, act=act)
    return x2, saved, wl


def _layer_bwd(dx2, dx2b, sv, wl, sm, tabs, bias, t, send_ffn, send_attn):
    s, d = dx2.shape
    (cos_a, sin_a), (cos_b, sin_b), (cos_c, sin_c) = tabs
    gw, gs = {}, {}
    dgate, dup = _ffn_down_dx(dx2b, wl["w_down"], sv["gate"], sv["up"], name="ffn_down_dx")
    gw["w_down"] = _matmul(sv["act"], dx2b, mode="tn", out_dtype=BF16, name="ffn_down_dw", tm=512, tn=512)
    dh2 = _ffn_up_dx(dgate, dup, wl["w_gate"], wl["w_up"], name="ffn_up_dx")
    gw["w_gate"] = _matmul(sv["h2"], dgate, mode="tn", out_dtype=BF16, name="ffn_gate_dw", tm=512, col_shards=True)
    gw["w_up"] = _matmul(sv["h2"], dup, mode="tn", out_dtype=BF16, name="ffn_up_dw", tm=512, col_shards=True)
    dx1, gs["ffn_norm"], dx1b = _norm_bwd(sv["x1"], sm["ffn_norm"], dh2, wb=d, cb=0, nb=1, shared_gain=True,
                                          out_dtype=F32, name="ffn_norm_bwd", add=dx2, bf16_copy=True)
    dy = _matmul(dx1b, wl["w_out"], mode="nt", out_dtype=F32, name="out_proj_dx", tm=512, tn=512)
    gw["w_out"] = _matmul(sv["y"], dx1b, mode="tn", out_dtype=BF16, name="out_proj_dw", tm=512, tn=512)
    token = send_ffn(gw)
    g_out = _tie(sm["out_norm"], token)
    ga, gb, gc = g_out[:, :A_WIDTH], g_out[:, A_WIDTH:A_WIDTH + B_WIDTH], g_out[:, A_WIDTH + B_WIDTH:]
    dya, dyb, dyc = dy[:, :A_WIDTH], dy[:, A_WIDTH:A_WIDTH + B_WIDTH], dy[:, A_WIDTH + B_WIDTH:]
    doa, dga = _norm_bwd(sv["oa"], ga, dya, wb=A_WIDTH, cb=0, nb=1, shared_gain=True, out_dtype=F32, name="out_norm_a_bwd")
    dob, dgb = _norm_bwd(sv["ob"], gb, dyb, wb=B_WIDTH, cb=0, nb=1, shared_gain=True, out_dtype=F32, name="out_norm_b_bwd")
    doc, dgc = _norm_bwd(sv["oc"], gc, dyc, wb=C_WIDTH, cb=0, nb=1, shared_gain=True, out_dtype=F32, name="out_norm_c_bwd")
    gs["out_norm"] = jnp.concatenate([dga, dgb, dgc], axis=1)
    p = sv["p"]
    dqc, dkc, dvc = _flash_bwd(sv["qc"], sv["kc"], sv["vc"], sv["oc"], doc, sv["lse_c"], None, hkv=C_KV_HEADS,
                               g=C_GROUP, dqk=LANE, q_cb=0, k_cb=0, v_cb=0, v_step=1, scale=HEAD_DIM ** -0.5, tq=t,
                               band=None, name="c_flash_bwd")
    dqn = _rope(dqc, cos_c, sin_c, tw=LANE, cb=0, nb=C_HEADS, half=HEAD_DIM // 4, sign=-1, out_dtype=F32, name="c_rope_q_bwd")
    dkn = _rope(dkc, cos_c, sin_c, tw=LANE, cb=0, nb=C_KV_HEADS, half=HEAD_DIM // 4, sign=-1, out_dtype=F32, name="c_rope_k_bwd")
    dpcq, gs["c_q_norm"] = _norm_bwd(p, sm["c_q_norm"], dqn, wb=LANE, cb=PB_CQH, nb=C_HEADS, shared_gain=True,
                                     out_dtype=BF16, name="c_q_norm_bwd")
    dpck, gs["c_k_norm"] = _norm_bwd(p, sm["c_k_norm"], dkn, wb=LANE, cb=PB_CKH, nb=C_KV_HEADS, shared_gain=True,
                                     out_dtype=BF16, name="c_k_norm_bwd")
    table, band = bias
    dqb, dkb, dvb = _flash_bwd(sv["qb"], sv["kb"], sv["vb"], sv["ob"], dob, sv["lse_b"], table, hkv=B_HEADS, g=1,
                               dqk=LANE, q_cb=0, k_cb=0, v_cb=0, v_step=1, scale=HEAD_DIM ** -0.5, tq=t, band=band,
                               name="b_flash_bwd")
    dpbq = _rope(dqb, cos_b, sin_b, tw=LANE, cb=0, nb=B_HEADS, half=HEAD_DIM // 2, sign=-1, out_dtype=BF16, name="b_rope_q_bwd")
    dpbk = _rope(dkb, cos_b, sin_b, tw=LANE, cb=0, nb=B_HEADS, half=HEAD_DIM // 2, sign=-1, out_dtype=BF16, name="b_rope_k_bwd")
    dqa, dka, dva = _flash_bwd(sv["qa"], sv["ka"], sv["kv"], sv["oa"], doa, sv["lse_a"], None, hkv=A_HEADS, g=1,
                               dqk=A_QK, q_cb=0, k_cb=0, v_cb=1, v_step=2, scale=(A_NOPE + A_ROPE) ** -0.5, tq=t,
                               band=None, name="a_flash_bwd")
    dqa_raw = _rope(dqa, cos_a, sin_a, tw=A_QK, cb=0, nb=A_HEADS, half=A_ROPE // 2, sign=-1, out_dtype=BF16, name="a_rope_q_bwd")
    dka_raw = _rope(dka, cos_a, sin_a, tw=A_QK, cb=0, nb=A_HEADS, half=A_ROPE // 2, sign=-1, out_dtype=BF16, name="a_rope_k_bwd")
    dkr = _group_sum(dka_raw, n_out=1, g=A_HEADS, src=lambda n, j: 2 * j + 1, out_dtype=BF16, name="a_k_rope_sum")
    dkv = jnp.stack([dka_raw.reshape(s, A_HEADS, 2, LANE)[:, :, 0], dva.reshape(s, A_HEADS, LANE).astype(BF16)], axis=2)
    dkv = dkv.reshape(s, A_HEADS * 2 * LANE)
    dckv_n = _matmul(dkv, wl["ukv"], mode="nt", out_dtype=F32, name="a_ukv_dx", tm=1024, tn=512)
    gw["ukv"] = _matmul(sv["ckv_n"], dkv, mode="tn", out_dtype=BF16, name="a_ukv_dw", tm=512, tn=1024)
    dcq_n = _matmul(dqa_raw, wl["uq"], mode="nt", out_dtype=F32, name="a_uq_dx", tm=1024, tn=512)
    gw["uq"] = _matmul(sv["cq_n"], dqa_raw, mode="tn", out_dtype=BF16, name="a_uq_dw", tm=512, tn=1024)
    dcq, gs["a_q_norm"] = _norm_bwd(p, sm["a_q_norm"], dcq_n, wb=A_Q_RANK, cb=0, nb=1, shared_gain=True, out_dtype=BF16,
                                    name="a_q_norm_bwd")
    dckv, gs["a_kv_norm"] = _norm_bwd(p, sm["a_kv_norm"], dckv_n, wb=A_KV_RANK, cb=1, nb=1, shared_gain=True,
                                      out_dtype=BF16, name="a_kv_norm_bwd")
    dp = jnp.concatenate([dcq, dckv, dkr, jnp.zeros((s, A_PAD - (PB_KR + 1) * LANE), BF16), dpbq, dpbk,
                          dvb.astype(BF16), dpcq, dpck, dvc.astype(BF16)], axis=1)
    gw["w_all"] = _matmul(sv["h"], dp, mode="tn", out_dtype=BF16, name="in_proj_dw", tm=512, tn=640)
    token = send_attn(gw)
    dh = _matmul(dp, wl["w_all"], mode="nt", out_dtype=F32, name="in_proj_dx", tm=512, tn=512, after=token)
    dx, gs["attn_norm"], dxb = _norm_bwd(sv["x"], sm["attn_norm"], dh, wb=d, cb=0, nb=1, shared_gain=True,
                                         out_dtype=F32, name="attn_norm_bwd", add=dx1, bf16_copy=True)
    return dx, dxb, gs, token


def _pack_small(vals):
    flat = jnp.concatenate([vals[n].reshape(-1).astype(F32) for n in _SMALL])
    tile = SUBLANE * LANE
    padded = -(-flat.shape[0] // tile) * tile
    return jnp.pad(flat, (0, padded - flat.shape[0])).reshape(padded // LANE, LANE)


def _unpack_small(packed, like):
    flat = packed.reshape(-1)
    out, off = {}, 0
    for n in _SMALL:
        size = math.prod(like[n].shape)
        out[n] = flat[off:off + size].reshape(like[n].shape)
        off += size
    return out


def kernel(x, attn_norm, w_in, a_q_norm, a_w_uq, a_kv_norm, a_w_ukv, c_q_norm, c_k_norm, out_norm, w_out, ffn_norm, w_gate, w_up, w_down, final_norm, loss_target, m_attn_norm, m_w_in, m_a_q_norm, m_a_w_uq, m_a_kv_norm, m_a_w_ukv, m_c_q_norm, m_c_k_norm, m_out_norm, m_w_out, m_ffn_norm, m_w_gate, m_w_up, m_w_down, m_final_norm, v_attn_norm, v_w_in, v_a_q_norm, v_a_w_uq, v_a_kv_norm, v_a_w_ukv, v_c_q_norm, v_c_k_norm, v_out_norm, v_w_out, v_ffn_norm, v_w_gate, v_w_up, v_w_down, v_final_norm):
    w = dict(attn_norm=attn_norm, w_in=w_in, a_q_norm=a_q_norm, a_w_uq=a_w_uq, a_kv_norm=a_kv_norm, a_w_ukv=a_w_ukv,
             c_q_norm=c_q_norm, c_k_norm=c_k_norm, out_norm=out_norm, w_out=w_out, ffn_norm=ffn_norm, w_gate=w_gate,
             w_up=w_up, w_down=w_down, final_norm=final_norm)
    m = dict(attn_norm=m_attn_norm, w_in=m_w_in, a_q_norm=m_a_q_norm, a_w_uq=m_a_w_uq, a_kv_norm=m_a_kv_norm,
             a_w_ukv=m_a_w_ukv, c_q_norm=m_c_q_norm, c_k_norm=m_c_k_norm, out_norm=m_out_norm, w_out=m_w_out,
             ffn_norm=m_ffn_norm, w_gate=m_w_gate, w_up=m_w_up, w_down=m_w_down, final_norm=m_final_norm)
    v = dict(attn_norm=v_attn_norm, w_in=v_w_in, a_q_norm=v_a_q_norm, a_w_uq=v_a_w_uq, a_kv_norm=v_a_kv_norm,
             a_w_ukv=v_a_w_ukv, c_q_norm=v_c_q_norm, c_k_norm=v_c_k_norm, out_norm=v_out_norm, w_out=v_w_out,
             ffn_norm=v_ffn_norm, w_gate=v_w_gate, w_up=v_w_up, w_down=v_w_down, final_norm=v_final_norm)
    _, s, d = x.shape
    depth = attn_norm.shape[0]
    t = _pick(s, 512)

    me = (2 * lax.axis_index("x") + lax.axis_index("y")).astype(jnp.int32).reshape(1)

    gathers, after = {}, me
    for l in range(depth):
        for group, names in (("attn", _ATTN), ("ffn", _FFN)):
            bufs = [_cast_to_slot(w[n].reshape(-1, w[n].shape[-1]), me, layer=l, rows=w[n].shape[1], name=f"cast_{n}")
                    for n in names]
            send_sems, recv_sems, bufs, _, after = _exchange_start(bufs, None, after, kind="gather",
                                                                   name=f"gather_start_{group}{l}")
            gathers[group, l] = (send_sems, recv_sems, bufs)
    all_started = after

    def gathered(group, l, after):
        send_sems, recv_sems, bufs = gathers[group, l]
        return _exchange_wait(send_sems, recv_sems, bufs, None, after, kind="gather", name=f"gather_wait_{group}{l}")

    tabs = _rope_tables(s)
    bias = _band_table(t, s)

    xs = x.reshape(s, d)
    saved, wls, sms = [], [], []
    for l in range(depth):
        wl = _assemble_attn(gathered("attn", l, all_started if l == 0 else xs))
        sm = {n: w[n][l][None, :] for n in _SMALL if n != "final_norm"}
        xs, sv, wl = _layer_fwd(xs, wl, lambda after, l=l: _assemble_ffn(gathered("ffn", l, after)), sm, tabs, bias, t)
        saved.append(sv)
        wls.append(wl)
        sms.append(sm)
    dx, g_final, loss_row, dxb = _final_loss(xs, final_norm[None, :], loss_target.reshape(s, d), name="final_loss")
    loss = lax.psum(loss_row[0, 0], ("x", "y", "c"))

    sends = {}

    def send(group, l, srcs, after):
        lands = [lax.empty((3,) + a.shape[1:], BF16) for a in srcs]
        send_sems, recv_sems, srcs, lands, token = _exchange_start(srcs, lands, after, kind="scatter",
                                                                   name=f"scatter_start_{group}{l}")
        sends[group, l] = (send_sems, recv_sems, srcs, lands)
        return token

    gs_layers, token = [None] * depth, all_started
    for l in reversed(range(depth)):
        dx, dxb, gs_layers[l], token = _layer_bwd(
            dx, dxb, saved[l], wls[l], sms[l], tabs, bias, t,
            lambda gw, l=l, tk=token: send("ffn", l, _split_ffn_grads(gw), tk),
            lambda gw, l=l: send("attn", l, _split_attn_grads(gw), dx))
    grad_x = dx.reshape(x.shape)

    srcs, lands = {}, {}

    def arrive(key, after):
        send_sems, recv_sems, s_bufs, l_bufs = sends[key]
        got = _exchange_wait(send_sems, recv_sems, s_bufs, l_bufs, after, kind="scatter",
                             name=f"scatter_wait_{key[0]}{key[1]}")
        for k, n in enumerate(_ATTN if key[0] == "attn" else _FFN):
            srcs[n, key[1]], lands[n, key[1]] = got[k], got[len(s_bufs) + k]

    def summed(names):
        return [_sum_parts([srcs[n, l] for l in range(depth)], [lands[n, l] for l in range(depth)], me, name="sum_" + n)
                for n in names]

    last = ("attn", 0)
    for key in sends:
        if key != last:
            arrive(key, token)
    sums_ffn = summed(_FFN)
    swap = _exchange_start(sums_ffn, [lax.empty(a.shape, F32) for a in sums_ffn], token, kind="swap",
                           name="swap_start_ffn")
    arrive(last, swap[4])
    sums_attn = summed(_ATTN)
    sib_attn = list(_sibling_exchange(sums_attn, name="swap_core_sums_attn"))
    swapped = _exchange_wait(swap[0], swap[1], swap[2], swap[3], sib_attn[0], kind="swap", name="swap_wait_ffn")
    mine_of = dict(zip(_FFN + _ATTN, swapped[:len(_FFN)] + sums_attn))
    other_of = dict(zip(_FFN + _ATTN, swapped[len(_FFN):] + sib_attn))
    grads, deltas, new_m, new_v = {}, {}, {}, {}
    for n in _BIG:
        mine, other = mine_of[n], other_of[n]
        shp = w[n].shape
        two_d = (-1, shp[-1])
        res = _adamw(mine, other, w[n].reshape(two_d), m[n].reshape(two_d), v[n].reshape(two_d), name="adamw_" + n)
        grads[n], deltas[n], new_m[n], new_v[n] = [r.reshape(shp) for r in res]

    gsm = {n: jnp.stack([gs_layers[l][n][0] for l in range(depth)]) for n in _SMALL if n != "final_norm"}
    gsm["final_norm"] = g_final[0]
    packed = _pack_small(gsm)
    everyone = _all_gather_small(packed, name="gather_gain_grads").reshape(N_DEV, packed.shape[0], LANE)
    res = _small_adamw(everyone, _pack_small(w), _pack_small(m), _pack_small(v), name="adamw_gains")
    for dst, r in zip((grads, deltas, new_m, new_v), res):
        dst.update(_unpack_small(r, w))

    return (loss, grad_x, *[grads[n] for n in _WEIGHTS], *[deltas[n] for n in _WEIGHTS],
            *[new_m[n] for n in _WEIGHTS], *[new_v[n] for n in _WEIGHTS])
```

```python
import functools
import math

import jax
import jax.numpy as jnp
import numpy as np
from jax import lax
from jax.experimental import pallas as pl
from jax.experimental.pallas import tpu as pltpu

F32 = jnp.float32
BF16 = jnp.bfloat16
MESH = pl.DeviceIdType.MESH

HEAD_DIM = 128
ROPE_THETA = 10000.0
GRID_W = 64
EPS = 1e-6
NEG = -1e30
A_HEADS, A_Q_RANK, A_KV_RANK, A_NOPE, A_ROPE, A_V = 4, 512, 512, 128, 64, 128
B_HEADS = 6
B_PATTERNS = ((128, 1), (512, 4), (2048, 16))
C_HEADS, C_KV_HEADS = 6, 2
C_GROUP = C_HEADS // C_KV_HEADS
A_WIDTH, B_WIDTH, C_WIDTH = A_HEADS * A_V, B_HEADS * HEAD_DIM, C_HEADS * HEAD_DIM
IN_A = A_Q_RANK + A_KV_RANK + A_ROPE
IN_B = 3 * B_WIDTH
IN_C = C_WIDTH + 2 * C_KV_HEADS * HEAD_DIM
ADAM_LR, ADAM_B1, ADAM_B2, ADAM_EPS, ADAM_WD, ADAM_STEP = 0.001, 0.9, 0.999, 1e-08, 0.01, 10

LANE = 128
SUBLANE = 8
VMEM_BYTES_V7X = 64 * 1024 * 1024
VMEM_LIMIT_CAP = VMEM_BYTES_V7X - 8 * 1024 * 1024
N_CHIPS = 4
N_DEV = 8

A_PAD = 12 * LANE
PB_CQ, PB_CKV, PB_KR = 0, 4, 8
PB_BQ, PB_BK, PB_BV = 12, 18, 24
PB_CQH, PB_CKH, PB_CVH = 30, 36, 38
NP = 40 * LANE
A_QK = 2 * LANE


def _pick(n, cap, mult=LANE):
    if n <= cap:
        return n
    t = cap - cap % mult
    while t >= mult:
        if n % t == 0:
            return t
        t -= mult
    return n


def _rows_for(width_bytes, n_rows, target=2 * 1024 * 1024):
    return _pick(n_rows, max(SUBLANE, target // max(width_bytes, 1)), SUBLANE)


def _params(est_bytes):
    limit = int(min(max(est_bytes + (4 << 20), 32 << 20), VMEM_LIMIT_CAP))
    return pltpu.CompilerParams(vmem_limit_bytes=limit)


def _isz(x):
    return jnp.dtype(x.dtype).itemsize


def _hbm(shape, dtype):
    return pltpu.HBM(shape, dtype)


def _pin(*arrays):
    return [pltpu.with_memory_space_constraint(a, pltpu.HBM) for a in arrays]


_DIMS = {"nn": (((1,), (0,)), ((), ())), "nt": (((1,), (1,)), ((), ())), "tn": (((0,), (0,)), ((), ()))}


def _matmul(a, b, *, mode, out_dtype, name, add=None, tm=512, tn=512, col_shards=False, after=None):
    if mode == "tn":
        (k, m), (k2, n) = a.shape, b.shape
    elif mode == "nt":
        (m, k), (n, k2) = a.shape, b.shape
    else:
        (m, k), (k2, n) = a.shape, b.shape
    assert k == k2, (a.shape, b.shape, mode)
    tm, tn = _pick(m, tm), (n // N_CHIPS if col_shards else _pick(n, tn))
    a_spec = pl.BlockSpec((k, tm), lambda i, j: (0, i)) if mode == "tn" else pl.BlockSpec((tm, k), lambda i, j: (i, 0))
    b_spec = pl.BlockSpec((tn, k), lambda i, j: (j, 0)) if mode == "nt" else pl.BlockSpec((k, tn), lambda i, j: (0, j))
    o_spec = pl.BlockSpec((None, tm, tn), lambda i, j: (j, i, 0)) if col_shards else pl.BlockSpec((tm, tn), lambda i, j: (i, j))
    dims = _DIMS[mode]

    def body(*refs):
        a_ref, b_ref, o_ref = refs[0], refs[1], refs[-1]
        acc = lax.dot_general(a_ref[...].astype(BF16), b_ref[...].astype(BF16), dims, preferred_element_type=F32)
        if add is not None:
            acc = acc + refs[2][...].astype(F32)
        o_ref[...] = acc.astype(out_dtype)

    ins, specs = [a, b], [a_spec, b_spec]
    if add is not None:
        ins.append(add)
        specs.append(o_spec)
    if after is not None:
        ins.append(after)
        specs.append(pl.BlockSpec(memory_space=pl.ANY))
    est = 2 * (tm * k * _isz(a) + tn * k * _isz(b) + tm * tn * (jnp.dtype(out_dtype).itemsize + (4 if add is not None else 0)))
    est += (tm + tn) * k * 2 + 2 * tm * tn * 4
    return pl.pallas_call(
        body, name=name, grid=(m // tm, n // tn), in_specs=specs, out_specs=o_spec,
        out_shape=_hbm((N_CHIPS, m, tn) if col_shards else (m, n), out_dtype),
        compiler_params=_params(est),
    )(*_pin(*ins))


def _ffn_up(h, wg, wu, *, name):
    s, d = h.shape
    _, _, c = wg.shape
    tm = _pick(s, 512, SUBLANE)

    def body(h_ref, wg_ref, wu_ref, g_ref, u_ref, a_ref):
        hv = h_ref[...]
        gv = jnp.dot(hv, wg_ref[...], preferred_element_type=F32)
        uv = jnp.dot(hv, wu_ref[...], preferred_element_type=F32)
        g_ref[...] = gv
        u_ref[...] = uv
        a_ref[...] = (gv / (1.0 + jnp.exp(-gv)) * uv).astype(BF16)

    w_spec = pl.BlockSpec((None, d, c), lambda j, i: (j, 0, 0))
    o_spec = pl.BlockSpec((tm, c), lambda j, i: (i, j))
    est = 2 * (tm * d * 2 + 2 * d * c * 2 + tm * c * 10) + 4 * tm * c * 4
    return pl.pallas_call(
        body, name=name, grid=(N_CHIPS, s // tm), in_specs=[pl.BlockSpec((tm, d), lambda j, i: (i, 0)), w_spec, w_spec],
        out_specs=[o_spec, o_spec, o_spec],
        out_shape=[_hbm((s, N_CHIPS * c), F32), _hbm((s, N_CHIPS * c), F32),
                   _hbm((s, N_CHIPS * c), BF16)],
        compiler_params=_params(est),
    )(*_pin(h, wg, wu))


def _ffn_down_dx(dx, w_down, gate, up, *, name):
    s, d = dx.shape
    f = w_down.shape[0]
    tm, tn = _pick(s, 512, SUBLANE), _pick(f, 512)

    def body(dx_ref, w_ref, g_ref, u_ref, dg_ref, du_ref):
        dact = lax.dot_general(dx_ref[...], w_ref[...], _DIMS["nt"], preferred_element_type=F32)
        gv, uv = g_ref[...], u_ref[...]
        sig = 1.0 / (1.0 + jnp.exp(-gv))
        dg_ref[...] = (dact * uv * (sig * (1.0 + gv * (1.0 - sig)))).astype(BF16)
        du_ref[...] = (dact * (gv * sig)).astype(BF16)

    t_spec = pl.BlockSpec((tm, tn), lambda i, j: (i, j))
    est = 2 * (tm * d * 2 + tn * d * 2 + tm * tn * 12) + 6 * tm * tn * 4
    return pl.pallas_call(
        body, name=name, grid=(s // tm, f // tn),
        in_specs=[pl.BlockSpec((tm, d), lambda i, j: (i, 0)), pl.BlockSpec((tn, d), lambda i, j: (j, 0)), t_spec, t_spec],
        out_specs=[t_spec, t_spec], out_shape=[_hbm((s, f), BF16)] * 2, compiler_params=_params(est),
    )(*_pin(dx, w_down, gate, up))


def _ffn_up_dx(dgate, dup, wg, wu, *, name):
    s, f = dgate.shape
    _, d, c = wg.shape
    tm, tn = _pick(s, 1024, SUBLANE), _pick(d, 1024)
    nk = 2 * N_CHIPS

    def body(dg_ref, du_ref, wg_ref, wu_ref, o_ref, acc):
        kk = pl.program_id(2)

        @pl.when(kk == 0)
        def _():
            acc[...] = jnp.zeros_like(acc)

        @pl.when(kk < N_CHIPS)
        def _():
            acc[...] += lax.dot_general(dg_ref[...], wg_ref[...], _DIMS["nt"], preferred_element_type=F32)

        @pl.when(kk >= N_CHIPS)
        def _():
            acc[...] += lax.dot_general(du_ref[...], wu_ref[...], _DIMS["nt"], preferred_element_type=F32)

        @pl.when(kk == nk - 1)
        def _():
            o_ref[...] = acc[...]

    last = N_CHIPS - 1
    est = 2 * (2 * tm * c * 2 + 2 * tn * c * 2 + tm * tn * 4) + 2 * tm * tn * 4
    return pl.pallas_call(
        body, name=name, grid=(s // tm, d // tn, nk),
        in_specs=[pl.BlockSpec((tm, c), lambda i, j, kk: (i, jnp.minimum(kk, last))),
                  pl.BlockSpec((tm, c), lambda i, j, kk: (i, jnp.maximum(kk - N_CHIPS, 0))),
                  pl.BlockSpec((None, tn, c), lambda i, j, kk: (jnp.minimum(kk, last), j, 0)),
                  pl.BlockSpec((None, tn, c), lambda i, j, kk: (jnp.maximum(kk - N_CHIPS, 0), j, 0))],
        out_specs=pl.BlockSpec((tm, tn), lambda i, j, kk: (i, j)),
        out_shape=_hbm((s, d), F32), scratch_shapes=[pltpu.VMEM((tm, tn), F32)],
        compiler_params=_params(est),
    )(*_pin(dgate, dup, wg, wu))


def _norm_fwd(x, gain, *, wb, cb, nb, shared_gain, out_dtype, name):
    s = x.shape[0]
    ts = _rows_for(wb * 4, s)

    def body(x_ref, g_ref, o_ref):
        xv = x_ref[...].astype(F32)
        r = lax.rsqrt(jnp.mean(xv * xv, axis=1, keepdims=True) + EPS)
        o_ref[...] = ((xv * r) * g_ref[...]).astype(out_dtype)

    return pl.pallas_call(
        body, name=name, grid=(nb, s // ts),
        in_specs=[pl.BlockSpec((ts, wb), lambda n, i: (i, cb + n)),
                  pl.BlockSpec((1, wb), (lambda n, i: (0, 0)) if shared_gain else (lambda n, i: (0, n)))],
        out_specs=pl.BlockSpec((ts, wb), lambda n, i: (i, n)),
        out_shape=_hbm((s, nb * wb), out_dtype), compiler_params=_params(6 * ts * wb * 4),
    )(*_pin(x), gain)


def _norm_bwd(x, gain, dy, *, wb, cb, nb, shared_gain, out_dtype, name, dy_cb=0, add=None, bf16_copy=False):
    s = x.shape[0]
    ts = _rows_for(wb * 4, s, target=1024 * 1024)
    gw = wb if shared_gain else nb * wb

    def body(*refs):
        refs = list(refs)
        dxb_ref = refs.pop() if bf16_copy else None
        if add is None:
            x_ref, g_ref, dy_ref, dx_ref, dg_ref = refs
        else:
            x_ref, g_ref, dy_ref, add_ref, dx_ref, dg_ref = refs
        n, i = pl.program_id(0), pl.program_id(1)
        xv = x_ref[...].astype(F32)
        dyv = dy_ref[...].astype(F32)
        r = lax.rsqrt(jnp.mean(xv * xv, axis=1, keepdims=True) + EPS)
        xh = xv * r
        dyg = dyv * g_ref[...]
        dx = r * (dyg - xh * jnp.mean(dyg * xh, axis=1, keepdims=True))
        if add is not None:
            dx = dx + add_ref[...]
        dx_ref[...] = dx.astype(out_dtype)
        if bf16_copy:
            dxb_ref[...] = dx.astype(BF16)
        first = jnp.logical_and(n == 0, i == 0) if shared_gain else (i == 0)

        @pl.when(first)
        def _():
            dg_ref[...] = jnp.zeros_like(dg_ref)

        dg_ref[...] += jnp.sum(dyv * xh, axis=0, keepdims=True)

    ins = [x, gain, dy]
    specs = [pl.BlockSpec((ts, wb), lambda n, i: (i, cb + n)),
             pl.BlockSpec((1, wb), (lambda n, i: (0, 0)) if shared_gain else (lambda n, i: (0, n))),
             pl.BlockSpec((ts, wb), lambda n, i: (i, dy_cb + n))]
    if add is not None:
        ins.append(add)
        specs.append(pl.BlockSpec((ts, wb), lambda n, i: (i, n)))
    out_specs = [pl.BlockSpec((ts, wb), lambda n, i: (i, n)),
                 pl.BlockSpec((1, wb), (lambda n, i: (0, 0)) if shared_gain else (lambda n, i: (0, n)))]
    out_shape = [_hbm((s, nb * wb), out_dtype), jax.ShapeDtypeStruct((1, gw), F32)]
    if bf16_copy:
        out_specs.append(out_specs[0])
        out_shape.append(_hbm((s, nb * wb), BF16))
    return pl.pallas_call(
        body, name=name, grid=(nb, s // ts), in_specs=specs, out_specs=out_specs, out_shape=out_shape,
        compiler_params=_params(14 * ts * wb * 4),
    )(*_pin(*ins))


def _swap_halves(x, half):
    if 2 * half == LANE:
        return pltpu.roll(x, half, axis=1)
    lane = lax.broadcasted_iota(jnp.int32, x.shape, 1)
    first = jnp.bitwise_and(lane, 2 * half - 1) < half
    return jnp.where(first, pltpu.roll(x, LANE - half, axis=1), pltpu.roll(x, half, axis=1))


def _rope(x, cos_t, sin_t, *, tw, cb, nb, half, sign, out_dtype, name):
    s = x.shape[0]
    ts = _rows_for(tw * 4, s)

    def body(x_ref, c_ref, s_ref, o_ref):
        for q in range(tw // LANE):
            sl = slice(q * LANE, (q + 1) * LANE)
            xv = x_ref[:, sl].astype(F32)
            sv = s_ref[:, sl]
            if sign < 0:
                sv = -sv
            o_ref[:, sl] = (xv * c_ref[:, sl] + _swap_halves(xv, half) * sv).astype(out_dtype)

    return pl.pallas_call(
        body, name=name, grid=(nb, s // ts),
        in_specs=[pl.BlockSpec((ts, tw), lambda n, i: (i, cb + n)),
                  pl.BlockSpec((ts, tw), lambda n, i: (i, 0)),
                  pl.BlockSpec((ts, tw), lambda n, i: (i, 0))],
        out_specs=pl.BlockSpec((ts, tw), lambda n, i: (i, n)),
        out_shape=_hbm((s, nb * tw), out_dtype), compiler_params=_params(10 * ts * tw * 4),
    )(*_pin(x), cos_t, sin_t)


def _cast_cols(x, *, cb, nb, name):
    s = x.shape[0]
    ts = _rows_for(LANE * 4, s)

    def body(x_ref, o_ref):
        o_ref[...] = x_ref[...].astype(BF16)

    return pl.pallas_call(
        body, name=name, grid=(nb, s // ts), in_specs=[pl.BlockSpec((ts, LANE), lambda n, i: (i, cb + n))],
        out_specs=pl.BlockSpec((ts, LANE), lambda n, i: (i, n)),
        out_shape=_hbm((s, nb * LANE), BF16), compiler_params=_params(4 * ts * LANE * 4),
    )(*_pin(x))


def _group_sum(x, *, n_out, g, src, out_dtype, name):
    s = x.shape[0]
    ts = _rows_for(LANE * 4, s)

    def body(*refs):
        acc = refs[0][...].astype(F32)
        for r in refs[1:-1]:
            acc = acc + r[...].astype(F32)
        refs[-1][...] = acc.astype(out_dtype)

    return pl.pallas_call(
        body, name=name, grid=(n_out, s // ts),
        in_specs=[pl.BlockSpec((ts, LANE), functools.partial(lambda n, i, j: (i, src(n, j)), j=j)) for j in range(g)],
        out_specs=pl.BlockSpec((ts, LANE), lambda n, i: (i, n)),
        out_shape=_hbm((s, n_out * LANE), out_dtype), compiler_params=_params(4 * g * ts * LANE * 4),
    )(*_pin(*([x] * g)))


LOG2E = 1.4426950408889634
ATTN_ROW_CHUNK = 256


def _attn_window(i, tq, s, band):
    w, r = band
    start = jnp.clip(i * tq - r, 0, s - w)
    return pl.multiple_of(start, tq), pl.multiple_of((w - tq) - (i * tq - start), LANE)


def _flash_fwd(q, k, v, table, *, hkv, g, dqk, q_cb, k_cb, v_cb, v_step, scale, tq, band, name):
    s = q.shape[0]
    n = s // tq
    hq = hkv * g
    rc = min(tq, ATTN_ROW_CHUNK)
    w = s if band is None else band[0]

    def body(*refs):
        if band is None:
            q_ref, k_ref, v_ref, o_ref, lse_ref = refs
            kw, vw = k_ref[...], v_ref[...]
        else:
            q_ref, k_ref, v_ref, t_ref, o_ref, lse_ref = refs
            start, u = _attn_window(pl.program_id(1), tq, s, band)
            kw, vw = k_ref[pl.ds(start, w), :], v_ref[pl.ds(start, w), :]
        for c in range(tq // rc):
            rows = slice(c * rc, (c + 1) * rc)
            sc = lax.dot_general(q_ref[rows, :], kw, _DIMS["nt"], preferred_element_type=F32) * (scale * LOG2E)
            if band is not None:
                sc = sc + t_ref[rows, pl.ds(u, w)]
            m = jnp.max(sc, axis=1, keepdims=True)
            p = jnp.exp2(sc - m)
            l = jnp.sum(p, axis=1, keepdims=True)
            o_ref[rows, :] = jnp.dot(p.astype(BF16), vw, preferred_element_type=F32) / l
            lse_ref[0, rows, :] = jnp.broadcast_to(m + jnp.log2(l), (rc, LANE))

    ins = [q, k, v]
    specs = [pl.BlockSpec((tq, dqk), lambda h, i: (i, q_cb + h)),
             pl.BlockSpec((s, dqk), lambda h, i: (0, k_cb + h // g)),
             pl.BlockSpec((s, LANE), lambda h, i: (0, v_cb + v_step * (h // g)))]
    if band is not None:
        ins.append(table)
        specs.append(pl.BlockSpec(table.shape, lambda h, i: (0, 0)))
    est = 4 * s * (dqk + LANE) + 6 * rc * w * 4 + 8 * tq * LANE * 4 + (0 if band is None else 2 * table.size * 4)
    return pl.pallas_call(
        body, name=name, grid=(hq, n), in_specs=specs,
        out_specs=[pl.BlockSpec((tq, LANE), lambda h, i: (i, h)), pl.BlockSpec((1, tq, LANE), lambda h, i: (h, i, 0))],
        out_shape=[_hbm((s, hq * LANE), F32), _hbm((hq, s, LANE), F32)],
        compiler_params=_params(est),
    )(*_pin(*ins))


def _flash_bwd(q, k, v, o, do, lse, table, *, hkv, g, dqk, q_cb, k_cb, v_cb, v_step, scale, tq, band, name):
    s = q.shape[0]
    n = s // tq
    hq = hkv * g
    rc = min(tq, ATTN_ROW_CHUNK)
    w = s if band is None else band[0]

    def body(*refs):
        if band is None:
            q_ref, k_ref, v_ref, o_ref, do_ref, lse_ref, dq_ref, dk_ref, dv_ref = refs
            keys = slice(None)
        else:
            q_ref, k_ref, v_ref, o_ref, do_ref, lse_ref, t_ref, dq_ref, dk_ref, dv_ref = refs
            start, u = _attn_window(pl.program_id(1), tq, s, band)
            keys = pl.ds(start, w)
        h, i = pl.program_id(0), pl.program_id(1)

        @pl.when(jnp.logical_and(h % g == 0, i == 0))
        def _():
            dk_ref[...] = jnp.zeros_like(dk_ref)
            dv_ref[...] = jnp.zeros_like(dv_ref)

        kw, vw = k_ref[keys, :], v_ref[keys, :]
        for c in range(tq // rc):
            rows = slice(c * rc, (c + 1) * rc)
            qv = q_ref[rows, :]
            dof = do_ref[rows, :]
            dov = dof.astype(BF16)
            sc = lax.dot_general(qv, kw, _DIMS["nt"], preferred_element_type=F32) * (scale * LOG2E)
            if band is not None:
                sc = sc + t_ref[rows, pl.ds(u, w)]
            p = jnp.exp2(sc - lse_ref[0, rows, 0:1])
            dp = lax.dot_general(dov, vw, _DIMS["nt"], preferred_element_type=F32)
            delta = jnp.sum(dof * o_ref[rows, :], axis=1, keepdims=True)
            ds = (p * (dp - delta) * scale).astype(BF16)
            dv_ref[keys, :] += lax.dot_general(p.astype(BF16), dov, _DIMS["tn"], preferred_element_type=F32)
            dk_ref[keys, :] += lax.dot_general(ds, qv, _DIMS["tn"], preferred_element_type=F32)
            dq_ref[rows, :] = jnp.dot(ds, kw, preferred_element_type=F32)

    ins = [q, k, v, o, do, lse]
    specs = [pl.BlockSpec((tq, dqk), lambda h, i: (i, q_cb + h)),
             pl.BlockSpec((s, dqk), lambda h, i: (0, k_cb + h // g)),
             pl.BlockSpec((s, LANE), lambda h, i: (0, v_cb + v_step * (h // g))),
             pl.BlockSpec((tq, LANE), lambda h, i: (i, h)),
             pl.BlockSpec((tq, LANE), lambda h, i: (i, h)),
             pl.BlockSpec((1, tq, LANE), lambda h, i: (h, i, 0))]
    if band is not None:
        ins.append(table)
        specs.append(pl.BlockSpec(table.shape, lambda h, i: (0, 0)))
    est = (4 + 8) * s * (dqk + LANE) + 10 * rc * w * 4 + 12 * tq * LANE * 4 + (0 if band is None else 2 * table.size * 4)
    return pl.pallas_call(
        body, name=name, grid=(hq, n), in_specs=specs,
        out_specs=[pl.BlockSpec((tq, dqk), lambda h, i: (i, h)),
                   pl.BlockSpec((s, dqk), lambda h, i: (0, h // g)),
                   pl.BlockSpec((s, LANE), lambda h, i: (0, h // g))],
        out_shape=[_hbm((s, hq * dqk), F32), _hbm((s, hkv * dqk), F32),
                   _hbm((s, hkv * LANE), F32)],
        compiler_params=_params(est),
    )(*_pin(*ins))


def _final_loss(x, gain, target, *, name):
    s, d = x.shape
    ts = _rows_for(d * 4, s, target=1024 * 1024)

    def body(x_ref, g_ref, t_ref, dx_ref, dg_ref, loss_ref, dxb_ref):
        i = pl.program_id(0)
        xv = x_ref[...]
        gv = g_ref[...]
        r = lax.rsqrt(jnp.mean(xv * xv, axis=1, keepdims=True) + EPS)
        xh = xv * r
        err = xh * gv - t_ref[...]
        dy = err / d
        dyg = dy * gv
        dx = r * (dyg - xh * jnp.mean(dyg * xh, axis=1, keepdims=True))
        dx_ref[...] = dx
        dxb_ref[...] = dx.astype(BF16)

        @pl.when(i == 0)
        def _():
            dg_ref[...] = jnp.zeros_like(dg_ref)
            loss_ref[...] = jnp.zeros_like(loss_ref)

        dg_ref[...] += jnp.sum(dy * xh, axis=0, keepdims=True)
        part = jnp.sum(jnp.mean(err * err, axis=1, keepdims=True), axis=0, keepdims=True)
        loss_ref[...] += jnp.broadcast_to(0.5 * part, (1, LANE))

    row = pl.BlockSpec((ts, d), lambda i: (i, 0))
    return pl.pallas_call(
        body, name=name, grid=(s // ts,),
        in_specs=[row, pl.BlockSpec((1, d), lambda i: (0, 0)), row],
        out_specs=[row, pl.BlockSpec((1, d), lambda i: (0, 0)), pl.BlockSpec((1, LANE), lambda i: (0, 0)), row],
        out_shape=[_hbm((s, d), F32), jax.ShapeDtypeStruct((1, d), F32),
                   jax.ShapeDtypeStruct((1, LANE), F32), _hbm((s, d), BF16)],
        compiler_params=_params(14 * ts * d * 4),
    )(*_pin(x), gain, *_pin(target))


def _cast_to_slot(x2d, me, *, layer, rows, name):
    c = x2d.shape[1]
    tr = _rows_for(c * 4, rows)
    nt = rows // tr

    def body(me_ref, x_ref, o_ref):
        o_ref[...] = x_ref[...].astype(BF16)

    return pl.pallas_call(
        body, name=name,
        grid_spec=pltpu.PrefetchScalarGridSpec(
            num_scalar_prefetch=1, grid=(nt,),
            in_specs=[pl.BlockSpec((tr, c), lambda i, me_ref: (layer * nt + i, 0))],
            out_specs=pl.BlockSpec((None, tr, c), lambda i, me_ref: (me_ref[0], i, 0))),
        out_shape=_hbm((N_CHIPS, rows, c), BF16), compiler_params=_params(6 * tr * c * 4),
    )(me, *_pin(x2d))


def _sum_parts(srcs, lands, me, *, name):
    depth = len(srcs)
    _, r, c = srcs[0].shape
    tr = _rows_for(c * 4, r, target=1024 * 1024)
    nt = r // tr

    def body(me_ref, *refs):
        o_ref = refs[-1]
        l = pl.program_id(0)
        for k in range(depth):
            @pl.when(l == k)
            def _(k=k):
                acc = refs[k][...].astype(F32)
                for p in range(3):
                    acc = acc + refs[depth + k][p].astype(F32)
                o_ref[...] = acc

    def rows_of(k):
        return lambda l, i, me_ref: jnp.where(l == k, i, jnp.where(l < k, 0, nt - 1))

    in_specs = [pl.BlockSpec((None, tr, c), functools.partial(lambda l, i, me_ref, f: (me_ref[0], f(l, i, me_ref), 0), f=rows_of(k)))
                for k in range(depth)]
    in_specs += [pl.BlockSpec((3, tr, c), functools.partial(lambda l, i, me_ref, f: (0, f(l, i, me_ref), 0), f=rows_of(k)))
                 for k in range(depth)]
    return pl.pallas_call(
        body, name=name,
        grid_spec=pltpu.PrefetchScalarGridSpec(
            num_scalar_prefetch=1, grid=(depth, nt), in_specs=in_specs,
            out_specs=pl.BlockSpec((tr, c), lambda l, i, me_ref: (l * nt + i, 0))),
        out_shape=_hbm((depth * r, c), F32), compiler_params=_params(depth * 10 * tr * c * 4),
    )(me, *_pin(*srcs, *lands))


def _adamw_math(w, g, m, v):
    m2 = ADAM_B1 * m + (1.0 - ADAM_B1) * g
    v2 = ADAM_B2 * v + (1.0 - ADAM_B2) * (g * g)
    m_hat = m2 / (1.0 - ADAM_B1 ** ADAM_STEP)
    v_hat = v2 / (1.0 - ADAM_B2 ** ADAM_STEP)
    delta = -ADAM_LR * (m_hat / (jnp.sqrt(v_hat) + ADAM_EPS) + ADAM_WD * w)
    return delta, m2, v2


def _adamw(g_a, g_b, w, m, v, *, name):
    r, c = w.shape
    tr = _rows_for(c * 4, r, target=512 * 1024)

    def body(a_ref, b_ref, w_ref, m_ref, v_ref, g_out, d_out, m_out, v_out):
        gv = a_ref[...] + b_ref[...]
        delta, m2, v2 = _adamw_math(w_ref[...], gv, m_ref[...], v_ref[...])
        g_out[...] = gv
        d_out[...] = delta
        m_out[...] = m2
        v_out[...] = v2

    spec = pl.BlockSpec((tr, c), lambda i: (i, 0))
    return pl.pallas_call(
        body, name=name, grid=(r // tr,), in_specs=[spec] * 5, out_specs=[spec] * 4,
        out_shape=[_hbm((r, c), F32)] * 4, compiler_params=_params(22 * tr * c * 4),
    )(*_pin(g_a, g_b, w, m, v))


def _small_adamw(g_all, w, m, v, *, name):
    r, c = w.shape

    def body(ga_ref, w_ref, m_ref, v_ref, g_out, d_out, m_out, v_out):
        gv = ga_ref[0]
        for j in range(1, N_DEV):
            gv = gv + ga_ref[j]
        delta, m2, v2 = _adamw_math(w_ref[...], gv, m_ref[...], v_ref[...])
        g_out[...] = gv
        d_out[...] = delta
        m_out[...] = m2
        v_out[...] = v2

    return pl.pallas_call(body, name=name, out_shape=[jax.ShapeDtypeStruct((r, c), F32)] * 4)(g_all, w, m, v)


_ANY = pl.BlockSpec(memory_space=pl.ANY)


_HBM = pl.BlockSpec(memory_space=pltpu.HBM)
_SEM = pl.BlockSpec(memory_space=pltpu.SEMAPHORE)
_EFFECT = pltpu.SideEffectType.DATAFLOW_SIDE_EFFECTING


def _peer_chips():
    x, y = lax.axis_index("x"), lax.axis_index("y")
    return 2 * x + y, [(1 - x, y), (x, 1 - y), (1 - x, 1 - y)]


def _exchange_copy(srcs, lands, send_sems, recv_sems, k, p, kind):
    c = lax.axis_index("c")
    if kind == "swap":
        return pltpu.make_async_remote_copy(
            src_ref=srcs[k], dst_ref=lands[k], send_sem=send_sems.at[k], recv_sem=recv_sems.at[k],
            device_id=(lax.axis_index("x"), lax.axis_index("y"), 1 - c), device_id_type=MESH)
    me, peers = _peer_chips()
    px, py = peers[p]
    return pltpu.make_async_remote_copy(
        src_ref=srcs[k].at[2 * px + py] if kind == "scatter" else srcs[k].at[me],
        dst_ref=lands[k].at[p] if kind == "scatter" else lands[k].at[me],
        send_sem=send_sems.at[3 * k + p], recv_sem=recv_sems.at[3 * k + p],
        device_id=(px, py, c), device_id_type=MESH)


def _exchange_start(srcs, lands, after, *, kind, name):
    n = len(srcs)
    npeer = 1 if kind == "swap" else 3
    bufs = list(srcs) + (list(lands) if lands is not None else [])
    nb = len(bufs)

    def body(*refs):
        buf_refs, send_sems, recv_sems = refs[:nb], refs[nb + 1], refs[nb + 2]
        token = refs[-1]
        s_refs = buf_refs[:n]
        l_refs = buf_refs[n:] if lands is not None else s_refs
        for k in range(n):
            for p in range(npeer):
                _exchange_copy(s_refs, l_refs, send_sems, recv_sems, k, p, kind).start()
        token[...] = jnp.zeros_like(token)

    out = pl.pallas_call(
        body, name=name,
        out_shape=(pltpu.SemaphoreType.DMA((npeer * n,)), pltpu.SemaphoreType.DMA((npeer * n,)),
                   *[pltpu.HBM(b.shape, b.dtype) for b in bufs], jax.ShapeDtypeStruct((SUBLANE, LANE), F32)),
        in_specs=[_HBM] * nb + [_ANY],
        out_specs=(_SEM, _SEM, *[_HBM] * nb, pl.BlockSpec(memory_space=pltpu.VMEM)),
        input_output_aliases={i: 2 + i for i in range(nb)},
        compiler_params=pltpu.CompilerParams(has_side_effects=_EFFECT),
    )(*[pltpu.with_memory_space_constraint(b, pltpu.HBM) for b in bufs], after)
    send_sems, recv_sems = out[0], out[1]
    thru = out[2:2 + nb]
    return send_sems, recv_sems, list(thru[:n]), (list(thru[n:]) if lands is not None else None), out[-1]


def _exchange_wait(send_sems, recv_sems, srcs, lands, after, *, kind, name):
    n = len(srcs)
    npeer = 1 if kind == "swap" else 3
    bufs = list(srcs) + (list(lands) if lands is not None else [])
    nb = len(bufs)

    def body(*refs):
        buf_refs, send_sems_ref, recv_sems_ref = refs[:nb], refs[nb], refs[nb + 1]
        s_refs = buf_refs[:n]
        l_refs = buf_refs[n:] if lands is not None else s_refs
        for k in range(n):
            for p in range(npeer):
                cp = _exchange_copy(s_refs, l_refs, send_sems_ref, recv_sems_ref, k, p, kind)
                cp.wait_send()
                cp.wait_recv()

    out = pl.pallas_call(
        body, name=name, out_shape=tuple(pltpu.HBM(b.shape, b.dtype) for b in bufs),
        in_specs=[_HBM] * nb + [_SEM, _SEM, _ANY], out_specs=tuple([_HBM] * nb),
        input_output_aliases={i: i for i in range(nb)},
        compiler_params=pltpu.CompilerParams(has_side_effects=_EFFECT),
    )(*bufs, send_sems, recv_sems, after)
    return list(out)


def _sibling_exchange(srcs, *, name):
    n = len(srcs)

    def body(*refs):
        src, out = refs[:n], refs[n:2 * n]
        send_sems, recv_sems = refs[2 * n:]
        sibling = (lax.axis_index("x"), lax.axis_index("y"), 1 - lax.axis_index("c"))
        copies = [pltpu.make_async_remote_copy(src_ref=src[k], dst_ref=out[k], send_sem=send_sems.at[k],
                                               recv_sem=recv_sems.at[k], device_id=sibling, device_id_type=MESH)
                  for k in range(n)]
        for cp in copies:
            cp.start()
        for cp in copies:
            cp.wait_recv()
        for cp in copies:
            cp.wait_send()

    return pl.pallas_call(
        body, name=name, in_specs=[_ANY] * n, out_specs=[_ANY] * n,
        out_shape=[jax.ShapeDtypeStruct(a.shape, a.dtype) for a in srcs],
        scratch_shapes=[pltpu.SemaphoreType.DMA((n,)), pltpu.SemaphoreType.DMA((n,))],
    )(*srcs)


def _all_gather_small(block, *, name):
    m_per, ncol = block.shape

    def body(x_ref, out_ref, send_sems, recv_sems, local_sem):
        x, y, c = lax.axis_index("x"), lax.axis_index("y"), lax.axis_index("c")
        me, sibling = (x, y, c), (x, y, 1 - c)
        chips = [(1 - x, y), (x, 1 - y), (1 - x, 1 - y)]

        def rows(px, py, pc):
            return out_ref.at[pl.ds((4 * px + 2 * py + pc) * m_per, m_per), :]

        def copy(k, blk, to, src=None):
            return pltpu.make_async_remote_copy(
                src_ref=rows(*blk) if src is None else src, dst_ref=rows(*blk),
                send_sem=send_sems.at[k], recv_sem=recv_sems.at[k], device_id=to, device_id_type=MESH)

        mine = pltpu.make_async_copy(x_ref, rows(*me), local_sem)
        mine.start()
        first = [copy(0, me, sibling, src=x_ref)]
        first += [copy(1 + j, me, (*chip, c), src=x_ref) for j, chip in enumerate(chips)]
        for cp in first:
            cp.start()
        passed = [copy(4 + j, (*chip, c), sibling) for j, chip in enumerate(chips)]
        for j, chip in enumerate(chips):
            copy(1 + j, (*chip, c), me).wait_recv()
            passed[j].start()
        copy(0, sibling, me).wait_recv()
        for j, chip in enumerate(chips):
            copy(4 + j, (*chip, 1 - c), me).wait_recv()
        for cp in first + passed:
            cp.wait_send()
        mine.wait()

    return pl.pallas_call(
        body, name=name, out_shape=jax.ShapeDtypeStruct((N_DEV * m_per, ncol), block.dtype),
        in_specs=[pl.BlockSpec(memory_space=pltpu.VMEM)], out_specs=pl.BlockSpec(memory_space=pltpu.VMEM),
        scratch_shapes=[pltpu.SemaphoreType.DMA((7,)), pltpu.SemaphoreType.DMA((7,)), pltpu.SemaphoreType.DMA],
    )(block)


def _rope_angles(pos, dim):
    inv = ROPE_THETA ** (-jnp.arange(0, dim, 2, dtype=F32) / dim)
    return pos.astype(F32)[:, None] * inv[None, :]


def _rope_tables(s):
    pos = jnp.arange(s, dtype=jnp.int32)
    rows = s // GRID_W
    row = jnp.repeat(jnp.arange(rows, dtype=jnp.int32), GRID_W)
    col = jnp.tile(jnp.arange(GRID_W, dtype=jnp.int32), rows)
    a1 = _rope_angles(pos, HEAD_DIM)
    aa = _rope_angles(pos, A_ROPE)
    ar = _rope_angles(row, HEAD_DIM // 2)
    ac = _rope_angles(col, HEAD_DIM // 2)
    one = jnp.ones((s, LANE), F32)
    zero = jnp.zeros((s, LANE), F32)
    pad = LANE - A_ROPE
    cos_a = jnp.concatenate([one, jnp.cos(aa), jnp.cos(aa), jnp.ones((s, pad), F32)], axis=1)
    sin_a = jnp.concatenate([zero, -jnp.sin(aa), jnp.sin(aa), jnp.zeros((s, pad), F32)], axis=1)
    cos_b = jnp.concatenate([jnp.cos(a1), jnp.cos(a1)], axis=1)
    sin_b = jnp.concatenate([-jnp.sin(a1), jnp.sin(a1)], axis=1)
    cos_c = jnp.concatenate([jnp.cos(ar), jnp.cos(ar), jnp.cos(ac), jnp.cos(ac)], axis=1)
    sin_c = jnp.concatenate([-jnp.sin(ar), jnp.sin(ar), -jnp.sin(ac), jnp.sin(ac)], axis=1)
    return (cos_a, sin_a), (cos_b, sin_b), (cos_c, sin_c)


def _band_table(tq, s):
    reach = max((win // (2 * d)) * d for win, d in B_PATTERNS)
    r = -(-reach // tq) * tq
    w = min(s, tq + 2 * r)
    j = jnp.arange(tq, dtype=jnp.int32)[:, None]
    x = jnp.arange(2 * w - tq, dtype=jnp.int32)[None, :]
    rel = x - (w - tq) - j
    mult = jnp.zeros(rel.shape, F32)
    for win, d in B_PATTERNS:
        mult = mult + jnp.logical_and(rel % d == 0, jnp.abs(rel) <= (win // (2 * d)) * d).astype(F32)
    return jnp.where(mult > 0, jnp.log2(jnp.maximum(mult, 1.0)), NEG), (w, r)


_BIG = ("w_in", "a_w_uq", "a_w_ukv", "w_out", "w_gate", "w_up", "w_down")
_SMALL = ("attn_norm", "a_q_norm", "a_kv_norm", "c_q_norm", "c_k_norm", "out_norm", "ffn_norm", "final_norm")
_WEIGHTS = ("attn_norm", "w_in", "a_q_norm", "a_w_uq", "a_kv_norm", "a_w_ukv", "c_q_norm", "c_k_norm", "out_norm",
            "w_out", "ffn_norm", "w_gate", "w_up", "w_down", "final_norm")


_ATTN = ("w_in", "a_w_uq", "a_w_ukv")
_FFN = ("w_out", "w_gate", "w_up", "w_down")


def _from_cols(a):
    return jnp.transpose(a, (1, 0, 2)).reshape(a.shape[1], N_CHIPS * a.shape[2])


def _from_rows(a):
    return a.reshape(N_CHIPS * a.shape[1], a.shape[2])


def _to_cols(a):
    return jnp.transpose(a.reshape(a.shape[0], N_CHIPS, a.shape[1] // N_CHIPS), (1, 0, 2))


def _to_rows(a):
    return a.reshape(N_CHIPS, a.shape[0] // N_CHIPS, a.shape[1])


def _assemble_attn(gw):
    w_in, uq, ukv = _from_cols(gw[0]), _from_cols(gw[1]), _from_cols(gw[2])
    d = w_in.shape[0]
    w_all = jnp.concatenate([w_in[:, :IN_A], jnp.zeros((d, A_PAD - IN_A), BF16), w_in[:, IN_A:]], axis=1)
    uq = uq.reshape(A_Q_RANK, A_HEADS, A_NOPE + A_ROPE)
    uq = jnp.pad(uq, ((0, 0), (0, 0), (0, A_QK - A_NOPE - A_ROPE))).reshape(A_Q_RANK, A_HEADS * A_QK)
    return dict(w_all=w_all, uq=uq, ukv=ukv)


def _assemble_ffn(gw):
    return dict(w_out=_from_rows(gw[0]), w_gate=gw[1], w_up=gw[2], w_down=_from_rows(gw[3]))


def _split_attn_grads(gl):
    w_all = gl["w_all"]
    w_in = jnp.concatenate([w_all[:, :IN_A], w_all[:, A_PAD:]], axis=1)
    uq = gl["uq"].reshape(A_Q_RANK, A_HEADS, A_QK)[:, :, :A_NOPE + A_ROPE].reshape(A_Q_RANK, A_HEADS * (A_NOPE + A_ROPE))
    return [_to_cols(w_in), _to_cols(uq), _to_cols(gl["ukv"])]


def _split_ffn_grads(gl):
    return [_to_rows(gl["w_out"]), gl["w_gate"], gl["w_up"], _to_rows(gl["w_down"])]


def _tie(a, token):
    return a + token[0:1, 0:1]


def _layer_fwd(x, wl, ffn_weights, sm, tabs, bias, t):
    s = x.shape[0]
    (cos_a, sin_a), (cos_b, sin_b), (cos_c, sin_c) = tabs
    h = _norm_fwd(x, sm["attn_norm"], wb=x.shape[1], cb=0, nb=1, shared_gain=True, out_dtype=BF16, name="attn_norm_fwd")
    p = _matmul(h, wl["w_all"], mode="nn", out_dtype=F32, name="in_proj", tm=1024, tn=640)
    cq_n = _norm_fwd(p, sm["a_q_norm"], wb=A_Q_RANK, cb=0, nb=1, shared_gain=True, out_dtype=BF16, name="a_q_norm_fwd")
    ckv_n = _norm_fwd(p, sm["a_kv_norm"], wb=A_KV_RANK, cb=1, nb=1, shared_gain=True, out_dtype=BF16, name="a_kv_norm_fwd")
    qa_raw = _matmul(cq_n, wl["uq"], mode="nn", out_dtype=F32, name="a_uq", tm=1024, tn=1024)
    kv = _matmul(ckv_n, wl["ukv"], mode="nn", out_dtype=BF16, name="a_ukv", tm=1024, tn=1024)
    k_nope = kv.reshape(s, A_HEADS, 2, LANE)[:, :, 0].astype(F32)
    k_rope = jnp.broadcast_to(p[:, PB_KR * LANE:(PB_KR + 1) * LANE][:, None, :], (s, A_HEADS, LANE))
    ka_raw = jnp.stack([k_nope, k_rope], axis=2).reshape(s, A_HEADS * A_QK)
    qa = _rope(qa_raw, cos_a, sin_a, tw=A_QK, cb=0, nb=A_HEADS, half=A_ROPE // 2, sign=1, out_dtype=BF16, name="a_rope_q")
    ka = _rope(ka_raw, cos_a, sin_a, tw=A_QK, cb=0, nb=A_HEADS, half=A_ROPE // 2, sign=1, out_dtype=BF16, name="a_rope_k")
    oa, lse_a = _flash_fwd(qa, ka, kv, None, hkv=A_HEADS, g=1, dqk=A_QK, q_cb=0, k_cb=0, v_cb=1, v_step=2,
                           scale=(A_NOPE + A_ROPE) ** -0.5, tq=t, band=None, name="a_flash_fwd")
    table, band = bias
    qb = _rope(p, cos_b, sin_b, tw=LANE, cb=PB_BQ, nb=B_HEADS, half=HEAD_DIM // 2, sign=1, out_dtype=BF16, name="b_rope_q")
    kb = _rope(p, cos_b, sin_b, tw=LANE, cb=PB_BK, nb=B_HEADS, half=HEAD_DIM // 2, sign=1, out_dtype=BF16, name="b_rope_k")
    vb = _cast_cols(p, cb=PB_BV, nb=B_HEADS, name="b_cast_v")
    ob, lse_b = _flash_fwd(qb, kb, vb, table, hkv=B_HEADS, g=1, dqk=LANE, q_cb=0, k_cb=0, v_cb=0, v_step=1,
                           scale=HEAD_DIM ** -0.5, tq=t, band=band, name="b_flash_fwd")
    qn = _norm_fwd(p, sm["c_q_norm"], wb=LANE, cb=PB_CQH, nb=C_HEADS, shared_gain=True, out_dtype=F32, name="c_q_norm_fwd")
    kn = _norm_fwd(p, sm["c_k_norm"], wb=LANE, cb=PB_CKH, nb=C_KV_HEADS, shared_gain=True, out_dtype=F32, name="c_k_norm_fwd")
    qc = _rope(qn, cos_c, sin_c, tw=LANE, cb=0, nb=C_HEADS, half=HEAD_DIM // 4, sign=1, out_dtype=BF16, name="c_rope_q")
    kc = _rope(kn, cos_c, sin_c, tw=LANE, cb=0, nb=C_KV_HEADS, half=HEAD_DIM // 4, sign=1, out_dtype=BF16, name="c_rope_k")
    vc = _cast_cols(p, cb=PB_CVH, nb=C_KV_HEADS, name="c_cast_v")
    oc, lse_c = _flash_fwd(qc, kc, vc, None, hkv=C_KV_HEADS, g=C_GROUP, dqk=LANE, q_cb=0, k_cb=0, v_cb=0, v_step=1,
                           scale=HEAD_DIM ** -0.5, tq=t, band=None, name="c_flash_fwd")
    g_out = sm["out_norm"]
    ga, gb, gc = g_out[:, :A_WIDTH], g_out[:, A_WIDTH:A_WIDTH + B_WIDTH], g_out[:, A_WIDTH + B_WIDTH:]
    ya = _norm_fwd(oa, ga, wb=A_WIDTH, cb=0, nb=1, shared_gain=True, out_dtype=BF16, name="out_norm_a_fwd")
    yb = _norm_fwd(ob, gb, wb=B_WIDTH, cb=0, nb=1, shared_gain=True, out_dtype=BF16, name="out_norm_b_fwd")
    yc = _norm_fwd(oc, gc, wb=C_WIDTH, cb=0, nb=1, shared_gain=True, out_dtype=BF16, name="out_norm_c_fwd")
    y = jnp.concatenate([ya, yb, yc], axis=1)
    wl = {**wl, **ffn_weights(y)}
    x1 = _matmul(y, wl["w_out"], mode="nn", out_dtype=F32, name="out_proj", add=x, tm=1024, tn=512)
    h2 = _norm_fwd(x1, sm["ffn_norm"], wb=x.shape[1], cb=0, nb=1, shared_gain=True, out_dtype=BF16, name="ffn_norm_fwd")
    gate, up, act = _ffn_up(h2, wl["w_gate"], wl["w_up"], name="ffn_up")
    x2 =_matmul(act, wl["w_down"], mode="nn", out_dtype=F32, name="ffn_down", add=x1, tm=512, tn=512)
    saved = dict(x=x, h=h, p=p, cq_n=cq_n, ckv_n=ckv_n, kv=kv, qa=qa, ka=ka, oa=oa, lse_a=lse_a, qb=qb, kb=kb, vb=vb, ob=ob,
                 lse_b=lse_b, qc=qc, kc=kc, vc=vc, oc=oc, lse_c=lse_c, y=y, x1=x1, h2=h2, gate=gate, up=up, act=act)
    return x2, saved, wl


def _layer_bwd(dx2, dx2b, sv, wl, sm, tabs, bias, t, send_ffn, send_attn):
    s, d = dx2.shape
    (cos_a, sin_a), (cos_b, sin_b), (cos_c, sin_c) = tabs
    gw, gs = {}, {}
    dgate, dup = _ffn_down_dx(dx2b, wl["w_down"], sv["gate"], sv["up"], name="ffn_down_dx")
    gw["w_down"] = _matmul(sv["act"], dx2b, mode="tn", out_dtype=BF16, name="ffn_down_dw", tm=512, tn=512)
    dh2 = _ffn_up_dx(dgate, dup, wl["w_gate"], wl["w_up"], name="ffn_up_dx")
    gw["w_gate"] = _matmul(sv["h2"], dgate, mode="tn", out_dtype=BF16, name="ffn_gate_dw", tm=512, col_shards=True)
    gw["w_up"] = _matmul(sv["h2"], dup, mode="tn", out_dtype=BF16, name="ffn_up_dw", tm=512, col_shards=True)
    dx1, gs["ffn_norm"], dx1b = _norm_bwd(sv["x1"], sm["ffn_norm"], dh2, wb=d, cb=0, nb=1, shared_gain=True,
                                          out_dtype=F32, name="ffn_norm_bwd", add=dx2, bf16_copy=True)
    dy = _matmul(dx1b, wl["w_out"], mode="nt", out_dtype=F32, name="out_proj_dx", tm=512, tn=512)
    gw["w_out"] = _matmul(sv["y"], dx1b, mode="tn", out_dtype=BF16, name="out_proj_dw", tm=512, tn=512)
    token = send_ffn(gw)
    g_out = _tie(sm["out_norm"], token)
    ga, gb, gc = g_out[:, :A_WIDTH], g_out[:, A_WIDTH:A_WIDTH + B_WIDTH], g_out[:, A_WIDTH + B_WIDTH:]
    dya, dyb, dyc = dy[:, :A_WIDTH], dy[:, A_WIDTH:A_WIDTH + B_WIDTH], dy[:, A_WIDTH + B_WIDTH:]
    doa, dga = _norm_bwd(sv["oa"], ga, dya, wb=A_WIDTH, cb=0, nb=1, shared_gain=True, out_dtype=F32, name="out_norm_a_bwd")
    dob, dgb = _norm_bwd(sv["ob"], gb, dyb, wb=B_WIDTH, cb=0, nb=1, shared_gain=True, out_dtype=F32, name="out_norm_b_bwd")
    doc, dgc = _norm_bwd(sv["oc"], gc, dyc, wb=C_WIDTH, cb=0, nb=1, shared_gain=True, out_dtype=F32, name="out_norm_c_bwd")
    gs["out_norm"] = jnp.concatenate([dga, dgb, dgc], axis=1)
    p = sv["p"]
    dqc, dkc, dvc = _flash_bwd(sv["qc"], sv["kc"], sv["vc"], sv["oc"], doc, sv["lse_c"], None, hkv=C_KV_HEADS,
                               g=C_GROUP, dqk=LANE, q_cb=0, k_cb=0, v_cb=0, v_step=1, scale=HEAD_DIM ** -0.5, tq=t,
                               band=None, name="c_flash_bwd")
    dqn = _rope(dqc, cos_c, sin_c, tw=LANE, cb=0, nb=C_HEADS, half=HEAD_DIM // 4, sign=-1, out_dtype=F32, name="c_rope_q_bwd")
    dkn = _rope(dkc, cos_c, sin_c, tw=LANE, cb=0, nb=C_KV_HEADS, half=HEAD_DIM // 4, sign=-1, out_dtype=F32, name="c_rope_k_bwd")
    dpcq, gs["c_q_norm"] = _norm_bwd(p, sm["c_q_norm"], dqn, wb=LANE, cb=PB_CQH, nb=C_HEADS, shared_gain=True,
                                     out_dtype=BF16, name="c_q_norm_bwd")
    dpck, gs["c_k_norm"] = _norm_bwd(p, sm["c_k_norm"], dkn, wb=LANE, cb=PB_CKH, nb=C_KV_HEADS, shared_gain=True,
                                     out_dtype=BF16, name="c_k_norm_bwd")
    table, band = bias
    dqb, dkb, dvb = _flash_bwd(sv["qb"], sv["kb"], sv["vb"], sv["ob"], dob, sv["lse_b"], table, hkv=B_HEADS, g=1,
                               dqk=LANE, q_cb=0, k_cb=0, v_cb=0, v_step=1, scale=HEAD_DIM ** -0.5, tq=t, band=band,
                               name="b_flash_bwd")
    dpbq = _rope(dqb, cos_b, sin_b, tw=LANE, cb=0, nb=B_HEADS, half=HEAD_DIM // 2, sign=-1, out_dtype=BF16, name="b_rope_q_bwd")
    dpbk = _rope(dkb, cos_b, sin_b, tw=LANE, cb=0, nb=B_HEADS, half=HEAD_DIM // 2, sign=-1, out_dtype=BF16, name="b_rope_k_bwd")
    dqa, dka, dva = _flash_bwd(sv["qa"], sv["ka"], sv["kv"], sv["oa"], doa, sv["lse_a"], None, hkv=A_HEADS, g=1,
                               dqk=A_QK, q_cb=0, k_cb=0, v_cb=1, v_step=2, scale=(A_NOPE + A_ROPE) ** -0.5, tq=t,
                               band=None, name="a_flash_bwd")
    dqa_raw = _rope(dqa, cos_a, sin_a, tw=A_QK, cb=0, nb=A_HEADS, half=A_ROPE // 2, sign=-1, out_dtype=BF16, name="a_rope_q_bwd")
    dka_raw = _rope(dka, cos_a, sin_a, tw=A_QK, cb=0, nb=A_HEADS, half=A_ROPE // 2, sign=-1, out_dtype=BF16, name="a_rope_k_bwd")
    dkr = _group_sum(dka_raw, n_out=1, g=A_HEADS, src=lambda n, j: 2 * j + 1, out_dtype=BF16, name="a_k_rope_sum")
    dkv = jnp.stack([dka_raw.reshape(s, A_HEADS, 2, LANE)[:, :, 0], dva.reshape(s, A_HEADS, LANE).astype(BF16)], axis=2)
    dkv = dkv.reshape(s, A_HEADS * 2 * LANE)
    dckv_n = _matmul(dkv, wl["ukv"], mode="nt", out_dtype=F32, name="a_ukv_dx", tm=1024, tn=512)
    gw["ukv"] = _matmul(sv["ckv_n"], dkv, mode="tn", out_dtype=BF16, name="a_ukv_dw", tm=512, tn=1024)
    dcq_n = _matmul(dqa_raw, wl["uq"], mode="nt", out_dtype=F32, name="a_uq_dx", tm=1024, tn=512)
    gw["uq"] = _matmul(sv["cq_n"], dqa_raw, mode="tn", out_dtype=BF16, name="a_uq_dw", tm=512, tn=1024)
    dcq, gs["a_q_norm"] = _norm_bwd(p, sm["a_q_norm"], dcq_n, wb=A_Q_RANK, cb=0, nb=1, shared_gain=True, out_dtype=BF16,
                                    name="a_q_norm_bwd")
    dckv, gs["a_kv_norm"] = _norm_bwd(p, sm["a_kv_norm"], dckv_n, wb=A_KV_RANK, cb=1, nb=1, shared_gain=True,
                                      out_dtype=BF16, name="a_kv_norm_bwd")
    dp = jnp.concatenate([dcq, dckv, dkr, jnp.zeros((s, A_PAD - (PB_KR + 1) * LANE), BF16), dpbq, dpbk,
                          dvb.astype(BF16), dpcq, dpck, dvc.astype(BF16)], axis=1)
    gw["w_all"] = _matmul(sv["h"], dp, mode="tn", out_dtype=BF16, name="in_proj_dw", tm=512, tn=640)
    token = send_attn(gw)
    dh = _matmul(dp, wl["w_all"], mode="nt", out_dtype=F32, name="in_proj_dx", tm=512, tn=512, after=token)
    dx, gs["attn_norm"], dxb = _norm_bwd(sv["x"], sm["attn_norm"], dh, wb=d, cb=0, nb=1, shared_gain=True,
                                         out_dtype=F32, name="attn_norm_bwd", add=dx1, bf16_copy=True)
    return dx, dxb, gs, token


def _pack_small(vals):
    flat = jnp.concatenate([vals[n].reshape(-1).astype(F32) for n in _SMALL])
    tile = SUBLANE * LANE
    padded = -(-flat.shape[0] // tile) * tile
    return jnp.pad(flat, (0, padded - flat.shape[0])).reshape(padded // LANE, LANE)


def _unpack_small(packed, like):
    flat = packed.reshape(-1)
    out, off = {}, 0
    for n in _SMALL:
        size = math.prod(like[n].shape)
        out[n] = flat[off:off + size].reshape(like[n].shape)
        off += size
    return out


def kernel(x, attn_norm, w_in, a_q_norm, a_w_uq, a_kv_norm, a_w_ukv, c_q_norm, c_k_norm, out_norm, w_out, ffn_norm, w_gate, w_up, w_down, final_norm, loss_target, m_attn_norm, m_w_in, m_a_q_norm, m_a_w_uq, m_a_kv_norm, m_a_w_ukv, m_c_q_norm, m_c_k_norm, m_out_norm, m_w_out, m_ffn_norm, m_w_gate, m_w_up, m_w_down, m_final_norm, v_attn_norm, v_w_in, v_a_q_norm, v_a_w_uq, v_a_kv_norm, v_a_w_ukv, v_c_q_norm, v_c_k_norm, v_out_norm, v_w_out, v_ffn_norm, v_w_gate, v_w_up, v_w_down, v_final_norm):
    w = dict(attn_norm=attn_norm, w_in=w_in, a_q_norm=a_q_norm, a_w_uq=a_w_uq, a_kv_norm=a_kv_norm, a_w_ukv=a_w_ukv,
             c_q_norm=c_q_norm, c_k_norm=c_k_norm, out_norm=out_norm, w_out=w_out, ffn_norm=ffn_norm, w_gate=w_gate,
             w_up=w_up, w_down=w_down, final_norm=final_norm)
    m = dict(attn_norm=m_attn_norm, w_in=m_w_in, a_q_norm=m_a_q_norm, a_w_uq=m_a_w_uq, a_kv_norm=m_a_kv_norm,
             a_w_ukv=m_a_w_ukv, c_q_norm=m_c_q_norm, c_k_norm=m_c_k_norm, out_norm=m_out_norm, w_out=m_w_out,
             ffn_norm=m_ffn_norm, w_gate=m_w_gate, w_up=m_w_up, w_down=m_w_down, final_norm=m_final_norm)
    v = dict(attn_norm=v_attn_norm, w_in=v_w_in, a_q_norm=v_a_q_norm, a_w_uq=v_a_w_uq, a_kv_norm=v_a_kv_norm,
             a_w_ukv=v_a_w_ukv, c_q_norm=v_c_q_norm, c_k_norm=v_c_k_norm, out_norm=v_out_norm, w_out=v_w_out,
             ffn_norm=v_ffn_norm, w_gate=v_w_gate, w_up=v_w_up, w_down=v_w_down, final_norm=v_final_norm)
    _, s, d = x.shape
    depth = attn_norm.shape[0]
    t = _pick(s, 512)

    me = (2 * lax.axis_index("x") + lax.axis_index("y")).astype(jnp.int32).reshape(1)

    gathers, after = {}, me
    for l in range(depth):
        for group, names in (("attn", _ATTN), ("ffn", _FFN)):
            bufs = [_cast_to_slot(w[n].reshape(-1, w[n].shape[-1]), me, layer=l, rows=w[n].shape[1], name=f"cast_{n}")
                    for n in names]
            send_sems, recv_sems, bufs, _, after = _exchange_start(bufs, None, after, kind="gather",
                                                                   name=f"gather_start_{group}{l}")
            gathers[group, l] = (send_sems, recv_sems, bufs)
    all_started = after

    def gathered(group, l, after):
        send_sems, recv_sems, bufs = gathers[group, l]
        return _exchange_wait(send_sems, recv_sems, bufs, None, after, kind="gather", name=f"gather_wait_{group}{l}")

    tabs = _rope_tables(s)
    bias = _band_table(t, s)

    xs = x.reshape(s, d)
    saved, wls, sms = [], [], []
    for l in range(depth):
        wl = _assemble_attn(gathered("attn", l, all_started if l == 0 else xs))
        sm = {n: w[n][l][None, :] for n in _SMALL if n != "final_norm"}
        xs, sv, wl = _layer_fwd(xs, wl, lambda after, l=l: _assemble_ffn(gathered("ffn", l, after)), sm, tabs, bias, t)
        saved.append(sv)
        wls.append(wl)
        sms.append(sm)
    dx, g_final, loss_row, dxb = _final_loss(xs, final_norm[None, :], loss_target.reshape(s, d), name="final_loss")
    loss = lax.psum(loss_row[0, 0], ("x", "y", "c"))

    sends = {}

    def send(group, l, srcs, after):
        lands = [lax.empty((3,) + a.shape[1:], BF16) for a in srcs]
        send_sems, recv_sems, srcs, lands, token = _exchange_start(srcs, lands, after, kind="scatter",
                                                                   name=f"scatter_start_{group}{l}")
        sends[group, l] = (send_sems, recv_sems, srcs, lands)
        return token

    gs_layers, token = [None] * depth, all_started
    for l in reversed(range(depth)):
        dx, dxb, gs_layers[l], token = _layer_bwd(
            dx, dxb, saved[l], wls[l], sms[l], tabs, bias, t,
            lambda gw, l=l, tk=token: send("ffn", l, _split_ffn_grads(gw), tk),
            lambda gw, l=l: send("attn", l, _split_attn_grads(gw), dx))
    grad_x = dx.reshape(x.shape)

    srcs, lands = {}, {}

    def arrive(key, after):
        send_sems, recv_sems, s_bufs, l_bufs = sends[key]
        got = _exchange_wait(send_sems, recv_sems, s_bufs, l_bufs, after, kind="scatter",
                             name=f"scatter_wait_{key[0]}{key[1]}")
        for k, n in enumerate(_ATTN if key[0] == "attn" else _FFN):
            srcs[n, key[1]], lands[n, key[1]] = got[k], got[len(s_bufs) + k]

    def summed(names):
        return [_sum_parts([srcs[n, l] for l in range(depth)], [lands[n, l] for l in range(depth)], me, name="sum_" + n)
                for n in names]

    last = ("attn", 0)
    for key in sends:
        if key != last:
            arrive(key, token)
    sums_ffn = summed(_FFN)
    swap = _exchange_start(sums_ffn, [lax.empty(a.shape, F32) for a in sums_ffn], token, kind="swap",
                           name="swap_start_ffn")
    arrive(last, swap[4])
    sums_attn = summed(_ATTN)
    sib_attn = list(_sibling_exchange(sums_attn, name="swap_core_sums_attn"))
    swapped = _exchange_wait(swap[0], swap[1], swap[2], swap[3], sib_attn[0], kind="swap", name="swap_wait_ffn")
    mine_of = dict(zip(_FFN + _ATTN, swapped[:len(_FFN)] + sums_attn))
    other_of = dict(zip(_FFN + _ATTN, swapped[len(_FFN):] + sib_attn))
    grads, deltas, new_m, new_v = {}, {}, {}, {}
    for n in _BIG:
        mine, other = mine_of[n], other_of[n]
        shp = w[n].shape
        two_d = (-1, shp[-1])
        res = _adamw(mine, other, w[n].reshape(two_d), m[n].reshape(two_d), v[n].reshape(two_d), name="adamw_" + n)
        grads[n], deltas[n], new_m[n], new_v[n] = [r.reshape(shp) for r in res]

    gsm = {n: jnp.stack([gs_layers[l][n][0] for l in range(depth)]) for n in _SMALL if n != "final_norm"}
    gsm["final_norm"] = g_final[0]
    packed = _pack_small(gsm)
    everyone = _all_gather_small(packed, name="gather_gain_grads").reshape(N_DEV, packed.shape[0], LANE)
    res = _small_adamw(everyone, _pack_small(w), _pack_small(m), _pack_small(v), name="adamw_gains")
    for dst, r in zip((grads, deltas, new_m, new_v), res):
        dst.update(_unpack_small(r, w))

    return (loss, grad_x, *[grads[n] for n in _WEIGHTS], *[deltas[n] for n in _WEIGHTS],
            *[new_m[n] for n in _WEIGHTS], *[new_v[n] for n in _WEIGHTS])
```

```python
import functools
import math

import jax
import jax.numpy as jnp
import numpy as np
from jax import lax
from jax.experimental import pallas as pl
from jax.experimental.pallas import tpu as pltpu

F32 = jnp.float32
BF16 = jnp.bfloat16
MESH = pl.DeviceIdType.MESH

HEAD_DIM = 128
ROPE_THETA = 10000.0
GRID_W = 64
EPS = 1e-6
NEG = -1e30
A_HEADS, A_Q_RANK, A_KV_RANK, A_NOPE, A_ROPE, A_V = 4, 512, 512, 128, 64, 128
B_HEADS = 6
B_PATTERNS = ((128, 1), (512, 4), (2048, 16))
C_HEADS, C_KV_HEADS = 6, 2
C_GROUP = C_HEADS // C_KV_HEADS
A_WIDTH, B_WIDTH, C_WIDTH = A_HEADS * A_V, B_HEADS * HEAD_DIM, C_HEADS * HEAD_DIM
IN_A = A_Q_RANK + A_KV_RANK + A_ROPE
IN_B = 3 * B_WIDTH
IN_C = C_WIDTH + 2 * C_KV_HEADS * HEAD_DIM
ADAM_LR, ADAM_B1, ADAM_B2, ADAM_EPS, ADAM_WD, ADAM_STEP = 0.001, 0.9, 0.999, 1e-08, 0.01, 10

LANE = 128
SUBLANE = 8
VMEM_BYTES_V7X = 64 * 1024 * 1024
VMEM_LIMIT_CAP = VMEM_BYTES_V7X - 8 * 1024 * 1024
N_CHIPS = 4
N_DEV = 8

A_PAD = 12 * LANE
PB_CQ, PB_CKV, PB_KR = 0, 4, 8
PB_BQ, PB_BK, PB_BV = 12, 18, 24
PB_CQH, PB_CKH, PB_CVH = 30, 36, 38
NP = 40 * LANE
A_QK = 2 * LANE


def _pick(n, cap, mult=LANE):
    if n <= cap:
        return n
    t = cap - cap % mult
    while t >= mult:
        if n % t == 0:
            return t
        t -= mult
    return n


def _rows_for(width_bytes, n_rows, target=2 * 1024 * 1024):
    return _pick(n_rows, max(SUBLANE, target // max(width_bytes, 1)), SUBLANE)


def _params(est_bytes):
    limit = int(min(max(est_bytes + (4 << 20), 32 << 20), VMEM_LIMIT_CAP))
    return pltpu.CompilerParams(vmem_limit_bytes=limit)


def _isz(x):
    return jnp.dtype(x.dtype).itemsize


def _hbm(shape, dtype):
    return pltpu.HBM(shape, dtype)


def _pin(*arrays):
    return [pltpu.with_memory_space_constraint(a, pltpu.HBM) for a in arrays]


_DIMS = {"nn": (((1,), (0,)), ((), ())), "nt": (((1,), (1,)), ((), ())), "tn": (((0,), (0,)), ((), ()))}


def _matmul(a, b, *, mode, out_dtype, name, add=None, tm=512, tn=512, col_shards=False, after=None):
    if mode == "tn":
        (k, m), (k2, n) = a.shape, b.shape
    elif mode == "nt":
        (m, k), (n, k2) = a.shape, b.shape
    else:
        (m, k), (k2, n) = a.shape, b.shape
    assert k == k2, (a.shape, b.shape, mode)
    tm, tn = _pick(m, tm), (n // N_CHIPS if col_shards else _pick(n, tn))
    a_spec = pl.BlockSpec((k, tm), lambda i, j: (0, i)) if mode == "tn" else pl.BlockSpec((tm, k), lambda i, j: (i, 0))
    b_spec = pl.BlockSpec((tn, k), lambda i, j: (j, 0)) if mode == "nt" else pl.BlockSpec((k, tn), lambda i, j: (0, j))
    o_spec = pl.BlockSpec((None, tm, tn), lambda i, j: (j, i, 0)) if col_shards else pl.BlockSpec((tm, tn), lambda i, j: (i, j))
    dims = _DIMS[mode]

    def body(*refs):
        a_ref, b_ref, o_ref = refs[0], refs[1], refs[-1]
        acc = lax.dot_general(a_ref[...].astype(BF16), b_ref[...].astype(BF16), dims, preferred_element_type=F32)
        if add is not None:
            acc = acc + refs[2][...].astype(F32)
        o_ref[...] = acc.astype(out_dtype)

    ins, specs = [a, b], [a_spec, b_spec]
    if add is not None:
        ins.append(add)
        specs.append(o_spec)
    if after is not None:
        ins.append(after)
        specs.append(pl.BlockSpec(memory_space=pl.ANY))
    est = 2 * (tm * k * _isz(a) + tn * k * _isz(b) + tm * tn * (jnp.dtype(out_dtype).itemsize + (4 if add is not None else 0)))
    est += (tm + tn) * k * 2 + 2 * tm * tn * 4
    return pl.pallas_call(
        body, name=name, grid=(m // tm, n // tn), in_specs=specs, out_specs=o_spec,
        out_shape=_hbm((N_CHIPS, m, tn) if col_shards else (m, n), out_dtype),
        compiler_params=_params(est),
    )(*_pin(*ins))


def _ffn_up(h, wg, wu, *, name):
    s, d = h.shape
    _, _, c = wg.shape
    tm = _pick(s, 512, SUBLANE)

    def body(h_ref, wg_ref, wu_ref, g_ref, u_ref, a_ref):
        hv = h_ref[...]
        gv = jnp.dot(hv, wg_ref[...], preferred_element_type=F32)
        uv = jnp.dot(hv, wu_ref[...], preferred_element_type=F32)
        g_ref[...] = gv.astype(BF16)
        u_ref[...] = uv.astype(BF16)
        a_ref[...] = (gv / (1.0 + jnp.exp(-gv)) * uv).astype(BF16)

    w_spec = pl.BlockSpec((None, d, c), lambda j, i: (j, 0, 0))
    o_spec = pl.BlockSpec((tm, c), lambda j, i: (i, j))
    est = 2 * (tm * d * 2 + 2 * d * c * 2 + tm * c * 10) + 4 * tm * c * 4
    return pl.pallas_call(
        body, name=name, grid=(N_CHIPS, s // tm), in_specs=[pl.BlockSpec((tm, d), lambda j, i: (i, 0)), w_spec, w_spec],
        out_specs=[o_spec, o_spec, o_spec],
        out_shape=[_hbm((s, N_CHIPS * c), BF16)] * 3,
        compiler_params=_params(est),
    )(*_pin(h, wg, wu))


def _ffn_down_dx(dx, w_down, gate, up, *, name):
    s, d = dx.shape
    f = w_down.shape[0]
    tm, tn = _pick(s, 512, SUBLANE), _pick(f, 512)

    def body(dx_ref, w_ref, g_ref, u_ref, dg_ref, du_ref):
        dact = lax.dot_general(dx_ref[...], w_ref[...], _DIMS["nt"], preferred_element_type=F32)
        gv, uv = g_ref[...].astype(F32), u_ref[...].astype(F32)
        sig = 1.0 / (1.0 + jnp.exp(-gv))
        dg_ref[...] = (dact * uv * (sig * (1.0 + gv * (1.0 - sig)))).astype(BF16)
        du_ref[...] = (dact * (gv * sig)).astype(BF16)

    t_spec = pl.BlockSpec((tm, tn), lambda i, j: (i, j))
    est = 2 * (tm * d * 2 + tn * d * 2 + tm * tn * 12) + 6 * tm * tn * 4
    return pl.pallas_call(
        body, name=name, grid=(s // tm, f // tn),
        in_specs=[pl.BlockSpec((tm, d), lambda i, j: (i, 0)), pl.BlockSpec((tn, d), lambda i, j: (j, 0)), t_spec, t_spec],
        out_specs=[t_spec, t_spec], out_shape=[_hbm((s, f), BF16)] * 2, compiler_params=_params(est),
    )(*_pin(dx, w_down, gate, up))


def _ffn_up_dx(dgate, dup, wg, wu, *, name):
    s, f = dgate.shape
    _, d, c = wg.shape
    tm, tn = _pick(s, 1024, SUBLANE), _pick(d, 1024)
    nk = 2 * N_CHIPS

    def body(dg_ref, du_ref, wg_ref, wu_ref, o_ref, acc):
        kk = pl.program_id(2)

        @pl.when(kk == 0)
        def _():
            acc[...] = jnp.zeros_like(acc)

        @pl.when(kk < N_CHIPS)
        def _():
            acc[...] += lax.dot_general(dg_ref[...], wg_ref[...], _DIMS["nt"], preferred_element_type=F32)

        @pl.when(kk >= N_CHIPS)
        def _():
            acc[...] += lax.dot_general(du_ref[...], wu_ref[...], _DIMS["nt"], preferred_element_type=F32)

        @pl.when(kk == nk - 1)
        def _():
            o_ref[...] = acc[...]

    last = N_CHIPS - 1
    est = 2 * (2 * tm * c * 2 + 2 * tn * c * 2 + tm * tn * 4) + 2 * tm * tn * 4
    return pl.pallas_call(
        body, name=name, grid=(s // tm, d // tn, nk),
        in_specs=[pl.BlockSpec((tm, c), lambda i, j, kk: (i, jnp.minimum(kk, last))),
                  pl.BlockSpec((tm, c), lambda i, j, kk: (i, jnp.maximum(kk - N_CHIPS, 0))),
                  pl.BlockSpec((None, tn, c), lambda i, j, kk: (jnp.minimum(kk, last), j, 0)),
                  pl.BlockSpec((None, tn, c), lambda i, j, kk: (jnp.maximum(kk - N_CHIPS, 0), j, 0))],
        out_specs=pl.BlockSpec((tm, tn), lambda i, j, kk: (i, j)),
        out_shape=_hbm((s, d), F32), scratch_shapes=[pltpu.VMEM((tm, tn), F32)],
        compiler_params=_params(est),
    )(*_pin(dgate, dup, wg, wu))


def _norm_fwd(x, gain, *, wb, cb, nb, shared_gain, out_dtype, name):
    s = x.shape[0]
    ts = _rows_for(wb * 4, s)

    def body(x_ref, g_ref, o_ref):
        xv = x_ref[...].astype(F32)
        r = lax.rsqrt(jnp.mean(xv * xv, axis=1, keepdims=True) + EPS)
        o_ref[...] = ((xv * r) * g_ref[...]).astype(out_dtype)

    return pl.pallas_call(
        body, name=name, grid=(nb, s // ts),
        in_specs=[pl.BlockSpec((ts, wb), lambda n, i: (i, cb + n)),
                  pl.BlockSpec((1, wb), (lambda n, i: (0, 0)) if shared_gain else (lambda n, i: (0, n)))],
        out_specs=pl.BlockSpec((ts, wb), lambda n, i: (i, n)),
        out_shape=_hbm((s, nb * wb), out_dtype), compiler_params=_params(6 * ts * wb * 4),
    )(*_pin(x), gain)


def _norm_bwd(x, gain, dy, *, wb, cb, nb, shared_gain, out_dtype, name, dy_cb=0, add=None, bf16_copy=False):
    s = x.shape[0]
    ts = _rows_for(wb * 4, s, target=1024 * 1024)
    gw = wb if shared_gain else nb * wb

    def body(*refs):
        refs = list(refs)
        dxb_ref = refs.pop() if bf16_copy else None
        if add is None:
            x_ref, g_ref, dy_ref, dx_ref, dg_ref = refs
        else:
            x_ref, g_ref, dy_ref, add_ref, dx_ref, dg_ref = refs
        n, i = pl.program_id(0), pl.program_id(1)
        xv = x_ref[...].astype(F32)
        dyv = dy_ref[...].astype(F32)
        r = lax.rsqrt(jnp.mean(xv * xv, axis=1, keepdims=True) + EPS)
        xh = xv * r
        dyg = dyv * g_ref[...]
        dx = r * (dyg - xh * jnp.mean(dyg * xh, axis=1, keepdims=True))
        if add is not None:
            dx = dx + add_ref[...]
        dx_ref[...] = dx.astype(out_dtype)
        if bf16_copy:
            dxb_ref[...] = dx.astype(BF16)
        first = jnp.logical_and(n == 0, i == 0) if shared_gain else (i == 0)

        @pl.when(first)
        def _():
            dg_ref[...] = jnp.zeros_like(dg_ref)

        dg_ref[...] += jnp.sum(dyv * xh, axis=0, keepdims=True)

    ins = [x, gain, dy]
    specs = [pl.BlockSpec((ts, wb), lambda n, i: (i, cb + n)),
             pl.BlockSpec((1, wb), (lambda n, i: (0, 0)) if shared_gain else (lambda n, i: (0, n))),
             pl.BlockSpec((ts, wb), lambda n, i: (i, dy_cb + n))]
    if add is not None:
        ins.append(add)
        specs.append(pl.BlockSpec((ts, wb), lambda n, i: (i, n)))
    out_specs = [pl.BlockSpec((ts, wb), lambda n, i: (i, n)),
                 pl.BlockSpec((1, wb), (lambda n, i: (0, 0)) if shared_gain else (lambda n, i: (0, n)))]
    out_shape = [_hbm((s, nb * wb), out_dtype), jax.ShapeDtypeStruct((1, gw), F32)]
    if bf16_copy:
        out_specs.append(out_specs[0])
        out_shape.append(_hbm((s, nb * wb), BF16))
    return pl.pallas_call(
        body, name=name, grid=(nb, s // ts), in_specs=specs, out_specs=out_specs, out_shape=out_shape,
        compiler_params=_params(14 * ts * wb * 4),
    )(*_pin(*ins))


def _swap_halves(x, half):
    if 2 * half == LANE:
        return pltpu.roll(x, half, axis=1)
    lane = lax.broadcasted_iota(jnp.int32, x.shape, 1)
    first = jnp.bitwise_and(lane, 2 * half - 1) < half
    return jnp.where(first, pltpu.roll(x, LANE - half, axis=1), pltpu.roll(x, half, axis=1))


def _rope(x, cos_t, sin_t, *, tw, cb, nb, half, sign, out_dtype, name):
    s = x.shape[0]
    ts = _rows_for(tw * 4, s)

    def body(x_ref, c_ref, s_ref, o_ref):
        for q in range(tw // LANE):
            sl = slice(q * LANE, (q + 1) * LANE)
            xv = x_ref[:, sl].astype(F32)
            sv = s_ref[:, sl]
            if sign < 0:
                sv = -sv
            o_ref[:, sl] = (xv * c_ref[:, sl] + _swap_halves(xv, half) * sv).astype(out_dtype)

    return pl.pallas_call(
        body, name=name, grid=(nb, s // ts),
        in_specs=[pl.BlockSpec((ts, tw), lambda n, i: (i, cb + n)),
                  pl.BlockSpec((ts, tw), lambda n, i: (i, 0)),
                  pl.BlockSpec((ts, tw), lambda n, i: (i, 0))],
        out_specs=pl.BlockSpec((ts, tw), lambda n, i: (i, n)),
        out_shape=_hbm((s, nb * tw), out_dtype), compiler_params=_params(10 * ts * tw * 4),
    )(*_pin(x), cos_t, sin_t)


def _cast_cols(x, *, cb, nb, name):
    s = x.shape[0]
    ts = _rows_for(LANE * 4, s)

    def body(x_ref, o_ref):
        o_ref[...] = x_ref[...].astype(BF16)

    return pl.pallas_call(
        body, name=name, grid=(nb, s // ts), in_specs=[pl.BlockSpec((ts, LANE), lambda n, i: (i, cb + n))],
        out_specs=pl.BlockSpec((ts, LANE), lambda n, i: (i, n)),
        out_shape=_hbm((s, nb * LANE), BF16), compiler_params=_params(4 * ts * LANE * 4),
    )(*_pin(x))


def _group_sum(x, *, n_out, g, src, out_dtype, name):
    s = x.shape[0]
    ts = _rows_for(LANE * 4, s)

    def body(*refs):
        acc = refs[0][...].astype(F32)
        for r in refs[1:-1]:
            acc = acc + r[...].astype(F32)
        refs[-1][...] = acc.astype(out_dtype)

    return pl.pallas_call(
        body, name=name, grid=(n_out, s // ts),
        in_specs=[pl.BlockSpec((ts, LANE), functools.partial(lambda n, i, j: (i, src(n, j)), j=j)) for j in range(g)],
        out_specs=pl.BlockSpec((ts, LANE), lambda n, i: (i, n)),
        out_shape=_hbm((s, n_out * LANE), out_dtype), compiler_params=_params(4 * g * ts * LANE * 4),
    )(*_pin(*([x] * g)))


LOG2E = 1.4426950408889634
ATTN_ROW_CHUNK = 256


def _attn_window(i, tq, s, band):
    w, r = band
    start = jnp.clip(i * tq - r, 0, s - w)
    return pl.multiple_of(start, tq), pl.multiple_of((w - tq) - (i * tq - start), LANE)


def _flash_fwd(q, k, v, table, *, hkv, g, dqk, q_cb, k_cb, v_cb, v_step, scale, tq, band, name):
    s = q.shape[0]
    n = s // tq
    hq = hkv * g
    rc = min(tq, ATTN_ROW_CHUNK)
    w = s if band is None else band[0]

    def body(*refs):
        if band is None:
            q_ref, k_ref, v_ref, o_ref, lse_ref = refs
            kw, vw = k_ref[...], v_ref[...]
        else:
            q_ref, k_ref, v_ref, t_ref, o_ref, lse_ref = refs
            start, u = _attn_window(pl.program_id(1), tq, s, band)
            kw, vw = k_ref[pl.ds(start, w), :], v_ref[pl.ds(start, w), :]
        for c in range(tq // rc):
            rows = slice(c * rc, (c + 1) * rc)
            sc = lax.dot_general(q_ref[rows, :], kw, _DIMS["nt"], preferred_element_type=F32) * (scale * LOG2E)
            if band is not None:
                sc = sc + t_ref[rows, pl.ds(u, w)]
            m = jnp.max(sc, axis=1, keepdims=True)
            p = jnp.exp2(sc - m)
            l = jnp.sum(p, axis=1, keepdims=True)
            o_ref[rows, :] = jnp.dot(p.astype(BF16), vw, preferred_element_type=F32) / l
            lse_ref[0, rows, :] = jnp.broadcast_to(m + jnp.log2(l), (rc, LANE))

    ins = [q, k, v]
    specs = [pl.BlockSpec((tq, dqk), lambda h, i: (i, q_cb + h)),
             pl.BlockSpec((s, dqk), lambda h, i: (0, k_cb + h // g)),
             pl.BlockSpec((s, LANE), lambda h, i: (0, v_cb + v_step * (h // g)))]
    if band is not None:
        ins.append(table)
        specs.append(pl.BlockSpec(table.shape, lambda h, i: (0, 0)))
    est = 4 * s * (dqk + LANE) + 6 * rc * w * 4 + 8 * tq * LANE * 4 + (0 if band is None else 2 * table.size * 4)
    return pl.pallas_call(
        body, name=name, grid=(hq, n), in_specs=specs,
        out_specs=[pl.BlockSpec((tq, LANE), lambda h, i: (i, h)), pl.BlockSpec((1, tq, LANE), lambda h, i: (h, i, 0))],
        out_shape=[_hbm((s, hq * LANE), F32), _hbm((hq, s, LANE), F32)],
        compiler_params=_params(est),
    )(*_pin(*ins))


def _flash_bwd(q, k, v, o, do, lse, table, *, hkv, g, dqk, q_cb, k_cb, v_cb, v_step, scale, tq, band, name):
    s = q.shape[0]
    n = s // tq
    hq = hkv * g
    rc = min(tq, ATTN_ROW_CHUNK)
    w = s if band is None else band[0]

    def body(*refs):
        if band is None:
            q_ref, k_ref, v_ref, o_ref, do_ref, lse_ref, dq_ref, dk_ref, dv_ref = refs
            keys = slice(None)
        else:
            q_ref, k_ref, v_ref, o_ref, do_ref, lse_ref, t_ref, dq_ref, dk_ref, dv_ref = refs
            start, u = _attn_window(pl.program_id(1), tq, s, band)
            keys = pl.ds(start, w)
        h, i = pl.program_id(0), pl.program_id(1)

        @pl.when(jnp.logical_and(h % g == 0, i == 0))
        def _():
            dk_ref[...] = jnp.zeros_like(dk_ref)
            dv_ref[...] = jnp.zeros_like(dv_ref)

        kw, vw = k_ref[keys, :], v_ref[keys, :]
        for c in range(tq // rc):
            rows = slice(c * rc, (c + 1) * rc)
            qv = q_ref[rows, :]
            dof = do_ref[rows, :]
            dov = dof.astype(BF16)
            sc = lax.dot_general(qv, kw, _DIMS["nt"], preferred_element_type=F32) * (scale * LOG2E)
            if band is not None:
                sc = sc + t_ref[rows, pl.ds(u, w)]
            p = jnp.exp2(sc - lse_ref[0, rows, 0:1])
            dp = lax.dot_general(dov, vw, _DIMS["nt"], preferred_element_type=F32)
            delta = jnp.sum(dof * o_ref[rows, :], axis=1, keepdims=True)
            ds = (p * (dp - delta) * scale).astype(BF16)
            dv_ref[keys, :] += lax.dot_general(p.astype(BF16), dov, _DIMS["tn"], preferred_element_type=F32)
            dk_ref[keys, :] += lax.dot_general(ds, qv, _DIMS["tn"], preferred_element_type=F32)
            dq_ref[rows, :] = jnp.dot(ds, kw, preferred_element_type=F32)

    ins = [q, k, v, o, do, lse]
    specs = [pl.BlockSpec((tq, dqk), lambda h, i: (i, q_cb + h)),
             pl.BlockSpec((s, dqk), lambda h, i: (0, k_cb + h // g)),
             pl.BlockSpec((s, LANE), lambda h, i: (0, v_cb + v_step * (h // g))),
             pl.BlockSpec((tq, LANE), lambda h, i: (i, h)),
             pl.BlockSpec((tq, LANE), lambda h, i: (i, h)),
             pl.BlockSpec((1, tq, LANE), lambda h, i: (h, i, 0))]
    if band is not None:
        ins.append(table)
        specs.append(pl.BlockSpec(table.shape, lambda h, i: (0, 0)))
    est = (4 + 8) * s * (dqk + LANE) + 10 * rc * w * 4 + 12 * tq * LANE * 4 + (0 if band is None else 2 * table.size * 4)
    return pl.pallas_call(
        body, name=name, grid=(hq, n), in_specs=specs,
        out_specs=[pl.BlockSpec((tq, dqk), lambda h, i: (i, h)),
                   pl.BlockSpec((s, dqk), lambda h, i: (0, h // g)),
                   pl.BlockSpec((s, LANE), lambda h, i: (0, h // g))],
        out_shape=[_hbm((s, hq * dqk), F32), _hbm((s, hkv * dqk), F32),
                   _hbm((s, hkv * LANE), F32)],
        compiler_params=_params(est),
    )(*_pin(*ins))


def _final_loss(x, gain, target, *, name):
    s, d = x.shape
    ts = _rows_for(d * 4, s, target=1024 * 1024)

    def body(x_ref, g_ref, t_ref, dx_ref, dg_ref, loss_ref, dxb_ref):
        i = pl.program_id(0)
        xv = x_ref[...]
        gv = g_ref[...]
        r = lax.rsqrt(jnp.mean(xv * xv, axis=1, keepdims=True) + EPS)
        xh = xv * r
        err = xh * gv - t_ref[...]
        dy = err / d
        dyg = dy * gv
        dx = r * (dyg - xh * jnp.mean(dyg * xh, axis=1, keepdims=True))
        dx_ref[...] = dx
        dxb_ref[...] = dx.astype(BF16)

        @pl.when(i == 0)
        def _():
            dg_ref[...] = jnp.zeros_like(dg_ref)
            loss_ref[...] = jnp.zeros_like(loss_ref)

        dg_ref[...] += jnp.sum(dy * xh, axis=0, keepdims=True)
        part = jnp.sum(jnp.mean(err * err, axis=1, keepdims=True), axis=0, keepdims=True)
        loss_ref[...] += jnp.broadcast_to(0.5 * part, (1, LANE))

    row = pl.BlockSpec((ts, d), lambda i: (i, 0))
    return pl.pallas_call(
        body, name=name, grid=(s // ts,),
        in_specs=[row, pl.BlockSpec((1, d), lambda i: (0, 0)), row],
        out_specs=[row, pl.BlockSpec((1, d), lambda i: (0, 0)), pl.BlockSpec((1, LANE), lambda i: (0, 0)), row],
        out_shape=[_hbm((s, d), F32), jax.ShapeDtypeStruct((1, d), F32),
                   jax.ShapeDtypeStruct((1, LANE), F32), _hbm((s, d), BF16)],
        compiler_params=_params(14 * ts * d * 4),
    )(*_pin(x), gain, *_pin(target))


def _cast_to_slot(x3d, me, *, layer, name):
    _, rows, c = x3d.shape
    tr = _rows_for(c * 4, rows)

    def body(me_ref, x_ref, o_ref):
        o_ref[...] = x_ref[...].astype(BF16)

    return pl.pallas_call(
        body, name=name,
        grid_spec=pltpu.PrefetchScalarGridSpec(
            num_scalar_prefetch=1, grid=(rows // tr,),
            in_specs=[pl.BlockSpec((None, tr, c), lambda i, me_ref: (layer, i, 0))],
            out_specs=pl.BlockSpec((None, tr, c), lambda i, me_ref: (me_ref[0], i, 0))),
        out_shape=_hbm((N_CHIPS, rows, c), BF16), compiler_params=_params(6 * tr * c * 4),
    )(me, *_pin(x3d))


def _sum_parts(srcs, lands, me, *, name):
    depth = len(srcs)
    _, r, c = srcs[0].shape
    tr = _rows_for(c * 4, r, target=1024 * 1024)
    nt = r // tr

    def body(me_ref, *refs):
        o_ref = refs[-1]
        l = pl.program_id(0)
        for k in range(depth):
            @pl.when(l == k)
            def _(k=k):
                acc = refs[k][...].astype(F32)
                for p in range(3):
                    acc = acc + refs[depth + k][p].astype(F32)
                o_ref[...] = acc

    def rows_of(k):
        return lambda l, i, me_ref: jnp.where(l == k, i, jnp.where(l < k, 0, nt - 1))

    in_specs = [pl.BlockSpec((None, tr, c), functools.partial(lambda l, i, me_ref, f: (me_ref[0], f(l, i, me_ref), 0), f=rows_of(k)))
                for k in range(depth)]
    in_specs += [pl.BlockSpec((3, tr, c), functools.partial(lambda l, i, me_ref, f: (0, f(l, i, me_ref), 0), f=rows_of(k)))
                 for k in range(depth)]
    return pl.pallas_call(
        body, name=name,
        grid_spec=pltpu.PrefetchScalarGridSpec(
            num_scalar_prefetch=1, grid=(depth, nt), in_specs=in_specs,
            out_specs=pl.BlockSpec((tr, c), lambda l, i, me_ref: (l * nt + i, 0))),
        out_shape=_hbm((depth * r, c), F32), compiler_params=_params(depth * 10 * tr * c * 4),
    )(me, *_pin(*srcs, *lands))


def _adamw_math(w, g, m, v):
    m2 = ADAM_B1 * m + (1.0 - ADAM_B1) * g
    v2 = ADAM_B2 * v + (1.0 - ADAM_B2) * (g * g)
    m_hat = m2 / (1.0 - ADAM_B1 ** ADAM_STEP)
    v_hat = v2 / (1.0 - ADAM_B2 ** ADAM_STEP)
    delta = -ADAM_LR * (m_hat / (jnp.sqrt(v_hat) + ADAM_EPS) + ADAM_WD * w)
    return delta, m2, v2


def _adamw(g_a, g_b, w, m, v, *, name):
    depth, r, c = w.shape
    tr = _rows_for(c * 4, r, target=512 * 1024)
    nt = r // tr

    def body(a_ref, b_ref, w_ref, m_ref, v_ref, g_out, d_out, m_out, v_out):
        gv = a_ref[...] + b_ref[...]
        delta, m2, v2 = _adamw_math(w_ref[...], gv, m_ref[...], v_ref[...])
        g_out[...] = gv
        d_out[...] = delta
        m_out[...] = m2
        v_out[...] = v2

    flat = pl.BlockSpec((tr, c), lambda l, i: (l * nt + i, 0))
    spec = pl.BlockSpec((None, tr, c), lambda l, i: (l, i, 0))
    return pl.pallas_call(
        body, name=name, grid=(depth, nt), in_specs=[flat, flat, spec, spec, spec], out_specs=[spec] * 4,
        out_shape=[_hbm((depth, r, c), F32)] * 4, compiler_params=_params(22 * tr * c * 4),
    )(*_pin(g_a, g_b, w, m, v))


def _small_adamw(g_all, w, m, v, *, name):
    r, c = w.shape

    def body(ga_ref, w_ref, m_ref, v_ref, g_out, d_out, m_out, v_out):
        gv = ga_ref[0]
        for j in range(1, N_DEV):
            gv = gv + ga_ref[j]
        delta, m2, v2 = _adamw_math(w_ref[...], gv, m_ref[...], v_ref[...])
        g_out[...] = gv
        d_out[...] = delta
        m_out[...] = m2
        v_out[...] = v2

    return pl.pallas_call(body, name=name, out_shape=[jax.ShapeDtypeStruct((r, c), F32)] * 4)(g_all, w, m, v)


_ANY = pl.BlockSpec(memory_space=pl.ANY)


_HBM = pl.BlockSpec(memory_space=pltpu.HBM)
_SEM = pl.BlockSpec(memory_space=pltpu.SEMAPHORE)
_EFFECT = pltpu.SideEffectType.DATAFLOW_SIDE_EFFECTING


def _peer_chips():
    x, y = lax.axis_index("x"), lax.axis_index("y")
    return 2 * x + y, [(1 - x, y), (x, 1 - y), (1 - x, 1 - y)]


def _exchange_copy(srcs, lands, send_sems, recv_sems, k, p, kind):
    c = lax.axis_index("c")
    if kind == "swap":
        return pltpu.make_async_remote_copy(
            src_ref=srcs[k], dst_ref=lands[k], send_sem=send_sems.at[k], recv_sem=recv_sems.at[k],
            device_id=(lax.axis_index("x"), lax.axis_index("y"), 1 - c), device_id_type=MESH)
    me, peers = _peer_chips()
    px, py = peers[p]
    return pltpu.make_async_remote_copy(
        src_ref=srcs[k].at[2 * px + py] if kind == "scatter" else srcs[k].at[me],
        dst_ref=lands[k].at[p] if kind == "scatter" else lands[k].at[me],
        send_sem=send_sems.at[3 * k + p], recv_sem=recv_sems.at[3 * k + p],
        device_id=(px, py, c), device_id_type=MESH)


def _exchange_start(srcs, lands, after, *, kind, name):
    n = len(srcs)
    npeer = 1 if kind == "swap" else 3
    bufs = list(srcs) + (list(lands) if lands is not None else [])
    nb = len(bufs)

    def body(*refs):
        buf_refs, send_sems, recv_sems = refs[:nb], refs[nb + 1], refs[nb + 2]
        token = refs[-1]
        s_refs = buf_refs[:n]
        l_refs = buf_refs[n:] if lands is not None else s_refs
        for k in range(n):
            for p in range(npeer):
                _exchange_copy(s_refs, l_refs, send_sems, recv_sems, k, p, kind).start()
        token[...] = jnp.zeros_like(token)

    out = pl.pallas_call(
        body, name=name,
        out_shape=(pltpu.SemaphoreType.DMA((npeer * n,)), pltpu.SemaphoreType.DMA((npeer * n,)),
                   *[pltpu.HBM(b.shape, b.dtype) for b in bufs], jax.ShapeDtypeStruct((SUBLANE, LANE), F32)),
        in_specs=[_HBM] * nb + [_ANY],
        out_specs=(_SEM, _SEM, *[_HBM] * nb, pl.BlockSpec(memory_space=pltpu.VMEM)),
        input_output_aliases={i: 2 + i for i in range(nb)},
        compiler_params=pltpu.CompilerParams(has_side_effects=_EFFECT),
    )(*[pltpu.with_memory_space_constraint(b, pltpu.HBM) for b in bufs], after)
    send_sems, recv_sems = out[0], out[1]
    thru = out[2:2 + nb]
    return send_sems, recv_sems, list(thru[:n]), (list(thru[n:]) if lands is not None else None), out[-1]


def _exchange_wait(send_sems, recv_sems, srcs, lands, after, *, kind, name):
    n = len(srcs)
    npeer = 1 if kind == "swap" else 3
    bufs = list(srcs) + (list(lands) if lands is not None else [])
    nb = len(bufs)

    def body(*refs):
        buf_refs, send_sems_ref, recv_sems_ref = refs[:nb], refs[nb], refs[nb + 1]
        s_refs = buf_refs[:n]
        l_refs = buf_refs[n:] if lands is not None else s_refs
        for k in range(n):
            for p in range(npeer):
                cp = _exchange_copy(s_refs, l_refs, send_sems_ref, recv_sems_ref, k, p, kind)
                cp.wait_send()
                cp.wait_recv()

    out = pl.pallas_call(
        body, name=name, out_shape=tuple(pltpu.HBM(b.shape, b.dtype) for b in bufs),
        in_specs=[_HBM] * nb + [_SEM, _SEM, _ANY], out_specs=tuple([_HBM] * nb),
        input_output_aliases={i: i for i in range(nb)},
        compiler_params=pltpu.CompilerParams(has_side_effects=_EFFECT),
    )(*bufs, send_sems, recv_sems, after)
    return list(out)


def _sibling_exchange(srcs, *, name):
    n = len(srcs)

    def body(*refs):
        src, out = refs[:n], refs[n:2 * n]
        send_sems, recv_sems = refs[2 * n:]
        sibling = (lax.axis_index("x"), lax.axis_index("y"), 1 - lax.axis_index("c"))
        copies = [pltpu.make_async_remote_copy(src_ref=src[k], dst_ref=out[k], send_sem=send_sems.at[k],
                                               recv_sem=recv_sems.at[k], device_id=sibling, device_id_type=MESH)
                  for k in range(n)]
        for cp in copies:
            cp.start()
        for cp in copies:
            cp.wait_recv()
        for cp in copies:
            cp.wait_send()

    return pl.pallas_call(
        body, name=name, in_specs=[_ANY] * n, out_specs=[_ANY] * n,
        out_shape=[jax.ShapeDtypeStruct(a.shape, a.dtype) for a in srcs],
        scratch_shapes=[pltpu.SemaphoreType.DMA((n,)), pltpu.SemaphoreType.DMA((n,))],
    )(*srcs)


def _all_gather_small(block, *, name):
    m_per, ncol = block.shape

    def body(x_ref, out_ref, send_sems, recv_sems, local_sem):
        x, y, c = lax.axis_index("x"), lax.axis_index("y"), lax.axis_index("c")
        me, sibling = (x, y, c), (x, y, 1 - c)
        chips = [(1 - x, y), (x, 1 - y), (1 - x, 1 - y)]

        def rows(px, py, pc):
            return out_ref.at[pl.ds((4 * px + 2 * py + pc) * m_per, m_per), :]

        def copy(k, blk, to, src=None):
            return pltpu.make_async_remote_copy(
                src_ref=rows(*blk) if src is None else src, dst_ref=rows(*blk),
                send_sem=send_sems.at[k], recv_sem=recv_sems.at[k], device_id=to, device_id_type=MESH)

        mine = pltpu.make_async_copy(x_ref, rows(*me), local_sem)
        mine.start()
        first = [copy(0, me, sibling, src=x_ref)]
        first += [copy(1 + j, me, (*chip, c), src=x_ref) for j, chip in enumerate(chips)]
        for cp in first:
            cp.start()
        passed = [copy(4 + j, (*chip, c), sibling) for j, chip in enumerate(chips)]
        for j, chip in enumerate(chips):
            copy(1 + j, (*chip, c), me).wait_recv()
            passed[j].start()
        copy(0, sibling, me).wait_recv()
        for j, chip in enumerate(chips):
            copy(4 + j, (*chip, 1 - c), me).wait_recv()
        for cp in first + passed:
            cp.wait_send()
        mine.wait()

    return pl.pallas_call(
        body, name=name, out_shape=jax.ShapeDtypeStruct((N_DEV * m_per, ncol), block.dtype),
        in_specs=[pl.BlockSpec(memory_space=pltpu.VMEM)], out_specs=pl.BlockSpec(memory_space=pltpu.VMEM),
        scratch_shapes=[pltpu.SemaphoreType.DMA((7,)), pltpu.SemaphoreType.DMA((7,)), pltpu.SemaphoreType.DMA],
    )(block)


def _rope_angles(pos, dim):
    inv = ROPE_THETA ** (-jnp.arange(0, dim, 2, dtype=F32) / dim)
    return pos.astype(F32)[:, None] * inv[None, :]


def _rope_tables(s):
    pos = jnp.arange(s, dtype=jnp.int32)
    rows = s // GRID_W
    row = jnp.repeat(jnp.arange(rows, dtype=jnp.int32), GRID_W)
    col = jnp.tile(jnp.arange(GRID_W, dtype=jnp.int32), rows)
    a1 = _rope_angles(pos, HEAD_DIM)
    aa = _rope_angles(pos, A_ROPE)
    ar = _rope_angles(row, HEAD_DIM // 2)
    ac = _rope_angles(col, HEAD_DIM // 2)
    one = jnp.ones((s, LANE), F32)
    zero = jnp.zeros((s, LANE), F32)
    pad = LANE - A_ROPE
    cos_a = jnp.concatenate([one, jnp.cos(aa), jnp.cos(aa), jnp.ones((s, pad), F32)], axis=1)
    sin_a = jnp.concatenate([zero, -jnp.sin(aa), jnp.sin(aa), jnp.zeros((s, pad), F32)], axis=1)
    cos_b = jnp.concatenate([jnp.cos(a1), jnp.cos(a1)], axis=1)
    sin_b = jnp.concatenate([-jnp.sin(a1), jnp.sin(a1)], axis=1)
    cos_c = jnp.concatenate([jnp.cos(ar), jnp.cos(ar), jnp.cos(ac), jnp.cos(ac)], axis=1)
    sin_c = jnp.concatenate([-jnp.sin(ar), jnp.sin(ar), -jnp.sin(ac), jnp.sin(ac)], axis=1)
    return (cos_a, sin_a), (cos_b, sin_b), (cos_c, sin_c)


def _band_table(tq, s):
    reach = max((win // (2 * d)) * d for win, d in B_PATTERNS)
    r = -(-reach // tq) * tq
    w = min(s, tq + 2 * r)
    j = jnp.arange(tq, dtype=jnp.int32)[:, None]
    x = jnp.arange(2 * w - tq, dtype=jnp.int32)[None, :]
    rel = x - (w - tq) - j
    mult = jnp.zeros(rel.shape, F32)
    for win, d in B_PATTERNS:
        mult = mult + jnp.logical_and(rel % d == 0, jnp.abs(rel) <= (win // (2 * d)) * d).astype(F32)
    return jnp.where(mult > 0, jnp.log2(jnp.maximum(mult, 1.0)), NEG), (w, r)


_BIG = ("w_in", "a_w_uq", "a_w_ukv", "w_out", "w_gate", "w_up", "w_down")
_SMALL = ("attn_norm", "a_q_norm", "a_kv_norm", "c_q_norm", "c_k_norm", "out_norm", "ffn_norm", "final_norm")
_WEIGHTS = ("attn_norm", "w_in", "a_q_norm", "a_w_uq", "a_kv_norm", "a_w_ukv", "c_q_norm", "c_k_norm", "out_norm",
            "w_out", "ffn_norm", "w_gate", "w_up", "w_down", "final_norm")


_ATTN = ("w_in", "a_w_uq", "a_w_ukv")
_FFN = ("w_out", "w_gate", "w_up", "w_down")


def _from_cols(a):
    return jnp.transpose(a, (1, 0, 2)).reshape(a.shape[1], N_CHIPS * a.shape[2])


def _from_rows(a):
    return a.reshape(N_CHIPS * a.shape[1], a.shape[2])


def _to_cols(a):
    return jnp.transpose(a.reshape(a.shape[0], N_CHIPS, a.shape[1] // N_CHIPS), (1, 0, 2))


def _to_rows(a):
    return a.reshape(N_CHIPS, a.shape[0] // N_CHIPS, a.shape[1])


def _assemble_attn(gw):
    w_in, uq, ukv = _from_cols(gw[0]), _from_cols(gw[1]), _from_cols(gw[2])
    d = w_in.shape[0]
    w_all = jnp.concatenate([w_in[:, :IN_A], jnp.zeros((d, A_PAD - IN_A), BF16), w_in[:, IN_A:]], axis=1)
    uq = uq.reshape(A_Q_RANK, A_HEADS, A_NOPE + A_ROPE)
    uq = jnp.pad(uq, ((0, 0), (0, 0), (0, A_QK - A_NOPE - A_ROPE))).reshape(A_Q_RANK, A_HEADS * A_QK)
    return dict(w_all=w_all, uq=uq, ukv=ukv)


def _assemble_ffn(gw):
    return dict(w_out=_from_rows(gw[0]), w_gate=gw[1], w_up=gw[2], w_down=_from_rows(gw[3]))


def _split_attn_grads(gl):
    w_all = gl["w_all"]
    w_in = jnp.concatenate([w_all[:, :IN_A], w_all[:, A_PAD:]], axis=1)
    uq = gl["uq"].reshape(A_Q_RANK, A_HEADS, A_QK)[:, :, :A_NOPE + A_ROPE].reshape(A_Q_RANK, A_HEADS * (A_NOPE + A_ROPE))
    return [_to_cols(w_in), _to_cols(uq), _to_cols(gl["ukv"])]


def _split_ffn_grads(gl):
    return [_to_rows(gl["w_out"]), gl["w_gate"], gl["w_up"], _to_rows(gl["w_down"])]


def _tie(a, token):
    return a + token[0:1, 0:1]


def _layer_fwd(x, wl, ffn_weights, sm, tabs, bias, t):
    s = x.shape[0]
    (cos_a, sin_a), (cos_b, sin_b), (cos_c, sin_c) = tabs
    h = _norm_fwd(x, sm["attn_norm"], wb=x.shape[1], cb=0, nb=1, shared_gain=True, out_dtype=BF16, name="attn_norm_fwd")
    p = _matmul(h, wl["w_all"], mode="nn", out_dtype=F32, name="in_proj", tm=1024, tn=640)
    cq_n = _norm_fwd(p, sm["a_q_norm"], wb=A_Q_RANK, cb=0, nb=1, shared_gain=True, out_dtype=BF16, name="a_q_norm_fwd")
    ckv_n = _norm_fwd(p, sm["a_kv_norm"], wb=A_KV_RANK, cb=1, nb=1, shared_gain=True, out_dtype=BF16, name="a_kv_norm_fwd")
    qa_raw = _matmul(cq_n, wl["uq"], mode="nn", out_dtype=F32, name="a_uq", tm=1024, tn=1024)
    kv = _matmul(ckv_n, wl["ukv"], mode="nn", out_dtype=BF16, name="a_ukv", tm=1024, tn=1024)
    k_nope = kv.reshape(s, A_HEADS, 2, LANE)[:, :, 0].astype(F32)
    k_rope = jnp.broadcast_to(p[:, PB_KR * LANE:(PB_KR + 1) * LANE][:, None, :], (s, A_HEADS, LANE))
    ka_raw = jnp.stack([k_nope, k_rope], axis=2).reshape(s, A_HEADS * A_QK)
    qa = _rope(qa_raw, cos_a, sin_a, tw=A_QK, cb=0, nb=A_HEADS, half=A_ROPE // 2, sign=1, out_dtype=BF16, name="a_rope_q")
    ka = _rope(ka_raw, cos_a, sin_a, tw=A_QK, cb=0, nb=A_HEADS, half=A_ROPE // 2, sign=1, out_dtype=BF16, name="a_rope_k")
    oa, lse_a = _flash_fwd(qa, ka, kv, None, hkv=A_HEADS, g=1, dqk=A_QK, q_cb=0, k_cb=0, v_cb=1, v_step=2,
                           scale=(A_NOPE + A_ROPE) ** -0.5, tq=_pick(s, 2 * t), band=None, name="a_flash_fwd")
    table, band = bias
    qb = _rope(p, cos_b, sin_b, tw=LANE, cb=PB_BQ, nb=B_HEADS, half=HEAD_DIM // 2, sign=1, out_dtype=BF16, name="b_rope_q")
    kb = _rope(p, cos_b, sin_b, tw=LANE, cb=PB_BK, nb=B_HEADS, half=HEAD_DIM // 2, sign=1, out_dtype=BF16, name="b_rope_k")
    vb = _cast_cols(p, cb=PB_BV, nb=B_HEADS, name="b_cast_v")
    ob, lse_b = _flash_fwd(qb, kb, vb, table, hkv=B_HEADS, g=1, dqk=LANE, q_cb=0, k_cb=0, v_cb=0, v_step=1,
                           scale=HEAD_DIM ** -0.5, tq=t, band=band, name="b_flash_fwd")
    qn = _norm_fwd(p, sm["c_q_norm"], wb=LANE, cb=PB_CQH, nb=C_HEADS, shared_gain=True, out_dtype=F32, name="c_q_norm_fwd")
    kn = _norm_fwd(p, sm["c_k_norm"], wb=LANE, cb=PB_CKH, nb=C_KV_HEADS, shared_gain=True, out_dtype=F32, name="c_k_norm_fwd")
    qc = _rope(qn, cos_c, sin_c, tw=LANE, cb=0, nb=C_HEADS, half=HEAD_DIM // 4, sign=1, out_dtype=BF16, name="c_rope_q")
    kc = _rope(kn, cos_c, sin_c, tw=LANE, cb=0, nb=C_KV_HEADS, half=HEAD_DIM // 4, sign=1, out_dtype=BF16, name="c_rope_k")
    vc = _cast_cols(p, cb=PB_CVH, nb=C_KV_HEADS, name="c_cast_v")
    oc, lse_c = _flash_fwd(qc, kc, vc, None, hkv=C_KV_HEADS, g=C_GROUP, dqk=LANE, q_cb=0, k_cb=0, v_cb=0, v_step=1,
                           scale=HEAD_DIM ** -0.5, tq=_pick(s, 2 * t), band=None, name="c_flash_fwd")
    g_out = sm["out_norm"]
    ga, gb, gc = g_out[:, :A_WIDTH], g_out[:, A_WIDTH:A_WIDTH + B_WIDTH], g_out[:, A_WIDTH + B_WIDTH:]
    ya = _norm_fwd(oa, ga, wb=A_WIDTH, cb=0, nb=1, shared_gain=True, out_dtype=BF16, name="out_norm_a_fwd")
    yb = _norm_fwd(ob, gb, wb=B_WIDTH, cb=0, nb=1, shared_gain=True, out_dtype=BF16, name="out_norm_b_fwd")
    yc = _norm_fwd(oc, gc, wb=C_WIDTH, cb=0, nb=1, shared_gain=True, out_dtype=BF16, name="out_norm_c_fwd")
    y = jnp.concatenate([ya, yb, yc], axis=1)
    wl = {**wl, **ffn_weights(y)}
    x1 = _matmul(y, wl["w_out"], mode="nn", out_dtype=F32, name="out_proj", add=x, tm=1024, tn=512)
    h2 = _norm_fwd(x1, sm["ffn_norm"], wb=x.shape[1], cb=0, nb=1, shared_gain=True, out_dtype=BF16, name="ffn_norm_fwd")
    gate, up, act = _ffn_up(h2, wl["w_gate"], wl["w_up"], name="ffn_up")
    x2 =_matmul(act, wl["w_down"], mode="nn", out_dtype=F32, name="ffn_down", add=x1, tm=512, tn=512)
    saved = dict(x=x, h=h, p=p, cq_n=cq_n, ckv_n=ckv_n, kv=kv, qa=qa, ka=ka, oa=oa, lse_a=lse_a, qb=qb, kb=kb, vb=vb, ob=ob,
                 lse_b=lse_b, qc=qc, kc=kc, vc=vc, oc=oc, lse_c=lse_c, y=y, x1=x1, h2=h2, gate=gate, up=up, act=act)
    return x2, saved, wl


def _layer_bwd(dx2, dx2b, sv, wl, sm, tabs, bias, t, send_ffn, send_attn):
    s, d = dx2.shape
    (cos_a, sin_a), (cos_b, sin_b), (cos_c, sin_c) = tabs
    gw, gs = {}, {}
    dgate, dup = _ffn_down_dx(dx2b, wl["w_down"], sv["gate"], sv["up"], name="ffn_down_dx")
    gw["w_down"] = _matmul(sv["act"], dx2b, mode="tn", out_dtype=BF16, name="ffn_down_dw", tm=512, tn=512)
    dh2 = _ffn_up_dx(dgate, dup, wl["w_gate"], wl["w_up"], name="ffn_up_dx")
    gw["w_gate"] = _matmul(sv["h2"], dgate, mode="tn", out_dtype=BF16, name="ffn_gate_dw", tm=512, col_shards=True)
    gw["w_up"] = _matmul(sv["h2"], dup, mode="tn", out_dtype=BF16, name="ffn_up_dw", tm=512, col_shards=True)
    dx1, gs["ffn_norm"], dx1b = _norm_bwd(sv["x1"], sm["ffn_norm"], dh2, wb=d, cb=0, nb=1, shared_gain=True,
                                          out_dtype=F32, name="ffn_norm_bwd", add=dx2, bf16_copy=True)
    dy = _matmul(dx1b, wl["w_out"], mode="nt", out_dtype=F32, name="out_proj_dx", tm=512, tn=512)
    gw["w_out"] = _matmul(sv["y"], dx1b, mode="tn", out_dtype=BF16, name="out_proj_dw", tm=512, tn=512)
    token = send_ffn(gw)
    g_out = _tie(sm["out_norm"], token)
    ga, gb, gc = g_out[:, :A_WIDTH], g_out[:, A_WIDTH:A_WIDTH + B_WIDTH], g_out[:, A_WIDTH + B_WIDTH:]
    dya, dyb, dyc = dy[:, :A_WIDTH], dy[:, A_WIDTH:A_WIDTH + B_WIDTH], dy[:, A_WIDTH + B_WIDTH:]
    doa, dga = _norm_bwd(sv["oa"], ga, dya, wb=A_WIDTH, cb=0, nb=1, shared_gain=True, out_dtype=F32, name="out_norm_a_bwd")
    dob, dgb = _norm_bwd(sv["ob"], gb, dyb, wb=B_WIDTH, cb=0, nb=1, shared_gain=True, out_dtype=F32, name="out_norm_b_bwd")
    doc, dgc = _norm_bwd(sv["oc"], gc, dyc, wb=C_WIDTH, cb=0, nb=1, shared_gain=True, out_dtype=F32, name="out_norm_c_bwd")
    gs["out_norm"] = jnp.concatenate([dga, dgb, dgc], axis=1)
    p = sv["p"]
    dqc, dkc, dvc = _flash_bwd(sv["qc"], sv["kc"], sv["vc"], sv["oc"], doc, sv["lse_c"], None, hkv=C_KV_HEADS,
                               g=C_GROUP, dqk=LANE, q_cb=0, k_cb=0, v_cb=0, v_step=1, scale=HEAD_DIM ** -0.5,
                               tq=_pick(s, 2 * t), band=None, name="c_flash_bwd")
    dqn = _rope(dqc, cos_c, sin_c, tw=LANE, cb=0, nb=C_HEADS, half=HEAD_DIM // 4, sign=-1, out_dtype=F32, name="c_rope_q_bwd")
    dkn = _rope(dkc, cos_c, sin_c, tw=LANE, cb=0, nb=C_KV_HEADS, half=HEAD_DIM // 4, sign=-1, out_dtype=F32, name="c_rope_k_bwd")
    dpcq, gs["c_q_norm"] = _norm_bwd(p, sm["c_q_norm"], dqn, wb=LANE, cb=PB_CQH, nb=C_HEADS, shared_gain=True,
                                     out_dtype=BF16, name="c_q_norm_bwd")
    dpck, gs["c_k_norm"] = _norm_bwd(p, sm["c_k_norm"], dkn, wb=LANE, cb=PB_CKH, nb=C_KV_HEADS, shared_gain=True,
                                     out_dtype=BF16, name="c_k_norm_bwd")
    table, band = bias
    dqb, dkb, dvb = _flash_bwd(sv["qb"], sv["kb"], sv["vb"], sv["ob"], dob, sv["lse_b"], table, hkv=B_HEADS, g=1,
                               dqk=LANE, q_cb=0, k_cb=0, v_cb=0, v_step=1, scale=HEAD_DIM ** -0.5, tq=t, band=band,
                               name="b_flash_bwd")
    dpbq = _rope(dqb, cos_b, sin_b, tw=LANE, cb=0, nb=B_HEADS, half=HEAD_DIM // 2, sign=-1, out_dtype=BF16, name="b_rope_q_bwd")
    dpbk = _rope(dkb, cos_b, sin_b, tw=LANE, cb=0, nb=B_HEADS, half=HEAD_DIM // 2, sign=-1, out_dtype=BF16, name="b_rope_k_bwd")
    dqa, dka, dva = _flash_bwd(sv["qa"], sv["ka"], sv["kv"], sv["oa"], doa, sv["lse_a"], None, hkv=A_HEADS, g=1,
                               dqk=A_QK, q_cb=0, k_cb=0, v_cb=1, v_step=2, scale=(A_NOPE + A_ROPE) ** -0.5,
                               tq=_pick(s, 2 * t), band=None, name="a_flash_bwd")
    dqa_raw = _rope(dqa, cos_a, sin_a, tw=A_QK, cb=0, nb=A_HEADS, half=A_ROPE // 2, sign=-1, out_dtype=BF16, name="a_rope_q_bwd")
    dka_raw = _rope(dka, cos_a, sin_a, tw=A_QK, cb=0, nb=A_HEADS, half=A_ROPE // 2, sign=-1, out_dtype=BF16, name="a_rope_k_bwd")
    dkr = _group_sum(dka_raw, n_out=1, g=A_HEADS, src=lambda n, j: 2 * j + 1, out_dtype=BF16, name="a_k_rope_sum")
    dkv = jnp.stack([dka_raw.reshape(s, A_HEADS, 2, LANE)[:, :, 0], dva.reshape(s, A_HEADS, LANE).astype(BF16)], axis=2)
    dkv = dkv.reshape(s, A_HEADS * 2 * LANE)
    dckv_n = _matmul(dkv, wl["ukv"], mode="nt", out_dtype=F32, name="a_ukv_dx", tm=1024, tn=512)
    gw["ukv"] = _matmul(sv["ckv_n"], dkv, mode="tn", out_dtype=BF16, name="a_ukv_dw", tm=512, tn=1024)
    dcq_n = _matmul(dqa_raw, wl["uq"], mode="nt", out_dtype=F32, name="a_uq_dx", tm=1024, tn=512)
    gw["uq"] = _matmul(sv["cq_n"], dqa_raw, mode="tn", out_dtype=BF16, name="a_uq_dw", tm=512, tn=1024)
    dcq, gs["a_q_norm"] = _norm_bwd(p, sm["a_q_norm"], dcq_n, wb=A_Q_RANK, cb=0, nb=1, shared_gain=True, out_dtype=BF16,
                                    name="a_q_norm_bwd")
    dckv, gs["a_kv_norm"] = _norm_bwd(p, sm["a_kv_norm"], dckv_n, wb=A_KV_RANK, cb=1, nb=1, shared_gain=True,
                                      out_dtype=BF16, name="a_kv_norm_bwd")
    dp = jnp.concatenate([dcq, dckv, dkr, jnp.zeros((s, A_PAD - (PB_KR + 1) * LANE), BF16), dpbq, dpbk,
                          dvb.astype(BF16), dpcq, dpck, dvc.astype(BF16)], axis=1)
    gw["w_all"] = _matmul(sv["h"], dp, mode="tn", out_dtype=BF16, name="in_proj_dw", tm=512, tn=640)
    token = send_attn(gw)
    dh = _matmul(dp, wl["w_all"], mode="nt", out_dtype=F32, name="in_proj_dx", tm=512, tn=512, after=token)
    dx, gs["attn_norm"], dxb = _norm_bwd(sv["x"], sm["attn_norm"], dh, wb=d, cb=0, nb=1, shared_gain=True,
                                         out_dtype=F32, name="attn_norm_bwd", add=dx1, bf16_copy=True)
    return dx, dxb, gs, token


def _pack_small(vals):
    flat = jnp.concatenate([vals[n].reshape(-1).astype(F32) for n in _SMALL])
    tile = SUBLANE * LANE
    padded = -(-flat.shape[0] // tile) * tile
    return jnp.pad(flat, (0, padded - flat.shape[0])).reshape(padded // LANE, LANE)


def _unpack_small(packed, like):
    flat = packed.reshape(-1)
    out, off = {}, 0
    for n in _SMALL:
        size = math.prod(like[n].shape)
        out[n] = flat[off:off + size].reshape(like[n].shape)
        off += size
    return out


def kernel(x, attn_norm, w_in, a_q_norm, a_w_uq, a_kv_norm, a_w_ukv, c_q_norm, c_k_norm, out_norm, w_out, ffn_norm, w_gate, w_up, w_down, final_norm, loss_target, m_attn_norm, m_w_in, m_a_q_norm, m_a_w_uq, m_a_kv_norm, m_a_w_ukv, m_c_q_norm, m_c_k_norm, m_out_norm, m_w_out, m_ffn_norm, m_w_gate, m_w_up, m_w_down, m_final_norm, v_attn_norm, v_w_in, v_a_q_norm, v_a_w_uq, v_a_kv_norm, v_a_w_ukv, v_c_q_norm, v_c_k_norm, v_out_norm, v_w_out, v_ffn_norm, v_w_gate, v_w_up, v_w_down, v_final_norm):
    w = dict(attn_norm=attn_norm, w_in=w_in, a_q_norm=a_q_norm, a_w_uq=a_w_uq, a_kv_norm=a_kv_norm, a_w_ukv=a_w_ukv,
             c_q_norm=c_q_norm, c_k_norm=c_k_norm, out_norm=out_norm, w_out=w_out, ffn_norm=ffn_norm, w_gate=w_gate,
             w_up=w_up, w_down=w_down, final_norm=final_norm)
    m = dict(attn_norm=m_attn_norm, w_in=m_w_in, a_q_norm=m_a_q_norm, a_w_uq=m_a_w_uq, a_kv_norm=m_a_kv_norm,
             a_w_ukv=m_a_w_ukv, c_q_norm=m_c_q_norm, c_k_norm=m_c_k_norm, out_norm=m_out_norm, w_out=m_w_out,
             ffn_norm=m_ffn_norm, w_gate=m_w_gate, w_up=m_w_up, w_down=m_w_down, final_norm=m_final_norm)
    v = dict(attn_norm=v_attn_norm, w_in=v_w_in, a_q_norm=v_a_q_norm, a_w_uq=v_a_w_uq, a_kv_norm=v_a_kv_norm,
             a_w_ukv=v_a_w_ukv, c_q_norm=v_c_q_norm, c_k_norm=v_c_k_norm, out_norm=v_out_norm, w_out=v_w_out,
             ffn_norm=v_ffn_norm, w_gate=v_w_gate, w_up=v_w_up, w_down=v_w_down, final_norm=v_final_norm)
    _, s, d = x.shape
    depth = attn_norm.shape[0]
    t = _pick(s, 512)

    me = (2 * lax.axis_index("x") + lax.axis_index("y")).astype(jnp.int32).reshape(1)

    gathers, after = {}, me
    for l in range(depth):
        for group, names in (("attn", _ATTN), ("ffn", _FFN)):
            bufs = [_cast_to_slot(w[n], me, layer=l, name=f"cast_{n}")
                    for n in names]
            send_sems, recv_sems, bufs, _, after = _exchange_start(bufs, None, after, kind="gather",
                                                                   name=f"gather_start_{group}{l}")
            gathers[group, l] = (send_sems, recv_sems, bufs)
    all_started = after

    def gathered(group, l, after):
        send_sems, recv_sems, bufs = gathers[group, l]
        return _exchange_wait(send_sems, recv_sems, bufs, None, after, kind="gather", name=f"gather_wait_{group}{l}")

    tabs = _rope_tables(s)
    bias = _band_table(t, s)

    xs = x.reshape(s, d)
    saved, wls, sms = [], [], []
    for l in range(depth):
        wl = _assemble_attn(gathered("attn", l, all_started if l == 0 else xs))
        sm = {n: w[n][l][None, :] for n in _SMALL if n != "final_norm"}
        xs, sv, wl = _layer_fwd(xs, wl, lambda after, l=l: _assemble_ffn(gathered("ffn", l, after)), sm, tabs, bias, t)
        saved.append(sv)
        wls.append(wl)
        sms.append(sm)
    dx, g_final, loss_row, dxb = _final_loss(xs, final_norm[None, :], loss_target.reshape(s, d), name="final_loss")
    loss = lax.psum(loss_row[0, 0], ("x", "y", "c"))

    sends = {}

    def send(group, l, srcs, after):
        lands = [lax.empty((3,) + a.shape[1:], BF16) for a in srcs]
        send_sems, recv_sems, srcs, lands, token = _exchange_start(srcs, lands, after, kind="scatter",
                                                                   name=f"scatter_start_{group}{l}")
        sends[group, l] = (send_sems, recv_sems, srcs, lands)
        return token

    gs_layers, token = [None] * depth, all_started
    for l in reversed(range(depth)):
        dx, dxb, gs_layers[l], token = _layer_bwd(
            dx, dxb, saved[l], wls[l], sms[l], tabs, bias, t,
            lambda gw, l=l, tk=token: send("ffn", l, _split_ffn_grads(gw), tk),
            lambda gw, l=l: send("attn", l, _split_attn_grads(gw), dx))
    grad_x = dx.reshape(x.shape)

    srcs, lands = {}, {}

    def arrive(key, after):
        send_sems, recv_sems, s_bufs, l_bufs = sends[key]
        got = _exchange_wait(send_sems, recv_sems, s_bufs, l_bufs, after, kind="scatter",
                             name=f"scatter_wait_{key[0]}{key[1]}")
        for k, n in enumerate(_ATTN if key[0] == "attn" else _FFN):
            srcs[n, key[1]], lands[n, key[1]] = got[k], got[len(s_bufs) + k]

    def summed(names):
        return [_sum_parts([srcs[n, l] for l in range(depth)], [lands[n, l] for l in range(depth)], me, name="sum_" + n)
                for n in names]

    last = ("attn", 0)
    for key in sends:
        if key != last:
            arrive(key, token)
    sums_ffn = summed(_FFN)
    swap = _exchange_start(sums_ffn, [lax.empty(a.shape, F32) for a in sums_ffn], token, kind="swap",
                           name="swap_start_ffn")
    arrive(last, swap[4])
    sums_attn = summed(_ATTN)
    sib_attn = list(_sibling_exchange(sums_attn, name="swap_core_sums_attn"))
    swapped = _exchange_wait(swap[0], swap[1], swap[2], swap[3], sib_attn[0], kind="swap", name="swap_wait_ffn")
    mine_of = dict(zip(_FFN + _ATTN, swapped[:len(_FFN)] + sums_attn))
    other_of = dict(zip(_FFN + _ATTN, swapped[len(_FFN):] + sib_attn))
    grads, deltas, new_m, new_v = {}, {}, {}, {}
    for n in _BIG:
        grads[n], deltas[n], new_m[n], new_v[n] = _adamw(mine_of[n], other_of[n], w[n], m[n], v[n], name="adamw_" + n)

    gsm = {n: jnp.stack([gs_layers[l][n][0] for l in range(depth)]) for n in _SMALL if n != "final_norm"}
    gsm["final_norm"] = g_final[0]
    packed = _pack_small(gsm)
    everyone = _all_gather_small(packed, name="gather_gain_grads").reshape(N_DEV, packed.shape[0], LANE)
    res = _small_adamw(everyone, _pack_small(w), _pack_small(m), _pack_small(v), name="adamw_gains")
    for dst, r in zip((grads, deltas, new_m, new_v), res):
        dst.update(_unpack_small(r, w))

    return (loss, grad_x, *[grads[n] for n in _WEIGHTS], *[deltas[n] for n in _WEIGHTS],
            *[new_m[n] for n in _WEIGHTS], *[new_v[n] for n in _WEIGHTS])
```

```python
import functools
import math

import jax
import jax.numpy as jnp
import numpy as np
from jax import lax
from jax.experimental import pallas as pl
from jax.experimental.pallas import tpu as pltpu

F32 = jnp.float32
BF16 = jnp.bfloat16
MESH = pl.DeviceIdType.MESH

HEAD_DIM = 128
ROPE_THETA = 10000.0
GRID_W = 64
EPS = 1e-6
NEG = -1e30
A_HEADS, A_Q_RANK, A_KV_RANK, A_NOPE, A_ROPE, A_V = 4, 512, 512, 128, 64, 128
B_HEADS = 6
B_PATTERNS = ((128, 1), (512, 4), (2048, 16))
C_HEADS, C_KV_HEADS = 6, 2
C_GROUP = C_HEADS // C_KV_HEADS
A_WIDTH, B_WIDTH, C_WIDTH = A_HEADS * A_V, B_HEADS * HEAD_DIM, C_HEADS * HEAD_DIM
IN_A = A_Q_RANK + A_KV_RANK + A_ROPE
IN_B = 3 * B_WIDTH
IN_C = C_WIDTH + 2 * C_KV_HEADS * HEAD_DIM
ADAM_LR, ADAM_B1, ADAM_B2, ADAM_EPS, ADAM_WD, ADAM_STEP = 0.001, 0.9, 0.999, 1e-08, 0.01, 10

LANE = 128
SUBLANE = 8
VMEM_BYTES_V7X = 64 * 1024 * 1024
VMEM_LIMIT_CAP = VMEM_BYTES_V7X - 8 * 1024 * 1024
N_CHIPS = 4
N_DEV = 8

A_PAD = 12 * LANE
PB_CQ, PB_CKV, PB_KR = 0, 4, 8
PB_BQ, PB_BK, PB_BV = 12, 18, 24
PB_CQH, PB_CKH, PB_CVH = 30, 36, 38
NP = 40 * LANE
A_QK = 2 * LANE


def _pick(n, cap, mult=LANE):
    if n <= cap:
        return n
    t = cap - cap % mult
    while t >= mult:
        if n % t == 0:
            return t
        t -= mult
    return n


def _rows_for(width_bytes, n_rows, target=2 * 1024 * 1024):
    return _pick(n_rows, max(SUBLANE, target // max(width_bytes, 1)), SUBLANE)


def _tile2(rows, cols, target):
    tc = _pick(cols, 4 * LANE)
    if tc < 4 * LANE:
        tc = cols
    return _rows_for(tc * 4, rows, target), tc


def _params(est_bytes):
    limit = int(min(max(est_bytes + (4 << 20), 32 << 20), VMEM_LIMIT_CAP))
    return pltpu.CompilerParams(vmem_limit_bytes=limit)


def _isz(x):
    return jnp.dtype(x.dtype).itemsize


def _hbm(shape, dtype):
    return pltpu.HBM(shape, dtype)


def _pin(*arrays):
    return [pltpu.with_memory_space_constraint(a, pltpu.HBM) for a in arrays]


_DIMS = {"nn": (((1,), (0,)), ((), ())), "nt": (((1,), (1,)), ((), ())), "tn": (((0,), (0,)), ((), ()))}


def _matmul(a, b, *, mode, out_dtype, name, add=None, tm=512, tn=512, col_shards=False, after=None):
    if mode == "tn":
        (k, m), (k2, n) = a.shape, b.shape
    elif mode == "nt":
        (m, k), (n, k2) = a.shape, b.shape
    else:
        (m, k), (k2, n) = a.shape, b.shape
    assert k == k2, (a.shape, b.shape, mode)
    tm, tn = _pick(m, tm), (n // N_CHIPS if col_shards else _pick(n, tn))
    a_spec = pl.BlockSpec((k, tm), lambda i, j: (0, i)) if mode == "tn" else pl.BlockSpec((tm, k), lambda i, j: (i, 0))
    b_spec = pl.BlockSpec((tn, k), lambda i, j: (j, 0)) if mode == "nt" else pl.BlockSpec((k, tn), lambda i, j: (0, j))
    o_spec = pl.BlockSpec((None, tm, tn), lambda i, j: (j, i, 0)) if col_shards else pl.BlockSpec((tm, tn), lambda i, j: (i, j))
    dims = _DIMS[mode]

    def body(*refs):
        a_ref, b_ref, o_ref = refs[0], refs[1], refs[-1]
        acc = lax.dot_general(a_ref[...].astype(BF16), b_ref[...].astype(BF16), dims, preferred_element_type=F32)
        if add is not None:
            acc = acc + refs[2][...].astype(F32)
        o_ref[...] = acc.astype(out_dtype)

    ins, specs = [a, b], [a_spec, b_spec]
    if add is not None:
        ins.append(add)
        specs.append(o_spec)
    if after is not None:
        ins.append(after)
        specs.append(pl.BlockSpec(memory_space=pl.ANY))
    est = 2 * (tm * k * _isz(a) + tn * k * _isz(b) + tm * tn * (jnp.dtype(out_dtype).itemsize + (4 if add is not None else 0)))
    est += (tm + tn) * k * 2 + 2 * tm * tn * 4
    return pl.pallas_call(
        body, name=name, grid=(m // tm, n // tn), in_specs=specs, out_specs=o_spec,
        out_shape=_hbm((N_CHIPS, m, tn) if col_shards else (m, n), out_dtype),
        compiler_params=_params(est),
    )(*_pin(*ins))


def _ffn_up(h, wg, wu, *, name):
    s, d = h.shape
    _, _, c = wg.shape
    tm = _pick(s, 512, SUBLANE)

    def body(h_ref, wg_ref, wu_ref, g_ref, u_ref, a_ref):
        hv = h_ref[...]
        gv = jnp.dot(hv, wg_ref[...], preferred_element_type=F32)
        uv = jnp.dot(hv, wu_ref[...], preferred_element_type=F32)
        g_ref[...] = gv.astype(BF16)
        u_ref[...] = uv.astype(BF16)
        a_ref[...] = (gv / (1.0 + jnp.exp(-gv)) * uv).astype(BF16)

    w_spec = pl.BlockSpec((None, d, c), lambda j, i: (j, 0, 0))
    o_spec = pl.BlockSpec((tm, c), lambda j, i: (i, j))
    est = 2 * (tm * d * 2 + 2 * d * c * 2 + tm * c * 10) + 4 * tm * c * 4
    return pl.pallas_call(
        body, name=name, grid=(N_CHIPS, s // tm), in_specs=[pl.BlockSpec((tm, d), lambda j, i: (i, 0)), w_spec, w_spec],
        out_specs=[o_spec, o_spec, o_spec],
        out_shape=[_hbm((s, N_CHIPS * c), BF16)] * 3,
        compiler_params=_params(est),
    )(*_pin(h, wg, wu))


def _ffn_down_dx(dx, w_down, gate, up, *, name):
    s, d = dx.shape
    f = w_down.shape[0]
    tm, tn = _pick(s, 512, SUBLANE), _pick(f, 512)

    def body(dx_ref, w_ref, g_ref, u_ref, dg_ref, du_ref):
        dact = lax.dot_general(dx_ref[...], w_ref[...], _DIMS["nt"], preferred_element_type=F32)
        gv, uv = g_ref[...].astype(F32), u_ref[...].astype(F32)
        sig = 1.0 / (1.0 + jnp.exp(-gv))
        dg_ref[...] = (dact * uv * (sig * (1.0 + gv * (1.0 - sig)))).astype(BF16)
        du_ref[...] = (dact * (gv * sig)).astype(BF16)

    t_spec = pl.BlockSpec((tm, tn), lambda i, j: (i, j))
    est = 2 * (tm * d * 2 + tn * d * 2 + tm * tn * 12) + 6 * tm * tn * 4
    return pl.pallas_call(
        body, name=name, grid=(s // tm, f // tn),
        in_specs=[pl.BlockSpec((tm, d), lambda i, j: (i, 0)), pl.BlockSpec((tn, d), lambda i, j: (j, 0)), t_spec, t_spec],
        out_specs=[t_spec, t_spec], out_shape=[_hbm((s, f), BF16)] * 2, compiler_params=_params(est),
    )(*_pin(dx, w_down, gate, up))


def _ffn_up_dx(dgate, dup, wg, wu, *, name):
    s, f = dgate.shape
    _, d, c = wg.shape
    tm, tn = _pick(s, 1024, SUBLANE), _pick(d, 1024)
    nk = 2 * N_CHIPS

    def body(dg_ref, du_ref, wg_ref, wu_ref, o_ref, acc):
        kk = pl.program_id(2)

        @pl.when(kk == 0)
        def _():
            acc[...] = jnp.zeros_like(acc)

        @pl.when(kk < N_CHIPS)
        def _():
            acc[...] += lax.dot_general(dg_ref[...], wg_ref[...], _DIMS["nt"], preferred_element_type=F32)

        @pl.when(kk >= N_CHIPS)
        def _():
            acc[...] += lax.dot_general(du_ref[...], wu_ref[...], _DIMS["nt"], preferred_element_type=F32)

        @pl.when(kk == nk - 1)
        def _():
            o_ref[...] = acc[...]

    last = N_CHIPS - 1
    est = 2 * (2 * tm * c * 2 + 2 * tn * c * 2 + tm * tn * 4) + 2 * tm * tn * 4
    return pl.pallas_call(
        body, name=name, grid=(s // tm, d // tn, nk),
        in_specs=[pl.BlockSpec((tm, c), lambda i, j, kk: (i, jnp.minimum(kk, last))),
                  pl.BlockSpec((tm, c), lambda i, j, kk: (i, jnp.maximum(kk - N_CHIPS, 0))),
                  pl.BlockSpec((None, tn, c), lambda i, j, kk: (jnp.minimum(kk, last), j, 0)),
                  pl.BlockSpec((None, tn, c), lambda i, j, kk: (jnp.maximum(kk - N_CHIPS, 0), j, 0))],
        out_specs=pl.BlockSpec((tm, tn), lambda i, j, kk: (i, j)),
        out_shape=_hbm((s, d), F32), scratch_shapes=[pltpu.VMEM((tm, tn), F32)],
        compiler_params=_params(est),
    )(*_pin(dgate, dup, wg, wu))


def _norm_fwd(x, gain, *, wb, cb, nb, shared_gain, out_dtype, name):
    s = x.shape[0]
    ts = _rows_for(wb * 4, s)

    def body(x_ref, g_ref, o_ref):
        xv = x_ref[...].astype(F32)
        r = lax.rsqrt(jnp.mean(xv * xv, axis=1, keepdims=True) + EPS)
        o_ref[...] = ((xv * r) * g_ref[...]).astype(out_dtype)

    return pl.pallas_call(
        body, name=name, grid=(nb, s // ts),
        in_specs=[pl.BlockSpec((ts, wb), lambda n, i: (i, cb + n)),
                  pl.BlockSpec((1, wb), (lambda n, i: (0, 0)) if shared_gain else (lambda n, i: (0, n)))],
        out_specs=pl.BlockSpec((ts, wb), lambda n, i: (i, n)),
        out_shape=_hbm((s, nb * wb), out_dtype), compiler_params=_params(6 * ts * wb * 4),
    )(*_pin(x), gain)


def _norm_bwd(x, gain, dy, *, wb, cb, nb, shared_gain, out_dtype, name, dy_cb=0, add=None, bf16_copy=False):
    s = x.shape[0]
    ts = _rows_for(wb * 4, s, target=1024 * 1024)
    gw = wb if shared_gain else nb * wb

    def body(*refs):
        refs = list(refs)
        dxb_ref = refs.pop() if bf16_copy else None
        if add is None:
            x_ref, g_ref, dy_ref, dx_ref, dg_ref = refs
        else:
            x_ref, g_ref, dy_ref, add_ref, dx_ref, dg_ref = refs
        n, i = pl.program_id(0), pl.program_id(1)
        xv = x_ref[...].astype(F32)
        dyv = dy_ref[...].astype(F32)
        r = lax.rsqrt(jnp.mean(xv * xv, axis=1, keepdims=True) + EPS)
        xh = xv * r
        dyg = dyv * g_ref[...]
        dx = r * (dyg - xh * jnp.mean(dyg * xh, axis=1, keepdims=True))
        if add is not None:
            dx = dx + add_ref[...]
        dx_ref[...] = dx.astype(out_dtype)
        if bf16_copy:
            dxb_ref[...] = dx.astype(BF16)
        first = jnp.logical_and(n == 0, i == 0) if shared_gain else (i == 0)

        @pl.when(first)
        def _():
            dg_ref[...] = jnp.zeros_like(dg_ref)

        dg_ref[...] += jnp.sum(dyv * xh, axis=0, keepdims=True)

    ins = [x, gain, dy]
    specs = [pl.BlockSpec((ts, wb), lambda n, i: (i, cb + n)),
             pl.BlockSpec((1, wb), (lambda n, i: (0, 0)) if shared_gain else (lambda n, i: (0, n))),
             pl.BlockSpec((ts, wb), lambda n, i: (i, dy_cb + n))]
    if add is not None:
        ins.append(add)
        specs.append(pl.BlockSpec((ts, wb), lambda n, i: (i, n)))
    out_specs = [pl.BlockSpec((ts, wb), lambda n, i: (i, n)),
                 pl.BlockSpec((1, wb), (lambda n, i: (0, 0)) if shared_gain else (lambda n, i: (0, n)))]
    out_shape = [_hbm((s, nb * wb), out_dtype), jax.ShapeDtypeStruct((1, gw), F32)]
    if bf16_copy:
        out_specs.append(out_specs[0])
        out_shape.append(_hbm((s, nb * wb), BF16))
    return pl.pallas_call(
        body, name=name, grid=(nb, s // ts), in_specs=specs, out_specs=out_specs, out_shape=out_shape,
        compiler_params=_params(14 * ts * wb * 4),
    )(*_pin(*ins))


def _swap_halves(x, half):
    if 2 * half == LANE:
        return pltpu.roll(x, half, axis=1)
    lane = lax.broadcasted_iota(jnp.int32, x.shape, 1)
    first = jnp.bitwise_and(lane, 2 * half - 1) < half
    return jnp.where(first, pltpu.roll(x, LANE - half, axis=1), pltpu.roll(x, half, axis=1))


def _rope(x, cos_t, sin_t, *, tw, cb, nb, half, sign, out_dtype, name):
    s = x.shape[0]
    ts = _rows_for(tw * 4, s)

    def body(x_ref, c_ref, s_ref, o_ref):
        for q in range(tw // LANE):
            sl = slice(q * LANE, (q + 1) * LANE)
            xv = x_ref[:, sl].astype(F32)
            sv = s_ref[:, sl]
            if sign < 0:
                sv = -sv
            o_ref[:, sl] = (xv * c_ref[:, sl] + _swap_halves(xv, half) * sv).astype(out_dtype)

    return pl.pallas_call(
        body, name=name, grid=(nb, s // ts),
        in_specs=[pl.BlockSpec((ts, tw), lambda n, i: (i, cb + n)),
                  pl.BlockSpec((ts, tw), lambda n, i: (i, 0)),
                  pl.BlockSpec((ts, tw), lambda n, i: (i, 0))],
        out_specs=pl.BlockSpec((ts, tw), lambda n, i: (i, n)),
        out_shape=_hbm((s, nb * tw), out_dtype), compiler_params=_params(10 * ts * tw * 4),
    )(*_pin(x), cos_t, sin_t)


def _cast_cols(x, *, cb, nb, name):
    s = x.shape[0]
    ts = _rows_for(LANE * 4, s)

    def body(x_ref, o_ref):
        o_ref[...] = x_ref[...].astype(BF16)

    return pl.pallas_call(
        body, name=name, grid=(nb, s // ts), in_specs=[pl.BlockSpec((ts, LANE), lambda n, i: (i, cb + n))],
        out_specs=pl.BlockSpec((ts, LANE), lambda n, i: (i, n)),
        out_shape=_hbm((s, nb * LANE), BF16), compiler_params=_params(4 * ts * LANE * 4),
    )(*_pin(x))


def _group_sum(x, *, n_out, g, src, out_dtype, name):
    s = x.shape[0]
    ts = _rows_for(LANE * 4, s)

    def body(*refs):
        acc = refs[0][...].astype(F32)
        for r in refs[1:-1]:
            acc = acc + r[...].astype(F32)
        refs[-1][...] = acc.astype(out_dtype)

    return pl.pallas_call(
        body, name=name, grid=(n_out, s // ts),
        in_specs=[pl.BlockSpec((ts, LANE), functools.partial(lambda n, i, j: (i, src(n, j)), j=j)) for j in range(g)],
        out_specs=pl.BlockSpec((ts, LANE), lambda n, i: (i, n)),
        out_shape=_hbm((s, n_out * LANE), out_dtype), compiler_params=_params(4 * g * ts * LANE * 4),
    )(*_pin(*([x] * g)))


LOG2E = 1.4426950408889634
ATTN_ROW_CHUNK = 256


def _attn_window(i, tq, s, band):
    w, r = band
    start = jnp.clip(i * tq - r, 0, s - w)
    return pl.multiple_of(start, tq), pl.multiple_of((w - tq) - (i * tq - start), LANE)


def _flash_fwd(q, k, v, table, *, hkv, g, dqk, q_cb, k_cb, v_cb, v_step, scale, tq, band, name):
    s = q.shape[0]
    n = s // tq
    hq = hkv * g
    rc = min(tq, ATTN_ROW_CHUNK)
    w = s if band is None else band[0]

    def body(*refs):
        if band is None:
            q_ref, k_ref, v_ref, o_ref, lse_ref = refs
            kw, vw = k_ref[...], v_ref[...]
        else:
            q_ref, k_ref, v_ref, t_ref, o_ref, lse_ref = refs
            start, u = _attn_window(pl.program_id(1), tq, s, band)
            kw, vw = k_ref[pl.ds(start, w), :], v_ref[pl.ds(start, w), :]
        for c in range(tq // rc):
            rows = slice(c * rc, (c + 1) * rc)
            sc = lax.dot_general(q_ref[rows, :], kw, _DIMS["nt"], preferred_element_type=F32) * (scale * LOG2E)
            if band is not None:
                sc = sc + t_ref[rows, pl.ds(u, w)]
            m = jnp.max(sc, axis=1, keepdims=True)
            p = jnp.exp2(sc - m)
            l = jnp.sum(p, axis=1, keepdims=True)
            o_ref[rows, :] = jnp.dot(p.astype(BF16), vw, preferred_element_type=F32) / l
            lse_ref[0, rows, :] = jnp.broadcast_to(m + jnp.log2(l), (rc, LANE))

    ins = [q, k, v]
    specs = [pl.BlockSpec((tq, dqk), lambda h, i: (i, q_cb + h)),
             pl.BlockSpec((s, dqk), lambda h, i: (0, k_cb + h // g)),
             pl.BlockSpec((s, LANE), lambda h, i: (0, v_cb + v_step * (h // g)))]
    if band is not None:
        ins.append(table)
        specs.append(pl.BlockSpec(table.shape, lambda h, i: (0, 0)))
    est = 4 * s * (dqk + LANE) + 6 * rc * w * 4 + 8 * tq * LANE * 4 + (0 if band is None else 2 * table.size * 4)
    return pl.pallas_call(
        body, name=name, grid=(hq, n), in_specs=specs,
        out_specs=[pl.BlockSpec((tq, LANE), lambda h, i: (i, h)), pl.BlockSpec((1, tq, LANE), lambda h, i: (h, i, 0))],
        out_shape=[_hbm((s, hq * LANE), F32), _hbm((hq, s, LANE), F32)],
        compiler_params=_params(est),
    )(*_pin(*ins))


def _flash_bwd(q, k, v, o, do, lse, table, *, hkv, g, dqk, q_cb, k_cb, v_cb, v_step, scale, tq, band, name):
    s = q.shape[0]
    n = s // tq
    hq = hkv * g
    rc = min(tq, ATTN_ROW_CHUNK)
    w = s if band is None else band[0]

    def body(*refs):
        if band is None:
            q_ref, k_ref, v_ref, o_ref, do_ref, lse_ref, dq_ref, dk_ref, dv_ref = refs
            keys = slice(None)
        else:
            q_ref, k_ref, v_ref, o_ref, do_ref, lse_ref, t_ref, dq_ref, dk_ref, dv_ref = refs
            start, u = _attn_window(pl.program_id(1), tq, s, band)
            keys = pl.ds(start, w)
        h, i = pl.program_id(0), pl.program_id(1)

        @pl.when(jnp.logical_and(h % g == 0, i == 0))
        def _():
            dk_ref[...] = jnp.zeros_like(dk_ref)
            dv_ref[...] = jnp.zeros_like(dv_ref)

        kw, vw = k_ref[keys, :], v_ref[keys, :]
        for c in range(tq // rc):
            rows = slice(c * rc, (c + 1) * rc)
            qv = q_ref[rows, :]
            dof = do_ref[rows, :]
            dov = dof.astype(BF16)
            sc = lax.dot_general(qv, kw, _DIMS["nt"], preferred_element_type=F32) * (scale * LOG2E)
            if band is not None:
                sc = sc + t_ref[rows, pl.ds(u, w)]
            p = jnp.exp2(sc - lse_ref[0, rows, 0:1])
            dp = lax.dot_general(dov, vw, _DIMS["nt"], preferred_element_type=F32)
            delta = jnp.sum(dof * o_ref[rows, :], axis=1, keepdims=True)
            ds = (p * (dp - delta) * scale).astype(BF16)
            dv_ref[keys, :] += lax.dot_general(p.astype(BF16), dov, _DIMS["tn"], preferred_element_type=F32)
            dk_ref[keys, :] += lax.dot_general(ds, qv, _DIMS["tn"], preferred_element_type=F32)
            dq_ref[rows, :] = jnp.dot(ds, kw, preferred_element_type=F32)

    ins = [q, k, v, o, do, lse]
    specs = [pl.BlockSpec((tq, dqk), lambda h, i: (i, q_cb + h)),
             pl.BlockSpec((s, dqk), lambda h, i: (0, k_cb + h // g)),
             pl.BlockSpec((s, LANE), lambda h, i: (0, v_cb + v_step * (h // g))),
             pl.BlockSpec((tq, LANE), lambda h, i: (i, h)),
             pl.BlockSpec((tq, LANE), lambda h, i: (i, h)),
             pl.BlockSpec((1, tq, LANE), lambda h, i: (h, i, 0))]
    if band is not None:
        ins.append(table)
        specs.append(pl.BlockSpec(table.shape, lambda h, i: (0, 0)))
    est = (4 + 8) * s * (dqk + LANE) + 10 * rc * w * 4 + 12 * tq * LANE * 4 + (0 if band is None else 2 * table.size * 4)
    return pl.pallas_call(
        body, name=name, grid=(hq, n), in_specs=specs,
        out_specs=[pl.BlockSpec((tq, dqk), lambda h, i: (i, h)),
                   pl.BlockSpec((s, dqk), lambda h, i: (0, h // g)),
                   pl.BlockSpec((s, LANE), lambda h, i: (0, h // g))],
        out_shape=[_hbm((s, hq * dqk), F32), _hbm((s, hkv * dqk), F32),
                   _hbm((s, hkv * LANE), F32)],
        compiler_params=_params(est),
    )(*_pin(*ins))


def _final_loss(x, gain, target, *, name):
    s, d = x.shape
    ts = _rows_for(d * 4, s, target=1024 * 1024)

    def body(x_ref, g_ref, t_ref, dx_ref, dg_ref, loss_ref, dxb_ref):
        i = pl.program_id(0)
        xv = x_ref[...]
        gv = g_ref[...]
        r = lax.rsqrt(jnp.mean(xv * xv, axis=1, keepdims=True) + EPS)
        xh = xv * r
        err = xh * gv - t_ref[...]
        dy = err / d
        dyg = dy * gv
        dx = r * (dyg - xh * jnp.mean(dyg * xh, axis=1, keepdims=True))
        dx_ref[...] = dx
        dxb_ref[...] = dx.astype(BF16)

        @pl.when(i == 0)
        def _():
            dg_ref[...] = jnp.zeros_like(dg_ref)
            loss_ref[...] = jnp.zeros_like(loss_ref)

        dg_ref[...] += jnp.sum(dy * xh, axis=0, keepdims=True)
        part = jnp.sum(jnp.mean(err * err, axis=1, keepdims=True), axis=0, keepdims=True)
        loss_ref[...] += jnp.broadcast_to(0.5 * part, (1, LANE))

    row = pl.BlockSpec((ts, d), lambda i: (i, 0))
    return pl.pallas_call(
        body, name=name, grid=(s // ts,),
        in_specs=[row, pl.BlockSpec((1, d), lambda i: (0, 0)), row],
        out_specs=[row, pl.BlockSpec((1, d), lambda i: (0, 0)), pl.BlockSpec((1, LANE), lambda i: (0, 0)), row],
        out_shape=[_hbm((s, d), F32), jax.ShapeDtypeStruct((1, d), F32),
                   jax.ShapeDtypeStruct((1, LANE), F32), _hbm((s, d), BF16)],
        compiler_params=_params(14 * ts * d * 4),
    )(*_pin(x), gain, *_pin(target))


def _cast_to_slot(x3d, me, *, layer, name):
    _, rows, c = x3d.shape
    tr, tc = _tile2(rows, c, 2 * 1024 * 1024)

    def body(me_ref, x_ref, o_ref):
        o_ref[...] = x_ref[...].astype(BF16)

    return pl.pallas_call(
        body, name=name,
        grid_spec=pltpu.PrefetchScalarGridSpec(
            num_scalar_prefetch=1, grid=(rows // tr, c // tc),
            in_specs=[pl.BlockSpec((None, tr, tc), lambda i, j, me_ref: (layer, i, j))],
            out_specs=pl.BlockSpec((None, tr, tc), lambda i, j, me_ref: (me_ref[0], i, j))),
        out_shape=_hbm((N_CHIPS, rows, c), BF16), compiler_params=_params(6 * tr * tc * 4),
    )(me, *_pin(x3d))


def _sum_parts(srcs, lands, me, *, name):
    depth = len(srcs)
    _, r, c = srcs[0].shape
    tr, tc = _tile2(r, c, 1024 * 1024)
    nt, nc = r // tr, c // tc

    def body(me_ref, *refs):
        o_ref = refs[-1]
        l = pl.program_id(0)
        for k in range(depth):
            @pl.when(l == k)
            def _(k=k):
                acc = refs[k][...].astype(F32)
                for p in range(3):
                    acc = acc + refs[depth + k][p].astype(F32)
                o_ref[...] = acc

    def tile_of(k):
        def f(l, i, j):
            return (jnp.where(l == k, i, jnp.where(l < k, 0, nt - 1)), jnp.where(l == k, j, jnp.where(l < k, 0, nc - 1)))
        return f

    in_specs = [pl.BlockSpec((None, tr, tc), functools.partial(lambda l, i, j, me_ref, f: (me_ref[0], *f(l, i, j)), f=tile_of(k)))
                for k in range(depth)]
    in_specs += [pl.BlockSpec((3, tr, tc), functools.partial(lambda l, i, j, me_ref, f: (0, *f(l, i, j)), f=tile_of(k)))
                 for k in range(depth)]
    return pl.pallas_call(
        body, name=name,
        grid_spec=pltpu.PrefetchScalarGridSpec(
            num_scalar_prefetch=1, grid=(depth, nt, nc), in_specs=in_specs,
            out_specs=pl.BlockSpec((tr, tc), lambda l, i, j, me_ref: (l * nt + i, j))),
        out_shape=_hbm((depth * r, c), F32), compiler_params=_params(depth * 10 * tr * tc * 4),
    )(me, *_pin(*srcs, *lands))


def _adamw_math(w, g, m, v):
    m2 = ADAM_B1 * m + (1.0 - ADAM_B1) * g
    v2 = ADAM_B2 * v + (1.0 - ADAM_B2) * (g * g)
    m_hat = m2 / (1.0 - ADAM_B1 ** ADAM_STEP)
    v_hat = v2 / (1.0 - ADAM_B2 ** ADAM_STEP)
    delta = -ADAM_LR * (m_hat / (jnp.sqrt(v_hat) + ADAM_EPS) + ADAM_WD * w)
    return delta, m2, v2


def _adamw(g_a, g_b, w, m, v, *, name):
    depth, r, c = w.shape
    tr, tc = _tile2(r, c, 512 * 1024)
    nt = r // tr

    def body(a_ref, b_ref, w_ref, m_ref, v_ref, g_out, d_out, m_out, v_out):
        gv = a_ref[...] + b_ref[...]
        delta, m2, v2 = _adamw_math(w_ref[...], gv, m_ref[...], v_ref[...])
        g_out[...] = gv
        d_out[...] = delta
        m_out[...] = m2
        v_out[...] = v2

    flat = pl.BlockSpec((tr, tc), lambda l, i, j: (l * nt + i, j))
    spec = pl.BlockSpec((None, tr, tc), lambda l, i, j: (l, i, j))
    return pl.pallas_call(
        body, name=name, grid=(depth, nt, c // tc), in_specs=[flat, flat, spec, spec, spec], out_specs=[spec] * 4,
        out_shape=[_hbm((depth, r, c), F32)] * 4, compiler_params=_params(22 * tr * tc * 4),
    )(*_pin(g_a, g_b, w, m, v))


def _small_adamw(g_all, w, m, v, *, name):
    r, c = w.shape

    def body(ga_ref, w_ref, m_ref, v_ref, g_out, d_out, m_out, v_out):
        gv = ga_ref[0]
        for j in range(1, N_DEV):
            gv = gv + ga_ref[j]
        delta, m2, v2 = _adamw_math(w_ref[...], gv, m_ref[...], v_ref[...])
        g_out[...] = gv
        d_out[...] = delta
        m_out[...] = m2
        v_out[...] = v2

    return pl.pallas_call(body, name=name, out_shape=[jax.ShapeDtypeStruct((r, c), F32)] * 4)(g_all, w, m, v)


_ANY = pl.BlockSpec(memory_space=pl.ANY)


_HBM = pl.BlockSpec(memory_space=pltpu.HBM)
_SEM = pl.BlockSpec(memory_space=pltpu.SEMAPHORE)
_EFFECT = pltpu.SideEffectType.DATAFLOW_SIDE_EFFECTING


def _peer_chips():
    x, y = lax.axis_index("x"), lax.axis_index("y")
    return 2 * x + y, [(1 - x, y), (x, 1 - y), (1 - x, 1 - y)]


def _exchange_copy(srcs, lands, send_sems, recv_sems, k, p, kind):
    c = lax.axis_index("c")
    if kind == "swap":
        return pltpu.make_async_remote_copy(
            src_ref=srcs[k], dst_ref=lands[k], send_sem=send_sems.at[k], recv_sem=recv_sems.at[k],
            device_id=(lax.axis_index("x"), lax.axis_index("y"), 1 - c), device_id_type=MESH)
    me, peers = _peer_chips()
    px, py = peers[p]
    return pltpu.make_async_remote_copy(
        src_ref=srcs[k].at[2 * px + py] if kind == "scatter" else srcs[k].at[me],
        dst_ref=lands[k].at[p] if kind == "scatter" else lands[k].at[me],
        send_sem=send_sems.at[3 * k + p], recv_sem=recv_sems.at[3 * k + p],
        device_id=(px, py, c), device_id_type=MESH)


def _exchange_start(srcs, lands, after, *, kind, name):
    n = len(srcs)
    npeer = 1 if kind == "swap" else 3
    bufs = list(srcs) + (list(lands) if lands is not None else [])
    nb = len(bufs)

    def body(*refs):
        buf_refs, send_sems, recv_sems = refs[:nb], refs[nb + 1], refs[nb + 2]
        token = refs[-1]
        s_refs = buf_refs[:n]
        l_refs = buf_refs[n:] if lands is not None else s_refs
        for k in range(n):
            for p in range(npeer):
                _exchange_copy(s_refs, l_refs, send_sems, recv_sems, k, p, kind).start()
        token[...] = jnp.zeros_like(token)

    out = pl.pallas_call(
        body, name=name,
        out_shape=(pltpu.SemaphoreType.DMA((npeer * n,)), pltpu.SemaphoreType.DMA((npeer * n,)),
                   *[pltpu.HBM(b.shape, b.dtype) for b in bufs], jax.ShapeDtypeStruct((SUBLANE, LANE), F32)),
        in_specs=[_HBM] * nb + [_ANY],
        out_specs=(_SEM, _SEM, *[_HBM] * nb, pl.BlockSpec(memory_space=pltpu.VMEM)),
        input_output_aliases={i: 2 + i for i in range(nb)},
        compiler_params=pltpu.CompilerParams(has_side_effects=_EFFECT),
    )(*[pltpu.with_memory_space_constraint(b, pltpu.HBM) for b in bufs], after)
    send_sems, recv_sems = out[0], out[1]
    thru = out[2:2 + nb]
    return send_sems, recv_sems, list(thru[:n]), (list(thru[n:]) if lands is not None else None), out[-1]


def _exchange_wait(send_sems, recv_sems, srcs, lands, after, *, kind, name):
    n = len(srcs)
    npeer = 1 if kind == "swap" else 3
    bufs = list(srcs) + (list(lands) if lands is not None else [])
    nb = len(bufs)

    def body(*refs):
        buf_refs, send_sems_ref, recv_sems_ref = refs[:nb], refs[nb], refs[nb + 1]
        s_refs = buf_refs[:n]
        l_refs = buf_refs[n:] if lands is not None else s_refs
        for k in range(n):
            for p in range(npeer):
                cp = _exchange_copy(s_refs, l_refs, send_sems_ref, recv_sems_ref, k, p, kind)
                cp.wait_send()
                cp.wait_recv()

    out = pl.pallas_call(
        body, name=name, out_shape=tuple(pltpu.HBM(b.shape, b.dtype) for b in bufs),
        in_specs=[_HBM] * nb + [_SEM, _SEM, _ANY], out_specs=tuple([_HBM] * nb),
        input_output_aliases={i: i for i in range(nb)},
        compiler_params=pltpu.CompilerParams(has_side_effects=_EFFECT),
    )(*bufs, send_sems, recv_sems, after)
    return list(out)


def _sibling_exchange(srcs, *, name):
    n = len(srcs)

    def body(*refs):
        src, out = refs[:n], refs[n:2 * n]
        send_sems, recv_sems = refs[2 * n:]
        sibling = (lax.axis_index("x"), lax.axis_index("y"), 1 - lax.axis_index("c"))
        copies = [pltpu.make_async_remote_copy(src_ref=src[k], dst_ref=out[k], send_sem=send_sems.at[k],
                                               recv_sem=recv_sems.at[k], device_id=sibling, device_id_type=MESH)
                  for k in range(n)]
        for cp in copies:
            cp.start()
        for cp in copies:
            cp.wait_recv()
        for cp in copies:
            cp.wait_send()

    return pl.pallas_call(
        body, name=name, in_specs=[_ANY] * n, out_specs=[_ANY] * n,
        out_shape=[jax.ShapeDtypeStruct(a.shape, a.dtype) for a in srcs],
        scratch_shapes=[pltpu.SemaphoreType.DMA((n,)), pltpu.SemaphoreType.DMA((n,))],
    )(*srcs)


def _all_gather_small(block, *, name):
    m_per, ncol = block.shape

    def body(x_ref, out_ref, send_sems, recv_sems, local_sem):
        x, y, c = lax.axis_index("x"), lax.axis_index("y"), lax.axis_index("c")
        me, sibling = (x, y, c), (x, y, 1 - c)
        chips = [(1 - x, y), (x, 1 - y), (1 - x, 1 - y)]

        def rows(px, py, pc):
            return out_ref.at[pl.ds((4 * px + 2 * py + pc) * m_per, m_per), :]

        def copy(k, blk, to, src=None):
            return pltpu.make_async_remote_copy(
                src_ref=rows(*blk) if src is None else src, dst_ref=rows(*blk),
                send_sem=send_sems.at[k], recv_sem=recv_sems.at[k], device_id=to, device_id_type=MESH)

        mine = pltpu.make_async_copy(x_ref, rows(*me), local_sem)
        mine.start()
        first = [copy(0, me, sibling, src=x_ref)]
        first += [copy(1 + j, me, (*chip, c), src=x_ref) for j, chip in enumerate(chips)]
        for cp in first:
            cp.start()
        passed = [copy(4 + j, (*chip, c), sibling) for j, chip in enumerate(chips)]
        for j, chip in enumerate(chips):
            copy(1 + j, (*chip, c), me).wait_recv()
            passed[j].start()
        copy(0, sibling, me).wait_recv()
        for j, chip in enumerate(chips):
            copy(4 + j, (*chip, 1 - c), me).wait_recv()
        for cp in first + passed:
            cp.wait_send()
        mine.wait()

    return pl.pallas_call(
        body, name=name, out_shape=jax.ShapeDtypeStruct((N_DEV * m_per, ncol), block.dtype),
        in_specs=[pl.BlockSpec(memory_space=pltpu.VMEM)], out_specs=pl.BlockSpec(memory_space=pltpu.VMEM),
        scratch_shapes=[pltpu.SemaphoreType.DMA((7,)), pltpu.SemaphoreType.DMA((7,)), pltpu.SemaphoreType.DMA],
    )(block)


def _rope_angles(pos, dim):
    inv = ROPE_THETA ** (-jnp.arange(0, dim, 2, dtype=F32) / dim)
    return pos.astype(F32)[:, None] * inv[None, :]


def _rope_tables(s):
    pos = jnp.arange(s, dtype=jnp.int32)
    rows = s // GRID_W
    row = jnp.repeat(jnp.arange(rows, dtype=jnp.int32), GRID_W)
    col = jnp.tile(jnp.arange(GRID_W, dtype=jnp.int32), rows)
    a1 = _rope_angles(pos, HEAD_DIM)
    aa = _rope_angles(pos, A_ROPE)
    ar = _rope_angles(row, HEAD_DIM // 2)
    ac = _rope_angles(col, HEAD_DIM // 2)
    one = jnp.ones((s, LANE), F32)
    zero = jnp.zeros((s, LANE), F32)
    pad = LANE - A_ROPE
    cos_a = jnp.concatenate([one, jnp.cos(aa), jnp.cos(aa), jnp.ones((s, pad), F32)], axis=1)
    sin_a = jnp.concatenate([zero, -jnp.sin(aa), jnp.sin(aa), jnp.zeros((s, pad), F32)], axis=1)
    cos_b = jnp.concatenate([jnp.cos(a1), jnp.cos(a1)], axis=1)
    sin_b = jnp.concatenate([-jnp.sin(a1), jnp.sin(a1)], axis=1)
    cos_c = jnp.concatenate([jnp.cos(ar), jnp.cos(ar), jnp.cos(ac), jnp.cos(ac)], axis=1)
    sin_c = jnp.concatenate([-jnp.sin(ar), jnp.sin(ar), -jnp.sin(ac), jnp.sin(ac)], axis=1)
    return (cos_a, sin_a), (cos_b, sin_b), (cos_c, sin_c)


def _band_table(tq, s):
    reach = max((win // (2 * d)) * d for win, d in B_PATTERNS)
    r = -(-reach // tq) * tq
    w = min(s, tq + 2 * r)
    j = jnp.arange(tq, dtype=jnp.int32)[:, None]
    x = jnp.arange(2 * w - tq, dtype=jnp.int32)[None, :]
    rel = x - (w - tq) - j
    mult = jnp.zeros(rel.shape, F32)
    for win, d in B_PATTERNS:
        mult = mult + jnp.logical_and(rel % d == 0, jnp.abs(rel) <= (win // (2 * d)) * d).astype(F32)
    return jnp.where(mult > 0, jnp.log2(jnp.maximum(mult, 1.0)), NEG), (w, r)


_BIG = ("w_in", "a_w_uq", "a_w_ukv", "w_out", "w_gate", "w_up", "w_down")
_SMALL = ("attn_norm", "a_q_norm", "a_kv_norm", "c_q_norm", "c_k_norm", "out_norm", "ffn_norm", "final_norm")
_WEIGHTS = ("attn_norm", "w_in", "a_q_norm", "a_w_uq", "a_kv_norm", "a_w_ukv", "c_q_norm", "c_k_norm", "out_norm",
            "w_out", "ffn_norm", "w_gate", "w_up", "w_down", "final_norm")


_ATTN = ("w_in", "a_w_uq", "a_w_ukv")
_FFN = ("w_out", "w_gate", "w_up", "w_down")


def _from_cols(a):
    return jnp.transpose(a, (1, 0, 2)).reshape(a.shape[1], N_CHIPS * a.shape[2])


def _from_rows(a):
    return a.reshape(N_CHIPS * a.shape[1], a.shape[2])


def _to_cols(a):
    return jnp.transpose(a.reshape(a.shape[0], N_CHIPS, a.shape[1] // N_CHIPS), (1, 0, 2))


def _to_rows(a):
    return a.reshape(N_CHIPS, a.shape[0] // N_CHIPS, a.shape[1])


def _assemble_attn(gw):
    w_in_t, uq, ukv = _from_rows(gw[0]), _from_cols(gw[1]), _from_cols(gw[2])
    d = w_in_t.shape[1]
    w_all = jnp.concatenate([w_in_t[:IN_A], jnp.zeros((A_PAD - IN_A, d), BF16), w_in_t[IN_A:]], axis=0)
    uq = uq.reshape(A_Q_RANK, A_HEADS, A_NOPE + A_ROPE)
    uq = jnp.pad(uq, ((0, 0), (0, 0), (0, A_QK - A_NOPE - A_ROPE))).reshape(A_Q_RANK, A_HEADS * A_QK)
    return dict(w_all=w_all, uq=uq, ukv=ukv)


def _assemble_ffn(gw):
    return dict(w_out=_from_rows(gw[0]), w_gate=gw[1], w_up=gw[2], w_down=_from_rows(gw[3]))


def _split_attn_grads(gl):
    w_all = gl["w_all"]
    w_in_t = jnp.concatenate([w_all[:IN_A], w_all[A_PAD:]], axis=0)
    uq = gl["uq"].reshape(A_Q_RANK, A_HEADS, A_QK)[:, :, :A_NOPE + A_ROPE].reshape(A_Q_RANK, A_HEADS * (A_NOPE + A_ROPE))
    return [_to_rows(w_in_t), _to_cols(uq), _to_cols(gl["ukv"])]


def _split_ffn_grads(gl):
    return [_to_rows(gl["w_out"]), gl["w_gate"], gl["w_up"], _to_rows(gl["w_down"])]


def _tie(a, token):
    return a + token[0:1, 0:1]


def _layer_fwd(x, wl, ffn_weights, sm, tabs, bias, t):
    s = x.shape[0]
    (cos_a, sin_a), (cos_b, sin_b), (cos_c, sin_c) = tabs
    h = _norm_fwd(x, sm["attn_norm"], wb=x.shape[1], cb=0, nb=1, shared_gain=True, out_dtype=BF16, name="attn_norm_fwd")
    p = _matmul(h, wl["w_all"], mode="nt", out_dtype=F32, name="in_proj", tm=1024, tn=640)
    cq_n = _norm_fwd(p, sm["a_q_norm"], wb=A_Q_RANK, cb=0, nb=1, shared_gain=True, out_dtype=BF16, name="a_q_norm_fwd")
    ckv_n = _norm_fwd(p, sm["a_kv_norm"], wb=A_KV_RANK, cb=1, nb=1, shared_gain=True, out_dtype=BF16, name="a_kv_norm_fwd")
    qa_raw = _matmul(cq_n, wl["uq"], mode="nn", out_dtype=F32, name="a_uq", tm=1024, tn=1024)
    kv = _matmul(ckv_n, wl["ukv"], mode="nn", out_dtype=BF16, name="a_ukv", tm=1024, tn=1024)
    k_nope = kv.reshape(s, A_HEADS, 2, LANE)[:, :, 0].astype(F32)
    k_rope = jnp.broadcast_to(p[:, PB_KR * LANE:(PB_KR + 1) * LANE][:, None, :], (s, A_HEADS, LANE))
    ka_raw = jnp.stack([k_nope, k_rope], axis=2).reshape(s, A_HEADS * A_QK)
    qa = _rope(qa_raw, cos_a, sin_a, tw=A_QK, cb=0, nb=A_HEADS, half=A_ROPE // 2, sign=1, out_dtype=BF16, name="a_rope_q")
    ka = _rope(ka_raw, cos_a, sin_a, tw=A_QK, cb=0, nb=A_HEADS, half=A_ROPE // 2, sign=1, out_dtype=BF16, name="a_rope_k")
    oa, lse_a = _flash_fwd(qa, ka, kv, None, hkv=A_HEADS, g=1, dqk=A_QK, q_cb=0, k_cb=0, v_cb=1, v_step=2,
                           scale=(A_NOPE + A_ROPE) ** -0.5, tq=_pick(s, 2 * t), band=None, name="a_flash_fwd")
    table, band = bias
    qb = _rope(p, cos_b, sin_b, tw=LANE, cb=PB_BQ, nb=B_HEADS, half=HEAD_DIM // 2, sign=1, out_dtype=BF16, name="b_rope_q")
    kb = _rope(p, cos_b, sin_b, tw=LANE, cb=PB_BK, nb=B_HEADS, half=HEAD_DIM // 2, sign=1, out_dtype=BF16, name="b_rope_k")
    vb = _cast_cols(p, cb=PB_BV, nb=B_HEADS, name="b_cast_v")
    ob, lse_b = _flash_fwd(qb, kb, vb, table, hkv=B_HEADS, g=1, dqk=LANE, q_cb=0, k_cb=0, v_cb=0, v_step=1,
                           scale=HEAD_DIM ** -0.5, tq=t, band=band, name="b_flash_fwd")
    qn = _norm_fwd(p, sm["c_q_norm"], wb=LANE, cb=PB_CQH, nb=C_HEADS, shared_gain=True, out_dtype=F32, name="c_q_norm_fwd")
    kn = _norm_fwd(p, sm["c_k_norm"], wb=LANE, cb=PB_CKH, nb=C_KV_HEADS, shared_gain=True, out_dtype=F32, name="c_k_norm_fwd")
    qc = _rope(qn, cos_c, sin_c, tw=LANE, cb=0, nb=C_HEADS, half=HEAD_DIM // 4, sign=1, out_dtype=BF16, name="c_rope_q")
    kc = _rope(kn, cos_c, sin_c, tw=LANE, cb=0, nb=C_KV_HEADS, half=HEAD_DIM // 4, sign=1, out_dtype=BF16, name="c_rope_k")
    vc = _cast_cols(p, cb=PB_CVH, nb=C_KV_HEADS, name="c_cast_v")
    oc, lse_c = _flash_fwd(qc, kc, vc, None, hkv=C_KV_HEADS, g=C_GROUP, dqk=LANE, q_cb=0, k_cb=0, v_cb=0, v_step=1,
                           scale=HEAD_DIM ** -0.5, tq=_pick(s, 2 * t), band=None, name="c_flash_fwd")
    g_out = sm["out_norm"]
    ga, gb, gc = g_out[:, :A_WIDTH], g_out[:, A_WIDTH:A_WIDTH + B_WIDTH], g_out[:, A_WIDTH + B_WIDTH:]
    ya = _norm_fwd(oa, ga, wb=A_WIDTH, cb=0, nb=1, shared_gain=True, out_dtype=BF16, name="out_norm_a_fwd")
    yb = _norm_fwd(ob, gb, wb=B_WIDTH, cb=0, nb=1, shared_gain=True, out_dtype=BF16, name="out_norm_b_fwd")
    yc = _norm_fwd(oc, gc, wb=C_WIDTH, cb=0, nb=1, shared_gain=True, out_dtype=BF16, name="out_norm_c_fwd")
    y = jnp.concatenate([ya, yb, yc], axis=1)
    wl = {**wl, **ffn_weights(y)}
    x1 = _matmul(y, wl["w_out"], mode="nn", out_dtype=F32, name="out_proj", add=x, tm=1024, tn=512)
    h2 = _norm_fwd(x1, sm["ffn_norm"], wb=x.shape[1], cb=0, nb=1, shared_gain=True, out_dtype=BF16, name="ffn_norm_fwd")
    gate, up, act = _ffn_up(h2, wl["w_gate"], wl["w_up"], name="ffn_up")
    x2 =_matmul(act, wl["w_down"], mode="nn", out_dtype=F32, name="ffn_down", add=x1, tm=512, tn=512)
    saved = dict(x=x, h=h, p=p, cq_n=cq_n, ckv_n=ckv_n, kv=kv, qa=qa, ka=ka, oa=oa, lse_a=lse_a, qb=qb, kb=kb, vb=vb, ob=ob,
                 lse_b=lse_b, qc=qc, kc=kc, vc=vc, oc=oc, lse_c=lse_c, y=y, x1=x1, h2=h2, gate=gate, up=up, act=act)
    return x2, saved, wl


def _layer_bwd(dx2, dx2b, sv, wl, sm, tabs, bias, t, send_ffn, send_attn):
    s, d = dx2.shape
    (cos_a, sin_a), (cos_b, sin_b), (cos_c, sin_c) = tabs
    gw, gs = {}, {}
    dgate, dup = _ffn_down_dx(dx2b, wl["w_down"], sv["gate"], sv["up"], name="ffn_down_dx")
    gw["w_down"] = _matmul(sv["act"], dx2b, mode="tn", out_dtype=BF16, name="ffn_down_dw", tm=512, tn=512)
    dh2 = _ffn_up_dx(dgate, dup, wl["w_gate"], wl["w_up"], name="ffn_up_dx")
    gw["w_gate"] = _matmul(sv["h2"], dgate, mode="tn", out_dtype=BF16, name="ffn_gate_dw", tm=512, col_shards=True)
    gw["w_up"] = _matmul(sv["h2"], dup, mode="tn", out_dtype=BF16, name="ffn_up_dw", tm=512, col_shards=True)
    dx1, gs["ffn_norm"], dx1b = _norm_bwd(sv["x1"], sm["ffn_norm"], dh2, wb=d, cb=0, nb=1, shared_gain=True,
                                          out_dtype=F32, name="ffn_norm_bwd", add=dx2, bf16_copy=True)
    dy = _matmul(dx1b, wl["w_out"], mode="nt", out_dtype=F32, name="out_proj_dx", tm=512, tn=512)
    gw["w_out"] = _matmul(sv["y"], dx1b, mode="tn", out_dtype=BF16, name="out_proj_dw", tm=512, tn=512)
    token = send_ffn(gw)
    g_out = _tie(sm["out_norm"], token)
    ga, gb, gc = g_out[:, :A_WIDTH], g_out[:, A_WIDTH:A_WIDTH + B_WIDTH], g_out[:, A_WIDTH + B_WIDTH:]
    dya, dyb, dyc = dy[:, :A_WIDTH], dy[:, A_WIDTH:A_WIDTH + B_WIDTH], dy[:, A_WIDTH + B_WIDTH:]
    doa, dga = _norm_bwd(sv["oa"], ga, dya, wb=A_WIDTH, cb=0, nb=1, shared_gain=True, out_dtype=F32, name="out_norm_a_bwd")
    dob, dgb = _norm_bwd(sv["ob"], gb, dyb, wb=B_WIDTH, cb=0, nb=1, shared_gain=True, out_dtype=F32, name="out_norm_b_bwd")
    doc, dgc = _norm_bwd(sv["oc"], gc, dyc, wb=C_WIDTH, cb=0, nb=1, shared_gain=True, out_dtype=F32, name="out_norm_c_bwd")
    gs["out_norm"] = jnp.concatenate([dga, dgb, dgc], axis=1)
    p = sv["p"]
    dqc, dkc, dvc = _flash_bwd(sv["qc"], sv["kc"], sv["vc"], sv["oc"], doc, sv["lse_c"], None, hkv=C_KV_HEADS,
                               g=C_GROUP, dqk=LANE, q_cb=0, k_cb=0, v_cb=0, v_step=1, scale=HEAD_DIM ** -0.5,
                               tq=_pick(s, 2 * t), band=None, name="c_flash_bwd")
    dqn = _rope(dqc, cos_c, sin_c, tw=LANE, cb=0, nb=C_HEADS, half=HEAD_DIM // 4, sign=-1, out_dtype=F32, name="c_rope_q_bwd")
    dkn = _rope(dkc, cos_c, sin_c, tw=LANE, cb=0, nb=C_KV_HEADS, half=HEAD_DIM // 4, sign=-1, out_dtype=F32, name="c_rope_k_bwd")
    dpcq, gs["c_q_norm"] = _norm_bwd(p, sm["c_q_norm"], dqn, wb=LANE, cb=PB_CQH, nb=C_HEADS, shared_gain=True,
                                     out_dtype=BF16, name="c_q_norm_bwd")
    dpck, gs["c_k_norm"] = _norm_bwd(p, sm["c_k_norm"], dkn, wb=LANE, cb=PB_CKH, nb=C_KV_HEADS, shared_gain=True,
                                     out_dtype=BF16, name="c_k_norm_bwd")
    table, band = bias
    dqb, dkb, dvb = _flash_bwd(sv["qb"], sv["kb"], sv["vb"], sv["ob"], dob, sv["lse_b"], table, hkv=B_HEADS, g=1,
                               dqk=LANE, q_cb=0, k_cb=0, v_cb=0, v_step=1, scale=HEAD_DIM ** -0.5, tq=t, band=band,
                               name="b_flash_bwd")
    dpbq = _rope(dqb, cos_b, sin_b, tw=LANE, cb=0, nb=B_HEADS, half=HEAD_DIM // 2, sign=-1, out_dtype=BF16, name="b_rope_q_bwd")
    dpbk = _rope(dkb, cos_b, sin_b, tw=LANE, cb=0, nb=B_HEADS, half=HEAD_DIM // 2, sign=-1, out_dtype=BF16, name="b_rope_k_bwd")
    dqa, dka, dva = _flash_bwd(sv["qa"], sv["ka"], sv["kv"], sv["oa"], doa, sv["lse_a"], None, hkv=A_HEADS, g=1,
                               dqk=A_QK, q_cb=0, k_cb=0, v_cb=1, v_step=2, scale=(A_NOPE + A_ROPE) ** -0.5,
                               tq=_pick(s, 2 * t), band=None, name="a_flash_bwd")
    dqa_raw = _rope(dqa, cos_a, sin_a, tw=A_QK, cb=0, nb=A_HEADS, half=A_ROPE // 2, sign=-1, out_dtype=BF16, name="a_rope_q_bwd")
    dka_raw = _rope(dka, cos_a, sin_a, tw=A_QK, cb=0, nb=A_HEADS, half=A_ROPE // 2, sign=-1, out_dtype=BF16, name="a_rope_k_bwd")
    dkr = _group_sum(dka_raw, n_out=1, g=A_HEADS, src=lambda n, j: 2 * j + 1, out_dtype=BF16, name="a_k_rope_sum")
    dkv = jnp.stack([dka_raw.reshape(s, A_HEADS, 2, LANE)[:, :, 0], dva.reshape(s, A_HEADS, LANE).astype(BF16)], axis=2)
    dkv = dkv.reshape(s, A_HEADS * 2 * LANE)
    dckv_n = _matmul(dkv, wl["ukv"], mode="nt", out_dtype=F32, name="a_ukv_dx", tm=1024, tn=512)
    gw["ukv"] = _matmul(sv["ckv_n"], dkv, mode="tn", out_dtype=BF16, name="a_ukv_dw", tm=512, tn=1024)
    dcq_n = _matmul(dqa_raw, wl["uq"], mode="nt", out_dtype=F32, name="a_uq_dx", tm=1024, tn=512)
    gw["uq"] = _matmul(sv["cq_n"], dqa_raw, mode="tn", out_dtype=BF16, name="a_uq_dw", tm=512, tn=1024)
    dcq, gs["a_q_norm"] = _norm_bwd(p, sm["a_q_norm"], dcq_n, wb=A_Q_RANK, cb=0, nb=1, shared_gain=True, out_dtype=BF16,
                                    name="a_q_norm_bwd")
    dckv, gs["a_kv_norm"] = _norm_bwd(p, sm["a_kv_norm"], dckv_n, wb=A_KV_RANK, cb=1, nb=1, shared_gain=True,
                                      out_dtype=BF16, name="a_kv_norm_bwd")
    dp = jnp.concatenate([dcq, dckv, dkr, jnp.zeros((s, A_PAD - (PB_KR + 1) * LANE), BF16), dpbq, dpbk,
                          dvb.astype(BF16), dpcq, dpck, dvc.astype(BF16)], axis=1)
    gw["w_all"] = _matmul(dp, sv["h"], mode="tn", out_dtype=BF16, name="in_proj_dw", tm=640, tn=512)
    token = send_attn(gw)
    dh = _matmul(dp, wl["w_all"], mode="nn", out_dtype=F32, name="in_proj_dx", tm=512, tn=512, after=token)
    dx, gs["attn_norm"], dxb = _norm_bwd(sv["x"], sm["attn_norm"], dh, wb=d, cb=0, nb=1, shared_gain=True,
                                         out_dtype=F32, name="attn_norm_bwd", add=dx1, bf16_copy=True)
    return dx, dxb, gs, token


def _pack_small(vals):
    flat = jnp.concatenate([vals[n].reshape(-1).astype(F32) for n in _SMALL])
    tile = SUBLANE * LANE
    padded = -(-flat.shape[0] // tile) * tile
    return jnp.pad(flat, (0, padded - flat.shape[0])).reshape(padded // LANE, LANE)


def _unpack_small(packed, like):
    flat = packed.reshape(-1)
    out, off = {}, 0
    for n in _SMALL:
        size = math.prod(like[n].shape)
        out[n] = flat[off:off + size].reshape(like[n].shape)
        off += size
    return out


def kernel(x, attn_norm, w_in, a_q_norm, a_w_uq, a_kv_norm, a_w_ukv, c_q_norm, c_k_norm, out_norm, w_out, ffn_norm, w_gate, w_up, w_down, final_norm, loss_target, m_attn_norm, m_w_in, m_a_q_norm, m_a_w_uq, m_a_kv_norm, m_a_w_ukv, m_c_q_norm, m_c_k_norm, m_out_norm, m_w_out, m_ffn_norm, m_w_gate, m_w_up, m_w_down, m_final_norm, v_attn_norm, v_w_in, v_a_q_norm, v_a_w_uq, v_a_kv_norm, v_a_w_ukv, v_c_q_norm, v_c_k_norm, v_out_norm, v_w_out, v_ffn_norm, v_w_gate, v_w_up, v_w_down, v_final_norm):
    w = dict(attn_norm=attn_norm, w_in=w_in, a_q_norm=a_q_norm, a_w_uq=a_w_uq, a_kv_norm=a_kv_norm, a_w_ukv=a_w_ukv,
             c_q_norm=c_q_norm, c_k_norm=c_k_norm, out_norm=out_norm, w_out=w_out, ffn_norm=ffn_norm, w_gate=w_gate,
             w_up=w_up, w_down=w_down, final_norm=final_norm)
    m = dict(attn_norm=m_attn_norm, w_in=m_w_in, a_q_norm=m_a_q_norm, a_w_uq=m_a_w_uq, a_kv_norm=m_a_kv_norm,
             a_w_ukv=m_a_w_ukv, c_q_norm=m_c_q_norm, c_k_norm=m_c_k_norm, out_norm=m_out_norm, w_out=m_w_out,
             ffn_norm=m_ffn_norm, w_gate=m_w_gate, w_up=m_w_up, w_down=m_w_down, final_norm=m_final_norm)
    v = dict(attn_norm=v_attn_norm, w_in=v_w_in, a_q_norm=v_a_q_norm, a_w_uq=v_a_w_uq, a_kv_norm=v_a_kv_norm,
             a_w_ukv=v_a_w_ukv, c_q_norm=v_c_q_norm, c_k_norm=v_c_k_norm, out_norm=v_out_norm, w_out=v_w_out,
             ffn_norm=v_ffn_norm, w_gate=v_w_gate, w_up=v_w_up, w_down=v_w_down, final_norm=v_final_norm)
    _, s, d = x.shape
    depth = attn_norm.shape[0]

    def as_stored(a, n):
        return jnp.swapaxes(a, 1, 2) if n == "w_in" else a
    t = _pick(s, 512)

    me = (2 * lax.axis_index("x") + lax.axis_index("y")).astype(jnp.int32).reshape(1)

    gathers, after = {}, me
    for l in range(depth):
        for group, names in (("attn", _ATTN), ("ffn", _FFN)):
            bufs = [_cast_to_slot(as_stored(w[n], n), me, layer=l, name=f"cast_{n}")
                    for n in names]
            send_sems, recv_sems, bufs, _, after = _exchange_start(bufs, None, after, kind="gather",
                                                                   name=f"gather_start_{group}{l}")
            gathers[group, l] = (send_sems, recv_sems, bufs)
    all_started = after

    def gathered(group, l, after):
        send_sems, recv_sems, bufs = gathers[group, l]
        return _exchange_wait(send_sems, recv_sems, bufs, None, after, kind="gather", name=f"gather_wait_{group}{l}")

    tabs = _rope_tables(s)
    bias = _band_table(t, s)

    xs = x.reshape(s, d)
    saved, wls, sms = [], [], []
    for l in range(depth):
        wl = _assemble_attn(gathered("attn", l, all_started if l == 0 else xs))
        sm = {n: w[n][l][None, :] for n in _SMALL if n != "final_norm"}
        xs, sv, wl = _layer_fwd(xs, wl, lambda after, l=l: _assemble_ffn(gathered("ffn", l, after)), sm, tabs, bias, t)
        saved.append(sv)
        wls.append(wl)
        sms.append(sm)
    dx, g_final, loss_row, dxb = _final_loss(xs, final_norm[None, :], loss_target.reshape(s, d), name="final_loss")
    loss = lax.psum(loss_row[0, 0], ("x", "y", "c"))

    sends = {}

    def send(group, l, srcs, after):
        lands = [lax.empty((3,) + a.shape[1:], BF16) for a in srcs]
        send_sems, recv_sems, srcs, lands, token = _exchange_start(srcs, lands, after, kind="scatter",
                                                                   name=f"scatter_start_{group}{l}")
        sends[group, l] = (send_sems, recv_sems, srcs, lands)
        return token

    gs_layers, token = [None] * depth, all_started
    for l in reversed(range(depth)):
        dx, dxb, gs_layers[l], token = _layer_bwd(
            dx, dxb, saved[l], wls[l], sms[l], tabs, bias, t,
            lambda gw, l=l, tk=token: send("ffn", l, _split_ffn_grads(gw), tk),
            lambda gw, l=l: send("attn", l, _split_attn_grads(gw), dx))
    grad_x = dx.reshape(x.shape)

    srcs, lands = {}, {}

    def arrive(key, after):
        send_sems, recv_sems, s_bufs, l_bufs = sends[key]
        got = _exchange_wait(send_sems, recv_sems, s_bufs, l_bufs, after, kind="scatter",
                             name=f"scatter_wait_{key[0]}{key[1]}")
        for k, n in enumerate(_ATTN if key[0] == "attn" else _FFN):
            srcs[n, key[1]], lands[n, key[1]] = got[k], got[len(s_bufs) + k]

    def summed(names):
        return [_sum_parts([srcs[n, l] for l in range(depth)], [lands[n, l] for l in range(depth)], me, name="sum_" + n)
                for n in names]

    last = ("attn", 0)
    for key in sends:
        if key != last:
            arrive(key, token)
    sums_ffn = summed(_FFN)
    swap = _exchange_start(sums_ffn, [lax.empty(a.shape, F32) for a in sums_ffn], token, kind="swap",
                           name="swap_start_ffn")
    arrive(last, swap[4])
    sums_attn = summed(_ATTN)
    sib_attn = list(_sibling_exchange(sums_attn, name="swap_core_sums_attn"))
    swapped = _exchange_wait(swap[0], swap[1], swap[2], swap[3], sib_attn[0], kind="swap", name="swap_wait_ffn")
    mine_of = dict(zip(_FFN + _ATTN, swapped[:len(_FFN)] + sums_attn))
    other_of = dict(zip(_FFN + _ATTN, swapped[len(_FFN):] + sib_attn))
    grads, deltas, new_m, new_v = {}, {}, {}, {}
    for n in _BIG:
        res = _adamw(mine_of[n], other_of[n], as_stored(w[n], n), as_stored(m[n], n), as_stored(v[n], n), name="adamw_" + n)
        grads[n], deltas[n], new_m[n], new_v[n] = [as_stored(r, n) for r in res]

    gsm = {n: jnp.stack([gs_layers[l][n][0] for l in range(depth)]) for n in _SMALL if n != "final_norm"}
    gsm["final_norm"] = g_final[0]
    packed = _pack_small(gsm)
    everyone = _all_gather_small(packed, name="gather_gain_grads").reshape(N_DEV, packed.shape[0], LANE)
    res = _small_adamw(everyone, _pack_small(w), _pack_small(m), _pack_small(v), name="adamw_gains")
    for dst, r in zip((grads, deltas, new_m, new_v), res):
        dst.update(_unpack_small(r, w))

    return (loss, grad_x, *[grads[n] for n in _WEIGHTS], *[deltas[n] for n in _WEIGHTS],
            *[new_m[n] for n in _WEIGHTS], *[new_v[n] for n in _WEIGHTS])
```

```python
import functools
import math

import jax
import jax.numpy as jnp
import numpy as np
from jax import lax
from jax.experimental import pallas as pl
from jax.experimental.pallas import tpu as pltpu

F32 = jnp.float32
BF16 = jnp.bfloat16
MESH = pl.DeviceIdType.MESH

HEAD_DIM = 128
ROPE_THETA = 10000.0
GRID_W = 64
EPS = 1e-6
NEG = -1e30
A_HEADS, A_Q_RANK, A_KV_RANK, A_NOPE, A_ROPE, A_V = 4, 512, 512, 128, 64, 128
B_HEADS = 6
B_PATTERNS = ((128, 1), (512, 4), (2048, 16))
C_HEADS, C_KV_HEADS = 6, 2
C_GROUP = C_HEADS // C_KV_HEADS
A_WIDTH, B_WIDTH, C_WIDTH = A_HEADS * A_V, B_HEADS * HEAD_DIM, C_HEADS * HEAD_DIM
IN_A = A_Q_RANK + A_KV_RANK + A_ROPE
IN_B = 3 * B_WIDTH
IN_C = C_WIDTH + 2 * C_KV_HEADS * HEAD_DIM
ADAM_LR, ADAM_B1, ADAM_B2, ADAM_EPS, ADAM_WD, ADAM_STEP = 0.001, 0.9, 0.999, 1e-08, 0.01, 10

LANE = 128
SUBLANE = 8
VMEM_BYTES_V7X = 64 * 1024 * 1024
VMEM_LIMIT_CAP = VMEM_BYTES_V7X - 8 * 1024 * 1024
N_CHIPS = 4
N_DEV = 8

A_PAD = 12 * LANE
PB_CQ, PB_CKV, PB_KR = 0, 4, 8
PB_BQ, PB_BK, PB_BV = 12, 18, 24
PB_CQH, PB_CKH, PB_CVH = 30, 36, 38
NP = 40 * LANE
A_QK = 2 * LANE


def _pick(n, cap, mult=LANE):
    if n <= cap:
        return n
    t = cap - cap % mult
    while t >= mult:
        if n % t == 0:
            return t
        t -= mult
    return n


def _rows_for(width_bytes, n_rows, target=2 * 1024 * 1024):
    return _pick(n_rows, max(SUBLANE, target // max(width_bytes, 1)), SUBLANE)


def _tile2(rows, cols, target):
    tc = _pick(cols, 4 * LANE)
    if tc < 4 * LANE:
        tc = cols
    fits = [t for t in range(SUBLANE, rows + 1, SUBLANE) if rows % t == 0] or [rows]
    return min(fits, key=lambda t: abs(math.log(t * tc * 4 / target))), tc


def _params(est_bytes):
    limit = int(min(max(est_bytes + (4 << 20), 32 << 20), VMEM_LIMIT_CAP))
    return pltpu.CompilerParams(vmem_limit_bytes=limit)


def _isz(x):
    return jnp.dtype(x.dtype).itemsize


def _hbm(shape, dtype):
    return pltpu.HBM(shape, dtype)


def _pin(*arrays):
    return [pltpu.with_memory_space_constraint(a, pltpu.HBM) for a in arrays]


_DIMS = {"nn": (((1,), (0,)), ((), ())), "nt": (((1,), (1,)), ((), ())), "tn": (((0,), (0,)), ((), ()))}


def _matmul(a, b, *, mode, out_dtype, name, add=None, tm=512, tn=512, col_shards=False, after=None):
    if mode == "tn":
        (k, m), (k2, n) = a.shape, b.shape
    elif mode == "nt":
        (m, k), (n, k2) = a.shape, b.shape
    else:
        (m, k), (k2, n) = a.shape, b.shape
    assert k == k2, (a.shape, b.shape, mode)
    tm, tn = _pick(m, tm), (n // N_CHIPS if col_shards else _pick(n, tn))
    a_spec = pl.BlockSpec((k, tm), lambda i, j: (0, i)) if mode == "tn" else pl.BlockSpec((tm, k), lambda i, j: (i, 0))
    b_spec = pl.BlockSpec((tn, k), lambda i, j: (j, 0)) if mode == "nt" else pl.BlockSpec((k, tn), lambda i, j: (0, j))
    o_spec = pl.BlockSpec((None, tm, tn), lambda i, j: (j, i, 0)) if col_shards else pl.BlockSpec((tm, tn), lambda i, j: (i, j))
    dims = _DIMS[mode]

    def body(*refs):
        a_ref, b_ref, o_ref = refs[0], refs[1], refs[-1]
        acc = lax.dot_general(a_ref[...].astype(BF16), b_ref[...].astype(BF16), dims, preferred_element_type=F32)
        if add is not None:
            acc = acc + refs[2][...].astype(F32)
        o_ref[...] = acc.astype(out_dtype)

    ins, specs = [a, b], [a_spec, b_spec]
    if add is not None:
        ins.append(add)
        specs.append(o_spec)
    if after is not None:
        ins.append(after)
        specs.append(pl.BlockSpec(memory_space=pl.ANY))
    est = 2 * (tm * k * _isz(a) + tn * k * _isz(b) + tm * tn * (jnp.dtype(out_dtype).itemsize + (4 if add is not None else 0)))
    est += (tm + tn) * k * 2 + 2 * tm * tn * 4
    return pl.pallas_call(
        body, name=name, grid=(m // tm, n // tn), in_specs=specs, out_specs=o_spec,
        out_shape=_hbm((N_CHIPS, m, tn) if col_shards else (m, n), out_dtype),
        compiler_params=_params(est),
    )(*_pin(*ins))


def _ffn_up(h, wg, wu, *, name):
    s, d = h.shape
    _, _, c = wg.shape
    tm = _pick(s, 512, SUBLANE)

    def body(h_ref, wg_ref, wu_ref, g_ref, u_ref, a_ref):
        hv = h_ref[...]
        gv = jnp.dot(hv, wg_ref[...], preferred_element_type=F32)
        uv = jnp.dot(hv, wu_ref[...], preferred_element_type=F32)
        g_ref[...] = gv.astype(BF16)
        u_ref[...] = uv.astype(BF16)
        a_ref[...] = (gv / (1.0 + jnp.exp(-gv)) * uv).astype(BF16)

    w_spec = pl.BlockSpec((None, d, c), lambda j, i: (j, 0, 0))
    o_spec = pl.BlockSpec((tm, c), lambda j, i: (i, j))
    est = 2 * (tm * d * 2 + 2 * d * c * 2 + tm * c * 10) + 4 * tm * c * 4
    return pl.pallas_call(
        body, name=name, grid=(N_CHIPS, s // tm), in_specs=[pl.BlockSpec((tm, d), lambda j, i: (i, 0)), w_spec, w_spec],
        out_specs=[o_spec, o_spec, o_spec],
        out_shape=[_hbm((s, N_CHIPS * c), BF16)] * 3,
        compiler_params=_params(est),
    )(*_pin(h, wg, wu))


def _ffn_down_dx(dx, w_down, gate, up, *, name):
    s, d = dx.shape
    f = w_down.shape[0]
    tm, tn = _pick(s, 512, SUBLANE), _pick(f, 512)

    def body(dx_ref, w_ref, g_ref, u_ref, dg_ref, du_ref):
        dact = lax.dot_general(dx_ref[...], w_ref[...], _DIMS["nt"], preferred_element_type=F32)
        gv, uv = g_ref[...].astype(F32), u_ref[...].astype(F32)
        sig = 1.0 / (1.0 + jnp.exp(-gv))
        dg_ref[...] = (dact * uv * (sig * (1.0 + gv * (1.0 - sig)))).astype(BF16)
        du_ref[...] = (dact * (gv * sig)).astype(BF16)

    t_spec = pl.BlockSpec((tm, tn), lambda i, j: (i, j))
    est = 2 * (tm * d * 2 + tn * d * 2 + tm * tn * 12) + 6 * tm * tn * 4
    return pl.pallas_call(
        body, name=name, grid=(s // tm, f // tn),
        in_specs=[pl.BlockSpec((tm, d), lambda i, j: (i, 0)), pl.BlockSpec((tn, d), lambda i, j: (j, 0)), t_spec, t_spec],
        out_specs=[t_spec, t_spec], out_shape=[_hbm((s, f), BF16)] * 2, compiler_params=_params(est),
    )(*_pin(dx, w_down, gate, up))


def _ffn_up_dx(dgate, dup, wg, wu, *, name):
    s, f = dgate.shape
    _, d, c = wg.shape
    tm, tn = _pick(s, 1024, SUBLANE), _pick(d, 1024)
    nk = 2 * N_CHIPS

    def body(dg_ref, du_ref, wg_ref, wu_ref, o_ref, acc):
        kk = pl.program_id(2)

        @pl.when(kk == 0)
        def _():
            acc[...] = jnp.zeros_like(acc)

        @pl.when(kk < N_CHIPS)
        def _():
            acc[...] += lax.dot_general(dg_ref[...], wg_ref[...], _DIMS["nt"], preferred_element_type=F32)

        @pl.when(kk >= N_CHIPS)
        def _():
            acc[...] += lax.dot_general(du_ref[...], wu_ref[...], _DIMS["nt"], preferred_element_type=F32)

        @pl.when(kk == nk - 1)
        def _():
            o_ref[...] = acc[...]

    last = N_CHIPS - 1
    est = 2 * (2 * tm * c * 2 + 2 * tn * c * 2 + tm * tn * 4) + 2 * tm * tn * 4
    return pl.pallas_call(
        body, name=name, grid=(s // tm, d // tn, nk),
        in_specs=[pl.BlockSpec((tm, c), lambda i, j, kk: (i, jnp.minimum(kk, last))),
                  pl.BlockSpec((tm, c), lambda i, j, kk: (i, jnp.maximum(kk - N_CHIPS, 0))),
                  pl.BlockSpec((None, tn, c), lambda i, j, kk: (jnp.minimum(kk, last), j, 0)),
                  pl.BlockSpec((None, tn, c), lambda i, j, kk: (jnp.maximum(kk - N_CHIPS, 0), j, 0))],
        out_specs=pl.BlockSpec((tm, tn), lambda i, j, kk: (i, j)),
        out_shape=_hbm((s, d), F32), scratch_shapes=[pltpu.VMEM((tm, tn), F32)],
        compiler_params=_params(est),
    )(*_pin(dgate, dup, wg, wu))


def _norm_fwd(x, gain, *, wb, cb, nb, shared_gain, out_dtype, name):
    s = x.shape[0]
    ts = _rows_for(wb * 4, s)

    def body(x_ref, g_ref, o_ref):
        xv = x_ref[...].astype(F32)
        r = lax.rsqrt(jnp.mean(xv * xv, axis=1, keepdims=True) + EPS)
        o_ref[...] = ((xv * r) * g_ref[...]).astype(out_dtype)

    return pl.pallas_call(
        body, name=name, grid=(nb, s // ts),
        in_specs=[pl.BlockSpec((ts, wb), lambda n, i: (i, cb + n)),
                  pl.BlockSpec((1, wb), (lambda n, i: (0, 0)) if shared_gain else (lambda n, i: (0, n)))],
        out_specs=pl.BlockSpec((ts, wb), lambda n, i: (i, n)),
        out_shape=_hbm((s, nb * wb), out_dtype), compiler_params=_params(6 * ts * wb * 4),
    )(*_pin(x), gain)


def _norm_bwd(x, gain, dy, *, wb, cb, nb, shared_gain, out_dtype, name, dy_cb=0, add=None, bf16_copy=False):
    s = x.shape[0]
    ts = _rows_for(wb * 4, s, target=1024 * 1024)
    gw = wb if shared_gain else nb * wb

    def body(*refs):
        refs = list(refs)
        dxb_ref = refs.pop() if bf16_copy else None
        if add is None:
            x_ref, g_ref, dy_ref, dx_ref, dg_ref = refs
        else:
            x_ref, g_ref, dy_ref, add_ref, dx_ref, dg_ref = refs
        n, i = pl.program_id(0), pl.program_id(1)
        xv = x_ref[...].astype(F32)
        dyv = dy_ref[...].astype(F32)
        r = lax.rsqrt(jnp.mean(xv * xv, axis=1, keepdims=True) + EPS)
        xh = xv * r
        dyg = dyv * g_ref[...]
        dx = r * (dyg - xh * jnp.mean(dyg * xh, axis=1, keepdims=True))
        if add is not None:
            dx = dx + add_ref[...]
        dx_ref[...] = dx.astype(out_dtype)
        if bf16_copy:
            dxb_ref[...] = dx.astype(BF16)
        first = jnp.logical_and(n == 0, i == 0) if shared_gain else (i == 0)

        @pl.when(first)
        def _():
            dg_ref[...] = jnp.zeros_like(dg_ref)

        dg_ref[...] += jnp.sum(dyv * xh, axis=0, keepdims=True)

    ins = [x, gain, dy]
    specs = [pl.BlockSpec((ts, wb), lambda n, i: (i, cb + n)),
             pl.BlockSpec((1, wb), (lambda n, i: (0, 0)) if shared_gain else (lambda n, i: (0, n))),
             pl.BlockSpec((ts, wb), lambda n, i: (i, dy_cb + n))]
    if add is not None:
        ins.append(add)
        specs.append(pl.BlockSpec((ts, wb), lambda n, i: (i, n)))
    out_specs = [pl.BlockSpec((ts, wb), lambda n, i: (i, n)),
                 pl.BlockSpec((1, wb), (lambda n, i: (0, 0)) if shared_gain else (lambda n, i: (0, n)))]
    out_shape = [_hbm((s, nb * wb), out_dtype), jax.ShapeDtypeStruct((1, gw), F32)]
    if bf16_copy:
        out_specs.append(out_specs[0])
        out_shape.append(_hbm((s, nb * wb), BF16))
    return pl.pallas_call(
        body, name=name, grid=(nb, s // ts), in_specs=specs, out_specs=out_specs, out_shape=out_shape,
        compiler_params=_params(14 * ts * wb * 4),
    )(*_pin(*ins))


def _swap_halves(x, half):
    if 2 * half == LANE:
        return pltpu.roll(x, half, axis=1)
    lane = lax.broadcasted_iota(jnp.int32, x.shape, 1)
    first = jnp.bitwise_and(lane, 2 * half - 1) < half
    return jnp.where(first, pltpu.roll(x, LANE - half, axis=1), pltpu.roll(x, half, axis=1))


def _rope(x, cos_t, sin_t, *, tw, cb, nb, half, sign, out_dtype, name):
    s = x.shape[0]
    ts = _rows_for(tw * 4, s)

    def body(x_ref, c_ref, s_ref, o_ref):
        for q in range(tw // LANE):
            sl = slice(q * LANE, (q + 1) * LANE)
            xv = x_ref[:, sl].astype(F32)
            sv = s_ref[:, sl]
            if sign < 0:
                sv = -sv
            o_ref[:, sl] = (xv * c_ref[:, sl] + _swap_halves(xv, half) * sv).astype(out_dtype)

    return pl.pallas_call(
        body, name=name, grid=(nb, s // ts),
        in_specs=[pl.BlockSpec((ts, tw), lambda n, i: (i, cb + n)),
                  pl.BlockSpec((ts, tw), lambda n, i: (i, 0)),
                  pl.BlockSpec((ts, tw), lambda n, i: (i, 0))],
        out_specs=pl.BlockSpec((ts, tw), lambda n, i: (i, n)),
        out_shape=_hbm((s, nb * tw), out_dtype), compiler_params=_params(10 * ts * tw * 4),
    )(*_pin(x), cos_t, sin_t)


def _cast_cols(x, *, cb, nb, name):
    s = x.shape[0]
    ts = _rows_for(LANE * 4, s)

    def body(x_ref, o_ref):
        o_ref[...] = x_ref[...].astype(BF16)

    return pl.pallas_call(
        body, name=name, grid=(nb, s // ts), in_specs=[pl.BlockSpec((ts, LANE), lambda n, i: (i, cb + n))],
        out_specs=pl.BlockSpec((ts, LANE), lambda n, i: (i, n)),
        out_shape=_hbm((s, nb * LANE), BF16), compiler_params=_params(4 * ts * LANE * 4),
    )(*_pin(x))


def _group_sum(x, *, n_out, g, src, out_dtype, name):
    s = x.shape[0]
    ts = _rows_for(LANE * 4, s)

    def body(*refs):
        acc = refs[0][...].astype(F32)
        for r in refs[1:-1]:
            acc = acc + r[...].astype(F32)
        refs[-1][...] = acc.astype(out_dtype)

    return pl.pallas_call(
        body, name=name, grid=(n_out, s // ts),
        in_specs=[pl.BlockSpec((ts, LANE), functools.partial(lambda n, i, j: (i, src(n, j)), j=j)) for j in range(g)],
        out_specs=pl.BlockSpec((ts, LANE), lambda n, i: (i, n)),
        out_shape=_hbm((s, n_out * LANE), out_dtype), compiler_params=_params(4 * g * ts * LANE * 4),
    )(*_pin(*([x] * g)))


LOG2E = 1.4426950408889634
ATTN_ROW_CHUNK = 256


def _attn_window(i, tq, s, band):
    w, r = band
    start = jnp.clip(i * tq - r, 0, s - w)
    return pl.multiple_of(start, tq), pl.multiple_of((w - tq) - (i * tq - start), LANE)


def _flash_fwd(q, k, v, table, *, hkv, g, dqk, q_cb, k_cb, v_cb, v_step, scale, tq, band, name):
    s = q.shape[0]
    n = s // tq
    hq = hkv * g
    rc = min(tq, ATTN_ROW_CHUNK)
    w = s if band is None else band[0]

    def body(*refs):
        if band is None:
            q_ref, k_ref, v_ref, o_ref, lse_ref = refs
            kw, vw = k_ref[...], v_ref[...]
        else:
            q_ref, k_ref, v_ref, t_ref, o_ref, lse_ref = refs
            start, u = _attn_window(pl.program_id(1), tq, s, band)
            kw, vw = k_ref[pl.ds(start, w), :], v_ref[pl.ds(start, w), :]
        for c in range(tq // rc):
            rows = slice(c * rc, (c + 1) * rc)
            sc = lax.dot_general(q_ref[rows, :], kw, _DIMS["nt"], preferred_element_type=F32) * (scale * LOG2E)
            if band is not None:
                sc = sc + t_ref[rows, pl.ds(u, w)]
            m = jnp.max(sc, axis=1, keepdims=True)
            p = jnp.exp2(sc - m)
            l = jnp.sum(p, axis=1, keepdims=True)
            o_ref[rows, :] = jnp.dot(p.astype(BF16), vw, preferred_element_type=F32) / l
            lse_ref[0, rows, :] = jnp.broadcast_to(m + jnp.log2(l), (rc, LANE))

    ins = [q, k, v]
    specs = [pl.BlockSpec((tq, dqk), lambda h, i: (i, q_cb + h)),
             pl.BlockSpec((s, dqk), lambda h, i: (0, k_cb + h // g)),
             pl.BlockSpec((s, LANE), lambda h, i: (0, v_cb + v_step * (h // g)))]
    if band is not None:
        ins.append(table)
        specs.append(pl.BlockSpec(table.shape, lambda h, i: (0, 0)))
    est = 4 * s * (dqk + LANE) + 6 * rc * w * 4 + 8 * tq * LANE * 4 + (0 if band is None else 2 * table.size * 4)
    return pl.pallas_call(
        body, name=name, grid=(hq, n), in_specs=specs,
        out_specs=[pl.BlockSpec((tq, LANE), lambda h, i: (i, h)), pl.BlockSpec((1, tq, LANE), lambda h, i: (h, i, 0))],
        out_shape=[_hbm((s, hq * LANE), F32), _hbm((hq, s, LANE), F32)],
        compiler_params=_params(est),
    )(*_pin(*ins))


def _flash_bwd(q, k, v, o, do, lse, table, *, hkv, g, dqk, q_cb, k_cb, v_cb, v_step, scale, tq, band, name):
    s = q.shape[0]
    n = s // tq
    hq = hkv * g
    rc = min(tq, ATTN_ROW_CHUNK)
    w = s if band is None else band[0]

    def body(*refs):
        if band is None:
            q_ref, k_ref, v_ref, o_ref, do_ref, lse_ref, dq_ref, dk_ref, dv_ref = refs
            keys = slice(None)
        else:
            q_ref, k_ref, v_ref, o_ref, do_ref, lse_ref, t_ref, dq_ref, dk_ref, dv_ref = refs
            start, u = _attn_window(pl.program_id(1), tq, s, band)
            keys = pl.ds(start, w)
        h, i = pl.program_id(0), pl.program_id(1)

        @pl.when(jnp.logical_and(h % g == 0, i == 0))
        def _():
            dk_ref[...] = jnp.zeros_like(dk_ref)
            dv_ref[...] = jnp.zeros_like(dv_ref)

        kw, vw = k_ref[keys, :], v_ref[keys, :]
        for c in range(tq // rc):
            rows = slice(c * rc, (c + 1) * rc)
            qv = q_ref[rows, :]
            dof = do_ref[rows, :]
            dov = dof.astype(BF16)
            sc = lax.dot_general(qv, kw, _DIMS["nt"], preferred_element_type=F32) * (scale * LOG2E)
            if band is not None:
                sc = sc + t_ref[rows, pl.ds(u, w)]
            p = jnp.exp2(sc - lse_ref[0, rows, 0:1])
            dp = lax.dot_general(dov, vw, _DIMS["nt"], preferred_element_type=F32)
            delta = jnp.sum(dof * o_ref[rows, :], axis=1, keepdims=True)
            ds = (p * (dp - delta) * scale).astype(BF16)
            dv_ref[keys, :] += lax.dot_general(p.astype(BF16), dov, _DIMS["tn"], preferred_element_type=F32)
            dk_ref[keys, :] += lax.dot_general(ds, qv, _DIMS["tn"], preferred_element_type=F32)
            dq_ref[rows, :] = jnp.dot(ds, kw, preferred_element_type=F32)

    ins = [q, k, v, o, do, lse]
    specs = [pl.BlockSpec((tq, dqk), lambda h, i: (i, q_cb + h)),
             pl.BlockSpec((s, dqk), lambda h, i: (0, k_cb + h // g)),
             pl.BlockSpec((s, LANE), lambda h, i: (0, v_cb + v_step * (h // g))),
             pl.BlockSpec((tq, LANE), lambda h, i: (i, h)),
             pl.BlockSpec((tq, LANE), lambda h, i: (i, h)),
             pl.BlockSpec((1, tq, LANE), lambda h, i: (h, i, 0))]
    if band is not None:
        ins.append(table)
        specs.append(pl.BlockSpec(table.shape, lambda h, i: (0, 0)))
    est = (4 + 8) * s * (dqk + LANE) + 10 * rc * w * 4 + 12 * tq * LANE * 4 + (0 if band is None else 2 * table.size * 4)
    return pl.pallas_call(
        body, name=name, grid=(hq, n), in_specs=specs,
        out_specs=[pl.BlockSpec((tq, dqk), lambda h, i: (i, h)),
                   pl.BlockSpec((s, dqk), lambda h, i: (0, h // g)),
                   pl.BlockSpec((s, LANE), lambda h, i: (0, h // g))],
        out_shape=[_hbm((s, hq * dqk), F32), _hbm((s, hkv * dqk), F32),
                   _hbm((s, hkv * LANE), F32)],
        compiler_params=_params(est),
    )(*_pin(*ins))


def _final_loss(x, gain, target, *, name):
    s, d = x.shape
    ts = _rows_for(d * 4, s, target=1024 * 1024)

    def body(x_ref, g_ref, t_ref, dx_ref, dg_ref, loss_ref, dxb_ref):
        i = pl.program_id(0)
        xv = x_ref[...]
        gv = g_ref[...]
        r = lax.rsqrt(jnp.mean(xv * xv, axis=1, keepdims=True) + EPS)
        xh = xv * r
        err = xh * gv - t_ref[...]
        dy = err / d
        dyg = dy * gv
        dx = r * (dyg - xh * jnp.mean(dyg * xh, axis=1, keepdims=True))
        dx_ref[...] = dx
        dxb_ref[...] = dx.astype(BF16)

        @pl.when(i == 0)
        def _():
            dg_ref[...] = jnp.zeros_like(dg_ref)
            loss_ref[...] = jnp.zeros_like(loss_ref)

        dg_ref[...] += jnp.sum(dy * xh, axis=0, keepdims=True)
        part = jnp.sum(jnp.mean(err * err, axis=1, keepdims=True), axis=0, keepdims=True)
        loss_ref[...] += jnp.broadcast_to(0.5 * part, (1, LANE))

    row = pl.BlockSpec((ts, d), lambda i: (i, 0))
    return pl.pallas_call(
        body, name=name, grid=(s // ts,),
        in_specs=[row, pl.BlockSpec((1, d), lambda i: (0, 0)), row],
        out_specs=[row, pl.BlockSpec((1, d), lambda i: (0, 0)), pl.BlockSpec((1, LANE), lambda i: (0, 0)), row],
        out_shape=[_hbm((s, d), F32), jax.ShapeDtypeStruct((1, d), F32),
                   jax.ShapeDtypeStruct((1, LANE), F32), _hbm((s, d), BF16)],
        compiler_params=_params(14 * ts * d * 4),
    )(*_pin(x), gain, *_pin(target))


def _cast_to_slot(x3d, me, *, layer, name):
    _, rows, c = x3d.shape
    tr, tc = _tile2(rows, c, 2 * 1024 * 1024)

    def body(me_ref, x_ref, o_ref):
        o_ref[...] = x_ref[...].astype(BF16)

    return pl.pallas_call(
        body, name=name,
        grid_spec=pltpu.PrefetchScalarGridSpec(
            num_scalar_prefetch=1, grid=(rows // tr, c // tc),
            in_specs=[pl.BlockSpec((None, tr, tc), lambda i, j, me_ref: (layer, i, j))],
            out_specs=pl.BlockSpec((None, tr, tc), lambda i, j, me_ref: (me_ref[0], i, j))),
        out_shape=_hbm((N_CHIPS, rows, c), BF16), compiler_params=_params(6 * tr * tc * 4),
    )(me, *_pin(x3d))


def _sum_parts(srcs, lands, me, *, name):
    depth = len(srcs)
    _, r, c = srcs[0].shape
    tr, tc = _tile2(r, c, 1024 * 1024)
    nt, nc = r // tr, c // tc

    def body(me_ref, *refs):
        o_ref = refs[-1]
        l = pl.program_id(0)
        for k in range(depth):
            @pl.when(l == k)
            def _(k=k):
                acc = refs[k][...].astype(F32)
                for p in range(3):
                    acc = acc + refs[depth + k][p].astype(F32)
                o_ref[...] = acc

    def tile_of(k):
        def f(l, i, j):
            return (jnp.where(l == k, i, jnp.where(l < k, 0, nt - 1)), jnp.where(l == k, j, jnp.where(l < k, 0, nc - 1)))
        return f

    in_specs = [pl.BlockSpec((None, tr, tc), functools.partial(lambda l, i, j, me_ref, f: (me_ref[0], *f(l, i, j)), f=tile_of(k)))
                for k in range(depth)]
    in_specs += [pl.BlockSpec((3, tr, tc), functools.partial(lambda l, i, j, me_ref, f: (0, *f(l, i, j)), f=tile_of(k)))
                 for k in range(depth)]
    return pl.pallas_call(
        body, name=name,
        grid_spec=pltpu.PrefetchScalarGridSpec(
            num_scalar_prefetch=1, grid=(depth, nt, nc), in_specs=in_specs,
            out_specs=pl.BlockSpec((tr, tc), lambda l, i, j, me_ref: (l * nt + i, j))),
        out_shape=_hbm((depth * r, c), F32), compiler_params=_params(depth * 10 * tr * tc * 4),
    )(me, *_pin(*srcs, *lands))


def _adamw_math(w, g, m, v):
    m2 = ADAM_B1 * m + (1.0 - ADAM_B1) * g
    v2 = ADAM_B2 * v + (1.0 - ADAM_B2) * (g * g)
    m_hat = m2 / (1.0 - ADAM_B1 ** ADAM_STEP)
    v_hat = v2 / (1.0 - ADAM_B2 ** ADAM_STEP)
    delta = -ADAM_LR * (m_hat / (jnp.sqrt(v_hat) + ADAM_EPS) + ADAM_WD * w)
    return delta, m2, v2


def _adamw(g_a, g_b, w, m, v, *, name):
    depth, r, c = w.shape
    tr, tc = _tile2(r, c, 512 * 1024)
    nt = r // tr

    def body(a_ref, b_ref, w_ref, m_ref, v_ref, g_out, d_out, m_out, v_out):
        gv = a_ref[...] + b_ref[...]
        delta, m2, v2 = _adamw_math(w_ref[...], gv, m_ref[...], v_ref[...])
        g_out[...] = gv
        d_out[...] = delta
        m_out[...] = m2
        v_out[...] = v2

    flat = pl.BlockSpec((tr, tc), lambda l, i, j: (l * nt + i, j))
    spec = pl.BlockSpec((None, tr, tc), lambda l, i, j: (l, i, j))
    return pl.pallas_call(
        body, name=name, grid=(depth, nt, c // tc), in_specs=[flat, flat, spec, spec, spec], out_specs=[spec] * 4,
        out_shape=[_hbm((depth, r, c), F32)] * 4, compiler_params=_params(22 * tr * tc * 4),
    )(*_pin(g_a, g_b, w, m, v))


def _small_adamw(g_all, w, m, v, *, name):
    r, c = w.shape

    def body(ga_ref, w_ref, m_ref, v_ref, g_out, d_out, m_out, v_out):
        gv = ga_ref[0]
        for j in range(1, N_DEV):
            gv = gv + ga_ref[j]
        delta, m2, v2 = _adamw_math(w_ref[...], gv, m_ref[...], v_ref[...])
        g_out[...] = gv
        d_out[...] = delta
        m_out[...] = m2
        v_out[...] = v2

    return pl.pallas_call(body, name=name, out_shape=[jax.ShapeDtypeStruct((r, c), F32)] * 4)(g_all, w, m, v)


_ANY = pl.BlockSpec(memory_space=pl.ANY)


_HBM = pl.BlockSpec(memory_space=pltpu.HBM)
_SEM = pl.BlockSpec(memory_space=pltpu.SEMAPHORE)
_EFFECT = pltpu.SideEffectType.DATAFLOW_SIDE_EFFECTING


def _peer_chips():
    x, y = lax.axis_index("x"), lax.axis_index("y")
    return 2 * x + y, [(1 - x, y), (x, 1 - y), (1 - x, 1 - y)]


def _exchange_copy(srcs, lands, send_sems, recv_sems, k, p, kind):
    c = lax.axis_index("c")
    if kind == "swap":
        return pltpu.make_async_remote_copy(
            src_ref=srcs[k], dst_ref=lands[k], send_sem=send_sems.at[k], recv_sem=recv_sems.at[k],
            device_id=(lax.axis_index("x"), lax.axis_index("y"), 1 - c), device_id_type=MESH)
    me, peers = _peer_chips()
    px, py = peers[p]
    return pltpu.make_async_remote_copy(
        src_ref=srcs[k].at[2 * px + py] if kind == "scatter" else srcs[k].at[me],
        dst_ref=lands[k].at[p] if kind == "scatter" else lands[k].at[me],
        send_sem=send_sems.at[3 * k + p], recv_sem=recv_sems.at[3 * k + p],
        device_id=(px, py, c), device_id_type=MESH)


def _exchange_start(srcs, lands, after, *, kind, name):
    n = len(srcs)
    npeer = 1 if kind == "swap" else 3
    bufs = list(srcs) + (list(lands) if lands is not None else [])
    nb = len(bufs)

    def body(*refs):
        buf_refs, send_sems, recv_sems = refs[:nb], refs[nb + 1], refs[nb + 2]
        token = refs[-1]
        s_refs = buf_refs[:n]
        l_refs = buf_refs[n:] if lands is not None else s_refs
        for k in range(n):
            for p in range(npeer):
                _exchange_copy(s_refs, l_refs, send_sems, recv_sems, k, p, kind).start()
        token[...] = jnp.zeros_like(token)

    out = pl.pallas_call(
        body, name=name,
        out_shape=(pltpu.SemaphoreType.DMA((npeer * n,)), pltpu.SemaphoreType.DMA((npeer * n,)),
                   *[pltpu.HBM(b.shape, b.dtype) for b in bufs], jax.ShapeDtypeStruct((SUBLANE, LANE), F32)),
        in_specs=[_HBM] * nb + [_ANY],
        out_specs=(_SEM, _SEM, *[_HBM] * nb, pl.BlockSpec(memory_space=pltpu.VMEM)),
        input_output_aliases={i: 2 + i for i in range(nb)},
        compiler_params=pltpu.CompilerParams(has_side_effects=_EFFECT),
    )(*[pltpu.with_memory_space_constraint(b, pltpu.HBM) for b in bufs], after)
    send_sems, recv_sems = out[0], out[1]
    thru = out[2:2 + nb]
    return send_sems, recv_sems, list(thru[:n]), (list(thru[n:]) if lands is not None else None), out[-1]


def _exchange_wait(send_sems, recv_sems, srcs, lands, after, *, kind, name):
    n = len(srcs)
    npeer = 1 if kind == "swap" else 3
    bufs = list(srcs) + (list(lands) if lands is not None else [])
    nb = len(bufs)

    def body(*refs):
        buf_refs, send_sems_ref, recv_sems_ref = refs[:nb], refs[nb], refs[nb + 1]
        s_refs = buf_refs[:n]
        l_refs = buf_refs[n:] if lands is not None else s_refs
        for k in range(n):
            for p in range(npeer):
                cp = _exchange_copy(s_refs, l_refs, send_sems_ref, recv_sems_ref, k, p, kind)
                cp.wait_send()
                cp.wait_recv()

    out = pl.pallas_call(
        body, name=name, out_shape=tuple(pltpu.HBM(b.shape, b.dtype) for b in bufs),
        in_specs=[_HBM] * nb + [_SEM, _SEM, _ANY], out_specs=tuple([_HBM] * nb),
        input_output_aliases={i: i for i in range(nb)},
        compiler_params=pltpu.CompilerParams(has_side_effects=_EFFECT),
    )(*bufs, send_sems, recv_sems, after)
    return list(out)


def _sibling_exchange(srcs, *, name):
    n = len(srcs)

    def body(*refs):
        src, out = refs[:n], refs[n:2 * n]
        send_sems, recv_sems = refs[2 * n:]
        sibling = (lax.axis_index("x"), lax.axis_index("y"), 1 - lax.axis_index("c"))
        copies = [pltpu.make_async_remote_copy(src_ref=src[k], dst_ref=out[k], send_sem=send_sems.at[k],
                                               recv_sem=recv_sems.at[k], device_id=sibling, device_id_type=MESH)
                  for k in range(n)]
        for cp in copies:
            cp.start()
        for cp in copies:
            cp.wait_recv()
        for cp in copies:
            cp.wait_send()

    return pl.pallas_call(
        body, name=name, in_specs=[_ANY] * n, out_specs=[_ANY] * n,
        out_shape=[jax.ShapeDtypeStruct(a.shape, a.dtype) for a in srcs],
        scratch_shapes=[pltpu.SemaphoreType.DMA((n,)), pltpu.SemaphoreType.DMA((n,))],
    )(*srcs)


def _all_gather_small(block, *, name):
    m_per, ncol = block.shape

    def body(x_ref, out_ref, send_sems, recv_sems, local_sem):
        x, y, c = lax.axis_index("x"), lax.axis_index("y"), lax.axis_index("c")
        me, sibling = (x, y, c), (x, y, 1 - c)
        chips = [(1 - x, y), (x, 1 - y), (1 - x, 1 - y)]

        def rows(px, py, pc):
            return out_ref.at[pl.ds((4 * px + 2 * py + pc) * m_per, m_per), :]

        def copy(k, blk, to, src=None):
            return pltpu.make_async_remote_copy(
                src_ref=rows(*blk) if src is None else src, dst_ref=rows(*blk),
                send_sem=send_sems.at[k], recv_sem=recv_sems.at[k], device_id=to, device_id_type=MESH)

        mine = pltpu.make_async_copy(x_ref, rows(*me), local_sem)
        mine.start()
        first = [copy(0, me, sibling, src=x_ref)]
        first += [copy(1 + j, me, (*chip, c), src=x_ref) for j, chip in enumerate(chips)]
        for cp in first:
            cp.start()
        passed = [copy(4 + j, (*chip, c), sibling) for j, chip in enumerate(chips)]
        for j, chip in enumerate(chips):
            copy(1 + j, (*chip, c), me).wait_recv()
            passed[j].start()
        copy(0, sibling, me).wait_recv()
        for j, chip in enumerate(chips):
            copy(4 + j, (*chip, 1 - c), me).wait_recv()
        for cp in first + passed:
            cp.wait_send()
        mine.wait()

    return pl.pallas_call(
        body, name=name, out_shape=jax.ShapeDtypeStruct((N_DEV * m_per, ncol), block.dtype),
        in_specs=[pl.BlockSpec(memory_space=pltpu.VMEM)], out_specs=pl.BlockSpec(memory_space=pltpu.VMEM),
        scratch_shapes=[pltpu.SemaphoreType.DMA((7,)), pltpu.SemaphoreType.DMA((7,)), pltpu.SemaphoreType.DMA],
    )(block)


def _rope_angles(pos, dim):
    inv = ROPE_THETA ** (-jnp.arange(0, dim, 2, dtype=F32) / dim)
    return pos.astype(F32)[:, None] * inv[None, :]


def _rope_tables(s):
    pos = jnp.arange(s, dtype=jnp.int32)
    rows = s // GRID_W
    row = jnp.repeat(jnp.arange(rows, dtype=jnp.int32), GRID_W)
    col = jnp.tile(jnp.arange(GRID_W, dtype=jnp.int32), rows)
    a1 = _rope_angles(pos, HEAD_DIM)
    aa = _rope_angles(pos, A_ROPE)
    ar = _rope_angles(row, HEAD_DIM // 2)
    ac = _rope_angles(col, HEAD_DIM // 2)
    one = jnp.ones((s, LANE), F32)
    zero = jnp.zeros((s, LANE), F32)
    pad = LANE - A_ROPE
    cos_a = jnp.concatenate([one, jnp.cos(aa), jnp.cos(aa), jnp.ones((s, pad), F32)], axis=1)
    sin_a = jnp.concatenate([zero, -jnp.sin(aa), jnp.sin(aa), jnp.zeros((s, pad), F32)], axis=1)
    cos_b = jnp.concatenate([jnp.cos(a1), jnp.cos(a1)], axis=1)
    sin_b = jnp.concatenate([-jnp.sin(a1), jnp.sin(a1)], axis=1)
    cos_c = jnp.concatenate([jnp.cos(ar), jnp.cos(ar), jnp.cos(ac), jnp.cos(ac)], axis=1)
    sin_c = jnp.concatenate([-jnp.sin(ar), jnp.sin(ar), -jnp.sin(ac), jnp.sin(ac)], axis=1)
    return (cos_a, sin_a), (cos_b, sin_b), (cos_c, sin_c)


def _band_table(tq, s):
    reach = max((win // (2 * d)) * d for win, d in B_PATTERNS)
    r = -(-reach // tq) * tq
    w = min(s, tq + 2 * r)
    j = jnp.arange(tq, dtype=jnp.int32)[:, None]
    x = jnp.arange(2 * w - tq, dtype=jnp.int32)[None, :]
    rel = x - (w - tq) - j
    mult = jnp.zeros(rel.shape, F32)
    for win, d in B_PATTERNS:
        mult = mult + jnp.logical_and(rel % d == 0, jnp.abs(rel) <= (win // (2 * d)) * d).astype(F32)
    return jnp.where(mult > 0, jnp.log2(jnp.maximum(mult, 1.0)), NEG), (w, r)


_BIG = ("w_in", "a_w_uq", "a_w_ukv", "w_out", "w_gate", "w_up", "w_down")
_SMALL = ("attn_norm", "a_q_norm", "a_kv_norm", "c_q_norm", "c_k_norm", "out_norm", "ffn_norm", "final_norm")
_WEIGHTS = ("attn_norm", "w_in", "a_q_norm", "a_w_uq", "a_kv_norm", "a_w_ukv", "c_q_norm", "c_k_norm", "out_norm",
            "w_out", "ffn_norm", "w_gate", "w_up", "w_down", "final_norm")


_ATTN = ("w_in", "a_w_uq", "a_w_ukv")
_FFN = ("w_out", "w_gate", "w_up", "w_down")


def _from_cols(a):
    return jnp.transpose(a, (1, 0, 2)).reshape(a.shape[1], N_CHIPS * a.shape[2])


def _from_rows(a):
    return a.reshape(N_CHIPS * a.shape[1], a.shape[2])


def _to_cols(a):
    return jnp.transpose(a.reshape(a.shape[0], N_CHIPS, a.shape[1] // N_CHIPS), (1, 0, 2))


def _to_rows(a):
    return a.reshape(N_CHIPS, a.shape[0] // N_CHIPS, a.shape[1])


def _assemble_attn(gw):
    w_in_t, uq, ukv = _from_rows(gw[0]), _from_cols(gw[1]), _from_cols(gw[2])
    d = w_in_t.shape[1]
    w_all = jnp.concatenate([w_in_t[:IN_A], jnp.zeros((A_PAD - IN_A, d), BF16), w_in_t[IN_A:]], axis=0)
    uq = uq.reshape(A_Q_RANK, A_HEADS, A_NOPE + A_ROPE)
    uq = jnp.pad(uq, ((0, 0), (0, 0), (0, A_QK - A_NOPE - A_ROPE))).reshape(A_Q_RANK, A_HEADS * A_QK)
    return dict(w_all=w_all, uq=uq, ukv=ukv)


def _assemble_ffn(gw):
    return dict(w_out=_from_rows(gw[0]), w_gate=gw[1], w_up=gw[2], w_down=_from_rows(gw[3]))


def _split_attn_grads(gl):
    w_all = gl["w_all"]
    w_in_t = jnp.concatenate([w_all[:IN_A], w_all[A_PAD:]], axis=0)
    uq = gl["uq"].reshape(A_Q_RANK, A_HEADS, A_QK)[:, :, :A_NOPE + A_ROPE].reshape(A_Q_RANK, A_HEADS * (A_NOPE + A_ROPE))
    return [_to_rows(w_in_t), _to_cols(uq), _to_cols(gl["ukv"])]


def _split_ffn_grads(gl):
    return [_to_rows(gl["w_out"]), gl["w_gate"], gl["w_up"], _to_rows(gl["w_down"])]


def _tie(a, token):
    return a + token[0:1, 0:1]


def _layer_fwd(x, wl, ffn_weights, sm, tabs, bias, t):
    s = x.shape[0]
    (cos_a, sin_a), (cos_b, sin_b), (cos_c, sin_c) = tabs
    h = _norm_fwd(x, sm["attn_norm"], wb=x.shape[1], cb=0, nb=1, shared_gain=True, out_dtype=BF16, name="attn_norm_fwd")
    p = _matmul(h, wl["w_all"], mode="nt", out_dtype=F32, name="in_proj", tm=1024, tn=640)
    cq_n = _norm_fwd(p, sm["a_q_norm"], wb=A_Q_RANK, cb=0, nb=1, shared_gain=True, out_dtype=BF16, name="a_q_norm_fwd")
    ckv_n = _norm_fwd(p, sm["a_kv_norm"], wb=A_KV_RANK, cb=1, nb=1, shared_gain=True, out_dtype=BF16, name="a_kv_norm_fwd")
    qa_raw = _matmul(cq_n, wl["uq"], mode="nn", out_dtype=F32, name="a_uq", tm=1024, tn=1024)
    kv = _matmul(ckv_n, wl["ukv"], mode="nn", out_dtype=BF16, name="a_ukv", tm=1024, tn=1024)
    k_nope = kv.reshape(s, A_HEADS, 2, LANE)[:, :, 0].astype(F32)
    k_rope = jnp.broadcast_to(p[:, PB_KR * LANE:(PB_KR + 1) * LANE][:, None, :], (s, A_HEADS, LANE))
    ka_raw = jnp.stack([k_nope, k_rope], axis=2).reshape(s, A_HEADS * A_QK)
    qa = _rope(qa_raw, cos_a, sin_a, tw=A_QK, cb=0, nb=A_HEADS, half=A_ROPE // 2, sign=1, out_dtype=BF16, name="a_rope_q")
    ka = _rope(ka_raw, cos_a, sin_a, tw=A_QK, cb=0, nb=A_HEADS, half=A_ROPE // 2, sign=1, out_dtype=BF16, name="a_rope_k")
    oa, lse_a = _flash_fwd(qa, ka, kv, None, hkv=A_HEADS, g=1, dqk=A_QK, q_cb=0, k_cb=0, v_cb=1, v_step=2,
                           scale=(A_NOPE + A_ROPE) ** -0.5, tq=_pick(s, 2 * t), band=None, name="a_flash_fwd")
    table, band = bias
    qb = _rope(p, cos_b, sin_b, tw=LANE, cb=PB_BQ, nb=B_HEADS, half=HEAD_DIM // 2, sign=1, out_dtype=BF16, name="b_rope_q")
    kb = _rope(p, cos_b, sin_b, tw=LANE, cb=PB_BK, nb=B_HEADS, half=HEAD_DIM // 2, sign=1, out_dtype=BF16, name="b_rope_k")
    vb = _cast_cols(p, cb=PB_BV, nb=B_HEADS, name="b_cast_v")
    ob, lse_b = _flash_fwd(qb, kb, vb, table, hkv=B_HEADS, g=1, dqk=LANE, q_cb=0, k_cb=0, v_cb=0, v_step=1,
                           scale=HEAD_DIM ** -0.5, tq=t, band=band, name="b_flash_fwd")
    qn = _norm_fwd(p, sm["c_q_norm"], wb=LANE, cb=PB_CQH, nb=C_HEADS, shared_gain=True, out_dtype=F32, name="c_q_norm_fwd")
    kn = _norm_fwd(p, sm["c_k_norm"], wb=LANE, cb=PB_CKH, nb=C_KV_HEADS, shared_gain=True, out_dtype=F32, name="c_k_norm_fwd")
    qc = _rope(qn, cos_c, sin_c, tw=LANE, cb=0, nb=C_HEADS, half=HEAD_DIM // 4, sign=1, out_dtype=BF16, name="c_rope_q")
    kc = _rope(kn, cos_c, sin_c, tw=LANE, cb=0, nb=C_KV_HEADS, half=HEAD_DIM // 4, sign=1, out_dtype=BF16, name="c_rope_k")
    vc = _cast_cols(p, cb=PB_CVH, nb=C_KV_HEADS, name="c_cast_v")
    oc, lse_c = _flash_fwd(qc, kc, vc, None, hkv=C_KV_HEADS, g=C_GROUP, dqk=LANE, q_cb=0, k_cb=0, v_cb=0, v_step=1,
                           scale=HEAD_DIM ** -0.5, tq=_pick(s, 2 * t), band=None, name="c_flash_fwd")
    g_out = sm["out_norm"]
    ga, gb, gc = g_out[:, :A_WIDTH], g_out[:, A_WIDTH:A_WIDTH + B_WIDTH], g_out[:, A_WIDTH + B_WIDTH:]
    ya = _norm_fwd(oa, ga, wb=A_WIDTH, cb=0, nb=1, shared_gain=True, out_dtype=BF16, name="out_norm_a_fwd")
    yb = _norm_fwd(ob, gb, wb=B_WIDTH, cb=0, nb=1, shared_gain=True, out_dtype=BF16, name="out_norm_b_fwd")
    yc = _norm_fwd(oc, gc, wb=C_WIDTH, cb=0, nb=1, shared_gain=True, out_dtype=BF16, name="out_norm_c_fwd")
    y = jnp.concatenate([ya, yb, yc], axis=1)
    wl = {**wl, **ffn_weights(y)}
    x1 = _matmul(y, wl["w_out"], mode="nn", out_dtype=F32, name="out_proj", add=x, tm=1024, tn=512)
    h2 = _norm_fwd(x1, sm["ffn_norm"], wb=x.shape[1], cb=0, nb=1, shared_gain=True, out_dtype=BF16, name="ffn_norm_fwd")
    gate, up, act = _ffn_up(h2, wl["w_gate"], wl["w_up"], name="ffn_up")
    x2 =_matmul(act, wl["w_down"], mode="nn", out_dtype=F32, name="ffn_down", add=x1, tm=512, tn=512)
    saved = dict(x=x, h=h, p=p, cq_n=cq_n, ckv_n=ckv_n, kv=kv, qa=qa, ka=ka, oa=oa, lse_a=lse_a, qb=qb, kb=kb, vb=vb, ob=ob,
                 lse_b=lse_b, qc=qc, kc=kc, vc=vc, oc=oc, lse_c=lse_c, y=y, x1=x1, h2=h2, gate=gate, up=up, act=act)
    return x2, saved, wl


def _layer_bwd(dx2, dx2b, sv, wl, sm, tabs, bias, t, send_ffn, send_attn):
    s, d = dx2.shape
    (cos_a, sin_a), (cos_b, sin_b), (cos_c, sin_c) = tabs
    gw, gs = {}, {}
    dgate, dup = _ffn_down_dx(dx2b, wl["w_down"], sv["gate"], sv["up"], name="ffn_down_dx")
    gw["w_down"] = _matmul(sv["act"], dx2b, mode="tn", out_dtype=BF16, name="ffn_down_dw", tm=512, tn=512)
    dh2 = _ffn_up_dx(dgate, dup, wl["w_gate"], wl["w_up"], name="ffn_up_dx")
    gw["w_gate"] = _matmul(sv["h2"], dgate, mode="tn", out_dtype=BF16, name="ffn_gate_dw", tm=512, col_shards=True)
    gw["w_up"] = _matmul(sv["h2"], dup, mode="tn", out_dtype=BF16, name="ffn_up_dw", tm=512, col_shards=True)
    dx1, gs["ffn_norm"], dx1b = _norm_bwd(sv["x1"], sm["ffn_norm"], dh2, wb=d, cb=0, nb=1, shared_gain=True,
                                          out_dtype=F32, name="ffn_norm_bwd", add=dx2, bf16_copy=True)
    dy = _matmul(dx1b, wl["w_out"], mode="nt", out_dtype=F32, name="out_proj_dx", tm=512, tn=512)
    gw["w_out"] = _matmul(sv["y"], dx1b, mode="tn", out_dtype=BF16, name="out_proj_dw", tm=512, tn=512)
    token = send_ffn(gw)
    g_out = _tie(sm["out_norm"], token)
    ga, gb, gc = g_out[:, :A_WIDTH], g_out[:, A_WIDTH:A_WIDTH + B_WIDTH], g_out[:, A_WIDTH + B_WIDTH:]
    dya, dyb, dyc = dy[:, :A_WIDTH], dy[:, A_WIDTH:A_WIDTH + B_WIDTH], dy[:, A_WIDTH + B_WIDTH:]
    doa, dga = _norm_bwd(sv["oa"], ga, dya, wb=A_WIDTH, cb=0, nb=1, shared_gain=True, out_dtype=F32, name="out_norm_a_bwd")
    dob, dgb = _norm_bwd(sv["ob"], gb, dyb, wb=B_WIDTH, cb=0, nb=1, shared_gain=True, out_dtype=F32, name="out_norm_b_bwd")
    doc, dgc = _norm_bwd(sv["oc"], gc, dyc, wb=C_WIDTH, cb=0, nb=1, shared_gain=True, out_dtype=F32, name="out_norm_c_bwd")
    gs["out_norm"] = jnp.concatenate([dga, dgb, dgc], axis=1)
    p = sv["p"]
    dqc, dkc, dvc = _flash_bwd(sv["qc"], sv["kc"], sv["vc"], sv["oc"], doc, sv["lse_c"], None, hkv=C_KV_HEADS,
                               g=C_GROUP, dqk=LANE, q_cb=0, k_cb=0, v_cb=0, v_step=1, scale=HEAD_DIM ** -0.5,
                               tq=_pick(s, 2 * t), band=None, name="c_flash_bwd")
    dqn = _rope(dqc, cos_c, sin_c, tw=LANE, cb=0, nb=C_HEADS, half=HEAD_DIM // 4, sign=-1, out_dtype=F32, name="c_rope_q_bwd")
    dkn = _rope(dkc, cos_c, sin_c, tw=LANE, cb=0, nb=C_KV_HEADS, half=HEAD_DIM // 4, sign=-1, out_dtype=F32, name="c_rope_k_bwd")
    dpcq, gs["c_q_norm"] = _norm_bwd(p, sm["c_q_norm"], dqn, wb=LANE, cb=PB_CQH, nb=C_HEADS, shared_gain=True,
                                     out_dtype=BF16, name="c_q_norm_bwd")
    dpck, gs["c_k_norm"] = _norm_bwd(p, sm["c_k_norm"], dkn, wb=LANE, cb=PB_CKH, nb=C_KV_HEADS, shared_gain=True,
                                     out_dtype=BF16, name="c_k_norm_bwd")
    table, band = bias
    dqb, dkb, dvb = _flash_bwd(sv["qb"], sv["kb"], sv["vb"], sv["ob"], dob, sv["lse_b"], table, hkv=B_HEADS, g=1,
                               dqk=LANE, q_cb=0, k_cb=0, v_cb=0, v_step=1, scale=HEAD_DIM ** -0.5, tq=t, band=band,
                               name="b_flash_bwd")
    dpbq = _rope(dqb, cos_b, sin_b, tw=LANE, cb=0, nb=B_HEADS, half=HEAD_DIM // 2, sign=-1, out_dtype=BF16, name="b_rope_q_bwd")
    dpbk = _rope(dkb, cos_b, sin_b, tw=LANE, cb=0, nb=B_HEADS, half=HEAD_DIM // 2, sign=-1, out_dtype=BF16, name="b_rope_k_bwd")
    dqa, dka, dva = _flash_bwd(sv["qa"], sv["ka"], sv["kv"], sv["oa"], doa, sv["lse_a"], None, hkv=A_HEADS, g=1,
                               dqk=A_QK, q_cb=0, k_cb=0, v_cb=1, v_step=2, scale=(A_NOPE + A_ROPE) ** -0.5,
                               tq=_pick(s, 2 * t), band=None, name="a_flash_bwd")
    dqa_raw = _rope(dqa, cos_a, sin_a, tw=A_QK, cb=0, nb=A_HEADS, half=A_ROPE // 2, sign=-1, out_dtype=BF16, name="a_rope_q_bwd")
    dka_raw = _rope(dka, cos_a, sin_a, tw=A_QK, cb=0, nb=A_HEADS, half=A_ROPE // 2, sign=-1, out_dtype=BF16, name="a_rope_k_bwd")
    dkr = _group_sum(dka_raw, n_out=1, g=A_HEADS, src=lambda n, j: 2 * j + 1, out_dtype=BF16, name="a_k_rope_sum")
    dkv = jnp.stack([dka_raw.reshape(s, A_HEADS, 2, LANE)[:, :, 0], dva.reshape(s, A_HEADS, LANE).astype(BF16)], axis=2)
    dkv = dkv.reshape(s, A_HEADS * 2 * LANE)
    dckv_n = _matmul(dkv, wl["ukv"], mode="nt", out_dtype=F32, name="a_ukv_dx", tm=1024, tn=512)
    gw["ukv"] = _matmul(sv["ckv_n"], dkv, mode="tn", out_dtype=BF16, name="a_ukv_dw", tm=512, tn=1024)
    dcq_n = _matmul(dqa_raw, wl["uq"], mode="nt", out_dtype=F32, name="a_uq_dx", tm=1024, tn=512)
    gw["uq"] = _matmul(sv["cq_n"], dqa_raw, mode="tn", out_dtype=BF16, name="a_uq_dw", tm=512, tn=1024)
    dcq, gs["a_q_norm"] = _norm_bwd(p, sm["a_q_norm"], dcq_n, wb=A_Q_RANK, cb=0, nb=1, shared_gain=True, out_dtype=BF16,
                                    name="a_q_norm_bwd")
    dckv, gs["a_kv_norm"] = _norm_bwd(p, sm["a_kv_norm"], dckv_n, wb=A_KV_RANK, cb=1, nb=1, shared_gain=True,
                                      out_dtype=BF16, name="a_kv_norm_bwd")
    dp = jnp.concatenate([dcq, dckv, dkr, jnp.zeros((s, A_PAD - (PB_KR + 1) * LANE), BF16), dpbq, dpbk,
                          dvb.astype(BF16), dpcq, dpck, dvc.astype(BF16)], axis=1)
    gw["w_all"] = _matmul(dp, sv["h"], mode="tn", out_dtype=BF16, name="in_proj_dw", tm=640, tn=512)
    token = send_attn(gw)
    dh = _matmul(dp, wl["w_all"], mode="nn", out_dtype=F32, name="in_proj_dx", tm=512, tn=512, after=token)
    dx, gs["attn_norm"], dxb = _norm_bwd(sv["x"], sm["attn_norm"], dh, wb=d, cb=0, nb=1, shared_gain=True,
                                         out_dtype=F32, name="attn_norm_bwd", add=dx1, bf16_copy=True)
    return dx, dxb, gs, token


def _pack_small(vals):
    flat = jnp.concatenate([vals[n].reshape(-1).astype(F32) for n in _SMALL])
    tile = SUBLANE * LANE
    padded = -(-flat.shape[0] // tile) * tile
    return jnp.pad(flat, (0, padded - flat.shape[0])).reshape(padded // LANE, LANE)


def _unpack_small(packed, like):
    flat = packed.reshape(-1)
    out, off = {}, 0
    for n in _SMALL:
        size = math.prod(like[n].shape)
        out[n] = flat[off:off + size].reshape(like[n].shape)
        off += size
    return out


def kernel(x, attn_norm, w_in, a_q_norm, a_w_uq, a_kv_norm, a_w_ukv, c_q_norm, c_k_norm, out_norm, w_out, ffn_norm, w_gate, w_up, w_down, final_norm, loss_target, m_attn_norm, m_w_in, m_a_q_norm, m_a_w_uq, m_a_kv_norm, m_a_w_ukv, m_c_q_norm, m_c_k_norm, m_out_norm, m_w_out, m_ffn_norm, m_w_gate, m_w_up, m_w_down, m_final_norm, v_attn_norm, v_w_in, v_a_q_norm, v_a_w_uq, v_a_kv_norm, v_a_w_ukv, v_c_q_norm, v_c_k_norm, v_out_norm, v_w_out, v_ffn_norm, v_w_gate, v_w_up, v_w_down, v_final_norm):
    w = dict(attn_norm=attn_norm, w_in=w_in, a_q_norm=a_q_norm, a_w_uq=a_w_uq, a_kv_norm=a_kv_norm, a_w_ukv=a_w_ukv,
             c_q_norm=c_q_norm, c_k_norm=c_k_norm, out_norm=out_norm, w_out=w_out, ffn_norm=ffn_norm, w_gate=w_gate,
             w_up=w_up, w_down=w_down, final_norm=final_norm)
    m = dict(attn_norm=m_attn_norm, w_in=m_w_in, a_q_norm=m_a_q_norm, a_w_uq=m_a_w_uq, a_kv_norm=m_a_kv_norm,
             a_w_ukv=m_a_w_ukv, c_q_norm=m_c_q_norm, c_k_norm=m_c_k_norm, out_norm=m_out_norm, w_out=m_w_out,
             ffn_norm=m_ffn_norm, w_gate=m_w_gate, w_up=m_w_up, w_down=m_w_down, final_norm=m_final_norm)
    v = dict(attn_norm=v_attn_norm, w_in=v_w_in, a_q_norm=v_a_q_norm, a_w_uq=v_a_w_uq, a_kv_norm=v_a_kv_norm,
             a_w_ukv=v_a_w_ukv, c_q_norm=v_c_q_norm, c_k_norm=v_c_k_norm, out_norm=v_out_norm, w_out=v_w_out,
             ffn_norm=v_ffn_norm, w_gate=v_w_gate, w_up=v_w_up, w_down=v_w_down, final_norm=v_final_norm)
    _, s, d = x.shape
    depth = attn_norm.shape[0]

    def as_stored(a, n):
        return jnp.swapaxes(a, 1, 2) if n == "w_in" else a
    t = _pick(s, 512)

    me = (2 * lax.axis_index("x") + lax.axis_index("y")).astype(jnp.int32).reshape(1)

    gathers, after = {}, me
    for l in range(depth):
        for group, names in (("attn", _ATTN), ("ffn", _FFN)):
            bufs = [_cast_to_slot(as_stored(w[n], n), me, layer=l, name=f"cast_{n}")
                    for n in names]
            send_sems, recv_sems, bufs, _, after = _exchange_start(bufs, None, after, kind="gather",
                                                                   name=f"gather_start_{group}{l}")
            gathers[group, l] = (send_sems, recv_sems, bufs)
    all_started = after

    def gathered(group, l, after):
        send_sems, recv_sems, bufs = gathers[group, l]
        return _exchange_wait(send_sems, recv_sems, bufs, None, after, kind="gather", name=f"gather_wait_{group}{l}")

    tabs = _rope_tables(s)
    bias = _band_table(t, s)

    xs = x.reshape(s, d)
    saved, wls, sms = [], [], []
    for l in range(depth):
        wl = _assemble_attn(gathered("attn", l, all_started if l == 0 else xs))
        sm = {n: w[n][l][None, :] for n in _SMALL if n != "final_norm"}
        xs, sv, wl = _layer_fwd(xs, wl, lambda after, l=l: _assemble_ffn(gathered("ffn", l, after)), sm, tabs, bias, t)
        saved.append(sv)
        wls.append(wl)
        sms.append(sm)
    dx, g_final, loss_row, dxb = _final_loss(xs, final_norm[None, :], loss_target.reshape(s, d), name="final_loss")
    loss = lax.psum(loss_row[0, 0], ("x", "y", "c"))

    sends = {}

    def send(group, l, srcs, after):
        lands = [lax.empty((3,) + a.shape[1:], BF16) for a in srcs]
        send_sems, recv_sems, srcs, lands, token = _exchange_start(srcs, lands, after, kind="scatter",
                                                                   name=f"scatter_start_{group}{l}")
        sends[group, l] = (send_sems, recv_sems, srcs, lands)
        return token

    gs_layers, token = [None] * depth, all_started
    for l in reversed(range(depth)):
        dx, dxb, gs_layers[l], token = _layer_bwd(
            dx, dxb, saved[l], wls[l], sms[l], tabs, bias, t,
            lambda gw, l=l, tk=token: send("ffn", l, _split_ffn_grads(gw), tk),
            lambda gw, l=l: send("attn", l, _split_attn_grads(gw), dx))
    grad_x = dx.reshape(x.shape)

    srcs, lands = {}, {}

    def arrive(key, after):
        send_sems, recv_sems, s_bufs, l_bufs = sends[key]
        got = _exchange_wait(send_sems, recv_sems, s_bufs, l_bufs, after, kind="scatter",
                             name=f"scatter_wait_{key[0]}{key[1]}")
        for k, n in enumerate(_ATTN if key[0] == "attn" else _FFN):
            srcs[n, key[1]], lands[n, key[1]] = got[k], got[len(s_bufs) + k]

    def summed(names):
        return [_sum_parts([srcs[n, l] for l in range(depth)], [lands[n, l] for l in range(depth)], me, name="sum_" + n)
                for n in names]

    last = ("attn", 0)
    for key in sends:
        if key != last:
            arrive(key, token)
    sums_ffn = summed(_FFN)
    swap = _exchange_start(sums_ffn, [lax.empty(a.shape, F32) for a in sums_ffn], token, kind="swap",
                           name="swap_start_ffn")
    arrive(last, swap[4])
    sums_attn = summed(_ATTN)
    sib_attn = list(_sibling_exchange(sums_attn, name="swap_core_sums_attn"))
    swapped = _exchange_wait(swap[0], swap[1], swap[2], swap[3], sib_attn[0], kind="swap", name="swap_wait_ffn")
    mine_of = dict(zip(_FFN + _ATTN, swapped[:len(_FFN)] + sums_attn))
    other_of = dict(zip(_FFN + _ATTN, swapped[len(_FFN):] + sib_attn))
    grads, deltas, new_m, new_v = {}, {}, {}, {}
    for n in _BIG:
        res = _adamw(mine_of[n], other_of[n], as_stored(w[n], n), as_stored(m[n], n), as_stored(v[n], n), name="adamw_" + n)
        grads[n], deltas[n], new_m[n], new_v[n] = [as_stored(r, n) for r in res]

    gsm = {n: jnp.stack([gs_layers[l][n][0] for l in range(depth)]) for n in _SMALL if n != "final_norm"}
    gsm["final_norm"] = g_final[0]
    packed = _pack_small(gsm)
    everyone = _all_gather_small(packed, name="gather_gain_grads").reshape(N_DEV, packed.shape[0], LANE)
    res = _small_adamw(everyone, _pack_small(w), _pack_small(m), _pack_small(v), name="adamw_gains")
    for dst, r in zip((grads, deltas, new_m, new_v), res):
        dst.update(_unpack_small(r, w))

    return (loss, grad_x, *[grads[n] for n in _WEIGHTS], *[deltas[n] for n in _WEIGHTS],
            *[new_m[n] for n in _WEIGHTS], *[new_v[n] for n in _WEIGHTS])
```

```python
import functools
import math

import jax
import jax.numpy as jnp
import numpy as np
from jax import lax
from jax.experimental import pallas as pl
from jax.experimental.pallas import tpu as pltpu

F32 = jnp.float32
BF16 = jnp.bfloat16
MESH = pl.DeviceIdType.MESH

HEAD_DIM = 128
ROPE_THETA = 10000.0
GRID_W = 64
EPS = 1e-6
NEG = -1e30
A_HEADS, A_Q_RANK, A_KV_RANK, A_NOPE, A_ROPE, A_V = 4, 512, 512, 128, 64, 128
B_HEADS = 6
B_PATTERNS = ((128, 1), (512, 4), (2048, 16))
C_HEADS, C_KV_HEADS = 6, 2
C_GROUP = C_HEADS // C_KV_HEADS
A_WIDTH, B_WIDTH, C_WIDTH = A_HEADS * A_V, B_HEADS * HEAD_DIM, C_HEADS * HEAD_DIM
IN_A = A_Q_RANK + A_KV_RANK + A_ROPE
IN_B = 3 * B_WIDTH
IN_C = C_WIDTH + 2 * C_KV_HEADS * HEAD_DIM
ADAM_LR, ADAM_B1, ADAM_B2, ADAM_EPS, ADAM_WD, ADAM_STEP = 0.001, 0.9, 0.999, 1e-08, 0.01, 10

LANE = 128
SUBLANE = 8
VMEM_BYTES_V7X = 64 * 1024 * 1024
VMEM_LIMIT_CAP = VMEM_BYTES_V7X - 8 * 1024 * 1024
N_CHIPS = 4
N_DEV = 8

A_PAD = 12 * LANE
PB_CQ, PB_CKV, PB_KR = 0, 4, 8
PB_BQ, PB_BK, PB_BV = 12, 18, 24
PB_CQH, PB_CKH, PB_CVH = 30, 36, 38
NP = 40 * LANE
A_QK = 2 * LANE


def _pick(n, cap, mult=LANE):
    if n <= cap:
        return n
    t = cap - cap % mult
    while t >= mult:
        if n % t == 0:
            return t
        t -= mult
    return n


def _rows_for(width_bytes, n_rows, target=2 * 1024 * 1024):
    return _pick(n_rows, max(SUBLANE, target // max(width_bytes, 1)), SUBLANE)


def _tile2(rows, cols, target):
    tc = _pick(cols, 4 * LANE)
    if tc < 4 * LANE:
        tc = cols
    fits = [t for t in range(SUBLANE, rows + 1, SUBLANE) if rows % t == 0] or [rows]
    return min(fits, key=lambda t: abs(math.log(t * tc * 4 / target))), tc


def _params(est_bytes):
    limit = int(min(max(est_bytes + (4 << 20), 32 << 20), VMEM_LIMIT_CAP))
    return pltpu.CompilerParams(vmem_limit_bytes=limit)


def _isz(x):
    return jnp.dtype(x.dtype).itemsize


def _hbm(shape, dtype):
    return pltpu.HBM(shape, dtype)


def _pin(*arrays):
    return [pltpu.with_memory_space_constraint(a, pltpu.HBM) for a in arrays]


_DIMS = {"nn": (((1,), (0,)), ((), ())), "nt": (((1,), (1,)), ((), ())), "tn": (((0,), (0,)), ((), ()))}


def _matmul(a, b, *, mode, out_dtype, name, add=None, tm=512, tn=512, col_shards=False, after=None):
    if mode == "tn":
        (k, m), (k2, n) = a.shape, b.shape
    elif mode == "nt":
        (m, k), (n, k2) = a.shape, b.shape
    else:
        (m, k), (k2, n) = a.shape, b.shape
    assert k == k2, (a.shape, b.shape, mode)
    tm, tn = _pick(m, tm), (n // N_CHIPS if col_shards else _pick(n, tn))
    a_spec = pl.BlockSpec((k, tm), lambda i, j: (0, i)) if mode == "tn" else pl.BlockSpec((tm, k), lambda i, j: (i, 0))
    b_spec = pl.BlockSpec((tn, k), lambda i, j: (j, 0)) if mode == "nt" else pl.BlockSpec((k, tn), lambda i, j: (0, j))
    o_spec = pl.BlockSpec((None, tm, tn), lambda i, j: (j, i, 0)) if col_shards else pl.BlockSpec((tm, tn), lambda i, j: (i, j))
    dims = _DIMS[mode]

    def body(*refs):
        a_ref, b_ref, o_ref = refs[0], refs[1], refs[-1]
        acc = lax.dot_general(a_ref[...].astype(BF16), b_ref[...].astype(BF16), dims, preferred_element_type=F32)
        if add is not None:
            acc = acc + refs[2][...].astype(F32)
        o_ref[...] = acc.astype(out_dtype)

    ins, specs = [a, b], [a_spec, b_spec]
    if add is not None:
        ins.append(add)
        specs.append(o_spec)
    if after is not None:
        ins.append(after)
        specs.append(pl.BlockSpec(memory_space=pl.ANY))
    est = 2 * (tm * k * _isz(a) + tn * k * _isz(b) + tm * tn * (jnp.dtype(out_dtype).itemsize + (4 if add is not None else 0)))
    est += (tm + tn) * k * 2 + 2 * tm * tn * 4
    return pl.pallas_call(
        body, name=name, grid=(m // tm, n // tn), in_specs=specs, out_specs=o_spec,
        out_shape=_hbm((N_CHIPS, m, tn) if col_shards else (m, n), out_dtype),
        compiler_params=_params(est),
    )(*_pin(*ins))


def _ffn_up(h, wg, wu, *, name):
    s, d = h.shape
    _, _, c = wg.shape
    tm = _pick(s, 512, SUBLANE)

    def body(h_ref, wg_ref, wu_ref, g_ref, u_ref, a_ref):
        hv = h_ref[...]
        gv = jnp.dot(hv, wg_ref[...], preferred_element_type=F32)
        uv = jnp.dot(hv, wu_ref[...], preferred_element_type=F32)
        g_ref[...] = gv.astype(BF16)
        u_ref[...] = uv.astype(BF16)
        a_ref[...] = (gv / (1.0 + jnp.exp(-gv)) * uv).astype(BF16)

    w_spec = pl.BlockSpec((None, d, c), lambda j, i: (j, 0, 0))
    o_spec = pl.BlockSpec((tm, c), lambda j, i: (i, j))
    est = 2 * (tm * d * 2 + 2 * d * c * 2 + tm * c * 10) + 4 * tm * c * 4
    return pl.pallas_call(
        body, name=name, grid=(N_CHIPS, s // tm), in_specs=[pl.BlockSpec((tm, d), lambda j, i: (i, 0)), w_spec, w_spec],
        out_specs=[o_spec, o_spec, o_spec],
        out_shape=[_hbm((s, N_CHIPS * c), BF16)] * 3,
        compiler_params=_params(est),
    )(*_pin(h, wg, wu))


def _ffn_down_dx(dx, w_down, gate, up, *, name):
    s, d = dx.shape
    f = w_down.shape[0]
    tm, tn = _pick(s, 512, SUBLANE), _pick(f, 512)

    def body(dx_ref, w_ref, g_ref, u_ref, dg_ref, du_ref):
        dact = lax.dot_general(dx_ref[...], w_ref[...], _DIMS["nt"], preferred_element_type=F32)
        gv, uv = g_ref[...].astype(F32), u_ref[...].astype(F32)
        sig = 1.0 / (1.0 + jnp.exp(-gv))
        dg_ref[...] = (dact * uv * (sig * (1.0 + gv * (1.0 - sig)))).astype(BF16)
        du_ref[...] = (dact * (gv * sig)).astype(BF16)

    t_spec = pl.BlockSpec((tm, tn), lambda i, j: (i, j))
    est = 2 * (tm * d * 2 + tn * d * 2 + tm * tn * 12) + 6 * tm * tn * 4
    return pl.pallas_call(
        body, name=name, grid=(s // tm, f // tn),
        in_specs=[pl.BlockSpec((tm, d), lambda i, j: (i, 0)), pl.BlockSpec((tn, d), lambda i, j: (j, 0)), t_spec, t_spec],
        out_specs=[t_spec, t_spec], out_shape=[_hbm((s, f), BF16)] * 2, compiler_params=_params(est),
    )(*_pin(dx, w_down, gate, up))


def _ffn_up_dx(dgate, dup, wg, wu, *, name):
    s, f = dgate.shape
    _, d, c = wg.shape
    tm, tn = _pick(s, 1024, SUBLANE), _pick(d, 1024)
    nk = 2 * N_CHIPS

    def body(dg_ref, du_ref, wg_ref, wu_ref, o_ref, acc):
        kk = pl.program_id(2)

        @pl.when(kk == 0)
        def _():
            acc[...] = jnp.zeros_like(acc)

        @pl.when(kk < N_CHIPS)
        def _():
            acc[...] += lax.dot_general(dg_ref[...], wg_ref[...], _DIMS["nt"], preferred_element_type=F32)

        @pl.when(kk >= N_CHIPS)
        def _():
            acc[...] += lax.dot_general(du_ref[...], wu_ref[...], _DIMS["nt"], preferred_element_type=F32)

        @pl.when(kk == nk - 1)
        def _():
            o_ref[...] = acc[...]

    last = N_CHIPS - 1
    est = 2 * (2 * tm * c * 2 + 2 * tn * c * 2 + tm * tn * 4) + 2 * tm * tn * 4
    return pl.pallas_call(
        body, name=name, grid=(s // tm, d // tn, nk),
        in_specs=[pl.BlockSpec((tm, c), lambda i, j, kk: (i, jnp.minimum(kk, last))),
                  pl.BlockSpec((tm, c), lambda i, j, kk: (i, jnp.maximum(kk - N_CHIPS, 0))),
                  pl.BlockSpec((None, tn, c), lambda i, j, kk: (jnp.minimum(kk, last), j, 0)),
                  pl.BlockSpec((None, tn, c), lambda i, j, kk: (jnp.maximum(kk - N_CHIPS, 0), j, 0))],
        out_specs=pl.BlockSpec((tm, tn), lambda i, j, kk: (i, j)),
        out_shape=_hbm((s, d), F32), scratch_shapes=[pltpu.VMEM((tm, tn), F32)],
        compiler_params=_params(est),
    )(*_pin(dgate, dup, wg, wu))


def _norm_fwd(x, gain, *, wb, cb, nb, shared_gain, out_dtype, name):
    s = x.shape[0]
    ts = _rows_for(wb * 4, s)

    def body(x_ref, g_ref, o_ref):
        xv = x_ref[...].astype(F32)
        r = lax.rsqrt(jnp.mean(xv * xv, axis=1, keepdims=True) + EPS)
        o_ref[...] = ((xv * r) * g_ref[...]).astype(out_dtype)

    return pl.pallas_call(
        body, name=name, grid=(nb, s // ts),
        in_specs=[pl.BlockSpec((ts, wb), lambda n, i: (i, cb + n)),
                  pl.BlockSpec((1, wb), (lambda n, i: (0, 0)) if shared_gain else (lambda n, i: (0, n)))],
        out_specs=pl.BlockSpec((ts, wb), lambda n, i: (i, n)),
        out_shape=_hbm((s, nb * wb), out_dtype), compiler_params=_params(6 * ts * wb * 4),
    )(*_pin(x), gain)


def _norm_bwd(x, gain, dy, *, wb, cb, nb, shared_gain, out_dtype, name, dy_cb=0, add=None, bf16_copy=False):
    s = x.shape[0]
    ts = _rows_for(wb * 4, s, target=1024 * 1024)
    gw = wb if shared_gain else nb * wb

    def body(*refs):
        refs = list(refs)
        dxb_ref = refs.pop() if bf16_copy else None
        if add is None:
            x_ref, g_ref, dy_ref, dx_ref, dg_ref = refs
        else:
            x_ref, g_ref, dy_ref, add_ref, dx_ref, dg_ref = refs
        n, i = pl.program_id(0), pl.program_id(1)
        xv = x_ref[...].astype(F32)
        dyv = dy_ref[...].astype(F32)
        r = lax.rsqrt(jnp.mean(xv * xv, axis=1, keepdims=True) + EPS)
        xh = xv * r
        dyg = dyv * g_ref[...]
        dx = r * (dyg - xh * jnp.mean(dyg * xh, axis=1, keepdims=True))
        if add is not None:
            dx = dx + add_ref[...]
        dx_ref[...] = dx.astype(out_dtype)
        if bf16_copy:
            dxb_ref[...] = dx.astype(BF16)
        first = jnp.logical_and(n == 0, i == 0) if shared_gain else (i == 0)

        @pl.when(first)
        def _():
            dg_ref[...] = jnp.zeros_like(dg_ref)

        dg_ref[...] += jnp.sum(dyv * xh, axis=0, keepdims=True)

    ins = [x, gain, dy]
    specs = [pl.BlockSpec((ts, wb), lambda n, i: (i, cb + n)),
             pl.BlockSpec((1, wb), (lambda n, i: (0, 0)) if shared_gain else (lambda n, i: (0, n))),
             pl.BlockSpec((ts, wb), lambda n, i: (i, dy_cb + n))]
    if add is not None:
        ins.append(add)
        specs.append(pl.BlockSpec((ts, wb), lambda n, i: (i, n)))
    out_specs = [pl.BlockSpec((ts, wb), lambda n, i: (i, n)),
                 pl.BlockSpec((1, wb), (lambda n, i: (0, 0)) if shared_gain else (lambda n, i: (0, n)))]
    out_shape = [_hbm((s, nb * wb), out_dtype), jax.ShapeDtypeStruct((1, gw), F32)]
    if bf16_copy:
        out_specs.append(out_specs[0])
        out_shape.append(_hbm((s, nb * wb), BF16))
    return pl.pallas_call(
        body, name=name, grid=(nb, s // ts), in_specs=specs, out_specs=out_specs, out_shape=out_shape,
        compiler_params=_params(14 * ts * wb * 4),
    )(*_pin(*ins))


def _swap_halves(x, half):
    if 2 * half == LANE:
        return pltpu.roll(x, half, axis=1)
    lane = lax.broadcasted_iota(jnp.int32, x.shape, 1)
    first = jnp.bitwise_and(lane, 2 * half - 1) < half
    return jnp.where(first, pltpu.roll(x, LANE - half, axis=1), pltpu.roll(x, half, axis=1))


def _rope(x, cos_t, sin_t, *, tw, cb, nb, half, sign, out_dtype, name):
    s = x.shape[0]
    ts = _rows_for(tw * 4, s)

    def body(x_ref, c_ref, s_ref, o_ref):
        for q in range(tw // LANE):
            sl = slice(q * LANE, (q + 1) * LANE)
            xv = x_ref[:, sl].astype(F32)
            sv = s_ref[:, sl]
            if sign < 0:
                sv = -sv
            o_ref[:, sl] = (xv * c_ref[:, sl] + _swap_halves(xv, half) * sv).astype(out_dtype)

    return pl.pallas_call(
        body, name=name, grid=(nb, s // ts),
        in_specs=[pl.BlockSpec((ts, tw), lambda n, i: (i, cb + n)),
                  pl.BlockSpec((ts, tw), lambda n, i: (i, 0)),
                  pl.BlockSpec((ts, tw), lambda n, i: (i, 0))],
        out_specs=pl.BlockSpec((ts, tw), lambda n, i: (i, n)),
        out_shape=_hbm((s, nb * tw), out_dtype), compiler_params=_params(10 * ts * tw * 4),
    )(*_pin(x), cos_t, sin_t)


def _latent_keys(kv, p, cos_t, sin_t, *, kr_cb, name):
    s = kv.shape[0]
    ts = _rows_for(A_QK * 4, s)

    def body(kv_ref, kr_ref, c_ref, s_ref, o_ref):
        o_ref[:, :LANE] = kv_ref[...].astype(BF16)
        x = kr_ref[...].astype(F32)
        o_ref[:, LANE:] = (x * c_ref[:, LANE:] + _swap_halves(x, A_ROPE // 2) * s_ref[:, LANE:]).astype(BF16)

    tab = pl.BlockSpec((ts, A_QK), lambda n, i: (i, 0))
    return pl.pallas_call(
        body, name=name, grid=(A_HEADS, s // ts),
        in_specs=[pl.BlockSpec((ts, LANE), lambda n, i: (i, 2 * n)), pl.BlockSpec((ts, LANE), lambda n, i: (i, kr_cb)),
                  tab, tab],
        out_specs=pl.BlockSpec((ts, A_QK), lambda n, i: (i, n)),
        out_shape=_hbm((s, A_HEADS * A_QK), BF16), compiler_params=_params(10 * ts * A_QK * 4),
    )(*_pin(kv, p), cos_t, sin_t)


def _latent_keys_bwd(dka, dva, cos_t, sin_t, *, name):
    s = dka.shape[0]
    ts = _rows_for(A_HEADS * A_QK * 4, s)

    def body(dka_ref, dva_ref, c_ref, s_ref, dkv_ref, dkr_ref):
        acc = jnp.zeros((ts, LANE), F32)
        for h in range(A_HEADS):
            dkv_ref[:, h * A_QK:h * A_QK + LANE] = dka_ref[:, h * A_QK:h * A_QK + LANE].astype(BF16)
            dkv_ref[:, h * A_QK + LANE:(h + 1) * A_QK] = dva_ref[:, h * LANE:(h + 1) * LANE].astype(BF16)
            y = dka_ref[:, h * A_QK + LANE:(h + 1) * A_QK]
            acc = acc + (y * c_ref[:, LANE:] - _swap_halves(y, A_ROPE // 2) * s_ref[:, LANE:])
        dkr_ref[...] = acc.astype(BF16)

    def rows(width):
        return pl.BlockSpec((ts, width), lambda i: (i, 0))

    return pl.pallas_call(
        body, name=name, grid=(s // ts,),
        in_specs=[rows(A_HEADS * A_QK), rows(A_HEADS * LANE), rows(A_QK), rows(A_QK)],
        out_specs=[rows(A_HEADS * A_QK), rows(LANE)],
        out_shape=[_hbm((s, A_HEADS * A_QK), BF16), _hbm((s, LANE), BF16)],
        compiler_params=_params(8 * ts * A_HEADS * A_QK * 4),
    )(*_pin(dka, dva), cos_t, sin_t)


def _copy_rows(src, n_out, moves, zero, *, name):
    c = src.shape[1]
    n_sem = len(moves) + (zero is not None)

    def body(src_ref, out_ref, *scratch):
        sems = scratch[-1]
        copies = [pltpu.make_async_copy(src_ref.at[pl.ds(a, n)], out_ref.at[pl.ds(b, n)], sems.at[k])
                  for k, (a, b, n) in enumerate(moves)]
        if zero is not None:
            scratch[0][...] = jnp.zeros_like(scratch[0])
            copies.append(pltpu.make_async_copy(scratch[0], out_ref.at[pl.ds(zero[0], zero[1])], sems.at[len(moves)]))
        for cp in copies:
            cp.start()
        for cp in copies:
            cp.wait()

    scratch = ([pltpu.VMEM((zero[1], c), src.dtype)] if zero is not None else []) + [pltpu.SemaphoreType.DMA((n_sem,))]
    return pl.pallas_call(
        body, name=name, in_specs=[_ANY], out_specs=_ANY, out_shape=_hbm((n_out, c), src.dtype), scratch_shapes=scratch,
    )(src)


def _cast_cols(x, *, cb, nb, name):
    s = x.shape[0]
    ts = _rows_for(LANE * 4, s)

    def body(x_ref, o_ref):
        o_ref[...] = x_ref[...].astype(BF16)

    return pl.pallas_call(
        body, name=name, grid=(nb, s // ts), in_specs=[pl.BlockSpec((ts, LANE), lambda n, i: (i, cb + n))],
        out_specs=pl.BlockSpec((ts, LANE), lambda n, i: (i, n)),
        out_shape=_hbm((s, nb * LANE), BF16), compiler_params=_params(4 * ts * LANE * 4),
    )(*_pin(x))


LOG2E = 1.4426950408889634
ATTN_ROW_CHUNK = 256


def _attn_window(i, tq, s, band):
    w, r = band
    start = jnp.clip(i * tq - r, 0, s - w)
    return pl.multiple_of(start, tq), pl.multiple_of((w - tq) - (i * tq - start), LANE)


def _flash_fwd(q, k, v, table, *, hkv, g, dqk, q_cb, k_cb, v_cb, v_step, scale, tq, band, name):
    s = q.shape[0]
    n = s // tq
    hq = hkv * g
    rc = min(tq, ATTN_ROW_CHUNK)
    w = s if band is None else band[0]

    def body(*refs):
        if band is None:
            q_ref, k_ref, v_ref, o_ref, lse_ref = refs
            kw, vw = k_ref[...], v_ref[...]
        else:
            q_ref, k_ref, v_ref, t_ref, o_ref, lse_ref = refs
            start, u = _attn_window(pl.program_id(1), tq, s, band)
            kw, vw = k_ref[pl.ds(start, w), :], v_ref[pl.ds(start, w), :]
        for c in range(tq // rc):
            rows = slice(c * rc, (c + 1) * rc)
            sc = lax.dot_general(q_ref[rows, :], kw, _DIMS["nt"], preferred_element_type=F32) * (scale * LOG2E)
            if band is not None:
                sc = sc + t_ref[rows, pl.ds(u, w)]
            m = jnp.max(sc, axis=1, keepdims=True)
            p = jnp.exp2(sc - m)
            l = jnp.sum(p, axis=1, keepdims=True)
            o_ref[rows, :] = jnp.dot(p.astype(BF16), vw, preferred_element_type=F32) / l
            lse_ref[0, rows, :] = jnp.broadcast_to(m + jnp.log2(l), (rc, LANE))

    ins = [q, k, v]
    specs = [pl.BlockSpec((tq, dqk), lambda h, i: (i, q_cb + h)),
             pl.BlockSpec((s, dqk), lambda h, i: (0, k_cb + h // g)),
             pl.BlockSpec((s, LANE), lambda h, i: (0, v_cb + v_step * (h // g)))]
    if band is not None:
        ins.append(table)
        specs.append(pl.BlockSpec(table.shape, lambda h, i: (0, 0)))
    est = 4 * s * (dqk + LANE) + 6 * rc * w * 4 + 8 * tq * LANE * 4 + (0 if band is None else 2 * table.size * 4)
    return pl.pallas_call(
        body, name=name, grid=(hq, n), in_specs=specs,
        out_specs=[pl.BlockSpec((tq, LANE), lambda h, i: (i, h)), pl.BlockSpec((1, tq, LANE), lambda h, i: (h, i, 0))],
        out_shape=[_hbm((s, hq * LANE), F32), _hbm((hq, s, LANE), F32)],
        compiler_params=_params(est),
    )(*_pin(*ins))


def _flash_bwd(q, k, v, o, do, lse, table, *, hkv, g, dqk, q_cb, k_cb, v_cb, v_step, scale, tq, band, name):
    s = q.shape[0]
    n = s // tq
    hq = hkv * g
    rc = min(tq, ATTN_ROW_CHUNK)
    w = s if band is None else band[0]

    def body(*refs):
        if band is None:
            q_ref, k_ref, v_ref, o_ref, do_ref, lse_ref, dq_ref, dk_ref, dv_ref = refs
            keys = slice(None)
        else:
            q_ref, k_ref, v_ref, o_ref, do_ref, lse_ref, t_ref, dq_ref, dk_ref, dv_ref = refs
            start, u = _attn_window(pl.program_id(1), tq, s, band)
            keys = pl.ds(start, w)
        h, i = pl.program_id(0), pl.program_id(1)

        @pl.when(jnp.logical_and(h % g == 0, i == 0))
        def _():
            dk_ref[...] = jnp.zeros_like(dk_ref)
            dv_ref[...] = jnp.zeros_like(dv_ref)

        kw, vw = k_ref[keys, :], v_ref[keys, :]
        for c in range(tq // rc):
            rows = slice(c * rc, (c + 1) * rc)
            qv = q_ref[rows, :]
            dof = do_ref[rows, :]
            dov = dof.astype(BF16)
            sc = lax.dot_general(qv, kw, _DIMS["nt"], preferred_element_type=F32) * (scale * LOG2E)
            if band is not None:
                sc = sc + t_ref[rows, pl.ds(u, w)]
            p = jnp.exp2(sc - lse_ref[0, rows, 0:1])
            dp = lax.dot_general(dov, vw, _DIMS["nt"], preferred_element_type=F32)
            delta = jnp.sum(dof * o_ref[rows, :], axis=1, keepdims=True)
            ds = (p * (dp - delta) * scale).astype(BF16)
            dv_ref[keys, :] += lax.dot_general(p.astype(BF16), dov, _DIMS["tn"], preferred_element_type=F32)
            dk_ref[keys, :] += lax.dot_general(ds, qv, _DIMS["tn"], preferred_element_type=F32)
            dq_ref[rows, :] = jnp.dot(ds, kw, preferred_element_type=F32)

    ins = [q, k, v, o, do, lse]
    specs = [pl.BlockSpec((tq, dqk), lambda h, i: (i, q_cb + h)),
             pl.BlockSpec((s, dqk), lambda h, i: (0, k_cb + h // g)),
             pl.BlockSpec((s, LANE), lambda h, i: (0, v_cb + v_step * (h // g))),
             pl.BlockSpec((tq, LANE), lambda h, i: (i, h)),
             pl.BlockSpec((tq, LANE), lambda h, i: (i, h)),
             pl.BlockSpec((1, tq, LANE), lambda h, i: (h, i, 0))]
    if band is not None:
        ins.append(table)
        specs.append(pl.BlockSpec(table.shape, lambda h, i: (0, 0)))
    est = (4 + 8) * s * (dqk + LANE) + 10 * rc * w * 4 + 12 * tq * LANE * 4 + (0 if band is None else 2 * table.size * 4)
    return pl.pallas_call(
        body, name=name, grid=(hq, n), in_specs=specs,
        out_specs=[pl.BlockSpec((tq, dqk), lambda h, i: (i, h)),
                   pl.BlockSpec((s, dqk), lambda h, i: (0, h // g)),
                   pl.BlockSpec((s, LANE), lambda h, i: (0, h // g))],
        out_shape=[_hbm((s, hq * dqk), F32), _hbm((s, hkv * dqk), F32),
                   _hbm((s, hkv * LANE), F32)],
        compiler_params=_params(est),
    )(*_pin(*ins))


def _final_loss(x, gain, target, *, name):
    s, d = x.shape
    ts = _rows_for(d * 4, s, target=1024 * 1024)

    def body(x_ref, g_ref, t_ref, dx_ref, dg_ref, loss_ref, dxb_ref):
        i = pl.program_id(0)
        xv = x_ref[...]
        gv = g_ref[...]
        r = lax.rsqrt(jnp.mean(xv * xv, axis=1, keepdims=True) + EPS)
        xh = xv * r
        err = xh * gv - t_ref[...]
        dy = err / d
        dyg = dy * gv
        dx = r * (dyg - xh * jnp.mean(dyg * xh, axis=1, keepdims=True))
        dx_ref[...] = dx
        dxb_ref[...] = dx.astype(BF16)

        @pl.when(i == 0)
        def _():
            dg_ref[...] = jnp.zeros_like(dg_ref)
            loss_ref[...] = jnp.zeros_like(loss_ref)

        dg_ref[...] += jnp.sum(dy * xh, axis=0, keepdims=True)
        part = jnp.sum(jnp.mean(err * err, axis=1, keepdims=True), axis=0, keepdims=True)
        loss_ref[...] += jnp.broadcast_to(0.5 * part, (1, LANE))

    row = pl.BlockSpec((ts, d), lambda i: (i, 0))
    return pl.pallas_call(
        body, name=name, grid=(s // ts,),
        in_specs=[row, pl.BlockSpec((1, d), lambda i: (0, 0)), row],
        out_specs=[row, pl.BlockSpec((1, d), lambda i: (0, 0)), pl.BlockSpec((1, LANE), lambda i: (0, 0)), row],
        out_shape=[_hbm((s, d), F32), jax.ShapeDtypeStruct((1, d), F32),
                   jax.ShapeDtypeStruct((1, LANE), F32), _hbm((s, d), BF16)],
        compiler_params=_params(14 * ts * d * 4),
    )(*_pin(x), gain, *_pin(target))


def _cast_to_slot(x3d, me, *, layer, name):
    _, rows, c = x3d.shape
    tr, tc = _tile2(rows, c, 2 * 1024 * 1024)

    def body(me_ref, x_ref, o_ref):
        o_ref[...] = x_ref[...].astype(BF16)

    return pl.pallas_call(
        body, name=name,
        grid_spec=pltpu.PrefetchScalarGridSpec(
            num_scalar_prefetch=1, grid=(rows // tr, c // tc),
            in_specs=[pl.BlockSpec((None, tr, tc), lambda i, j, me_ref: (layer, i, j))],
            out_specs=pl.BlockSpec((None, tr, tc), lambda i, j, me_ref: (me_ref[0], i, j))),
        out_shape=_hbm((N_CHIPS, rows, c), BF16), compiler_params=_params(6 * tr * tc * 4),
    )(me, *_pin(x3d))


def _sum_parts(srcs, lands, me, *, name):
    depth = len(srcs)
    _, r, c = srcs[0].shape
    tr, tc = _tile2(r, c, 1024 * 1024)
    nt, nc = r // tr, c // tc

    def body(me_ref, *refs):
        o_ref = refs[-1]
        l = pl.program_id(0)
        for k in range(depth):
            @pl.when(l == k)
            def _(k=k):
                acc = refs[k][...].astype(F32)
                for p in range(3):
                    acc = acc + refs[depth + k][p].astype(F32)
                o_ref[...] = acc

    def tile_of(k):
        def f(l, i, j):
            return (jnp.where(l == k, i, jnp.where(l < k, 0, nt - 1)), jnp.where(l == k, j, jnp.where(l < k, 0, nc - 1)))
        return f

    in_specs = [pl.BlockSpec((None, tr, tc), functools.partial(lambda l, i, j, me_ref, f: (me_ref[0], *f(l, i, j)), f=tile_of(k)))
                for k in range(depth)]
    in_specs += [pl.BlockSpec((3, tr, tc), functools.partial(lambda l, i, j, me_ref, f: (0, *f(l, i, j)), f=tile_of(k)))
                 for k in range(depth)]
    return pl.pallas_call(
        body, name=name,
        grid_spec=pltpu.PrefetchScalarGridSpec(
            num_scalar_prefetch=1, grid=(depth, nt, nc), in_specs=in_specs,
            out_specs=pl.BlockSpec((tr, tc), lambda l, i, j, me_ref: (l * nt + i, j))),
        out_shape=_hbm((depth * r, c), F32), compiler_params=_params(depth * 10 * tr * tc * 4),
    )(me, *_pin(*srcs, *lands))


def _adamw_math(w, g, m, v):
    m2 = ADAM_B1 * m + (1.0 - ADAM_B1) * g
    v2 = ADAM_B2 * v + (1.0 - ADAM_B2) * (g * g)
    m_hat = m2 / (1.0 - ADAM_B1 ** ADAM_STEP)
    v_hat = v2 / (1.0 - ADAM_B2 ** ADAM_STEP)
    delta = -ADAM_LR * (m_hat / (jnp.sqrt(v_hat) + ADAM_EPS) + ADAM_WD * w)
    return delta, m2, v2


def _adamw(g_a, g_b, w, m, v, *, name):
    depth, r, c = w.shape
    tr, tc = _tile2(r, c, 512 * 1024)
    nt = r // tr

    def body(a_ref, b_ref, w_ref, m_ref, v_ref, g_out, d_out, m_out, v_out):
        gv = a_ref[...] + b_ref[...]
        delta, m2, v2 = _adamw_math(w_ref[...], gv, m_ref[...], v_ref[...])
        g_out[...] = gv
        d_out[...] = delta
        m_out[...] = m2
        v_out[...] = v2

    flat = pl.BlockSpec((tr, tc), lambda l, i, j: (l * nt + i, j))
    spec = pl.BlockSpec((None, tr, tc), lambda l, i, j: (l, i, j))
    return pl.pallas_call(
        body, name=name, grid=(depth, nt, c // tc), in_specs=[flat, flat, spec, spec, spec], out_specs=[spec] * 4,
        out_shape=[_hbm((depth, r, c), F32)] * 4, compiler_params=_params(22 * tr * tc * 4),
    )(*_pin(g_a, g_b, w, m, v))


def _small_adamw(g_all, w, m, v, *, name):
    r, c = w.shape

    def body(ga_ref, w_ref, m_ref, v_ref, g_out, d_out, m_out, v_out):
        gv = ga_ref[0]
        for j in range(1, N_DEV):
            gv = gv + ga_ref[j]
        delta, m2, v2 = _adamw_math(w_ref[...], gv, m_ref[...], v_ref[...])
        g_out[...] = gv
        d_out[...] = delta
        m_out[...] = m2
        v_out[...] = v2

    return pl.pallas_call(body, name=name, out_shape=[jax.ShapeDtypeStruct((r, c), F32)] * 4)(g_all, w, m, v)


_ANY = pl.BlockSpec(memory_space=pl.ANY)


_HBM = pl.BlockSpec(memory_space=pltpu.HBM)
_SEM = pl.BlockSpec(memory_space=pltpu.SEMAPHORE)
_EFFECT = pltpu.SideEffectType.DATAFLOW_SIDE_EFFECTING


def _peer_chips():
    x, y = lax.axis_index("x"), lax.axis_index("y")
    return 2 * x + y, [(1 - x, y), (x, 1 - y), (1 - x, 1 - y)]


def _exchange_copy(srcs, lands, send_sems, recv_sems, k, p, kind):
    c = lax.axis_index("c")
    if kind == "swap":
        return pltpu.make_async_remote_copy(
            src_ref=srcs[k], dst_ref=lands[k], send_sem=send_sems.at[k], recv_sem=recv_sems.at[k],
            device_id=(lax.axis_index("x"), lax.axis_index("y"), 1 - c), device_id_type=MESH)
    me, peers = _peer_chips()
    px, py = peers[p]
    return pltpu.make_async_remote_copy(
        src_ref=srcs[k].at[2 * px + py] if kind == "scatter" else srcs[k].at[me],
        dst_ref=lands[k].at[p] if kind == "scatter" else lands[k].at[me],
        send_sem=send_sems.at[3 * k + p], recv_sem=recv_sems.at[3 * k + p],
        device_id=(px, py, c), device_id_type=MESH)


def _exchange_start(srcs, lands, after, *, kind, name):
    n = len(srcs)
    npeer = 1 if kind == "swap" else 3
    bufs = list(srcs) + (list(lands) if lands is not None else [])
    nb = len(bufs)

    def body(*refs):
        buf_refs, send_sems, recv_sems = refs[:nb], refs[nb + 1], refs[nb + 2]
        token = refs[-1]
        s_refs = buf_refs[:n]
        l_refs = buf_refs[n:] if lands is not None else s_refs
        for k in range(n):
            for p in range(npeer):
                _exchange_copy(s_refs, l_refs, send_sems, recv_sems, k, p, kind).start()
        token[...] = jnp.zeros_like(token)

    out = pl.pallas_call(
        body, name=name,
        out_shape=(pltpu.SemaphoreType.DMA((npeer * n,)), pltpu.SemaphoreType.DMA((npeer * n,)),
                   *[pltpu.HBM(b.shape, b.dtype) for b in bufs], jax.ShapeDtypeStruct((SUBLANE, LANE), F32)),
        in_specs=[_HBM] * nb + [_ANY],
        out_specs=(_SEM, _SEM, *[_HBM] * nb, pl.BlockSpec(memory_space=pltpu.VMEM)),
        input_output_aliases={i: 2 + i for i in range(nb)},
        compiler_params=pltpu.CompilerParams(has_side_effects=_EFFECT),
    )(*[pltpu.with_memory_space_constraint(b, pltpu.HBM) for b in bufs], after)
    send_sems, recv_sems = out[0], out[1]
    thru = out[2:2 + nb]
    return send_sems, recv_sems, list(thru[:n]), (list(thru[n:]) if lands is not None else None), out[-1]


def _exchange_wait(send_sems, recv_sems, srcs, lands, after, *, kind, name):
    n = len(srcs)
    npeer = 1 if kind == "swap" else 3
    bufs = list(srcs) + (list(lands) if lands is not None else [])
    nb = len(bufs)

    def body(*refs):
        buf_refs, send_sems_ref, recv_sems_ref = refs[:nb], refs[nb], refs[nb + 1]
        s_refs = buf_refs[:n]
        l_refs = buf_refs[n:] if lands is not None else s_refs
        for k in range(n):
            for p in range(npeer):
                cp = _exchange_copy(s_refs, l_refs, send_sems_ref, recv_sems_ref, k, p, kind)
                cp.wait_send()
                cp.wait_recv()

    out = pl.pallas_call(
        body, name=name, out_shape=tuple(pltpu.HBM(b.shape, b.dtype) for b in bufs),
        in_specs=[_HBM] * nb + [_SEM, _SEM, _ANY], out_specs=tuple([_HBM] * nb),
        input_output_aliases={i: i for i in range(nb)},
        compiler_params=pltpu.CompilerParams(has_side_effects=_EFFECT),
    )(*bufs, send_sems, recv_sems, after)
    return list(out)


def _sibling_exchange(srcs, *, name):
    n = len(srcs)

    def body(*refs):
        src, out = refs[:n], refs[n:2 * n]
        send_sems, recv_sems = refs[2 * n:]
        sibling = (lax.axis_index("x"), lax.axis_index("y"), 1 - lax.axis_index("c"))
        copies = [pltpu.make_async_remote_copy(src_ref=src[k], dst_ref=out[k], send_sem=send_sems.at[k],
                                               recv_sem=recv_sems.at[k], device_id=sibling, device_id_type=MESH)
                  for k in range(n)]
        for cp in copies:
            cp.start()
        for cp in copies:
            cp.wait_recv()
        for cp in copies:
            cp.wait_send()

    return pl.pallas_call(
        body, name=name, in_specs=[_ANY] * n, out_specs=[_ANY] * n,
        out_shape=[jax.ShapeDtypeStruct(a.shape, a.dtype) for a in srcs],
        scratch_shapes=[pltpu.SemaphoreType.DMA((n,)), pltpu.SemaphoreType.DMA((n,))],
    )(*srcs)


def _all_gather_small(block, *, name):
    m_per, ncol = block.shape

    def body(x_ref, out_ref, send_sems, recv_sems, local_sem):
        x, y, c = lax.axis_index("x"), lax.axis_index("y"), lax.axis_index("c")
        me, sibling = (x, y, c), (x, y, 1 - c)
        chips = [(1 - x, y), (x, 1 - y), (1 - x, 1 - y)]

        def rows(px, py, pc):
            return out_ref.at[pl.ds((4 * px + 2 * py + pc) * m_per, m_per), :]

        def copy(k, blk, to, src=None):
            return pltpu.make_async_remote_copy(
                src_ref=rows(*blk) if src is None else src, dst_ref=rows(*blk),
                send_sem=send_sems.at[k], recv_sem=recv_sems.at[k], device_id=to, device_id_type=MESH)

        mine = pltpu.make_async_copy(x_ref, rows(*me), local_sem)
        mine.start()
        first = [copy(0, me, sibling, src=x_ref)]
        first += [copy(1 + j, me, (*chip, c), src=x_ref) for j, chip in enumerate(chips)]
        for cp in first:
            cp.start()
        passed = [copy(4 + j, (*chip, c), sibling) for j, chip in enumerate(chips)]
        for j, chip in enumerate(chips):
            copy(1 + j, (*chip, c), me).wait_recv()
            passed[j].start()
        copy(0, sibling, me).wait_recv()
        for j, chip in enumerate(chips):
            copy(4 + j, (*chip, 1 - c), me).wait_recv()
        for cp in first + passed:
            cp.wait_send()
        mine.wait()

    return pl.pallas_call(
        body, name=name, out_shape=jax.ShapeDtypeStruct((N_DEV * m_per, ncol), block.dtype),
        in_specs=[pl.BlockSpec(memory_space=pltpu.VMEM)], out_specs=pl.BlockSpec(memory_space=pltpu.VMEM),
        scratch_shapes=[pltpu.SemaphoreType.DMA((7,)), pltpu.SemaphoreType.DMA((7,)), pltpu.SemaphoreType.DMA],
    )(block)


def _rope_angles(pos, dim):
    inv = ROPE_THETA ** (-jnp.arange(0, dim, 2, dtype=F32) / dim)
    return pos.astype(F32)[:, None] * inv[None, :]


def _rope_tables(s):
    pos = jnp.arange(s, dtype=jnp.int32)
    rows = s // GRID_W
    row = jnp.repeat(jnp.arange(rows, dtype=jnp.int32), GRID_W)
    col = jnp.tile(jnp.arange(GRID_W, dtype=jnp.int32), rows)
    a1 = _rope_angles(pos, HEAD_DIM)
    aa = _rope_angles(pos, A_ROPE)
    ar = _rope_angles(row, HEAD_DIM // 2)
    ac = _rope_angles(col, HEAD_DIM // 2)
    one = jnp.ones((s, LANE), F32)
    zero = jnp.zeros((s, LANE), F32)
    pad = LANE - A_ROPE
    cos_a = jnp.concatenate([one, jnp.cos(aa), jnp.cos(aa), jnp.ones((s, pad), F32)], axis=1)
    sin_a = jnp.concatenate([zero, -jnp.sin(aa), jnp.sin(aa), jnp.zeros((s, pad), F32)], axis=1)
    cos_b = jnp.concatenate([jnp.cos(a1), jnp.cos(a1)], axis=1)
    sin_b = jnp.concatenate([-jnp.sin(a1), jnp.sin(a1)], axis=1)
    cos_c = jnp.concatenate([jnp.cos(ar), jnp.cos(ar), jnp.cos(ac), jnp.cos(ac)], axis=1)
    sin_c = jnp.concatenate([-jnp.sin(ar), jnp.sin(ar), -jnp.sin(ac), jnp.sin(ac)], axis=1)
    return (cos_a, sin_a), (cos_b, sin_b), (cos_c, sin_c)


def _band_table(tq, s):
    reach = max((win // (2 * d)) * d for win, d in B_PATTERNS)
    r = -(-reach // tq) * tq
    w = min(s, tq + 2 * r)
    j = jnp.arange(tq, dtype=jnp.int32)[:, None]
    x = jnp.arange(2 * w - tq, dtype=jnp.int32)[None, :]
    rel = x - (w - tq) - j
    mult = jnp.zeros(rel.shape, F32)
    for win, d in B_PATTERNS:
        mult = mult + jnp.logical_and(rel % d == 0, jnp.abs(rel) <= (win // (2 * d)) * d).astype(F32)
    return jnp.where(mult > 0, jnp.log2(jnp.maximum(mult, 1.0)), NEG), (w, r)


_BIG = ("w_in", "a_w_uq", "a_w_ukv", "w_out", "w_gate", "w_up", "w_down")
_SMALL = ("attn_norm", "a_q_norm", "a_kv_norm", "c_q_norm", "c_k_norm", "out_norm", "ffn_norm", "final_norm")
_WEIGHTS = ("attn_norm", "w_in", "a_q_norm", "a_w_uq", "a_kv_norm", "a_w_ukv", "c_q_norm", "c_k_norm", "out_norm",
            "w_out", "ffn_norm", "w_gate", "w_up", "w_down", "final_norm")


_ATTN = ("w_in", "a_w_uq", "a_w_ukv")
_FFN = ("w_out", "w_gate", "w_up", "w_down")


def _from_cols(a):
    return jnp.transpose(a, (1, 0, 2)).reshape(a.shape[1], N_CHIPS * a.shape[2])


def _from_rows(a):
    return a.reshape(N_CHIPS * a.shape[1], a.shape[2])


def _to_cols(a):
    return jnp.transpose(a.reshape(a.shape[0], N_CHIPS, a.shape[1] // N_CHIPS), (1, 0, 2))


def _to_rows(a):
    return a.reshape(N_CHIPS, a.shape[0] // N_CHIPS, a.shape[1])


def _assemble_attn(gw):
    w_in_t, uq, ukv = _from_rows(gw[0]), _from_cols(gw[1]), _from_cols(gw[2])
    n_in = w_in_t.shape[0]
    w_all = _copy_rows(w_in_t, NP, [(0, 0, IN_A), (IN_A, A_PAD, n_in - IN_A)], (IN_A, A_PAD - IN_A), name="pad_w_in")
    uq = uq.reshape(A_Q_RANK, A_HEADS, A_NOPE + A_ROPE)
    uq = jnp.pad(uq, ((0, 0), (0, 0), (0, A_QK - A_NOPE - A_ROPE))).reshape(A_Q_RANK, A_HEADS * A_QK)
    return dict(w_all=w_all, uq=uq, ukv=ukv)


def _assemble_ffn(gw):
    return dict(w_out=_from_rows(gw[0]), w_gate=gw[1], w_up=gw[2], w_down=_from_rows(gw[3]))


def _split_attn_grads(gl):
    w_all = gl["w_all"]
    n_in = NP - (A_PAD - IN_A)
    w_in_t = _copy_rows(w_all, n_in, [(0, 0, IN_A), (A_PAD, IN_A, n_in - IN_A)], None, name="unpad_dw_in")
    uq = gl["uq"].reshape(A_Q_RANK, A_HEADS, A_QK)[:, :, :A_NOPE + A_ROPE].reshape(A_Q_RANK, A_HEADS * (A_NOPE + A_ROPE))
    return [_to_rows(w_in_t), _to_cols(uq), _to_cols(gl["ukv"])]


def _split_ffn_grads(gl):
    return [_to_rows(gl["w_out"]), gl["w_gate"], gl["w_up"], _to_rows(gl["w_down"])]


def _tie(a, token):
    return a + token[0:1, 0:1]


def _layer_fwd(x, wl, ffn_weights, sm, tabs, bias, t):
    s = x.shape[0]
    (cos_a, sin_a), (cos_b, sin_b), (cos_c, sin_c) = tabs
    h = _norm_fwd(x, sm["attn_norm"], wb=x.shape[1], cb=0, nb=1, shared_gain=True, out_dtype=BF16, name="attn_norm_fwd")
    p = _matmul(h, wl["w_all"], mode="nt", out_dtype=F32, name="in_proj", tm=1024, tn=640)
    cq_n = _norm_fwd(p, sm["a_q_norm"], wb=A_Q_RANK, cb=0, nb=1, shared_gain=True, out_dtype=BF16, name="a_q_norm_fwd")
    ckv_n = _norm_fwd(p, sm["a_kv_norm"], wb=A_KV_RANK, cb=1, nb=1, shared_gain=True, out_dtype=BF16, name="a_kv_norm_fwd")
    qa_raw = _matmul(cq_n, wl["uq"], mode="nn", out_dtype=F32, name="a_uq", tm=1024, tn=1024)
    kv = _matmul(ckv_n, wl["ukv"], mode="nn", out_dtype=BF16, name="a_ukv", tm=1024, tn=1024)
    qa =_rope(qa_raw, cos_a, sin_a, tw=A_QK, cb=0, nb=A_HEADS, half=A_ROPE // 2, sign=1, out_dtype=BF16, name="a_rope_q")
    ka = _latent_keys(kv, p, cos_a, sin_a, kr_cb=PB_KR, name="a_keys")
    oa, lse_a = _flash_fwd(qa, ka, kv, None, hkv=A_HEADS, g=1, dqk=A_QK, q_cb=0, k_cb=0, v_cb=1, v_step=2,
                           scale=(A_NOPE + A_ROPE) ** -0.5, tq=_pick(s, 2 * t), band=None, name="a_flash_fwd")
    table, band = bias
    qb = _rope(p, cos_b, sin_b, tw=LANE, cb=PB_BQ, nb=B_HEADS, half=HEAD_DIM // 2, sign=1, out_dtype=BF16, name="b_rope_q")
    kb = _rope(p, cos_b, sin_b, tw=LANE, cb=PB_BK, nb=B_HEADS, half=HEAD_DIM // 2, sign=1, out_dtype=BF16, name="b_rope_k")
    vb = _cast_cols(p, cb=PB_BV, nb=B_HEADS, name="b_cast_v")
    ob, lse_b = _flash_fwd(qb, kb, vb, table, hkv=B_HEADS, g=1, dqk=LANE, q_cb=0, k_cb=0, v_cb=0, v_step=1,
                           scale=HEAD_DIM ** -0.5, tq=t, band=band, name="b_flash_fwd")
    qn = _norm_fwd(p, sm["c_q_norm"], wb=LANE, cb=PB_CQH, nb=C_HEADS, shared_gain=True, out_dtype=F32, name="c_q_norm_fwd")
    kn = _norm_fwd(p, sm["c_k_norm"], wb=LANE, cb=PB_CKH, nb=C_KV_HEADS, shared_gain=True, out_dtype=F32, name="c_k_norm_fwd")
    qc = _rope(qn, cos_c, sin_c, tw=LANE, cb=0, nb=C_HEADS, half=HEAD_DIM // 4, sign=1, out_dtype=BF16, name="c_rope_q")
    kc = _rope(kn, cos_c, sin_c, tw=LANE, cb=0, nb=C_KV_HEADS, half=HEAD_DIM // 4, sign=1, out_dtype=BF16, name="c_rope_k")
    vc = _cast_cols(p, cb=PB_CVH, nb=C_KV_HEADS, name="c_cast_v")
    oc, lse_c = _flash_fwd(qc, kc, vc, None, hkv=C_KV_HEADS, g=C_GROUP, dqk=LANE, q_cb=0, k_cb=0, v_cb=0, v_step=1,
                           scale=HEAD_DIM ** -0.5, tq=_pick(s, 2 * t), band=None, name="c_flash_fwd")
    g_out = sm["out_norm"]
    ga, gb, gc = g_out[:, :A_WIDTH], g_out[:, A_WIDTH:A_WIDTH + B_WIDTH], g_out[:, A_WIDTH + B_WIDTH:]
    ya = _norm_fwd(oa, ga, wb=A_WIDTH, cb=0, nb=1, shared_gain=True, out_dtype=BF16, name="out_norm_a_fwd")
    yb = _norm_fwd(ob, gb, wb=B_WIDTH, cb=0, nb=1, shared_gain=True, out_dtype=BF16, name="out_norm_b_fwd")
    yc = _norm_fwd(oc, gc, wb=C_WIDTH, cb=0, nb=1, shared_gain=True, out_dtype=BF16, name="out_norm_c_fwd")
    y = jnp.concatenate([ya, yb, yc], axis=1)
    wl = {**wl, **ffn_weights(y)}
    x1 = _matmul(y, wl["w_out"], mode="nn", out_dtype=F32, name="out_proj", add=x, tm=1024, tn=512)
    h2 = _norm_fwd(x1, sm["ffn_norm"], wb=x.shape[1], cb=0, nb=1, shared_gain=True, out_dtype=BF16, name="ffn_norm_fwd")
    gate, up, act = _ffn_up(h2, wl["w_gate"], wl["w_up"], name="ffn_up")
    x2 =_matmul(act, wl["w_down"], mode="nn", out_dtype=F32, name="ffn_down", add=x1, tm=512, tn=512)
    saved = dict(x=x, h=h, p=p, cq_n=cq_n, ckv_n=ckv_n, kv=kv, qa=qa, ka=ka, oa=oa, lse_a=lse_a, qb=qb, kb=kb, vb=vb, ob=ob,
                 lse_b=lse_b, qc=qc, kc=kc, vc=vc, oc=oc, lse_c=lse_c, y=y, x1=x1, h2=h2, gate=gate, up=up, act=act)
    return x2, saved, wl


def _layer_bwd(dx2, dx2b, sv, wl, sm, tabs, bias, t, send_ffn, send_attn):
    s, d = dx2.shape
    (cos_a, sin_a), (cos_b, sin_b), (cos_c, sin_c) = tabs
    gw, gs = {}, {}
    dgate, dup = _ffn_down_dx(dx2b, wl["w_down"], sv["gate"], sv["up"], name="ffn_down_dx")
    gw["w_down"] = _matmul(sv["act"], dx2b, mode="tn", out_dtype=BF16, name="ffn_down_dw", tm=512, tn=512)
    dh2 = _ffn_up_dx(dgate, dup, wl["w_gate"], wl["w_up"], name="ffn_up_dx")
    gw["w_gate"] = _matmul(sv["h2"], dgate, mode="tn", out_dtype=BF16, name="ffn_gate_dw", tm=512, col_shards=True)
    gw["w_up"] = _matmul(sv["h2"], dup, mode="tn", out_dtype=BF16, name="ffn_up_dw", tm=512, col_shards=True)
    dx1, gs["ffn_norm"], dx1b = _norm_bwd(sv["x1"], sm["ffn_norm"], dh2, wb=d, cb=0, nb=1, shared_gain=True,
                                          out_dtype=F32, name="ffn_norm_bwd", add=dx2, bf16_copy=True)
    dy = _matmul(dx1b, wl["w_out"], mode="nt", out_dtype=F32, name="out_proj_dx", tm=512, tn=512)
    gw["w_out"] = _matmul(sv["y"], dx1b, mode="tn", out_dtype=BF16, name="out_proj_dw", tm=512, tn=512)
    token = send_ffn(gw)
    g_out = _tie(sm["out_norm"], token)
    ga, gb, gc = g_out[:, :A_WIDTH], g_out[:, A_WIDTH:A_WIDTH + B_WIDTH], g_out[:, A_WIDTH + B_WIDTH:]
    dya, dyb, dyc = dy[:, :A_WIDTH], dy[:, A_WIDTH:A_WIDTH + B_WIDTH], dy[:, A_WIDTH + B_WIDTH:]
    doa, dga = _norm_bwd(sv["oa"], ga, dya, wb=A_WIDTH, cb=0, nb=1, shared_gain=True, out_dtype=F32, name="out_norm_a_bwd")
    dob, dgb = _norm_bwd(sv["ob"], gb, dyb, wb=B_WIDTH, cb=0, nb=1, shared_gain=True, out_dtype=F32, name="out_norm_b_bwd")
    doc, dgc = _norm_bwd(sv["oc"], gc, dyc, wb=C_WIDTH, cb=0, nb=1, shared_gain=True, out_dtype=F32, name="out_norm_c_bwd")
    gs["out_norm"] = jnp.concatenate([dga, dgb, dgc], axis=1)
    p = sv["p"]
    dqc, dkc, dvc = _flash_bwd(sv["qc"], sv["kc"], sv["vc"], sv["oc"], doc, sv["lse_c"], None, hkv=C_KV_HEADS,
                               g=C_GROUP, dqk=LANE, q_cb=0, k_cb=0, v_cb=0, v_step=1, scale=HEAD_DIM ** -0.5,
                               tq=_pick(s, 2 * t), band=None, name="c_flash_bwd")
    dqn = _rope(dqc, cos_c, sin_c, tw=LANE, cb=0, nb=C_HEADS, half=HEAD_DIM // 4, sign=-1, out_dtype=F32, name="c_rope_q_bwd")
    dkn = _rope(dkc, cos_c, sin_c, tw=LANE, cb=0, nb=C_KV_HEADS, half=HEAD_DIM // 4, sign=-1, out_dtype=F32, name="c_rope_k_bwd")
    dpcq, gs["c_q_norm"] = _norm_bwd(p, sm["c_q_norm"], dqn, wb=LANE, cb=PB_CQH, nb=C_HEADS, shared_gain=True,
                                     out_dtype=BF16, name="c_q_norm_bwd")
    dpck, gs["c_k_norm"] = _norm_bwd(p, sm["c_k_norm"], dkn, wb=LANE, cb=PB_CKH, nb=C_KV_HEADS, shared_gain=True,
                                     out_dtype=BF16, name="c_k_norm_bwd")
    table, band = bias
    dqb, dkb, dvb = _flash_bwd(sv["qb"], sv["kb"], sv["vb"], sv["ob"], dob, sv["lse_b"], table, hkv=B_HEADS, g=1,
                               dqk=LANE, q_cb=0, k_cb=0, v_cb=0, v_step=1, scale=HEAD_DIM ** -0.5, tq=t, band=band,
                               name="b_flash_bwd")
    dpbq = _rope(dqb, cos_b, sin_b, tw=LANE, cb=0, nb=B_HEADS, half=HEAD_DIM // 2, sign=-1, out_dtype=BF16, name="b_rope_q_bwd")
    dpbk = _rope(dkb, cos_b, sin_b, tw=LANE, cb=0, nb=B_HEADS, half=HEAD_DIM // 2, sign=-1, out_dtype=BF16, name="b_rope_k_bwd")
    dqa, dka, dva = _flash_bwd(sv["qa"], sv["ka"], sv["kv"], sv["oa"], doa, sv["lse_a"], None, hkv=A_HEADS, g=1,
                               dqk=A_QK, q_cb=0, k_cb=0, v_cb=1, v_step=2, scale=(A_NOPE + A_ROPE) ** -0.5,
                               tq=_pick(s, 2 * t), band=None, name="a_flash_bwd")
    dqa_raw = _rope(dqa, cos_a, sin_a, tw=A_QK, cb=0, nb=A_HEADS, half=A_ROPE // 2, sign=-1, out_dtype=BF16, name="a_rope_q_bwd")
    dkv, dkr = _latent_keys_bwd(dka, dva, cos_a, sin_a, name="a_keys_bwd")
    dckv_n = _matmul(dkv, wl["ukv"], mode="nt", out_dtype=F32, name="a_ukv_dx", tm=1024, tn=512)
    gw["ukv"] = _matmul(sv["ckv_n"], dkv, mode="tn", out_dtype=BF16, name="a_ukv_dw", tm=512, tn=1024)
    dcq_n = _matmul(dqa_raw, wl["uq"], mode="nt", out_dtype=F32, name="a_uq_dx", tm=1024, tn=512)
    gw["uq"] = _matmul(sv["cq_n"], dqa_raw, mode="tn", out_dtype=BF16, name="a_uq_dw", tm=512, tn=1024)
    dcq, gs["a_q_norm"] = _norm_bwd(p, sm["a_q_norm"], dcq_n, wb=A_Q_RANK, cb=0, nb=1, shared_gain=True, out_dtype=BF16,
                                    name="a_q_norm_bwd")
    dckv, gs["a_kv_norm"] = _norm_bwd(p, sm["a_kv_norm"], dckv_n, wb=A_KV_RANK, cb=1, nb=1, shared_gain=True,
                                      out_dtype=BF16, name="a_kv_norm_bwd")
    dp = jnp.concatenate([dcq, dckv, dkr, jnp.zeros((s, A_PAD - (PB_KR + 1) * LANE), BF16), dpbq, dpbk,
                          dvb.astype(BF16), dpcq, dpck, dvc.astype(BF16)], axis=1)
    gw["w_all"] = _matmul(dp, sv["h"], mode="tn", out_dtype=BF16, name="in_proj_dw", tm=640, tn=512)
    token = send_attn(gw)
    dh = _matmul(dp, wl["w_all"], mode="nn", out_dtype=F32, name="in_proj_dx", tm=512, tn=512, after=token)
    dx, gs["attn_norm"], dxb = _norm_bwd(sv["x"], sm["attn_norm"], dh, wb=d, cb=0, nb=1, shared_gain=True,
                                         out_dtype=F32, name="attn_norm_bwd", add=dx1, bf16_copy=True)
    return dx, dxb, gs, token


def _pack_small(vals):
    flat = jnp.concatenate([vals[n].reshape(-1).astype(F32) for n in _SMALL])
    tile = SUBLANE * LANE
    padded = -(-flat.shape[0] // tile) * tile
    return jnp.pad(flat, (0, padded - flat.shape[0])).reshape(padded // LANE, LANE)


def _unpack_small(packed, like):
    flat = packed.reshape(-1)
    out, off = {}, 0
    for n in _SMALL:
        size = math.prod(like[n].shape)
        out[n] = flat[off:off + size].reshape(like[n].shape)
        off += size
    return out


def kernel(x, attn_norm, w_in, a_q_norm, a_w_uq, a_kv_norm, a_w_ukv, c_q_norm, c_k_norm, out_norm, w_out, ffn_norm, w_gate, w_up, w_down, final_norm, loss_target, m_attn_norm, m_w_in, m_a_q_norm, m_a_w_uq, m_a_kv_norm, m_a_w_ukv, m_c_q_norm, m_c_k_norm, m_out_norm, m_w_out, m_ffn_norm, m_w_gate, m_w_up, m_w_down, m_final_norm, v_attn_norm, v_w_in, v_a_q_norm, v_a_w_uq, v_a_kv_norm, v_a_w_ukv, v_c_q_norm, v_c_k_norm, v_out_norm, v_w_out, v_ffn_norm, v_w_gate, v_w_up, v_w_down, v_final_norm):
    w = dict(attn_norm=attn_norm, w_in=w_in, a_q_norm=a_q_norm, a_w_uq=a_w_uq, a_kv_norm=a_kv_norm, a_w_ukv=a_w_ukv,
             c_q_norm=c_q_norm, c_k_norm=c_k_norm, out_norm=out_norm, w_out=w_out, ffn_norm=ffn_norm, w_gate=w_gate,
             w_up=w_up, w_down=w_down, final_norm=final_norm)
    m = dict(attn_norm=m_attn_norm, w_in=m_w_in, a_q_norm=m_a_q_norm, a_w_uq=m_a_w_uq, a_kv_norm=m_a_kv_norm,
             a_w_ukv=m_a_w_ukv, c_q_norm=m_c_q_norm, c_k_norm=m_c_k_norm, out_norm=m_out_norm, w_out=m_w_out,
             ffn_norm=m_ffn_norm, w_gate=m_w_gate, w_up=m_w_up, w_down=m_w_down, final_norm=m_final_norm)
    v = dict(attn_norm=v_attn_norm, w_in=v_w_in, a_q_norm=v_a_q_norm, a_w_uq=v_a_w_uq, a_kv_norm=v_a_kv_norm,
             a_w_ukv=v_a_w_ukv, c_q_norm=v_c_q_norm, c_k_norm=v_c_k_norm, out_norm=v_out_norm, w_out=v_w_out,
             ffn_norm=v_ffn_norm, w_gate=v_w_gate, w_up=v_w_up, w_down=v_w_down, final_norm=v_final_norm)
    _, s, d = x.shape
    depth = attn_norm.shape[0]

    def as_stored(a, n):
        return jnp.swapaxes(a, 1, 2) if n == "w_in" else a
    t = _pick(s, 512)

    me = (2 * lax.axis_index("x") + lax.axis_index("y")).astype(jnp.int32).reshape(1)

    gathers, after = {}, me
    for l in range(depth):
        for group, names in (("attn", _ATTN), ("ffn", _FFN)):
            bufs = [_cast_to_slot(as_stored(w[n], n), me, layer=l, name=f"cast_{n}")
                    for n in names]
            send_sems, recv_sems, bufs, _, after = _exchange_start(bufs, None, after, kind="gather",
                                                                   name=f"gather_start_{group}{l}")
            gathers[group, l] = (send_sems, recv_sems, bufs)
    all_started = after

    def gathered(group, l, after):
        send_sems, recv_sems, bufs = gathers[group, l]
        return _exchange_wait(send_sems, recv_sems, bufs, None, after, kind="gather", name=f"gather_wait_{group}{l}")

    tabs = _rope_tables(s)
    bias = _band_table(t, s)

    xs = x.reshape(s, d)
    saved, wls, sms = [], [], []
    for l in range(depth):
        wl = _assemble_attn(gathered("attn", l, all_started if l == 0 else xs))
        sm = {n: w[n][l][None, :] for n in _SMALL if n != "final_norm"}
        xs, sv, wl = _layer_fwd(xs, wl, lambda after, l=l: _assemble_ffn(gathered("ffn", l, after)), sm, tabs, bias, t)
        saved.append(sv)
        wls.append(wl)
        sms.append(sm)
    dx, g_final, loss_row, dxb = _final_loss(xs, final_norm[None, :], loss_target.reshape(s, d), name="final_loss")
    loss = lax.psum(loss_row[0, 0], ("x", "y", "c"))

    sends = {}

    def send(group, l, srcs, after):
        lands = [lax.empty((3,) + a.shape[1:], BF16) for a in srcs]
        send_sems, recv_sems, srcs, lands, token = _exchange_start(srcs, lands, after, kind="scatter",
                                                                   name=f"scatter_start_{group}{l}")
        sends[group, l] = (send_sems, recv_sems, srcs, lands)
        return token

    gs_layers, token = [None] * depth, all_started
    for l in reversed(range(depth)):
        dx, dxb, gs_layers[l], token = _layer_bwd(
            dx, dxb, saved[l], wls[l], sms[l], tabs, bias, t,
            lambda gw, l=l, tk=token: send("ffn", l, _split_ffn_grads(gw), tk),
            lambda gw, l=l: send("attn", l, _split_attn_grads(gw), dx))
    grad_x = dx.reshape(x.shape)

    srcs, lands = {}, {}

    def arrive(key, after):
        send_sems, recv_sems, s_bufs, l_bufs = sends[key]
        got = _exchange_wait(send_sems, recv_sems, s_bufs, l_bufs, after, kind="scatter",
                             name=f"scatter_wait_{key[0]}{key[1]}")
        for k, n in enumerate(_ATTN if key[0] == "attn" else _FFN):
            srcs[n, key[1]], lands[n, key[1]] = got[k], got[len(s_bufs) + k]

    def summed(names):
        return [_sum_parts([srcs[n, l] for l in range(depth)], [lands[n, l] for l in range(depth)], me, name="sum_" + n)
                for n in names]

    last = ("attn", 0)
    for key in sends:
        if key != last:
            arrive(key, token)
    sums_ffn = summed(_FFN)
    swap = _exchange_start(sums_ffn, [lax.empty(a.shape, F32) for a in sums_ffn], token, kind="swap",
                           name="swap_start_ffn")
    arrive(last, swap[4])
    sums_attn = summed(_ATTN)
    sib_attn = list(_sibling_exchange(sums_attn, name="swap_core_sums_attn"))
    swapped = _exchange_wait(swap[0], swap[1], swap[2], swap[3], sib_attn[0], kind="swap", name="swap_wait_ffn")
    mine_of = dict(zip(_FFN + _ATTN, swapped[:len(_FFN)] + sums_attn))
    other_of = dict(zip(_FFN + _ATTN, swapped[len(_FFN):] + sib_attn))
    grads, deltas, new_m, new_v = {}, {}, {}, {}
    for n in _BIG:
        res = _adamw(mine_of[n], other_of[n], as_stored(w[n], n), as_stored(m[n], n), as_stored(v[n], n), name="adamw_" + n)
        grads[n], deltas[n], new_m[n], new_v[n] = [as_stored(r, n) for r in res]

    gsm = {n: jnp.stack([gs_layers[l][n][0] for l in range(depth)]) for n in _SMALL if n != "final_norm"}
    gsm["final_norm"] = g_final[0]
    packed = _pack_small(gsm)
    everyone = _all_gather_small(packed, name="gather_gain_grads").reshape(N_DEV, packed.shape[0], LANE)
    res = _small_adamw(everyone, _pack_small(w), _pack_small(m), _pack_small(v), name="adamw_gains")
    for dst, r in zip((grads, deltas, new_m, new_v), res):
        dst.update(_unpack_small(r, w))

    return (loss, grad_x, *[grads[n] for n in _WEIGHTS], *[deltas[n] for n in _WEIGHTS],
            *[new_m[n] for n in _WEIGHTS], *[new_v[n] for n in _WEIGHTS])
```

```python
import functools
import math

import jax
import jax.numpy as jnp
import numpy as np
from jax import lax
from jax.experimental import pallas as pl
from jax.experimental.pallas import tpu as pltpu

F32 = jnp.float32
BF16 = jnp.bfloat16
MESH = pl.DeviceIdType.MESH

HEAD_DIM = 128
ROPE_THETA = 10000.0
GRID_W = 64
EPS = 1e-6
NEG = -1e30
A_HEADS, A_Q_RANK, A_KV_RANK, A_NOPE, A_ROPE, A_V = 4, 512, 512, 128, 64, 128
B_HEADS = 6
B_PATTERNS = ((128, 1), (512, 4), (2048, 16))
C_HEADS, C_KV_HEADS = 6, 2
C_GROUP = C_HEADS // C_KV_HEADS
A_WIDTH, B_WIDTH, C_WIDTH = A_HEADS * A_V, B_HEADS * HEAD_DIM, C_HEADS * HEAD_DIM
IN_A = A_Q_RANK + A_KV_RANK + A_ROPE
IN_B = 3 * B_WIDTH
IN_C = C_WIDTH + 2 * C_KV_HEADS * HEAD_DIM
ADAM_LR, ADAM_B1, ADAM_B2, ADAM_EPS, ADAM_WD, ADAM_STEP = 0.001, 0.9, 0.999, 1e-08, 0.01, 10

LANE = 128
SUBLANE = 8
VMEM_BYTES_V7X = 64 * 1024 * 1024
VMEM_LIMIT_CAP = VMEM_BYTES_V7X - 8 * 1024 * 1024
N_CHIPS = 4
N_DEV = 8

A_PAD = 12 * LANE
PB_CQ, PB_CKV, PB_KR = 0, 4, 8
PB_BQ, PB_BK, PB_BV = 12, 18, 24
PB_CQH, PB_CKH, PB_CVH = 30, 36, 38
NP = 40 * LANE
A_QK = 2 * LANE


def _pick(n, cap, mult=LANE):
    if n <= cap:
        return n
    t = cap - cap % mult
    while t >= mult:
        if n % t == 0:
            return t
        t -= mult
    return n


def _rows_for(width_bytes, n_rows, target=2 * 1024 * 1024):
    return _pick(n_rows, max(SUBLANE, target // max(width_bytes, 1)), SUBLANE)


def _tile2(rows, cols, target):
    tc = _pick(cols, 4 * LANE)
    if tc < 4 * LANE:
        tc = cols
    fits = [t for t in range(SUBLANE, rows + 1, SUBLANE) if rows % t == 0] or [rows]
    return min(fits, key=lambda t: abs(math.log(t * tc * 4 / target))), tc


def _params(est_bytes):
    limit = int(min(max(est_bytes + (4 << 20), 32 << 20), VMEM_LIMIT_CAP))
    return pltpu.CompilerParams(vmem_limit_bytes=limit)


def _isz(x):
    return jnp.dtype(x.dtype).itemsize


def _hbm(shape, dtype):
    return pltpu.HBM(shape, dtype)


def _pin(*arrays):
    return [pltpu.with_memory_space_constraint(a, pltpu.HBM) for a in arrays]


_DIMS = {"nn": (((1,), (0,)), ((), ())), "nt": (((1,), (1,)), ((), ())), "tn": (((0,), (0,)), ((), ()))}


def _matmul(a, b, *, mode, out_dtype, name, add=None, tm=512, tn=512, col_shards=False, after=None):
    if mode == "tn":
        (k, m), (k2, n) = a.shape, b.shape
    elif mode == "nt":
        (m, k), (n, k2) = a.shape, b.shape
    else:
        (m, k), (k2, n) = a.shape, b.shape
    assert k == k2, (a.shape, b.shape, mode)
    tm, tn = _pick(m, tm), (n // N_CHIPS if col_shards else _pick(n, tn))
    a_spec = pl.BlockSpec((k, tm), lambda i, j: (0, i)) if mode == "tn" else pl.BlockSpec((tm, k), lambda i, j: (i, 0))
    b_spec = pl.BlockSpec((tn, k), lambda i, j: (j, 0)) if mode == "nt" else pl.BlockSpec((k, tn), lambda i, j: (0, j))
    o_spec = pl.BlockSpec((None, tm, tn), lambda i, j: (j, i, 0)) if col_shards else pl.BlockSpec((tm, tn), lambda i, j: (i, j))
    dims = _DIMS[mode]

    def body(*refs):
        a_ref, b_ref, o_ref = refs[0], refs[1], refs[-1]
        acc = lax.dot_general(a_ref[...].astype(BF16), b_ref[...].astype(BF16), dims, preferred_element_type=F32)
        if add is not None:
            acc = acc + refs[2][...].astype(F32)
        o_ref[...] = acc.astype(out_dtype)

    ins, specs = [a, b], [a_spec, b_spec]
    if add is not None:
        ins.append(add)
        specs.append(o_spec)
    if after is not None:
        ins.append(after)
        specs.append(pl.BlockSpec(memory_space=pl.ANY))
    est = 2 * (tm * k * _isz(a) + tn * k * _isz(b) + tm * tn * (jnp.dtype(out_dtype).itemsize + (4 if add is not None else 0)))
    est += (tm + tn) * k * 2 + 2 * tm * tn * 4
    return pl.pallas_call(
        body, name=name, grid=(m // tm, n // tn), in_specs=specs, out_specs=o_spec,
        out_shape=_hbm((N_CHIPS, m, tn) if col_shards else (m, n), out_dtype),
        compiler_params=_params(est),
    )(*_pin(*ins))


def _ffn_up(h, wg, wu, *, name):
    s, d = h.shape
    _, _, c = wg.shape
    tm = _pick(s, 512, SUBLANE)

    def body(h_ref, wg_ref, wu_ref, g_ref, u_ref, a_ref):
        hv = h_ref[...]
        gv = jnp.dot(hv, wg_ref[...], preferred_element_type=F32)
        uv = jnp.dot(hv, wu_ref[...], preferred_element_type=F32)
        g_ref[...] = gv.astype(BF16)
        u_ref[...] = uv.astype(BF16)
        a_ref[...] = (gv / (1.0 + jnp.exp(-gv)) * uv).astype(BF16)

    w_spec = pl.BlockSpec((None, d, c), lambda j, i: (j, 0, 0))
    o_spec = pl.BlockSpec((tm, c), lambda j, i: (i, j))
    est = 2 * (tm * d * 2 + 2 * d * c * 2 + tm * c * 10) + 4 * tm * c * 4
    return pl.pallas_call(
        body, name=name, grid=(N_CHIPS, s // tm), in_specs=[pl.BlockSpec((tm, d), lambda j, i: (i, 0)), w_spec, w_spec],
        out_specs=[o_spec, o_spec, o_spec],
        out_shape=[_hbm((s, N_CHIPS * c), BF16)] * 3,
        compiler_params=_params(est),
    )(*_pin(h, wg, wu))


def _ffn_down_dx(dx, w_down, gate, up, *, name):
    s, d = dx.shape
    f = w_down.shape[0]
    tm, tn = _pick(s, 512, SUBLANE), _pick(f, 512)

    def body(dx_ref, w_ref, g_ref, u_ref, dg_ref, du_ref):
        dact = lax.dot_general(dx_ref[...], w_ref[...], _DIMS["nt"], preferred_element_type=F32)
        gv, uv = g_ref[...].astype(F32), u_ref[...].astype(F32)
        sig = 1.0 / (1.0 + jnp.exp(-gv))
        dg_ref[...] = (dact * uv * (sig * (1.0 + gv * (1.0 - sig)))).astype(BF16)
        du_ref[...] = (dact * (gv * sig)).astype(BF16)

    t_spec = pl.BlockSpec((tm, tn), lambda i, j: (i, j))
    est = 2 * (tm * d * 2 + tn * d * 2 + tm * tn * 12) + 6 * tm * tn * 4
    return pl.pallas_call(
        body, name=name, grid=(s // tm, f // tn),
        in_specs=[pl.BlockSpec((tm, d), lambda i, j: (i, 0)), pl.BlockSpec((tn, d), lambda i, j: (j, 0)), t_spec, t_spec],
        out_specs=[t_spec, t_spec], out_shape=[_hbm((s, f), BF16)] * 2, compiler_params=_params(est),
    )(*_pin(dx, w_down, gate, up))


def _ffn_up_dx(dgate, dup, wg, wu, *, name):
    s, f = dgate.shape
    _, d, c = wg.shape
    tm, tn = _pick(s, 1024, SUBLANE), _pick(d, 1024)
    nk = 2 * N_CHIPS

    def body(dg_ref, du_ref, wg_ref, wu_ref, o_ref, acc):
        kk = pl.program_id(2)

        @pl.when(kk == 0)
        def _():
            acc[...] = jnp.zeros_like(acc)

        @pl.when(kk < N_CHIPS)
        def _():
            acc[...] += lax.dot_general(dg_ref[...], wg_ref[...], _DIMS["nt"], preferred_element_type=F32)

        @pl.when(kk >= N_CHIPS)
        def _():
            acc[...] += lax.dot_general(du_ref[...], wu_ref[...], _DIMS["nt"], preferred_element_type=F32)

        @pl.when(kk == nk - 1)
        def _():
            o_ref[...] = acc[...]

    last = N_CHIPS - 1
    est = 2 * (2 * tm * c * 2 + 2 * tn * c * 2 + tm * tn * 4) + 2 * tm * tn * 4
    return pl.pallas_call(
        body, name=name, grid=(s // tm, d // tn, nk),
        in_specs=[pl.BlockSpec((tm, c), lambda i, j, kk: (i, jnp.minimum(kk, last))),
                  pl.BlockSpec((tm, c), lambda i, j, kk: (i, jnp.maximum(kk - N_CHIPS, 0))),
                  pl.BlockSpec((None, tn, c), lambda i, j, kk: (jnp.minimum(kk, last), j, 0)),
                  pl.BlockSpec((None, tn, c), lambda i, j, kk: (jnp.maximum(kk - N_CHIPS, 0), j, 0))],
        out_specs=pl.BlockSpec((tm, tn), lambda i, j, kk: (i, j)),
        out_shape=_hbm((s, d), F32), scratch_shapes=[pltpu.VMEM((tm, tn), F32)],
        compiler_params=_params(est),
    )(*_pin(dgate, dup, wg, wu))


def _norm_fwd(x, gain, *, wb, cb, nb, shared_gain, out_dtype, name):
    s = x.shape[0]
    ts = _rows_for(wb * 4, s)

    def body(x_ref, g_ref, o_ref):
        xv = x_ref[...].astype(F32)
        r = lax.rsqrt(jnp.mean(xv * xv, axis=1, keepdims=True) + EPS)
        o_ref[...] = ((xv * r) * g_ref[...]).astype(out_dtype)

    return pl.pallas_call(
        body, name=name, grid=(nb, s // ts),
        in_specs=[pl.BlockSpec((ts, wb), lambda n, i: (i, cb + n)),
                  pl.BlockSpec((1, wb), (lambda n, i: (0, 0)) if shared_gain else (lambda n, i: (0, n)))],
        out_specs=pl.BlockSpec((ts, wb), lambda n, i: (i, n)),
        out_shape=_hbm((s, nb * wb), out_dtype), compiler_params=_params(6 * ts * wb * 4),
    )(*_pin(x), gain)


def _norm_bwd(x, gain, dy, *, wb, cb, nb, shared_gain, out_dtype, name, dy_cb=0, add=None, bf16_copy=False):
    s = x.shape[0]
    ts = _rows_for(wb * 4, s, target=1024 * 1024)
    gw = wb if shared_gain else nb * wb

    def body(*refs):
        refs = list(refs)
        dxb_ref = refs.pop() if bf16_copy else None
        if add is None:
            x_ref, g_ref, dy_ref, dx_ref, dg_ref = refs
        else:
            x_ref, g_ref, dy_ref, add_ref, dx_ref, dg_ref = refs
        n, i = pl.program_id(0), pl.program_id(1)
        xv = x_ref[...].astype(F32)
        dyv = dy_ref[...].astype(F32)
        r = lax.rsqrt(jnp.mean(xv * xv, axis=1, keepdims=True) + EPS)
        xh = xv * r
        dyg = dyv * g_ref[...]
        dx = r * (dyg - xh * jnp.mean(dyg * xh, axis=1, keepdims=True))
        if add is not None:
            dx = dx + add_ref[...]
        dx_ref[...] = dx.astype(out_dtype)
        if bf16_copy:
            dxb_ref[...] = dx.astype(BF16)
        first = jnp.logical_and(n == 0, i == 0) if shared_gain else (i == 0)

        @pl.when(first)
        def _():
            dg_ref[...] = jnp.zeros_like(dg_ref)

        dg_ref[...] += jnp.sum(dyv * xh, axis=0, keepdims=True)

    ins = [x, gain, dy]
    specs = [pl.BlockSpec((ts, wb), lambda n, i: (i, cb + n)),
             pl.BlockSpec((1, wb), (lambda n, i: (0, 0)) if shared_gain else (lambda n, i: (0, n))),
             pl.BlockSpec((ts, wb), lambda n, i: (i, dy_cb + n))]
    if add is not None:
        ins.append(add)
        specs.append(pl.BlockSpec((ts, wb), lambda n, i: (i, n)))
    out_specs = [pl.BlockSpec((ts, wb), lambda n, i: (i, n)),
                 pl.BlockSpec((1, wb), (lambda n, i: (0, 0)) if shared_gain else (lambda n, i: (0, n)))]
    out_shape = [_hbm((s, nb * wb), out_dtype), jax.ShapeDtypeStruct((1, gw), F32)]
    if bf16_copy:
        out_specs.append(out_specs[0])
        out_shape.append(_hbm((s, nb * wb), BF16))
    return pl.pallas_call(
        body, name=name, grid=(nb, s // ts), in_specs=specs, out_specs=out_specs, out_shape=out_shape,
        compiler_params=_params(14 * ts * wb * 4),
    )(*_pin(*ins))


def _swap_halves(x, half):
    if 2 * half == LANE:
        return pltpu.roll(x, half, axis=1)
    lane = lax.broadcasted_iota(jnp.int32, x.shape, 1)
    first = jnp.bitwise_and(lane, 2 * half - 1) < half
    return jnp.where(first, pltpu.roll(x, LANE - half, axis=1), pltpu.roll(x, half, axis=1))


def _rope(x, cos_t, sin_t, *, tw, cb, nb, half, sign, out_dtype, name):
    s = x.shape[0]
    ts = _rows_for(tw * 4, s)

    def body(x_ref, c_ref, s_ref, o_ref):
        for q in range(tw // LANE):
            sl = slice(q * LANE, (q + 1) * LANE)
            xv = x_ref[:, sl].astype(F32)
            sv = s_ref[:, sl]
            if sign < 0:
                sv = -sv
            o_ref[:, sl] = (xv * c_ref[:, sl] + _swap_halves(xv, half) * sv).astype(out_dtype)

    return pl.pallas_call(
        body, name=name, grid=(nb, s // ts),
        in_specs=[pl.BlockSpec((ts, tw), lambda n, i: (i, cb + n)),
                  pl.BlockSpec((ts, tw), lambda n, i: (i, 0)),
                  pl.BlockSpec((ts, tw), lambda n, i: (i, 0))],
        out_specs=pl.BlockSpec((ts, tw), lambda n, i: (i, n)),
        out_shape=_hbm((s, nb * tw), out_dtype), compiler_params=_params(10 * ts * tw * 4),
    )(*_pin(x), cos_t, sin_t)


def _latent_keys(kv, p, cos_t, sin_t, *, kr_cb, name):
    s = kv.shape[0]
    ts = _rows_for(A_QK * 4, s)

    def body(kv_ref, kr_ref, c_ref, s_ref, o_ref):
        o_ref[:, :LANE] = kv_ref[...].astype(BF16)
        x = kr_ref[...].astype(F32)
        o_ref[:, LANE:] = (x * c_ref[:, LANE:] + _swap_halves(x, A_ROPE // 2) * s_ref[:, LANE:]).astype(BF16)

    tab = pl.BlockSpec((ts, A_QK), lambda n, i: (i, 0))
    return pl.pallas_call(
        body, name=name, grid=(A_HEADS, s // ts),
        in_specs=[pl.BlockSpec((ts, LANE), lambda n, i: (i, 2 * n)), pl.BlockSpec((ts, LANE), lambda n, i: (i, kr_cb)),
                  tab, tab],
        out_specs=pl.BlockSpec((ts, A_QK), lambda n, i: (i, n)),
        out_shape=_hbm((s, A_HEADS * A_QK), BF16), compiler_params=_params(10 * ts * A_QK * 4),
    )(*_pin(kv, p), cos_t, sin_t)


def _latent_keys_bwd(dka, dva, cos_t, sin_t, *, name):
    s = dka.shape[0]
    ts = _rows_for(A_HEADS * A_QK * 4, s)

    def body(dka_ref, dva_ref, c_ref, s_ref, dkv_ref, dkr_ref):
        acc = jnp.zeros((ts, LANE), F32)
        for h in range(A_HEADS):
            dkv_ref[:, h * A_QK:h * A_QK + LANE] = dka_ref[:, h * A_QK:h * A_QK + LANE].astype(BF16)
            dkv_ref[:, h * A_QK + LANE:(h + 1) * A_QK] = dva_ref[:, h * LANE:(h + 1) * LANE].astype(BF16)
            y = dka_ref[:, h * A_QK + LANE:(h + 1) * A_QK]
            acc = acc + (y * c_ref[:, LANE:] - _swap_halves(y, A_ROPE // 2) * s_ref[:, LANE:])
        dkr_ref[...] = acc.astype(BF16)

    def rows(width):
        return pl.BlockSpec((ts, width), lambda i: (i, 0))

    return pl.pallas_call(
        body, name=name, grid=(s // ts,),
        in_specs=[rows(A_HEADS * A_QK), rows(A_HEADS * LANE), rows(A_QK), rows(A_QK)],
        out_specs=[rows(A_HEADS * A_QK), rows(LANE)],
        out_shape=[_hbm((s, A_HEADS * A_QK), BF16), _hbm((s, LANE), BF16)],
        compiler_params=_params(8 * ts * A_HEADS * A_QK * 4),
    )(*_pin(dka, dva), cos_t, sin_t)


def _cast_cols(x, *, cb, nb, name):
    s = x.shape[0]
    ts = _rows_for(LANE * 4, s)

    def body(x_ref, o_ref):
        o_ref[...] = x_ref[...].astype(BF16)

    return pl.pallas_call(
        body, name=name, grid=(nb, s // ts), in_specs=[pl.BlockSpec((ts, LANE), lambda n, i: (i, cb + n))],
        out_specs=pl.BlockSpec((ts, LANE), lambda n, i: (i, n)),
        out_shape=_hbm((s, nb * LANE), BF16), compiler_params=_params(4 * ts * LANE * 4),
    )(*_pin(x))


LOG2E = 1.4426950408889634
ATTN_ROW_CHUNK = 256


def _attn_window(i, tq, s, band):
    w, r = band
    start = jnp.clip(i * tq - r, 0, s - w)
    return pl.multiple_of(start, tq), pl.multiple_of((w - tq) - (i * tq - start), LANE)


def _flash_fwd(q, k, v, table, *, hkv, g, dqk, q_cb, k_cb, v_cb, v_step, scale, tq, band, name):
    s = q.shape[0]
    n = s // tq
    hq = hkv * g
    rc = min(tq, ATTN_ROW_CHUNK)
    w = s if band is None else band[0]

    def body(*refs):
        if band is None:
            q_ref, k_ref, v_ref, o_ref, lse_ref = refs
            kw, vw = k_ref[...], v_ref[...]
        else:
            q_ref, k_ref, v_ref, t_ref, o_ref, lse_ref = refs
            start, u = _attn_window(pl.program_id(1), tq, s, band)
            kw, vw = k_ref[pl.ds(start, w), :], v_ref[pl.ds(start, w), :]
        for c in range(tq // rc):
            rows = slice(c * rc, (c + 1) * rc)
            sc = lax.dot_general(q_ref[rows, :], kw, _DIMS["nt"], preferred_element_type=F32) * (scale * LOG2E)
            if band is not None:
                sc = sc + t_ref[rows, pl.ds(u, w)]
            m = jnp.max(sc, axis=1, keepdims=True)
            p = jnp.exp2(sc - m)
            l = jnp.sum(p, axis=1, keepdims=True)
            o_ref[rows, :] = jnp.dot(p.astype(BF16), vw, preferred_element_type=F32) / l
            lse_ref[0, rows, :] = jnp.broadcast_to(m + jnp.log2(l), (rc, LANE))

    ins = [q, k, v]
    specs = [pl.BlockSpec((tq, dqk), lambda h, i: (i, q_cb + h)),
             pl.BlockSpec((s, dqk), lambda h, i: (0, k_cb + h // g)),
             pl.BlockSpec((s, LANE), lambda h, i: (0, v_cb + v_step * (h // g)))]
    if band is not None:
        ins.append(table)
        specs.append(pl.BlockSpec(table.shape, lambda h, i: (0, 0)))
    est = 4 * s * (dqk + LANE) + 6 * rc * w * 4 + 8 * tq * LANE * 4 + (0 if band is None else 2 * table.size * 4)
    return pl.pallas_call(
        body, name=name, grid=(hq, n), in_specs=specs,
        out_specs=[pl.BlockSpec((tq, LANE), lambda h, i: (i, h)), pl.BlockSpec((1, tq, LANE), lambda h, i: (h, i, 0))],
        out_shape=[_hbm((s, hq * LANE), F32), _hbm((hq, s, LANE), F32)],
        compiler_params=_params(est),
    )(*_pin(*ins))


def _flash_bwd(q, k, v, o, do, lse, table, *, hkv, g, dqk, q_cb, k_cb, v_cb, v_step, scale, tq, band, name):
    s = q.shape[0]
    n = s // tq
    hq = hkv * g
    rc = min(tq, ATTN_ROW_CHUNK)
    w = s if band is None else band[0]

    def body(*refs):
        if band is None:
            q_ref, k_ref, v_ref, o_ref, do_ref, lse_ref, dq_ref, dk_ref, dv_ref = refs
            keys = slice(None)
        else:
            q_ref, k_ref, v_ref, o_ref, do_ref, lse_ref, t_ref, dq_ref, dk_ref, dv_ref = refs
            start, u = _attn_window(pl.program_id(1), tq, s, band)
            keys = pl.ds(start, w)
        h, i = pl.program_id(0), pl.program_id(1)

        @pl.when(jnp.logical_and(h % g == 0, i == 0))
        def _():
            dk_ref[...] = jnp.zeros_like(dk_ref)
            dv_ref[...] = jnp.zeros_like(dv_ref)

        kw, vw = k_ref[keys, :], v_ref[keys, :]
        for c in range(tq // rc):
            rows = slice(c * rc, (c + 1) * rc)
            qv = q_ref[rows, :]
            dof = do_ref[rows, :]
            dov = dof.astype(BF16)
            sc = lax.dot_general(qv, kw, _DIMS["nt"], preferred_element_type=F32) * (scale * LOG2E)
            if band is not None:
                sc = sc + t_ref[rows, pl.ds(u, w)]
            p = jnp.exp2(sc - lse_ref[0, rows, 0:1])
            dp = lax.dot_general(dov, vw, _DIMS["nt"], preferred_element_type=F32)
            delta = jnp.sum(dof * o_ref[rows, :], axis=1, keepdims=True)
            ds = (p * (dp - delta) * scale).astype(BF16)
            dv_ref[keys, :] += lax.dot_general(p.astype(BF16), dov, _DIMS["tn"], preferred_element_type=F32)
            dk_ref[keys, :] += lax.dot_general(ds, qv, _DIMS["tn"], preferred_element_type=F32)
            dq_ref[rows, :] = jnp.dot(ds, kw, preferred_element_type=F32)

    ins = [q, k, v, o, do, lse]
    specs = [pl.BlockSpec((tq, dqk), lambda h, i: (i, q_cb + h)),
             pl.BlockSpec((s, dqk), lambda h, i: (0, k_cb + h // g)),
             pl.BlockSpec((s, LANE), lambda h, i: (0, v_cb + v_step * (h // g))),
             pl.BlockSpec((tq, LANE), lambda h, i: (i, h)),
             pl.BlockSpec((tq, LANE), lambda h, i: (i, h)),
             pl.BlockSpec((1, tq, LANE), lambda h, i: (h, i, 0))]
    if band is not None:
        ins.append(table)
        specs.append(pl.BlockSpec(table.shape, lambda h, i: (0, 0)))
    est = (4 + 8) * s * (dqk + LANE) + 10 * rc * w * 4 + 12 * tq * LANE * 4 + (0 if band is None else 2 * table.size * 4)
    return pl.pallas_call(
        body, name=name, grid=(hq, n), in_specs=specs,
        out_specs=[pl.BlockSpec((tq, dqk), lambda h, i: (i, h)),
                   pl.BlockSpec((s, dqk), lambda h, i: (0, h // g)),
                   pl.BlockSpec((s, LANE), lambda h, i: (0, h // g))],
        out_shape=[_hbm((s, hq * dqk), F32), _hbm((s, hkv * dqk), F32),
                   _hbm((s, hkv * LANE), F32)],
        compiler_params=_params(est),
    )(*_pin(*ins))


def _final_loss(x, gain, target, *, name):
    s, d = x.shape
    ts = _rows_for(d * 4, s, target=1024 * 1024)

    def body(x_ref, g_ref, t_ref, dx_ref, dg_ref, loss_ref, dxb_ref):
        i = pl.program_id(0)
        xv = x_ref[...]
        gv = g_ref[...]
        r = lax.rsqrt(jnp.mean(xv * xv, axis=1, keepdims=True) + EPS)
        xh = xv * r
        err = xh * gv - t_ref[...]
        dy = err / d
        dyg = dy * gv
        dx = r * (dyg - xh * jnp.mean(dyg * xh, axis=1, keepdims=True))
        dx_ref[...] = dx
        dxb_ref[...] = dx.astype(BF16)

        @pl.when(i == 0)
        def _():
            dg_ref[...] = jnp.zeros_like(dg_ref)
            loss_ref[...] = jnp.zeros_like(loss_ref)

        dg_ref[...] += jnp.sum(dy * xh, axis=0, keepdims=True)
        part = jnp.sum(jnp.mean(err * err, axis=1, keepdims=True), axis=0, keepdims=True)
        loss_ref[...] += jnp.broadcast_to(0.5 * part, (1, LANE))

    row = pl.BlockSpec((ts, d), lambda i: (i, 0))
    return pl.pallas_call(
        body, name=name, grid=(s // ts,),
        in_specs=[row, pl.BlockSpec((1, d), lambda i: (0, 0)), row],
        out_specs=[row, pl.BlockSpec((1, d), lambda i: (0, 0)), pl.BlockSpec((1, LANE), lambda i: (0, 0)), row],
        out_shape=[_hbm((s, d), F32), jax.ShapeDtypeStruct((1, d), F32),
                   jax.ShapeDtypeStruct((1, LANE), F32), _hbm((s, d), BF16)],
        compiler_params=_params(14 * ts * d * 4),
    )(*_pin(x), gain, *_pin(target))


def _cast_to_slot(x3d, me, *, layer, name):
    _, rows, c = x3d.shape
    tr, tc = _tile2(rows, c, 2 * 1024 * 1024)

    def body(me_ref, x_ref, o_ref):
        o_ref[...] = x_ref[...].astype(BF16)

    return pl.pallas_call(
        body, name=name,
        grid_spec=pltpu.PrefetchScalarGridSpec(
            num_scalar_prefetch=1, grid=(rows // tr, c // tc),
            in_specs=[pl.BlockSpec((None, tr, tc), lambda i, j, me_ref: (layer, i, j))],
            out_specs=pl.BlockSpec((None, tr, tc), lambda i, j, me_ref: (me_ref[0], i, j))),
        out_shape=_hbm((N_CHIPS, rows, c), BF16), compiler_params=_params(6 * tr * tc * 4),
    )(me, *_pin(x3d))


def _sum_parts(srcs, lands, me, *, name):
    depth = len(srcs)
    _, r, c = srcs[0].shape
    tr, tc = _tile2(r, c, 1024 * 1024)
    nt, nc = r // tr, c // tc

    def body(me_ref, *refs):
        o_ref = refs[-1]
        l = pl.program_id(0)
        for k in range(depth):
            @pl.when(l == k)
            def _(k=k):
                acc = refs[k][...].astype(F32)
                for p in range(3):
                    acc = acc + refs[depth + k][p].astype(F32)
                o_ref[...] = acc

    def tile_of(k):
        def f(l, i, j):
            return (jnp.where(l == k, i, jnp.where(l < k, 0, nt - 1)), jnp.where(l == k, j, jnp.where(l < k, 0, nc - 1)))
        return f

    in_specs = [pl.BlockSpec((None, tr, tc), functools.partial(lambda l, i, j, me_ref, f: (me_ref[0], *f(l, i, j)), f=tile_of(k)))
                for k in range(depth)]
    in_specs += [pl.BlockSpec((3, tr, tc), functools.partial(lambda l, i, j, me_ref, f: (0, *f(l, i, j)), f=tile_of(k)))
                 for k in range(depth)]
    return pl.pallas_call(
        body, name=name,
        grid_spec=pltpu.PrefetchScalarGridSpec(
            num_scalar_prefetch=1, grid=(depth, nt, nc), in_specs=in_specs,
            out_specs=pl.BlockSpec((tr, tc), lambda l, i, j, me_ref: (l * nt + i, j))),
        out_shape=_hbm((depth * r, c), F32), compiler_params=_params(depth * 10 * tr * tc * 4),
    )(me, *_pin(*srcs, *lands))


def _adamw_math(w, g, m, v):
    m2 = ADAM_B1 * m + (1.0 - ADAM_B1) * g
    v2 = ADAM_B2 * v + (1.0 - ADAM_B2) * (g * g)
    m_hat = m2 / (1.0 - ADAM_B1 ** ADAM_STEP)
    v_hat = v2 / (1.0 - ADAM_B2 ** ADAM_STEP)
    delta = -ADAM_LR * (m_hat / (jnp.sqrt(v_hat) + ADAM_EPS) + ADAM_WD * w)
    return delta, m2, v2


def _adamw(g_a, g_b, w, m, v, *, name):
    depth, r, c = w.shape
    tr, tc = _tile2(r, c, 512 * 1024)
    nt = r // tr

    def body(a_ref, b_ref, w_ref, m_ref, v_ref, g_out, d_out, m_out, v_out):
        gv = a_ref[...] + b_ref[...]
        delta, m2, v2 = _adamw_math(w_ref[...], gv, m_ref[...], v_ref[...])
        g_out[...] = gv
        d_out[...] = delta
        m_out[...] = m2
        v_out[...] = v2

    flat = pl.BlockSpec((tr, tc), lambda l, i, j: (l * nt + i, j))
    spec = pl.BlockSpec((None, tr, tc), lambda l, i, j: (l, i, j))
    return pl.pallas_call(
        body, name=name, grid=(depth, nt, c // tc), in_specs=[flat, flat, spec, spec, spec], out_specs=[spec] * 4,
        out_shape=[_hbm((depth, r, c), F32)] * 4, compiler_params=_params(22 * tr * tc * 4),
    )(*_pin(g_a, g_b, w, m, v))


def _small_adamw(g_all, w, m, v, *, name):
    r, c = w.shape

    def body(ga_ref, w_ref, m_ref, v_ref, g_out, d_out, m_out, v_out):
        gv = ga_ref[0]
        for j in range(1, N_DEV):
            gv = gv + ga_ref[j]
        delta, m2, v2 = _adamw_math(w_ref[...], gv, m_ref[...], v_ref[...])
        g_out[...] = gv
        d_out[...] = delta
        m_out[...] = m2
        v_out[...] = v2

    return pl.pallas_call(body, name=name, out_shape=[jax.ShapeDtypeStruct((r, c), F32)] * 4)(g_all, w, m, v)


_ANY = pl.BlockSpec(memory_space=pl.ANY)


_HBM = pl.BlockSpec(memory_space=pltpu.HBM)
_SEM = pl.BlockSpec(memory_space=pltpu.SEMAPHORE)
_EFFECT = pltpu.SideEffectType.DATAFLOW_SIDE_EFFECTING


def _peer_chips():
    x, y = lax.axis_index("x"), lax.axis_index("y")
    return 2 * x + y, [(1 - x, y), (x, 1 - y), (1 - x, 1 - y)]


def _exchange_copy(srcs, lands, send_sems, recv_sems, k, p, kind):
    c = lax.axis_index("c")
    if kind == "swap":
        return pltpu.make_async_remote_copy(
            src_ref=srcs[k], dst_ref=lands[k], send_sem=send_sems.at[k], recv_sem=recv_sems.at[k],
            device_id=(lax.axis_index("x"), lax.axis_index("y"), 1 - c), device_id_type=MESH)
    me, peers = _peer_chips()
    px, py = peers[p]
    return pltpu.make_async_remote_copy(
        src_ref=srcs[k].at[2 * px + py] if kind == "scatter" else srcs[k].at[me],
        dst_ref=lands[k].at[p] if kind == "scatter" else lands[k].at[me],
        send_sem=send_sems.at[3 * k + p], recv_sem=recv_sems.at[3 * k + p],
        device_id=(px, py, c), device_id_type=MESH)


def _exchange_start(srcs, lands, after, *, kind, name):
    n = len(srcs)
    npeer = 1 if kind == "swap" else 3
    bufs = list(srcs) + (list(lands) if lands is not None else [])
    nb = len(bufs)

    def body(*refs):
        buf_refs, send_sems, recv_sems = refs[:nb], refs[nb + 1], refs[nb + 2]
        token = refs[-1]
        s_refs = buf_refs[:n]
        l_refs = buf_refs[n:] if lands is not None else s_refs
        for k in range(n):
            for p in range(npeer):
                _exchange_copy(s_refs, l_refs, send_sems, recv_sems, k, p, kind).start()
        token[...] = jnp.zeros_like(token)

    out = pl.pallas_call(
        body, name=name,
        out_shape=(pltpu.SemaphoreType.DMA((npeer * n,)), pltpu.SemaphoreType.DMA((npeer * n,)),
                   *[pltpu.HBM(b.shape, b.dtype) for b in bufs], jax.ShapeDtypeStruct((SUBLANE, LANE), F32)),
        in_specs=[_HBM] * nb + [_ANY],
        out_specs=(_SEM, _SEM, *[_HBM] * nb, pl.BlockSpec(memory_space=pltpu.VMEM)),
        input_output_aliases={i: 2 + i for i in range(nb)},
        compiler_params=pltpu.CompilerParams(has_side_effects=_EFFECT),
    )(*[pltpu.with_memory_space_constraint(b, pltpu.HBM) for b in bufs], after)
    send_sems, recv_sems = out[0], out[1]
    thru = out[2:2 + nb]
    return send_sems, recv_sems, list(thru[:n]), (list(thru[n:]) if lands is not None else None), out[-1]


def _exchange_wait(send_sems, recv_sems, srcs, lands, after, *, kind, name):
    n = len(srcs)
    npeer = 1 if kind == "swap" else 3
    bufs = list(srcs) + (list(lands) if lands is not None else [])
    nb = len(bufs)

    def body(*refs):
        buf_refs, send_sems_ref, recv_sems_ref = refs[:nb], refs[nb], refs[nb + 1]
        s_refs = buf_refs[:n]
        l_refs = buf_refs[n:] if lands is not None else s_refs
        for k in range(n):
            for p in range(npeer):
                cp = _exchange_copy(s_refs, l_refs, send_sems_ref, recv_sems_ref, k, p, kind)
                cp.wait_send()
                cp.wait_recv()

    out = pl.pallas_call(
        body, name=name, out_shape=tuple(pltpu.HBM(b.shape, b.dtype) for b in bufs),
        in_specs=[_HBM] * nb + [_SEM, _SEM, _ANY], out_specs=tuple([_HBM] * nb),
        input_output_aliases={i: i for i in range(nb)},
        compiler_params=pltpu.CompilerParams(has_side_effects=_EFFECT),
    )(*bufs, send_sems, recv_sems, after)
    return list(out)


def _sibling_exchange(srcs, *, name):
    n = len(srcs)

    def body(*refs):
        src, out = refs[:n], refs[n:2 * n]
        send_sems, recv_sems = refs[2 * n:]
        sibling = (lax.axis_index("x"), lax.axis_index("y"), 1 - lax.axis_index("c"))
        copies = [pltpu.make_async_remote_copy(src_ref=src[k], dst_ref=out[k], send_sem=send_sems.at[k],
                                               recv_sem=recv_sems.at[k], device_id=sibling, device_id_type=MESH)
                  for k in range(n)]
        for cp in copies:
            cp.start()
        for cp in copies:
            cp.wait_recv()
        for cp in copies:
            cp.wait_send()

    return pl.pallas_call(
        body, name=name, in_specs=[_ANY] * n, out_specs=[_ANY] * n,
        out_shape=[jax.ShapeDtypeStruct(a.shape, a.dtype) for a in srcs],
        scratch_shapes=[pltpu.SemaphoreType.DMA((n,)), pltpu.SemaphoreType.DMA((n,))],
    )(*srcs)


def _all_gather_small(block, *, name):
    m_per, ncol = block.shape

    def body(x_ref, out_ref, send_sems, recv_sems, local_sem):
        x, y, c = lax.axis_index("x"), lax.axis_index("y"), lax.axis_index("c")
        me, sibling = (x, y, c), (x, y, 1 - c)
        chips = [(1 - x, y), (x, 1 - y), (1 - x, 1 - y)]

        def rows(px, py, pc):
            return out_ref.at[pl.ds((4 * px + 2 * py + pc) * m_per, m_per), :]

        def copy(k, blk, to, src=None):
            return pltpu.make_async_remote_copy(
                src_ref=rows(*blk) if src is None else src, dst_ref=rows(*blk),
                send_sem=send_sems.at[k], recv_sem=recv_sems.at[k], device_id=to, device_id_type=MESH)

        mine = pltpu.make_async_copy(x_ref, rows(*me), local_sem)
        mine.start()
        first = [copy(0, me, sibling, src=x_ref)]
        first += [copy(1 + j, me, (*chip, c), src=x_ref) for j, chip in enumerate(chips)]
        for cp in first:
            cp.start()
        passed = [copy(4 + j, (*chip, c), sibling) for j, chip in enumerate(chips)]
        for j, chip in enumerate(chips):
            copy(1 + j, (*chip, c), me).wait_recv()
            passed[j].start()
        copy(0, sibling, me).wait_recv()
        for j, chip in enumerate(chips):
            copy(4 + j, (*chip, 1 - c), me).wait_recv()
        for cp in first + passed:
            cp.wait_send()
        mine.wait()

    return pl.pallas_call(
        body, name=name, out_shape=jax.ShapeDtypeStruct((N_DEV * m_per, ncol), block.dtype),
        in_specs=[pl.BlockSpec(memory_space=pltpu.VMEM)], out_specs=pl.BlockSpec(memory_space=pltpu.VMEM),
        scratch_shapes=[pltpu.SemaphoreType.DMA((7,)), pltpu.SemaphoreType.DMA((7,)), pltpu.SemaphoreType.DMA],
    )(block)


def _rope_angles(pos, dim):
    inv = ROPE_THETA ** (-jnp.arange(0, dim, 2, dtype=F32) / dim)
    return pos.astype(F32)[:, None] * inv[None, :]


def _rope_tables(s):
    pos = jnp.arange(s, dtype=jnp.int32)
    rows = s // GRID_W
    row = jnp.repeat(jnp.arange(rows, dtype=jnp.int32), GRID_W)
    col = jnp.tile(jnp.arange(GRID_W, dtype=jnp.int32), rows)
    a1 = _rope_angles(pos, HEAD_DIM)
    aa = _rope_angles(pos, A_ROPE)
    ar = _rope_angles(row, HEAD_DIM // 2)
    ac = _rope_angles(col, HEAD_DIM // 2)
    one = jnp.ones((s, LANE), F32)
    zero = jnp.zeros((s, LANE), F32)
    pad = LANE - A_ROPE
    cos_a = jnp.concatenate([one, jnp.cos(aa), jnp.cos(aa), jnp.ones((s, pad), F32)], axis=1)
    sin_a = jnp.concatenate([zero, -jnp.sin(aa), jnp.sin(aa), jnp.zeros((s, pad), F32)], axis=1)
    cos_b = jnp.concatenate([jnp.cos(a1), jnp.cos(a1)], axis=1)
    sin_b = jnp.concatenate([-jnp.sin(a1), jnp.sin(a1)], axis=1)
    cos_c = jnp.concatenate([jnp.cos(ar), jnp.cos(ar), jnp.cos(ac), jnp.cos(ac)], axis=1)
    sin_c = jnp.concatenate([-jnp.sin(ar), jnp.sin(ar), -jnp.sin(ac), jnp.sin(ac)], axis=1)
    return (cos_a, sin_a), (cos_b, sin_b), (cos_c, sin_c)


def _band_table(tq, s):
    reach = max((win // (2 * d)) * d for win, d in B_PATTERNS)
    r = -(-reach // tq) * tq
    w = min(s, tq + 2 * r)
    j = jnp.arange(tq, dtype=jnp.int32)[:, None]
    x = jnp.arange(2 * w - tq, dtype=jnp.int32)[None, :]
    rel = x - (w - tq) - j
    mult = jnp.zeros(rel.shape, F32)
    for win, d in B_PATTERNS:
        mult = mult + jnp.logical_and(rel % d == 0, jnp.abs(rel) <= (win // (2 * d)) * d).astype(F32)
    return jnp.where(mult > 0, jnp.log2(jnp.maximum(mult, 1.0)), NEG), (w, r)


_BIG = ("w_in", "a_w_uq", "a_w_ukv", "w_out", "w_gate", "w_up", "w_down")
_SMALL = ("attn_norm", "a_q_norm", "a_kv_norm", "c_q_norm", "c_k_norm", "out_norm", "ffn_norm", "final_norm")
_WEIGHTS = ("attn_norm", "w_in", "a_q_norm", "a_w_uq", "a_kv_norm", "a_w_ukv", "c_q_norm", "c_k_norm", "out_norm",
            "w_out", "ffn_norm", "w_gate", "w_up", "w_down", "final_norm")


_ATTN = ("w_in", "a_w_uq", "a_w_ukv")
_FFN = ("w_out", "w_gate", "w_up", "w_down")


def _from_cols(a):
    return jnp.transpose(a, (1, 0, 2)).reshape(a.shape[1], N_CHIPS * a.shape[2])


def _from_rows(a):
    return a.reshape(N_CHIPS * a.shape[1], a.shape[2])


def _to_cols(a):
    return jnp.transpose(a.reshape(a.shape[0], N_CHIPS, a.shape[1] // N_CHIPS), (1, 0, 2))


def _to_rows(a):
    return a.reshape(N_CHIPS, a.shape[0] // N_CHIPS, a.shape[1])


def _assemble_attn(gw):
    w_in_t, uq, ukv = _from_rows(gw[0]), _from_cols(gw[1]), _from_cols(gw[2])
    d = w_in_t.shape[1]
    w_all = jnp.concatenate([w_in_t[:IN_A], jnp.zeros((A_PAD - IN_A, d), BF16), w_in_t[IN_A:]], axis=0)
    uq = uq.reshape(A_Q_RANK, A_HEADS, A_NOPE + A_ROPE)
    uq = jnp.pad(uq, ((0, 0), (0, 0), (0, A_QK - A_NOPE - A_ROPE))).reshape(A_Q_RANK, A_HEADS * A_QK)
    return dict(w_all=w_all, uq=uq, ukv=ukv)


def _assemble_ffn(gw):
    return dict(w_out=_from_rows(gw[0]), w_gate=gw[1], w_up=gw[2], w_down=_from_rows(gw[3]))


def _split_attn_grads(gl):
    w_all = gl["w_all"]
    w_in_t = jnp.concatenate([w_all[:IN_A], w_all[A_PAD:]], axis=0)
    uq = gl["uq"].reshape(A_Q_RANK, A_HEADS, A_QK)[:, :, :A_NOPE + A_ROPE].reshape(A_Q_RANK, A_HEADS * (A_NOPE + A_ROPE))
    return [_to_rows(w_in_t), _to_cols(uq), _to_cols(gl["ukv"])]


def _split_ffn_grads(gl):
    return [_to_rows(gl["w_out"]), gl["w_gate"], gl["w_up"], _to_rows(gl["w_down"])]


def _tie(a, token):
    return a + token[0:1, 0:1]


def _layer_fwd(x, wl, ffn_weights, sm, tabs, bias, t):
    s = x.shape[0]
    (cos_a, sin_a), (cos_b, sin_b), (cos_c, sin_c) = tabs
    h = _norm_fwd(x, sm["attn_norm"], wb=x.shape[1], cb=0, nb=1, shared_gain=True, out_dtype=BF16, name="attn_norm_fwd")
    p = _matmul(h, wl["w_all"], mode="nt", out_dtype=F32, name="in_proj", tm=1024, tn=640)
    cq_n = _norm_fwd(p, sm["a_q_norm"], wb=A_Q_RANK, cb=0, nb=1, shared_gain=True, out_dtype=BF16, name="a_q_norm_fwd")
    ckv_n = _norm_fwd(p, sm["a_kv_norm"], wb=A_KV_RANK, cb=1, nb=1, shared_gain=True, out_dtype=BF16, name="a_kv_norm_fwd")
    qa_raw = _matmul(cq_n, wl["uq"], mode="nn", out_dtype=F32, name="a_uq", tm=1024, tn=1024)
    kv = _matmul(ckv_n, wl["ukv"], mode="nn", out_dtype=BF16, name="a_ukv", tm=1024, tn=1024)
    qa =_rope(qa_raw, cos_a, sin_a, tw=A_QK, cb=0, nb=A_HEADS, half=A_ROPE // 2, sign=1, out_dtype=BF16, name="a_rope_q")
    ka = _latent_keys(kv, p, cos_a, sin_a, kr_cb=PB_KR, name="a_keys")
    oa, lse_a = _flash_fwd(qa, ka, kv, None, hkv=A_HEADS, g=1, dqk=A_QK, q_cb=0, k_cb=0, v_cb=1, v_step=2,
                           scale=(A_NOPE + A_ROPE) ** -0.5, tq=_pick(s, 2 * t), band=None, name="a_flash_fwd")
    table, band = bias
    qb = _rope(p, cos_b, sin_b, tw=LANE, cb=PB_BQ, nb=B_HEADS, half=HEAD_DIM // 2, sign=1, out_dtype=BF16, name="b_rope_q")
    kb = _rope(p, cos_b, sin_b, tw=LANE, cb=PB_BK, nb=B_HEADS, half=HEAD_DIM // 2, sign=1, out_dtype=BF16, name="b_rope_k")
    vb = _cast_cols(p, cb=PB_BV, nb=B_HEADS, name="b_cast_v")
    ob, lse_b = _flash_fwd(qb, kb, vb, table, hkv=B_HEADS, g=1, dqk=LANE, q_cb=0, k_cb=0, v_cb=0, v_step=1,
                           scale=HEAD_DIM ** -0.5, tq=t, band=band, name="b_flash_fwd")
    qn = _norm_fwd(p, sm["c_q_norm"], wb=LANE, cb=PB_CQH, nb=C_HEADS, shared_gain=True, out_dtype=F32, name="c_q_norm_fwd")
    kn = _norm_fwd(p, sm["c_k_norm"], wb=LANE, cb=PB_CKH, nb=C_KV_HEADS, shared_gain=True, out_dtype=F32, name="c_k_norm_fwd")
    qc = _rope(qn, cos_c, sin_c, tw=LANE, cb=0, nb=C_HEADS, half=HEAD_DIM // 4, sign=1, out_dtype=BF16, name="c_rope_q")
    kc = _rope(kn, cos_c, sin_c, tw=LANE, cb=0, nb=C_KV_HEADS, half=HEAD_DIM // 4, sign=1, out_dtype=BF16, name="c_rope_k")
    vc = _cast_cols(p, cb=PB_CVH, nb=C_KV_HEADS, name="c_cast_v")
    oc, lse_c = _flash_fwd(qc, kc, vc, None, hkv=C_KV_HEADS, g=C_GROUP, dqk=LANE, q_cb=0, k_cb=0, v_cb=0, v_step=1,
                           scale=HEAD_DIM ** -0.5, tq=_pick(s, 2 * t), band=None, name="c_flash_fwd")
    g_out = sm["out_norm"]
    ga, gb, gc = g_out[:, :A_WIDTH], g_out[:, A_WIDTH:A_WIDTH + B_WIDTH], g_out[:, A_WIDTH + B_WIDTH:]
    ya = _norm_fwd(oa, ga, wb=A_WIDTH, cb=0, nb=1, shared_gain=True, out_dtype=BF16, name="out_norm_a_fwd")
    yb = _norm_fwd(ob, gb, wb=B_WIDTH, cb=0, nb=1, shared_gain=True, out_dtype=BF16, name="out_norm_b_fwd")
    yc = _norm_fwd(oc, gc, wb=C_WIDTH, cb=0, nb=1, shared_gain=True, out_dtype=BF16, name="out_norm_c_fwd")
    y = jnp.concatenate([ya, yb, yc], axis=1)
    wl = {**wl, **ffn_weights(y)}
    x1 = _matmul(y, wl["w_out"], mode="nn", out_dtype=F32, name="out_proj", add=x, tm=1024, tn=512)
    h2 = _norm_fwd(x1, sm["ffn_norm"], wb=x.shape[1], cb=0, nb=1, shared_gain=True, out_dtype=BF16, name="ffn_norm_fwd")
    gate, up, act = _ffn_up(h2, wl["w_gate"], wl["w_up"], name="ffn_up")
    x2 =_matmul(act, wl["w_down"], mode="nn", out_dtype=F32, name="ffn_down", add=x1, tm=512, tn=512)
    saved = dict(x=x, h=h, p=p, cq_n=cq_n, ckv_n=ckv_n, kv=kv, qa=qa, ka=ka, oa=oa, lse_a=lse_a, qb=qb, kb=kb, vb=vb, ob=ob,
                 lse_b=lse_b, qc=qc, kc=kc, vc=vc, oc=oc, lse_c=lse_c, y=y, x1=x1, h2=h2, gate=gate, up=up, act=act)
    return x2, saved, wl


def _layer_bwd(dx2, dx2b, sv, wl, sm, tabs, bias, t, send_ffn, send_attn):
    s, d = dx2.shape
    (cos_a, sin_a), (cos_b, sin_b), (cos_c, sin_c) = tabs
    gw, gs = {}, {}
    dgate, dup = _ffn_down_dx(dx2b, wl["w_down"], sv["gate"], sv["up"], name="ffn_down_dx")
    gw["w_down"] = _matmul(sv["act"], dx2b, mode="tn", out_dtype=BF16, name="ffn_down_dw", tm=512, tn=512)
    dh2 = _ffn_up_dx(dgate, dup, wl["w_gate"], wl["w_up"], name="ffn_up_dx")
    gw["w_gate"] = _matmul(sv["h2"], dgate, mode="tn", out_dtype=BF16, name="ffn_gate_dw", tm=512, col_shards=True)
    gw["w_up"] = _matmul(sv["h2"], dup, mode="tn", out_dtype=BF16, name="ffn_up_dw", tm=512, col_shards=True)
    dx1, gs["ffn_norm"], dx1b = _norm_bwd(sv["x1"], sm["ffn_norm"], dh2, wb=d, cb=0, nb=1, shared_gain=True,
                                          out_dtype=F32, name="ffn_norm_bwd", add=dx2, bf16_copy=True)
    dy = _matmul(dx1b, wl["w_out"], mode="nt", out_dtype=F32, name="out_proj_dx", tm=512, tn=512)
    gw["w_out"] = _matmul(sv["y"], dx1b, mode="tn", out_dtype=BF16, name="out_proj_dw", tm=512, tn=512)
    token = send_ffn(gw)
    g_out = _tie(sm["out_norm"], token)
    ga, gb, gc = g_out[:, :A_WIDTH], g_out[:, A_WIDTH:A_WIDTH + B_WIDTH], g_out[:, A_WIDTH + B_WIDTH:]
    dya, dyb, dyc = dy[:, :A_WIDTH], dy[:, A_WIDTH:A_WIDTH + B_WIDTH], dy[:, A_WIDTH + B_WIDTH:]
    doa, dga = _norm_bwd(sv["oa"], ga, dya, wb=A_WIDTH, cb=0, nb=1, shared_gain=True, out_dtype=F32, name="out_norm_a_bwd")
    dob, dgb = _norm_bwd(sv["ob"], gb, dyb, wb=B_WIDTH, cb=0, nb=1, shared_gain=True, out_dtype=F32, name="out_norm_b_bwd")
    doc, dgc = _norm_bwd(sv["oc"], gc, dyc, wb=C_WIDTH, cb=0, nb=1, shared_gain=True, out_dtype=F32, name="out_norm_c_bwd")
    gs["out_norm"] = jnp.concatenate([dga, dgb, dgc], axis=1)
    p = sv["p"]
    dqc, dkc, dvc = _flash_bwd(sv["qc"], sv["kc"], sv["vc"], sv["oc"], doc, sv["lse_c"], None, hkv=C_KV_HEADS,
                               g=C_GROUP, dqk=LANE, q_cb=0, k_cb=0, v_cb=0, v_step=1, scale=HEAD_DIM ** -0.5,
                               tq=_pick(s, 2 * t), band=None, name="c_flash_bwd")
    dqn = _rope(dqc, cos_c, sin_c, tw=LANE, cb=0, nb=C_HEADS, half=HEAD_DIM // 4, sign=-1, out_dtype=F32, name="c_rope_q_bwd")
    dkn = _rope(dkc, cos_c, sin_c, tw=LANE, cb=0, nb=C_KV_HEADS, half=HEAD_DIM // 4, sign=-1, out_dtype=F32, name="c_rope_k_bwd")
    dpcq, gs["c_q_norm"] = _norm_bwd(p, sm["c_q_norm"], dqn, wb=LANE, cb=PB_CQH, nb=C_HEADS, shared_gain=True,
                                     out_dtype=BF16, name="c_q_norm_bwd")
    dpck, gs["c_k_norm"] = _norm_bwd(p, sm["c_k_norm"], dkn, wb=LANE, cb=PB_CKH, nb=C_KV_HEADS, shared_gain=True,
                                     out_dtype=BF16, name="c_k_norm_bwd")
    table, band = bias
    dqb, dkb, dvb = _flash_bwd(sv["qb"], sv["kb"], sv["vb"], sv["ob"], dob, sv["lse_b"], table, hkv=B_HEADS, g=1,
                               dqk=LANE, q_cb=0, k_cb=0, v_cb=0, v_step=1, scale=HEAD_DIM ** -0.5, tq=t, band=band,
                               name="b_flash_bwd")
    dpbq = _rope(dqb, cos_b, sin_b, tw=LANE, cb=0, nb=B_HEADS, half=HEAD_DIM // 2, sign=-1, out_dtype=BF16, name="b_rope_q_bwd")
    dpbk = _rope(dkb, cos_b, sin_b, tw=LANE, cb=0, nb=B_HEADS, half=HEAD_DIM // 2, sign=-1, out_dtype=BF16, name="b_rope_k_bwd")
    dqa, dka, dva = _flash_bwd(sv["qa"], sv["ka"], sv["kv"], sv["oa"], doa, sv["lse_a"], None, hkv=A_HEADS, g=1,
                               dqk=A_QK, q_cb=0, k_cb=0, v_cb=1, v_step=2, scale=(A_NOPE + A_ROPE) ** -0.5,
                               tq=_pick(s, 2 * t), band=None, name="a_flash_bwd")
    dqa_raw = _rope(dqa, cos_a, sin_a, tw=A_QK, cb=0, nb=A_HEADS, half=A_ROPE // 2, sign=-1, out_dtype=BF16, name="a_rope_q_bwd")
    dkv, dkr = _latent_keys_bwd(dka, dva, cos_a, sin_a, name="a_keys_bwd")
    dckv_n = _matmul(dkv, wl["ukv"], mode="nt", out_dtype=F32, name="a_ukv_dx", tm=1024, tn=512)
    gw["ukv"] = _matmul(sv["ckv_n"], dkv, mode="tn", out_dtype=BF16, name="a_ukv_dw", tm=512, tn=1024)
    dcq_n = _matmul(dqa_raw, wl["uq"], mode="nt", out_dtype=F32, name="a_uq_dx", tm=1024, tn=512)
    gw["uq"] = _matmul(sv["cq_n"], dqa_raw, mode="tn", out_dtype=BF16, name="a_uq_dw", tm=512, tn=1024)
    dcq, gs["a_q_norm"] = _norm_bwd(p, sm["a_q_norm"], dcq_n, wb=A_Q_RANK, cb=0, nb=1, shared_gain=True, out_dtype=BF16,
                                    name="a_q_norm_bwd")
    dckv, gs["a_kv_norm"] = _norm_bwd(p, sm["a_kv_norm"], dckv_n, wb=A_KV_RANK, cb=1, nb=1, shared_gain=True,
                                      out_dtype=BF16, name="a_kv_norm_bwd")
    dp = jnp.concatenate([dcq, dckv, dkr, jnp.zeros((s, A_PAD - (PB_KR + 1) * LANE), BF16), dpbq, dpbk,
                          dvb.astype(BF16), dpcq, dpck, dvc.astype(BF16)], axis=1)
    gw["w_all"] = _matmul(dp, sv["h"], mode="tn", out_dtype=BF16, name="in_proj_dw", tm=640, tn=512)
    token = send_attn(gw)
    dh = _matmul(dp, wl["w_all"], mode="nn", out_dtype=F32, name="in_proj_dx", tm=512, tn=512, after=token)
    dx, gs["attn_norm"], dxb = _norm_bwd(sv["x"], sm["attn_norm"], dh, wb=d, cb=0, nb=1, shared_gain=True,
                                         out_dtype=F32, name="attn_norm_bwd", add=dx1, bf16_copy=True)
    return dx, dxb, gs, token


def _pack_small(vals):
    flat = jnp.concatenate([vals[n].reshape(-1).astype(F32) for n in _SMALL])
    tile = SUBLANE * LANE
    padded = -(-flat.shape[0] // tile) * tile
    return jnp.pad(flat, (0, padded - flat.shape[0])).reshape(padded // LANE, LANE)


def _unpack_small(packed, like):
    flat = packed.reshape(-1)
    out, off = {}, 0
    for n in _SMALL:
        size = math.prod(like[n].shape)
        out[n] = flat[off:off + size].reshape(like[n].shape)
        off += size
    return out


def kernel(x, attn_norm, w_in, a_q_norm, a_w_uq, a_kv_norm, a_w_ukv, c_q_norm, c_k_norm, out_norm, w_out, ffn_norm, w_gate, w_up, w_down, final_norm, loss_target, m_attn_norm, m_w_in, m_a_q_norm, m_a_w_uq, m_a_kv_norm, m_a_w_ukv, m_c_q_norm, m_c_k_norm, m_out_norm, m_w_out, m_ffn_norm, m_w_gate, m_w_up, m_w_down, m_final_norm, v_attn_norm, v_w_in, v_a_q_norm, v_a_w_uq, v_a_kv_norm, v_a_w_ukv, v_c_q_norm, v_c_k_norm, v_out_norm, v_w_out, v_ffn_norm, v_w_gate, v_w_up, v_w_down, v_final_norm):
    w = dict(attn_norm=attn_norm, w_in=w_in, a_q_norm=a_q_norm, a_w_uq=a_w_uq, a_kv_norm=a_kv_norm, a_w_ukv=a_w_ukv,
             c_q_norm=c_q_norm, c_k_norm=c_k_norm, out_norm=out_norm, w_out=w_out, ffn_norm=ffn_norm, w_gate=w_gate,
             w_up=w_up, w_down=w_down, final_norm=final_norm)
    m = dict(attn_norm=m_attn_norm, w_in=m_w_in, a_q_norm=m_a_q_norm, a_w_uq=m_a_w_uq, a_kv_norm=m_a_kv_norm,
             a_w_ukv=m_a_w_ukv, c_q_norm=m_c_q_norm, c_k_norm=m_c_k_norm, out_norm=m_out_norm, w_out=m_w_out,
             ffn_norm=m_ffn_norm, w_gate=m_w_gate, w_up=m_w_up, w_down=m_w_down, final_norm=m_final_norm)
    v = dict(attn_norm=v_attn_norm, w_in=v_w_in, a_q_norm=v_a_q_norm, a_w_uq=v_a_w_uq, a_kv_norm=v_a_kv_norm,
             a_w_ukv=v_a_w_ukv, c_q_norm=v_c_q_norm, c_k_norm=v_c_k_norm, out_norm=v_out_norm, w_out=v_w_out,
             ffn_norm=v_ffn_norm, w_gate=v_w_gate, w_up=v_w_up, w_down=v_w_down, final_norm=v_final_norm)
    _, s, d = x.shape
    depth = attn_norm.shape[0]

    def as_stored(a, n):
        return jnp.swapaxes(a, 1, 2) if n == "w_in" else a
    t = _pick(s, 512)

    me = (2 * lax.axis_index("x") + lax.axis_index("y")).astype(jnp.int32).reshape(1)

    gathers, after = {}, me
    for l in range(depth):
        for group, names in (("attn", _ATTN), ("ffn", _FFN)):
            bufs = [_cast_to_slot(as_stored(w[n], n), me, layer=l, name=f"cast_{n}")
                    for n in names]
            send_sems, recv_sems, bufs, _, after = _exchange_start(bufs, None, after, kind="gather",
                                                                   name=f"gather_start_{group}{l}")
            gathers[group, l] = (send_sems, recv_sems, bufs)
    all_started = after

    def gathered(group, l, after):
        send_sems, recv_sems, bufs = gathers[group, l]
        return _exchange_wait(send_sems, recv_sems, bufs, None, after, kind="gather", name=f"gather_wait_{group}{l}")

    tabs = _rope_tables(s)
    bias = _band_table(t, s)

    xs = x.reshape(s, d)
    saved, wls, sms = [], [], []
    for l in range(depth):
        wl = _assemble_attn(gathered("attn", l, all_started if l == 0 else xs))
        sm = {n: w[n][l][None, :] for n in _SMALL if n != "final_norm"}
        xs, sv, wl = _layer_fwd(xs, wl, lambda after, l=l: _assemble_ffn(gathered("ffn", l, after)), sm, tabs, bias, t)
        saved.append(sv)
        wls.append(wl)
        sms.append(sm)
    dx, g_final, loss_row, dxb = _final_loss(xs, final_norm[None, :], loss_target.reshape(s, d), name="final_loss")
    loss = lax.psum(loss_row[0, 0], ("x", "y", "c"))

    sends = {}

    def send(group, l, srcs, after):
        lands = [lax.empty((3,) + a.shape[1:], BF16) for a in srcs]
        send_sems, recv_sems, srcs, lands, token = _exchange_start(srcs, lands, after, kind="scatter",
                                                                   name=f"scatter_start_{group}{l}")
        sends[group, l] = (send_sems, recv_sems, srcs, lands)
        return token

    gs_layers, token = [None] * depth, all_started
    for l in reversed(range(depth)):
        dx, dxb, gs_layers[l], token = _layer_bwd(
            dx, dxb, saved[l], wls[l], sms[l], tabs, bias, t,
            lambda gw, l=l, tk=token: send("ffn", l, _split_ffn_grads(gw), tk),
            lambda gw, l=l: send("attn", l, _split_attn_grads(gw), dx))
    grad_x = dx.reshape(x.shape)

    srcs, lands = {}, {}

    def arrive(key, after):
        send_sems, recv_sems, s_bufs, l_bufs = sends[key]
        got = _exchange_wait(send_sems, recv_sems, s_bufs, l_bufs, after, kind="scatter",
                             name=f"scatter_wait_{key[0]}{key[1]}")
        for k, n in enumerate(_ATTN if key[0] == "attn" else _FFN):
            srcs[n, key[1]], lands[n, key[1]] = got[k], got[len(s_bufs) + k]

    def summed(names):
        return [_sum_parts([srcs[n, l] for l in range(depth)], [lands[n, l] for l in range(depth)], me, name="sum_" + n)
                for n in names]

    last = ("attn", 0)
    for key in sends:
        if key != last:
            arrive(key, token)
    sums_ffn = summed(_FFN)
    swap = _exchange_start(sums_ffn, [lax.empty(a.shape, F32) for a in sums_ffn], token, kind="swap",
                           name="swap_start_ffn")
    arrive(last, swap[4])
    sums_attn = summed(_ATTN)
    sib_attn = list(_sibling_exchange(sums_attn, name="swap_core_sums_attn"))
    swapped = _exchange_wait(swap[0], swap[1], swap[2], swap[3], sib_attn[0], kind="swap", name="swap_wait_ffn")
    mine_of = dict(zip(_FFN + _ATTN, swapped[:len(_FFN)] + sums_attn))
    other_of = dict(zip(_FFN + _ATTN, swapped[len(_FFN):] + sib_attn))
    grads, deltas, new_m, new_v = {}, {}, {}, {}
    for n in _BIG:
        res = _adamw(mine_of[n], other_of[n], as_stored(w[n], n), as_stored(m[n], n), as_stored(v[n], n), name="adamw_" + n)
        grads[n], deltas[n], new_m[n], new_v[n] = [as_stored(r, n) for r in res]

    gsm = {n: jnp.stack([gs_layers[l][n][0] for l in range(depth)]) for n in _SMALL if n != "final_norm"}
    gsm["final_norm"] = g_final[0]
    packed = _pack_small(gsm)
    everyone = _all_gather_small(packed, name="gather_gain_grads").reshape(N_DEV, packed.shape[0], LANE)
    res = _small_adamw(everyone, _pack_small(w), _pack_small(m), _pack_small(v), name="adamw_gains")
    for dst, r in zip((grads, deltas, new_m, new_v), res):
        dst.update(_unpack_small(r, w))

    return (loss, grad_x, *[grads[n] for n in _WEIGHTS], *[deltas[n] for n in _WEIGHTS],
            *[new_m[n] for n in _WEIGHTS], *[new_v[n] for n in _WEIGHTS])
```

```python
import functools
import math

import jax
import jax.numpy as jnp
import numpy as np
from jax import lax
from jax.experimental import pallas as pl
from jax.experimental.pallas import tpu as pltpu

F32 = jnp.float32
BF16 = jnp.bfloat16
MESH = pl.DeviceIdType.MESH

HEAD_DIM = 128
ROPE_THETA = 10000.0
GRID_W = 64
EPS = 1e-6
NEG = -1e30
A_HEADS, A_Q_RANK, A_KV_RANK, A_NOPE, A_ROPE, A_V = 4, 512, 512, 128, 64, 128
B_HEADS = 6
B_PATTERNS = ((128, 1), (512, 4), (2048, 16))
C_HEADS, C_KV_HEADS = 6, 2
C_GROUP = C_HEADS // C_KV_HEADS
A_WIDTH, B_WIDTH, C_WIDTH = A_HEADS * A_V, B_HEADS * HEAD_DIM, C_HEADS * HEAD_DIM
IN_A = A_Q_RANK + A_KV_RANK + A_ROPE
IN_B = 3 * B_WIDTH
IN_C = C_WIDTH + 2 * C_KV_HEADS * HEAD_DIM
ADAM_LR, ADAM_B1, ADAM_B2, ADAM_EPS, ADAM_WD, ADAM_STEP = 0.001, 0.9, 0.999, 1e-08, 0.01, 10

LANE = 128
SUBLANE = 8
VMEM_BYTES_V7X = 64 * 1024 * 1024
VMEM_LIMIT_CAP = VMEM_BYTES_V7X - 8 * 1024 * 1024
N_CHIPS = 4
N_DEV = 8

A_PAD = 12 * LANE
PB_CQ, PB_CKV, PB_KR = 0, 4, 8
PB_BQ, PB_BK, PB_BV = 12, 18, 24
PB_CQH, PB_CKH, PB_CVH = 30, 36, 38
NP = 40 * LANE
A_QK = 2 * LANE


def _pick(n, cap, mult=LANE):
    if n <= cap:
        return n
    t = cap - cap % mult
    while t >= mult:
        if n % t == 0:
            return t
        t -= mult
    return n


def _rows_for(width_bytes, n_rows, target=2 * 1024 * 1024):
    return _pick(n_rows, max(SUBLANE, target // max(width_bytes, 1)), SUBLANE)


def _tile2(rows, cols, target):
    tc = _pick(cols, 4 * LANE)
    if tc < 4 * LANE:
        tc = cols
    fits = [t for t in range(SUBLANE, rows + 1, SUBLANE) if rows % t == 0] or [rows]
    return min(fits, key=lambda t: abs(math.log(t * tc * 4 / target))), tc


def _params(est_bytes):
    limit = int(min(max(est_bytes + (4 << 20), 32 << 20), VMEM_LIMIT_CAP))
    return pltpu.CompilerParams(vmem_limit_bytes=limit)


def _isz(x):
    return jnp.dtype(x.dtype).itemsize


def _hbm(shape, dtype):
    return pltpu.HBM(shape, dtype)


def _pin(*arrays):
    return [pltpu.with_memory_space_constraint(a, pltpu.HBM) for a in arrays]


_DIMS = {"nn": (((1,), (0,)), ((), ())), "nt": (((1,), (1,)), ((), ())), "tn": (((0,), (0,)), ((), ()))}


def _matmul(a, b, *, mode, out_dtype, name, add=None, tm=512, tn=512, col_shards=False, after=None):
    if mode == "tn":
        (k, m), (k2, n) = a.shape, b.shape
    elif mode == "nt":
        (m, k), (n, k2) = a.shape, b.shape
    else:
        (m, k), (k2, n) = a.shape, b.shape
    assert k == k2, (a.shape, b.shape, mode)
    tm, tn = _pick(m, tm), (n // N_CHIPS if col_shards else _pick(n, tn))
    a_spec = pl.BlockSpec((k, tm), lambda i, j: (0, i)) if mode == "tn" else pl.BlockSpec((tm, k), lambda i, j: (i, 0))
    b_spec = pl.BlockSpec((tn, k), lambda i, j: (j, 0)) if mode == "nt" else pl.BlockSpec((k, tn), lambda i, j: (0, j))
    o_spec = pl.BlockSpec((None, tm, tn), lambda i, j: (j, i, 0)) if col_shards else pl.BlockSpec((tm, tn), lambda i, j: (i, j))
    dims = _DIMS[mode]

    def body(*refs):
        a_ref, b_ref, o_ref = refs[0], refs[1], refs[-1]
        acc = lax.dot_general(a_ref[...].astype(BF16), b_ref[...].astype(BF16), dims, preferred_element_type=F32)
        if add is not None:
            acc = acc + refs[2][...].astype(F32)
        o_ref[...] = acc.astype(out_dtype)

    ins, specs = [a, b], [a_spec, b_spec]
    if add is not None:
        ins.append(add)
        specs.append(o_spec)
    if after is not None:
        ins.append(after)
        specs.append(pl.BlockSpec(memory_space=pl.ANY))
    est = 2 * (tm * k * _isz(a) + tn * k * _isz(b) + tm * tn * (jnp.dtype(out_dtype).itemsize + (4 if add is not None else 0)))
    est += (tm + tn) * k * 2 + 2 * tm * tn * 4
    return pl.pallas_call(
        body, name=name, grid=(m // tm, n // tn), in_specs=specs, out_specs=o_spec,
        out_shape=_hbm((N_CHIPS, m, tn) if col_shards else (m, n), out_dtype),
        compiler_params=_params(est),
    )(*_pin(*ins))


def _ffn_up(h, wg, wu, *, name):
    s, d = h.shape
    _, _, c = wg.shape
    tm = _pick(s, 512, SUBLANE)

    def body(h_ref, wg_ref, wu_ref, g_ref, u_ref, a_ref):
        hv = h_ref[...]
        gv = jnp.dot(hv, wg_ref[...], preferred_element_type=F32)
        uv = jnp.dot(hv, wu_ref[...], preferred_element_type=F32)
        g_ref[...] = gv.astype(BF16)
        u_ref[...] = uv.astype(BF16)
        a_ref[...] = (gv / (1.0 + jnp.exp(-gv)) * uv).astype(BF16)

    w_spec = pl.BlockSpec((None, d, c), lambda j, i: (j, 0, 0))
    o_spec = pl.BlockSpec((tm, c), lambda j, i: (i, j))
    est = 2 * (tm * d * 2 + 2 * d * c * 2 + tm * c * 10) + 4 * tm * c * 4
    return pl.pallas_call(
        body, name=name, grid=(N_CHIPS, s // tm), in_specs=[pl.BlockSpec((tm, d), lambda j, i: (i, 0)), w_spec, w_spec],
        out_specs=[o_spec, o_spec, o_spec],
        out_shape=[_hbm((s, N_CHIPS * c), BF16)] * 3,
        compiler_params=_params(est),
    )(*_pin(h, wg, wu))


def _ffn_down_dx(dx, w_down, gate, up, *, name):
    s, d = dx.shape
    f = w_down.shape[0]
    tm, tn = _pick(s, 512, SUBLANE), _pick(f, 512)

    def body(dx_ref, w_ref, g_ref, u_ref, dg_ref, du_ref):
        dact = lax.dot_general(dx_ref[...], w_ref[...], _DIMS["nt"], preferred_element_type=F32)
        gv, uv = g_ref[...].astype(F32), u_ref[...].astype(F32)
        sig = 1.0 / (1.0 + jnp.exp(-gv))
        dg_ref[...] = (dact * uv * (sig * (1.0 + gv * (1.0 - sig)))).astype(BF16)
        du_ref[...] = (dact * (gv * sig)).astype(BF16)

    t_spec = pl.BlockSpec((tm, tn), lambda i, j: (i, j))
    est = 2 * (tm * d * 2 + tn * d * 2 + tm * tn * 12) + 6 * tm * tn * 4
    return pl.pallas_call(
        body, name=name, grid=(s // tm, f // tn),
        in_specs=[pl.BlockSpec((tm, d), lambda i, j: (i, 0)), pl.BlockSpec((tn, d), lambda i, j: (j, 0)), t_spec, t_spec],
        out_specs=[t_spec, t_spec], out_shape=[_hbm((s, f), BF16)] * 2, compiler_params=_params(est),
    )(*_pin(dx, w_down, gate, up))


def _ffn_up_dx(dgate, dup, wg, wu, *, name):
    s, f = dgate.shape
    _, d, c = wg.shape
    tm, tn = _pick(s, 1024, SUBLANE), _pick(d, 1024)
    nk = 2 * N_CHIPS

    def body(dg_ref, du_ref, wg_ref, wu_ref, o_ref, acc):
        kk = pl.program_id(2)

        @pl.when(kk == 0)
        def _():
            acc[...] = jnp.zeros_like(acc)

        @pl.when(kk < N_CHIPS)
        def _():
            acc[...] += lax.dot_general(dg_ref[...], wg_ref[...], _DIMS["nt"], preferred_element_type=F32)

        @pl.when(kk >= N_CHIPS)
        def _():
            acc[...] += lax.dot_general(du_ref[...], wu_ref[...], _DIMS["nt"], preferred_element_type=F32)

        @pl.when(kk == nk - 1)
        def _():
            o_ref[...] = acc[...]

    last = N_CHIPS - 1
    est = 2 * (2 * tm * c * 2 + 2 * tn * c * 2 + tm * tn * 4) + 2 * tm * tn * 4
    return pl.pallas_call(
        body, name=name, grid=(s // tm, d // tn, nk),
        in_specs=[pl.BlockSpec((tm, c), lambda i, j, kk: (i, jnp.minimum(kk, last))),
                  pl.BlockSpec((tm, c), lambda i, j, kk: (i, jnp.maximum(kk - N_CHIPS, 0))),
                  pl.BlockSpec((None, tn, c), lambda i, j, kk: (jnp.minimum(kk, last), j, 0)),
                  pl.BlockSpec((None, tn, c), lambda i, j, kk: (jnp.maximum(kk - N_CHIPS, 0), j, 0))],
        out_specs=pl.BlockSpec((tm, tn), lambda i, j, kk: (i, j)),
        out_shape=_hbm((s, d), F32), scratch_shapes=[pltpu.VMEM((tm, tn), F32)],
        compiler_params=_params(est),
    )(*_pin(dgate, dup, wg, wu))


def _norm_fwd(x, gain, *, wb, cb, nb, shared_gain, out_dtype, name):
    s = x.shape[0]
    ts = _rows_for(wb * 4, s)

    def body(x_ref, g_ref, o_ref):
        xv = x_ref[...].astype(F32)
        r = lax.rsqrt(jnp.mean(xv * xv, axis=1, keepdims=True) + EPS)
        o_ref[...] = ((xv * r) * g_ref[...]).astype(out_dtype)

    return pl.pallas_call(
        body, name=name, grid=(nb, s // ts),
        in_specs=[pl.BlockSpec((ts, wb), lambda n, i: (i, cb + n)),
                  pl.BlockSpec((1, wb), (lambda n, i: (0, 0)) if shared_gain else (lambda n, i: (0, n)))],
        out_specs=pl.BlockSpec((ts, wb), lambda n, i: (i, n)),
        out_shape=_hbm((s, nb * wb), out_dtype), compiler_params=_params(6 * ts * wb * 4),
    )(*_pin(x), gain)


def _norm_bwd(x, gain, dy, *, wb, cb, nb, shared_gain, out_dtype, name, dy_cb=0, add=None, bf16_copy=False):
    s = x.shape[0]
    ts = _rows_for(wb * 4, s, target=1024 * 1024)
    gw = wb if shared_gain else nb * wb

    def body(*refs):
        refs = list(refs)
        dxb_ref = refs.pop() if bf16_copy else None
        if add is None:
            x_ref, g_ref, dy_ref, dx_ref, dg_ref = refs
        else:
            x_ref, g_ref, dy_ref, add_ref, dx_ref, dg_ref = refs
        n, i = pl.program_id(0), pl.program_id(1)
        xv = x_ref[...].astype(F32)
        dyv = dy_ref[...].astype(F32)
        r = lax.rsqrt(jnp.mean(xv * xv, axis=1, keepdims=True) + EPS)
        xh = xv * r
        dyg = dyv * g_ref[...]
        dx = r * (dyg - xh * jnp.mean(dyg * xh, axis=1, keepdims=True))
        if add is not None:
            dx = dx + add_ref[...]
        dx_ref[...] = dx.astype(out_dtype)
        if bf16_copy:
            dxb_ref[...] = dx.astype(BF16)
        first = jnp.logical_and(n == 0, i == 0) if shared_gain else (i == 0)

        @pl.when(first)
        def _():
            dg_ref[...] = jnp.zeros_like(dg_ref)

        dg_ref[...] += jnp.sum(dyv * xh, axis=0, keepdims=True)

    ins = [x, gain, dy]
    specs = [pl.BlockSpec((ts, wb), lambda n, i: (i, cb + n)),
             pl.BlockSpec((1, wb), (lambda n, i: (0, 0)) if shared_gain else (lambda n, i: (0, n))),
             pl.BlockSpec((ts, wb), lambda n, i: (i, dy_cb + n))]
    if add is not None:
        ins.append(add)
        specs.append(pl.BlockSpec((ts, wb), lambda n, i: (i, n)))
    out_specs = [pl.BlockSpec((ts, wb), lambda n, i: (i, n)),
                 pl.BlockSpec((1, wb), (lambda n, i: (0, 0)) if shared_gain else (lambda n, i: (0, n)))]
    out_shape = [_hbm((s, nb * wb), out_dtype), jax.ShapeDtypeStruct((1, gw), F32)]
    if bf16_copy:
        out_specs.append(out_specs[0])
        out_shape.append(_hbm((s, nb * wb), BF16))
    return pl.pallas_call(
        body, name=name, grid=(nb, s // ts), in_specs=specs, out_specs=out_specs, out_shape=out_shape,
        compiler_params=_params(14 * ts * wb * 4),
    )(*_pin(*ins))


def _swap_halves(x, half):
    if 2 * half == LANE:
        return pltpu.roll(x, half, axis=1)
    lane = lax.broadcasted_iota(jnp.int32, x.shape, 1)
    first = jnp.bitwise_and(lane, 2 * half - 1) < half
    return jnp.where(first, pltpu.roll(x, LANE - half, axis=1), pltpu.roll(x, half, axis=1))


def _rope(x, cos_t, sin_t, *, tw, cb, nb, half, sign, out_dtype, name):
    s = x.shape[0]
    ts = _rows_for(tw * 4, s)

    def body(x_ref, c_ref, s_ref, o_ref):
        for q in range(tw // LANE):
            sl = slice(q * LANE, (q + 1) * LANE)
            xv = x_ref[:, sl].astype(F32)
            sv = s_ref[:, sl]
            if sign < 0:
                sv = -sv
            o_ref[:, sl] = (xv * c_ref[:, sl] + _swap_halves(xv, half) * sv).astype(out_dtype)

    return pl.pallas_call(
        body, name=name, grid=(nb, s // ts),
        in_specs=[pl.BlockSpec((ts, tw), lambda n, i: (i, cb + n)),
                  pl.BlockSpec((ts, tw), lambda n, i: (i, 0)),
                  pl.BlockSpec((ts, tw), lambda n, i: (i, 0))],
        out_specs=pl.BlockSpec((ts, tw), lambda n, i: (i, n)),
        out_shape=_hbm((s, nb * tw), out_dtype), compiler_params=_params(10 * ts * tw * 4),
    )(*_pin(x), cos_t, sin_t)


def _latent_keys(kv, p, cos_t, sin_t, *, kr_cb, name):
    s = kv.shape[0]
    ts = _rows_for(A_QK * 4, s)

    def body(kv_ref, kr_ref, c_ref, s_ref, o_ref):
        o_ref[:, :LANE] = kv_ref[...].astype(BF16)
        x = kr_ref[...].astype(F32)
        o_ref[:, LANE:] = (x * c_ref[:, LANE:] + _swap_halves(x, A_ROPE // 2) * s_ref[:, LANE:]).astype(BF16)

    tab = pl.BlockSpec((ts, A_QK), lambda n, i: (i, 0))
    return pl.pallas_call(
        body, name=name, grid=(A_HEADS, s // ts),
        in_specs=[pl.BlockSpec((ts, LANE), lambda n, i: (i, 2 * n)), pl.BlockSpec((ts, LANE), lambda n, i: (i, kr_cb)),
                  tab, tab],
        out_specs=pl.BlockSpec((ts, A_QK), lambda n, i: (i, n)),
        out_shape=_hbm((s, A_HEADS * A_QK), BF16), compiler_params=_params(10 * ts * A_QK * 4),
    )(*_pin(kv, p), cos_t, sin_t)


def _latent_keys_bwd(dka, dva, cos_t, sin_t, *, name):
    s = dka.shape[0]
    ts = _rows_for(A_HEADS * A_QK * 4, s)

    def body(dka_ref, dva_ref, c_ref, s_ref, dkv_ref, dkr_ref):
        acc = jnp.zeros((ts, LANE), F32)
        for h in range(A_HEADS):
            dkv_ref[:, h * A_QK:h * A_QK + LANE] = dka_ref[:, h * A_QK:h * A_QK + LANE].astype(BF16)
            dkv_ref[:, h * A_QK + LANE:(h + 1) * A_QK] = dva_ref[:, h * LANE:(h + 1) * LANE].astype(BF16)
            y = dka_ref[:, h * A_QK + LANE:(h + 1) * A_QK]
            acc = acc + (y * c_ref[:, LANE:] - _swap_halves(y, A_ROPE // 2) * s_ref[:, LANE:])
        dkr_ref[...] = acc.astype(BF16)

    def rows(width):
        return pl.BlockSpec((ts, width), lambda i: (i, 0))

    return pl.pallas_call(
        body, name=name, grid=(s // ts,),
        in_specs=[rows(A_HEADS * A_QK), rows(A_HEADS * LANE), rows(A_QK), rows(A_QK)],
        out_specs=[rows(A_HEADS * A_QK), rows(LANE)],
        out_shape=[_hbm((s, A_HEADS * A_QK), BF16), _hbm((s, LANE), BF16)],
        compiler_params=_params(8 * ts * A_HEADS * A_QK * 4),
    )(*_pin(dka, dva), cos_t, sin_t)


def _cast_cols(x, *, cb, nb, name):
    s = x.shape[0]
    ts = _rows_for(LANE * 4, s)

    def body(x_ref, o_ref):
        o_ref[...] = x_ref[...].astype(BF16)

    return pl.pallas_call(
        body, name=name, grid=(nb, s // ts), in_specs=[pl.BlockSpec((ts, LANE), lambda n, i: (i, cb + n))],
        out_specs=pl.BlockSpec((ts, LANE), lambda n, i: (i, n)),
        out_shape=_hbm((s, nb * LANE), BF16), compiler_params=_params(4 * ts * LANE * 4),
    )(*_pin(x))


LOG2E = 1.4426950408889634
ATTN_ROW_CHUNK = 256


def _attn_window(i, rc, s, band):
    w, r = band
    start = jnp.clip(i * rc - r, 0, s - w)
    return pl.multiple_of(start, rc), pl.multiple_of((w - rc) - (i * rc - start), LANE)


def _flash_fwd(q, k, v, table, *, hkv, g, dqk, q_cb, k_cb, v_cb, v_step, scale, tq, band, name):
    s = q.shape[0]
    n = s // tq
    hq = hkv * g
    rc = min(tq, ATTN_ROW_CHUNK)
    w = s if band is None else band[0]

    def body(*refs):
        if band is None:
            q_ref, k_ref, v_ref, o_ref, lse_ref = refs
            kw, vw = k_ref[...], v_ref[...]
        else:
            q_ref, k_ref, v_ref, t_ref, o_ref, lse_ref = refs
        for c in range(tq // rc):
            rows = slice(c * rc, (c + 1) * rc)
            if band is not None:
                start, u = _attn_window(pl.program_id(1) * (tq // rc) + c, rc, s, band)
                kw, vw = k_ref[pl.ds(start, w), :], v_ref[pl.ds(start, w), :]
            sc = lax.dot_general(q_ref[rows, :], kw, _DIMS["nt"], preferred_element_type=F32) * (scale * LOG2E)
            if band is not None:
                sc = sc + t_ref[:, pl.ds(u, w)]
            m = jnp.max(sc, axis=1, keepdims=True)
            p = jnp.exp2(sc - m)
            l = jnp.sum(p, axis=1, keepdims=True)
            o_ref[rows, :] = jnp.dot(p.astype(BF16), vw, preferred_element_type=F32) / l
            lse_ref[0, rows, :] = jnp.broadcast_to(m + jnp.log2(l), (rc, LANE))

    ins = [q, k, v]
    specs = [pl.BlockSpec((tq, dqk), lambda h, i: (i, q_cb + h)),
             pl.BlockSpec((s, dqk), lambda h, i: (0, k_cb + h // g)),
             pl.BlockSpec((s, LANE), lambda h, i: (0, v_cb + v_step * (h // g)))]
    if band is not None:
        ins.append(table)
        specs.append(pl.BlockSpec(table.shape, lambda h, i: (0, 0)))
    est = 4 * s * (dqk + LANE) + 6 * rc * w * 4 + 8 * tq * LANE * 4 + (0 if band is None else 2 * table.size * 4)
    return pl.pallas_call(
        body, name=name, grid=(hq, n), in_specs=specs,
        out_specs=[pl.BlockSpec((tq, LANE), lambda h, i: (i, h)), pl.BlockSpec((1, tq, LANE), lambda h, i: (h, i, 0))],
        out_shape=[_hbm((s, hq * LANE), F32), _hbm((hq, s, LANE), F32)],
        compiler_params=_params(est),
    )(*_pin(*ins))


def _flash_bwd(q, k, v, o, do, lse, table, *, hkv, g, dqk, q_cb, k_cb, v_cb, v_step, scale, tq, band, name):
    s = q.shape[0]
    n = s // tq
    hq = hkv * g
    rc = min(tq, ATTN_ROW_CHUNK)
    w = s if band is None else band[0]

    def body(*refs):
        if band is None:
            q_ref, k_ref, v_ref, o_ref, do_ref, lse_ref, dq_ref, dk_ref, dv_ref = refs
            keys = slice(None)
            kw, vw = k_ref[...], v_ref[...]
        else:
            q_ref, k_ref, v_ref, o_ref, do_ref, lse_ref, t_ref, dq_ref, dk_ref, dv_ref = refs
        h, i = pl.program_id(0), pl.program_id(1)

        @pl.when(jnp.logical_and(h % g == 0, i == 0))
        def _():
            dk_ref[...] = jnp.zeros_like(dk_ref)
            dv_ref[...] = jnp.zeros_like(dv_ref)

        for c in range(tq // rc):
            rows = slice(c * rc, (c + 1) * rc)
            if band is not None:
                start, u = _attn_window(i * (tq // rc) + c, rc, s, band)
                keys = pl.ds(start, w)
                kw, vw = k_ref[keys, :], v_ref[keys, :]
            qv = q_ref[rows, :]
            dof = do_ref[rows, :]
            dov = dof.astype(BF16)
            sc = lax.dot_general(qv, kw, _DIMS["nt"], preferred_element_type=F32) * (scale * LOG2E)
            if band is not None:
                sc = sc + t_ref[:, pl.ds(u, w)]
            p = jnp.exp2(sc - lse_ref[0, rows, 0:1])
            dp = lax.dot_general(dov, vw, _DIMS["nt"], preferred_element_type=F32)
            delta = jnp.sum(dof * o_ref[rows, :], axis=1, keepdims=True)
            ds = (p * (dp - delta) * scale).astype(BF16)
            dv_ref[keys, :] += lax.dot_general(p.astype(BF16), dov, _DIMS["tn"], preferred_element_type=F32)
            dk_ref[keys, :] += lax.dot_general(ds, qv, _DIMS["tn"], preferred_element_type=F32)
            dq_ref[rows, :] = jnp.dot(ds, kw, preferred_element_type=F32)

    ins = [q, k, v, o, do, lse]
    specs = [pl.BlockSpec((tq, dqk), lambda h, i: (i, q_cb + h)),
             pl.BlockSpec((s, dqk), lambda h, i: (0, k_cb + h // g)),
             pl.BlockSpec((s, LANE), lambda h, i: (0, v_cb + v_step * (h // g))),
             pl.BlockSpec((tq, LANE), lambda h, i: (i, h)),
             pl.BlockSpec((tq, LANE), lambda h, i: (i, h)),
             pl.BlockSpec((1, tq, LANE), lambda h, i: (h, i, 0))]
    if band is not None:
        ins.append(table)
        specs.append(pl.BlockSpec(table.shape, lambda h, i: (0, 0)))
    est = (4 + 8) * s * (dqk + LANE) + 10 * rc * w * 4 + 12 * tq * LANE * 4 + (0 if band is None else 2 * table.size * 4)
    return pl.pallas_call(
        body, name=name, grid=(hq, n), in_specs=specs,
        out_specs=[pl.BlockSpec((tq, dqk), lambda h, i: (i, h)),
                   pl.BlockSpec((s, dqk), lambda h, i: (0, h // g)),
                   pl.BlockSpec((s, LANE), lambda h, i: (0, h // g))],
        out_shape=[_hbm((s, hq * dqk), F32), _hbm((s, hkv * dqk), F32),
                   _hbm((s, hkv * LANE), F32)],
        compiler_params=_params(est),
    )(*_pin(*ins))


def _final_loss(x, gain, target, *, name):
    s, d = x.shape
    ts = _rows_for(d * 4, s, target=1024 * 1024)

    def body(x_ref, g_ref, t_ref, dx_ref, dg_ref, loss_ref, dxb_ref):
        i = pl.program_id(0)
        xv = x_ref[...]
        gv = g_ref[...]
        r = lax.rsqrt(jnp.mean(xv * xv, axis=1, keepdims=True) + EPS)
        xh = xv * r
        err = xh * gv - t_ref[...]
        dy = err / d
        dyg = dy * gv
        dx = r * (dyg - xh * jnp.mean(dyg * xh, axis=1, keepdims=True))
        dx_ref[...] = dx
        dxb_ref[...] = dx.astype(BF16)

        @pl.when(i == 0)
        def _():
            dg_ref[...] = jnp.zeros_like(dg_ref)
            loss_ref[...] = jnp.zeros_like(loss_ref)

        dg_ref[...] += jnp.sum(dy * xh, axis=0, keepdims=True)
        part = jnp.sum(jnp.mean(err * err, axis=1, keepdims=True), axis=0, keepdims=True)
        loss_ref[...] += jnp.broadcast_to(0.5 * part, (1, LANE))

    row = pl.BlockSpec((ts, d), lambda i: (i, 0))
    return pl.pallas_call(
        body, name=name, grid=(s // ts,),
        in_specs=[row, pl.BlockSpec((1, d), lambda i: (0, 0)), row],
        out_specs=[row, pl.BlockSpec((1, d), lambda i: (0, 0)), pl.BlockSpec((1, LANE), lambda i: (0, 0)), row],
        out_shape=[_hbm((s, d), F32), jax.ShapeDtypeStruct((1, d), F32),
                   jax.ShapeDtypeStruct((1, LANE), F32), _hbm((s, d), BF16)],
        compiler_params=_params(14 * ts * d * 4),
    )(*_pin(x), gain, *_pin(target))


def _cast_to_slot(x3d, me, *, layer, name):
    _, rows, c = x3d.shape
    tr, tc = _tile2(rows, c, 2 * 1024 * 1024)

    def body(me_ref, x_ref, o_ref):
        o_ref[...] = x_ref[...].astype(BF16)

    return pl.pallas_call(
        body, name=name,
        grid_spec=pltpu.PrefetchScalarGridSpec(
            num_scalar_prefetch=1, grid=(rows // tr, c // tc),
            in_specs=[pl.BlockSpec((None, tr, tc), lambda i, j, me_ref: (layer, i, j))],
            out_specs=pl.BlockSpec((None, tr, tc), lambda i, j, me_ref: (me_ref[0], i, j))),
        out_shape=_hbm((N_CHIPS, rows, c), BF16), compiler_params=_params(6 * tr * tc * 4),
    )(me, *_pin(x3d))


def _sum_parts(srcs, lands, me, *, name):
    depth = len(srcs)
    _, r, c = srcs[0].shape
    tr, tc = _tile2(r, c, 1024 * 1024)
    nt, nc = r // tr, c // tc

    def body(me_ref, *refs):
        o_ref = refs[-1]
        l = pl.program_id(0)
        for k in range(depth):
            @pl.when(l == k)
            def _(k=k):
                acc = refs[k][...].astype(F32)
                for p in range(3):
                    acc = acc + refs[depth + k][p].astype(F32)
                o_ref[...] = acc

    def tile_of(k):
        def f(l, i, j):
            return (jnp.where(l == k, i, jnp.where(l < k, 0, nt - 1)), jnp.where(l == k, j, jnp.where(l < k, 0, nc - 1)))
        return f

    in_specs = [pl.BlockSpec((None, tr, tc), functools.partial(lambda l, i, j, me_ref, f: (me_ref[0], *f(l, i, j)), f=tile_of(k)))
                for k in range(depth)]
    in_specs += [pl.BlockSpec((3, tr, tc), functools.partial(lambda l, i, j, me_ref, f: (0, *f(l, i, j)), f=tile_of(k)))
                 for k in range(depth)]
    return pl.pallas_call(
        body, name=name,
        grid_spec=pltpu.PrefetchScalarGridSpec(
            num_scalar_prefetch=1, grid=(depth, nt, nc), in_specs=in_specs,
            out_specs=pl.BlockSpec((tr, tc), lambda l, i, j, me_ref: (l * nt + i, j))),
        out_shape=_hbm((depth * r, c), F32), compiler_params=_params(depth * 10 * tr * tc * 4),
    )(me, *_pin(*srcs, *lands))


def _adamw_math(w, g, m, v):
    m2 = ADAM_B1 * m + (1.0 - ADAM_B1) * g
    v2 = ADAM_B2 * v + (1.0 - ADAM_B2) * (g * g)
    m_hat = m2 / (1.0 - ADAM_B1 ** ADAM_STEP)
    v_hat = v2 / (1.0 - ADAM_B2 ** ADAM_STEP)
    delta = -ADAM_LR * (m_hat / (jnp.sqrt(v_hat) + ADAM_EPS) + ADAM_WD * w)
    return delta, m2, v2


def _adamw(g_a, g_b, w, m, v, *, name):
    depth, r, c = w.shape
    tr, tc = _tile2(r, c, 512 * 1024)
    nt = r // tr

    def body(a_ref, b_ref, w_ref, m_ref, v_ref, g_out, d_out, m_out, v_out):
        gv = a_ref[...] + b_ref[...]
        delta, m2, v2 = _adamw_math(w_ref[...], gv, m_ref[...], v_ref[...])
        g_out[...] = gv
        d_out[...] = delta
        m_out[...] = m2
        v_out[...] = v2

    flat = pl.BlockSpec((tr, tc), lambda l, i, j: (l * nt + i, j))
    spec = pl.BlockSpec((None, tr, tc), lambda l, i, j: (l, i, j))
    return pl.pallas_call(
        body, name=name, grid=(depth, nt, c // tc), in_specs=[flat, flat, spec, spec, spec], out_specs=[spec] * 4,
        out_shape=[_hbm((depth, r, c), F32)] * 4, compiler_params=_params(22 * tr * tc * 4),
    )(*_pin(g_a, g_b, w, m, v))


def _small_adamw(g_all, w, m, v, *, name):
    r, c = w.shape

    def body(ga_ref, w_ref, m_ref, v_ref, g_out, d_out, m_out, v_out):
        gv = ga_ref[0]
        for j in range(1, N_DEV):
            gv = gv + ga_ref[j]
        delta, m2, v2 = _adamw_math(w_ref[...], gv, m_ref[...], v_ref[...])
        g_out[...] = gv
        d_out[...] = delta
        m_out[...] = m2
        v_out[...] = v2

    return pl.pallas_call(body, name=name, out_shape=[jax.ShapeDtypeStruct((r, c), F32)] * 4)(g_all, w, m, v)


_ANY = pl.BlockSpec(memory_space=pl.ANY)


_HBM = pl.BlockSpec(memory_space=pltpu.HBM)
_SEM = pl.BlockSpec(memory_space=pltpu.SEMAPHORE)
_EFFECT = pltpu.SideEffectType.DATAFLOW_SIDE_EFFECTING


def _peer_chips():
    x, y = lax.axis_index("x"), lax.axis_index("y")
    return 2 * x + y, [(1 - x, y), (x, 1 - y), (1 - x, 1 - y)]


def _exchange_copy(srcs, lands, send_sems, recv_sems, k, p, kind):
    c = lax.axis_index("c")
    if kind == "swap":
        return pltpu.make_async_remote_copy(
            src_ref=srcs[k], dst_ref=lands[k], send_sem=send_sems.at[k], recv_sem=recv_sems.at[k],
            device_id=(lax.axis_index("x"), lax.axis_index("y"), 1 - c), device_id_type=MESH)
    me, peers = _peer_chips()
    px, py = peers[p]
    return pltpu.make_async_remote_copy(
        src_ref=srcs[k].at[2 * px + py] if kind == "scatter" else srcs[k].at[me],
        dst_ref=lands[k].at[p] if kind == "scatter" else lands[k].at[me],
        send_sem=send_sems.at[3 * k + p], recv_sem=recv_sems.at[3 * k + p],
        device_id=(px, py, c), device_id_type=MESH)


def _exchange_start(srcs, lands, after, *, kind, name):
    n = len(srcs)
    npeer = 1 if kind == "swap" else 3
    bufs = list(srcs) + (list(lands) if lands is not None else [])
    nb = len(bufs)

    def body(*refs):
        buf_refs, send_sems, recv_sems = refs[:nb], refs[nb + 1], refs[nb + 2]
        token = refs[-1]
        s_refs = buf_refs[:n]
        l_refs = buf_refs[n:] if lands is not None else s_refs
        for k in range(n):
            for p in range(npeer):
                _exchange_copy(s_refs, l_refs, send_sems, recv_sems, k, p, kind).start()
        token[...] = jnp.zeros_like(token)

    out = pl.pallas_call(
        body, name=name,
        out_shape=(pltpu.SemaphoreType.DMA((npeer * n,)), pltpu.SemaphoreType.DMA((npeer * n,)),
                   *[pltpu.HBM(b.shape, b.dtype) for b in bufs], jax.ShapeDtypeStruct((SUBLANE, LANE), F32)),
        in_specs=[_HBM] * nb + [_ANY],
        out_specs=(_SEM, _SEM, *[_HBM] * nb, pl.BlockSpec(memory_space=pltpu.VMEM)),
        input_output_aliases={i: 2 + i for i in range(nb)},
        compiler_params=pltpu.CompilerParams(has_side_effects=_EFFECT),
    )(*[pltpu.with_memory_space_constraint(b, pltpu.HBM) for b in bufs], after)
    send_sems, recv_sems = out[0], out[1]
    thru = out[2:2 + nb]
    return send_sems, recv_sems, list(thru[:n]), (list(thru[n:]) if lands is not None else None), out[-1]


def _exchange_wait(send_sems, recv_sems, srcs, lands, after, *, kind, name):
    n = len(srcs)
    npeer = 1 if kind == "swap" else 3
    bufs = list(srcs) + (list(lands) if lands is not None else [])
    nb = len(bufs)

    def body(*refs):
        buf_refs, send_sems_ref, recv_sems_ref = refs[:nb], refs[nb], refs[nb + 1]
        s_refs = buf_refs[:n]
        l_refs = buf_refs[n:] if lands is not None else s_refs
        for k in range(n):
            for p in range(npeer):
                cp = _exchange_copy(s_refs, l_refs, send_sems_ref, recv_sems_ref, k, p, kind)
                cp.wait_send()
                cp.wait_recv()

    out = pl.pallas_call(
        body, name=name, out_shape=tuple(pltpu.HBM(b.shape, b.dtype) for b in bufs),
        in_specs=[_HBM] * nb + [_SEM, _SEM, _ANY], out_specs=tuple([_HBM] * nb),
        input_output_aliases={i: i for i in range(nb)},
        compiler_params=pltpu.CompilerParams(has_side_effects=_EFFECT),
    )(*bufs, send_sems, recv_sems, after)
    return list(out)


def _sibling_exchange(srcs, *, name):
    n = len(srcs)

    def body(*refs):
        src, out = refs[:n], refs[n:2 * n]
        send_sems, recv_sems = refs[2 * n:]
        sibling = (lax.axis_index("x"), lax.axis_index("y"), 1 - lax.axis_index("c"))
        copies = [pltpu.make_async_remote_copy(src_ref=src[k], dst_ref=out[k], send_sem=send_sems.at[k],
                                               recv_sem=recv_sems.at[k], device_id=sibling, device_id_type=MESH)
                  for k in range(n)]
        for cp in copies:
            cp.start()
        for cp in copies:
            cp.wait_recv()
        for cp in copies:
            cp.wait_send()

    return pl.pallas_call(
        body, name=name, in_specs=[_ANY] * n, out_specs=[_ANY] * n,
        out_shape=[jax.ShapeDtypeStruct(a.shape, a.dtype) for a in srcs],
        scratch_shapes=[pltpu.SemaphoreType.DMA((n,)), pltpu.SemaphoreType.DMA((n,))],
    )(*srcs)


def _all_gather_small(block, *, name):
    m_per, ncol = block.shape

    def body(x_ref, out_ref, send_sems, recv_sems, local_sem):
        x, y, c = lax.axis_index("x"), lax.axis_index("y"), lax.axis_index("c")
        me, sibling = (x, y, c), (x, y, 1 - c)
        chips = [(1 - x, y), (x, 1 - y), (1 - x, 1 - y)]

        def rows(px, py, pc):
            return out_ref.at[pl.ds((4 * px + 2 * py + pc) * m_per, m_per), :]

        def copy(k, blk, to, src=None):
            return pltpu.make_async_remote_copy(
                src_ref=rows(*blk) if src is None else src, dst_ref=rows(*blk),
                send_sem=send_sems.at[k], recv_sem=recv_sems.at[k], device_id=to, device_id_type=MESH)

        mine = pltpu.make_async_copy(x_ref, rows(*me), local_sem)
        mine.start()
        first = [copy(0, me, sibling, src=x_ref)]
        first += [copy(1 + j, me, (*chip, c), src=x_ref) for j, chip in enumerate(chips)]
        for cp in first:
            cp.start()
        passed = [copy(4 + j, (*chip, c), sibling) for j, chip in enumerate(chips)]
        for j, chip in enumerate(chips):
            copy(1 + j, (*chip, c), me).wait_recv()
            passed[j].start()
        copy(0, sibling, me).wait_recv()
        for j, chip in enumerate(chips):
            copy(4 + j, (*chip, 1 - c), me).wait_recv()
        for cp in first + passed:
            cp.wait_send()
        mine.wait()

    return pl.pallas_call(
        body, name=name, out_shape=jax.ShapeDtypeStruct((N_DEV * m_per, ncol), block.dtype),
        in_specs=[pl.BlockSpec(memory_space=pltpu.VMEM)], out_specs=pl.BlockSpec(memory_space=pltpu.VMEM),
        scratch_shapes=[pltpu.SemaphoreType.DMA((7,)), pltpu.SemaphoreType.DMA((7,)), pltpu.SemaphoreType.DMA],
    )(block)


def _rope_angles(pos, dim):
    inv = ROPE_THETA ** (-jnp.arange(0, dim, 2, dtype=F32) / dim)
    return pos.astype(F32)[:, None] * inv[None, :]


def _rope_tables(s):
    pos = jnp.arange(s, dtype=jnp.int32)
    rows = s // GRID_W
    row = jnp.repeat(jnp.arange(rows, dtype=jnp.int32), GRID_W)
    col = jnp.tile(jnp.arange(GRID_W, dtype=jnp.int32), rows)
    a1 = _rope_angles(pos, HEAD_DIM)
    aa = _rope_angles(pos, A_ROPE)
    ar = _rope_angles(row, HEAD_DIM // 2)
    ac = _rope_angles(col, HEAD_DIM // 2)
    one = jnp.ones((s, LANE), F32)
    zero = jnp.zeros((s, LANE), F32)
    pad = LANE - A_ROPE
    cos_a = jnp.concatenate([one, jnp.cos(aa), jnp.cos(aa), jnp.ones((s, pad), F32)], axis=1)
    sin_a = jnp.concatenate([zero, -jnp.sin(aa), jnp.sin(aa), jnp.zeros((s, pad), F32)], axis=1)
    cos_b = jnp.concatenate([jnp.cos(a1), jnp.cos(a1)], axis=1)
    sin_b = jnp.concatenate([-jnp.sin(a1), jnp.sin(a1)], axis=1)
    cos_c = jnp.concatenate([jnp.cos(ar), jnp.cos(ar), jnp.cos(ac), jnp.cos(ac)], axis=1)
    sin_c = jnp.concatenate([-jnp.sin(ar), jnp.sin(ar), -jnp.sin(ac), jnp.sin(ac)], axis=1)
    return (cos_a, sin_a), (cos_b, sin_b), (cos_c, sin_c)


def _band_table(rc, s):
    reach = max((win // (2 * d)) * d for win, d in B_PATTERNS)
    r = -(-reach // rc) * rc
    w = min(s, rc + 2 * r)
    j = jnp.arange(rc, dtype=jnp.int32)[:, None]
    x = jnp.arange(2 * w - rc, dtype=jnp.int32)[None, :]
    rel = x - (w - rc) - j
    mult = jnp.zeros(rel.shape, F32)
    for win, d in B_PATTERNS:
        mult = mult + jnp.logical_and(rel % d == 0, jnp.abs(rel) <= (win // (2 * d)) * d).astype(F32)
    return jnp.where(mult > 0, jnp.log2(jnp.maximum(mult, 1.0)), NEG), (w, r)


_BIG = ("w_in", "a_w_uq", "a_w_ukv", "w_out", "w_gate", "w_up", "w_down")
_SMALL = ("attn_norm", "a_q_norm", "a_kv_norm", "c_q_norm", "c_k_norm", "out_norm", "ffn_norm", "final_norm")
_WEIGHTS = ("attn_norm", "w_in", "a_q_norm", "a_w_uq", "a_kv_norm", "a_w_ukv", "c_q_norm", "c_k_norm", "out_norm",
            "w_out", "ffn_norm", "w_gate", "w_up", "w_down", "final_norm")


_ATTN = ("w_in", "a_w_uq", "a_w_ukv")
_FFN = ("w_out", "w_gate", "w_up", "w_down")


def _from_cols(a):
    return jnp.transpose(a, (1, 0, 2)).reshape(a.shape[1], N_CHIPS * a.shape[2])


def _from_rows(a):
    return a.reshape(N_CHIPS * a.shape[1], a.shape[2])


def _to_cols(a):
    return jnp.transpose(a.reshape(a.shape[0], N_CHIPS, a.shape[1] // N_CHIPS), (1, 0, 2))


def _to_rows(a):
    return a.reshape(N_CHIPS, a.shape[0] // N_CHIPS, a.shape[1])


def _assemble_attn(gw):
    w_in_t, uq, ukv = _from_rows(gw[0]), _from_cols(gw[1]), _from_cols(gw[2])
    d = w_in_t.shape[1]
    w_all = jnp.concatenate([w_in_t[:IN_A], jnp.zeros((A_PAD - IN_A, d), BF16), w_in_t[IN_A:]], axis=0)
    uq = uq.reshape(A_Q_RANK, A_HEADS, A_NOPE + A_ROPE)
    uq = jnp.pad(uq, ((0, 0), (0, 0), (0, A_QK - A_NOPE - A_ROPE))).reshape(A_Q_RANK, A_HEADS * A_QK)
    return dict(w_all=w_all, uq=uq, ukv=ukv)


def _assemble_ffn(gw):
    return dict(w_out=_from_rows(gw[0]), w_gate=gw[1], w_up=gw[2], w_down=_from_rows(gw[3]))


def _split_attn_grads(gl):
    w_all = gl["w_all"]
    w_in_t = jnp.concatenate([w_all[:IN_A], w_all[A_PAD:]], axis=0)
    uq = gl["uq"].reshape(A_Q_RANK, A_HEADS, A_QK)[:, :, :A_NOPE + A_ROPE].reshape(A_Q_RANK, A_HEADS * (A_NOPE + A_ROPE))
    return [_to_rows(w_in_t), _to_cols(uq), _to_cols(gl["ukv"])]


def _split_ffn_grads(gl):
    return [_to_rows(gl["w_out"]), gl["w_gate"], gl["w_up"], _to_rows(gl["w_down"])]


def _tie(a, token):
    return a + token[0:1, 0:1]


def _layer_fwd(x, wl, ffn_weights, sm, tabs, bias, t):
    s = x.shape[0]
    (cos_a, sin_a), (cos_b, sin_b), (cos_c, sin_c) = tabs
    h = _norm_fwd(x, sm["attn_norm"], wb=x.shape[1], cb=0, nb=1, shared_gain=True, out_dtype=BF16, name="attn_norm_fwd")
    p = _matmul(h, wl["w_all"], mode="nt", out_dtype=F32, name="in_proj", tm=1024, tn=640)
    cq_n = _norm_fwd(p, sm["a_q_norm"], wb=A_Q_RANK, cb=0, nb=1, shared_gain=True, out_dtype=BF16, name="a_q_norm_fwd")
    ckv_n = _norm_fwd(p, sm["a_kv_norm"], wb=A_KV_RANK, cb=1, nb=1, shared_gain=True, out_dtype=BF16, name="a_kv_norm_fwd")
    qa_raw = _matmul(cq_n, wl["uq"], mode="nn", out_dtype=F32, name="a_uq", tm=1024, tn=1024)
    kv = _matmul(ckv_n, wl["ukv"], mode="nn", out_dtype=BF16, name="a_ukv", tm=1024, tn=1024)
    qa =_rope(qa_raw, cos_a, sin_a, tw=A_QK, cb=0, nb=A_HEADS, half=A_ROPE // 2, sign=1, out_dtype=BF16, name="a_rope_q")
    ka = _latent_keys(kv, p, cos_a, sin_a, kr_cb=PB_KR, name="a_keys")
    oa, lse_a = _flash_fwd(qa, ka, kv, None, hkv=A_HEADS, g=1, dqk=A_QK, q_cb=0, k_cb=0, v_cb=1, v_step=2,
                           scale=(A_NOPE + A_ROPE) ** -0.5, tq=t, band=None, name="a_flash_fwd")
    table, band = bias
    qb = _rope(p, cos_b, sin_b, tw=LANE, cb=PB_BQ, nb=B_HEADS, half=HEAD_DIM // 2, sign=1, out_dtype=BF16, name="b_rope_q")
    kb = _rope(p, cos_b, sin_b, tw=LANE, cb=PB_BK, nb=B_HEADS, half=HEAD_DIM // 2, sign=1, out_dtype=BF16, name="b_rope_k")
    vb = _cast_cols(p, cb=PB_BV, nb=B_HEADS, name="b_cast_v")
    ob, lse_b = _flash_fwd(qb, kb, vb, table, hkv=B_HEADS, g=1, dqk=LANE, q_cb=0, k_cb=0, v_cb=0, v_step=1,
                           scale=HEAD_DIM ** -0.5, tq=t, band=band, name="b_flash_fwd")
    qn = _norm_fwd(p, sm["c_q_norm"], wb=LANE, cb=PB_CQH, nb=C_HEADS, shared_gain=True, out_dtype=F32, name="c_q_norm_fwd")
    kn = _norm_fwd(p, sm["c_k_norm"], wb=LANE, cb=PB_CKH, nb=C_KV_HEADS, shared_gain=True, out_dtype=F32, name="c_k_norm_fwd")
    qc = _rope(qn, cos_c, sin_c, tw=LANE, cb=0, nb=C_HEADS, half=HEAD_DIM // 4, sign=1, out_dtype=BF16, name="c_rope_q")
    kc = _rope(kn, cos_c, sin_c, tw=LANE, cb=0, nb=C_KV_HEADS, half=HEAD_DIM // 4, sign=1, out_dtype=BF16, name="c_rope_k")
    vc = _cast_cols(p, cb=PB_CVH, nb=C_KV_HEADS, name="c_cast_v")
    oc, lse_c = _flash_fwd(qc, kc, vc, None, hkv=C_KV_HEADS, g=C_GROUP, dqk=LANE, q_cb=0, k_cb=0, v_cb=0, v_step=1,
                           scale=HEAD_DIM ** -0.5, tq=t, band=None, name="c_flash_fwd")
    g_out = sm["out_norm"]
    ga, gb, gc = g_out[:, :A_WIDTH], g_out[:, A_WIDTH:A_WIDTH + B_WIDTH], g_out[:, A_WIDTH + B_WIDTH:]
    ya = _norm_fwd(oa, ga, wb=A_WIDTH, cb=0, nb=1, shared_gain=True, out_dtype=BF16, name="out_norm_a_fwd")
    yb = _norm_fwd(ob, gb, wb=B_WIDTH, cb=0, nb=1, shared_gain=True, out_dtype=BF16, name="out_norm_b_fwd")
    yc = _norm_fwd(oc, gc, wb=C_WIDTH, cb=0, nb=1, shared_gain=True, out_dtype=BF16, name="out_norm_c_fwd")
    y = jnp.concatenate([ya, yb, yc], axis=1)
    wl = {**wl, **ffn_weights(y)}
    x1 = _matmul(y, wl["w_out"], mode="nn", out_dtype=F32, name="out_proj", add=x, tm=1024, tn=512)
    h2 = _norm_fwd(x1, sm["ffn_norm"], wb=x.shape[1], cb=0, nb=1, shared_gain=True, out_dtype=BF16, name="ffn_norm_fwd")
    gate, up, act = _ffn_up(h2, wl["w_gate"], wl["w_up"], name="ffn_up")
    x2 =_matmul(act, wl["w_down"], mode="nn", out_dtype=F32, name="ffn_down", add=x1, tm=512, tn=512)
    saved = dict(x=x, h=h, p=p, cq_n=cq_n, ckv_n=ckv_n, kv=kv, qa=qa, ka=ka, oa=oa, lse_a=lse_a, qb=qb, kb=kb, vb=vb, ob=ob,
                 lse_b=lse_b, qc=qc, kc=kc, vc=vc, oc=oc, lse_c=lse_c, y=y, x1=x1, h2=h2, gate=gate, up=up, act=act)
    return x2, saved, wl


def _layer_bwd(dx2, dx2b, sv, wl, sm, tabs, bias, t, send_ffn, send_attn):
    s, d = dx2.shape
    (cos_a, sin_a), (cos_b, sin_b), (cos_c, sin_c) = tabs
    gw, gs = {}, {}
    dgate, dup = _ffn_down_dx(dx2b, wl["w_down"], sv["gate"], sv["up"], name="ffn_down_dx")
    gw["w_down"] = _matmul(sv["act"], dx2b, mode="tn", out_dtype=BF16, name="ffn_down_dw", tm=512, tn=512)
    dh2 = _ffn_up_dx(dgate, dup, wl["w_gate"], wl["w_up"], name="ffn_up_dx")
    gw["w_gate"] = _matmul(sv["h2"], dgate, mode="tn", out_dtype=BF16, name="ffn_gate_dw", tm=512, col_shards=True)
    gw["w_up"] = _matmul(sv["h2"], dup, mode="tn", out_dtype=BF16, name="ffn_up_dw", tm=512, col_shards=True)
    dx1, gs["ffn_norm"], dx1b = _norm_bwd(sv["x1"], sm["ffn_norm"], dh2, wb=d, cb=0, nb=1, shared_gain=True,
                                          out_dtype=F32, name="ffn_norm_bwd", add=dx2, bf16_copy=True)
    dy = _matmul(dx1b, wl["w_out"], mode="nt", out_dtype=F32, name="out_proj_dx", tm=512, tn=512)
    gw["w_out"] = _matmul(sv["y"], dx1b, mode="tn", out_dtype=BF16, name="out_proj_dw", tm=512, tn=512)
    token = send_ffn(gw)
    g_out = _tie(sm["out_norm"], token)
    ga, gb, gc = g_out[:, :A_WIDTH], g_out[:, A_WIDTH:A_WIDTH + B_WIDTH], g_out[:, A_WIDTH + B_WIDTH:]
    dya, dyb, dyc = dy[:, :A_WIDTH], dy[:, A_WIDTH:A_WIDTH + B_WIDTH], dy[:, A_WIDTH + B_WIDTH:]
    doa, dga = _norm_bwd(sv["oa"], ga, dya, wb=A_WIDTH, cb=0, nb=1, shared_gain=True, out_dtype=F32, name="out_norm_a_bwd")
    dob, dgb = _norm_bwd(sv["ob"], gb, dyb, wb=B_WIDTH, cb=0, nb=1, shared_gain=True, out_dtype=F32, name="out_norm_b_bwd")
    doc, dgc = _norm_bwd(sv["oc"], gc, dyc, wb=C_WIDTH, cb=0, nb=1, shared_gain=True, out_dtype=F32, name="out_norm_c_bwd")
    gs["out_norm"] = jnp.concatenate([dga, dgb, dgc], axis=1)
    p = sv["p"]
    dqc, dkc, dvc = _flash_bwd(sv["qc"], sv["kc"], sv["vc"], sv["oc"], doc, sv["lse_c"], None, hkv=C_KV_HEADS,
                               g=C_GROUP, dqk=LANE, q_cb=0, k_cb=0, v_cb=0, v_step=1, scale=HEAD_DIM ** -0.5,
                               tq=t, band=None, name="c_flash_bwd")
    dqn = _rope(dqc, cos_c, sin_c, tw=LANE, cb=0, nb=C_HEADS, half=HEAD_DIM // 4, sign=-1, out_dtype=F32, name="c_rope_q_bwd")
    dkn = _rope(dkc, cos_c, sin_c, tw=LANE, cb=0, nb=C_KV_HEADS, half=HEAD_DIM // 4, sign=-1, out_dtype=F32, name="c_rope_k_bwd")
    dpcq, gs["c_q_norm"] = _norm_bwd(p, sm["c_q_norm"], dqn, wb=LANE, cb=PB_CQH, nb=C_HEADS, shared_gain=True,
                                     out_dtype=BF16, name="c_q_norm_bwd")
    dpck, gs["c_k_norm"] = _norm_bwd(p, sm["c_k_norm"], dkn, wb=LANE, cb=PB_CKH, nb=C_KV_HEADS, shared_gain=True,
                                     out_dtype=BF16, name="c_k_norm_bwd")
    table, band = bias
    dqb, dkb, dvb = _flash_bwd(sv["qb"], sv["kb"], sv["vb"], sv["ob"], dob, sv["lse_b"], table, hkv=B_HEADS, g=1,
                               dqk=LANE, q_cb=0, k_cb=0, v_cb=0, v_step=1, scale=HEAD_DIM ** -0.5, tq=t, band=band,
                               name="b_flash_bwd")
    dpbq = _rope(dqb, cos_b, sin_b, tw=LANE, cb=0, nb=B_HEADS, half=HEAD_DIM // 2, sign=-1, out_dtype=BF16, name="b_rope_q_bwd")
    dpbk = _rope(dkb, cos_b, sin_b, tw=LANE, cb=0, nb=B_HEADS, half=HEAD_DIM // 2, sign=-1, out_dtype=BF16, name="b_rope_k_bwd")
    dqa, dka, dva = _flash_bwd(sv["qa"], sv["ka"], sv["kv"], sv["oa"], doa, sv["lse_a"], None, hkv=A_HEADS, g=1,
                               dqk=A_QK, q_cb=0, k_cb=0, v_cb=1, v_step=2, scale=(A_NOPE + A_ROPE) ** -0.5,
                               tq=t, band=None, name="a_flash_bwd")
    dqa_raw = _rope(dqa, cos_a, sin_a, tw=A_QK, cb=0, nb=A_HEADS, half=A_ROPE // 2, sign=-1, out_dtype=BF16, name="a_rope_q_bwd")
    dkv, dkr = _latent_keys_bwd(dka, dva, cos_a, sin_a, name="a_keys_bwd")
    dckv_n = _matmul(dkv, wl["ukv"], mode="nt", out_dtype=F32, name="a_ukv_dx", tm=1024, tn=512)
    gw["ukv"] = _matmul(sv["ckv_n"], dkv, mode="tn", out_dtype=BF16, name="a_ukv_dw", tm=512, tn=1024)
    dcq_n = _matmul(dqa_raw, wl["uq"], mode="nt", out_dtype=F32, name="a_uq_dx", tm=1024, tn=512)
    gw["uq"] = _matmul(sv["cq_n"], dqa_raw, mode="tn", out_dtype=BF16, name="a_uq_dw", tm=512, tn=1024)
    dcq, gs["a_q_norm"] = _norm_bwd(p, sm["a_q_norm"], dcq_n, wb=A_Q_RANK, cb=0, nb=1, shared_gain=True, out_dtype=BF16,
                                    name="a_q_norm_bwd")
    dckv, gs["a_kv_norm"] = _norm_bwd(p, sm["a_kv_norm"], dckv_n, wb=A_KV_RANK, cb=1, nb=1, shared_gain=True,
                                      out_dtype=BF16, name="a_kv_norm_bwd")
    dp = jnp.concatenate([dcq, dckv, dkr, jnp.zeros((s, A_PAD - (PB_KR + 1) * LANE), BF16), dpbq, dpbk,
                          dvb.astype(BF16), dpcq, dpck, dvc.astype(BF16)], axis=1)
    gw["w_all"] = _matmul(dp, sv["h"], mode="tn", out_dtype=BF16, name="in_proj_dw", tm=640, tn=512)
    token = send_attn(gw)
    dh = _matmul(dp, wl["w_all"], mode="nn", out_dtype=F32, name="in_proj_dx", tm=512, tn=512, after=token)
    dx, gs["attn_norm"], dxb = _norm_bwd(sv["x"], sm["attn_norm"], dh, wb=d, cb=0, nb=1, shared_gain=True,
                                         out_dtype=F32, name="attn_norm_bwd", add=dx1, bf16_copy=True)
    return dx, dxb, gs, token


def _pack_small(vals):
    flat = jnp.concatenate([vals[n].reshape(-1).astype(F32) for n in _SMALL])
    tile = SUBLANE * LANE
    padded = -(-flat.shape[0] // tile) * tile
    return jnp.pad(flat, (0, padded - flat.shape[0])).reshape(padded // LANE, LANE)


def _unpack_small(packed, like):
    flat = packed.reshape(-1)
    out, off = {}, 0
    for n in _SMALL:
        size = math.prod(like[n].shape)
        out[n] = flat[off:off + size].reshape(like[n].shape)
        off += size
    return out


def kernel(x, attn_norm, w_in, a_q_norm, a_w_uq, a_kv_norm, a_w_ukv, c_q_norm, c_k_norm, out_norm, w_out, ffn_norm, w_gate, w_up, w_down, final_norm, loss_target, m_attn_norm, m_w_in, m_a_q_norm, m_a_w_uq, m_a_kv_norm, m_a_w_ukv, m_c_q_norm, m_c_k_norm, m_out_norm, m_w_out, m_ffn_norm, m_w_gate, m_w_up, m_w_down, m_final_norm, v_attn_norm, v_w_in, v_a_q_norm, v_a_w_uq, v_a_kv_norm, v_a_w_ukv, v_c_q_norm, v_c_k_norm, v_out_norm, v_w_out, v_ffn_norm, v_w_gate, v_w_up, v_w_down, v_final_norm):
    w = dict(attn_norm=attn_norm, w_in=w_in, a_q_norm=a_q_norm, a_w_uq=a_w_uq, a_kv_norm=a_kv_norm, a_w_ukv=a_w_ukv,
             c_q_norm=c_q_norm, c_k_norm=c_k_norm, out_norm=out_norm, w_out=w_out, ffn_norm=ffn_norm, w_gate=w_gate,
             w_up=w_up, w_down=w_down, final_norm=final_norm)
    m = dict(attn_norm=m_attn_norm, w_in=m_w_in, a_q_norm=m_a_q_norm, a_w_uq=m_a_w_uq, a_kv_norm=m_a_kv_norm,
             a_w_ukv=m_a_w_ukv, c_q_norm=m_c_q_norm, c_k_norm=m_c_k_norm, out_norm=m_out_norm, w_out=m_w_out,
             ffn_norm=m_ffn_norm, w_gate=m_w_gate, w_up=m_w_up, w_down=m_w_down, final_norm=m_final_norm)
    v = dict(attn_norm=v_attn_norm, w_in=v_w_in, a_q_norm=v_a_q_norm, a_w_uq=v_a_w_uq, a_kv_norm=v_a_kv_norm,
             a_w_ukv=v_a_w_ukv, c_q_norm=v_c_q_norm, c_k_norm=v_c_k_norm, out_norm=v_out_norm, w_out=v_w_out,
             ffn_norm=v_ffn_norm, w_gate=v_w_gate, w_up=v_w_up, w_down=v_w_down, final_norm=v_final_norm)
    _, s, d = x.shape
    depth = attn_norm.shape[0]

    def as_stored(a, n):
        return jnp.swapaxes(a, 1, 2) if n == "w_in" else a
    t = _pick(s, 1024)

    me = (2 * lax.axis_index("x") + lax.axis_index("y")).astype(jnp.int32).reshape(1)

    gathers, after = {}, me
    for l in range(depth):
        for group, names in (("attn", _ATTN), ("ffn", _FFN)):
            bufs = [_cast_to_slot(as_stored(w[n], n), me, layer=l, name=f"cast_{n}")
                    for n in names]
            send_sems, recv_sems, bufs, _, after = _exchange_start(bufs, None, after, kind="gather",
                                                                   name=f"gather_start_{group}{l}")
            gathers[group, l] = (send_sems, recv_sems, bufs)
    all_started = after

    def gathered(group, l, after):
        send_sems, recv_sems, bufs = gathers[group, l]
        return _exchange_wait(send_sems, recv_sems, bufs, None, after, kind="gather", name=f"gather_wait_{group}{l}")

    tabs = _rope_tables(s)
    bias = _band_table(min(t, ATTN_ROW_CHUNK), s)

    xs = x.reshape(s, d)
    saved, wls, sms = [], [], []
    for l in range(depth):
        wl = _assemble_attn(gathered("attn", l, all_started if l == 0 else xs))
        sm = {n: w[n][l][None, :] for n in _SMALL if n != "final_norm"}
        xs, sv, wl = _layer_fwd(xs, wl, lambda after, l=l: _assemble_ffn(gathered("ffn", l, after)), sm, tabs, bias, t)
        saved.append(sv)
        wls.append(wl)
        sms.append(sm)
    dx, g_final, loss_row, dxb = _final_loss(xs, final_norm[None, :], loss_target.reshape(s, d), name="final_loss")
    loss = lax.psum(loss_row[0, 0], ("x", "y", "c"))

    sends = {}

    def send(group, l, srcs, after):
        lands = [lax.empty((3,) + a.shape[1:], BF16) for a in srcs]
        send_sems, recv_sems, srcs, lands, token = _exchange_start(srcs, lands, after, kind="scatter",
                                                                   name=f"scatter_start_{group}{l}")
        sends[group, l] = (send_sems, recv_sems, srcs, lands)
        return token

    gs_layers, token = [None] * depth, all_started
    for l in reversed(range(depth)):
        dx, dxb, gs_layers[l], token = _layer_bwd(
            dx, dxb, saved[l], wls[l], sms[l], tabs, bias, t,
            lambda gw, l=l, tk=token: send("ffn", l, _split_ffn_grads(gw), tk),
            lambda gw, l=l: send("attn", l, _split_attn_grads(gw), dx))
    grad_x = dx.reshape(x.shape)

    srcs, lands = {}, {}

    def arrive(key, after):
        send_sems, recv_sems, s_bufs, l_bufs = sends[key]
        got = _exchange_wait(send_sems, recv_sems, s_bufs, l_bufs, after, kind="scatter",
                             name=f"scatter_wait_{key[0]}{key[1]}")
        for k, n in enumerate(_ATTN if key[0] == "attn" else _FFN):
            srcs[n, key[1]], lands[n, key[1]] = got[k], got[len(s_bufs) + k]

    def summed(names):
        return [_sum_parts([srcs[n, l] for l in range(depth)], [lands[n, l] for l in range(depth)], me, name="sum_" + n)
                for n in names]

    last = ("attn", 0)
    for key in sends:
        if key != last:
            arrive(key, token)
    sums_ffn = summed(_FFN)
    swap = _exchange_start(sums_ffn, [lax.empty(a.shape, F32) for a in sums_ffn], token, kind="swap",
                           name="swap_start_ffn")
    arrive(last, swap[4])
    sums_attn = summed(_ATTN)
    sib_attn = list(_sibling_exchange(sums_attn, name="swap_core_sums_attn"))
    swapped = _exchange_wait(swap[0], swap[1], swap[2], swap[3], sib_attn[0], kind="swap", name="swap_wait_ffn")
    mine_of = dict(zip(_FFN + _ATTN, swapped[:len(_FFN)] + sums_attn))
    other_of = dict(zip(_FFN + _ATTN, swapped[len(_FFN):] + sib_attn))
    grads, deltas, new_m, new_v = {}, {}, {}, {}
    for n in _BIG:
        res = _adamw(mine_of[n], other_of[n], as_stored(w[n], n), as_stored(m[n], n), as_stored(v[n], n), name="adamw_" + n)
        grads[n], deltas[n], new_m[n], new_v[n] = [as_stored(r, n) for r in res]

    gsm = {n: jnp.stack([gs_layers[l][n][0] for l in range(depth)]) for n in _SMALL if n != "final_norm"}
    gsm["final_norm"] = g_final[0]
    packed = _pack_small(gsm)
    everyone = _all_gather_small(packed, name="gather_gain_grads").reshape(N_DEV, packed.shape[0], LANE)
    res = _small_adamw(everyone, _pack_small(w), _pack_small(m), _pack_small(v), name="adamw_gains")
    for dst, r in zip((grads, deltas, new_m, new_v), res):
        dst.update(_unpack_small(r, w))

    return (loss, grad_x, *[grads[n] for n in _WEIGHTS], *[deltas[n] for n in _WEIGHTS],
            *[new_m[n] for n in _WEIGHTS], *[new_v[n] for n in _WEIGHTS])
```

```python
import functools
import math

import jax
import jax.numpy as jnp
import numpy as np
from jax import lax
from jax.experimental import pallas as pl
from jax.experimental.pallas import tpu as pltpu

F32 = jnp.float32
BF16 = jnp.bfloat16
MESH = pl.DeviceIdType.MESH

HEAD_DIM = 128
ROPE_THETA = 10000.0
GRID_W = 64
EPS = 1e-6
NEG = -1e30
A_HEADS, A_Q_RANK, A_KV_RANK, A_NOPE, A_ROPE, A_V = 4, 512, 512, 128, 64, 128
B_HEADS = 6
B_PATTERNS = ((128, 1), (512, 4), (2048, 16))
C_HEADS, C_KV_HEADS = 6, 2
C_GROUP = C_HEADS // C_KV_HEADS
A_WIDTH, B_WIDTH, C_WIDTH = A_HEADS * A_V, B_HEADS * HEAD_DIM, C_HEADS * HEAD_DIM
IN_A = A_Q_RANK + A_KV_RANK + A_ROPE
IN_B = 3 * B_WIDTH
IN_C = C_WIDTH + 2 * C_KV_HEADS * HEAD_DIM
ADAM_LR, ADAM_B1, ADAM_B2, ADAM_EPS, ADAM_WD, ADAM_STEP = 0.001, 0.9, 0.999, 1e-08, 0.01, 10

LANE = 128
SUBLANE = 8
VMEM_BYTES_V7X = 64 * 1024 * 1024
VMEM_LIMIT_CAP = VMEM_BYTES_V7X - 8 * 1024 * 1024
N_CHIPS = 4
N_DEV = 8

A_PAD = 12 * LANE
PB_CQ, PB_CKV, PB_KR = 0, 4, 8
PB_BQ, PB_BK, PB_BV = 12, 18, 24
PB_CQH, PB_CKH, PB_CVH = 30, 36, 38
NP = 40 * LANE
A_QK = 2 * LANE


def _pick(n, cap, mult=LANE):
    if n <= cap:
        return n
    t = cap - cap % mult
    while t >= mult:
        if n % t == 0:
            return t
        t -= mult
    return n


def _rows_for(width_bytes, n_rows, target=2 * 1024 * 1024):
    return _pick(n_rows, max(SUBLANE, target // max(width_bytes, 1)), SUBLANE)


def _tile2(rows, cols, target):
    tc = _pick(cols, 4 * LANE)
    if tc < 4 * LANE:
        tc = cols
    fits = [t for t in range(SUBLANE, rows + 1, SUBLANE) if rows % t == 0] or [rows]
    return min(fits, key=lambda t: abs(math.log(t * tc * 4 / target))), tc


def _params(est_bytes):
    limit = int(min(max(est_bytes + (4 << 20), 32 << 20), VMEM_LIMIT_CAP))
    return pltpu.CompilerParams(vmem_limit_bytes=limit)


def _isz(x):
    return jnp.dtype(x.dtype).itemsize


def _hbm(shape, dtype):
    return pltpu.HBM(shape, dtype)


def _pin(*arrays):
    return [pltpu.with_memory_space_constraint(a, pltpu.HBM) for a in arrays]


_DIMS = {"nn": (((1,), (0,)), ((), ())), "nt": (((1,), (1,)), ((), ())), "tn": (((0,), (0,)), ((), ()))}


def _matmul(a, b, *, mode, out_dtype, name, add=None, tm=512, tn=512, col_shards=False, after=None):
    if mode == "tn":
        (k, m), (k2, n) = a.shape, b.shape
    elif mode == "nt":
        (m, k), (n, k2) = a.shape, b.shape
    else:
        (m, k), (k2, n) = a.shape, b.shape
    assert k == k2, (a.shape, b.shape, mode)
    tm, tn = _pick(m, tm), (n // N_CHIPS if col_shards else _pick(n, tn))
    a_spec = pl.BlockSpec((k, tm), lambda i, j: (0, i)) if mode == "tn" else pl.BlockSpec((tm, k), lambda i, j: (i, 0))
    b_spec = pl.BlockSpec((tn, k), lambda i, j: (j, 0)) if mode == "nt" else pl.BlockSpec((k, tn), lambda i, j: (0, j))
    o_spec = pl.BlockSpec((None, tm, tn), lambda i, j: (j, i, 0)) if col_shards else pl.BlockSpec((tm, tn), lambda i, j: (i, j))
    dims = _DIMS[mode]

    def body(*refs):
        a_ref, b_ref, o_ref = refs[0], refs[1], refs[-1]
        acc = lax.dot_general(a_ref[...].astype(BF16), b_ref[...].astype(BF16), dims, preferred_element_type=F32)
        if add is not None:
            acc = acc + refs[2][...].astype(F32)
        o_ref[...] = acc.astype(out_dtype)

    ins, specs = [a, b], [a_spec, b_spec]
    if add is not None:
        ins.append(add)
        specs.append(o_spec)
    if after is not None:
        ins.append(after)
        specs.append(pl.BlockSpec(memory_space=pl.ANY))
    est = 2 * (tm * k * _isz(a) + tn * k * _isz(b) + tm * tn * (jnp.dtype(out_dtype).itemsize + (4 if add is not None else 0)))
    est += (tm + tn) * k * 2 + 2 * tm * tn * 4
    return pl.pallas_call(
        body, name=name, grid=(m // tm, n // tn), in_specs=specs, out_specs=o_spec,
        out_shape=_hbm((N_CHIPS, m, tn) if col_shards else (m, n), out_dtype),
        compiler_params=_params(est),
    )(*_pin(*ins))


def _ffn_up(h, wg, wu, *, name):
    s, d = h.shape
    _, _, c = wg.shape
    tm = _pick(s, 512, SUBLANE)

    def body(h_ref, wg_ref, wu_ref, g_ref, u_ref, a_ref):
        hv = h_ref[...]
        gv = jnp.dot(hv, wg_ref[...], preferred_element_type=F32)
        uv = jnp.dot(hv, wu_ref[...], preferred_element_type=F32)
        g_ref[...] = gv.astype(BF16)
        u_ref[...] = uv.astype(BF16)
        a_ref[...] = (gv / (1.0 + jnp.exp(-gv)) * uv).astype(BF16)

    w_spec = pl.BlockSpec((None, d, c), lambda j, i: (j, 0, 0))
    o_spec = pl.BlockSpec((tm, c), lambda j, i: (i, j))
    est = 2 * (tm * d * 2 + 2 * d * c * 2 + tm * c * 10) + 4 * tm * c * 4
    return pl.pallas_call(
        body, name=name, grid=(N_CHIPS, s // tm), in_specs=[pl.BlockSpec((tm, d), lambda j, i: (i, 0)), w_spec, w_spec],
        out_specs=[o_spec, o_spec, o_spec],
        out_shape=[_hbm((s, N_CHIPS * c), BF16)] * 3,
        compiler_params=_params(est),
    )(*_pin(h, wg, wu))


def _ffn_down_dx(dx, w_down, gate, up, *, name):
    s, d = dx.shape
    f = w_down.shape[0]
    tm, tn = _pick(s, 1024, SUBLANE), _pick(f, 512)

    def body(dx_ref, w_ref, g_ref, u_ref, dg_ref, du_ref):
        dact = lax.dot_general(dx_ref[...], w_ref[...], _DIMS["nt"], preferred_element_type=F32)
        gv, uv = g_ref[...].astype(F32), u_ref[...].astype(F32)
        sig = 1.0 / (1.0 + jnp.exp(-gv))
        dg_ref[...] = (dact * uv * (sig * (1.0 + gv * (1.0 - sig)))).astype(BF16)
        du_ref[...] = (dact * (gv * sig)).astype(BF16)

    t_spec = pl.BlockSpec((tm, tn), lambda i, j: (i, j))
    est = 2 * (tm * d * 2 + tn * d * 2 + tm * tn * 12) + 6 * tm * tn * 4
    return pl.pallas_call(
        body, name=name, grid=(s // tm, f // tn),
        in_specs=[pl.BlockSpec((tm, d), lambda i, j: (i, 0)), pl.BlockSpec((tn, d), lambda i, j: (j, 0)), t_spec, t_spec],
        out_specs=[t_spec, t_spec], out_shape=[_hbm((s, f), BF16)] * 2, compiler_params=_params(est),
    )(*_pin(dx, w_down, gate, up))


def _ffn_up_dx(dgate, dup, wg, wu, *, name):
    s, f = dgate.shape
    _, d, c = wg.shape
    tm, tn = _pick(s, 1024, SUBLANE), _pick(d, 1024)
    nk = 2 * N_CHIPS

    def body(dg_ref, du_ref, wg_ref, wu_ref, o_ref, acc):
        kk = pl.program_id(2)

        @pl.when(kk == 0)
        def _():
            acc[...] = jnp.zeros_like(acc)

        @pl.when(kk < N_CHIPS)
        def _():
            acc[...] += lax.dot_general(dg_ref[...], wg_ref[...], _DIMS["nt"], preferred_element_type=F32)

        @pl.when(kk >= N_CHIPS)
        def _():
            acc[...] += lax.dot_general(du_ref[...], wu_ref[...], _DIMS["nt"], preferred_element_type=F32)

        @pl.when(kk == nk - 1)
        def _():
            o_ref[...] = acc[...]

    last = N_CHIPS - 1
    est = 2 * (2 * tm * c * 2 + 2 * tn * c * 2 + tm * tn * 4) + 2 * tm * tn * 4
    return pl.pallas_call(
        body, name=name, grid=(s // tm, d // tn, nk),
        in_specs=[pl.BlockSpec((tm, c), lambda i, j, kk: (i, jnp.minimum(kk, last))),
                  pl.BlockSpec((tm, c), lambda i, j, kk: (i, jnp.maximum(kk - N_CHIPS, 0))),
                  pl.BlockSpec((None, tn, c), lambda i, j, kk: (jnp.minimum(kk, last), j, 0)),
                  pl.BlockSpec((None, tn, c), lambda i, j, kk: (jnp.maximum(kk - N_CHIPS, 0), j, 0))],
        out_specs=pl.BlockSpec((tm, tn), lambda i, j, kk: (i, j)),
        out_shape=_hbm((s, d), F32), scratch_shapes=[pltpu.VMEM((tm, tn), F32)],
        compiler_params=_params(est),
    )(*_pin(dgate, dup, wg, wu))


def _norm_fwd(x, gain, *, wb, cb, nb, shared_gain, out_dtype, name):
    s = x.shape[0]
    ts = _rows_for(wb * 4, s)

    def body(x_ref, g_ref, o_ref):
        xv = x_ref[...].astype(F32)
        r = lax.rsqrt(jnp.mean(xv * xv, axis=1, keepdims=True) + EPS)
        o_ref[...] = ((xv * r) * g_ref[...]).astype(out_dtype)

    return pl.pallas_call(
        body, name=name, grid=(nb, s // ts),
        in_specs=[pl.BlockSpec((ts, wb), lambda n, i: (i, cb + n)),
                  pl.BlockSpec((1, wb), (lambda n, i: (0, 0)) if shared_gain else (lambda n, i: (0, n)))],
        out_specs=pl.BlockSpec((ts, wb), lambda n, i: (i, n)),
        out_shape=_hbm((s, nb * wb), out_dtype), compiler_params=_params(6 * ts * wb * 4),
    )(*_pin(x), gain)


def _norm_bwd(x, gain, dy, *, wb, cb, nb, shared_gain, out_dtype, name, dy_cb=0, add=None, bf16_copy=False):
    s = x.shape[0]
    ts = _rows_for(wb * 4, s, target=1024 * 1024)
    gw = wb if shared_gain else nb * wb

    def body(*refs):
        refs = list(refs)
        dxb_ref = refs.pop() if bf16_copy else None
        if add is None:
            x_ref, g_ref, dy_ref, dx_ref, dg_ref = refs
        else:
            x_ref, g_ref, dy_ref, add_ref, dx_ref, dg_ref = refs
        n, i = pl.program_id(0), pl.program_id(1)
        xv = x_ref[...].astype(F32)
        dyv = dy_ref[...].astype(F32)
        r = lax.rsqrt(jnp.mean(xv * xv, axis=1, keepdims=True) + EPS)
        xh = xv * r
        dyg = dyv * g_ref[...]
        dx = r * (dyg - xh * jnp.mean(dyg * xh, axis=1, keepdims=True))
        if add is not None:
            dx = dx + add_ref[...]
        dx_ref[...] = dx.astype(out_dtype)
        if bf16_copy:
            dxb_ref[...] = dx.astype(BF16)
        first = jnp.logical_and(n == 0, i == 0) if shared_gain else (i == 0)

        @pl.when(first)
        def _():
            dg_ref[...] = jnp.zeros_like(dg_ref)

        dg_ref[...] += jnp.sum(dyv * xh, axis=0, keepdims=True)

    ins = [x, gain, dy]
    specs = [pl.BlockSpec((ts, wb), lambda n, i: (i, cb + n)),
             pl.BlockSpec((1, wb), (lambda n, i: (0, 0)) if shared_gain else (lambda n, i: (0, n))),
             pl.BlockSpec((ts, wb), lambda n, i: (i, dy_cb + n))]
    if add is not None:
        ins.append(add)
        specs.append(pl.BlockSpec((ts, wb), lambda n, i: (i, n)))
    out_specs = [pl.BlockSpec((ts, wb), lambda n, i: (i, n)),
                 pl.BlockSpec((1, wb), (lambda n, i: (0, 0)) if shared_gain else (lambda n, i: (0, n)))]
    out_shape = [_hbm((s, nb * wb), out_dtype), jax.ShapeDtypeStruct((1, gw), F32)]
    if bf16_copy:
        out_specs.append(out_specs[0])
        out_shape.append(_hbm((s, nb * wb), BF16))
    return pl.pallas_call(
        body, name=name, grid=(nb, s // ts), in_specs=specs, out_specs=out_specs, out_shape=out_shape,
        compiler_params=_params(14 * ts * wb * 4),
    )(*_pin(*ins))


def _swap_halves(x, half):
    if 2 * half == LANE:
        return pltpu.roll(x, half, axis=1)
    lane = lax.broadcasted_iota(jnp.int32, x.shape, 1)
    first = jnp.bitwise_and(lane, 2 * half - 1) < half
    return jnp.where(first, pltpu.roll(x, LANE - half, axis=1), pltpu.roll(x, half, axis=1))


def _rope(x, cos_t, sin_t, *, tw, cb, nb, half, sign, out_dtype, name):
    s = x.shape[0]
    ts = _rows_for(tw * 4, s)

    def body(x_ref, c_ref, s_ref, o_ref):
        for q in range(tw // LANE):
            sl = slice(q * LANE, (q + 1) * LANE)
            xv = x_ref[:, sl].astype(F32)
            sv = s_ref[:, sl]
            if sign < 0:
                sv = -sv
            o_ref[:, sl] = (xv * c_ref[:, sl] + _swap_halves(xv, half) * sv).astype(out_dtype)

    return pl.pallas_call(
        body, name=name, grid=(nb, s // ts),
        in_specs=[pl.BlockSpec((ts, tw), lambda n, i: (i, cb + n)),
                  pl.BlockSpec((ts, tw), lambda n, i: (i, 0)),
                  pl.BlockSpec((ts, tw), lambda n, i: (i, 0))],
        out_specs=pl.BlockSpec((ts, tw), lambda n, i: (i, n)),
        out_shape=_hbm((s, nb * tw), out_dtype), compiler_params=_params(10 * ts * tw * 4),
    )(*_pin(x), cos_t, sin_t)


def _latent_keys(kv, p, cos_t, sin_t, *, kr_cb, name):
    s = kv.shape[0]
    ts = _rows_for(A_QK * 4, s)

    def body(kv_ref, kr_ref, c_ref, s_ref, o_ref):
        o_ref[:, :LANE] = kv_ref[...].astype(BF16)
        x = kr_ref[...].astype(F32)
        o_ref[:, LANE:] = (x * c_ref[:, LANE:] + _swap_halves(x, A_ROPE // 2) * s_ref[:, LANE:]).astype(BF16)

    tab = pl.BlockSpec((ts, A_QK), lambda n, i: (i, 0))
    return pl.pallas_call(
        body, name=name, grid=(A_HEADS, s // ts),
        in_specs=[pl.BlockSpec((ts, LANE), lambda n, i: (i, 2 * n)), pl.BlockSpec((ts, LANE), lambda n, i: (i, kr_cb)),
                  tab, tab],
        out_specs=pl.BlockSpec((ts, A_QK), lambda n, i: (i, n)),
        out_shape=_hbm((s, A_HEADS * A_QK), BF16), compiler_params=_params(10 * ts * A_QK * 4),
    )(*_pin(kv, p), cos_t, sin_t)


def _latent_keys_bwd(dka, dva, cos_t, sin_t, *, name):
    s = dka.shape[0]
    ts = _rows_for(A_HEADS * A_QK * 4, s)

    def body(dka_ref, dva_ref, c_ref, s_ref, dkv_ref, dkr_ref):
        acc = jnp.zeros((ts, LANE), F32)
        for h in range(A_HEADS):
            dkv_ref[:, h * A_QK:h * A_QK + LANE] = dka_ref[:, h * A_QK:h * A_QK + LANE].astype(BF16)
            dkv_ref[:, h * A_QK + LANE:(h + 1) * A_QK] = dva_ref[:, h * LANE:(h + 1) * LANE].astype(BF16)
            y = dka_ref[:, h * A_QK + LANE:(h + 1) * A_QK]
            acc = acc + (y * c_ref[:, LANE:] - _swap_halves(y, A_ROPE // 2) * s_ref[:, LANE:])
        dkr_ref[...] = acc.astype(BF16)

    def rows(width):
        return pl.BlockSpec((ts, width), lambda i: (i, 0))

    return pl.pallas_call(
        body, name=name, grid=(s // ts,),
        in_specs=[rows(A_HEADS * A_QK), rows(A_HEADS * LANE), rows(A_QK), rows(A_QK)],
        out_specs=[rows(A_HEADS * A_QK), rows(LANE)],
        out_shape=[_hbm((s, A_HEADS * A_QK), BF16), _hbm((s, LANE), BF16)],
        compiler_params=_params(8 * ts * A_HEADS * A_QK * 4),
    )(*_pin(dka, dva), cos_t, sin_t)


def _cast_cols(x, *, cb, nb, name):
    s = x.shape[0]
    ts = _rows_for(LANE * 4, s)

    def body(x_ref, o_ref):
        o_ref[...] = x_ref[...].astype(BF16)

    return pl.pallas_call(
        body, name=name, grid=(nb, s // ts), in_specs=[pl.BlockSpec((ts, LANE), lambda n, i: (i, cb + n))],
        out_specs=pl.BlockSpec((ts, LANE), lambda n, i: (i, n)),
        out_shape=_hbm((s, nb * LANE), BF16), compiler_params=_params(4 * ts * LANE * 4),
    )(*_pin(x))


LOG2E = 1.4426950408889634
ATTN_ROW_CHUNK = 256


def _attn_window(i, rc, s, band):
    w, r = band
    start = jnp.clip(i * rc - r, 0, s - w)
    return pl.multiple_of(start, rc), pl.multiple_of((w - rc) - (i * rc - start), LANE)


def _flash_fwd(q, k, v, table, *, hkv, g, dqk, q_cb, k_cb, v_cb, v_step, scale, tq, band, name):
    s = q.shape[0]
    n = s // tq
    hq = hkv * g
    rc = min(tq, ATTN_ROW_CHUNK)
    w = s if band is None else band[0]

    def body(*refs):
        if band is None:
            q_ref, k_ref, v_ref, o_ref, lse_ref = refs
            kw, vw = k_ref[...], v_ref[...]
        else:
            q_ref, k_ref, v_ref, t_ref, o_ref, lse_ref = refs
        for c in range(tq // rc):
            rows = slice(c * rc, (c + 1) * rc)
            if band is not None:
                start, u = _attn_window(pl.program_id(1) * (tq // rc) + c, rc, s, band)
                kw, vw = k_ref[pl.ds(start, w), :], v_ref[pl.ds(start, w), :]
            sc = lax.dot_general(q_ref[rows, :], kw, _DIMS["nt"], preferred_element_type=F32) * (scale * LOG2E)
            if band is not None:
                sc = sc + t_ref[:, pl.ds(u, w)]
            m = jnp.max(sc, axis=1, keepdims=True)
            p = jnp.exp2(sc - m)
            l = jnp.sum(p, axis=1, keepdims=True)
            o_ref[rows, :] = jnp.dot(p.astype(BF16), vw, preferred_element_type=F32) / l
            lse_ref[0, rows, :] = jnp.broadcast_to(m + jnp.log2(l), (rc, LANE))

    ins = [q, k, v]
    specs = [pl.BlockSpec((tq, dqk), lambda h, i: (i, q_cb + h)),
             pl.BlockSpec((s, dqk), lambda h, i: (0, k_cb + h // g)),
             pl.BlockSpec((s, LANE), lambda h, i: (0, v_cb + v_step * (h // g)))]
    if band is not None:
        ins.append(table)
        specs.append(pl.BlockSpec(table.shape, lambda h, i: (0, 0)))
    est = 4 * s * (dqk + LANE) + 6 * rc * w * 4 + 8 * tq * LANE * 4 + (0 if band is None else 2 * table.size * 4)
    return pl.pallas_call(
        body, name=name, grid=(hq, n), in_specs=specs,
        out_specs=[pl.BlockSpec((tq, LANE), lambda h, i: (i, h)), pl.BlockSpec((1, tq, LANE), lambda h, i: (h, i, 0))],
        out_shape=[_hbm((s, hq * LANE), F32), _hbm((hq, s, LANE), F32)],
        compiler_params=_params(est),
    )(*_pin(*ins))


def _flash_bwd(q, k, v, o, do, lse, table, *, hkv, g, dqk, q_cb, k_cb, v_cb, v_step, scale, tq, band, name):
    s = q.shape[0]
    n = s // tq
    hq = hkv * g
    rc = min(tq, ATTN_ROW_CHUNK)
    w = s if band is None else band[0]

    def body(*refs):
        if band is None:
            q_ref, k_ref, v_ref, o_ref, do_ref, lse_ref, dq_ref, dk_ref, dv_ref = refs
            keys = slice(None)
            kw, vw = k_ref[...], v_ref[...]
        else:
            q_ref, k_ref, v_ref, o_ref, do_ref, lse_ref, t_ref, dq_ref, dk_ref, dv_ref = refs
        h, i = pl.program_id(0), pl.program_id(1)

        @pl.when(jnp.logical_and(h % g == 0, i == 0))
        def _():
            dk_ref[...] = jnp.zeros_like(dk_ref)
            dv_ref[...] = jnp.zeros_like(dv_ref)

        for c in range(tq // rc):
            rows = slice(c * rc, (c + 1) * rc)
            if band is not None:
                start, u = _attn_window(i * (tq // rc) + c, rc, s, band)
                keys = pl.ds(start, w)
                kw, vw = k_ref[keys, :], v_ref[keys, :]
            qv = q_ref[rows, :]
            dof = do_ref[rows, :]
            dov = dof.astype(BF16)
            sc = lax.dot_general(qv, kw, _DIMS["nt"], preferred_element_type=F32) * (scale * LOG2E)
            if band is not None:
                sc = sc + t_ref[:, pl.ds(u, w)]
            p = jnp.exp2(sc - lse_ref[0, rows, 0:1])
            dp = lax.dot_general(dov, vw, _DIMS["nt"], preferred_element_type=F32)
            delta = jnp.sum(dof * o_ref[rows, :], axis=1, keepdims=True)
            ds = (p * (dp - delta) * scale).astype(BF16)
            dv_ref[keys, :] += lax.dot_general(p.astype(BF16), dov, _DIMS["tn"], preferred_element_type=F32)
            dk_ref[keys, :] += lax.dot_general(ds, qv, _DIMS["tn"], preferred_element_type=F32)
            dq_ref[rows, :] = jnp.dot(ds, kw, preferred_element_type=F32)

    ins = [q, k, v, o, do, lse]
    specs = [pl.BlockSpec((tq, dqk), lambda h, i: (i, q_cb + h)),
             pl.BlockSpec((s, dqk), lambda h, i: (0, k_cb + h // g)),
             pl.BlockSpec((s, LANE), lambda h, i: (0, v_cb + v_step * (h // g))),
             pl.BlockSpec((tq, LANE), lambda h, i: (i, h)),
             pl.BlockSpec((tq, LANE), lambda h, i: (i, h)),
             pl.BlockSpec((1, tq, LANE), lambda h, i: (h, i, 0))]
    if band is not None:
        ins.append(table)
        specs.append(pl.BlockSpec(table.shape, lambda h, i: (0, 0)))
    est = (4 + 8) * s * (dqk + LANE) + 10 * rc * w * 4 + 12 * tq * LANE * 4 + (0 if band is None else 2 * table.size * 4)
    return pl.pallas_call(
        body, name=name, grid=(hq, n), in_specs=specs,
        out_specs=[pl.BlockSpec((tq, dqk), lambda h, i: (i, h)),
                   pl.BlockSpec((s, dqk), lambda h, i: (0, h // g)),
                   pl.BlockSpec((s, LANE), lambda h, i: (0, h // g))],
        out_shape=[_hbm((s, hq * dqk), F32), _hbm((s, hkv * dqk), F32),
                   _hbm((s, hkv * LANE), F32)],
        compiler_params=_params(est),
    )(*_pin(*ins))


def _final_loss(x, gain, target, *, name):
    s, d = x.shape
    ts = _rows_for(d * 4, s, target=1024 * 1024)

    def body(x_ref, g_ref, t_ref, dx_ref, dg_ref, loss_ref, dxb_ref):
        i = pl.program_id(0)
        xv = x_ref[...]
        gv = g_ref[...]
        r = lax.rsqrt(jnp.mean(xv * xv, axis=1, keepdims=True) + EPS)
        xh = xv * r
        err = xh * gv - t_ref[...]
        dy = err / d
        dyg = dy * gv
        dx = r * (dyg - xh * jnp.mean(dyg * xh, axis=1, keepdims=True))
        dx_ref[...] = dx
        dxb_ref[...] = dx.astype(BF16)

        @pl.when(i == 0)
        def _():
            dg_ref[...] = jnp.zeros_like(dg_ref)
            loss_ref[...] = jnp.zeros_like(loss_ref)

        dg_ref[...] += jnp.sum(dy * xh, axis=0, keepdims=True)
        part = jnp.sum(jnp.mean(err * err, axis=1, keepdims=True), axis=0, keepdims=True)
        loss_ref[...] += jnp.broadcast_to(0.5 * part, (1, LANE))

    row = pl.BlockSpec((ts, d), lambda i: (i, 0))
    return pl.pallas_call(
        body, name=name, grid=(s // ts,),
        in_specs=[row, pl.BlockSpec((1, d), lambda i: (0, 0)), row],
        out_specs=[row, pl.BlockSpec((1, d), lambda i: (0, 0)), pl.BlockSpec((1, LANE), lambda i: (0, 0)), row],
        out_shape=[_hbm((s, d), F32), jax.ShapeDtypeStruct((1, d), F32),
                   jax.ShapeDtypeStruct((1, LANE), F32), _hbm((s, d), BF16)],
        compiler_params=_params(14 * ts * d * 4),
    )(*_pin(x), gain, *_pin(target))


def _cast_to_slot(x3d, me, *, layer, name):
    _, rows, c = x3d.shape
    tr, tc = _tile2(rows, c, 2 * 1024 * 1024)

    def body(me_ref, x_ref, o_ref):
        o_ref[...] = x_ref[...].astype(BF16)

    return pl.pallas_call(
        body, name=name,
        grid_spec=pltpu.PrefetchScalarGridSpec(
            num_scalar_prefetch=1, grid=(rows // tr, c // tc),
            in_specs=[pl.BlockSpec((None, tr, tc), lambda i, j, me_ref: (layer, i, j))],
            out_specs=pl.BlockSpec((None, tr, tc), lambda i, j, me_ref: (me_ref[0], i, j))),
        out_shape=_hbm((N_CHIPS, rows, c), BF16), compiler_params=_params(6 * tr * tc * 4),
    )(me, *_pin(x3d))


def _sum_parts(srcs, lands, me, *, name):
    depth = len(srcs)
    _, r, c = srcs[0].shape
    tr, tc = _tile2(r, c, 1024 * 1024)
    nt, nc = r // tr, c // tc

    def body(me_ref, *refs):
        o_ref = refs[-1]
        l = pl.program_id(0)
        for k in range(depth):
            @pl.when(l == k)
            def _(k=k):
                acc = refs[k][...].astype(F32)
                for p in range(3):
                    acc = acc + refs[depth + k][p].astype(F32)
                o_ref[...] = acc

    def tile_of(k):
        def f(l, i, j):
            return (jnp.where(l == k, i, jnp.where(l < k, 0, nt - 1)), jnp.where(l == k, j, jnp.where(l < k, 0, nc - 1)))
        return f

    in_specs = [pl.BlockSpec((None, tr, tc), functools.partial(lambda l, i, j, me_ref, f: (me_ref[0], *f(l, i, j)), f=tile_of(k)))
                for k in range(depth)]
    in_specs += [pl.BlockSpec((3, tr, tc), functools.partial(lambda l, i, j, me_ref, f: (0, *f(l, i, j)), f=tile_of(k)))
                 for k in range(depth)]
    return pl.pallas_call(
        body, name=name,
        grid_spec=pltpu.PrefetchScalarGridSpec(
            num_scalar_prefetch=1, grid=(depth, nt, nc), in_specs=in_specs,
            out_specs=pl.BlockSpec((tr, tc), lambda l, i, j, me_ref: (l * nt + i, j))),
        out_shape=_hbm((depth * r, c), F32), compiler_params=_params(depth * 10 * tr * tc * 4),
    )(me, *_pin(*srcs, *lands))


def _adamw_math(w, g, m, v):
    m2 = ADAM_B1 * m + (1.0 - ADAM_B1) * g
    v2 = ADAM_B2 * v + (1.0 - ADAM_B2) * (g * g)
    m_hat = m2 / (1.0 - ADAM_B1 ** ADAM_STEP)
    v_hat = v2 / (1.0 - ADAM_B2 ** ADAM_STEP)
    delta = -ADAM_LR * (m_hat / (jnp.sqrt(v_hat) + ADAM_EPS) + ADAM_WD * w)
    return delta, m2, v2


def _adamw(g_a, g_b, w, m, v, *, name):
    depth, r, c = w.shape
    tr, tc = _tile2(r, c, 512 * 1024)
    nt = r // tr

    def body(a_ref, b_ref, w_ref, m_ref, v_ref, g_out, d_out, m_out, v_out):
        gv = a_ref[...] + b_ref[...]
        delta, m2, v2 = _adamw_math(w_ref[...], gv, m_ref[...], v_ref[...])
        g_out[...] = gv
        d_out[...] = delta
        m_out[...] = m2
        v_out[...] = v2

    flat = pl.BlockSpec((tr, tc), lambda l, i, j: (l * nt + i, j))
    spec = pl.BlockSpec((None, tr, tc), lambda l, i, j: (l, i, j))
    return pl.pallas_call(
        body, name=name, grid=(depth, nt, c // tc), in_specs=[flat, flat, spec, spec, spec], out_specs=[spec] * 4,
        out_shape=[_hbm((depth, r, c), F32)] * 4, compiler_params=_params(22 * tr * tc * 4),
    )(*_pin(g_a, g_b, w, m, v))


def _small_adamw(g_all, w, m, v, *, name):
    r, c = w.shape

    def body(ga_ref, w_ref, m_ref, v_ref, g_out, d_out, m_out, v_out):
        gv = ga_ref[0]
        for j in range(1, N_DEV):
            gv = gv + ga_ref[j]
        delta, m2, v2 = _adamw_math(w_ref[...], gv, m_ref[...], v_ref[...])
        g_out[...] = gv
        d_out[...] = delta
        m_out[...] = m2
        v_out[...] = v2

    return pl.pallas_call(body, name=name, out_shape=[jax.ShapeDtypeStruct((r, c), F32)] * 4)(g_all, w, m, v)


_ANY = pl.BlockSpec(memory_space=pl.ANY)


_HBM = pl.BlockSpec(memory_space=pltpu.HBM)
_SEM = pl.BlockSpec(memory_space=pltpu.SEMAPHORE)
_EFFECT = pltpu.SideEffectType.DATAFLOW_SIDE_EFFECTING


def _peer_chips():
    x, y = lax.axis_index("x"), lax.axis_index("y")
    return 2 * x + y, [(1 - x, y), (x, 1 - y), (1 - x, 1 - y)]


def _exchange_copy(srcs, lands, send_sems, recv_sems, k, p, kind):
    c = lax.axis_index("c")
    if kind == "swap":
        return pltpu.make_async_remote_copy(
            src_ref=srcs[k], dst_ref=lands[k], send_sem=send_sems.at[k], recv_sem=recv_sems.at[k],
            device_id=(lax.axis_index("x"), lax.axis_index("y"), 1 - c), device_id_type=MESH)
    me, peers = _peer_chips()
    px, py = peers[p]
    return pltpu.make_async_remote_copy(
        src_ref=srcs[k].at[2 * px + py] if kind == "scatter" else srcs[k].at[me],
        dst_ref=lands[k].at[p] if kind == "scatter" else lands[k].at[me],
        send_sem=send_sems.at[3 * k + p], recv_sem=recv_sems.at[3 * k + p],
        device_id=(px, py, c), device_id_type=MESH)


def _exchange_start(srcs, lands, after, *, kind, name):
    n = len(srcs)
    npeer = 1 if kind == "swap" else 3
    bufs = list(srcs) + (list(lands) if lands is not None else [])
    nb = len(bufs)

    def body(*refs):
        buf_refs, send_sems, recv_sems = refs[:nb], refs[nb + 1], refs[nb + 2]
        token = refs[-1]
        s_refs = buf_refs[:n]
        l_refs = buf_refs[n:] if lands is not None else s_refs
        for k in range(n):
            for p in range(npeer):
                _exchange_copy(s_refs, l_refs, send_sems, recv_sems, k, p, kind).start()
        token[...] = jnp.zeros_like(token)

    out = pl.pallas_call(
        body, name=name,
        out_shape=(pltpu.SemaphoreType.DMA((npeer * n,)), pltpu.SemaphoreType.DMA((npeer * n,)),
                   *[pltpu.HBM(b.shape, b.dtype) for b in bufs], jax.ShapeDtypeStruct((SUBLANE, LANE), F32)),
        in_specs=[_HBM] * nb + [_ANY],
        out_specs=(_SEM, _SEM, *[_HBM] * nb, pl.BlockSpec(memory_space=pltpu.VMEM)),
        input_output_aliases={i: 2 + i for i in range(nb)},
        compiler_params=pltpu.CompilerParams(has_side_effects=_EFFECT),
    )(*[pltpu.with_memory_space_constraint(b, pltpu.HBM) for b in bufs], after)
    send_sems, recv_sems = out[0], out[1]
    thru = out[2:2 + nb]
    return send_sems, recv_sems, list(thru[:n]), (list(thru[n:]) if lands is not None else None), out[-1]


def _exchange_wait(send_sems, recv_sems, srcs, lands, after, *, kind, name):
    n = len(srcs)
    npeer = 1 if kind == "swap" else 3
    bufs = list(srcs) + (list(lands) if lands is not None else [])
    nb = len(bufs)

    def body(*refs):
        buf_refs, send_sems_ref, recv_sems_ref = refs[:nb], refs[nb], refs[nb + 1]
        s_refs = buf_refs[:n]
        l_refs = buf_refs[n:] if lands is not None else s_refs
        for k in range(n):
            for p in range(npeer):
                cp = _exchange_copy(s_refs, l_refs, send_sems_ref, recv_sems_ref, k, p, kind)
                cp.wait_send()
                cp.wait_recv()

    out = pl.pallas_call(
        body, name=name, out_shape=tuple(pltpu.HBM(b.shape, b.dtype) for b in bufs),
        in_specs=[_HBM] * nb + [_SEM, _SEM, _ANY], out_specs=tuple([_HBM] * nb),
        input_output_aliases={i: i for i in range(nb)},
        compiler_params=pltpu.CompilerParams(has_side_effects=_EFFECT),
    )(*bufs, send_sems, recv_sems, after)
    return list(out)


def _sibling_exchange(srcs, *, name):
    n = len(srcs)

    def body(*refs):
        src, out = refs[:n], refs[n:2 * n]
        send_sems, recv_sems = refs[2 * n:]
        sibling = (lax.axis_index("x"), lax.axis_index("y"), 1 - lax.axis_index("c"))
        copies = [pltpu.make_async_remote_copy(src_ref=src[k], dst_ref=out[k], send_sem=send_sems.at[k],
                                               recv_sem=recv_sems.at[k], device_id=sibling, device_id_type=MESH)
                  for k in range(n)]
        for cp in copies:
            cp.start()
        for cp in copies:
            cp.wait_recv()
        for cp in copies:
            cp.wait_send()

    return pl.pallas_call(
        body, name=name, in_specs=[_ANY] * n, out_specs=[_ANY] * n,
        out_shape=[jax.ShapeDtypeStruct(a.shape, a.dtype) for a in srcs],
        scratch_shapes=[pltpu.SemaphoreType.DMA((n,)), pltpu.SemaphoreType.DMA((n,))],
    )(*srcs)


def _all_gather_small(block, *, name):
    m_per, ncol = block.shape

    def body(x_ref, out_ref, send_sems, recv_sems, local_sem):
        x, y, c = lax.axis_index("x"), lax.axis_index("y"), lax.axis_index("c")
        me, sibling = (x, y, c), (x, y, 1 - c)
        chips = [(1 - x, y), (x, 1 - y), (1 - x, 1 - y)]

        def rows(px, py, pc):
            return out_ref.at[pl.ds((4 * px + 2 * py + pc) * m_per, m_per), :]

        def copy(k, blk, to, src=None):
            return pltpu.make_async_remote_copy(
                src_ref=rows(*blk) if src is None else src, dst_ref=rows(*blk),
                send_sem=send_sems.at[k], recv_sem=recv_sems.at[k], device_id=to, device_id_type=MESH)

        mine = pltpu.make_async_copy(x_ref, rows(*me), local_sem)
        mine.start()
        first = [copy(0, me, sibling, src=x_ref)]
        first += [copy(1 + j, me, (*chip, c), src=x_ref) for j, chip in enumerate(chips)]
        for cp in first:
            cp.start()
        passed = [copy(4 + j, (*chip, c), sibling) for j, chip in enumerate(chips)]
        for j, chip in enumerate(chips):
            copy(1 + j, (*chip, c), me).wait_recv()
            passed[j].start()
        copy(0, sibling, me).wait_recv()
        for j, chip in enumerate(chips):
            copy(4 + j, (*chip, 1 - c), me).wait_recv()
        for cp in first + passed:
            cp.wait_send()
        mine.wait()

    return pl.pallas_call(
        body, name=name, out_shape=jax.ShapeDtypeStruct((N_DEV * m_per, ncol), block.dtype),
        in_specs=[pl.BlockSpec(memory_space=pltpu.VMEM)], out_specs=pl.BlockSpec(memory_space=pltpu.VMEM),
        scratch_shapes=[pltpu.SemaphoreType.DMA((7,)), pltpu.SemaphoreType.DMA((7,)), pltpu.SemaphoreType.DMA],
    )(block)


def _rope_angles(pos, dim):
    inv = ROPE_THETA ** (-jnp.arange(0, dim, 2, dtype=F32) / dim)
    return pos.astype(F32)[:, None] * inv[None, :]


def _rope_tables(s):
    pos = jnp.arange(s, dtype=jnp.int32)
    rows = s // GRID_W
    row = jnp.repeat(jnp.arange(rows, dtype=jnp.int32), GRID_W)
    col = jnp.tile(jnp.arange(GRID_W, dtype=jnp.int32), rows)
    a1 = _rope_angles(pos, HEAD_DIM)
    aa = _rope_angles(pos, A_ROPE)
    ar = _rope_angles(row, HEAD_DIM // 2)
    ac = _rope_angles(col, HEAD_DIM // 2)
    one = jnp.ones((s, LANE), F32)
    zero = jnp.zeros((s, LANE), F32)
    pad = LANE - A_ROPE
    cos_a = jnp.concatenate([one, jnp.cos(aa), jnp.cos(aa), jnp.ones((s, pad), F32)], axis=1)
    sin_a = jnp.concatenate([zero, -jnp.sin(aa), jnp.sin(aa), jnp.zeros((s, pad), F32)], axis=1)
    cos_b = jnp.concatenate([jnp.cos(a1), jnp.cos(a1)], axis=1)
    sin_b = jnp.concatenate([-jnp.sin(a1), jnp.sin(a1)], axis=1)
    cos_c = jnp.concatenate([jnp.cos(ar), jnp.cos(ar), jnp.cos(ac), jnp.cos(ac)], axis=1)
    sin_c = jnp.concatenate([-jnp.sin(ar), jnp.sin(ar), -jnp.sin(ac), jnp.sin(ac)], axis=1)
    return (cos_a, sin_a), (cos_b, sin_b), (cos_c, sin_c)


def _band_table(rc, s):
    reach = max((win // (2 * d)) * d for win, d in B_PATTERNS)
    r = -(-reach // rc) * rc
    w = min(s, rc + 2 * r)
    j = jnp.arange(rc, dtype=jnp.int32)[:, None]
    x = jnp.arange(2 * w - rc, dtype=jnp.int32)[None, :]
    rel = x - (w - rc) - j
    mult = jnp.zeros(rel.shape, F32)
    for win, d in B_PATTERNS:
        mult = mult + jnp.logical_and(rel % d == 0, jnp.abs(rel) <= (win // (2 * d)) * d).astype(F32)
    return jnp.where(mult > 0, jnp.log2(jnp.maximum(mult, 1.0)), NEG), (w, r)


_BIG = ("w_in", "a_w_uq", "a_w_ukv", "w_out", "w_gate", "w_up", "w_down")
_SMALL = ("attn_norm", "a_q_norm", "a_kv_norm", "c_q_norm", "c_k_norm", "out_norm", "ffn_norm", "final_norm")
_WEIGHTS = ("attn_norm", "w_in", "a_q_norm", "a_w_uq", "a_kv_norm", "a_w_ukv", "c_q_norm", "c_k_norm", "out_norm",
            "w_out", "ffn_norm", "w_gate", "w_up", "w_down", "final_norm")


_ATTN = ("w_in", "a_w_uq", "a_w_ukv")
_FFN = ("w_out", "w_gate", "w_up", "w_down")


def _from_cols(a):
    return jnp.transpose(a, (1, 0, 2)).reshape(a.shape[1], N_CHIPS * a.shape[2])


def _from_rows(a):
    return a.reshape(N_CHIPS * a.shape[1], a.shape[2])


def _to_cols(a):
    return jnp.transpose(a.reshape(a.shape[0], N_CHIPS, a.shape[1] // N_CHIPS), (1, 0, 2))


def _to_rows(a):
    return a.reshape(N_CHIPS, a.shape[0] // N_CHIPS, a.shape[1])


def _assemble_attn(gw):
    w_in_t, uq, ukv = _from_rows(gw[0]), _from_cols(gw[1]), _from_cols(gw[2])
    d = w_in_t.shape[1]
    w_all = jnp.concatenate([w_in_t[:IN_A], jnp.zeros((A_PAD - IN_A, d), BF16), w_in_t[IN_A:]], axis=0)
    uq = uq.reshape(A_Q_RANK, A_HEADS, A_NOPE + A_ROPE)
    uq = jnp.pad(uq, ((0, 0), (0, 0), (0, A_QK - A_NOPE - A_ROPE))).reshape(A_Q_RANK, A_HEADS * A_QK)
    return dict(w_all=w_all, uq=uq, ukv=ukv)


def _assemble_ffn(gw):
    return dict(w_out=_from_rows(gw[0]), w_gate=gw[1], w_up=gw[2], w_down=_from_rows(gw[3]))


def _split_attn_grads(gl):
    w_all = gl["w_all"]
    w_in_t = jnp.concatenate([w_all[:IN_A], w_all[A_PAD:]], axis=0)
    uq = gl["uq"].reshape(A_Q_RANK, A_HEADS, A_QK)[:, :, :A_NOPE + A_ROPE].reshape(A_Q_RANK, A_HEADS * (A_NOPE + A_ROPE))
    return [_to_rows(w_in_t), _to_cols(uq), _to_cols(gl["ukv"])]


def _split_ffn_grads(gl):
    return [_to_rows(gl["w_out"]), gl["w_gate"], gl["w_up"], _to_rows(gl["w_down"])]


def _tie(a, token):
    return a + token[0:1, 0:1]


def _layer_fwd(x, wl, ffn_weights, sm, tabs, bias, t):
    s = x.shape[0]
    (cos_a, sin_a), (cos_b, sin_b), (cos_c, sin_c) = tabs
    h = _norm_fwd(x, sm["attn_norm"], wb=x.shape[1], cb=0, nb=1, shared_gain=True, out_dtype=BF16, name="attn_norm_fwd")
    p = _matmul(h, wl["w_all"], mode="nt", out_dtype=F32, name="in_proj", tm=1024, tn=640)
    cq_n = _norm_fwd(p, sm["a_q_norm"], wb=A_Q_RANK, cb=0, nb=1, shared_gain=True, out_dtype=BF16, name="a_q_norm_fwd")
    ckv_n = _norm_fwd(p, sm["a_kv_norm"], wb=A_KV_RANK, cb=1, nb=1, shared_gain=True, out_dtype=BF16, name="a_kv_norm_fwd")
    qa_raw = _matmul(cq_n, wl["uq"], mode="nn", out_dtype=F32, name="a_uq", tm=1024, tn=1024)
    kv = _matmul(ckv_n, wl["ukv"], mode="nn", out_dtype=BF16, name="a_ukv", tm=1024, tn=1024)
    qa =_rope(qa_raw, cos_a, sin_a, tw=A_QK, cb=0, nb=A_HEADS, half=A_ROPE // 2, sign=1, out_dtype=BF16, name="a_rope_q")
    ka = _latent_keys(kv, p, cos_a, sin_a, kr_cb=PB_KR, name="a_keys")
    oa, lse_a = _flash_fwd(qa, ka, kv, None, hkv=A_HEADS, g=1, dqk=A_QK, q_cb=0, k_cb=0, v_cb=1, v_step=2,
                           scale=(A_NOPE + A_ROPE) ** -0.5, tq=t, band=None, name="a_flash_fwd")
    table, band = bias
    qb = _rope(p, cos_b, sin_b, tw=LANE, cb=PB_BQ, nb=B_HEADS, half=HEAD_DIM // 2, sign=1, out_dtype=BF16, name="b_rope_q")
    kb = _rope(p, cos_b, sin_b, tw=LANE, cb=PB_BK, nb=B_HEADS, half=HEAD_DIM // 2, sign=1, out_dtype=BF16, name="b_rope_k")
    vb = _cast_cols(p, cb=PB_BV, nb=B_HEADS, name="b_cast_v")
    ob, lse_b = _flash_fwd(qb, kb, vb, table, hkv=B_HEADS, g=1, dqk=LANE, q_cb=0, k_cb=0, v_cb=0, v_step=1,
                           scale=HEAD_DIM ** -0.5, tq=t, band=band, name="b_flash_fwd")
    qn = _norm_fwd(p, sm["c_q_norm"], wb=LANE, cb=PB_CQH, nb=C_HEADS, shared_gain=True, out_dtype=F32, name="c_q_norm_fwd")
    kn = _norm_fwd(p, sm["c_k_norm"], wb=LANE, cb=PB_CKH, nb=C_KV_HEADS, shared_gain=True, out_dtype=F32, name="c_k_norm_fwd")
    qc = _rope(qn, cos_c, sin_c, tw=LANE, cb=0, nb=C_HEADS, half=HEAD_DIM // 4, sign=1, out_dtype=BF16, name="c_rope_q")
    kc = _rope(kn, cos_c, sin_c, tw=LANE, cb=0, nb=C_KV_HEADS, half=HEAD_DIM // 4, sign=1, out_dtype=BF16, name="c_rope_k")
    vc = _cast_cols(p, cb=PB_CVH, nb=C_KV_HEADS, name="c_cast_v")
    oc, lse_c = _flash_fwd(qc, kc, vc, None, hkv=C_KV_HEADS, g=C_GROUP, dqk=LANE, q_cb=0, k_cb=0, v_cb=0, v_step=1,
                           scale=HEAD_DIM ** -0.5, tq=t, band=None, name="c_flash_fwd")
    g_out = sm["out_norm"]
    ga, gb, gc = g_out[:, :A_WIDTH], g_out[:, A_WIDTH:A_WIDTH + B_WIDTH], g_out[:, A_WIDTH + B_WIDTH:]
    ya = _norm_fwd(oa, ga, wb=A_WIDTH, cb=0, nb=1, shared_gain=True, out_dtype=BF16, name="out_norm_a_fwd")
    yb = _norm_fwd(ob, gb, wb=B_WIDTH, cb=0, nb=1, shared_gain=True, out_dtype=BF16, name="out_norm_b_fwd")
    yc = _norm_fwd(oc, gc, wb=C_WIDTH, cb=0, nb=1, shared_gain=True, out_dtype=BF16, name="out_norm_c_fwd")
    y = jnp.concatenate([ya, yb, yc], axis=1)
    wl = {**wl, **ffn_weights(y)}
    x1 = _matmul(y, wl["w_out"], mode="nn", out_dtype=F32, name="out_proj", add=x, tm=1024, tn=512)
    h2 = _norm_fwd(x1, sm["ffn_norm"], wb=x.shape[1], cb=0, nb=1, shared_gain=True, out_dtype=BF16, name="ffn_norm_fwd")
    gate, up, act = _ffn_up(h2, wl["w_gate"], wl["w_up"], name="ffn_up")
    x2 = _matmul(act, wl["w_down"], mode="nn", out_dtype=F32, name="ffn_down", add=x1, tm=1024, tn=512)
    saved = dict(x=x, h=h, p=p, cq_n=cq_n, ckv_n=ckv_n, kv=kv, qa=qa, ka=ka, oa=oa, lse_a=lse_a, qb=qb, kb=kb, vb=vb, ob=ob,
                 lse_b=lse_b, qc=qc, kc=kc, vc=vc, oc=oc, lse_c=lse_c, y=y, x1=x1, h2=h2, gate=gate, up=up, act=act)
    return x2, saved, wl


def _layer_bwd(dx2, dx2b, sv, wl, sm, tabs, bias, t, send_ffn, send_attn):
    s, d = dx2.shape
    (cos_a, sin_a), (cos_b, sin_b), (cos_c, sin_c) = tabs
    gw, gs = {}, {}
    dgate, dup = _ffn_down_dx(dx2b, wl["w_down"], sv["gate"], sv["up"], name="ffn_down_dx")
    gw["w_down"] = _matmul(sv["act"], dx2b, mode="tn", out_dtype=BF16, name="ffn_down_dw", tm=512, tn=2048)
    dh2 = _ffn_up_dx(dgate, dup, wl["w_gate"], wl["w_up"], name="ffn_up_dx")
    gw["w_gate"] = _matmul(sv["h2"], dgate, mode="tn", out_dtype=BF16, name="ffn_gate_dw", tm=1024, col_shards=True)
    gw["w_up"] = _matmul(sv["h2"], dup, mode="tn", out_dtype=BF16, name="ffn_up_dw", tm=1024, col_shards=True)
    dx1, gs["ffn_norm"], dx1b = _norm_bwd(sv["x1"], sm["ffn_norm"], dh2, wb=d, cb=0, nb=1, shared_gain=True,
                                          out_dtype=F32, name="ffn_norm_bwd", add=dx2, bf16_copy=True)
    dy = _matmul(dx1b, wl["w_out"], mode="nt", out_dtype=F32, name="out_proj_dx", tm=1024, tn=512)
    gw["w_out"] = _matmul(sv["y"], dx1b, mode="tn", out_dtype=BF16, name="out_proj_dw", tm=512, tn=2048)
    token = send_ffn(gw)
    g_out = _tie(sm["out_norm"], token)
    ga, gb, gc = g_out[:, :A_WIDTH], g_out[:, A_WIDTH:A_WIDTH + B_WIDTH], g_out[:, A_WIDTH + B_WIDTH:]
    dya, dyb, dyc = dy[:, :A_WIDTH], dy[:, A_WIDTH:A_WIDTH + B_WIDTH], dy[:, A_WIDTH + B_WIDTH:]
    doa, dga = _norm_bwd(sv["oa"], ga, dya, wb=A_WIDTH, cb=0, nb=1, shared_gain=True, out_dtype=F32, name="out_norm_a_bwd")
    dob, dgb = _norm_bwd(sv["ob"], gb, dyb, wb=B_WIDTH, cb=0, nb=1, shared_gain=True, out_dtype=F32, name="out_norm_b_bwd")
    doc, dgc = _norm_bwd(sv["oc"], gc, dyc, wb=C_WIDTH, cb=0, nb=1, shared_gain=True, out_dtype=F32, name="out_norm_c_bwd")
    gs["out_norm"] = jnp.concatenate([dga, dgb, dgc], axis=1)
    p = sv["p"]
    dqc, dkc, dvc = _flash_bwd(sv["qc"], sv["kc"], sv["vc"], sv["oc"], doc, sv["lse_c"], None, hkv=C_KV_HEADS,
                               g=C_GROUP, dqk=LANE, q_cb=0, k_cb=0, v_cb=0, v_step=1, scale=HEAD_DIM ** -0.5,
                               tq=t, band=None, name="c_flash_bwd")
    dqn = _rope(dqc, cos_c, sin_c, tw=LANE, cb=0, nb=C_HEADS, half=HEAD_DIM // 4, sign=-1, out_dtype=F32, name="c_rope_q_bwd")
    dkn = _rope(dkc, cos_c, sin_c, tw=LANE, cb=0, nb=C_KV_HEADS, half=HEAD_DIM // 4, sign=-1, out_dtype=F32, name="c_rope_k_bwd")
    dpcq, gs["c_q_norm"] = _norm_bwd(p, sm["c_q_norm"], dqn, wb=LANE, cb=PB_CQH, nb=C_HEADS, shared_gain=True,
                                     out_dtype=BF16, name="c_q_norm_bwd")
    dpck, gs["c_k_norm"] = _norm_bwd(p, sm["c_k_norm"], dkn, wb=LANE, cb=PB_CKH, nb=C_KV_HEADS, shared_gain=True,
                                     out_dtype=BF16, name="c_k_norm_bwd")
    table, band = bias
    dqb, dkb, dvb = _flash_bwd(sv["qb"], sv["kb"], sv["vb"], sv["ob"], dob, sv["lse_b"], table, hkv=B_HEADS, g=1,
                               dqk=LANE, q_cb=0, k_cb=0, v_cb=0, v_step=1, scale=HEAD_DIM ** -0.5, tq=t, band=band,
                               name="b_flash_bwd")
    dpbq = _rope(dqb, cos_b, sin_b, tw=LANE, cb=0, nb=B_HEADS, half=HEAD_DIM // 2, sign=-1, out_dtype=BF16, name="b_rope_q_bwd")
    dpbk = _rope(dkb, cos_b, sin_b, tw=LANE, cb=0, nb=B_HEADS, half=HEAD_DIM // 2, sign=-1, out_dtype=BF16, name="b_rope_k_bwd")
    dqa, dka, dva = _flash_bwd(sv["qa"], sv["ka"], sv["kv"], sv["oa"], doa, sv["lse_a"], None, hkv=A_HEADS, g=1,
                               dqk=A_QK, q_cb=0, k_cb=0, v_cb=1, v_step=2, scale=(A_NOPE + A_ROPE) ** -0.5,
                               tq=t, band=None, name="a_flash_bwd")
    dqa_raw = _rope(dqa, cos_a, sin_a, tw=A_QK, cb=0, nb=A_HEADS, half=A_ROPE // 2, sign=-1, out_dtype=BF16, name="a_rope_q_bwd")
    dkv, dkr = _latent_keys_bwd(dka, dva, cos_a, sin_a, name="a_keys_bwd")
    dckv_n = _matmul(dkv, wl["ukv"], mode="nt", out_dtype=F32, name="a_ukv_dx", tm=1024, tn=512)
    gw["ukv"] = _matmul(sv["ckv_n"], dkv, mode="tn", out_dtype=BF16, name="a_ukv_dw", tm=512, tn=1024)
    dcq_n = _matmul(dqa_raw, wl["uq"], mode="nt", out_dtype=F32, name="a_uq_dx", tm=1024, tn=512)
    gw["uq"] = _matmul(sv["cq_n"], dqa_raw, mode="tn", out_dtype=BF16, name="a_uq_dw", tm=512, tn=1024)
    dcq, gs["a_q_norm"] = _norm_bwd(p, sm["a_q_norm"], dcq_n, wb=A_Q_RANK, cb=0, nb=1, shared_gain=True, out_dtype=BF16,
                                    name="a_q_norm_bwd")
    dckv, gs["a_kv_norm"] = _norm_bwd(p, sm["a_kv_norm"], dckv_n, wb=A_KV_RANK, cb=1, nb=1, shared_gain=True,
                                      out_dtype=BF16, name="a_kv_norm_bwd")
    dp = jnp.concatenate([dcq, dckv, dkr, jnp.zeros((s, A_PAD - (PB_KR + 1) * LANE), BF16), dpbq, dpbk,
                          dvb.astype(BF16), dpcq, dpck, dvc.astype(BF16)], axis=1)
    gw["w_all"] = _matmul(dp, sv["h"], mode="tn", out_dtype=BF16, name="in_proj_dw", tm=640, tn=2048)
    token = send_attn(gw)
    dh = _matmul(dp, wl["w_all"], mode="nn", out_dtype=F32, name="in_proj_dx", tm=1024, tn=512, after=token)
    dx, gs["attn_norm"], dxb = _norm_bwd(sv["x"], sm["attn_norm"], dh, wb=d, cb=0, nb=1, shared_gain=True,
                                         out_dtype=F32, name="attn_norm_bwd", add=dx1, bf16_copy=True)
    return dx, dxb, gs, token


def _pack_small(vals):
    flat = jnp.concatenate([vals[n].reshape(-1).astype(F32) for n in _SMALL])
    tile = SUBLANE * LANE
    padded = -(-flat.shape[0] // tile) * tile
    return jnp.pad(flat, (0, padded - flat.shape[0])).reshape(padded // LANE, LANE)


def _unpack_small(packed, like):
    flat = packed.reshape(-1)
    out, off = {}, 0
    for n in _SMALL:
        size = math.prod(like[n].shape)
        out[n] = flat[off:off + size].reshape(like[n].shape)
        off += size
    return out


def kernel(x, attn_norm, w_in, a_q_norm, a_w_uq, a_kv_norm, a_w_ukv, c_q_norm, c_k_norm, out_norm, w_out, ffn_norm, w_gate, w_up, w_down, final_norm, loss_target, m_attn_norm, m_w_in, m_a_q_norm, m_a_w_uq, m_a_kv_norm, m_a_w_ukv, m_c_q_norm, m_c_k_norm, m_out_norm, m_w_out, m_ffn_norm, m_w_gate, m_w_up, m_w_down, m_final_norm, v_attn_norm, v_w_in, v_a_q_norm, v_a_w_uq, v_a_kv_norm, v_a_w_ukv, v_c_q_norm, v_c_k_norm, v_out_norm, v_w_out, v_ffn_norm, v_w_gate, v_w_up, v_w_down, v_final_norm):
    w = dict(attn_norm=attn_norm, w_in=w_in, a_q_norm=a_q_norm, a_w_uq=a_w_uq, a_kv_norm=a_kv_norm, a_w_ukv=a_w_ukv,
             c_q_norm=c_q_norm, c_k_norm=c_k_norm, out_norm=out_norm, w_out=w_out, ffn_norm=ffn_norm, w_gate=w_gate,
             w_up=w_up, w_down=w_down, final_norm=final_norm)
    m = dict(attn_norm=m_attn_norm, w_in=m_w_in, a_q_norm=m_a_q_norm, a_w_uq=m_a_w_uq, a_kv_norm=m_a_kv_norm,
             a_w_ukv=m_a_w_ukv, c_q_norm=m_c_q_norm, c_k_norm=m_c_k_norm, out_norm=m_out_norm, w_out=m_w_out,
             ffn_norm=m_ffn_norm, w_gate=m_w_gate, w_up=m_w_up, w_down=m_w_down, final_norm=m_final_norm)
    v = dict(attn_norm=v_attn_norm, w_in=v_w_in, a_q_norm=v_a_q_norm, a_w_uq=v_a_w_uq, a_kv_norm=v_a_kv_norm,
             a_w_ukv=v_a_w_ukv, c_q_norm=v_c_q_norm, c_k_norm=v_c_k_norm, out_norm=v_out_norm, w_out=v_w_out,
             ffn_norm=v_ffn_norm, w_gate=v_w_gate, w_up=v_w_up, w_down=v_w_down, final_norm=v_final_norm)
    _, s, d = x.shape
    depth = attn_norm.shape[0]

    def as_stored(a, n):
        return jnp.swapaxes(a, 1, 2) if n == "w_in" else a
    t = _pick(s, 1024)

    me = (2 * lax.axis_index("x") + lax.axis_index("y")).astype(jnp.int32).reshape(1)

    gathers, after = {}, me
    for l in range(depth):
        for group, names in (("attn", _ATTN), ("ffn", _FFN)):
            bufs = [_cast_to_slot(as_stored(w[n], n), me, layer=l, name=f"cast_{n}")
                    for n in names]
            send_sems, recv_sems, bufs, _, after = _exchange_start(bufs, None, after, kind="gather",
                                                                   name=f"gather_start_{group}{l}")
            gathers[group, l] = (send_sems, recv_sems, bufs)
    all_started = after

    def gathered(group, l, after):
        send_sems, recv_sems, bufs = gathers[group, l]
        return _exchange_wait(send_sems, recv_sems, bufs, None, after, kind="gather", name=f"gather_wait_{group}{l}")

    tabs = _rope_tables(s)
    bias = _band_table(min(t, ATTN_ROW_CHUNK), s)

    xs = x.reshape(s, d)
    saved, wls, sms = [], [], []
    for l in range(depth):
        wl = _assemble_attn(gathered("attn", l, all_started if l == 0 else xs))
        sm = {n: w[n][l][None, :] for n in _SMALL if n != "final_norm"}
        xs, sv, wl = _layer_fwd(xs, wl, lambda after, l=l: _assemble_ffn(gathered("ffn", l, after)), sm, tabs, bias, t)
        saved.append(sv)
        wls.append(wl)
        sms.append(sm)
    dx, g_final, loss_row, dxb = _final_loss(xs, final_norm[None, :], loss_target.reshape(s, d), name="final_loss")
    loss = lax.psum(loss_row[0, 0], ("x", "y", "c"))

    sends = {}

    def send(group, l, srcs, after):
        lands = [lax.empty((3,) + a.shape[1:], BF16) for a in srcs]
        send_sems, recv_sems, srcs, lands, token = _exchange_start(srcs, lands, after, kind="scatter",
                                                                   name=f"scatter_start_{group}{l}")
        sends[group, l] = (send_sems, recv_sems, srcs, lands)
        return token

    gs_layers, token = [None] * depth, all_started
    for l in reversed(range(depth)):
        dx, dxb, gs_layers[l], token = _layer_bwd(
            dx, dxb, saved[l], wls[l], sms[l], tabs, bias, t,
            lambda gw, l=l, tk=token: send("ffn", l, _split_ffn_grads(gw), tk),
            lambda gw, l=l: send("attn", l, _split_attn_grads(gw), dx))
    grad_x = dx.reshape(x.shape)

    srcs, lands = {}, {}

    def arrive(key, after):
        send_sems, recv_sems, s_bufs, l_bufs = sends[key]
        got = _exchange_wait(send_sems, recv_sems, s_bufs, l_bufs, after, kind="scatter",
                             name=f"scatter_wait_{key[0]}{key[1]}")
        for k, n in enumerate(_ATTN if key[0] == "attn" else _FFN):
            srcs[n, key[1]], lands[n, key[1]] = got[k], got[len(s_bufs) + k]

    def summed(names):
        return [_sum_parts([srcs[n, l] for l in range(depth)], [lands[n, l] for l in range(depth)], me, name="sum_" + n)
                for n in names]

    last = ("attn", 0)
    for key in sends:
        if key != last:
            arrive(key, token)
    sums_ffn = summed(_FFN)
    swap = _exchange_start(sums_ffn, [lax.empty(a.shape, F32) for a in sums_ffn], token, kind="swap",
                           name="swap_start_ffn")
    arrive(last, swap[4])
    sums_attn = summed(_ATTN)
    sib_attn = list(_sibling_exchange(sums_attn, name="swap_core_sums_attn"))
    swapped = _exchange_wait(swap[0], swap[1], swap[2], swap[3], sib_attn[0], kind="swap", name="swap_wait_ffn")
    mine_of = dict(zip(_FFN + _ATTN, swapped[:len(_FFN)] + sums_attn))
    other_of = dict(zip(_FFN + _ATTN, swapped[len(_FFN):] + sib_attn))
    grads, deltas, new_m, new_v = {}, {}, {}, {}
    for n in _BIG:
        res = _adamw(mine_of[n], other_of[n], as_stored(w[n], n), as_stored(m[n], n), as_stored(v[n], n), name="adamw_" + n)
        grads[n], deltas[n], new_m[n], new_v[n] = [as_stored(r, n) for r in res]

    gsm = {n: jnp.stack([gs_layers[l][n][0] for l in range(depth)]) for n in _SMALL if n != "final_norm"}
    gsm["final_norm"] = g_final[0]
    packed = _pack_small(gsm)
    everyone = _all_gather_small(packed, name="gather_gain_grads").reshape(N_DEV, packed.shape[0], LANE)
    res = _small_adamw(everyone, _pack_small(w), _pack_small(m), _pack_small(v), name="adamw_gains")
    for dst, r in zip((grads, deltas, new_m, new_v), res):
        dst.update(_unpack_small(r, w))

    return (loss, grad_x, *[grads[n] for n in _WEIGHTS], *[deltas[n] for n in _WEIGHTS],
            *[new_m[n] for n in _WEIGHTS], *[new_v[n] for n in _WEIGHTS])
```

```python
import functools
import math

import jax
import jax.numpy as jnp
import numpy as np
from jax import lax
from jax.experimental import pallas as pl
from jax.experimental.pallas import tpu as pltpu

F32 = jnp.float32
BF16 = jnp.bfloat16
MESH = pl.DeviceIdType.MESH

HEAD_DIM = 128
ROPE_THETA = 10000.0
GRID_W = 64
EPS = 1e-6
NEG = -1e30
A_HEADS, A_Q_RANK, A_KV_RANK, A_NOPE, A_ROPE, A_V = 4, 512, 512, 128, 64, 128
B_HEADS = 6
B_PATTERNS = ((128, 1), (512, 4), (2048, 16))
C_HEADS, C_KV_HEADS = 6, 2
C_GROUP = C_HEADS // C_KV_HEADS
A_WIDTH, B_WIDTH, C_WIDTH = A_HEADS * A_V, B_HEADS * HEAD_DIM, C_HEADS * HEAD_DIM
IN_A = A_Q_RANK + A_KV_RANK + A_ROPE
IN_B = 3 * B_WIDTH
IN_C = C_WIDTH + 2 * C_KV_HEADS * HEAD_DIM
ADAM_LR, ADAM_B1, ADAM_B2, ADAM_EPS, ADAM_WD, ADAM_STEP = 0.001, 0.9, 0.999, 1e-08, 0.01, 10

LANE = 128
SUBLANE = 8
VMEM_BYTES_V7X = 64 * 1024 * 1024
VMEM_LIMIT_CAP = VMEM_BYTES_V7X - 8 * 1024 * 1024
N_CHIPS = 4
N_DEV = 8

A_PAD = 12 * LANE
PB_CQ, PB_CKV, PB_KR = 0, 4, 8
PB_BQ, PB_BK, PB_BV = 12, 18, 24
PB_CQH, PB_CKH, PB_CVH = 30, 36, 38
NP = 40 * LANE
A_QK = 2 * LANE


def _pick(n, cap, mult=LANE):
    if n <= cap:
        return n
    t = cap - cap % mult
    while t >= mult:
        if n % t == 0:
            return t
        t -= mult
    return n


def _rows_for(width_bytes, n_rows, target=2 * 1024 * 1024):
    return _pick(n_rows, max(SUBLANE, target // max(width_bytes, 1)), SUBLANE)


def _tile2(rows, cols, target):
    tc = _pick(cols, 4 * LANE)
    if tc < 4 * LANE:
        tc = cols
    fits = [t for t in range(SUBLANE, rows + 1, SUBLANE) if rows % t == 0] or [rows]
    return min(fits, key=lambda t: abs(math.log(t * tc * 4 / target))), tc


def _params(est_bytes):
    limit = int(min(max(est_bytes + (4 << 20), 32 << 20), VMEM_LIMIT_CAP))
    return pltpu.CompilerParams(vmem_limit_bytes=limit)


def _isz(x):
    return jnp.dtype(x.dtype).itemsize


def _hbm(shape, dtype):
    return pltpu.HBM(shape, dtype)


def _pin(*arrays):
    return [pltpu.with_memory_space_constraint(a, pltpu.HBM) for a in arrays]


_DIMS = {"nn": (((1,), (0,)), ((), ())), "nt": (((1,), (1,)), ((), ())), "tn": (((0,), (0,)), ((), ()))}


def _matmul(a, b, *, mode, out_dtype, name, add=None, tm=512, tn=512, col_shards=False, after=None):
    if mode == "tn":
        (k, m), (k2, n) = a.shape, b.shape
    elif mode == "nt":
        (m, k), (n, k2) = a.shape, b.shape
    else:
        (m, k), (k2, n) = a.shape, b.shape
    assert k == k2, (a.shape, b.shape, mode)
    tm, tn = _pick(m, tm), (n // N_CHIPS if col_shards else _pick(n, tn))
    a_spec = pl.BlockSpec((k, tm), lambda i, j: (0, i)) if mode == "tn" else pl.BlockSpec((tm, k), lambda i, j: (i, 0))
    b_spec = pl.BlockSpec((tn, k), lambda i, j: (j, 0)) if mode == "nt" else pl.BlockSpec((k, tn), lambda i, j: (0, j))
    o_spec = pl.BlockSpec((None, tm, tn), lambda i, j: (j, i, 0)) if col_shards else pl.BlockSpec((tm, tn), lambda i, j: (i, j))
    dims = _DIMS[mode]

    def body(*refs):
        a_ref, b_ref, o_ref = refs[0], refs[1], refs[-1]
        acc = lax.dot_general(a_ref[...].astype(BF16), b_ref[...].astype(BF16), dims, preferred_element_type=F32)
        if add is not None:
            acc = acc + refs[2][...].astype(F32)
        o_ref[...] = acc.astype(out_dtype)

    ins, specs = [a, b], [a_spec, b_spec]
    if add is not None:
        ins.append(add)
        specs.append(o_spec)
    if after is not None:
        ins.append(after)
        specs.append(pl.BlockSpec(memory_space=pl.ANY))
    est = 2 * (tm * k * _isz(a) + tn * k * _isz(b) + tm * tn * (jnp.dtype(out_dtype).itemsize + (4 if add is not None else 0)))
    est += (tm + tn) * k * 2 + 2 * tm * tn * 4
    return pl.pallas_call(
        body, name=name, grid=(m // tm, n // tn), in_specs=specs, out_specs=o_spec,
        out_shape=_hbm((N_CHIPS, m, tn) if col_shards else (m, n), out_dtype),
        compiler_params=_params(est),
    )(*_pin(*ins))


def _ffn_up(h, wg, wu, *, name):
    s, d = h.shape
    _, _, c = wg.shape
    tm = _pick(s, 512, SUBLANE)

    def body(h_ref, wg_ref, wu_ref, g_ref, u_ref, a_ref):
        hv = h_ref[...]
        gv = jnp.dot(hv, wg_ref[...], preferred_element_type=F32)
        uv = jnp.dot(hv, wu_ref[...], preferred_element_type=F32)
        g_ref[...] = gv.astype(BF16)
        u_ref[...] = uv.astype(BF16)
        a_ref[...] = (gv / (1.0 + jnp.exp(-gv)) * uv).astype(BF16)

    w_spec = pl.BlockSpec((None, d, c), lambda j, i: (j, 0, 0))
    o_spec = pl.BlockSpec((tm, c), lambda j, i: (i, j))
    est = 2 * (tm * d * 2 + 2 * d * c * 2 + tm * c * 10) + 4 * tm * c * 4
    return pl.pallas_call(
        body, name=name, grid=(N_CHIPS, s // tm), in_specs=[pl.BlockSpec((tm, d), lambda j, i: (i, 0)), w_spec, w_spec],
        out_specs=[o_spec, o_spec, o_spec],
        out_shape=[_hbm((s, N_CHIPS * c), BF16)] * 3,
        compiler_params=_params(est),
    )(*_pin(h, wg, wu))


def _ffn_down_dx(dx, w_down, gate, up, *, name):
    s, d = dx.shape
    f = w_down.shape[0]
    tm, tn = _pick(s, 1024, SUBLANE), _pick(f, 512)

    def body(dx_ref, w_ref, g_ref, u_ref, dg_ref, du_ref):
        dact = lax.dot_general(dx_ref[...], w_ref[...], _DIMS["nt"], preferred_element_type=F32)
        gv, uv = g_ref[...].astype(F32), u_ref[...].astype(F32)
        sig = 1.0 / (1.0 + jnp.exp(-gv))
        dg_ref[...] = (dact * uv * (sig * (1.0 + gv * (1.0 - sig)))).astype(BF16)
        du_ref[...] = (dact * (gv * sig)).astype(BF16)

    t_spec = pl.BlockSpec((tm, tn), lambda i, j: (i, j))
    est = 2 * (tm * d * 2 + tn * d * 2 + tm * tn * 12) + 6 * tm * tn * 4
    return pl.pallas_call(
        body, name=name, grid=(s // tm, f // tn),
        in_specs=[pl.BlockSpec((tm, d), lambda i, j: (i, 0)), pl.BlockSpec((tn, d), lambda i, j: (j, 0)), t_spec, t_spec],
        out_specs=[t_spec, t_spec], out_shape=[_hbm((s, f), BF16)] * 2, compiler_params=_params(est),
    )(*_pin(dx, w_down, gate, up))


def _ffn_up_dx(dgate, dup, wg, wu, *, name):
    s, f = dgate.shape
    _, d, c = wg.shape
    tm, tn = _pick(s, 1024, SUBLANE), _pick(d, 1024)
    nk = 2 * N_CHIPS

    def body(dg_ref, du_ref, wg_ref, wu_ref, o_ref, acc):
        kk = pl.program_id(2)

        @pl.when(kk == 0)
        def _():
            acc[...] = jnp.zeros_like(acc)

        @pl.when(kk < N_CHIPS)
        def _():
            acc[...] += lax.dot_general(dg_ref[...], wg_ref[...], _DIMS["nt"], preferred_element_type=F32)

        @pl.when(kk >= N_CHIPS)
        def _():
            acc[...] += lax.dot_general(du_ref[...], wu_ref[...], _DIMS["nt"], preferred_element_type=F32)

        @pl.when(kk == nk - 1)
        def _():
            o_ref[...] = acc[...]

    last = N_CHIPS - 1
    est = 2 * (2 * tm * c * 2 + 2 * tn * c * 2 + tm * tn * 4) + 2 * tm * tn * 4
    return pl.pallas_call(
        body, name=name, grid=(s // tm, d // tn, nk),
        in_specs=[pl.BlockSpec((tm, c), lambda i, j, kk: (i, jnp.minimum(kk, last))),
                  pl.BlockSpec((tm, c), lambda i, j, kk: (i, jnp.maximum(kk - N_CHIPS, 0))),
                  pl.BlockSpec((None, tn, c), lambda i, j, kk: (jnp.minimum(kk, last), j, 0)),
                  pl.BlockSpec((None, tn, c), lambda i, j, kk: (jnp.maximum(kk - N_CHIPS, 0), j, 0))],
        out_specs=pl.BlockSpec((tm, tn), lambda i, j, kk: (i, j)),
        out_shape=_hbm((s, d), F32), scratch_shapes=[pltpu.VMEM((tm, tn), F32)],
        compiler_params=_params(est),
    )(*_pin(dgate, dup, wg, wu))


def _norm_fwd(x, gain, *, wb, cb, nb, shared_gain, out_dtype, name):
    s = x.shape[0]
    ts = _rows_for(wb * 4, s)

    def body(x_ref, g_ref, o_ref):
        xv = x_ref[...].astype(F32)
        r = lax.rsqrt(jnp.mean(xv * xv, axis=1, keepdims=True) + EPS)
        o_ref[...] = ((xv * r) * g_ref[...]).astype(out_dtype)

    return pl.pallas_call(
        body, name=name, grid=(nb, s // ts),
        in_specs=[pl.BlockSpec((ts, wb), lambda n, i: (i, cb + n)),
                  pl.BlockSpec((1, wb), (lambda n, i: (0, 0)) if shared_gain else (lambda n, i: (0, n)))],
        out_specs=pl.BlockSpec((ts, wb), lambda n, i: (i, n)),
        out_shape=_hbm((s, nb * wb), out_dtype), compiler_params=_params(6 * ts * wb * 4),
    )(*_pin(x), gain)


def _norm_bwd(x, gain, dy, *, wb, cb, nb, shared_gain, out_dtype, name, dy_cb=0, add=None, bf16_copy=False):
    s = x.shape[0]
    ts = _rows_for(wb * 4, s, target=1024 * 1024)
    gw = wb if shared_gain else nb * wb

    def body(*refs):
        refs = list(refs)
        dxb_ref = refs.pop() if bf16_copy else None
        if add is None:
            x_ref, g_ref, dy_ref, dx_ref, dg_ref = refs
        else:
            x_ref, g_ref, dy_ref, add_ref, dx_ref, dg_ref = refs
        n, i = pl.program_id(0), pl.program_id(1)
        xv = x_ref[...].astype(F32)
        dyv = dy_ref[...].astype(F32)
        r = lax.rsqrt(jnp.mean(xv * xv, axis=1, keepdims=True) + EPS)
        xh = xv * r
        dyg = dyv * g_ref[...]
        dx = r * (dyg - xh * jnp.mean(dyg * xh, axis=1, keepdims=True))
        if add is not None:
            dx = dx + add_ref[...]
        dx_ref[...] = dx.astype(out_dtype)
        if bf16_copy:
            dxb_ref[...] = dx.astype(BF16)
        first = jnp.logical_and(n == 0, i == 0) if shared_gain else (i == 0)

        @pl.when(first)
        def _():
            dg_ref[...] = jnp.zeros_like(dg_ref)

        dg_ref[...] += jnp.sum(dyv * xh, axis=0, keepdims=True)

    ins = [x, gain, dy]
    specs = [pl.BlockSpec((ts, wb), lambda n, i: (i, cb + n)),
             pl.BlockSpec((1, wb), (lambda n, i: (0, 0)) if shared_gain else (lambda n, i: (0, n))),
             pl.BlockSpec((ts, wb), lambda n, i: (i, dy_cb + n))]
    if add is not None:
        ins.append(add)
        specs.append(pl.BlockSpec((ts, wb), lambda n, i: (i, n)))
    out_specs = [pl.BlockSpec((ts, wb), lambda n, i: (i, n)),
                 pl.BlockSpec((1, wb), (lambda n, i: (0, 0)) if shared_gain else (lambda n, i: (0, n)))]
    out_shape = [_hbm((s, nb * wb), out_dtype), jax.ShapeDtypeStruct((1, gw), F32)]
    if bf16_copy:
        out_specs.append(out_specs[0])
        out_shape.append(_hbm((s, nb * wb), BF16))
    return pl.pallas_call(
        body, name=name, grid=(nb, s // ts), in_specs=specs, out_specs=out_specs, out_shape=out_shape,
        compiler_params=_params(14 * ts * wb * 4),
    )(*_pin(*ins))


def _swap_halves(x, half):
    if 2 * half == LANE:
        return pltpu.roll(x, half, axis=1)
    lane = lax.broadcasted_iota(jnp.int32, x.shape, 1)
    first = jnp.bitwise_and(lane, 2 * half - 1) < half
    return jnp.where(first, pltpu.roll(x, LANE - half, axis=1), pltpu.roll(x, half, axis=1))


def _rope(x, cos_t, sin_t, *, tw, cb, nb, half, sign, out_dtype, name):
    s = x.shape[0]
    ts = _rows_for(tw * 4, s)

    def body(x_ref, c_ref, s_ref, o_ref):
        for q in range(tw // LANE):
            sl = slice(q * LANE, (q + 1) * LANE)
            xv = x_ref[:, sl].astype(F32)
            sv = s_ref[:, sl]
            if sign < 0:
                sv = -sv
            o_ref[:, sl] = (xv * c_ref[:, sl] + _swap_halves(xv, half) * sv).astype(out_dtype)

    return pl.pallas_call(
        body, name=name, grid=(nb, s // ts),
        in_specs=[pl.BlockSpec((ts, tw), lambda n, i: (i, cb + n)),
                  pl.BlockSpec((ts, tw), lambda n, i: (i, 0)),
                  pl.BlockSpec((ts, tw), lambda n, i: (i, 0))],
        out_specs=pl.BlockSpec((ts, tw), lambda n, i: (i, n)),
        out_shape=_hbm((s, nb * tw), out_dtype), compiler_params=_params(10 * ts * tw * 4),
    )(*_pin(x), cos_t, sin_t)


def _latent_keys(kv, p, cos_t, sin_t, *, kr_cb, name):
    s = kv.shape[0]
    ts = _rows_for(A_QK * 4, s)

    def body(kv_ref, kr_ref, c_ref, s_ref, o_ref):
        o_ref[:, :LANE] = kv_ref[...].astype(BF16)
        x = kr_ref[...].astype(F32)
        o_ref[:, LANE:] = (x * c_ref[:, LANE:] + _swap_halves(x, A_ROPE // 2) * s_ref[:, LANE:]).astype(BF16)

    tab = pl.BlockSpec((ts, A_QK), lambda n, i: (i, 0))
    return pl.pallas_call(
        body, name=name, grid=(A_HEADS, s // ts),
        in_specs=[pl.BlockSpec((ts, LANE), lambda n, i: (i, 2 * n)), pl.BlockSpec((ts, LANE), lambda n, i: (i, kr_cb)),
                  tab, tab],
        out_specs=pl.BlockSpec((ts, A_QK), lambda n, i: (i, n)),
        out_shape=_hbm((s, A_HEADS * A_QK), BF16), compiler_params=_params(10 * ts * A_QK * 4),
    )(*_pin(kv, p), cos_t, sin_t)


def _latent_keys_bwd(dka, dva, cos_t, sin_t, *, name):
    s = dka.shape[0]
    ts = _rows_for(A_HEADS * A_QK * 4, s)

    def body(dka_ref, dva_ref, c_ref, s_ref, dkv_ref, dkr_ref):
        acc = jnp.zeros((ts, LANE), F32)
        for h in range(A_HEADS):
            dkv_ref[:, h * A_QK:h * A_QK + LANE] = dka_ref[:, h * A_QK:h * A_QK + LANE].astype(BF16)
            dkv_ref[:, h * A_QK + LANE:(h + 1) * A_QK] = dva_ref[:, h * LANE:(h + 1) * LANE].astype(BF16)
            y = dka_ref[:, h * A_QK + LANE:(h + 1) * A_QK]
            acc = acc + (y * c_ref[:, LANE:] - _swap_halves(y, A_ROPE // 2) * s_ref[:, LANE:])
        dkr_ref[...] = acc.astype(BF16)

    def rows(width):
        return pl.BlockSpec((ts, width), lambda i: (i, 0))

    return pl.pallas_call(
        body, name=name, grid=(s // ts,),
        in_specs=[rows(A_HEADS * A_QK), rows(A_HEADS * LANE), rows(A_QK), rows(A_QK)],
        out_specs=[rows(A_HEADS * A_QK), rows(LANE)],
        out_shape=[_hbm((s, A_HEADS * A_QK), BF16), _hbm((s, LANE), BF16)],
        compiler_params=_params(8 * ts * A_HEADS * A_QK * 4),
    )(*_pin(dka, dva), cos_t, sin_t)


def _cast_cols(x, *, cb, nb, name):
    s = x.shape[0]
    ts = _rows_for(LANE * 4, s)

    def body(x_ref, o_ref):
        o_ref[...] = x_ref[...].astype(BF16)

    return pl.pallas_call(
        body, name=name, grid=(nb, s // ts), in_specs=[pl.BlockSpec((ts, LANE), lambda n, i: (i, cb + n))],
        out_specs=pl.BlockSpec((ts, LANE), lambda n, i: (i, n)),
        out_shape=_hbm((s, nb * LANE), BF16), compiler_params=_params(4 * ts * LANE * 4),
    )(*_pin(x))


LOG2E = 1.4426950408889634
ATTN_ROW_CHUNK = 256


def _attn_window(i, rc, s, band):
    w, r = band
    start = jnp.clip(i * rc - r, 0, s - w)
    return pl.multiple_of(start, rc), pl.multiple_of((w - rc) - (i * rc - start), LANE)


def _flash_fwd(q, k, v, table, *, hkv, g, dqk, q_cb, k_cb, v_cb, v_step, scale, tq, band, name):
    s = q.shape[0]
    n = s // tq
    hq = hkv * g
    rc = min(tq, ATTN_ROW_CHUNK)
    w = s if band is None else band[0]

    def body(*refs):
        if band is None:
            q_ref, k_ref, v_ref, o_ref, lse_ref = refs
            kw, vw = k_ref[...], v_ref[...]
        else:
            q_ref, k_ref, v_ref, t_ref, o_ref, lse_ref = refs
        for c in range(tq // rc):
            rows = slice(c * rc, (c + 1) * rc)
            if band is not None:
                start, u = _attn_window(pl.program_id(1) * (tq // rc) + c, rc, s, band)
                kw, vw = k_ref[pl.ds(start, w), :], v_ref[pl.ds(start, w), :]
            sc = lax.dot_general(q_ref[rows, :], kw, _DIMS["nt"], preferred_element_type=F32) * (scale * LOG2E)
            if band is not None:
                sc = sc + t_ref[:, pl.ds(u, w)]
            m = jnp.max(sc, axis=1, keepdims=True)
            p = jnp.exp2(sc - m)
            l = jnp.sum(p, axis=1, keepdims=True)
            o_ref[rows, :] = jnp.dot(p.astype(BF16), vw, preferred_element_type=F32) / l
            lse_ref[0, rows, :] = jnp.broadcast_to(m + jnp.log2(l), (rc, LANE))

    ins = [q, k, v]
    specs = [pl.BlockSpec((tq, dqk), lambda h, i: (i, q_cb + h)),
             pl.BlockSpec((s, dqk), lambda h, i: (0, k_cb + h // g)),
             pl.BlockSpec((s, LANE), lambda h, i: (0, v_cb + v_step * (h // g)))]
    if band is not None:
        ins.append(table)
        specs.append(pl.BlockSpec(table.shape, lambda h, i: (0, 0)))
    est = 4 * s * (dqk + LANE) + 6 * rc * w * 4 + 8 * tq * LANE * 4 + (0 if band is None else 2 * table.size * 4)
    return pl.pallas_call(
        body, name=name, grid=(hq, n), in_specs=specs,
        out_specs=[pl.BlockSpec((tq, LANE), lambda h, i: (i, h)), pl.BlockSpec((1, tq, LANE), lambda h, i: (h, i, 0))],
        out_shape=[_hbm((s, hq * LANE), F32), _hbm((hq, s, LANE), F32)],
        compiler_params=_params(est),
    )(*_pin(*ins))


def _flash_bwd(q, k, v, o, do, lse, table, *, hkv, g, dqk, q_cb, k_cb, v_cb, v_step, scale, tq, band, name):
    s = q.shape[0]
    n = s // tq
    hq = hkv * g
    rc = min(tq, ATTN_ROW_CHUNK)
    w = s if band is None else band[0]

    def body(*refs):
        if band is None:
            q_ref, k_ref, v_ref, o_ref, do_ref, lse_ref, dq_ref, dk_ref, dv_ref = refs
            keys = slice(None)
            kw, vw = k_ref[...], v_ref[...]
        else:
            q_ref, k_ref, v_ref, o_ref, do_ref, lse_ref, t_ref, dq_ref, dk_ref, dv_ref = refs
        h, i = pl.program_id(0), pl.program_id(1)

        @pl.when(jnp.logical_and(h % g == 0, i == 0))
        def _():
            dk_ref[...] = jnp.zeros_like(dk_ref)
            dv_ref[...] = jnp.zeros_like(dv_ref)

        for c in range(tq // rc):
            rows = slice(c * rc, (c + 1) * rc)
            if band is not None:
                start, u = _attn_window(i * (tq // rc) + c, rc, s, band)
                keys = pl.ds(start, w)
                kw, vw = k_ref[keys, :], v_ref[keys, :]
            qv = q_ref[rows, :]
            dof = do_ref[rows, :]
            dov = dof.astype(BF16)
            sc = lax.dot_general(qv, kw, _DIMS["nt"], preferred_element_type=F32) * (scale * LOG2E)
            if band is not None:
                sc = sc + t_ref[:, pl.ds(u, w)]
            p = jnp.exp2(sc - lse_ref[0, rows, 0:1])
            dp = lax.dot_general(dov, vw, _DIMS["nt"], preferred_element_type=F32)
            delta = jnp.sum(dof * o_ref[rows, :], axis=1, keepdims=True)
            ds = (p * (dp - delta) * scale).astype(BF16)
            dv_ref[keys, :] += lax.dot_general(p.astype(BF16), dov, _DIMS["tn"], preferred_element_type=F32)
            dk_ref[keys, :] += lax.dot_general(ds, qv, _DIMS["tn"], preferred_element_type=F32)
            dq_ref[rows, :] = jnp.dot(ds, kw, preferred_element_type=F32)

    ins = [q, k, v, o, do, lse]
    specs = [pl.BlockSpec((tq, dqk), lambda h, i: (i, q_cb + h)),
             pl.BlockSpec((s, dqk), lambda h, i: (0, k_cb + h // g)),
             pl.BlockSpec((s, LANE), lambda h, i: (0, v_cb + v_step * (h // g))),
             pl.BlockSpec((tq, LANE), lambda h, i: (i, h)),
             pl.BlockSpec((tq, LANE), lambda h, i: (i, h)),
             pl.BlockSpec((1, tq, LANE), lambda h, i: (h, i, 0))]
    if band is not None:
        ins.append(table)
        specs.append(pl.BlockSpec(table.shape, lambda h, i: (0, 0)))
    est = (4 + 8) * s * (dqk + LANE) + 10 * rc * w * 4 + 12 * tq * LANE * 4 + (0 if band is None else 2 * table.size * 4)
    return pl.pallas_call(
        body, name=name, grid=(hq, n), in_specs=specs,
        out_specs=[pl.BlockSpec((tq, dqk), lambda h, i: (i, h)),
                   pl.BlockSpec((s, dqk), lambda h, i: (0, h // g)),
                   pl.BlockSpec((s, LANE), lambda h, i: (0, h // g))],
        out_shape=[_hbm((s, hq * dqk), F32), _hbm((s, hkv * dqk), F32),
                   _hbm((s, hkv * LANE), F32)],
        compiler_params=_params(est),
    )(*_pin(*ins))


def _final_loss(x, gain, target, *, name):
    s, d = x.shape
    ts = _rows_for(d * 4, s, target=1024 * 1024)

    def body(x_ref, g_ref, t_ref, dx_ref, dg_ref, loss_ref, dxb_ref):
        i = pl.program_id(0)
        xv = x_ref[...]
        gv = g_ref[...]
        r = lax.rsqrt(jnp.mean(xv * xv, axis=1, keepdims=True) + EPS)
        xh = xv * r
        err = xh * gv - t_ref[...]
        dy = err / d
        dyg = dy * gv
        dx = r * (dyg - xh * jnp.mean(dyg * xh, axis=1, keepdims=True))
        dx_ref[...] = dx
        dxb_ref[...] = dx.astype(BF16)

        @pl.when(i == 0)
        def _():
            dg_ref[...] = jnp.zeros_like(dg_ref)
            loss_ref[...] = jnp.zeros_like(loss_ref)

        dg_ref[...] += jnp.sum(dy * xh, axis=0, keepdims=True)
        part = jnp.sum(jnp.mean(err * err, axis=1, keepdims=True), axis=0, keepdims=True)
        loss_ref[...] += jnp.broadcast_to(0.5 * part, (1, LANE))

    row = pl.BlockSpec((ts, d), lambda i: (i, 0))
    return pl.pallas_call(
        body, name=name, grid=(s // ts,),
        in_specs=[row, pl.BlockSpec((1, d), lambda i: (0, 0)), row],
        out_specs=[row, pl.BlockSpec((1, d), lambda i: (0, 0)), pl.BlockSpec((1, LANE), lambda i: (0, 0)), row],
        out_shape=[_hbm((s, d), F32), jax.ShapeDtypeStruct((1, d), F32),
                   jax.ShapeDtypeStruct((1, LANE), F32), _hbm((s, d), BF16)],
        compiler_params=_params(14 * ts * d * 4),
    )(*_pin(x), gain, *_pin(target))


def _cast_to_slot(x3d, me, *, layer, name):
    _, rows, c = x3d.shape
    tr, tc = _tile2(rows, c, 2 * 1024 * 1024)

    def body(me_ref, x_ref, o_ref):
        o_ref[...] = x_ref[...].astype(BF16)

    return pl.pallas_call(
        body, name=name,
        grid_spec=pltpu.PrefetchScalarGridSpec(
            num_scalar_prefetch=1, grid=(rows // tr, c // tc),
            in_specs=[pl.BlockSpec((None, tr, tc), lambda i, j, me_ref: (layer, i, j))],
            out_specs=pl.BlockSpec((None, tr, tc), lambda i, j, me_ref: (me_ref[0], i, j))),
        out_shape=_hbm((N_CHIPS, rows, c), BF16), compiler_params=_params(6 * tr * tc * 4),
    )(me, *_pin(x3d))


def _sum_parts(srcs, lands, me, *, name):
    depth = len(srcs)
    _, r, c = srcs[0].shape
    tr, tc = _tile2(r, c, 1024 * 1024)
    nt, nc = r // tr, c // tc

    def body(me_ref, *refs):
        o_ref = refs[-1]
        l = pl.program_id(0)
        for k in range(depth):
            @pl.when(l == k)
            def _(k=k):
                acc = refs[k][...].astype(F32)
                for p in range(3):
                    acc = acc + refs[depth + k][p].astype(F32)
                o_ref[...] = acc

    def tile_of(k):
        def f(l, i, j):
            return (jnp.where(l == k, i, jnp.where(l < k, 0, nt - 1)), jnp.where(l == k, j, jnp.where(l < k, 0, nc - 1)))
        return f

    in_specs = [pl.BlockSpec((None, tr, tc), functools.partial(lambda l, i, j, me_ref, f: (me_ref[0], *f(l, i, j)), f=tile_of(k)))
                for k in range(depth)]
    in_specs += [pl.BlockSpec((3, tr, tc), functools.partial(lambda l, i, j, me_ref, f: (0, *f(l, i, j)), f=tile_of(k)))
                 for k in range(depth)]
    return pl.pallas_call(
        body, name=name,
        grid_spec=pltpu.PrefetchScalarGridSpec(
            num_scalar_prefetch=1, grid=(depth, nt, nc), in_specs=in_specs,
            out_specs=pl.BlockSpec((tr, tc), lambda l, i, j, me_ref: (l * nt + i, j))),
        out_shape=_hbm((depth * r, c), F32), compiler_params=_params(depth * 10 * tr * tc * 4),
    )(me, *_pin(*srcs, *lands))


def _adamw_math(w, g, m, v):
    m2 = ADAM_B1 * m + (1.0 - ADAM_B1) * g
    v2 = ADAM_B2 * v + (1.0 - ADAM_B2) * (g * g)
    m_hat = m2 / (1.0 - ADAM_B1 ** ADAM_STEP)
    v_hat = v2 / (1.0 - ADAM_B2 ** ADAM_STEP)
    delta = -ADAM_LR * (m_hat / (jnp.sqrt(v_hat) + ADAM_EPS) + ADAM_WD * w)
    return delta, m2, v2


def _adamw(g_a, g_b, w, m, v, *, name):
    depth, r, c = w.shape
    tr, tc = _tile2(r, c, 512 * 1024)
    nt = r // tr

    def body(a_ref, b_ref, w_ref, m_ref, v_ref, g_out, d_out, m_out, v_out):
        gv = a_ref[...] + b_ref[...]
        delta, m2, v2 = _adamw_math(w_ref[...], gv, m_ref[...], v_ref[...])
        g_out[...] = gv
        d_out[...] = delta
        m_out[...] = m2
        v_out[...] = v2

    flat = pl.BlockSpec((tr, tc), lambda l, i, j: (l * nt + i, j))
    spec = pl.BlockSpec((None, tr, tc), lambda l, i, j: (l, i, j))
    return pl.pallas_call(
        body, name=name, grid=(depth, nt, c // tc), in_specs=[flat, flat, spec, spec, spec], out_specs=[spec] * 4,
        out_shape=[_hbm((depth, r, c), F32)] * 4, compiler_params=_params(22 * tr * tc * 4),
    )(*_pin(g_a, g_b, w, m, v))


def _small_adamw(g_all, w, m, v, *, name):
    r, c = w.shape

    def body(ga_ref, w_ref, m_ref, v_ref, g_out, d_out, m_out, v_out):
        gv = ga_ref[0]
        for j in range(1, N_DEV):
            gv = gv + ga_ref[j]
        delta, m2, v2 = _adamw_math(w_ref[...], gv, m_ref[...], v_ref[...])
        g_out[...] = gv
        d_out[...] = delta
        m_out[...] = m2
        v_out[...] = v2

    return pl.pallas_call(body, name=name, out_shape=[jax.ShapeDtypeStruct((r, c), F32)] * 4)(g_all, w, m, v)


_ANY = pl.BlockSpec(memory_space=pl.ANY)


_HBM = pl.BlockSpec(memory_space=pltpu.HBM)
_SEM = pl.BlockSpec(memory_space=pltpu.SEMAPHORE)
_EFFECT = pltpu.SideEffectType.DATAFLOW_SIDE_EFFECTING


def _peer_chips():
    x, y = lax.axis_index("x"), lax.axis_index("y")
    return 2 * x + y, [(1 - x, y), (x, 1 - y), (1 - x, 1 - y)]


def _exchange_copy(srcs, lands, send_sems, recv_sems, k, p, kind):
    c = lax.axis_index("c")
    if kind == "swap":
        return pltpu.make_async_remote_copy(
            src_ref=srcs[k], dst_ref=lands[k], send_sem=send_sems.at[k], recv_sem=recv_sems.at[k],
            device_id=(lax.axis_index("x"), lax.axis_index("y"), 1 - c), device_id_type=MESH)
    me, peers = _peer_chips()
    px, py = peers[p]
    return pltpu.make_async_remote_copy(
        src_ref=srcs[k].at[2 * px + py] if kind == "scatter" else srcs[k].at[me],
        dst_ref=lands[k].at[p] if kind == "scatter" else lands[k].at[me],
        send_sem=send_sems.at[3 * k + p], recv_sem=recv_sems.at[3 * k + p],
        device_id=(px, py, c), device_id_type=MESH)


def _exchange_start(srcs, lands, after, *, kind, name):
    n = len(srcs)
    npeer = 1 if kind == "swap" else 3
    bufs = list(srcs) + (list(lands) if lands is not None else [])
    nb = len(bufs)

    def body(*refs):
        buf_refs, send_sems, recv_sems = refs[:nb], refs[nb + 1], refs[nb + 2]
        token = refs[-1]
        s_refs = buf_refs[:n]
        l_refs = buf_refs[n:] if lands is not None else s_refs
        for k in range(n):
            for p in range(npeer):
                _exchange_copy(s_refs, l_refs, send_sems, recv_sems, k, p, kind).start()
        token[...] = jnp.zeros_like(token)

    out = pl.pallas_call(
        body, name=name,
        out_shape=(pltpu.SemaphoreType.DMA((npeer * n,)), pltpu.SemaphoreType.DMA((npeer * n,)),
                   *[pltpu.HBM(b.shape, b.dtype) for b in bufs], jax.ShapeDtypeStruct((SUBLANE, LANE), F32)),
        in_specs=[_HBM] * nb + [_ANY],
        out_specs=(_SEM, _SEM, *[_HBM] * nb, pl.BlockSpec(memory_space=pltpu.VMEM)),
        input_output_aliases={i: 2 + i for i in range(nb)},
        compiler_params=pltpu.CompilerParams(has_side_effects=_EFFECT),
    )(*[pltpu.with_memory_space_constraint(b, pltpu.HBM) for b in bufs], after)
    send_sems, recv_sems = out[0], out[1]
    thru = out[2:2 + nb]
    return send_sems, recv_sems, list(thru[:n]), (list(thru[n:]) if lands is not None else None), out[-1]


def _exchange_wait(send_sems, recv_sems, srcs, lands, after, *, kind, name):
    n = len(srcs)
    npeer = 1 if kind == "swap" else 3
    bufs = list(srcs) + (list(lands) if lands is not None else [])
    nb = len(bufs)

    def body(*refs):
        buf_refs, send_sems_ref, recv_sems_ref = refs[:nb], refs[nb], refs[nb + 1]
        s_refs = buf_refs[:n]
        l_refs = buf_refs[n:] if lands is not None else s_refs
        for k in range(n):
            for p in range(npeer):
                cp = _exchange_copy(s_refs, l_refs, send_sems_ref, recv_sems_ref, k, p, kind)
                cp.wait_send()
                cp.wait_recv()

    out = pl.pallas_call(
        body, name=name, out_shape=tuple(pltpu.HBM(b.shape, b.dtype) for b in bufs),
        in_specs=[_HBM] * nb + [_SEM, _SEM, _ANY], out_specs=tuple([_HBM] * nb),
        input_output_aliases={i: i for i in range(nb)},
        compiler_params=pltpu.CompilerParams(has_side_effects=_EFFECT),
    )(*bufs, send_sems, recv_sems, after)
    return list(out)


def _sibling_exchange(srcs, *, name):
    n = len(srcs)

    def body(*refs):
        src, out = refs[:n], refs[n:2 * n]
        send_sems, recv_sems = refs[2 * n:]
        sibling = (lax.axis_index("x"), lax.axis_index("y"), 1 - lax.axis_index("c"))
        copies = [pltpu.make_async_remote_copy(src_ref=src[k], dst_ref=out[k], send_sem=send_sems.at[k],
                                               recv_sem=recv_sems.at[k], device_id=sibling, device_id_type=MESH)
                  for k in range(n)]
        for cp in copies:
            cp.start()
        for cp in copies:
            cp.wait_recv()
        for cp in copies:
            cp.wait_send()

    return pl.pallas_call(
        body, name=name, in_specs=[_ANY] * n, out_specs=[_ANY] * n,
        out_shape=[jax.ShapeDtypeStruct(a.shape, a.dtype) for a in srcs],
        scratch_shapes=[pltpu.SemaphoreType.DMA((n,)), pltpu.SemaphoreType.DMA((n,))],
    )(*srcs)


def _all_gather_small(block, *, name):
    m_per, ncol = block.shape

    def body(x_ref, out_ref, send_sems, recv_sems, local_sem):
        x, y, c = lax.axis_index("x"), lax.axis_index("y"), lax.axis_index("c")
        me, sibling = (x, y, c), (x, y, 1 - c)
        chips = [(1 - x, y), (x, 1 - y), (1 - x, 1 - y)]

        def rows(px, py, pc):
            return out_ref.at[pl.ds((4 * px + 2 * py + pc) * m_per, m_per), :]

        def copy(k, blk, to, src=None):
            return pltpu.make_async_remote_copy(
                src_ref=rows(*blk) if src is None else src, dst_ref=rows(*blk),
                send_sem=send_sems.at[k], recv_sem=recv_sems.at[k], device_id=to, device_id_type=MESH)

        mine = pltpu.make_async_copy(x_ref, rows(*me), local_sem)
        mine.start()
        first = [copy(0, me, sibling, src=x_ref)]
        first += [copy(1 + j, me, (*chip, c), src=x_ref) for j, chip in enumerate(chips)]
        for cp in first:
            cp.start()
        passed = [copy(4 + j, (*chip, c), sibling) for j, chip in enumerate(chips)]
        for j, chip in enumerate(chips):
            copy(1 + j, (*chip, c), me).wait_recv()
            passed[j].start()
        copy(0, sibling, me).wait_recv()
        for j, chip in enumerate(chips):
            copy(4 + j, (*chip, 1 - c), me).wait_recv()
        for cp in first + passed:
            cp.wait_send()
        mine.wait()

    return pl.pallas_call(
        body, name=name, out_shape=jax.ShapeDtypeStruct((N_DEV * m_per, ncol), block.dtype),
        in_specs=[pl.BlockSpec(memory_space=pltpu.VMEM)], out_specs=pl.BlockSpec(memory_space=pltpu.VMEM),
        scratch_shapes=[pltpu.SemaphoreType.DMA((7,)), pltpu.SemaphoreType.DMA((7,)), pltpu.SemaphoreType.DMA],
    )(block)


def _rope_angles(pos, dim):
    inv = ROPE_THETA ** (-jnp.arange(0, dim, 2, dtype=F32) / dim)
    return pos.astype(F32)[:, None] * inv[None, :]


def _rope_tables(s):
    pos = jnp.arange(s, dtype=jnp.int32)
    rows = s // GRID_W
    row = jnp.repeat(jnp.arange(rows, dtype=jnp.int32), GRID_W)
    col = jnp.tile(jnp.arange(GRID_W, dtype=jnp.int32), rows)
    a1 = _rope_angles(pos, HEAD_DIM)
    aa = _rope_angles(pos, A_ROPE)
    ar = _rope_angles(row, HEAD_DIM // 2)
    ac = _rope_angles(col, HEAD_DIM // 2)
    one = jnp.ones((s, LANE), F32)
    zero = jnp.zeros((s, LANE), F32)
    pad = LANE - A_ROPE
    cos_a = jnp.concatenate([one, jnp.cos(aa), jnp.cos(aa), jnp.ones((s, pad), F32)], axis=1)
    sin_a = jnp.concatenate([zero, -jnp.sin(aa), jnp.sin(aa), jnp.zeros((s, pad), F32)], axis=1)
    cos_b = jnp.concatenate([jnp.cos(a1), jnp.cos(a1)], axis=1)
    sin_b = jnp.concatenate([-jnp.sin(a1), jnp.sin(a1)], axis=1)
    cos_c = jnp.concatenate([jnp.cos(ar), jnp.cos(ar), jnp.cos(ac), jnp.cos(ac)], axis=1)
    sin_c = jnp.concatenate([-jnp.sin(ar), jnp.sin(ar), -jnp.sin(ac), jnp.sin(ac)], axis=1)
    return (cos_a, sin_a), (cos_b, sin_b), (cos_c, sin_c)


def _band_table(rc, s):
    reach = max((win // (2 * d)) * d for win, d in B_PATTERNS)
    r = -(-reach // rc) * rc
    w = min(s, rc + 2 * r)
    j = jnp.arange(rc, dtype=jnp.int32)[:, None]
    x = jnp.arange(2 * w - rc, dtype=jnp.int32)[None, :]
    rel = x - (w - rc) - j
    mult = jnp.zeros(rel.shape, F32)
    for win, d in B_PATTERNS:
        mult = mult + jnp.logical_and(rel % d == 0, jnp.abs(rel) <= (win // (2 * d)) * d).astype(F32)
    return jnp.where(mult > 0, jnp.log2(jnp.maximum(mult, 1.0)), NEG), (w, r)


_BIG = ("w_in", "a_w_uq", "a_w_ukv", "w_out", "w_gate", "w_up", "w_down")
_SMALL = ("attn_norm", "a_q_norm", "a_kv_norm", "c_q_norm", "c_k_norm", "out_norm", "ffn_norm", "final_norm")
_WEIGHTS = ("attn_norm", "w_in", "a_q_norm", "a_w_uq", "a_kv_norm", "a_w_ukv", "c_q_norm", "c_k_norm", "out_norm",
            "w_out", "ffn_norm", "w_gate", "w_up", "w_down", "final_norm")


_ATTN = ("w_in", "a_w_uq", "a_w_ukv")
_FFN = ("w_out", "w_gate", "w_up", "w_down")


def _from_cols(a):
    return jnp.transpose(a, (1, 0, 2)).reshape(a.shape[1], N_CHIPS * a.shape[2])


def _from_rows(a):
    return a.reshape(N_CHIPS * a.shape[1], a.shape[2])


def _to_cols(a):
    return jnp.transpose(a.reshape(a.shape[0], N_CHIPS, a.shape[1] // N_CHIPS), (1, 0, 2))


def _to_rows(a):
    return a.reshape(N_CHIPS, a.shape[0] // N_CHIPS, a.shape[1])


def _assemble_attn(gw):
    w_in_t, uq, ukv = _from_rows(gw[0]), _from_cols(gw[1]), _from_cols(gw[2])
    d = w_in_t.shape[1]
    w_all = jnp.concatenate([w_in_t[:IN_A], jnp.zeros((A_PAD - IN_A, d), BF16), w_in_t[IN_A:]], axis=0)
    uq = uq.reshape(A_Q_RANK, A_HEADS, A_NOPE + A_ROPE)
    uq = jnp.pad(uq, ((0, 0), (0, 0), (0, A_QK - A_NOPE - A_ROPE))).reshape(A_Q_RANK, A_HEADS * A_QK)
    return dict(w_all=w_all, uq=uq, ukv=ukv)


def _assemble_ffn(gw):
    return dict(w_out=_from_rows(gw[0]), w_gate=gw[1], w_up=gw[2], w_down=_from_rows(gw[3]))


def _split_attn_grads(gl):
    w_all = gl["w_all"]
    w_in_t = jnp.concatenate([w_all[:IN_A], w_all[A_PAD:]], axis=0)
    uq = gl["uq"].reshape(A_Q_RANK, A_HEADS, A_QK)[:, :, :A_NOPE + A_ROPE].reshape(A_Q_RANK, A_HEADS * (A_NOPE + A_ROPE))
    return [_to_rows(w_in_t), _to_cols(uq), _to_cols(gl["ukv"])]


def _split_ffn_grads(gl):
    return [_to_rows(gl["w_out"]), gl["w_gate"], gl["w_up"], _to_rows(gl["w_down"])]


def _tie(a, token):
    return a + token[0:1, 0:1]


def _layer_fwd(x, wl, ffn_weights, sm, tabs, bias, t):
    s = x.shape[0]
    (cos_a, sin_a), (cos_b, sin_b), (cos_c, sin_c) = tabs
    h = _norm_fwd(x, sm["attn_norm"], wb=x.shape[1], cb=0, nb=1, shared_gain=True, out_dtype=BF16, name="attn_norm_fwd")
    p = _matmul(h, wl["w_all"], mode="nt", out_dtype=F32, name="in_proj", tm=1024, tn=1280)
    cq_n = _norm_fwd(p, sm["a_q_norm"], wb=A_Q_RANK, cb=0, nb=1, shared_gain=True, out_dtype=BF16, name="a_q_norm_fwd")
    ckv_n = _norm_fwd(p, sm["a_kv_norm"], wb=A_KV_RANK, cb=1, nb=1, shared_gain=True, out_dtype=BF16, name="a_kv_norm_fwd")
    qa_raw = _matmul(cq_n, wl["uq"], mode="nn", out_dtype=F32, name="a_uq", tm=1024, tn=1024)
    kv = _matmul(ckv_n, wl["ukv"], mode="nn", out_dtype=BF16, name="a_ukv", tm=1024, tn=1024)
    qa =_rope(qa_raw, cos_a, sin_a, tw=A_QK, cb=0, nb=A_HEADS, half=A_ROPE // 2, sign=1, out_dtype=BF16, name="a_rope_q")
    ka = _latent_keys(kv, p, cos_a, sin_a, kr_cb=PB_KR, name="a_keys")
    oa, lse_a = _flash_fwd(qa, ka, kv, None, hkv=A_HEADS, g=1, dqk=A_QK, q_cb=0, k_cb=0, v_cb=1, v_step=2,
                           scale=(A_NOPE + A_ROPE) ** -0.5, tq=t, band=None, name="a_flash_fwd")
    table, band = bias
    qb = _rope(p, cos_b, sin_b, tw=LANE, cb=PB_BQ, nb=B_HEADS, half=HEAD_DIM // 2, sign=1, out_dtype=BF16, name="b_rope_q")
    kb = _rope(p, cos_b, sin_b, tw=LANE, cb=PB_BK, nb=B_HEADS, half=HEAD_DIM // 2, sign=1, out_dtype=BF16, name="b_rope_k")
    vb = _cast_cols(p, cb=PB_BV, nb=B_HEADS, name="b_cast_v")
    ob, lse_b = _flash_fwd(qb, kb, vb, table, hkv=B_HEADS, g=1, dqk=LANE, q_cb=0, k_cb=0, v_cb=0, v_step=1,
                           scale=HEAD_DIM ** -0.5, tq=t, band=band, name="b_flash_fwd")
    qn = _norm_fwd(p, sm["c_q_norm"], wb=LANE, cb=PB_CQH, nb=C_HEADS, shared_gain=True, out_dtype=F32, name="c_q_norm_fwd")
    kn = _norm_fwd(p, sm["c_k_norm"], wb=LANE, cb=PB_CKH, nb=C_KV_HEADS, shared_gain=True, out_dtype=F32, name="c_k_norm_fwd")
    qc = _rope(qn, cos_c, sin_c, tw=LANE, cb=0, nb=C_HEADS, half=HEAD_DIM // 4, sign=1, out_dtype=BF16, name="c_rope_q")
    kc = _rope(kn, cos_c, sin_c, tw=LANE, cb=0, nb=C_KV_HEADS, half=HEAD_DIM // 4, sign=1, out_dtype=BF16, name="c_rope_k")
    vc = _cast_cols(p, cb=PB_CVH, nb=C_KV_HEADS, name="c_cast_v")
    oc, lse_c = _flash_fwd(qc, kc, vc, None, hkv=C_KV_HEADS, g=C_GROUP, dqk=LANE, q_cb=0, k_cb=0, v_cb=0, v_step=1,
                           scale=HEAD_DIM ** -0.5, tq=t, band=None, name="c_flash_fwd")
    g_out = sm["out_norm"]
    ga, gb, gc = g_out[:, :A_WIDTH], g_out[:, A_WIDTH:A_WIDTH + B_WIDTH], g_out[:, A_WIDTH + B_WIDTH:]
    ya = _norm_fwd(oa, ga, wb=A_WIDTH, cb=0, nb=1, shared_gain=True, out_dtype=BF16, name="out_norm_a_fwd")
    yb = _norm_fwd(ob, gb, wb=B_WIDTH, cb=0, nb=1, shared_gain=True, out_dtype=BF16, name="out_norm_b_fwd")
    yc = _norm_fwd(oc, gc, wb=C_WIDTH, cb=0, nb=1, shared_gain=True, out_dtype=BF16, name="out_norm_c_fwd")
    y = jnp.concatenate([ya, yb, yc], axis=1)
    wl = {**wl, **ffn_weights(y)}
    x1 = _matmul(y, wl["w_out"], mode="nn", out_dtype=F32, name="out_proj", add=x, tm=1024, tn=1024)
    h2 = _norm_fwd(x1, sm["ffn_norm"], wb=x.shape[1], cb=0, nb=1, shared_gain=True, out_dtype=BF16, name="ffn_norm_fwd")
    gate, up, act = _ffn_up(h2, wl["w_gate"], wl["w_up"], name="ffn_up")
    x2 = _matmul(act, wl["w_down"], mode="nn", out_dtype=F32, name="ffn_down", add=x1, tm=1024, tn=512)
    saved = dict(x=x, h=h, p=p, cq_n=cq_n, ckv_n=ckv_n, kv=kv, qa=qa, ka=ka, oa=oa, lse_a=lse_a, qb=qb, kb=kb, vb=vb, ob=ob,
                 lse_b=lse_b, qc=qc, kc=kc, vc=vc, oc=oc, lse_c=lse_c, y=y, x1=x1, h2=h2, gate=gate, up=up, act=act)
    return x2, saved, wl


def _layer_bwd(dx2, dx2b, sv, wl, sm, tabs, bias, t, send_ffn, send_attn):
    s, d = dx2.shape
    (cos_a, sin_a), (cos_b, sin_b), (cos_c, sin_c) = tabs
    gw, gs = {}, {}
    dgate, dup = _ffn_down_dx(dx2b, wl["w_down"], sv["gate"], sv["up"], name="ffn_down_dx")
    gw["w_down"] = _matmul(sv["act"], dx2b, mode="tn", out_dtype=BF16, name="ffn_down_dw", tm=512, tn=2048)
    dh2 = _ffn_up_dx(dgate, dup, wl["w_gate"], wl["w_up"], name="ffn_up_dx")
    gw["w_gate"] = _matmul(sv["h2"], dgate, mode="tn", out_dtype=BF16, name="ffn_gate_dw", tm=1024, col_shards=True)
    gw["w_up"] = _matmul(sv["h2"], dup, mode="tn", out_dtype=BF16, name="ffn_up_dw", tm=1024, col_shards=True)
    dx1, gs["ffn_norm"], dx1b = _norm_bwd(sv["x1"], sm["ffn_norm"], dh2, wb=d, cb=0, nb=1, shared_gain=True,
                                          out_dtype=F32, name="ffn_norm_bwd", add=dx2, bf16_copy=True)
    dy = _matmul(dx1b, wl["w_out"], mode="nt", out_dtype=F32, name="out_proj_dx", tm=1024, tn=1024)
    gw["w_out"] = _matmul(sv["y"], dx1b, mode="tn", out_dtype=BF16, name="out_proj_dw", tm=512, tn=2048)
    token = send_ffn(gw)
    g_out = _tie(sm["out_norm"], token)
    ga, gb, gc = g_out[:, :A_WIDTH], g_out[:, A_WIDTH:A_WIDTH + B_WIDTH], g_out[:, A_WIDTH + B_WIDTH:]
    dya, dyb, dyc = dy[:, :A_WIDTH], dy[:, A_WIDTH:A_WIDTH + B_WIDTH], dy[:, A_WIDTH + B_WIDTH:]
    doa, dga = _norm_bwd(sv["oa"], ga, dya, wb=A_WIDTH, cb=0, nb=1, shared_gain=True, out_dtype=F32, name="out_norm_a_bwd")
    dob, dgb = _norm_bwd(sv["ob"], gb, dyb, wb=B_WIDTH, cb=0, nb=1, shared_gain=True, out_dtype=F32, name="out_norm_b_bwd")
    doc, dgc = _norm_bwd(sv["oc"], gc, dyc, wb=C_WIDTH, cb=0, nb=1, shared_gain=True, out_dtype=F32, name="out_norm_c_bwd")
    gs["out_norm"] = jnp.concatenate([dga, dgb, dgc], axis=1)
    p = sv["p"]
    dqc, dkc, dvc = _flash_bwd(sv["qc"], sv["kc"], sv["vc"], sv["oc"], doc, sv["lse_c"], None, hkv=C_KV_HEADS,
                               g=C_GROUP, dqk=LANE, q_cb=0, k_cb=0, v_cb=0, v_step=1, scale=HEAD_DIM ** -0.5,
                               tq=t, band=None, name="c_flash_bwd")
    dqn = _rope(dqc, cos_c, sin_c, tw=LANE, cb=0, nb=C_HEADS, half=HEAD_DIM // 4, sign=-1, out_dtype=F32, name="c_rope_q_bwd")
    dkn = _rope(dkc, cos_c, sin_c, tw=LANE, cb=0, nb=C_KV_HEADS, half=HEAD_DIM // 4, sign=-1, out_dtype=F32, name="c_rope_k_bwd")
    dpcq, gs["c_q_norm"] = _norm_bwd(p, sm["c_q_norm"], dqn, wb=LANE, cb=PB_CQH, nb=C_HEADS, shared_gain=True,
                                     out_dtype=BF16, name="c_q_norm_bwd")
    dpck, gs["c_k_norm"] = _norm_bwd(p, sm["c_k_norm"], dkn, wb=LANE, cb=PB_CKH, nb=C_KV_HEADS, shared_gain=True,
                                     out_dtype=BF16, name="c_k_norm_bwd")
    table, band = bias
    dqb, dkb, dvb = _flash_bwd(sv["qb"], sv["kb"], sv["vb"], sv["ob"], dob, sv["lse_b"], table, hkv=B_HEADS, g=1,
                               dqk=LANE, q_cb=0, k_cb=0, v_cb=0, v_step=1, scale=HEAD_DIM ** -0.5, tq=t, band=band,
                               name="b_flash_bwd")
    dpbq = _rope(dqb, cos_b, sin_b, tw=LANE, cb=0, nb=B_HEADS, half=HEAD_DIM // 2, sign=-1, out_dtype=BF16, name="b_rope_q_bwd")
    dpbk = _rope(dkb, cos_b, sin_b, tw=LANE, cb=0, nb=B_HEADS, half=HEAD_DIM // 2, sign=-1, out_dtype=BF16, name="b_rope_k_bwd")
    dqa, dka, dva = _flash_bwd(sv["qa"], sv["ka"], sv["kv"], sv["oa"], doa, sv["lse_a"], None, hkv=A_HEADS, g=1,
                               dqk=A_QK, q_cb=0, k_cb=0, v_cb=1, v_step=2, scale=(A_NOPE + A_ROPE) ** -0.5,
                               tq=t, band=None, name="a_flash_bwd")
    dqa_raw = _rope(dqa, cos_a, sin_a, tw=A_QK, cb=0, nb=A_HEADS, half=A_ROPE // 2, sign=-1, out_dtype=BF16, name="a_rope_q_bwd")
    dkv, dkr = _latent_keys_bwd(dka, dva, cos_a, sin_a, name="a_keys_bwd")
    dckv_n = _matmul(dkv, wl["ukv"], mode="nt", out_dtype=F32, name="a_ukv_dx", tm=1024, tn=512)
    gw["ukv"] = _matmul(sv["ckv_n"], dkv, mode="tn", out_dtype=BF16, name="a_ukv_dw", tm=512, tn=1024)
    dcq_n = _matmul(dqa_raw, wl["uq"], mode="nt", out_dtype=F32, name="a_uq_dx", tm=1024, tn=512)
    gw["uq"] = _matmul(sv["cq_n"], dqa_raw, mode="tn", out_dtype=BF16, name="a_uq_dw", tm=512, tn=1024)
    dcq, gs["a_q_norm"] = _norm_bwd(p, sm["a_q_norm"], dcq_n, wb=A_Q_RANK, cb=0, nb=1, shared_gain=True, out_dtype=BF16,
                                    name="a_q_norm_bwd")
    dckv, gs["a_kv_norm"] = _norm_bwd(p, sm["a_kv_norm"], dckv_n, wb=A_KV_RANK, cb=1, nb=1, shared_gain=True,
                                      out_dtype=BF16, name="a_kv_norm_bwd")
    dp = jnp.concatenate([dcq, dckv, dkr, jnp.zeros((s, A_PAD - (PB_KR + 1) * LANE), BF16), dpbq, dpbk,
                          dvb.astype(BF16), dpcq, dpck, dvc.astype(BF16)], axis=1)
    gw["w_all"] = _matmul(dp, sv["h"], mode="tn", out_dtype=BF16, name="in_proj_dw", tm=640, tn=2048)
    token = send_attn(gw)
    dh = _matmul(dp, wl["w_all"], mode="nn", out_dtype=F32, name="in_proj_dx", tm=1024, tn=512, after=token)
    dx, gs["attn_norm"], dxb = _norm_bwd(sv["x"], sm["attn_norm"], dh, wb=d, cb=0, nb=1, shared_gain=True,
                                         out_dtype=F32, name="attn_norm_bwd", add=dx1, bf16_copy=True)
    return dx, dxb, gs, token


def _pack_small(vals):
    flat = jnp.concatenate([vals[n].reshape(-1).astype(F32) for n in _SMALL])
    tile = SUBLANE * LANE
    padded = -(-flat.shape[0] // tile) * tile
    return jnp.pad(flat, (0, padded - flat.shape[0])).reshape(padded // LANE, LANE)


def _unpack_small(packed, like):
    flat = packed.reshape(-1)
    out, off = {}, 0
    for n in _SMALL:
        size = math.prod(like[n].shape)
        out[n] = flat[off:off + size].reshape(like[n].shape)
        off += size
    return out


def kernel(x, attn_norm, w_in, a_q_norm, a_w_uq, a_kv_norm, a_w_ukv, c_q_norm, c_k_norm, out_norm, w_out, ffn_norm, w_gate, w_up, w_down, final_norm, loss_target, m_attn_norm, m_w_in, m_a_q_norm, m_a_w_uq, m_a_kv_norm, m_a_w_ukv, m_c_q_norm, m_c_k_norm, m_out_norm, m_w_out, m_ffn_norm, m_w_gate, m_w_up, m_w_down, m_final_norm, v_attn_norm, v_w_in, v_a_q_norm, v_a_w_uq, v_a_kv_norm, v_a_w_ukv, v_c_q_norm, v_c_k_norm, v_out_norm, v_w_out, v_ffn_norm, v_w_gate, v_w_up, v_w_down, v_final_norm):
    w = dict(attn_norm=attn_norm, w_in=w_in, a_q_norm=a_q_norm, a_w_uq=a_w_uq, a_kv_norm=a_kv_norm, a_w_ukv=a_w_ukv,
             c_q_norm=c_q_norm, c_k_norm=c_k_norm, out_norm=out_norm, w_out=w_out, ffn_norm=ffn_norm, w_gate=w_gate,
             w_up=w_up, w_down=w_down, final_norm=final_norm)
    m = dict(attn_norm=m_attn_norm, w_in=m_w_in, a_q_norm=m_a_q_norm, a_w_uq=m_a_w_uq, a_kv_norm=m_a_kv_norm,
             a_w_ukv=m_a_w_ukv, c_q_norm=m_c_q_norm, c_k_norm=m_c_k_norm, out_norm=m_out_norm, w_out=m_w_out,
             ffn_norm=m_ffn_norm, w_gate=m_w_gate, w_up=m_w_up, w_down=m_w_down, final_norm=m_final_norm)
    v = dict(attn_norm=v_attn_norm, w_in=v_w_in, a_q_norm=v_a_q_norm, a_w_uq=v_a_w_uq, a_kv_norm=v_a_kv_norm,
             a_w_ukv=v_a_w_ukv, c_q_norm=v_c_q_norm, c_k_norm=v_c_k_norm, out_norm=v_out_norm, w_out=v_w_out,
             ffn_norm=v_ffn_norm, w_gate=v_w_gate, w_up=v_w_up, w_down=v_w_down, final_norm=v_final_norm)
    _, s, d = x.shape
    depth = attn_norm.shape[0]

    def as_stored(a, n):
        return jnp.swapaxes(a, 1, 2) if n == "w_in" else a
    t = _pick(s, 1024)

    me = (2 * lax.axis_index("x") + lax.axis_index("y")).astype(jnp.int32).reshape(1)

    gathers, after = {}, me
    for l in range(depth):
        for group, names in (("attn", _ATTN), ("ffn", _FFN)):
            bufs = [_cast_to_slot(as_stored(w[n], n), me, layer=l, name=f"cast_{n}")
                    for n in names]
            send_sems, recv_sems, bufs, _, after = _exchange_start(bufs, None, after, kind="gather",
                                                                   name=f"gather_start_{group}{l}")
            gathers[group, l] = (send_sems, recv_sems, bufs)
    all_started = after

    def gathered(group, l, after):
        send_sems, recv_sems, bufs = gathers[group, l]
        return _exchange_wait(send_sems, recv_sems, bufs, None, after, kind="gather", name=f"gather_wait_{group}{l}")

    tabs = _rope_tables(s)
    bias = _band_table(min(t, ATTN_ROW_CHUNK), s)

    xs = x.reshape(s, d)
    saved, wls, sms = [], [], []
    for l in range(depth):
        wl = _assemble_attn(gathered("attn", l, all_started if l == 0 else xs))
        sm = {n: w[n][l][None, :] for n in _SMALL if n != "final_norm"}
        xs, sv, wl = _layer_fwd(xs, wl, lambda after, l=l: _assemble_ffn(gathered("ffn", l, after)), sm, tabs, bias, t)
        saved.append(sv)
        wls.append(wl)
        sms.append(sm)
    dx, g_final, loss_row, dxb = _final_loss(xs, final_norm[None, :], loss_target.reshape(s, d), name="final_loss")
    loss = lax.psum(loss_row[0, 0], ("x", "y", "c"))

    sends = {}

    def send(group, l, srcs, after):
        lands = [lax.empty((3,) + a.shape[1:], BF16) for a in srcs]
        send_sems, recv_sems, srcs, lands, token = _exchange_start(srcs, lands, after, kind="scatter",
                                                                   name=f"scatter_start_{group}{l}")
        sends[group, l] = (send_sems, recv_sems, srcs, lands)
        return token

    gs_layers, token = [None] * depth, all_started
    for l in reversed(range(depth)):
        dx, dxb, gs_layers[l], token = _layer_bwd(
            dx, dxb, saved[l], wls[l], sms[l], tabs, bias, t,
            lambda gw, l=l, tk=token: send("ffn", l, _split_ffn_grads(gw), tk),
            lambda gw, l=l: send("attn", l, _split_attn_grads(gw), dx))
    grad_x = dx.reshape(x.shape)

    srcs, lands = {}, {}

    def arrive(key, after):
        send_sems, recv_sems, s_bufs, l_bufs = sends[key]
        got = _exchange_wait(send_sems, recv_sems, s_bufs, l_bufs, after, kind="scatter",
                             name=f"scatter_wait_{key[0]}{key[1]}")
        for k, n in enumerate(_ATTN if key[0] == "attn" else _FFN):
            srcs[n, key[1]], lands[n, key[1]] = got[k], got[len(s_bufs) + k]

    def summed(names):
        return [_sum_parts([srcs[n, l] for l in range(depth)], [lands[n, l] for l in range(depth)], me, name="sum_" + n)
                for n in names]

    last = ("attn", 0)
    for key in sends:
        if key != last:
            arrive(key, token)
    sums_ffn = summed(_FFN)
    swap = _exchange_start(sums_ffn, [lax.empty(a.shape, F32) for a in sums_ffn], token, kind="swap",
                           name="swap_start_ffn")
    arrive(last, swap[4])
    sums_attn = summed(_ATTN)
    sib_attn = list(_sibling_exchange(sums_attn, name="swap_core_sums_attn"))
    swapped = _exchange_wait(swap[0], swap[1], swap[2], swap[3], sib_attn[0], kind="swap", name="swap_wait_ffn")
    mine_of = dict(zip(_FFN + _ATTN, swapped[:len(_FFN)] + sums_attn))
    other_of = dict(zip(_FFN + _ATTN, swapped[len(_FFN):] + sib_attn))
    grads, deltas, new_m, new_v = {}, {}, {}, {}
    for n in _BIG:
        res = _adamw(mine_of[n], other_of[n], as_stored(w[n], n), as_stored(m[n], n), as_stored(v[n], n), name="adamw_" + n)
        grads[n], deltas[n], new_m[n], new_v[n] = [as_stored(r, n) for r in res]

    gsm = {n: jnp.stack([gs_layers[l][n][0] for l in range(depth)]) for n in _SMALL if n != "final_norm"}
    gsm["final_norm"] = g_final[0]
    packed = _pack_small(gsm)
    everyone = _all_gather_small(packed, name="gather_gain_grads").reshape(N_DEV, packed.shape[0], LANE)
    res = _small_adamw(everyone, _pack_small(w), _pack_small(m), _pack_small(v), name="adamw_gains")
    for dst, r in zip((grads, deltas, new_m, new_v), res):
        dst.update(_unpack_small(r, w))

    return (loss, grad_x, *[grads[n] for n in _WEIGHTS], *[deltas[n] for n in _WEIGHTS],
            *[new_m[n] for n in _WEIGHTS], *[new_v[n] for n in _WEIGHTS])
```

```python
import functools
import math

import jax
import jax.numpy as jnp
import numpy as np
from jax import lax
from jax.experimental import pallas as pl
from jax.experimental.pallas import tpu as pltpu

F32 = jnp.float32
BF16 = jnp.bfloat16
MESH = pl.DeviceIdType.MESH

HEAD_DIM = 128
ROPE_THETA = 10000.0
GRID_W = 64
EPS = 1e-6
NEG = -1e30
A_HEADS, A_Q_RANK, A_KV_RANK, A_NOPE, A_ROPE, A_V = 4, 512, 512, 128, 64, 128
B_HEADS = 6
B_PATTERNS = ((128, 1), (512, 4), (2048, 16))
C_HEADS, C_KV_HEADS = 6, 2
C_GROUP = C_HEADS // C_KV_HEADS
A_WIDTH, B_WIDTH, C_WIDTH = A_HEADS * A_V, B_HEADS * HEAD_DIM, C_HEADS * HEAD_DIM
IN_A = A_Q_RANK + A_KV_RANK + A_ROPE
IN_B = 3 * B_WIDTH
IN_C = C_WIDTH + 2 * C_KV_HEADS * HEAD_DIM
ADAM_LR, ADAM_B1, ADAM_B2, ADAM_EPS, ADAM_WD, ADAM_STEP = 0.001, 0.9, 0.999, 1e-08, 0.01, 10

LANE = 128
SUBLANE = 8
VMEM_BYTES_V7X = 64 * 1024 * 1024
VMEM_LIMIT_CAP = VMEM_BYTES_V7X - 8 * 1024 * 1024
N_CHIPS = 4
N_DEV = 8

A_PAD = 12 * LANE
PB_CQ, PB_CKV, PB_KR = 0, 4, 8
PB_BQ, PB_BK, PB_BV = 12, 18, 24
PB_CQH, PB_CKH, PB_CVH = 30, 36, 38
NP = 40 * LANE
A_QK = 2 * LANE


def _pick(n, cap, mult=LANE):
    if n <= cap:
        return n
    t = cap - cap % mult
    while t >= mult:
        if n % t == 0:
            return t
        t -= mult
    return n


def _rows_for(width_bytes, n_rows, target=2 * 1024 * 1024):
    return _pick(n_rows, max(SUBLANE, target // max(width_bytes, 1)), SUBLANE)


def _tile2(rows, cols, target):
    tc = _pick(cols, 4 * LANE)
    if tc < 4 * LANE:
        tc = cols
    fits = [t for t in range(SUBLANE, rows + 1, SUBLANE) if rows % t == 0] or [rows]
    return min(fits, key=lambda t: abs(math.log(t * tc * 4 / target))), tc


def _params(est_bytes):
    limit = int(min(max(est_bytes + (4 << 20), 32 << 20), VMEM_LIMIT_CAP))
    return pltpu.CompilerParams(vmem_limit_bytes=limit)


def _isz(x):
    return jnp.dtype(x.dtype).itemsize


def _hbm(shape, dtype):
    return pltpu.HBM(shape, dtype)


def _pin(*arrays):
    return [pltpu.with_memory_space_constraint(a, pltpu.HBM) for a in arrays]


_DIMS = {"nn": (((1,), (0,)), ((), ())), "nt": (((1,), (1,)), ((), ())), "tn": (((0,), (0,)), ((), ()))}


def _matmul(a, b, *, mode, out_dtype, name, add=None, tm=512, tn=512, col_shards=False, after=None):
    if mode == "tn":
        (k, m), (k2, n) = a.shape, b.shape
    elif mode == "nt":
        (m, k), (n, k2) = a.shape, b.shape
    else:
        (m, k), (k2, n) = a.shape, b.shape
    assert k == k2, (a.shape, b.shape, mode)
    tm, tn = _pick(m, tm), (n // N_CHIPS if col_shards else _pick(n, tn))
    a_spec = pl.BlockSpec((k, tm), lambda i, j: (0, i)) if mode == "tn" else pl.BlockSpec((tm, k), lambda i, j: (i, 0))
    b_spec = pl.BlockSpec((tn, k), lambda i, j: (j, 0)) if mode == "nt" else pl.BlockSpec((k, tn), lambda i, j: (0, j))
    o_spec = pl.BlockSpec((None, tm, tn), lambda i, j: (j, i, 0)) if col_shards else pl.BlockSpec((tm, tn), lambda i, j: (i, j))
    dims = _DIMS[mode]

    def body(*refs):
        a_ref, b_ref, o_ref = refs[0], refs[1], refs[-1]
        acc = lax.dot_general(a_ref[...].astype(BF16), b_ref[...].astype(BF16), dims, preferred_element_type=F32)
        if add is not None:
            acc = acc + refs[2][...].astype(F32)
        o_ref[...] = acc.astype(out_dtype)

    ins, specs = [a, b], [a_spec, b_spec]
    if add is not None:
        ins.append(add)
        specs.append(o_spec)
    if after is not None:
        ins.append(after)
        specs.append(pl.BlockSpec(memory_space=pl.ANY))
    est = 2 * (tm * k * _isz(a) + tn * k * _isz(b) + tm * tn * (jnp.dtype(out_dtype).itemsize + (4 if add is not None else 0)))
    est += (tm + tn) * k * 2 + 2 * tm * tn * 4
    return pl.pallas_call(
        body, name=name, grid=(m // tm, n // tn), in_specs=specs, out_specs=o_spec,
        out_shape=_hbm((N_CHIPS, m, tn) if col_shards else (m, n), out_dtype),
        compiler_params=_params(est),
    )(*_pin(*ins))


def _ffn_up(h, wg, wu, *, name):
    s, d = h.shape
    _, _, c = wg.shape
    tm = _pick(s, 512, SUBLANE)

    def body(h_ref, wg_ref, wu_ref, g_ref, u_ref, a_ref):
        hv = h_ref[...]
        gv = jnp.dot(hv, wg_ref[...], preferred_element_type=F32)
        uv = jnp.dot(hv, wu_ref[...], preferred_element_type=F32)
        g_ref[...] = gv.astype(BF16)
        u_ref[...] = uv.astype(BF16)
        a_ref[...] = (gv / (1.0 + jnp.exp(-gv)) * uv).astype(BF16)

    w_spec = pl.BlockSpec((None, d, c), lambda j, i: (j, 0, 0))
    o_spec = pl.BlockSpec((tm, c), lambda j, i: (i, j))
    est = 2 * (tm * d * 2 + 2 * d * c * 2 + tm * c * 10) + 4 * tm * c * 4
    return pl.pallas_call(
        body, name=name, grid=(N_CHIPS, s // tm), in_specs=[pl.BlockSpec((tm, d), lambda j, i: (i, 0)), w_spec, w_spec],
        out_specs=[o_spec, o_spec, o_spec],
        out_shape=[_hbm((s, N_CHIPS * c), BF16)] * 3,
        compiler_params=_params(est),
    )(*_pin(h, wg, wu))


def _ffn_down_dx(dx, w_down, gate, up, *, name):
    s, d = dx.shape
    f = w_down.shape[0]
    tm, tn = _pick(s, 1024, SUBLANE), _pick(f, 512)

    def body(dx_ref, w_ref, g_ref, u_ref, dg_ref, du_ref):
        dact = lax.dot_general(dx_ref[...], w_ref[...], _DIMS["nt"], preferred_element_type=F32)
        gv, uv = g_ref[...].astype(F32), u_ref[...].astype(F32)
        sig = 1.0 / (1.0 + jnp.exp(-gv))
        dg_ref[...] = (dact * uv * (sig * (1.0 + gv * (1.0 - sig)))).astype(BF16)
        du_ref[...] = (dact * (gv * sig)).astype(BF16)

    t_spec = pl.BlockSpec((tm, tn), lambda i, j: (i, j))
    est = 2 * (tm * d * 2 + tn * d * 2 + tm * tn * 12) + 6 * tm * tn * 4
    return pl.pallas_call(
        body, name=name, grid=(s // tm, f // tn),
        in_specs=[pl.BlockSpec((tm, d), lambda i, j: (i, 0)), pl.BlockSpec((tn, d), lambda i, j: (j, 0)), t_spec, t_spec],
        out_specs=[t_spec, t_spec], out_shape=[_hbm((s, f), BF16)] * 2, compiler_params=_params(est),
    )(*_pin(dx, w_down, gate, up))


def _ffn_up_dx(dgate, dup, wg, wu, *, name):
    s, f = dgate.shape
    _, d, c = wg.shape
    tm, tn = _pick(s, 1024, SUBLANE), _pick(d, 1024)
    nk = 2 * N_CHIPS

    def body(dg_ref, du_ref, wg_ref, wu_ref, o_ref, acc):
        kk = pl.program_id(2)

        @pl.when(kk == 0)
        def _():
            acc[...] = jnp.zeros_like(acc)

        @pl.when(kk < N_CHIPS)
        def _():
            acc[...] += lax.dot_general(dg_ref[...], wg_ref[...], _DIMS["nt"], preferred_element_type=F32)

        @pl.when(kk >= N_CHIPS)
        def _():
            acc[...] += lax.dot_general(du_ref[...], wu_ref[...], _DIMS["nt"], preferred_element_type=F32)

        @pl.when(kk == nk - 1)
        def _():
            o_ref[...] = acc[...]

    last = N_CHIPS - 1
    est = 2 * (2 * tm * c * 2 + 2 * tn * c * 2 + tm * tn * 4) + 2 * tm * tn * 4
    return pl.pallas_call(
        body, name=name, grid=(s // tm, d // tn, nk),
        in_specs=[pl.BlockSpec((tm, c), lambda i, j, kk: (i, jnp.minimum(kk, last))),
                  pl.BlockSpec((tm, c), lambda i, j, kk: (i, jnp.maximum(kk - N_CHIPS, 0))),
                  pl.BlockSpec((None, tn, c), lambda i, j, kk: (jnp.minimum(kk, last), j, 0)),
                  pl.BlockSpec((None, tn, c), lambda i, j, kk: (jnp.maximum(kk - N_CHIPS, 0), j, 0))],
        out_specs=pl.BlockSpec((tm, tn), lambda i, j, kk: (i, j)),
        out_shape=_hbm((s, d), F32), scratch_shapes=[pltpu.VMEM((tm, tn), F32)],
        compiler_params=_params(est),
    )(*_pin(dgate, dup, wg, wu))


def _norm_fwd(x, gain, *, wb, cb, nb, shared_gain, out_dtype, name):
    s = x.shape[0]
    ts = _rows_for(wb * 4, s)

    def body(x_ref, g_ref, o_ref):
        xv = x_ref[...].astype(F32)
        r = lax.rsqrt(jnp.mean(xv * xv, axis=1, keepdims=True) + EPS)
        o_ref[...] = ((xv * r) * g_ref[...]).astype(out_dtype)

    return pl.pallas_call(
        body, name=name, grid=(nb, s // ts),
        in_specs=[pl.BlockSpec((ts, wb), lambda n, i: (i, cb + n)),
                  pl.BlockSpec((1, wb), (lambda n, i: (0, 0)) if shared_gain else (lambda n, i: (0, n)))],
        out_specs=pl.BlockSpec((ts, wb), lambda n, i: (i, n)),
        out_shape=_hbm((s, nb * wb), out_dtype), compiler_params=_params(6 * ts * wb * 4),
    )(*_pin(x), gain)


def _norm_bwd(x, gain, dy, *, wb, cb, nb, shared_gain, out_dtype, name, dy_cb=0, add=None, bf16_copy=False):
    s = x.shape[0]
    ts = _rows_for(wb * 4, s, target=1024 * 1024)
    gw = wb if shared_gain else nb * wb

    def body(*refs):
        refs = list(refs)
        dxb_ref = refs.pop() if bf16_copy else None
        if add is None:
            x_ref, g_ref, dy_ref, dx_ref, dg_ref = refs
        else:
            x_ref, g_ref, dy_ref, add_ref, dx_ref, dg_ref = refs
        n, i = pl.program_id(0), pl.program_id(1)
        xv = x_ref[...].astype(F32)
        dyv = dy_ref[...].astype(F32)
        r = lax.rsqrt(jnp.mean(xv * xv, axis=1, keepdims=True) + EPS)
        xh = xv * r
        dyg = dyv * g_ref[...]
        dx = r * (dyg - xh * jnp.mean(dyg * xh, axis=1, keepdims=True))
        if add is not None:
            dx = dx + add_ref[...]
        dx_ref[...] = dx.astype(out_dtype)
        if bf16_copy:
            dxb_ref[...] = dx.astype(BF16)
        first = jnp.logical_and(n == 0, i == 0) if shared_gain else (i == 0)

        @pl.when(first)
        def _():
            dg_ref[...] = jnp.zeros_like(dg_ref)

        dg_ref[...] += jnp.sum(dyv * xh, axis=0, keepdims=True)

    ins = [x, gain, dy]
    specs = [pl.BlockSpec((ts, wb), lambda n, i: (i, cb + n)),
             pl.BlockSpec((1, wb), (lambda n, i: (0, 0)) if shared_gain else (lambda n, i: (0, n))),
             pl.BlockSpec((ts, wb), lambda n, i: (i, dy_cb + n))]
    if add is not None:
        ins.append(add)
        specs.append(pl.BlockSpec((ts, wb), lambda n, i: (i, n)))
    out_specs = [pl.BlockSpec((ts, wb), lambda n, i: (i, n)),
                 pl.BlockSpec((1, wb), (lambda n, i: (0, 0)) if shared_gain else (lambda n, i: (0, n)))]
    out_shape = [_hbm((s, nb * wb), out_dtype), jax.ShapeDtypeStruct((1, gw), F32)]
    if bf16_copy:
        out_specs.append(out_specs[0])
        out_shape.append(_hbm((s, nb * wb), BF16))
    return pl.pallas_call(
        body, name=name, grid=(nb, s // ts), in_specs=specs, out_specs=out_specs, out_shape=out_shape,
        compiler_params=_params(14 * ts * wb * 4),
    )(*_pin(*ins))


def _swap_halves(x, half):
    if 2 * half == LANE:
        return pltpu.roll(x, half, axis=1)
    lane = lax.broadcasted_iota(jnp.int32, x.shape, 1)
    first = jnp.bitwise_and(lane, 2 * half - 1) < half
    return jnp.where(first, pltpu.roll(x, LANE - half, axis=1), pltpu.roll(x, half, axis=1))


def _rope(x, cos_t, sin_t, *, tw, cb, nb, half, sign, out_dtype, name):
    s = x.shape[0]
    ts = _rows_for(tw * 4, s)

    def body(x_ref, c_ref, s_ref, o_ref):
        for q in range(tw // LANE):
            sl = slice(q * LANE, (q + 1) * LANE)
            xv = x_ref[:, sl].astype(F32)
            sv = s_ref[:, sl]
            if sign < 0:
                sv = -sv
            o_ref[:, sl] = (xv * c_ref[:, sl] + _swap_halves(xv, half) * sv).astype(out_dtype)

    return pl.pallas_call(
        body, name=name, grid=(nb, s // ts),
        in_specs=[pl.BlockSpec((ts, tw), lambda n, i: (i, cb + n)),
                  pl.BlockSpec((ts, tw), lambda n, i: (i, 0)),
                  pl.BlockSpec((ts, tw), lambda n, i: (i, 0))],
        out_specs=pl.BlockSpec((ts, tw), lambda n, i: (i, n)),
        out_shape=_hbm((s, nb * tw), out_dtype), compiler_params=_params(10 * ts * tw * 4),
    )(*_pin(x), cos_t, sin_t)


def _latent_keys(kv, p, cos_t, sin_t, *, kr_cb, name):
    s = kv.shape[0]
    ts = _rows_for(A_QK * 4, s)

    def body(kv_ref, kr_ref, c_ref, s_ref, o_ref):
        o_ref[:, :LANE] = kv_ref[...].astype(BF16)
        x = kr_ref[...].astype(F32)
        o_ref[:, LANE:] = (x * c_ref[:, LANE:] + _swap_halves(x, A_ROPE // 2) * s_ref[:, LANE:]).astype(BF16)

    tab = pl.BlockSpec((ts, A_QK), lambda n, i: (i, 0))
    return pl.pallas_call(
        body, name=name, grid=(A_HEADS, s // ts),
        in_specs=[pl.BlockSpec((ts, LANE), lambda n, i: (i, 2 * n)), pl.BlockSpec((ts, LANE), lambda n, i: (i, kr_cb)),
                  tab, tab],
        out_specs=pl.BlockSpec((ts, A_QK), lambda n, i: (i, n)),
        out_shape=_hbm((s, A_HEADS * A_QK), BF16), compiler_params=_params(10 * ts * A_QK * 4),
    )(*_pin(kv, p), cos_t, sin_t)


def _latent_keys_bwd(dka, dva, cos_t, sin_t, *, name):
    s = dka.shape[0]
    ts = _rows_for(A_HEADS * A_QK * 4, s)

    def body(dka_ref, dva_ref, c_ref, s_ref, dkv_ref, dkr_ref):
        acc = jnp.zeros((ts, LANE), F32)
        for h in range(A_HEADS):
            dkv_ref[:, h * A_QK:h * A_QK + LANE] = dka_ref[:, h * A_QK:h * A_QK + LANE].astype(BF16)
            dkv_ref[:, h * A_QK + LANE:(h + 1) * A_QK] = dva_ref[:, h * LANE:(h + 1) * LANE].astype(BF16)
            y = dka_ref[:, h * A_QK + LANE:(h + 1) * A_QK]
            acc = acc + (y * c_ref[:, LANE:] - _swap_halves(y, A_ROPE // 2) * s_ref[:, LANE:])
        dkr_ref[...] = acc.astype(BF16)

    def rows(width):
        return pl.BlockSpec((ts, width), lambda i: (i, 0))

    return pl.pallas_call(
        body, name=name, grid=(s // ts,),
        in_specs=[rows(A_HEADS * A_QK), rows(A_HEADS * LANE), rows(A_QK), rows(A_QK)],
        out_specs=[rows(A_HEADS * A_QK), rows(LANE)],
        out_shape=[_hbm((s, A_HEADS * A_QK), BF16), _hbm((s, LANE), BF16)],
        compiler_params=_params(8 * ts * A_HEADS * A_QK * 4),
    )(*_pin(dka, dva), cos_t, sin_t)


def _cast_cols(x, *, cb, nb, name):
    s = x.shape[0]
    ts = _rows_for(LANE * 4, s)

    def body(x_ref, o_ref):
        o_ref[...] = x_ref[...].astype(BF16)

    return pl.pallas_call(
        body, name=name, grid=(nb, s // ts), in_specs=[pl.BlockSpec((ts, LANE), lambda n, i: (i, cb + n))],
        out_specs=pl.BlockSpec((ts, LANE), lambda n, i: (i, n)),
        out_shape=_hbm((s, nb * LANE), BF16), compiler_params=_params(4 * ts * LANE * 4),
    )(*_pin(x))


LOG2E = 1.4426950408889634
ATTN_ROW_CHUNK = 256


def _attn_window(i, rc, s, band):
    w, r = band
    start = jnp.clip(i * rc - r, 0, s - w)
    return pl.multiple_of(start, rc), pl.multiple_of((w - rc) - (i * rc - start), LANE)


def _flash_fwd(q, k, v, table, *, hkv, g, dqk, q_cb, k_cb, v_cb, v_step, scale, tq, band, name):
    s = q.shape[0]
    n = s // tq
    hq = hkv * g
    rc = min(tq, ATTN_ROW_CHUNK)
    w = s if band is None else band[0]

    def body(*refs):
        if band is None:
            q_ref, k_ref, v_ref, o_ref, lse_ref = refs
            kw, vw = k_ref[...], v_ref[...]
        else:
            q_ref, k_ref, v_ref, t_ref, o_ref, lse_ref = refs
        for c in range(tq // rc):
            rows = slice(c * rc, (c + 1) * rc)
            if band is not None:
                start, u = _attn_window(pl.program_id(1) * (tq // rc) + c, rc, s, band)
                kw, vw = k_ref[pl.ds(start, w), :], v_ref[pl.ds(start, w), :]
            sc = lax.dot_general(q_ref[rows, :], kw, _DIMS["nt"], preferred_element_type=F32) * (scale * LOG2E)
            if band is not None:
                sc = sc + t_ref[:, pl.ds(u, w)]
            m = jnp.max(sc, axis=1, keepdims=True)
            p = jnp.exp2(sc - m)
            l = jnp.sum(p, axis=1, keepdims=True)
            o_ref[rows, :] = jnp.dot(p.astype(BF16), vw, preferred_element_type=F32) / l
            lse_ref[0, rows, :] = jnp.broadcast_to(m + jnp.log2(l), (rc, LANE))

    ins = [q, k, v]
    specs = [pl.BlockSpec((tq, dqk), lambda h, i: (i, q_cb + h)),
             pl.BlockSpec((s, dqk), lambda h, i: (0, k_cb + h // g)),
             pl.BlockSpec((s, LANE), lambda h, i: (0, v_cb + v_step * (h // g)))]
    if band is not None:
        ins.append(table)
        specs.append(pl.BlockSpec(table.shape, lambda h, i: (0, 0)))
    est = 4 * s * (dqk + LANE) + 6 * rc * w * 4 + 8 * tq * LANE * 4 + (0 if band is None else 2 * table.size * 4)
    return pl.pallas_call(
        body, name=name, grid=(hq, n), in_specs=specs,
        out_specs=[pl.BlockSpec((tq, LANE), lambda h, i: (i, h)), pl.BlockSpec((1, tq, LANE), lambda h, i: (h, i, 0))],
        out_shape=[_hbm((s, hq * LANE), F32), _hbm((hq, s, LANE), F32)],
        compiler_params=_params(est),
    )(*_pin(*ins))


def _flash_bwd(q, k, v, o, do, lse, table, *, hkv, g, dqk, q_cb, k_cb, v_cb, v_step, scale, tq, band, name):
    s = q.shape[0]
    n = s // tq
    hq = hkv * g
    rc = min(tq, ATTN_ROW_CHUNK)
    w = s if band is None else band[0]

    def body(*refs):
        if band is None:
            q_ref, k_ref, v_ref, o_ref, do_ref, lse_ref, dq_ref, dk_ref, dv_ref = refs
            keys = slice(None)
            kw, vw = k_ref[...], v_ref[...]
        else:
            q_ref, k_ref, v_ref, o_ref, do_ref, lse_ref, t_ref, dq_ref, dk_ref, dv_ref = refs
        h, i = pl.program_id(0), pl.program_id(1)

        @pl.when(jnp.logical_and(h % g == 0, i == 0))
        def _():
            dk_ref[...] = jnp.zeros_like(dk_ref)
            dv_ref[...] = jnp.zeros_like(dv_ref)

        for c in range(tq // rc):
            rows = slice(c * rc, (c + 1) * rc)
            if band is not None:
                start, u = _attn_window(i * (tq // rc) + c, rc, s, band)
                keys = pl.ds(start, w)
                kw, vw = k_ref[keys, :], v_ref[keys, :]
            qv = q_ref[rows, :]
            dof = do_ref[rows, :]
            dov = dof.astype(BF16)
            sc = lax.dot_general(qv, kw, _DIMS["nt"], preferred_element_type=F32) * (scale * LOG2E)
            if band is not None:
                sc = sc + t_ref[:, pl.ds(u, w)]
            p = jnp.exp2(sc - lse_ref[0, rows, 0:1])
            dp = lax.dot_general(dov, vw, _DIMS["nt"], preferred_element_type=F32)
            delta = jnp.sum(dof * o_ref[rows, :], axis=1, keepdims=True)
            ds = (p * (dp - delta) * scale).astype(BF16)
            dv_ref[keys, :] += lax.dot_general(p.astype(BF16), dov, _DIMS["tn"], preferred_element_type=F32)
            dk_ref[keys, :] += lax.dot_general(ds, qv, _DIMS["tn"], preferred_element_type=F32)
            dq_ref[rows, :] = jnp.dot(ds, kw, preferred_element_type=F32)

    ins = [q, k, v, o, do, lse]
    specs = [pl.BlockSpec((tq, dqk), lambda h, i: (i, q_cb + h)),
             pl.BlockSpec((s, dqk), lambda h, i: (0, k_cb + h // g)),
             pl.BlockSpec((s, LANE), lambda h, i: (0, v_cb + v_step * (h // g))),
             pl.BlockSpec((tq, LANE), lambda h, i: (i, h)),
             pl.BlockSpec((tq, LANE), lambda h, i: (i, h)),
             pl.BlockSpec((1, tq, LANE), lambda h, i: (h, i, 0))]
    if band is not None:
        ins.append(table)
        specs.append(pl.BlockSpec(table.shape, lambda h, i: (0, 0)))
    est = (4 + 8) * s * (dqk + LANE) + 10 * rc * w * 4 + 12 * tq * LANE * 4 + (0 if band is None else 2 * table.size * 4)
    return pl.pallas_call(
        body, name=name, grid=(hq, n), in_specs=specs,
        out_specs=[pl.BlockSpec((tq, dqk), lambda h, i: (i, h)),
                   pl.BlockSpec((s, dqk), lambda h, i: (0, h // g)),
                   pl.BlockSpec((s, LANE), lambda h, i: (0, h // g))],
        out_shape=[_hbm((s, hq * dqk), F32), _hbm((s, hkv * dqk), F32),
                   _hbm((s, hkv * LANE), F32)],
        compiler_params=_params(est),
    )(*_pin(*ins))


def _final_loss(x, gain, target, *, name):
    s, d = x.shape
    ts = _rows_for(d * 4, s, target=1024 * 1024)

    def body(x_ref, g_ref, t_ref, dx_ref, dg_ref, loss_ref, dxb_ref):
        i = pl.program_id(0)
        xv = x_ref[...]
        gv = g_ref[...]
        r = lax.rsqrt(jnp.mean(xv * xv, axis=1, keepdims=True) + EPS)
        xh = xv * r
        err = xh * gv - t_ref[...]
        dy = err / d
        dyg = dy * gv
        dx = r * (dyg - xh * jnp.mean(dyg * xh, axis=1, keepdims=True))
        dx_ref[...] = dx
        dxb_ref[...] = dx.astype(BF16)

        @pl.when(i == 0)
        def _():
            dg_ref[...] = jnp.zeros_like(dg_ref)
            loss_ref[...] = jnp.zeros_like(loss_ref)

        dg_ref[...] += jnp.sum(dy * xh, axis=0, keepdims=True)
        part = jnp.sum(jnp.mean(err * err, axis=1, keepdims=True), axis=0, keepdims=True)
        loss_ref[...] += jnp.broadcast_to(0.5 * part, (1, LANE))

    row = pl.BlockSpec((ts, d), lambda i: (i, 0))
    return pl.pallas_call(
        body, name=name, grid=(s // ts,),
        in_specs=[row, pl.BlockSpec((1, d), lambda i: (0, 0)), row],
        out_specs=[row, pl.BlockSpec((1, d), lambda i: (0, 0)), pl.BlockSpec((1, LANE), lambda i: (0, 0)), row],
        out_shape=[_hbm((s, d), F32), jax.ShapeDtypeStruct((1, d), F32),
                   jax.ShapeDtypeStruct((1, LANE), F32), _hbm((s, d), BF16)],
        compiler_params=_params(14 * ts * d * 4),
    )(*_pin(x), gain, *_pin(target))


def _cast_to_slot(x3d, me, *, layer, name):
    _, rows, c = x3d.shape
    tr, tc = _tile2(rows, c, 2 * 1024 * 1024)

    def body(me_ref, x_ref, o_ref):
        o_ref[...] = x_ref[...].astype(BF16)

    return pl.pallas_call(
        body, name=name,
        grid_spec=pltpu.PrefetchScalarGridSpec(
            num_scalar_prefetch=1, grid=(rows // tr, c // tc),
            in_specs=[pl.BlockSpec((None, tr, tc), lambda i, j, me_ref: (layer, i, j))],
            out_specs=pl.BlockSpec((None, tr, tc), lambda i, j, me_ref: (me_ref[0], i, j))),
        out_shape=_hbm((N_CHIPS, rows, c), BF16), compiler_params=_params(6 * tr * tc * 4),
    )(me, *_pin(x3d))


def _sum_parts(srcs, lands, me, *, name):
    depth = len(srcs)
    _, r, c = srcs[0].shape
    tr, tc = _tile2(r, c, 1024 * 1024)
    nt, nc = r // tr, c // tc

    def body(me_ref, *refs):
        o_ref = refs[-1]
        l = pl.program_id(0)
        for k in range(depth):
            @pl.when(l == k)
            def _(k=k):
                acc = refs[k][...].astype(F32)
                for p in range(3):
                    acc = acc + refs[depth + k][p].astype(F32)
                o_ref[...] = acc

    def tile_of(k):
        def f(l, i, j):
            return (jnp.where(l == k, i, jnp.where(l < k, 0, nt - 1)), jnp.where(l == k, j, jnp.where(l < k, 0, nc - 1)))
        return f

    in_specs = [pl.BlockSpec((None, tr, tc), functools.partial(lambda l, i, j, me_ref, f: (me_ref[0], *f(l, i, j)), f=tile_of(k)))
                for k in range(depth)]
    in_specs += [pl.BlockSpec((3, tr, tc), functools.partial(lambda l, i, j, me_ref, f: (0, *f(l, i, j)), f=tile_of(k)))
                 for k in range(depth)]
    return pl.pallas_call(
        body, name=name,
        grid_spec=pltpu.PrefetchScalarGridSpec(
            num_scalar_prefetch=1, grid=(depth, nt, nc), in_specs=in_specs,
            out_specs=pl.BlockSpec((tr, tc), lambda l, i, j, me_ref: (l * nt + i, j))),
        out_shape=_hbm((depth * r, c), F32), compiler_params=_params(depth * 10 * tr * tc * 4),
    )(me, *_pin(*srcs, *lands))


def _adamw_math(w, g, m, v):
    m2 = ADAM_B1 * m + (1.0 - ADAM_B1) * g
    v2 = ADAM_B2 * v + (1.0 - ADAM_B2) * (g * g)
    m_hat = m2 / (1.0 - ADAM_B1 ** ADAM_STEP)
    v_hat = v2 / (1.0 - ADAM_B2 ** ADAM_STEP)
    delta = -ADAM_LR * (m_hat / (jnp.sqrt(v_hat) + ADAM_EPS) + ADAM_WD * w)
    return delta, m2, v2


def _adamw(g_a, g_b, w, m, v, *, name):
    depth, r, c = w.shape
    tr, tc = _tile2(r, c, 512 * 1024)
    nt = r // tr

    def body(a_ref, b_ref, w_ref, m_ref, v_ref, g_out, d_out, m_out, v_out):
        gv = a_ref[...] + b_ref[...]
        delta, m2, v2 = _adamw_math(w_ref[...], gv, m_ref[...], v_ref[...])
        g_out[...] = gv
        d_out[...] = delta
        m_out[...] = m2
        v_out[...] = v2

    flat = pl.BlockSpec((tr, tc), lambda l, i, j: (l * nt + i, j))
    spec = pl.BlockSpec((None, tr, tc), lambda l, i, j: (l, i, j))
    return pl.pallas_call(
        body, name=name, grid=(depth, nt, c // tc), in_specs=[flat, flat, spec, spec, spec], out_specs=[spec] * 4,
        out_shape=[_hbm((depth, r, c), F32)] * 4, compiler_params=_params(22 * tr * tc * 4),
    )(*_pin(g_a, g_b, w, m, v))


def _small_adamw(g_all, w, m, v, *, name):
    r, c = w.shape

    def body(ga_ref, w_ref, m_ref, v_ref, g_out, d_out, m_out, v_out):
        gv = ga_ref[0]
        for j in range(1, N_DEV):
            gv = gv + ga_ref[j]
        delta, m2, v2 = _adamw_math(w_ref[...], gv, m_ref[...], v_ref[...])
        g_out[...] = gv
        d_out[...] = delta
        m_out[...] = m2
        v_out[...] = v2

    return pl.pallas_call(body, name=name, out_shape=[jax.ShapeDtypeStruct((r, c), F32)] * 4)(g_all, w, m, v)


_ANY = pl.BlockSpec(memory_space=pl.ANY)


_HBM = pl.BlockSpec(memory_space=pltpu.HBM)
_SEM = pl.BlockSpec(memory_space=pltpu.SEMAPHORE)
_EFFECT = pltpu.SideEffectType.DATAFLOW_SIDE_EFFECTING


def _peer_chips():
    x, y = lax.axis_index("x"), lax.axis_index("y")
    return 2 * x + y, [(1 - x, y), (x, 1 - y), (1 - x, 1 - y)]


def _exchange_copy(srcs, lands, send_sems, recv_sems, k, p, kind):
    c = lax.axis_index("c")
    if kind == "swap":
        return pltpu.make_async_remote_copy(
            src_ref=srcs[k], dst_ref=lands[k], send_sem=send_sems.at[k], recv_sem=recv_sems.at[k],
            device_id=(lax.axis_index("x"), lax.axis_index("y"), 1 - c), device_id_type=MESH)
    me, peers = _peer_chips()
    px, py = peers[p]
    return pltpu.make_async_remote_copy(
        src_ref=srcs[k].at[2 * px + py] if kind == "scatter" else srcs[k].at[me],
        dst_ref=lands[k].at[p] if kind == "scatter" else lands[k].at[me],
        send_sem=send_sems.at[3 * k + p], recv_sem=recv_sems.at[3 * k + p],
        device_id=(px, py, c), device_id_type=MESH)


def _exchange_start(srcs, lands, after, *, kind, name):
    n = len(srcs)
    npeer = 1 if kind == "swap" else 3
    bufs = list(srcs) + (list(lands) if lands is not None else [])
    nb = len(bufs)

    def body(*refs):
        buf_refs, send_sems, recv_sems = refs[:nb], refs[nb + 1], refs[nb + 2]
        token = refs[-1]
        s_refs = buf_refs[:n]
        l_refs = buf_refs[n:] if lands is not None else s_refs
        for k in range(n):
            for p in range(npeer):
                _exchange_copy(s_refs, l_refs, send_sems, recv_sems, k, p, kind).start()
        token[...] = jnp.zeros_like(token)

    out = pl.pallas_call(
        body, name=name,
        out_shape=(pltpu.SemaphoreType.DMA((npeer * n,)), pltpu.SemaphoreType.DMA((npeer * n,)),
                   *[pltpu.HBM(b.shape, b.dtype) for b in bufs], jax.ShapeDtypeStruct((SUBLANE, LANE), F32)),
        in_specs=[_HBM] * nb + [_ANY],
        out_specs=(_SEM, _SEM, *[_HBM] * nb, pl.BlockSpec(memory_space=pltpu.VMEM)),
        input_output_aliases={i: 2 + i for i in range(nb)},
        compiler_params=pltpu.CompilerParams(has_side_effects=_EFFECT),
    )(*[pltpu.with_memory_space_constraint(b, pltpu.HBM) for b in bufs], after)
    send_sems, recv_sems = out[0], out[1]
    thru = out[2:2 + nb]
    return send_sems, recv_sems, list(thru[:n]), (list(thru[n:]) if lands is not None else None), out[-1]


def _exchange_wait(send_sems, recv_sems, srcs, lands, after, *, kind, name):
    n = len(srcs)
    npeer = 1 if kind == "swap" else 3
    bufs = list(srcs) + (list(lands) if lands is not None else [])
    nb = len(bufs)

    def body(*refs):
        buf_refs, send_sems_ref, recv_sems_ref = refs[:nb], refs[nb], refs[nb + 1]
        s_refs = buf_refs[:n]
        l_refs = buf_refs[n:] if lands is not None else s_refs
        for k in range(n):
            for p in range(npeer):
                cp = _exchange_copy(s_refs, l_refs, send_sems_ref, recv_sems_ref, k, p, kind)
                cp.wait_send()
                cp.wait_recv()

    out = pl.pallas_call(
        body, name=name, out_shape=tuple(pltpu.HBM(b.shape, b.dtype) for b in bufs),
        in_specs=[_HBM] * nb + [_SEM, _SEM, _ANY], out_specs=tuple([_HBM] * nb),
        input_output_aliases={i: i for i in range(nb)},
        compiler_params=pltpu.CompilerParams(has_side_effects=_EFFECT),
    )(*bufs, send_sems, recv_sems, after)
    return list(out)


def _sibling_exchange(srcs, *, name):
    n = len(srcs)

    def body(*refs):
        src, out = refs[:n], refs[n:2 * n]
        send_sems, recv_sems = refs[2 * n:]
        sibling = (lax.axis_index("x"), lax.axis_index("y"), 1 - lax.axis_index("c"))
        copies = [pltpu.make_async_remote_copy(src_ref=src[k], dst_ref=out[k], send_sem=send_sems.at[k],
                                               recv_sem=recv_sems.at[k], device_id=sibling, device_id_type=MESH)
                  for k in range(n)]
        for cp in copies:
            cp.start()
        for cp in copies:
            cp.wait_recv()
        for cp in copies:
            cp.wait_send()

    return pl.pallas_call(
        body, name=name, in_specs=[_ANY] * n, out_specs=[_ANY] * n,
        out_shape=[jax.ShapeDtypeStruct(a.shape, a.dtype) for a in srcs],
        scratch_shapes=[pltpu.SemaphoreType.DMA((n,)), pltpu.SemaphoreType.DMA((n,))],
    )(*srcs)


def _all_gather_small(block, *, name):
    m_per, ncol = block.shape

    def body(x_ref, out_ref, send_sems, recv_sems, local_sem):
        x, y, c = lax.axis_index("x"), lax.axis_index("y"), lax.axis_index("c")
        me, sibling = (x, y, c), (x, y, 1 - c)
        chips = [(1 - x, y), (x, 1 - y), (1 - x, 1 - y)]

        def rows(px, py, pc):
            return out_ref.at[pl.ds((4 * px + 2 * py + pc) * m_per, m_per), :]

        def copy(k, blk, to, src=None):
            return pltpu.make_async_remote_copy(
                src_ref=rows(*blk) if src is None else src, dst_ref=rows(*blk),
                send_sem=send_sems.at[k], recv_sem=recv_sems.at[k], device_id=to, device_id_type=MESH)

        mine = pltpu.make_async_copy(x_ref, rows(*me), local_sem)
        mine.start()
        first = [copy(0, me, sibling, src=x_ref)]
        first += [copy(1 + j, me, (*chip, c), src=x_ref) for j, chip in enumerate(chips)]
        for cp in first:
            cp.start()
        passed = [copy(4 + j, (*chip, c), sibling) for j, chip in enumerate(chips)]
        for j, chip in enumerate(chips):
            copy(1 + j, (*chip, c), me).wait_recv()
            passed[j].start()
        copy(0, sibling, me).wait_recv()
        for j, chip in enumerate(chips):
            copy(4 + j, (*chip, 1 - c), me).wait_recv()
        for cp in first + passed:
            cp.wait_send()
        mine.wait()

    return pl.pallas_call(
        body, name=name, out_shape=jax.ShapeDtypeStruct((N_DEV * m_per, ncol), block.dtype),
        in_specs=[pl.BlockSpec(memory_space=pltpu.VMEM)], out_specs=pl.BlockSpec(memory_space=pltpu.VMEM),
        scratch_shapes=[pltpu.SemaphoreType.DMA((7,)), pltpu.SemaphoreType.DMA((7,)), pltpu.SemaphoreType.DMA],
    )(block)


def _rope_angles(pos, dim):
    inv = ROPE_THETA ** (-jnp.arange(0, dim, 2, dtype=F32) / dim)
    return pos.astype(F32)[:, None] * inv[None, :]


def _rope_tables(s):
    pos = jnp.arange(s, dtype=jnp.int32)
    rows = s // GRID_W
    row = jnp.repeat(jnp.arange(rows, dtype=jnp.int32), GRID_W)
    col = jnp.tile(jnp.arange(GRID_W, dtype=jnp.int32), rows)
    a1 = _rope_angles(pos, HEAD_DIM)
    aa = _rope_angles(pos, A_ROPE)
    ar = _rope_angles(row, HEAD_DIM // 2)
    ac = _rope_angles(col, HEAD_DIM // 2)
    one = jnp.ones((s, LANE), F32)
    zero = jnp.zeros((s, LANE), F32)
    pad = LANE - A_ROPE
    cos_a = jnp.concatenate([one, jnp.cos(aa), jnp.cos(aa), jnp.ones((s, pad), F32)], axis=1)
    sin_a = jnp.concatenate([zero, -jnp.sin(aa), jnp.sin(aa), jnp.zeros((s, pad), F32)], axis=1)
    cos_b = jnp.concatenate([jnp.cos(a1), jnp.cos(a1)], axis=1)
    sin_b = jnp.concatenate([-jnp.sin(a1), jnp.sin(a1)], axis=1)
    cos_c = jnp.concatenate([jnp.cos(ar), jnp.cos(ar), jnp.cos(ac), jnp.cos(ac)], axis=1)
    sin_c = jnp.concatenate([-jnp.sin(ar), jnp.sin(ar), -jnp.sin(ac), jnp.sin(ac)], axis=1)
    return (cos_a, sin_a), (cos_b, sin_b), (cos_c, sin_c)


def _band_table(rc, s):
    reach = max((win // (2 * d)) * d for win, d in B_PATTERNS)
    r = -(-reach // rc) * rc
    w = min(s, rc + 2 * r)
    j = jnp.arange(rc, dtype=jnp.int32)[:, None]
    x = jnp.arange(2 * w - rc, dtype=jnp.int32)[None, :]
    rel = x - (w - rc) - j
    mult = jnp.zeros(rel.shape, F32)
    for win, d in B_PATTERNS:
        mult = mult + jnp.logical_and(rel % d == 0, jnp.abs(rel) <= (win // (2 * d)) * d).astype(F32)
    return jnp.where(mult > 0, jnp.log2(jnp.maximum(mult, 1.0)), NEG), (w, r)


_BIG = ("w_in", "a_w_uq", "a_w_ukv", "w_out", "w_gate", "w_up", "w_down")
_SMALL = ("attn_norm", "a_q_norm", "a_kv_norm", "c_q_norm", "c_k_norm", "out_norm", "ffn_norm", "final_norm")
_WEIGHTS = ("attn_norm", "w_in", "a_q_norm", "a_w_uq", "a_kv_norm", "a_w_ukv", "c_q_norm", "c_k_norm", "out_norm",
            "w_out", "ffn_norm", "w_gate", "w_up", "w_down", "final_norm")


_ATTN = ("w_in", "a_w_uq", "a_w_ukv")
_FFN = ("w_out", "w_gate", "w_up", "w_down")


def _from_cols(a):
    return jnp.transpose(a, (1, 0, 2)).reshape(a.shape[1], N_CHIPS * a.shape[2])


def _from_rows(a):
    return a.reshape(N_CHIPS * a.shape[1], a.shape[2])


def _to_cols(a):
    return jnp.transpose(a.reshape(a.shape[0], N_CHIPS, a.shape[1] // N_CHIPS), (1, 0, 2))


def _to_rows(a):
    return a.reshape(N_CHIPS, a.shape[0] // N_CHIPS, a.shape[1])


def _assemble_attn(gw):
    w_in_t, uq, ukv = _from_rows(gw[0]), _from_cols(gw[1]), _from_cols(gw[2])
    d = w_in_t.shape[1]
    w_all = jnp.concatenate([w_in_t[:IN_A], jnp.zeros((A_PAD - IN_A, d), BF16), w_in_t[IN_A:]], axis=0)
    uq = uq.reshape(A_Q_RANK, A_HEADS, A_NOPE + A_ROPE)
    uq = jnp.pad(uq, ((0, 0), (0, 0), (0, A_QK - A_NOPE - A_ROPE))).reshape(A_Q_RANK, A_HEADS * A_QK)
    return dict(w_all=w_all, uq=uq, ukv=ukv)


def _assemble_ffn(gw):
    return dict(w_out=_from_rows(gw[0]), w_gate=gw[1], w_up=gw[2], w_down=_from_rows(gw[3]))


def _split_attn_grads(gl):
    w_all = gl["w_all"]
    w_in_t = jnp.concatenate([w_all[:IN_A], w_all[A_PAD:]], axis=0)
    uq = gl["uq"].reshape(A_Q_RANK, A_HEADS, A_QK)[:, :, :A_NOPE + A_ROPE].reshape(A_Q_RANK, A_HEADS * (A_NOPE + A_ROPE))
    return [_to_rows(w_in_t), _to_cols(uq), _to_cols(gl["ukv"])]


def _split_ffn_grads(gl):
    return [_to_rows(gl["w_out"]), gl["w_gate"], gl["w_up"], _to_rows(gl["w_down"])]


def _tie(a, token):
    return a + token[0:1, 0:1]


def _layer_fwd(x, wl, ffn_weights, sm, tabs, bias, t):
    s = x.shape[0]
    (cos_a, sin_a), (cos_b, sin_b), (cos_c, sin_c) = tabs
    h = _norm_fwd(x, sm["attn_norm"], wb=x.shape[1], cb=0, nb=1, shared_gain=True, out_dtype=BF16, name="attn_norm_fwd")
    p = _matmul(h, wl["w_all"], mode="nt", out_dtype=F32, name="in_proj", tm=1024, tn=1280)
    cq_n = _norm_fwd(p, sm["a_q_norm"], wb=A_Q_RANK, cb=0, nb=1, shared_gain=True, out_dtype=BF16, name="a_q_norm_fwd")
    ckv_n = _norm_fwd(p, sm["a_kv_norm"], wb=A_KV_RANK, cb=1, nb=1, shared_gain=True, out_dtype=BF16, name="a_kv_norm_fwd")
    qa_raw = _matmul(cq_n, wl["uq"], mode="nn", out_dtype=F32, name="a_uq", tm=1024, tn=1024)
    kv = _matmul(ckv_n, wl["ukv"], mode="nn", out_dtype=BF16, name="a_ukv", tm=1024, tn=1024)
    qa =_rope(qa_raw, cos_a, sin_a, tw=A_QK, cb=0, nb=A_HEADS, half=A_ROPE // 2, sign=1, out_dtype=BF16, name="a_rope_q")
    ka = _latent_keys(kv, p, cos_a, sin_a, kr_cb=PB_KR, name="a_keys")
    oa, lse_a = _flash_fwd(qa, ka, kv, None, hkv=A_HEADS, g=1, dqk=A_QK, q_cb=0, k_cb=0, v_cb=1, v_step=2,
                           scale=(A_NOPE + A_ROPE) ** -0.5, tq=t, band=None, name="a_flash_fwd")
    table, band = bias
    qb = _rope(p, cos_b, sin_b, tw=LANE, cb=PB_BQ, nb=B_HEADS, half=HEAD_DIM // 2, sign=1, out_dtype=BF16, name="b_rope_q")
    kb = _rope(p, cos_b, sin_b, tw=LANE, cb=PB_BK, nb=B_HEADS, half=HEAD_DIM // 2, sign=1, out_dtype=BF16, name="b_rope_k")
    vb = _cast_cols(p, cb=PB_BV, nb=B_HEADS, name="b_cast_v")
    ob, lse_b = _flash_fwd(qb, kb, vb, table, hkv=B_HEADS, g=1, dqk=LANE, q_cb=0, k_cb=0, v_cb=0, v_step=1,
                           scale=HEAD_DIM ** -0.5, tq=t, band=band, name="b_flash_fwd")
    qn = _norm_fwd(p, sm["c_q_norm"], wb=LANE, cb=PB_CQH, nb=C_HEADS, shared_gain=True, out_dtype=F32, name="c_q_norm_fwd")
    kn = _norm_fwd(p, sm["c_k_norm"], wb=LANE, cb=PB_CKH, nb=C_KV_HEADS, shared_gain=True, out_dtype=F32, name="c_k_norm_fwd")
    qc = _rope(qn, cos_c, sin_c, tw=LANE, cb=0, nb=C_HEADS, half=HEAD_DIM // 4, sign=1, out_dtype=BF16, name="c_rope_q")
    kc = _rope(kn, cos_c, sin_c, tw=LANE, cb=0, nb=C_KV_HEADS, half=HEAD_DIM // 4, sign=1, out_dtype=BF16, name="c_rope_k")
    vc = _cast_cols(p, cb=PB_CVH, nb=C_KV_HEADS, name="c_cast_v")
    oc, lse_c = _flash_fwd(qc, kc, vc, None, hkv=C_KV_HEADS, g=C_GROUP, dqk=LANE, q_cb=0, k_cb=0, v_cb=0, v_step=1,
                           scale=HEAD_DIM ** -0.5, tq=t, band=None, name="c_flash_fwd")
    g_out = sm["out_norm"]
    ga, gb, gc = g_out[:, :A_WIDTH], g_out[:, A_WIDTH:A_WIDTH + B_WIDTH], g_out[:, A_WIDTH + B_WIDTH:]
    ya = _norm_fwd(oa, ga, wb=A_WIDTH, cb=0, nb=1, shared_gain=True, out_dtype=BF16, name="out_norm_a_fwd")
    yb = _norm_fwd(ob, gb, wb=B_WIDTH, cb=0, nb=1, shared_gain=True, out_dtype=BF16, name="out_norm_b_fwd")
    yc = _norm_fwd(oc, gc, wb=C_WIDTH, cb=0, nb=1, shared_gain=True, out_dtype=BF16, name="out_norm_c_fwd")
    y = jnp.concatenate([ya, yb, yc], axis=1)
    wl = {**wl, **ffn_weights(y)}
    x1 = _matmul(y, wl["w_out"], mode="nn", out_dtype=F32, name="out_proj", add=x, tm=1024, tn=1024)
    h2 = _norm_fwd(x1, sm["ffn_norm"], wb=x.shape[1], cb=0, nb=1, shared_gain=True, out_dtype=BF16, name="ffn_norm_fwd")
    gate, up, act = _ffn_up(h2, wl["w_gate"], wl["w_up"], name="ffn_up")
    x2 = _matmul(act, wl["w_down"], mode="nn", out_dtype=F32, name="ffn_down", add=x1, tm=1024, tn=512)
    saved = dict(x=x, h=h, p=p, cq_n=cq_n, ckv_n=ckv_n, kv=kv, qa=qa, ka=ka, oa=oa, lse_a=lse_a, qb=qb, kb=kb, vb=vb, ob=ob,
                 lse_b=lse_b, qc=qc, kc=kc, vc=vc, oc=oc, lse_c=lse_c, y=y, x1=x1, h2=h2, gate=gate, up=up, act=act)
    return x2, saved, wl


def _layer_bwd(dx2, dx2b, sv, wl, sm, tabs, bias, t, send_ffn, send_attn):
    s, d = dx2.shape
    (cos_a, sin_a), (cos_b, sin_b), (cos_c, sin_c) = tabs
    gw, gs = {}, {}
    dgate, dup = _ffn_down_dx(dx2b, wl["w_down"], sv["gate"], sv["up"], name="ffn_down_dx")
    gw["w_down"] = _matmul(sv["act"], dx2b, mode="tn", out_dtype=BF16, name="ffn_down_dw", tm=512, tn=2048)
    dh2 = _ffn_up_dx(dgate, dup, wl["w_gate"], wl["w_up"], name="ffn_up_dx")
    gw["w_gate"] = _matmul(sv["h2"], dgate, mode="tn", out_dtype=BF16, name="ffn_gate_dw", tm=1024, col_shards=True)
    gw["w_up"] = _matmul(sv["h2"], dup, mode="tn", out_dtype=BF16, name="ffn_up_dw", tm=1024, col_shards=True)
    dx1, gs["ffn_norm"], dx1b = _norm_bwd(sv["x1"], sm["ffn_norm"], dh2, wb=d, cb=0, nb=1, shared_gain=True,
                                          out_dtype=F32, name="ffn_norm_bwd", add=dx2, bf16_copy=True)
    w_out = wl["w_out"]
    dya = _matmul(dx1b, w_out[:A_WIDTH], mode="nt", out_dtype=F32, name="out_proj_dx_a", tm=1024, tn=1024)
    dyb = _matmul(dx1b, w_out[A_WIDTH:A_WIDTH + B_WIDTH], mode="nt", out_dtype=F32, name="out_proj_dx_b", tm=1024, tn=1024)
    dyc = _matmul(dx1b, w_out[A_WIDTH + B_WIDTH:], mode="nt", out_dtype=F32, name="out_proj_dx_c", tm=1024, tn=1024)
    gw["w_out"] = _matmul(sv["y"], dx1b, mode="tn", out_dtype=BF16, name="out_proj_dw", tm=512, tn=2048)
    token = send_ffn(gw)
    g_out = _tie(sm["out_norm"], token)
    ga, gb, gc = g_out[:, :A_WIDTH], g_out[:, A_WIDTH:A_WIDTH + B_WIDTH], g_out[:, A_WIDTH + B_WIDTH:]
    doa, dga = _norm_bwd(sv["oa"], ga, dya, wb=A_WIDTH, cb=0, nb=1, shared_gain=True, out_dtype=F32, name="out_norm_a_bwd")
    dob, dgb = _norm_bwd(sv["ob"], gb, dyb, wb=B_WIDTH, cb=0, nb=1, shared_gain=True, out_dtype=F32, name="out_norm_b_bwd")
    doc, dgc = _norm_bwd(sv["oc"], gc, dyc, wb=C_WIDTH, cb=0, nb=1, shared_gain=True, out_dtype=F32, name="out_norm_c_bwd")
    gs["out_norm"] = jnp.concatenate([dga, dgb, dgc], axis=1)
    p = sv["p"]
    dqc, dkc, dvc = _flash_bwd(sv["qc"], sv["kc"], sv["vc"], sv["oc"], doc, sv["lse_c"], None, hkv=C_KV_HEADS,
                               g=C_GROUP, dqk=LANE, q_cb=0, k_cb=0, v_cb=0, v_step=1, scale=HEAD_DIM ** -0.5,
                               tq=t, band=None, name="c_flash_bwd")
    dqn = _rope(dqc, cos_c, sin_c, tw=LANE, cb=0, nb=C_HEADS, half=HEAD_DIM // 4, sign=-1, out_dtype=F32, name="c_rope_q_bwd")
    dkn = _rope(dkc, cos_c, sin_c, tw=LANE, cb=0, nb=C_KV_HEADS, half=HEAD_DIM // 4, sign=-1, out_dtype=F32, name="c_rope_k_bwd")
    dpcq, gs["c_q_norm"] = _norm_bwd(p, sm["c_q_norm"], dqn, wb=LANE, cb=PB_CQH, nb=C_HEADS, shared_gain=True,
                                     out_dtype=BF16, name="c_q_norm_bwd")
    dpck, gs["c_k_norm"] = _norm_bwd(p, sm["c_k_norm"], dkn, wb=LANE, cb=PB_CKH, nb=C_KV_HEADS, shared_gain=True,
                                     out_dtype=BF16, name="c_k_norm_bwd")
    table, band = bias
    dqb, dkb, dvb = _flash_bwd(sv["qb"], sv["kb"], sv["vb"], sv["ob"], dob, sv["lse_b"], table, hkv=B_HEADS, g=1,
                               dqk=LANE, q_cb=0, k_cb=0, v_cb=0, v_step=1, scale=HEAD_DIM ** -0.5, tq=t, band=band,
                               name="b_flash_bwd")
    dpbq = _rope(dqb, cos_b, sin_b, tw=LANE, cb=0, nb=B_HEADS, half=HEAD_DIM // 2, sign=-1, out_dtype=BF16, name="b_rope_q_bwd")
    dpbk = _rope(dkb, cos_b, sin_b, tw=LANE, cb=0, nb=B_HEADS, half=HEAD_DIM // 2, sign=-1, out_dtype=BF16, name="b_rope_k_bwd")
    dqa, dka, dva = _flash_bwd(sv["qa"], sv["ka"], sv["kv"], sv["oa"], doa, sv["lse_a"], None, hkv=A_HEADS, g=1,
                               dqk=A_QK, q_cb=0, k_cb=0, v_cb=1, v_step=2, scale=(A_NOPE + A_ROPE) ** -0.5,
                               tq=t, band=None, name="a_flash_bwd")
    dqa_raw = _rope(dqa, cos_a, sin_a, tw=A_QK, cb=0, nb=A_HEADS, half=A_ROPE // 2, sign=-1, out_dtype=BF16, name="a_rope_q_bwd")
    dkv, dkr = _latent_keys_bwd(dka, dva, cos_a, sin_a, name="a_keys_bwd")
    dckv_n = _matmul(dkv, wl["ukv"], mode="nt", out_dtype=F32, name="a_ukv_dx", tm=1024, tn=512)
    gw["ukv"] = _matmul(sv["ckv_n"], dkv, mode="tn", out_dtype=BF16, name="a_ukv_dw", tm=512, tn=1024)
    dcq_n = _matmul(dqa_raw, wl["uq"], mode="nt", out_dtype=F32, name="a_uq_dx", tm=1024, tn=512)
    gw["uq"] = _matmul(sv["cq_n"], dqa_raw, mode="tn", out_dtype=BF16, name="a_uq_dw", tm=512, tn=1024)
    dcq, gs["a_q_norm"] = _norm_bwd(p, sm["a_q_norm"], dcq_n, wb=A_Q_RANK, cb=0, nb=1, shared_gain=True, out_dtype=BF16,
                                    name="a_q_norm_bwd")
    dckv, gs["a_kv_norm"] = _norm_bwd(p, sm["a_kv_norm"], dckv_n, wb=A_KV_RANK, cb=1, nb=1, shared_gain=True,
                                      out_dtype=BF16, name="a_kv_norm_bwd")
    dp = jnp.concatenate([dcq, dckv, dkr, jnp.zeros((s, A_PAD - (PB_KR + 1) * LANE), BF16), dpbq, dpbk,
                          dvb.astype(BF16), dpcq, dpck, dvc.astype(BF16)], axis=1)
    gw["w_all"] = _matmul(dp, sv["h"], mode="tn", out_dtype=BF16, name="in_proj_dw", tm=640, tn=2048)
    token = send_attn(gw)
    dh = _matmul(dp, wl["w_all"], mode="nn", out_dtype=F32, name="in_proj_dx", tm=1024, tn=512, after=token)
    dx, gs["attn_norm"], dxb = _norm_bwd(sv["x"], sm["attn_norm"], dh, wb=d, cb=0, nb=1, shared_gain=True,
                                         out_dtype=F32, name="attn_norm_bwd", add=dx1, bf16_copy=True)
    return dx, dxb, gs, token


def _pack_small(vals):
    flat = jnp.concatenate([vals[n].reshape(-1).astype(F32) for n in _SMALL])
    tile = SUBLANE * LANE
    padded = -(-flat.shape[0] // tile) * tile
    return jnp.pad(flat, (0, padded - flat.shape[0])).reshape(padded // LANE, LANE)


def _unpack_small(packed, like):
    flat = packed.reshape(-1)
    out, off = {}, 0
    for n in _SMALL:
        size = math.prod(like[n].shape)
        out[n] = flat[off:off + size].reshape(like[n].shape)
        off += size
    return out


def kernel(x, attn_norm, w_in, a_q_norm, a_w_uq, a_kv_norm, a_w_ukv, c_q_norm, c_k_norm, out_norm, w_out, ffn_norm, w_gate, w_up, w_down, final_norm, loss_target, m_attn_norm, m_w_in, m_a_q_norm, m_a_w_uq, m_a_kv_norm, m_a_w_ukv, m_c_q_norm, m_c_k_norm, m_out_norm, m_w_out, m_ffn_norm, m_w_gate, m_w_up, m_w_down, m_final_norm, v_attn_norm, v_w_in, v_a_q_norm, v_a_w_uq, v_a_kv_norm, v_a_w_ukv, v_c_q_norm, v_c_k_norm, v_out_norm, v_w_out, v_ffn_norm, v_w_gate, v_w_up, v_w_down, v_final_norm):
    w = dict(attn_norm=attn_norm, w_in=w_in, a_q_norm=a_q_norm, a_w_uq=a_w_uq, a_kv_norm=a_kv_norm, a_w_ukv=a_w_ukv,
             c_q_norm=c_q_norm, c_k_norm=c_k_norm, out_norm=out_norm, w_out=w_out, ffn_norm=ffn_norm, w_gate=w_gate,
             w_up=w_up, w_down=w_down, final_norm=final_norm)
    m = dict(attn_norm=m_attn_norm, w_in=m_w_in, a_q_norm=m_a_q_norm, a_w_uq=m_a_w_uq, a_kv_norm=m_a_kv_norm,
             a_w_ukv=m_a_w_ukv, c_q_norm=m_c_q_norm, c_k_norm=m_c_k_norm, out_norm=m_out_norm, w_out=m_w_out,
             ffn_norm=m_ffn_norm, w_gate=m_w_gate, w_up=m_w_up, w_down=m_w_down, final_norm=m_final_norm)
    v = dict(attn_norm=v_attn_norm, w_in=v_w_in, a_q_norm=v_a_q_norm, a_w_uq=v_a_w_uq, a_kv_norm=v_a_kv_norm,
             a_w_ukv=v_a_w_ukv, c_q_norm=v_c_q_norm, c_k_norm=v_c_k_norm, out_norm=v_out_norm, w_out=v_w_out,
             ffn_norm=v_ffn_norm, w_gate=v_w_gate, w_up=v_w_up, w_down=v_w_down, final_norm=v_final_norm)
    _, s, d = x.shape
    depth = attn_norm.shape[0]

    def as_stored(a, n):
        return jnp.swapaxes(a, 1, 2) if n == "w_in" else a
    t = _pick(s, 1024)

    me = (2 * lax.axis_index("x") + lax.axis_index("y")).astype(jnp.int32).reshape(1)

    gathers, after = {}, me
    for l in range(depth):
        for group, names in (("attn", _ATTN), ("ffn", _FFN)):
            bufs = [_cast_to_slot(as_stored(w[n], n), me, layer=l, name=f"cast_{n}")
                    for n in names]
            send_sems, recv_sems, bufs, _, after = _exchange_start(bufs, None, after, kind="gather",
                                                                   name=f"gather_start_{group}{l}")
            gathers[group, l] = (send_sems, recv_sems, bufs)
    all_started = after

    def gathered(group, l, after):
        send_sems, recv_sems, bufs = gathers[group, l]
        return _exchange_wait(send_sems, recv_sems, bufs, None, after, kind="gather", name=f"gather_wait_{group}{l}")

    tabs = _rope_tables(s)
    bias = _band_table(min(t, ATTN_ROW_CHUNK), s)

    xs = x.reshape(s, d)
    saved, wls, sms = [], [], []
    for l in range(depth):
        wl = _assemble_attn(gathered("attn", l, all_started if l == 0 else xs))
        sm = {n: w[n][l][None, :] for n in _SMALL if n != "final_norm"}
        xs, sv, wl = _layer_fwd(xs, wl, lambda after, l=l: _assemble_ffn(gathered("ffn", l, after)), sm, tabs, bias, t)
        saved.append(sv)
        wls.append(wl)
        sms.append(sm)
    dx, g_final, loss_row, dxb = _final_loss(xs, final_norm[None, :], loss_target.reshape(s, d), name="final_loss")
    loss = lax.psum(loss_row[0, 0], ("x", "y", "c"))

    sends = {}

    def send(group, l, srcs, after):
        lands = [lax.empty((3,) + a.shape[1:], BF16) for a in srcs]
        send_sems, recv_sems, srcs, lands, token = _exchange_start(srcs, lands, after, kind="scatter",
                                                                   name=f"scatter_start_{group}{l}")
        sends[group, l] = (send_sems, recv_sems, srcs, lands)
        return token

    gs_layers, token = [None] * depth, all_started
    for l in reversed(range(depth)):
        dx, dxb, gs_layers[l], token = _layer_bwd(
            dx, dxb, saved[l], wls[l], sms[l], tabs, bias, t,
            lambda gw, l=l, tk=token: send("ffn", l, _split_ffn_grads(gw), tk),
            lambda gw, l=l: send("attn", l, _split_attn_grads(gw), dx))
    grad_x = dx.reshape(x.shape)

    srcs, lands = {}, {}

    def arrive(key, after):
        send_sems, recv_sems, s_bufs, l_bufs = sends[key]
        got = _exchange_wait(send_sems, recv_sems, s_bufs, l_bufs, after, kind="scatter",
                             name=f"scatter_wait_{key[0]}{key[1]}")
        for k, n in enumerate(_ATTN if key[0] == "attn" else _FFN):
            srcs[n, key[1]], lands[n, key[1]] = got[k], got[len(s_bufs) + k]

    def summed(names):
        return [_sum_parts([srcs[n, l] for l in range(depth)], [lands[n, l] for l in range(depth)], me, name="sum_" + n)
                for n in names]

    last = ("attn", 0)
    for key in sends:
        if key != last:
            arrive(key, token)
    sums_ffn = summed(_FFN)
    swap = _exchange_start(sums_ffn, [lax.empty(a.shape, F32) for a in sums_ffn], token, kind="swap",
                           name="swap_start_ffn")
    arrive(last, swap[4])
    sums_attn = summed(_ATTN)
    sib_attn = list(_sibling_exchange(sums_attn, name="swap_core_sums_attn"))
    swapped = _exchange_wait(swap[0], swap[1], swap[2], swap[3], sib_attn[0], kind="swap", name="swap_wait_ffn")
    mine_of = dict(zip(_FFN + _ATTN, swapped[:len(_FFN)] + sums_attn))
    other_of = dict(zip(_FFN + _ATTN, swapped[len(_FFN):] + sib_attn))
    grads, deltas, new_m, new_v = {}, {}, {}, {}
    for n in _BIG:
        res = _adamw(mine_of[n], other_of[n], as_stored(w[n], n), as_stored(m[n], n), as_stored(v[n], n), name="adamw_" + n)
        grads[n], deltas[n], new_m[n], new_v[n] = [as_stored(r, n) for r in res]

    gsm = {n: jnp.stack([gs_layers[l][n][0] for l in range(depth)]) for n in _SMALL if n != "final_norm"}
    gsm["final_norm"] = g_final[0]
    packed = _pack_small(gsm)
    everyone = _all_gather_small(packed, name="gather_gain_grads").reshape(N_DEV, packed.shape[0], LANE)
    res = _small_adamw(everyone, _pack_small(w), _pack_small(m), _pack_small(v), name="adamw_gains")
    for dst, r in zip((grads, deltas, new_m, new_v), res):
        dst.update(_unpack_small(r, w))

    return (loss, grad_x, *[grads[n] for n in _WEIGHTS], *[deltas[n] for n in _WEIGHTS],
            *[new_m[n] for n in _WEIGHTS], *[new_v[n] for n in _WEIGHTS])
```

```python
import functools
import math

import jax
import jax.numpy as jnp
import numpy as np
from jax import lax
from jax.experimental import pallas as pl
from jax.experimental.pallas import tpu as pltpu

F32 = jnp.float32
BF16 = jnp.bfloat16
MESH = pl.DeviceIdType.MESH

HEAD_DIM = 128
ROPE_THETA = 10000.0
GRID_W = 64
EPS = 1e-6
NEG = -1e30
A_HEADS, A_Q_RANK, A_KV_RANK, A_NOPE, A_ROPE, A_V = 4, 512, 512, 128, 64, 128
B_HEADS = 6
B_PATTERNS = ((128, 1), (512, 4), (2048, 16))
C_HEADS, C_KV_HEADS = 6, 2
C_GROUP = C_HEADS // C_KV_HEADS
A_WIDTH, B_WIDTH, C_WIDTH = A_HEADS * A_V, B_HEADS * HEAD_DIM, C_HEADS * HEAD_DIM
IN_A = A_Q_RANK + A_KV_RANK + A_ROPE
IN_B = 3 * B_WIDTH
IN_C = C_WIDTH + 2 * C_KV_HEADS * HEAD_DIM
ADAM_LR, ADAM_B1, ADAM_B2, ADAM_EPS, ADAM_WD, ADAM_STEP = 0.001, 0.9, 0.999, 1e-08, 0.01, 10

LANE = 128
SUBLANE = 8
VMEM_BYTES_V7X = 64 * 1024 * 1024
VMEM_LIMIT_CAP = VMEM_BYTES_V7X - 8 * 1024 * 1024
N_CHIPS = 4
N_DEV = 8

A_PAD = 12 * LANE
PB_CQ, PB_CKV, PB_KR = 0, 4, 8
PB_BQ, PB_BK, PB_BV = 12, 18, 24
PB_CQH, PB_CKH, PB_CVH = 30, 36, 38
NP = 40 * LANE
A_QK = 2 * LANE


def _pick(n, cap, mult=LANE):
    if n <= cap:
        return n
    t = cap - cap % mult
    while t >= mult:
        if n % t == 0:
            return t
        t -= mult
    return n


def _rows_for(width_bytes, n_rows, target=2 * 1024 * 1024):
    return _pick(n_rows, max(SUBLANE, target // max(width_bytes, 1)), SUBLANE)


def _tile2(rows, cols, target):
    tc = _pick(cols, 4 * LANE)
    if tc < 4 * LANE:
        tc = cols
    fits = [t for t in range(SUBLANE, rows + 1, SUBLANE) if rows % t == 0] or [rows]
    return min(fits, key=lambda t: abs(math.log(t * tc * 4 / target))), tc


def _params(est_bytes):
    limit = int(min(max(est_bytes + (4 << 20), 32 << 20), VMEM_LIMIT_CAP))
    return pltpu.CompilerParams(vmem_limit_bytes=limit)


def _isz(x):
    return jnp.dtype(x.dtype).itemsize


def _hbm(shape, dtype):
    return pltpu.HBM(shape, dtype)


def _pin(*arrays):
    return [pltpu.with_memory_space_constraint(a, pltpu.HBM) for a in arrays]


_DIMS = {"nn": (((1,), (0,)), ((), ())), "nt": (((1,), (1,)), ((), ())), "tn": (((0,), (0,)), ((), ()))}


def _matmul(a, b, *, mode, out_dtype, name, add=None, tm=512, tn=512, col_shards=False, after=None):
    if mode == "tn":
        (k, m), (k2, n) = a.shape, b.shape
    elif mode == "nt":
        (m, k), (n, k2) = a.shape, b.shape
    else:
        (m, k), (k2, n) = a.shape, b.shape
    assert k == k2, (a.shape, b.shape, mode)
    tm, tn = _pick(m, tm), (n // N_CHIPS if col_shards else _pick(n, tn))
    a_spec = pl.BlockSpec((k, tm), lambda i, j: (0, i)) if mode == "tn" else pl.BlockSpec((tm, k), lambda i, j: (i, 0))
    b_spec = pl.BlockSpec((tn, k), lambda i, j: (j, 0)) if mode == "nt" else pl.BlockSpec((k, tn), lambda i, j: (0, j))
    o_spec = pl.BlockSpec((None, tm, tn), lambda i, j: (j, i, 0)) if col_shards else pl.BlockSpec((tm, tn), lambda i, j: (i, j))
    dims = _DIMS[mode]

    def body(*refs):
        a_ref, b_ref, o_ref = refs[0], refs[1], refs[-1]
        acc = lax.dot_general(a_ref[...].astype(BF16), b_ref[...].astype(BF16), dims, preferred_element_type=F32)
        if add is not None:
            acc = acc + refs[2][...].astype(F32)
        o_ref[...] = acc.astype(out_dtype)

    ins, specs = [a, b], [a_spec, b_spec]
    if add is not None:
        ins.append(add)
        specs.append(o_spec)
    if after is not None:
        ins.append(after)
        specs.append(pl.BlockSpec(memory_space=pl.ANY))
    est = 2 * (tm * k * _isz(a) + tn * k * _isz(b) + tm * tn * (jnp.dtype(out_dtype).itemsize + (4 if add is not None else 0)))
    est += (tm + tn) * k * 2 + 2 * tm * tn * 4
    return pl.pallas_call(
        body, name=name, grid=(m // tm, n // tn), in_specs=specs, out_specs=o_spec,
        out_shape=_hbm((N_CHIPS, m, tn) if col_shards else (m, n), out_dtype),
        compiler_params=_params(est),
    )(*_pin(*ins))


def _ffn_up(h, wg, wu, *, name):
    s, d = h.shape
    _, _, c = wg.shape
    tm = _pick(s, 512, SUBLANE)

    def body(h_ref, wg_ref, wu_ref, g_ref, u_ref, a_ref):
        hv = h_ref[...]
        gv = jnp.dot(hv, wg_ref[...], preferred_element_type=F32)
        uv = jnp.dot(hv, wu_ref[...], preferred_element_type=F32)
        g_ref[...] = gv.astype(BF16)
        u_ref[...] = uv.astype(BF16)
        a_ref[...] = (gv / (1.0 + jnp.exp(-gv)) * uv).astype(BF16)

    w_spec = pl.BlockSpec((None, d, c), lambda j, i: (j, 0, 0))
    o_spec = pl.BlockSpec((tm, c), lambda j, i: (i, j))
    est = 2 * (tm * d * 2 + 2 * d * c * 2 + tm * c * 10) + 4 * tm * c * 4
    return pl.pallas_call(
        body, name=name, grid=(N_CHIPS, s // tm), in_specs=[pl.BlockSpec((tm, d), lambda j, i: (i, 0)), w_spec, w_spec],
        out_specs=[o_spec, o_spec, o_spec],
        out_shape=[_hbm((s, N_CHIPS * c), BF16)] * 3,
        compiler_params=_params(est),
    )(*_pin(h, wg, wu))


def _ffn_down_dx(dx, w_down, gate, up, *, name):
    s, d = dx.shape
    f = w_down.shape[0]
    tm, tn = _pick(s, 1024, SUBLANE), _pick(f, 512)

    def body(dx_ref, w_ref, g_ref, u_ref, dg_ref, du_ref):
        dact = lax.dot_general(dx_ref[...], w_ref[...], _DIMS["nt"], preferred_element_type=F32)
        gv, uv = g_ref[...].astype(F32), u_ref[...].astype(F32)
        sig = 1.0 / (1.0 + jnp.exp(-gv))
        dg_ref[...] = (dact * uv * (sig * (1.0 + gv * (1.0 - sig)))).astype(BF16)
        du_ref[...] = (dact * (gv * sig)).astype(BF16)

    t_spec = pl.BlockSpec((tm, tn), lambda i, j: (i, j))
    est = 2 * (tm * d * 2 + tn * d * 2 + tm * tn * 12) + 6 * tm * tn * 4
    return pl.pallas_call(
        body, name=name, grid=(s // tm, f // tn),
        in_specs=[pl.BlockSpec((tm, d), lambda i, j: (i, 0)), pl.BlockSpec((tn, d), lambda i, j: (j, 0)), t_spec, t_spec],
        out_specs=[t_spec, t_spec], out_shape=[_hbm((s, f), BF16)] * 2, compiler_params=_params(est),
    )(*_pin(dx, w_down, gate, up))


def _ffn_up_dx(dgate, dup, wg, wu, *, name):
    s, f = dgate.shape
    _, d, c = wg.shape
    tm, tn = _pick(s, 1024, SUBLANE), _pick(d, 1024)
    nk = 2 * N_CHIPS

    def body(dg_ref, du_ref, wg_ref, wu_ref, o_ref, acc):
        kk = pl.program_id(2)

        @pl.when(kk == 0)
        def _():
            acc[...] = jnp.zeros_like(acc)

        @pl.when(kk < N_CHIPS)
        def _():
            acc[...] += lax.dot_general(dg_ref[...], wg_ref[...], _DIMS["nt"], preferred_element_type=F32)

        @pl.when(kk >= N_CHIPS)
        def _():
            acc[...] += lax.dot_general(du_ref[...], wu_ref[...], _DIMS["nt"], preferred_element_type=F32)

        @pl.when(kk == nk - 1)
        def _():
            o_ref[...] = acc[...]

    last = N_CHIPS - 1
    est = 2 * (2 * tm * c * 2 + 2 * tn * c * 2 + tm * tn * 4) + 2 * tm * tn * 4
    return pl.pallas_call(
        body, name=name, grid=(s // tm, d // tn, nk),
        in_specs=[pl.BlockSpec((tm, c), lambda i, j, kk: (i, jnp.minimum(kk, last))),
                  pl.BlockSpec((tm, c), lambda i, j, kk: (i, jnp.maximum(kk - N_CHIPS, 0))),
                  pl.BlockSpec((None, tn, c), lambda i, j, kk: (jnp.minimum(kk, last), j, 0)),
                  pl.BlockSpec((None, tn, c), lambda i, j, kk: (jnp.maximum(kk - N_CHIPS, 0), j, 0))],
        out_specs=pl.BlockSpec((tm, tn), lambda i, j, kk: (i, j)),
        out_shape=_hbm((s, d), F32), scratch_shapes=[pltpu.VMEM((tm, tn), F32)],
        compiler_params=_params(est),
    )(*_pin(dgate, dup, wg, wu))


def _norm_fwd(x, gain, *, wb, cb, nb, shared_gain, out_dtype, name):
    s = x.shape[0]
    ts = _rows_for(wb * 4, s)

    def body(x_ref, g_ref, o_ref):
        xv = x_ref[...].astype(F32)
        r = lax.rsqrt(jnp.mean(xv * xv, axis=1, keepdims=True) + EPS)
        o_ref[...] = ((xv * r) * g_ref[...]).astype(out_dtype)

    return pl.pallas_call(
        body, name=name, grid=(nb, s // ts),
        in_specs=[pl.BlockSpec((ts, wb), lambda n, i: (i, cb + n)),
                  pl.BlockSpec((1, wb), (lambda n, i: (0, 0)) if shared_gain else (lambda n, i: (0, n)))],
        out_specs=pl.BlockSpec((ts, wb), lambda n, i: (i, n)),
        out_shape=_hbm((s, nb * wb), out_dtype), compiler_params=_params(6 * ts * wb * 4),
    )(*_pin(x), gain)


def _norm_bwd(x, gain, dy, *, wb, cb, nb, shared_gain, out_dtype, name, dy_cb=0, add=None, bf16_copy=False):
    s = x.shape[0]
    ts = _rows_for(wb * 4, s, target=1024 * 1024)
    gw = wb if shared_gain else nb * wb

    def body(*refs):
        refs = list(refs)
        dxb_ref = refs.pop() if bf16_copy else None
        if add is None:
            x_ref, g_ref, dy_ref, dx_ref, dg_ref = refs
        else:
            x_ref, g_ref, dy_ref, add_ref, dx_ref, dg_ref = refs
        n, i = pl.program_id(0), pl.program_id(1)
        xv = x_ref[...].astype(F32)
        dyv = dy_ref[...].astype(F32)
        r = lax.rsqrt(jnp.mean(xv * xv, axis=1, keepdims=True) + EPS)
        xh = xv * r
        dyg = dyv * g_ref[...]
        dx = r * (dyg - xh * jnp.mean(dyg * xh, axis=1, keepdims=True))
        if add is not None:
            dx = dx + add_ref[...]
        dx_ref[...] = dx.astype(out_dtype)
        if bf16_copy:
            dxb_ref[...] = dx.astype(BF16)
        first = jnp.logical_and(n == 0, i == 0) if shared_gain else (i == 0)

        @pl.when(first)
        def _():
            dg_ref[...] = jnp.zeros_like(dg_ref)

        dg_ref[...] += jnp.sum(dyv * xh, axis=0, keepdims=True)

    ins = [x, gain, dy]
    specs = [pl.BlockSpec((ts, wb), lambda n, i: (i, cb + n)),
             pl.BlockSpec((1, wb), (lambda n, i: (0, 0)) if shared_gain else (lambda n, i: (0, n))),
             pl.BlockSpec((ts, wb), lambda n, i: (i, dy_cb + n))]
    if add is not None:
        ins.append(add)
        specs.append(pl.BlockSpec((ts, wb), lambda n, i: (i, n)))
    out_specs = [pl.BlockSpec((ts, wb), lambda n, i: (i, n)),
                 pl.BlockSpec((1, wb), (lambda n, i: (0, 0)) if shared_gain else (lambda n, i: (0, n)))]
    out_shape = [_hbm((s, nb * wb), out_dtype), jax.ShapeDtypeStruct((1, gw), F32)]
    if bf16_copy:
        out_specs.append(out_specs[0])
        out_shape.append(_hbm((s, nb * wb), BF16))
    return pl.pallas_call(
        body, name=name, grid=(nb, s // ts), in_specs=specs, out_specs=out_specs, out_shape=out_shape,
        compiler_params=_params(14 * ts * wb * 4),
    )(*_pin(*ins))


def _swap_halves(x, half):
    if 2 * half == LANE:
        return pltpu.roll(x, half, axis=1)
    lane = lax.broadcasted_iota(jnp.int32, x.shape, 1)
    first = jnp.bitwise_and(lane, 2 * half - 1) < half
    return jnp.where(first, pltpu.roll(x, LANE - half, axis=1), pltpu.roll(x, half, axis=1))


def _rope(x, cos_t, sin_t, *, tw, cb, nb, half, sign, out_dtype, name):
    s = x.shape[0]
    ts = _rows_for(tw * 4, s)

    def body(x_ref, c_ref, s_ref, o_ref):
        for q in range(tw // LANE):
            sl = slice(q * LANE, (q + 1) * LANE)
            xv = x_ref[:, sl].astype(F32)
            sv = s_ref[:, sl]
            if sign < 0:
                sv = -sv
            o_ref[:, sl] = (xv * c_ref[:, sl] + _swap_halves(xv, half) * sv).astype(out_dtype)

    return pl.pallas_call(
        body, name=name, grid=(nb, s // ts),
        in_specs=[pl.BlockSpec((ts, tw), lambda n, i: (i, cb + n)),
                  pl.BlockSpec((ts, tw), lambda n, i: (i, 0)),
                  pl.BlockSpec((ts, tw), lambda n, i: (i, 0))],
        out_specs=pl.BlockSpec((ts, tw), lambda n, i: (i, n)),
        out_shape=_hbm((s, nb * tw), out_dtype), compiler_params=_params(10 * ts * tw * 4),
    )(*_pin(x), cos_t, sin_t)


def _latent_keys(kv, p, cos_t, sin_t, *, kr_cb, name):
    s = kv.shape[0]
    ts = _rows_for(A_QK * 4, s)

    def body(kv_ref, kr_ref, c_ref, s_ref, o_ref):
        o_ref[:, :LANE] = kv_ref[...].astype(BF16)
        x = kr_ref[...].astype(F32)
        o_ref[:, LANE:] = (x * c_ref[:, LANE:] + _swap_halves(x, A_ROPE // 2) * s_ref[:, LANE:]).astype(BF16)

    tab = pl.BlockSpec((ts, A_QK), lambda n, i: (i, 0))
    return pl.pallas_call(
        body, name=name, grid=(A_HEADS, s // ts),
        in_specs=[pl.BlockSpec((ts, LANE), lambda n, i: (i, 2 * n)), pl.BlockSpec((ts, LANE), lambda n, i: (i, kr_cb)),
                  tab, tab],
        out_specs=pl.BlockSpec((ts, A_QK), lambda n, i: (i, n)),
        out_shape=_hbm((s, A_HEADS * A_QK), BF16), compiler_params=_params(10 * ts * A_QK * 4),
    )(*_pin(kv, p), cos_t, sin_t)


def _latent_keys_bwd(dka, dva, cos_t, sin_t, *, name):
    s = dka.shape[0]
    ts = _rows_for(A_HEADS * A_QK * 4, s)

    def body(dka_ref, dva_ref, c_ref, s_ref, dkv_ref, dkr_ref):
        acc = jnp.zeros((ts, LANE), F32)
        for h in range(A_HEADS):
            dkv_ref[:, h * A_QK:h * A_QK + LANE] = dka_ref[:, h * A_QK:h * A_QK + LANE].astype(BF16)
            dkv_ref[:, h * A_QK + LANE:(h + 1) * A_QK] = dva_ref[:, h * LANE:(h + 1) * LANE].astype(BF16)
            y = dka_ref[:, h * A_QK + LANE:(h + 1) * A_QK]
            acc = acc + (y * c_ref[:, LANE:] - _swap_halves(y, A_ROPE // 2) * s_ref[:, LANE:])
        dkr_ref[...] = acc.astype(BF16)

    def rows(width):
        return pl.BlockSpec((ts, width), lambda i: (i, 0))

    return pl.pallas_call(
        body, name=name, grid=(s // ts,),
        in_specs=[rows(A_HEADS * A_QK), rows(A_HEADS * LANE), rows(A_QK), rows(A_QK)],
        out_specs=[rows(A_HEADS * A_QK), rows(LANE)],
        out_shape=[_hbm((s, A_HEADS * A_QK), BF16), _hbm((s, LANE), BF16)],
        compiler_params=_params(8 * ts * A_HEADS * A_QK * 4),
    )(*_pin(dka, dva), cos_t, sin_t)


def _cast_cols(x, *, cb, nb, name):
    s = x.shape[0]
    ts = _rows_for(LANE * 4, s)

    def body(x_ref, o_ref):
        o_ref[...] = x_ref[...].astype(BF16)

    return pl.pallas_call(
        body, name=name, grid=(nb, s // ts), in_specs=[pl.BlockSpec((ts, LANE), lambda n, i: (i, cb + n))],
        out_specs=pl.BlockSpec((ts, LANE), lambda n, i: (i, n)),
        out_shape=_hbm((s, nb * LANE), BF16), compiler_params=_params(4 * ts * LANE * 4),
    )(*_pin(x))


LOG2E = 1.4426950408889634
ATTN_ROW_CHUNK = 256


def _attn_window(i, rc, s, band):
    w, r = band
    start = jnp.clip(i * rc - r, 0, s - w)
    return pl.multiple_of(start, rc), pl.multiple_of((w - rc) - (i * rc - start), LANE)


def _flash_fwd(q, k, v, table, *, hkv, g, dqk, q_cb, k_cb, v_cb, v_step, scale, tq, band, name):
    s = q.shape[0]
    n = s // tq
    hq = hkv * g
    rc = min(tq, ATTN_ROW_CHUNK)
    w = s if band is None else band[0]

    def body(*refs):
        if band is None:
            q_ref, k_ref, v_ref, o_ref, lse_ref = refs
            kw, vw = k_ref[...], v_ref[...]
        else:
            q_ref, k_ref, v_ref, t_ref, o_ref, lse_ref = refs
        for c in range(tq // rc):
            rows = slice(c * rc, (c + 1) * rc)
            if band is not None:
                start, u = _attn_window(pl.program_id(1) * (tq // rc) + c, rc, s, band)
                kw, vw = k_ref[pl.ds(start, w), :], v_ref[pl.ds(start, w), :]
            sc = lax.dot_general(q_ref[rows, :], kw, _DIMS["nt"], preferred_element_type=F32) * (scale * LOG2E)
            if band is not None:
                sc = sc + t_ref[:, pl.ds(u, w)]
            m = jnp.max(sc, axis=1, keepdims=True)
            p = jnp.exp2(sc - m)
            l = jnp.sum(p, axis=1, keepdims=True)
            o_ref[rows, :] = jnp.dot(p.astype(BF16), vw, preferred_element_type=F32) / l
            lse_ref[0, rows, :] = jnp.broadcast_to(m + jnp.log2(l), (rc, LANE))

    ins = [q, k, v]
    specs = [pl.BlockSpec((tq, dqk), lambda h, i: (i, q_cb + h)),
             pl.BlockSpec((s, dqk), lambda h, i: (0, k_cb + h // g)),
             pl.BlockSpec((s, LANE), lambda h, i: (0, v_cb + v_step * (h // g)))]
    if band is not None:
        ins.append(table)
        specs.append(pl.BlockSpec(table.shape, lambda h, i: (0, 0)))
    est = 4 * s * (dqk + LANE) + 6 * rc * w * 4 + 8 * tq * LANE * 4 + (0 if band is None else 2 * table.size * 4)
    return pl.pallas_call(
        body, name=name, grid=(hq, n), in_specs=specs,
        out_specs=[pl.BlockSpec((tq, LANE), lambda h, i: (i, h)), pl.BlockSpec((1, tq, LANE), lambda h, i: (h, i, 0))],
        out_shape=[_hbm((s, hq * LANE), F32), _hbm((hq, s, LANE), F32)],
        compiler_params=_params(est),
    )(*_pin(*ins))


def _flash_bwd(q, k, v, o, do, lse, table, *, hkv, g, dqk, q_cb, k_cb, v_cb, v_step, scale, tq, band, name):
    s = q.shape[0]
    n = s // tq
    hq = hkv * g
    rc = min(tq, ATTN_ROW_CHUNK)
    w = s if band is None else band[0]

    def body(*refs):
        if band is None:
            q_ref, k_ref, v_ref, o_ref, do_ref, lse_ref, dq_ref, dk_ref, dv_ref = refs
            keys = slice(None)
            kw, vw = k_ref[...], v_ref[...]
        else:
            q_ref, k_ref, v_ref, o_ref, do_ref, lse_ref, t_ref, dq_ref, dk_ref, dv_ref = refs
        h, i = pl.program_id(0), pl.program_id(1)

        @pl.when(jnp.logical_and(h % g == 0, i == 0))
        def _():
            dk_ref[...] = jnp.zeros_like(dk_ref)
            dv_ref[...] = jnp.zeros_like(dv_ref)

        for c in range(tq // rc):
            rows = slice(c * rc, (c + 1) * rc)
            if band is not None:
                start, u = _attn_window(i * (tq // rc) + c, rc, s, band)
                keys = pl.ds(start, w)
                kw, vw = k_ref[keys, :], v_ref[keys, :]
            qv = q_ref[rows, :]
            dof = do_ref[rows, :]
            dov = dof.astype(BF16)
            sc = lax.dot_general(qv, kw, _DIMS["nt"], preferred_element_type=F32) * (scale * LOG2E)
            if band is not None:
                sc = sc + t_ref[:, pl.ds(u, w)]
            p = jnp.exp2(sc - lse_ref[0, rows, 0:1])
            dp = lax.dot_general(dov, vw, _DIMS["nt"], preferred_element_type=F32)
            delta = jnp.sum(dof * o_ref[rows, :], axis=1, keepdims=True)
            ds = (p * (dp - delta) * scale).astype(BF16)
            dv_ref[keys, :] += lax.dot_general(p.astype(BF16), dov, _DIMS["tn"], preferred_element_type=F32)
            dk_ref[keys, :] += lax.dot_general(ds, qv, _DIMS["tn"], preferred_element_type=F32)
            dq_ref[rows, :] = jnp.dot(ds, kw, preferred_element_type=F32)

    ins = [q, k, v, o, do, lse]
    specs = [pl.BlockSpec((tq, dqk), lambda h, i: (i, q_cb + h)),
             pl.BlockSpec((s, dqk), lambda h, i: (0, k_cb + h // g)),
             pl.BlockSpec((s, LANE), lambda h, i: (0, v_cb + v_step * (h // g))),
             pl.BlockSpec((tq, LANE), lambda h, i: (i, h)),
             pl.BlockSpec((tq, LANE), lambda h, i: (i, h)),
             pl.BlockSpec((1, tq, LANE), lambda h, i: (h, i, 0))]
    if band is not None:
        ins.append(table)
        specs.append(pl.BlockSpec(table.shape, lambda h, i: (0, 0)))
    est = (4 + 8) * s * (dqk + LANE) + 10 * rc * w * 4 + 12 * tq * LANE * 4 + (0 if band is None else 2 * table.size * 4)
    return pl.pallas_call(
        body, name=name, grid=(hq, n), in_specs=specs,
        out_specs=[pl.BlockSpec((tq, dqk), lambda h, i: (i, h)),
                   pl.BlockSpec((s, dqk), lambda h, i: (0, h // g)),
                   pl.BlockSpec((s, LANE), lambda h, i: (0, h // g))],
        out_shape=[_hbm((s, hq * dqk), F32), _hbm((s, hkv * dqk), F32),
                   _hbm((s, hkv * LANE), F32)],
        compiler_params=_params(est),
    )(*_pin(*ins))


def _final_loss(x, gain, target, *, name):
    s, d = x.shape
    ts = _rows_for(d * 4, s, target=1024 * 1024)

    def body(x_ref, g_ref, t_ref, dx_ref, dg_ref, loss_ref, dxb_ref):
        i = pl.program_id(0)
        xv = x_ref[...]
        gv = g_ref[...]
        r = lax.rsqrt(jnp.mean(xv * xv, axis=1, keepdims=True) + EPS)
        xh = xv * r
        err = xh * gv - t_ref[...]
        dy = err / d
        dyg = dy * gv
        dx = r * (dyg - xh * jnp.mean(dyg * xh, axis=1, keepdims=True))
        dx_ref[...] = dx
        dxb_ref[...] = dx.astype(BF16)

        @pl.when(i == 0)
        def _():
            dg_ref[...] = jnp.zeros_like(dg_ref)
            loss_ref[...] = jnp.zeros_like(loss_ref)

        dg_ref[...] += jnp.sum(dy * xh, axis=0, keepdims=True)
        part = jnp.sum(jnp.mean(err * err, axis=1, keepdims=True), axis=0, keepdims=True)
        loss_ref[...] += jnp.broadcast_to(0.5 * part, (1, LANE))

    row = pl.BlockSpec((ts, d), lambda i: (i, 0))
    return pl.pallas_call(
        body, name=name, grid=(s // ts,),
        in_specs=[row, pl.BlockSpec((1, d), lambda i: (0, 0)), row],
        out_specs=[row, pl.BlockSpec((1, d), lambda i: (0, 0)), pl.BlockSpec((1, LANE), lambda i: (0, 0)), row],
        out_shape=[_hbm((s, d), F32), jax.ShapeDtypeStruct((1, d), F32),
                   jax.ShapeDtypeStruct((1, LANE), F32), _hbm((s, d), BF16)],
        compiler_params=_params(14 * ts * d * 4),
    )(*_pin(x), gain, *_pin(target))


def _cast_to_slot(x3d, me, *, layer, name):
    _, rows, c = x3d.shape
    tr, tc = _tile2(rows, c, 2 * 1024 * 1024)

    def body(me_ref, x_ref, o_ref):
        o_ref[...] = x_ref[...].astype(BF16)

    return pl.pallas_call(
        body, name=name,
        grid_spec=pltpu.PrefetchScalarGridSpec(
            num_scalar_prefetch=1, grid=(rows // tr, c // tc),
            in_specs=[pl.BlockSpec((None, tr, tc), lambda i, j, me_ref: (layer, i, j))],
            out_specs=pl.BlockSpec((None, tr, tc), lambda i, j, me_ref: (me_ref[0], i, j))),
        out_shape=_hbm((N_CHIPS, rows, c), BF16), compiler_params=_params(6 * tr * tc * 4),
    )(me, *_pin(x3d))


def _sum_parts(srcs, lands, me, *, name):
    depth = len(srcs)
    _, r, c = srcs[0].shape
    tr, tc = _tile2(r, c, 1024 * 1024)
    nt, nc = r // tr, c // tc

    def body(me_ref, *refs):
        o_ref = refs[-1]
        l = pl.program_id(0)
        for k in range(depth):
            @pl.when(l == k)
            def _(k=k):
                acc = refs[k][...].astype(F32)
                for p in range(3):
                    acc = acc + refs[depth + k][p].astype(F32)
                o_ref[...] = acc

    def tile_of(k):
        def f(l, i, j):
            return (jnp.where(l == k, i, jnp.where(l < k, 0, nt - 1)), jnp.where(l == k, j, jnp.where(l < k, 0, nc - 1)))
        return f

    in_specs = [pl.BlockSpec((None, tr, tc), functools.partial(lambda l, i, j, me_ref, f: (me_ref[0], *f(l, i, j)), f=tile_of(k)))
                for k in range(depth)]
    in_specs += [pl.BlockSpec((3, tr, tc), functools.partial(lambda l, i, j, me_ref, f: (0, *f(l, i, j)), f=tile_of(k)))
                 for k in range(depth)]
    return pl.pallas_call(
        body, name=name,
        grid_spec=pltpu.PrefetchScalarGridSpec(
            num_scalar_prefetch=1, grid=(depth, nt, nc), in_specs=in_specs,
            out_specs=pl.BlockSpec((tr, tc), lambda l, i, j, me_ref: (l * nt + i, j))),
        out_shape=_hbm((depth * r, c), F32), compiler_params=_params(depth * 10 * tr * tc * 4),
    )(me, *_pin(*srcs, *lands))


def _adamw_math(w, g, m, v):
    m2 = ADAM_B1 * m + (1.0 - ADAM_B1) * g
    v2 = ADAM_B2 * v + (1.0 - ADAM_B2) * (g * g)
    m_hat = m2 / (1.0 - ADAM_B1 ** ADAM_STEP)
    v_hat = v2 / (1.0 - ADAM_B2 ** ADAM_STEP)
    delta = -ADAM_LR * (m_hat / (jnp.sqrt(v_hat) + ADAM_EPS) + ADAM_WD * w)
    return delta, m2, v2


def _adamw(g_a, g_b, w, m, v, *, name):
    depth, r, c = w.shape
    tr, tc = _tile2(r, c, 512 * 1024)
    nt = r // tr

    def body(a_ref, b_ref, w_ref, m_ref, v_ref, g_out, d_out, m_out, v_out):
        gv = a_ref[...] + b_ref[...]
        delta, m2, v2 = _adamw_math(w_ref[...], gv, m_ref[...], v_ref[...])
        g_out[...] = gv
        d_out[...] = delta
        m_out[...] = m2
        v_out[...] = v2

    flat = pl.BlockSpec((tr, tc), lambda l, i, j: (l * nt + i, j))
    spec = pl.BlockSpec((None, tr, tc), lambda l, i, j: (l, i, j))
    return pl.pallas_call(
        body, name=name, grid=(depth, nt, c // tc), in_specs=[flat, flat, spec, spec, spec], out_specs=[spec] * 4,
        out_shape=[_hbm((depth, r, c), F32)] * 4, compiler_params=_params(22 * tr * tc * 4),
    )(*_pin(g_a, g_b, w, m, v))


def _small_adamw(g_all, w, m, v, *, name):
    r, c = w.shape

    def body(ga_ref, w_ref, m_ref, v_ref, g_out, d_out, m_out, v_out):
        gv = ga_ref[0]
        for j in range(1, N_DEV):
            gv = gv + ga_ref[j]
        delta, m2, v2 = _adamw_math(w_ref[...], gv, m_ref[...], v_ref[...])
        g_out[...] = gv
        d_out[...] = delta
        m_out[...] = m2
        v_out[...] = v2

    return pl.pallas_call(body, name=name, out_shape=[jax.ShapeDtypeStruct((r, c), F32)] * 4)(g_all, w, m, v)


_ANY = pl.BlockSpec(memory_space=pl.ANY)


_HBM = pl.BlockSpec(memory_space=pltpu.HBM)
_SEM = pl.BlockSpec(memory_space=pltpu.SEMAPHORE)
_EFFECT = pltpu.SideEffectType.DATAFLOW_SIDE_EFFECTING


def _peer_chips():
    x, y = lax.axis_index("x"), lax.axis_index("y")
    return 2 * x + y, [(1 - x, y), (x, 1 - y), (1 - x, 1 - y)]


def _exchange_copy(srcs, lands, send_sems, recv_sems, k, p, kind):
    c = lax.axis_index("c")
    if kind == "swap":
        return pltpu.make_async_remote_copy(
            src_ref=srcs[k], dst_ref=lands[k], send_sem=send_sems.at[k], recv_sem=recv_sems.at[k],
            device_id=(lax.axis_index("x"), lax.axis_index("y"), 1 - c), device_id_type=MESH)
    me, peers = _peer_chips()
    px, py = peers[p]
    return pltpu.make_async_remote_copy(
        src_ref=srcs[k].at[2 * px + py] if kind == "scatter" else srcs[k].at[me],
        dst_ref=lands[k].at[p] if kind == "scatter" else lands[k].at[me],
        send_sem=send_sems.at[3 * k + p], recv_sem=recv_sems.at[3 * k + p],
        device_id=(px, py, c), device_id_type=MESH)


def _exchange_start(srcs, lands, after, *, kind, name):
    n = len(srcs)
    npeer = 1 if kind == "swap" else 3
    bufs = list(srcs) + (list(lands) if lands is not None else [])
    nb = len(bufs)

    def body(*refs):
        buf_refs, send_sems, recv_sems = refs[:nb], refs[nb + 1], refs[nb + 2]
        token = refs[-1]
        s_refs = buf_refs[:n]
        l_refs = buf_refs[n:] if lands is not None else s_refs
        for k in range(n):
            for p in range(npeer):
                _exchange_copy(s_refs, l_refs, send_sems, recv_sems, k, p, kind).start()
        token[...] = jnp.zeros_like(token)

    out = pl.pallas_call(
        body, name=name,
        out_shape=(pltpu.SemaphoreType.DMA((npeer * n,)), pltpu.SemaphoreType.DMA((npeer * n,)),
                   *[pltpu.HBM(b.shape, b.dtype) for b in bufs], jax.ShapeDtypeStruct((SUBLANE, LANE), F32)),
        in_specs=[_HBM] * nb + [_ANY],
        out_specs=(_SEM, _SEM, *[_HBM] * nb, pl.BlockSpec(memory_space=pltpu.VMEM)),
        input_output_aliases={i: 2 + i for i in range(nb)},
        compiler_params=pltpu.CompilerParams(has_side_effects=_EFFECT),
    )(*[pltpu.with_memory_space_constraint(b, pltpu.HBM) for b in bufs], after)
    send_sems, recv_sems = out[0], out[1]
    thru = out[2:2 + nb]
    return send_sems, recv_sems, list(thru[:n]), (list(thru[n:]) if lands is not None else None), out[-1]


def _exchange_wait(send_sems, recv_sems, srcs, lands, after, *, kind, name):
    n = len(srcs)
    npeer = 1 if kind == "swap" else 3
    bufs = list(srcs) + (list(lands) if lands is not None else [])
    nb = len(bufs)

    def body(*refs):
        buf_refs, send_sems_ref, recv_sems_ref = refs[:nb], refs[nb], refs[nb + 1]
        s_refs = buf_refs[:n]
        l_refs = buf_refs[n:] if lands is not None else s_refs
        for k in range(n):
            for p in range(npeer):
                cp = _exchange_copy(s_refs, l_refs, send_sems_ref, recv_sems_ref, k, p, kind)
                cp.wait_send()
                cp.wait_recv()

    out = pl.pallas_call(
        body, name=name, out_shape=tuple(pltpu.HBM(b.shape, b.dtype) for b in bufs),
        in_specs=[_HBM] * nb + [_SEM, _SEM, _ANY], out_specs=tuple([_HBM] * nb),
        input_output_aliases={i: i for i in range(nb)},
        compiler_params=pltpu.CompilerParams(has_side_effects=_EFFECT),
    )(*bufs, send_sems, recv_sems, after)
    return list(out)


def _sibling_exchange(srcs, *, name):
    n = len(srcs)

    def body(*refs):
        src, out = refs[:n], refs[n:2 * n]
        send_sems, recv_sems = refs[2 * n:]
        sibling = (lax.axis_index("x"), lax.axis_index("y"), 1 - lax.axis_index("c"))
        copies = [pltpu.make_async_remote_copy(src_ref=src[k], dst_ref=out[k], send_sem=send_sems.at[k],
                                               recv_sem=recv_sems.at[k], device_id=sibling, device_id_type=MESH)
                  for k in range(n)]
        for cp in copies:
            cp.start()
        for cp in copies:
            cp.wait_recv()
        for cp in copies:
            cp.wait_send()

    return pl.pallas_call(
        body, name=name, in_specs=[_ANY] * n, out_specs=[_ANY] * n,
        out_shape=[jax.ShapeDtypeStruct(a.shape, a.dtype) for a in srcs],
        scratch_shapes=[pltpu.SemaphoreType.DMA((n,)), pltpu.SemaphoreType.DMA((n,))],
    )(*srcs)


def _all_gather_small(block, *, name):
    m_per, ncol = block.shape

    def body(x_ref, out_ref, send_sems, recv_sems, local_sem):
        x, y, c = lax.axis_index("x"), lax.axis_index("y"), lax.axis_index("c")
        me, sibling = (x, y, c), (x, y, 1 - c)
        chips = [(1 - x, y), (x, 1 - y), (1 - x, 1 - y)]

        def rows(px, py, pc):
            return out_ref.at[pl.ds((4 * px + 2 * py + pc) * m_per, m_per), :]

        def copy(k, blk, to, src=None):
            return pltpu.make_async_remote_copy(
                src_ref=rows(*blk) if src is None else src, dst_ref=rows(*blk),
                send_sem=send_sems.at[k], recv_sem=recv_sems.at[k], device_id=to, device_id_type=MESH)

        mine = pltpu.make_async_copy(x_ref, rows(*me), local_sem)
        mine.start()
        first = [copy(0, me, sibling, src=x_ref)]
        first += [copy(1 + j, me, (*chip, c), src=x_ref) for j, chip in enumerate(chips)]
        for cp in first:
            cp.start()
        passed = [copy(4 + j, (*chip, c), sibling) for j, chip in enumerate(chips)]
        for j, chip in enumerate(chips):
            copy(1 + j, (*chip, c), me).wait_recv()
            passed[j].start()
        copy(0, sibling, me).wait_recv()
        for j, chip in enumerate(chips):
            copy(4 + j, (*chip, 1 - c), me).wait_recv()
        for cp in first + passed:
            cp.wait_send()
        mine.wait()

    return pl.pallas_call(
        body, name=name, out_shape=jax.ShapeDtypeStruct((N_DEV * m_per, ncol), block.dtype),
        in_specs=[pl.BlockSpec(memory_space=pltpu.VMEM)], out_specs=pl.BlockSpec(memory_space=pltpu.VMEM),
        scratch_shapes=[pltpu.SemaphoreType.DMA((7,)), pltpu.SemaphoreType.DMA((7,)), pltpu.SemaphoreType.DMA],
    )(block)


def _rope_angles(pos, dim):
    inv = ROPE_THETA ** (-jnp.arange(0, dim, 2, dtype=F32) / dim)
    return pos.astype(F32)[:, None] * inv[None, :]


def _rope_tables(s):
    pos = jnp.arange(s, dtype=jnp.int32)
    rows = s // GRID_W
    row = jnp.repeat(jnp.arange(rows, dtype=jnp.int32), GRID_W)
    col = jnp.tile(jnp.arange(GRID_W, dtype=jnp.int32), rows)
    a1 = _rope_angles(pos, HEAD_DIM)
    aa = _rope_angles(pos, A_ROPE)
    ar = _rope_angles(row, HEAD_DIM // 2)
    ac = _rope_angles(col, HEAD_DIM // 2)
    one = jnp.ones((s, LANE), F32)
    zero = jnp.zeros((s, LANE), F32)
    pad = LANE - A_ROPE
    cos_a = jnp.concatenate([one, jnp.cos(aa), jnp.cos(aa), jnp.ones((s, pad), F32)], axis=1)
    sin_a = jnp.concatenate([zero, -jnp.sin(aa), jnp.sin(aa), jnp.zeros((s, pad), F32)], axis=1)
    cos_b = jnp.concatenate([jnp.cos(a1), jnp.cos(a1)], axis=1)
    sin_b = jnp.concatenate([-jnp.sin(a1), jnp.sin(a1)], axis=1)
    cos_c = jnp.concatenate([jnp.cos(ar), jnp.cos(ar), jnp.cos(ac), jnp.cos(ac)], axis=1)
    sin_c = jnp.concatenate([-jnp.sin(ar), jnp.sin(ar), -jnp.sin(ac), jnp.sin(ac)], axis=1)
    return (cos_a, sin_a), (cos_b, sin_b), (cos_c, sin_c)


def _band_table(rc, s):
    reach = max((win // (2 * d)) * d for win, d in B_PATTERNS)
    r = -(-reach // rc) * rc
    w = min(s, rc + 2 * r)
    j = jnp.arange(rc, dtype=jnp.int32)[:, None]
    x = jnp.arange(2 * w - rc, dtype=jnp.int32)[None, :]
    rel = x - (w - rc) - j
    mult = jnp.zeros(rel.shape, F32)
    for win, d in B_PATTERNS:
        mult = mult + jnp.logical_and(rel % d == 0, jnp.abs(rel) <= (win // (2 * d)) * d).astype(F32)
    return jnp.where(mult > 0, jnp.log2(jnp.maximum(mult, 1.0)), NEG), (w, r)


_BIG = ("w_in", "a_w_uq", "a_w_ukv", "w_out", "w_gate", "w_up", "w_down")
_SMALL = ("attn_norm", "a_q_norm", "a_kv_norm", "c_q_norm", "c_k_norm", "out_norm", "ffn_norm", "final_norm")
_WEIGHTS = ("attn_norm", "w_in", "a_q_norm", "a_w_uq", "a_kv_norm", "a_w_ukv", "c_q_norm", "c_k_norm", "out_norm",
            "w_out", "ffn_norm", "w_gate", "w_up", "w_down", "final_norm")


_ATTN = ("w_in", "a_w_uq", "a_w_ukv")
_FFN = ("w_out", "w_gate", "w_up", "w_down")


def _from_cols(a):
    return jnp.transpose(a, (1, 0, 2)).reshape(a.shape[1], N_CHIPS * a.shape[2])


def _from_rows(a):
    return a.reshape(N_CHIPS * a.shape[1], a.shape[2])


def _to_cols(a):
    return jnp.transpose(a.reshape(a.shape[0], N_CHIPS, a.shape[1] // N_CHIPS), (1, 0, 2))


def _to_rows(a):
    return a.reshape(N_CHIPS, a.shape[0] // N_CHIPS, a.shape[1])


def _assemble_attn(gw):
    w_in_t, uq, ukv = _from_rows(gw[0]), _from_cols(gw[1]), _from_cols(gw[2])
    d = w_in_t.shape[1]
    w_all = jnp.concatenate([w_in_t[:IN_A], jnp.zeros((A_PAD - IN_A, d), BF16), w_in_t[IN_A:]], axis=0)
    uq = uq.reshape(A_Q_RANK, A_HEADS, A_NOPE + A_ROPE)
    uq = jnp.pad(uq, ((0, 0), (0, 0), (0, A_QK - A_NOPE - A_ROPE))).reshape(A_Q_RANK, A_HEADS * A_QK)
    return dict(w_all=w_all, uq=uq, ukv=ukv)


def _assemble_ffn(gw):
    return dict(w_out=_from_rows(gw[0]), w_gate=gw[1], w_up=gw[2], w_down=_from_rows(gw[3]))


def _split_attn_grads(gl):
    w_all = gl["w_all"]
    w_in_t = jnp.concatenate([w_all[:IN_A], w_all[A_PAD:]], axis=0)
    uq = gl["uq"].reshape(A_Q_RANK, A_HEADS, A_QK)[:, :, :A_NOPE + A_ROPE].reshape(A_Q_RANK, A_HEADS * (A_NOPE + A_ROPE))
    return [_to_rows(w_in_t), _to_cols(uq), _to_cols(gl["ukv"])]


def _split_ffn_grads(gl):
    return [_to_rows(gl["w_out"]), gl["w_gate"], gl["w_up"], _to_rows(gl["w_down"])]


def _tie(a, token):
    return a + token[0:1, 0:1]


def _layer_fwd(x, wl, ffn_weights, sm, tabs, bias, t):
    s = x.shape[0]
    (cos_a, sin_a), (cos_b, sin_b), (cos_c, sin_c) = tabs
    h = _norm_fwd(x, sm["attn_norm"], wb=x.shape[1], cb=0, nb=1, shared_gain=True, out_dtype=BF16, name="attn_norm_fwd")
    p = _matmul(h, wl["w_all"], mode="nt", out_dtype=F32, name="in_proj", tm=1024, tn=1280)
    cq_n = _norm_fwd(p, sm["a_q_norm"], wb=A_Q_RANK, cb=0, nb=1, shared_gain=True, out_dtype=BF16, name="a_q_norm_fwd")
    ckv_n = _norm_fwd(p, sm["a_kv_norm"], wb=A_KV_RANK, cb=1, nb=1, shared_gain=True, out_dtype=BF16, name="a_kv_norm_fwd")
    qa_raw = _matmul(cq_n, wl["uq"], mode="nn", out_dtype=F32, name="a_uq", tm=1024, tn=1024)
    kv = _matmul(ckv_n, wl["ukv"], mode="nn", out_dtype=BF16, name="a_ukv", tm=1024, tn=1024)
    qa =_rope(qa_raw, cos_a, sin_a, tw=A_QK, cb=0, nb=A_HEADS, half=A_ROPE // 2, sign=1, out_dtype=BF16, name="a_rope_q")
    ka = _latent_keys(kv, p, cos_a, sin_a, kr_cb=PB_KR, name="a_keys")
    oa, lse_a = _flash_fwd(qa, ka, kv, None, hkv=A_HEADS, g=1, dqk=A_QK, q_cb=0, k_cb=0, v_cb=1, v_step=2,
                           scale=(A_NOPE + A_ROPE) ** -0.5, tq=t, band=None, name="a_flash_fwd")
    table, band = bias
    qb = _rope(p, cos_b, sin_b, tw=LANE, cb=PB_BQ, nb=B_HEADS, half=HEAD_DIM // 2, sign=1, out_dtype=BF16, name="b_rope_q")
    kb = _rope(p, cos_b, sin_b, tw=LANE, cb=PB_BK, nb=B_HEADS, half=HEAD_DIM // 2, sign=1, out_dtype=BF16, name="b_rope_k")
    vb = _cast_cols(p, cb=PB_BV, nb=B_HEADS, name="b_cast_v")
    ob, lse_b = _flash_fwd(qb, kb, vb, table, hkv=B_HEADS, g=1, dqk=LANE, q_cb=0, k_cb=0, v_cb=0, v_step=1,
                           scale=HEAD_DIM ** -0.5, tq=t, band=band, name="b_flash_fwd")
    qn = _norm_fwd(p, sm["c_q_norm"], wb=LANE, cb=PB_CQH, nb=C_HEADS, shared_gain=True, out_dtype=F32, name="c_q_norm_fwd")
    kn = _norm_fwd(p, sm["c_k_norm"], wb=LANE, cb=PB_CKH, nb=C_KV_HEADS, shared_gain=True, out_dtype=F32, name="c_k_norm_fwd")
    qc = _rope(qn, cos_c, sin_c, tw=LANE, cb=0, nb=C_HEADS, half=HEAD_DIM // 4, sign=1, out_dtype=BF16, name="c_rope_q")
    kc = _rope(kn, cos_c, sin_c, tw=LANE, cb=0, nb=C_KV_HEADS, half=HEAD_DIM // 4, sign=1, out_dtype=BF16, name="c_rope_k")
    vc = _cast_cols(p, cb=PB_CVH, nb=C_KV_HEADS, name="c_cast_v")
    oc, lse_c = _flash_fwd(qc, kc, vc, None, hkv=C_KV_HEADS, g=C_GROUP, dqk=LANE, q_cb=0, k_cb=0, v_cb=0, v_step=1,
                           scale=HEAD_DIM ** -0.5, tq=t, band=None, name="c_flash_fwd")
    g_out = sm["out_norm"]
    ga, gb, gc = g_out[:, :A_WIDTH], g_out[:, A_WIDTH:A_WIDTH + B_WIDTH], g_out[:, A_WIDTH + B_WIDTH:]
    ya = _norm_fwd(oa, ga, wb=A_WIDTH, cb=0, nb=1, shared_gain=True, out_dtype=BF16, name="out_norm_a_fwd")
    yb = _norm_fwd(ob, gb, wb=B_WIDTH, cb=0, nb=1, shared_gain=True, out_dtype=BF16, name="out_norm_b_fwd")
    yc = _norm_fwd(oc, gc, wb=C_WIDTH, cb=0, nb=1, shared_gain=True, out_dtype=BF16, name="out_norm_c_fwd")
    y = jnp.concatenate([ya, yb, yc], axis=1)
    wl = {**wl, **ffn_weights(y)}
    x1 = _matmul(y, wl["w_out"], mode="nn", out_dtype=F32, name="out_proj", add=x, tm=1024, tn=1024)
    h2 = _norm_fwd(x1, sm["ffn_norm"], wb=x.shape[1], cb=0, nb=1, shared_gain=True, out_dtype=BF16, name="ffn_norm_fwd")
    gate, up, act = _ffn_up(h2, wl["w_gate"], wl["w_up"], name="ffn_up")
    x2 = _matmul(act, wl["w_down"], mode="nn", out_dtype=F32, name="ffn_down", add=x1, tm=1024, tn=512)
    saved = dict(x=x, h=h, p=p, cq_n=cq_n, ckv_n=ckv_n, kv=kv, qa=qa, ka=ka, oa=oa, lse_a=lse_a, qb=qb, kb=kb, vb=vb, ob=ob,
                 lse_b=lse_b, qc=qc, kc=kc, vc=vc, oc=oc, lse_c=lse_c, y=y, x1=x1, h2=h2, gate=gate, up=up, act=act)
    return x2, saved, wl


def _layer_bwd(dx2, dx2b, sv, wl, sm, tabs, bias, t, send_ffn, send_attn):
    s, d = dx2.shape
    (cos_a, sin_a), (cos_b, sin_b), (cos_c, sin_c) = tabs
    gw, gs = {}, {}
    dgate, dup = _ffn_down_dx(dx2b, wl["w_down"], sv["gate"], sv["up"], name="ffn_down_dx")
    gw["w_down"] = _matmul(sv["act"], dx2b, mode="tn", out_dtype=BF16, name="ffn_down_dw", tm=512, tn=2048)
    dh2 = _ffn_up_dx(dgate, dup, wl["w_gate"], wl["w_up"], name="ffn_up_dx")
    gw["w_gate"] = _matmul(sv["h2"], dgate, mode="tn", out_dtype=BF16, name="ffn_gate_dw", tm=1024, col_shards=True)
    gw["w_up"] = _matmul(sv["h2"], dup, mode="tn", out_dtype=BF16, name="ffn_up_dw", tm=1024, col_shards=True)
    dx1, gs["ffn_norm"], dx1b = _norm_bwd(sv["x1"], sm["ffn_norm"], dh2, wb=d, cb=0, nb=1, shared_gain=True,
                                          out_dtype=F32, name="ffn_norm_bwd", add=dx2, bf16_copy=True)
    w_out = wl["w_out"]
    dya = _matmul(dx1b, w_out[:A_WIDTH], mode="nt", out_dtype=F32, name="out_proj_dx_a", tm=1024, tn=1024)
    dyb = _matmul(dx1b, w_out[A_WIDTH:A_WIDTH + B_WIDTH], mode="nt", out_dtype=F32, name="out_proj_dx_b", tm=1024, tn=1024)
    dyc = _matmul(dx1b, w_out[A_WIDTH + B_WIDTH:], mode="nt", out_dtype=F32, name="out_proj_dx_c", tm=1024, tn=1024)
    gw["w_out"] = _matmul(sv["y"], dx1b, mode="tn", out_dtype=BF16, name="out_proj_dw", tm=512, tn=2048)
    token = send_ffn(gw)
    g_out = _tie(sm["out_norm"], token)
    ga, gb, gc = g_out[:, :A_WIDTH], g_out[:, A_WIDTH:A_WIDTH + B_WIDTH], g_out[:, A_WIDTH + B_WIDTH:]
    doa, dga = _norm_bwd(sv["oa"], ga, dya, wb=A_WIDTH, cb=0, nb=1, shared_gain=True, out_dtype=F32, name="out_norm_a_bwd")
    dob, dgb = _norm_bwd(sv["ob"], gb, dyb, wb=B_WIDTH, cb=0, nb=1, shared_gain=True, out_dtype=F32, name="out_norm_b_bwd")
    doc, dgc = _norm_bwd(sv["oc"], gc, dyc, wb=C_WIDTH, cb=0, nb=1, shared_gain=True, out_dtype=F32, name="out_norm_c_bwd")
    gs["out_norm"] = jnp.concatenate([dga, dgb, dgc], axis=1)
    p = sv["p"]
    dqc, dkc, dvc = _flash_bwd(sv["qc"], sv["kc"], sv["vc"], sv["oc"], doc, sv["lse_c"], None, hkv=C_KV_HEADS,
                               g=C_GROUP, dqk=LANE, q_cb=0, k_cb=0, v_cb=0, v_step=1, scale=HEAD_DIM ** -0.5,
                               tq=t, band=None, name="c_flash_bwd")
    dqn = _rope(dqc, cos_c, sin_c, tw=LANE, cb=0, nb=C_HEADS, half=HEAD_DIM // 4, sign=-1, out_dtype=F32, name="c_rope_q_bwd")
    dkn = _rope(dkc, cos_c, sin_c, tw=LANE, cb=0, nb=C_KV_HEADS, half=HEAD_DIM // 4, sign=-1, out_dtype=F32, name="c_rope_k_bwd")
    dpcq, gs["c_q_norm"] = _norm_bwd(p, sm["c_q_norm"], dqn, wb=LANE, cb=PB_CQH, nb=C_HEADS, shared_gain=True,
                                     out_dtype=BF16, name="c_q_norm_bwd")
    dpck, gs["c_k_norm"] = _norm_bwd(p, sm["c_k_norm"], dkn, wb=LANE, cb=PB_CKH, nb=C_KV_HEADS, shared_gain=True,
                                     out_dtype=BF16, name="c_k_norm_bwd")
    table, band = bias
    dqb, dkb, dvb = _flash_bwd(sv["qb"], sv["kb"], sv["vb"], sv["ob"], dob, sv["lse_b"], table, hkv=B_HEADS, g=1,
                               dqk=LANE, q_cb=0, k_cb=0, v_cb=0, v_step=1, scale=HEAD_DIM ** -0.5, tq=t, band=band,
                               name="b_flash_bwd")
    dpbq = _rope(dqb, cos_b, sin_b, tw=LANE, cb=0, nb=B_HEADS, half=HEAD_DIM // 2, sign=-1, out_dtype=BF16, name="b_rope_q_bwd")
    dpbk = _rope(dkb, cos_b, sin_b, tw=LANE, cb=0, nb=B_HEADS, half=HEAD_DIM // 2, sign=-1, out_dtype=BF16, name="b_rope_k_bwd")
    dqa, dka, dva = _flash_bwd(sv["qa"], sv["ka"], sv["kv"], sv["oa"], doa, sv["lse_a"], None, hkv=A_HEADS, g=1,
                               dqk=A_QK, q_cb=0, k_cb=0, v_cb=1, v_step=2, scale=(A_NOPE + A_ROPE) ** -0.5,
                               tq=t, band=None, name="a_flash_bwd")
    dqa_raw = _rope(dqa, cos_a, sin_a, tw=A_QK, cb=0, nb=A_HEADS, half=A_ROPE // 2, sign=-1, out_dtype=BF16, name="a_rope_q_bwd")
    dkv, dkr = _latent_keys_bwd(dka, dva, cos_a, sin_a, name="a_keys_bwd")
    dckv_n = _matmul(dkv, wl["ukv"], mode="nt", out_dtype=F32, name="a_ukv_dx", tm=1024, tn=512)
    gw["ukv"] = _matmul(sv["ckv_n"], dkv, mode="tn", out_dtype=BF16, name="a_ukv_dw", tm=512, tn=1024)
    dcq_n = _matmul(dqa_raw, wl["uq"], mode="nt", out_dtype=F32, name="a_uq_dx", tm=1024, tn=512)
    gw["uq"] = _matmul(sv["cq_n"], dqa_raw, mode="tn", out_dtype=BF16, name="a_uq_dw", tm=512, tn=1024)
    dcq, gs["a_q_norm"] = _norm_bwd(p, sm["a_q_norm"], dcq_n, wb=A_Q_RANK, cb=0, nb=1, shared_gain=True, out_dtype=BF16,
                                    name="a_q_norm_bwd")
    dckv, gs["a_kv_norm"] = _norm_bwd(p, sm["a_kv_norm"], dckv_n, wb=A_KV_RANK, cb=1, nb=1, shared_gain=True,
                                      out_dtype=BF16, name="a_kv_norm_bwd")
    dp = jnp.concatenate([dcq, dckv, dkr, jnp.zeros((s, A_PAD - (PB_KR + 1) * LANE), BF16), dpbq, dpbk,
                          dvb.astype(BF16), dpcq, dpck, dvc.astype(BF16)], axis=1)
    gw["w_all"] = _matmul(dp, sv["h"], mode="tn", out_dtype=BF16, name="in_proj_dw", tm=640, tn=2048)
    token = send_attn(gw)
    dh = _matmul(dp, wl["w_all"], mode="nn", out_dtype=F32, name="in_proj_dx", tm=1024, tn=512, after=token)
    dx, gs["attn_norm"], dxb = _norm_bwd(sv["x"], sm["attn_norm"], dh, wb=d, cb=0, nb=1, shared_gain=True,
                                         out_dtype=F32, name="attn_norm_bwd", add=dx1, bf16_copy=True)
    return dx, dxb, gs, token


def _pack_small(vals):
    flat = jnp.concatenate([vals[n].reshape(-1).astype(F32) for n in _SMALL])
    tile = SUBLANE * LANE
    padded = -(-flat.shape[0] // tile) * tile
    return jnp.pad(flat, (0, padded - flat.shape[0])).reshape(padded // LANE, LANE)


def _unpack_small(packed, like):
    flat = packed.reshape(-1)
    out, off = {}, 0
    for n in _SMALL:
        size = math.prod(like[n].shape)
        out[n] = flat[off:off + size].reshape(like[n].shape)
        off += size
    return out


def kernel(x, attn_norm, w_in, a_q_norm, a_w_uq, a_kv_norm, a_w_ukv, c_q_norm, c_k_norm, out_norm, w_out, ffn_norm, w_gate, w_up, w_down, final_norm, loss_target, m_attn_norm, m_w_in, m_a_q_norm, m_a_w_uq, m_a_kv_norm, m_a_w_ukv, m_c_q_norm, m_c_k_norm, m_out_norm, m_w_out, m_ffn_norm, m_w_gate, m_w_up, m_w_down, m_final_norm, v_attn_norm, v_w_in, v_a_q_norm, v_a_w_uq, v_a_kv_norm, v_a_w_ukv, v_c_q_norm, v_c_k_norm, v_out_norm, v_w_out, v_ffn_norm, v_w_gate, v_w_up, v_w_down, v_final_norm):
    w = dict(attn_norm=attn_norm, w_in=w_in, a_q_norm=a_q_norm, a_w_uq=a_w_uq, a_kv_norm=a_kv_norm, a_w_ukv=a_w_ukv,
             c_q_norm=c_q_norm, c_k_norm=c_k_norm, out_norm=out_norm, w_out=w_out, ffn_norm=ffn_norm, w_gate=w_gate,
             w_up=w_up, w_down=w_down, final_norm=final_norm)
    m = dict(attn_norm=m_attn_norm, w_in=m_w_in, a_q_norm=m_a_q_norm, a_w_uq=m_a_w_uq, a_kv_norm=m_a_kv_norm,
             a_w_ukv=m_a_w_ukv, c_q_norm=m_c_q_norm, c_k_norm=m_c_k_norm, out_norm=m_out_norm, w_out=m_w_out,
             ffn_norm=m_ffn_norm, w_gate=m_w_gate, w_up=m_w_up, w_down=m_w_down, final_norm=m_final_norm)
    v = dict(attn_norm=v_attn_norm, w_in=v_w_in, a_q_norm=v_a_q_norm, a_w_uq=v_a_w_uq, a_kv_norm=v_a_kv_norm,
             a_w_ukv=v_a_w_ukv, c_q_norm=v_c_q_norm, c_k_norm=v_c_k_norm, out_norm=v_out_norm, w_out=v_w_out,
             ffn_norm=v_ffn_norm, w_gate=v_w_gate, w_up=v_w_up, w_down=v_w_down, final_norm=v_final_norm)
    _, s, d = x.shape
    depth = attn_norm.shape[0]

    def as_stored(a, n):
        return jnp.swapaxes(a, 1, 2) if n == "w_in" else a
    t = _pick(s, 2048)

    me = (2 * lax.axis_index("x") + lax.axis_index("y")).astype(jnp.int32).reshape(1)

    gathers, after = {}, me
    for l in range(depth):
        for group, names in (("attn", _ATTN), ("ffn", _FFN)):
            bufs = [_cast_to_slot(as_stored(w[n], n), me, layer=l, name=f"cast_{n}")
                    for n in names]
            send_sems, recv_sems, bufs, _, after = _exchange_start(bufs, None, after, kind="gather",
                                                                   name=f"gather_start_{group}{l}")
            gathers[group, l] = (send_sems, recv_sems, bufs)
    all_started = after

    def gathered(group, l, after):
        send_sems, recv_sems, bufs = gathers[group, l]
        return _exchange_wait(send_sems, recv_sems, bufs, None, after, kind="gather", name=f"gather_wait_{group}{l}")

    tabs = _rope_tables(s)
    bias = _band_table(min(t, ATTN_ROW_CHUNK), s)

    xs = x.reshape(s, d)
    saved, wls, sms = [], [], []
    for l in range(depth):
        wl = _assemble_attn(gathered("attn", l, all_started if l == 0 else xs))
        sm = {n: w[n][l][None, :] for n in _SMALL if n != "final_norm"}
        xs, sv, wl = _layer_fwd(xs, wl, lambda after, l=l: _assemble_ffn(gathered("ffn", l, after)), sm, tabs, bias, t)
        saved.append(sv)
        wls.append(wl)
        sms.append(sm)
    dx, g_final, loss_row, dxb = _final_loss(xs, final_norm[None, :], loss_target.reshape(s, d), name="final_loss")
    loss = lax.psum(loss_row[0, 0], ("x", "y", "c"))

    sends = {}

    def send(group, l, srcs, after):
        lands = [lax.empty((3,) + a.shape[1:], BF16) for a in srcs]
        send_sems, recv_sems, srcs, lands, token = _exchange_start(srcs, lands, after, kind="scatter",
                                                                   name=f"scatter_start_{group}{l}")
        sends[group, l] = (send_sems, recv_sems, srcs, lands)
        return token

    gs_layers, token = [None] * depth, all_started
    for l in reversed(range(depth)):
        dx, dxb, gs_layers[l], token = _layer_bwd(
            dx, dxb, saved[l], wls[l], sms[l], tabs, bias, t,
            lambda gw, l=l, tk=token: send("ffn", l, _split_ffn_grads(gw), tk),
            lambda gw, l=l: send("attn", l, _split_attn_grads(gw), dx))
    grad_x = dx.reshape(x.shape)

    srcs, lands = {}, {}

    def arrive(key, after):
        send_sems, recv_sems, s_bufs, l_bufs = sends[key]
        got = _exchange_wait(send_sems, recv_sems, s_bufs, l_bufs, after, kind="scatter",
                             name=f"scatter_wait_{key[0]}{key[1]}")
        for k, n in enumerate(_ATTN if key[0] == "attn" else _FFN):
            srcs[n, key[1]], lands[n, key[1]] = got[k], got[len(s_bufs) + k]

    def summed(names):
        return [_sum_parts([srcs[n, l] for l in range(depth)], [lands[n, l] for l in range(depth)], me, name="sum_" + n)
                for n in names]

    last = ("attn", 0)
    for key in sends:
        if key != last:
            arrive(key, token)
    sums_ffn = summed(_FFN)
    swap = _exchange_start(sums_ffn, [lax.empty(a.shape, F32) for a in sums_ffn], token, kind="swap",
                           name="swap_start_ffn")
    arrive(last, swap[4])
    sums_attn = summed(_ATTN)
    sib_attn = list(_sibling_exchange(sums_attn, name="swap_core_sums_attn"))
    swapped = _exchange_wait(swap[0], swap[1], swap[2], swap[3], sib_attn[0], kind="swap", name="swap_wait_ffn")
    mine_of = dict(zip(_FFN + _ATTN, swapped[:len(_FFN)] + sums_attn))
    other_of = dict(zip(_FFN + _ATTN, swapped[len(_FFN):] + sib_attn))
    grads, deltas, new_m, new_v = {}, {}, {}, {}
    for n in _BIG:
        res = _adamw(mine_of[n], other_of[n], as_stored(w[n], n), as_stored(m[n], n), as_stored(v[n], n), name="adamw_" + n)
        grads[n], deltas[n], new_m[n], new_v[n] = [as_stored(r, n) for r in res]

    gsm = {n: jnp.stack([gs_layers[l][n][0] for l in range(depth)]) for n in _SMALL if n != "final_norm"}
    gsm["final_norm"] = g_final[0]
    packed = _pack_small(gsm)
    everyone = _all_gather_small(packed, name="gather_gain_grads").reshape(N_DEV, packed.shape[0], LANE)
    res = _small_adamw(everyone, _pack_small(w), _pack_small(m), _pack_small(v), name="adamw_gains")
    for dst, r in zip((grads, deltas, new_m, new_v), res):
        dst.update(_unpack_small(r, w))

    return (loss, grad_x, *[grads[n] for n in _WEIGHTS], *[deltas[n] for n in _WEIGHTS],
            *[new_m[n] for n in _WEIGHTS], *[new_v[n] for n in _WEIGHTS])
```

```python
import functools
import math

import jax
import jax.numpy as jnp
import numpy as np
from jax import lax
from jax.experimental import pallas as pl
from jax.experimental.pallas import tpu as pltpu

F32 = jnp.float32
BF16 = jnp.bfloat16
MESH = pl.DeviceIdType.MESH

HEAD_DIM = 128
ROPE_THETA = 10000.0
GRID_W = 64
EPS = 1e-6
NEG = -1e30
A_HEADS, A_Q_RANK, A_KV_RANK, A_NOPE, A_ROPE, A_V = 4, 512, 512, 128, 64, 128
B_HEADS = 6
B_PATTERNS = ((128, 1), (512, 4), (2048, 16))
C_HEADS, C_KV_HEADS = 6, 2
C_GROUP = C_HEADS // C_KV_HEADS
A_WIDTH, B_WIDTH, C_WIDTH = A_HEADS * A_V, B_HEADS * HEAD_DIM, C_HEADS * HEAD_DIM
IN_A = A_Q_RANK + A_KV_RANK + A_ROPE
IN_B = 3 * B_WIDTH
IN_C = C_WIDTH + 2 * C_KV_HEADS * HEAD_DIM
ADAM_LR, ADAM_B1, ADAM_B2, ADAM_EPS, ADAM_WD, ADAM_STEP = 0.001, 0.9, 0.999, 1e-08, 0.01, 10

LANE = 128
SUBLANE = 8
VMEM_BYTES_V7X = 64 * 1024 * 1024
VMEM_LIMIT_CAP = VMEM_BYTES_V7X - 8 * 1024 * 1024
N_CHIPS = 4
N_DEV = 8

A_PAD = 12 * LANE
PB_CQ, PB_CKV, PB_KR = 0, 4, 8
PB_BQ, PB_BK, PB_BV = 12, 18, 24
PB_CQH, PB_CKH, PB_CVH = 30, 36, 38
NP = 40 * LANE
A_QK = 2 * LANE


def _pick(n, cap, mult=LANE):
    if n <= cap:
        return n
    t = cap - cap % mult
    while t >= mult:
        if n % t == 0:
            return t
        t -= mult
    return n


def _rows_for(width_bytes, n_rows, target=2 * 1024 * 1024):
    return _pick(n_rows, max(SUBLANE, target // max(width_bytes, 1)), SUBLANE)


def _tile2(rows, cols, target):
    tc = _pick(cols, 4 * LANE)
    if tc < 4 * LANE:
        tc = cols
    fits = [t for t in range(SUBLANE, rows + 1, SUBLANE) if rows % t == 0] or [rows]
    return min(fits, key=lambda t: abs(math.log(t * tc * 4 / target))), tc


def _params(est_bytes):
    limit = int(min(max(est_bytes + (4 << 20), 32 << 20), VMEM_LIMIT_CAP))
    return pltpu.CompilerParams(vmem_limit_bytes=limit)


def _isz(x):
    return jnp.dtype(x.dtype).itemsize


def _hbm(shape, dtype):
    return pltpu.HBM(shape, dtype)


def _pin(*arrays):
    return [pltpu.with_memory_space_constraint(a, pltpu.HBM) for a in arrays]


_DIMS = {"nn": (((1,), (0,)), ((), ())), "nt": (((1,), (1,)), ((), ())), "tn": (((0,), (0,)), ((), ()))}


def _matmul(a, b, *, mode, out_dtype, name, add=None, tm=512, tn=512, col_shards=False, after=None):
    if mode == "tn":
        (k, m), (k2, n) = a.shape, b.shape
    elif mode == "nt":
        (m, k), (n, k2) = a.shape, b.shape
    else:
        (m, k), (k2, n) = a.shape, b.shape
    assert k == k2, (a.shape, b.shape, mode)
    tm, tn = _pick(m, tm), (n // N_CHIPS if col_shards else _pick(n, tn))
    a_spec = pl.BlockSpec((k, tm), lambda i, j: (0, i)) if mode == "tn" else pl.BlockSpec((tm, k), lambda i, j: (i, 0))
    b_spec = pl.BlockSpec((tn, k), lambda i, j: (j, 0)) if mode == "nt" else pl.BlockSpec((k, tn), lambda i, j: (0, j))
    o_spec = pl.BlockSpec((None, tm, tn), lambda i, j: (j, i, 0)) if col_shards else pl.BlockSpec((tm, tn), lambda i, j: (i, j))
    dims = _DIMS[mode]

    def body(*refs):
        a_ref, b_ref, o_ref = refs[0], refs[1], refs[-1]
        acc = lax.dot_general(a_ref[...].astype(BF16), b_ref[...].astype(BF16), dims, preferred_element_type=F32)
        if add is not None:
            acc = acc + refs[2][...].astype(F32)
        o_ref[...] = acc.astype(out_dtype)

    ins, specs = [a, b], [a_spec, b_spec]
    if add is not None:
        ins.append(add)
        specs.append(o_spec)
    if after is not None:
        ins.append(after)
        specs.append(pl.BlockSpec(memory_space=pl.ANY))
    est = 2 * (tm * k * _isz(a) + tn * k * _isz(b) + tm * tn * (jnp.dtype(out_dtype).itemsize + (4 if add is not None else 0)))
    est += (tm + tn) * k * 2 + 2 * tm * tn * 4
    return pl.pallas_call(
        body, name=name, grid=(m // tm, n // tn), in_specs=specs, out_specs=o_spec,
        out_shape=_hbm((N_CHIPS, m, tn) if col_shards else (m, n), out_dtype),
        compiler_params=_params(est),
    )(*_pin(*ins))


def _ffn_up(h, wg, wu, *, name):
    s, d = h.shape
    _, _, c = wg.shape
    tm = _pick(s, 512, SUBLANE)

    def body(h_ref, wg_ref, wu_ref, g_ref, u_ref, a_ref):
        hv = h_ref[...]
        gv = jnp.dot(hv, wg_ref[...], preferred_element_type=F32)
        uv = jnp.dot(hv, wu_ref[...], preferred_element_type=F32)
        g_ref[...] = gv.astype(BF16)
        u_ref[...] = uv.astype(BF16)
        a_ref[...] = (gv / (1.0 + jnp.exp(-gv)) * uv).astype(BF16)

    w_spec = pl.BlockSpec((None, d, c), lambda j, i: (j, 0, 0))
    o_spec = pl.BlockSpec((tm, c), lambda j, i: (i, j))
    est = 2 * (tm * d * 2 + 2 * d * c * 2 + tm * c * 10) + 4 * tm * c * 4
    return pl.pallas_call(
        body, name=name, grid=(N_CHIPS, s // tm), in_specs=[pl.BlockSpec((tm, d), lambda j, i: (i, 0)), w_spec, w_spec],
        out_specs=[o_spec, o_spec, o_spec],
        out_shape=[_hbm((s, N_CHIPS * c), BF16)] * 3,
        compiler_params=_params(est),
    )(*_pin(h, wg, wu))


def _ffn_down_dx(dx, w_down, gate, up, *, name):
    s, d = dx.shape
    f = w_down.shape[0]
    tm, tn = _pick(s, 1024, SUBLANE), _pick(f, 512)

    def body(dx_ref, w_ref, g_ref, u_ref, dg_ref, du_ref):
        dact = lax.dot_general(dx_ref[...], w_ref[...], _DIMS["nt"], preferred_element_type=F32)
        gv, uv = g_ref[...].astype(F32), u_ref[...].astype(F32)
        sig = 1.0 / (1.0 + jnp.exp(-gv))
        dg_ref[...] = (dact * uv * (sig * (1.0 + gv * (1.0 - sig)))).astype(BF16)
        du_ref[...] = (dact * (gv * sig)).astype(BF16)

    t_spec = pl.BlockSpec((tm, tn), lambda i, j: (i, j))
    est = 2 * (tm * d * 2 + tn * d * 2 + tm * tn * 12) + 6 * tm * tn * 4
    return pl.pallas_call(
        body, name=name, grid=(s // tm, f // tn),
        in_specs=[pl.BlockSpec((tm, d), lambda i, j: (i, 0)), pl.BlockSpec((tn, d), lambda i, j: (j, 0)), t_spec, t_spec],
        out_specs=[t_spec, t_spec], out_shape=[_hbm((s, f), BF16)] * 2, compiler_params=_params(est),
    )(*_pin(dx, w_down, gate, up))


def _ffn_up_dx(dgate, dup, wg, wu, *, name):
    s, f = dgate.shape
    _, d, c = wg.shape
    tm, tn = _pick(s, 1024, SUBLANE), _pick(d, 1024)
    nk = 2 * N_CHIPS

    def body(dg_ref, du_ref, wg_ref, wu_ref, o_ref, acc):
        kk = pl.program_id(2)

        @pl.when(kk == 0)
        def _():
            acc[...] = jnp.zeros_like(acc)

        @pl.when(kk < N_CHIPS)
        def _():
            acc[...] += lax.dot_general(dg_ref[...], wg_ref[...], _DIMS["nt"], preferred_element_type=F32)

        @pl.when(kk >= N_CHIPS)
        def _():
            acc[...] += lax.dot_general(du_ref[...], wu_ref[...], _DIMS["nt"], preferred_element_type=F32)

        @pl.when(kk == nk - 1)
        def _():
            o_ref[...] = acc[...]

    last = N_CHIPS - 1
    est = 2 * (2 * tm * c * 2 + 2 * tn * c * 2 + tm * tn * 4) + 2 * tm * tn * 4
    return pl.pallas_call(
        body, name=name, grid=(s // tm, d // tn, nk),
        in_specs=[pl.BlockSpec((tm, c), lambda i, j, kk: (i, jnp.minimum(kk, last))),
                  pl.BlockSpec((tm, c), lambda i, j, kk: (i, jnp.maximum(kk - N_CHIPS, 0))),
                  pl.BlockSpec((None, tn, c), lambda i, j, kk: (jnp.minimum(kk, last), j, 0)),
                  pl.BlockSpec((None, tn, c), lambda i, j, kk: (jnp.maximum(kk - N_CHIPS, 0), j, 0))],
        out_specs=pl.BlockSpec((tm, tn), lambda i, j, kk: (i, j)),
        out_shape=_hbm((s, d), F32), scratch_shapes=[pltpu.VMEM((tm, tn), F32)],
        compiler_params=_params(est),
    )(*_pin(dgate, dup, wg, wu))


def _norm_fwd(x, gain, *, wb, cb, nb, shared_gain, out_dtype, name):
    s = x.shape[0]
    ts = _rows_for(wb * 4, s)

    def body(x_ref, g_ref, o_ref):
        xv = x_ref[...].astype(F32)
        r = lax.rsqrt(jnp.mean(xv * xv, axis=1, keepdims=True) + EPS)
        o_ref[...] = ((xv * r) * g_ref[...]).astype(out_dtype)

    return pl.pallas_call(
        body, name=name, grid=(nb, s // ts),
        in_specs=[pl.BlockSpec((ts, wb), lambda n, i: (i, cb + n)),
                  pl.BlockSpec((1, wb), (lambda n, i: (0, 0)) if shared_gain else (lambda n, i: (0, n)))],
        out_specs=pl.BlockSpec((ts, wb), lambda n, i: (i, n)),
        out_shape=_hbm((s, nb * wb), out_dtype), compiler_params=_params(6 * ts * wb * 4),
    )(*_pin(x), gain)


def _norm_bwd(x, gain, dy, *, wb, cb, nb, shared_gain, out_dtype, name, dy_cb=0, add=None, bf16_copy=False):
    s = x.shape[0]
    ts = _rows_for(wb * 4, s, target=1024 * 1024)
    gw = wb if shared_gain else nb * wb

    def body(*refs):
        refs = list(refs)
        dxb_ref = refs.pop() if bf16_copy else None
        if add is None:
            x_ref, g_ref, dy_ref, dx_ref, dg_ref = refs
        else:
            x_ref, g_ref, dy_ref, add_ref, dx_ref, dg_ref = refs
        n, i = pl.program_id(0), pl.program_id(1)
        xv = x_ref[...].astype(F32)
        dyv = dy_ref[...].astype(F32)
        r = lax.rsqrt(jnp.mean(xv * xv, axis=1, keepdims=True) + EPS)
        xh = xv * r
        dyg = dyv * g_ref[...]
        dx = r * (dyg - xh * jnp.mean(dyg * xh, axis=1, keepdims=True))
        if add is not None:
            dx = dx + add_ref[...]
        dx_ref[...] = dx.astype(out_dtype)
        if bf16_copy:
            dxb_ref[...] = dx.astype(BF16)
        first = jnp.logical_and(n == 0, i == 0) if shared_gain else (i == 0)

        @pl.when(first)
        def _():
            dg_ref[...] = jnp.zeros_like(dg_ref)

        dg_ref[...] += jnp.sum(dyv * xh, axis=0, keepdims=True)

    ins = [x, gain, dy]
    specs = [pl.BlockSpec((ts, wb), lambda n, i: (i, cb + n)),
             pl.BlockSpec((1, wb), (lambda n, i: (0, 0)) if shared_gain else (lambda n, i: (0, n))),
             pl.BlockSpec((ts, wb), lambda n, i: (i, dy_cb + n))]
    if add is not None:
        ins.append(add)
        specs.append(pl.BlockSpec((ts, wb), lambda n, i: (i, n)))
    out_specs = [pl.BlockSpec((ts, wb), lambda n, i: (i, n)),
                 pl.BlockSpec((1, wb), (lambda n, i: (0, 0)) if shared_gain else (lambda n, i: (0, n)))]
    out_shape = [_hbm((s, nb * wb), out_dtype), jax.ShapeDtypeStruct((1, gw), F32)]
    if bf16_copy:
        out_specs.append(out_specs[0])
        out_shape.append(_hbm((s, nb * wb), BF16))
    return pl.pallas_call(
        body, name=name, grid=(nb, s // ts), in_specs=specs, out_specs=out_specs, out_shape=out_shape,
        compiler_params=_params(14 * ts * wb * 4),
    )(*_pin(*ins))


def _swap_halves(x, half):
    if 2 * half == LANE:
        return pltpu.roll(x, half, axis=1)
    lane = lax.broadcasted_iota(jnp.int32, x.shape, 1)
    first = jnp.bitwise_and(lane, 2 * half - 1) < half
    return jnp.where(first, pltpu.roll(x, LANE - half, axis=1), pltpu.roll(x, half, axis=1))


def _rope(x, cos_t, sin_t, *, tw, cb, nb, half, sign, out_dtype, name):
    s = x.shape[0]
    ts = _rows_for(tw * 4, s)

    def body(x_ref, c_ref, s_ref, o_ref):
        for q in range(tw // LANE):
            sl = slice(q * LANE, (q + 1) * LANE)
            xv = x_ref[:, sl].astype(F32)
            sv = s_ref[:, sl]
            if sign < 0:
                sv = -sv
            o_ref[:, sl] = (xv * c_ref[:, sl] + _swap_halves(xv, half) * sv).astype(out_dtype)

    return pl.pallas_call(
        body, name=name, grid=(nb, s // ts),
        in_specs=[pl.BlockSpec((ts, tw), lambda n, i: (i, cb + n)),
                  pl.BlockSpec((ts, tw), lambda n, i: (i, 0)),
                  pl.BlockSpec((ts, tw), lambda n, i: (i, 0))],
        out_specs=pl.BlockSpec((ts, tw), lambda n, i: (i, n)),
        out_shape=_hbm((s, nb * tw), out_dtype), compiler_params=_params(10 * ts * tw * 4),
    )(*_pin(x), cos_t, sin_t)


def _latent_keys(kv, p, cos_t, sin_t, *, kr_cb, name):
    s = kv.shape[0]
    ts = _rows_for(A_QK * 4, s)

    def body(kv_ref, kr_ref, c_ref, s_ref, o_ref):
        o_ref[:, :LANE] = kv_ref[...].astype(BF16)
        x = kr_ref[...].astype(F32)
        o_ref[:, LANE:] = (x * c_ref[:, LANE:] + _swap_halves(x, A_ROPE // 2) * s_ref[:, LANE:]).astype(BF16)

    tab = pl.BlockSpec((ts, A_QK), lambda n, i: (i, 0))
    return pl.pallas_call(
        body, name=name, grid=(A_HEADS, s // ts),
        in_specs=[pl.BlockSpec((ts, LANE), lambda n, i: (i, 2 * n)), pl.BlockSpec((ts, LANE), lambda n, i: (i, kr_cb)),
                  tab, tab],
        out_specs=pl.BlockSpec((ts, A_QK), lambda n, i: (i, n)),
        out_shape=_hbm((s, A_HEADS * A_QK), BF16), compiler_params=_params(10 * ts * A_QK * 4),
    )(*_pin(kv, p), cos_t, sin_t)


def _latent_keys_bwd(dka, dva, cos_t, sin_t, *, name):
    s = dka.shape[0]
    ts = _rows_for(A_HEADS * A_QK * 4, s)

    def body(dka_ref, dva_ref, c_ref, s_ref, dkv_ref, dkr_ref):
        acc = jnp.zeros((ts, LANE), F32)
        for h in range(A_HEADS):
            dkv_ref[:, h * A_QK:h * A_QK + LANE] = dka_ref[:, h * A_QK:h * A_QK + LANE].astype(BF16)
            dkv_ref[:, h * A_QK + LANE:(h + 1) * A_QK] = dva_ref[:, h * LANE:(h + 1) * LANE].astype(BF16)
            y = dka_ref[:, h * A_QK + LANE:(h + 1) * A_QK]
            acc = acc + (y * c_ref[:, LANE:] - _swap_halves(y, A_ROPE // 2) * s_ref[:, LANE:])
        dkr_ref[...] = acc.astype(BF16)

    def rows(width):
        return pl.BlockSpec((ts, width), lambda i: (i, 0))

    return pl.pallas_call(
        body, name=name, grid=(s // ts,),
        in_specs=[rows(A_HEADS * A_QK), rows(A_HEADS * LANE), rows(A_QK), rows(A_QK)],
        out_specs=[rows(A_HEADS * A_QK), rows(LANE)],
        out_shape=[_hbm((s, A_HEADS * A_QK), BF16), _hbm((s, LANE), BF16)],
        compiler_params=_params(8 * ts * A_HEADS * A_QK * 4),
    )(*_pin(dka, dva), cos_t, sin_t)


def _cast_cols(x, *, cb, nb, name):
    s = x.shape[0]
    ts = _rows_for(LANE * 4, s)

    def body(x_ref, o_ref):
        o_ref[...] = x_ref[...].astype(BF16)

    return pl.pallas_call(
        body, name=name, grid=(nb, s // ts), in_specs=[pl.BlockSpec((ts, LANE), lambda n, i: (i, cb + n))],
        out_specs=pl.BlockSpec((ts, LANE), lambda n, i: (i, n)),
        out_shape=_hbm((s, nb * LANE), BF16), compiler_params=_params(4 * ts * LANE * 4),
    )(*_pin(x))


LOG2E = 1.4426950408889634
ATTN_ROW_CHUNK = 256


def _attn_window(i, rc, s, band):
    w, r = band
    start = jnp.clip(i * rc - r, 0, s - w)
    return pl.multiple_of(start, rc), pl.multiple_of((w - rc) - (i * rc - start), LANE)


def _flash_fwd(q, k, v, table, *, hkv, g, dqk, q_cb, k_cb, v_cb, v_step, scale, tq, band, name):
    s = q.shape[0]
    n = s // tq
    hq = hkv * g
    rc = min(tq, ATTN_ROW_CHUNK)
    w = s if band is None else band[0]

    def body(*refs):
        if band is None:
            q_ref, k_ref, v_ref, o_ref, lse_ref = refs
            kw, vw = k_ref[...], v_ref[...]
        else:
            q_ref, k_ref, v_ref, t_ref, o_ref, lse_ref = refs
        for c in range(tq // rc):
            rows = slice(c * rc, (c + 1) * rc)
            if band is not None:
                start, u = _attn_window(pl.program_id(1) * (tq // rc) + c, rc, s, band)
                kw, vw = k_ref[pl.ds(start, w), :], v_ref[pl.ds(start, w), :]
            sc = lax.dot_general(q_ref[rows, :], kw, _DIMS["nt"], preferred_element_type=F32) * (scale * LOG2E)
            if band is not None:
                sc = sc + t_ref[:, pl.ds(u, w)]
            m = jnp.max(sc, axis=1, keepdims=True)
            p = jnp.exp2(sc - m)
            l = jnp.sum(p, axis=1, keepdims=True)
            o_ref[rows, :] = jnp.dot(p.astype(BF16), vw, preferred_element_type=F32) / l
            lse_ref[0, rows, :] = jnp.broadcast_to(m + jnp.log2(l), (rc, LANE))

    ins = [q, k, v]
    specs = [pl.BlockSpec((tq, dqk), lambda h, i: (i, q_cb + h)),
             pl.BlockSpec((s, dqk), lambda h, i: (0, k_cb + h // g)),
             pl.BlockSpec((s, LANE), lambda h, i: (0, v_cb + v_step * (h // g)))]
    if band is not None:
        ins.append(table)
        specs.append(pl.BlockSpec(table.shape, lambda h, i: (0, 0)))
    est = 4 * s * (dqk + LANE) + 6 * rc * w * 4 + 8 * tq * LANE * 4 + (0 if band is None else 2 * table.size * 4)
    return pl.pallas_call(
        body, name=name, grid=(hq, n), in_specs=specs,
        out_specs=[pl.BlockSpec((tq, LANE), lambda h, i: (i, h)), pl.BlockSpec((1, tq, LANE), lambda h, i: (h, i, 0))],
        out_shape=[_hbm((s, hq * LANE), F32), _hbm((hq, s, LANE), F32)],
        compiler_params=_params(est),
    )(*_pin(*ins))


def _flash_bwd(q, k, v, o, do, lse, table, *, hkv, g, dqk, q_cb, k_cb, v_cb, v_step, scale, tq, band, name):
    s = q.shape[0]
    n = s // tq
    hq = hkv * g
    rc = min(tq, ATTN_ROW_CHUNK)
    w = s if band is None else band[0]

    def body(*refs):
        if band is None:
            q_ref, k_ref, v_ref, o_ref, do_ref, lse_ref, dq_ref, dk_ref, dv_ref = refs
            keys = slice(None)
            kw, vw = k_ref[...], v_ref[...]
        else:
            q_ref, k_ref, v_ref, o_ref, do_ref, lse_ref, t_ref, dq_ref, dk_ref, dv_ref = refs
        h, i = pl.program_id(0), pl.program_id(1)

        @pl.when(jnp.logical_and(h % g == 0, i == 0))
        def _():
            dk_ref[...] = jnp.zeros_like(dk_ref)
            dv_ref[...] = jnp.zeros_like(dv_ref)

        for c in range(tq // rc):
            rows = slice(c * rc, (c + 1) * rc)
            if band is not None:
                start, u = _attn_window(i * (tq // rc) + c, rc, s, band)
                keys = pl.ds(start, w)
                kw, vw = k_ref[keys, :], v_ref[keys, :]
            qv = q_ref[rows, :]
            dof = do_ref[rows, :]
            dov = dof.astype(BF16)
            sc = lax.dot_general(qv, kw, _DIMS["nt"], preferred_element_type=F32) * (scale * LOG2E)
            if band is not None:
                sc = sc + t_ref[:, pl.ds(u, w)]
            p = jnp.exp2(sc - lse_ref[0, rows, 0:1])
            dp = lax.dot_general(dov, vw, _DIMS["nt"], preferred_element_type=F32)
            delta = jnp.sum(dof * o_ref[rows, :], axis=1, keepdims=True)
            ds = (p * (dp - delta) * scale).astype(BF16)
            dv_ref[keys, :] += lax.dot_general(p.astype(BF16), dov, _DIMS["tn"], preferred_element_type=F32)
            dk_ref[keys, :] += lax.dot_general(ds, qv, _DIMS["tn"], preferred_element_type=F32)
            dq_ref[rows, :] = jnp.dot(ds, kw, preferred_element_type=F32)

    ins = [q, k, v, o, do, lse]
    specs = [pl.BlockSpec((tq, dqk), lambda h, i: (i, q_cb + h)),
             pl.BlockSpec((s, dqk), lambda h, i: (0, k_cb + h // g)),
             pl.BlockSpec((s, LANE), lambda h, i: (0, v_cb + v_step * (h // g))),
             pl.BlockSpec((tq, LANE), lambda h, i: (i, h)),
             pl.BlockSpec((tq, LANE), lambda h, i: (i, h)),
             pl.BlockSpec((1, tq, LANE), lambda h, i: (h, i, 0))]
    if band is not None:
        ins.append(table)
        specs.append(pl.BlockSpec(table.shape, lambda h, i: (0, 0)))
    est = (4 + 8) * s * (dqk + LANE) + 10 * rc * w * 4 + 12 * tq * LANE * 4 + (0 if band is None else 2 * table.size * 4)
    return pl.pallas_call(
        body, name=name, grid=(hq, n), in_specs=specs,
        out_specs=[pl.BlockSpec((tq, dqk), lambda h, i: (i, h)),
                   pl.BlockSpec((s, dqk), lambda h, i: (0, h // g)),
                   pl.BlockSpec((s, LANE), lambda h, i: (0, h // g))],
        out_shape=[_hbm((s, hq * dqk), F32), _hbm((s, hkv * dqk), F32),
                   _hbm((s, hkv * LANE), F32)],
        compiler_params=_params(est),
    )(*_pin(*ins))


def _final_loss(x, gain, target, *, name):
    s, d = x.shape
    ts = _rows_for(d * 4, s, target=1024 * 1024)

    def body(x_ref, g_ref, t_ref, dx_ref, dg_ref, loss_ref, dxb_ref):
        i = pl.program_id(0)
        xv = x_ref[...]
        gv = g_ref[...]
        r = lax.rsqrt(jnp.mean(xv * xv, axis=1, keepdims=True) + EPS)
        xh = xv * r
        err = xh * gv - t_ref[...]
        dy = err / d
        dyg = dy * gv
        dx = r * (dyg - xh * jnp.mean(dyg * xh, axis=1, keepdims=True))
        dx_ref[...] = dx
        dxb_ref[...] = dx.astype(BF16)

        @pl.when(i == 0)
        def _():
            dg_ref[...] = jnp.zeros_like(dg_ref)
            loss_ref[...] = jnp.zeros_like(loss_ref)

        dg_ref[...] += jnp.sum(dy * xh, axis=0, keepdims=True)
        part = jnp.sum(jnp.mean(err * err, axis=1, keepdims=True), axis=0, keepdims=True)
        loss_ref[...] += jnp.broadcast_to(0.5 * part, (1, LANE))

    row = pl.BlockSpec((ts, d), lambda i: (i, 0))
    return pl.pallas_call(
        body, name=name, grid=(s // ts,),
        in_specs=[row, pl.BlockSpec((1, d), lambda i: (0, 0)), row],
        out_specs=[row, pl.BlockSpec((1, d), lambda i: (0, 0)), pl.BlockSpec((1, LANE), lambda i: (0, 0)), row],
        out_shape=[_hbm((s, d), F32), jax.ShapeDtypeStruct((1, d), F32),
                   jax.ShapeDtypeStruct((1, LANE), F32), _hbm((s, d), BF16)],
        compiler_params=_params(14 * ts * d * 4),
    )(*_pin(x), gain, *_pin(target))


def _cast_to_slot(x3d, me, *, layer, name):
    _, rows, c = x3d.shape
    tr, tc = _tile2(rows, c, 2 * 1024 * 1024)

    def body(me_ref, x_ref, o_ref):
        o_ref[...] = x_ref[...].astype(BF16)

    return pl.pallas_call(
        body, name=name,
        grid_spec=pltpu.PrefetchScalarGridSpec(
            num_scalar_prefetch=1, grid=(rows // tr, c // tc),
            in_specs=[pl.BlockSpec((None, tr, tc), lambda i, j, me_ref: (layer, i, j))],
            out_specs=pl.BlockSpec((None, tr, tc), lambda i, j, me_ref: (me_ref[0], i, j))),
        out_shape=_hbm((N_CHIPS, rows, c), BF16), compiler_params=_params(6 * tr * tc * 4),
    )(me, *_pin(x3d))


def _sum_parts(srcs, lands, me, *, name):
    depth = len(srcs)
    _, r, c = srcs[0].shape
    tr, tc = _tile2(r, c, 1024 * 1024)
    nt, nc = r // tr, c // tc

    def body(me_ref, *refs):
        o_ref = refs[-1]
        l = pl.program_id(0)
        for k in range(depth):
            @pl.when(l == k)
            def _(k=k):
                acc = refs[k][...].astype(F32)
                for p in range(3):
                    acc = acc + refs[depth + k][p].astype(F32)
                o_ref[...] = acc

    def tile_of(k):
        def f(l, i, j):
            return (jnp.where(l == k, i, jnp.where(l < k, 0, nt - 1)), jnp.where(l == k, j, jnp.where(l < k, 0, nc - 1)))
        return f

    in_specs = [pl.BlockSpec((None, tr, tc), functools.partial(lambda l, i, j, me_ref, f: (me_ref[0], *f(l, i, j)), f=tile_of(k)))
                for k in range(depth)]
    in_specs += [pl.BlockSpec((3, tr, tc), functools.partial(lambda l, i, j, me_ref, f: (0, *f(l, i, j)), f=tile_of(k)))
                 for k in range(depth)]
    return pl.pallas_call(
        body, name=name,
        grid_spec=pltpu.PrefetchScalarGridSpec(
            num_scalar_prefetch=1, grid=(depth, nt, nc), in_specs=in_specs,
            out_specs=pl.BlockSpec((tr, tc), lambda l, i, j, me_ref: (l * nt + i, j))),
        out_shape=_hbm((depth * r, c), F32), compiler_params=_params(depth * 10 * tr * tc * 4),
    )(me, *_pin(*srcs, *lands))


def _adamw_math(w, g, m, v):
    m2 = ADAM_B1 * m + (1.0 - ADAM_B1) * g
    v2 = ADAM_B2 * v + (1.0 - ADAM_B2) * (g * g)
    m_hat = m2 / (1.0 - ADAM_B1 ** ADAM_STEP)
    v_hat = v2 / (1.0 - ADAM_B2 ** ADAM_STEP)
    delta = -ADAM_LR * (m_hat / (jnp.sqrt(v_hat) + ADAM_EPS) + ADAM_WD * w)
    return delta, m2, v2


def _adamw(g_a, g_b, w, m, v, *, name):
    depth, r, c = w.shape
    tr, tc = _tile2(r, c, 512 * 1024)
    nt = r // tr

    def body(a_ref, b_ref, w_ref, m_ref, v_ref, g_out, d_out, m_out, v_out):
        gv = a_ref[...] + b_ref[...]
        delta, m2, v2 = _adamw_math(w_ref[...], gv, m_ref[...], v_ref[...])
        g_out[...] = gv
        d_out[...] = delta
        m_out[...] = m2
        v_out[...] = v2

    flat = pl.BlockSpec((tr, tc), lambda l, i, j: (l * nt + i, j))
    spec = pl.BlockSpec((None, tr, tc), lambda l, i, j: (l, i, j))
    return pl.pallas_call(
        body, name=name, grid=(depth, nt, c // tc), in_specs=[flat, flat, spec, spec, spec], out_specs=[spec] * 4,
        out_shape=[_hbm((depth, r, c), F32)] * 4, compiler_params=_params(22 * tr * tc * 4),
    )(*_pin(g_a, g_b, w, m, v))


def _small_adamw(g_all, w, m, v, *, name):
    r, c = w.shape

    def body(ga_ref, w_ref, m_ref, v_ref, g_out, d_out, m_out, v_out):
        gv = ga_ref[0]
        for j in range(1, N_DEV):
            gv = gv + ga_ref[j]
        delta, m2, v2 = _adamw_math(w_ref[...], gv, m_ref[...], v_ref[...])
        g_out[...] = gv
        d_out[...] = delta
        m_out[...] = m2
        v_out[...] = v2

    return pl.pallas_call(body, name=name, out_shape=[jax.ShapeDtypeStruct((r, c), F32)] * 4)(g_all, w, m, v)


_ANY = pl.BlockSpec(memory_space=pl.ANY)


_HBM = pl.BlockSpec(memory_space=pltpu.HBM)
_SEM = pl.BlockSpec(memory_space=pltpu.SEMAPHORE)
_EFFECT = pltpu.SideEffectType.DATAFLOW_SIDE_EFFECTING


def _peer_chips():
    x, y = lax.axis_index("x"), lax.axis_index("y")
    return 2 * x + y, [(1 - x, y), (x, 1 - y), (1 - x, 1 - y)]


def _exchange_copy(srcs, lands, send_sems, recv_sems, k, p, kind):
    c = lax.axis_index("c")
    if kind == "swap":
        return pltpu.make_async_remote_copy(
            src_ref=srcs[k], dst_ref=lands[k], send_sem=send_sems.at[k], recv_sem=recv_sems.at[k],
            device_id=(lax.axis_index("x"), lax.axis_index("y"), 1 - c), device_id_type=MESH)
    me, peers = _peer_chips()
    px, py = peers[p]
    return pltpu.make_async_remote_copy(
        src_ref=srcs[k].at[2 * px + py] if kind == "scatter" else srcs[k].at[me],
        dst_ref=lands[k].at[p] if kind == "scatter" else lands[k].at[me],
        send_sem=send_sems.at[3 * k + p], recv_sem=recv_sems.at[3 * k + p],
        device_id=(px, py, c), device_id_type=MESH)


def _exchange_start(srcs, lands, after, *, kind, name):
    n = len(srcs)
    npeer = 1 if kind == "swap" else 3
    bufs = list(srcs) + (list(lands) if lands is not None else [])
    nb = len(bufs)

    def body(*refs):
        buf_refs, send_sems, recv_sems = refs[:nb], refs[nb + 1], refs[nb + 2]
        token = refs[-1]
        s_refs = buf_refs[:n]
        l_refs = buf_refs[n:] if lands is not None else s_refs
        for k in range(n):
            for p in range(npeer):
                _exchange_copy(s_refs, l_refs, send_sems, recv_sems, k, p, kind).start()
        token[...] = jnp.zeros_like(token)

    out = pl.pallas_call(
        body, name=name,
        out_shape=(pltpu.SemaphoreType.DMA((npeer * n,)), pltpu.SemaphoreType.DMA((npeer * n,)),
                   *[pltpu.HBM(b.shape, b.dtype) for b in bufs], jax.ShapeDtypeStruct((SUBLANE, LANE), F32)),
        in_specs=[_HBM] * nb + [_ANY],
        out_specs=(_SEM, _SEM, *[_HBM] * nb, pl.BlockSpec(memory_space=pltpu.VMEM)),
        input_output_aliases={i: 2 + i for i in range(nb)},
        compiler_params=pltpu.CompilerParams(has_side_effects=_EFFECT),
    )(*[pltpu.with_memory_space_constraint(b, pltpu.HBM) for b in bufs], after)
    send_sems, recv_sems = out[0], out[1]
    thru = out[2:2 + nb]
    return send_sems, recv_sems, list(thru[:n]), (list(thru[n:]) if lands is not None else None), out[-1]


def _exchange_wait(send_sems, recv_sems, srcs, lands, after, *, kind, name):
    n = len(srcs)
    npeer = 1 if kind == "swap" else 3
    bufs = list(srcs) + (list(lands) if lands is not None else [])
    nb = len(bufs)

    def body(*refs):
        buf_refs, send_sems_ref, recv_sems_ref = refs[:nb], refs[nb], refs[nb + 1]
        s_refs = buf_refs[:n]
        l_refs = buf_refs[n:] if lands is not None else s_refs
        for k in range(n):
            for p in range(npeer):
                cp = _exchange_copy(s_refs, l_refs, send_sems_ref, recv_sems_ref, k, p, kind)
                cp.wait_send()
                cp.wait_recv()

    out = pl.pallas_call(
        body, name=name, out_shape=tuple(pltpu.HBM(b.shape, b.dtype) for b in bufs),
        in_specs=[_HBM] * nb + [_SEM, _SEM, _ANY], out_specs=tuple([_HBM] * nb),
        input_output_aliases={i: i for i in range(nb)},
        compiler_params=pltpu.CompilerParams(has_side_effects=_EFFECT),
    )(*bufs, send_sems, recv_sems, after)
    return list(out)


def _sibling_exchange(srcs, *, name):
    n = len(srcs)

    def body(*refs):
        src, out = refs[:n], refs[n:2 * n]
        send_sems, recv_sems = refs[2 * n:]
        sibling = (lax.axis_index("x"), lax.axis_index("y"), 1 - lax.axis_index("c"))
        copies = [pltpu.make_async_remote_copy(src_ref=src[k], dst_ref=out[k], send_sem=send_sems.at[k],
                                               recv_sem=recv_sems.at[k], device_id=sibling, device_id_type=MESH)
                  for k in range(n)]
        for cp in copies:
            cp.start()
        for cp in copies:
            cp.wait_recv()
        for cp in copies:
            cp.wait_send()

    return pl.pallas_call(
        body, name=name, in_specs=[_ANY] * n, out_specs=[_ANY] * n,
        out_shape=[jax.ShapeDtypeStruct(a.shape, a.dtype) for a in srcs],
        scratch_shapes=[pltpu.SemaphoreType.DMA((n,)), pltpu.SemaphoreType.DMA((n,))],
    )(*srcs)


def _all_gather_small(block, *, name):
    m_per, ncol = block.shape

    def body(x_ref, out_ref, send_sems, recv_sems, local_sem):
        x, y, c = lax.axis_index("x"), lax.axis_index("y"), lax.axis_index("c")
        me, sibling = (x, y, c), (x, y, 1 - c)
        chips = [(1 - x, y), (x, 1 - y), (1 - x, 1 - y)]

        def rows(px, py, pc):
            return out_ref.at[pl.ds((4 * px + 2 * py + pc) * m_per, m_per), :]

        def copy(k, blk, to, src=None):
            return pltpu.make_async_remote_copy(
                src_ref=rows(*blk) if src is None else src, dst_ref=rows(*blk),
                send_sem=send_sems.at[k], recv_sem=recv_sems.at[k], device_id=to, device_id_type=MESH)

        mine = pltpu.make_async_copy(x_ref, rows(*me), local_sem)
        mine.start()
        first = [copy(0, me, sibling, src=x_ref)]
        first += [copy(1 + j, me, (*chip, c), src=x_ref) for j, chip in enumerate(chips)]
        for cp in first:
            cp.start()
        passed = [copy(4 + j, (*chip, c), sibling) for j, chip in enumerate(chips)]
        for j, chip in enumerate(chips):
            copy(1 + j, (*chip, c), me).wait_recv()
            passed[j].start()
        copy(0, sibling, me).wait_recv()
        for j, chip in enumerate(chips):
            copy(4 + j, (*chip, 1 - c), me).wait_recv()
        for cp in first + passed:
            cp.wait_send()
        mine.wait()

    return pl.pallas_call(
        body, name=name, out_shape=jax.ShapeDtypeStruct((N_DEV * m_per, ncol), block.dtype),
        in_specs=[pl.BlockSpec(memory_space=pltpu.VMEM)], out_specs=pl.BlockSpec(memory_space=pltpu.VMEM),
        scratch_shapes=[pltpu.SemaphoreType.DMA((7,)), pltpu.SemaphoreType.DMA((7,)), pltpu.SemaphoreType.DMA],
    )(block)


def _rope_angles(pos, dim):
    inv = ROPE_THETA ** (-jnp.arange(0, dim, 2, dtype=F32) / dim)
    return pos.astype(F32)[:, None] * inv[None, :]


def _rope_tables(s):
    pos = jnp.arange(s, dtype=jnp.int32)
    rows = s // GRID_W
    row = jnp.repeat(jnp.arange(rows, dtype=jnp.int32), GRID_W)
    col = jnp.tile(jnp.arange(GRID_W, dtype=jnp.int32), rows)
    a1 = _rope_angles(pos, HEAD_DIM)
    aa = _rope_angles(pos, A_ROPE)
    ar = _rope_angles(row, HEAD_DIM // 2)
    ac = _rope_angles(col, HEAD_DIM // 2)
    one = jnp.ones((s, LANE), F32)
    zero = jnp.zeros((s, LANE), F32)
    pad = LANE - A_ROPE
    cos_a = jnp.concatenate([one, jnp.cos(aa), jnp.cos(aa), jnp.ones((s, pad), F32)], axis=1)
    sin_a = jnp.concatenate([zero, -jnp.sin(aa), jnp.sin(aa), jnp.zeros((s, pad), F32)], axis=1)
    cos_b = jnp.concatenate([jnp.cos(a1), jnp.cos(a1)], axis=1)
    sin_b = jnp.concatenate([-jnp.sin(a1), jnp.sin(a1)], axis=1)
    cos_c = jnp.concatenate([jnp.cos(ar), jnp.cos(ar), jnp.cos(ac), jnp.cos(ac)], axis=1)
    sin_c = jnp.concatenate([-jnp.sin(ar), jnp.sin(ar), -jnp.sin(ac), jnp.sin(ac)], axis=1)
    return (cos_a, sin_a), (cos_b, sin_b), (cos_c, sin_c)


def _band_table(rc, s):
    reach = max((win // (2 * d)) * d for win, d in B_PATTERNS)
    r = -(-reach // rc) * rc
    w = min(s, rc + 2 * r)
    j = jnp.arange(rc, dtype=jnp.int32)[:, None]
    x = jnp.arange(2 * w - rc, dtype=jnp.int32)[None, :]
    rel = x - (w - rc) - j
    mult = jnp.zeros(rel.shape, F32)
    for win, d in B_PATTERNS:
        mult = mult + jnp.logical_and(rel % d == 0, jnp.abs(rel) <= (win // (2 * d)) * d).astype(F32)
    return jnp.where(mult > 0, jnp.log2(jnp.maximum(mult, 1.0)), NEG), (w, r)


_BIG = ("w_in", "a_w_uq", "a_w_ukv", "w_out", "w_gate", "w_up", "w_down")
_SMALL = ("attn_norm", "a_q_norm", "a_kv_norm", "c_q_norm", "c_k_norm", "out_norm", "ffn_norm", "final_norm")
_WEIGHTS = ("attn_norm", "w_in", "a_q_norm", "a_w_uq", "a_kv_norm", "a_w_ukv", "c_q_norm", "c_k_norm", "out_norm",
            "w_out", "ffn_norm", "w_gate", "w_up", "w_down", "final_norm")


_ATTN = ("w_in", "a_w_uq", "a_w_ukv")
_FFN = ("w_out", "w_gate", "w_up", "w_down")


def _from_cols(a):
    return jnp.transpose(a, (1, 0, 2)).reshape(a.shape[1], N_CHIPS * a.shape[2])


def _from_rows(a):
    return a.reshape(N_CHIPS * a.shape[1], a.shape[2])


def _to_cols(a):
    return jnp.transpose(a.reshape(a.shape[0], N_CHIPS, a.shape[1] // N_CHIPS), (1, 0, 2))


def _to_rows(a):
    return a.reshape(N_CHIPS, a.shape[0] // N_CHIPS, a.shape[1])


def _assemble_attn(gw):
    w_in_t, uq, ukv = _from_rows(gw[0]), _from_cols(gw[1]), _from_cols(gw[2])
    d = w_in_t.shape[1]
    w_all = jnp.concatenate([w_in_t[:IN_A], jnp.zeros((A_PAD - IN_A, d), BF16), w_in_t[IN_A:]], axis=0)
    uq = uq.reshape(A_Q_RANK, A_HEADS, A_NOPE + A_ROPE)
    uq = jnp.pad(uq, ((0, 0), (0, 0), (0, A_QK - A_NOPE - A_ROPE))).reshape(A_Q_RANK, A_HEADS * A_QK)
    return dict(w_all=w_all, uq=uq, ukv=ukv)


def _assemble_ffn(gw):
    return dict(w_out=_from_rows(gw[0]), w_gate=gw[1], w_up=gw[2], w_down=_from_rows(gw[3]))


def _split_attn_grads(gl):
    w_all = gl["w_all"]
    w_in_t = jnp.concatenate([w_all[:IN_A], w_all[A_PAD:]], axis=0)
    uq = gl["uq"].reshape(A_Q_RANK, A_HEADS, A_QK)[:, :, :A_NOPE + A_ROPE].reshape(A_Q_RANK, A_HEADS * (A_NOPE + A_ROPE))
    return [_to_rows(w_in_t), _to_cols(uq), _to_cols(gl["ukv"])]


def _split_ffn_grads(gl):
    return [_to_rows(gl["w_out"]), gl["w_gate"], gl["w_up"], _to_rows(gl["w_down"])]


def _tie(a, token):
    return a + token[0:1, 0:1]


def _layer_fwd(x, wl, ffn_weights, sm, tabs, bias, t):
    s = x.shape[0]
    (cos_a, sin_a), (cos_b, sin_b), (cos_c, sin_c) = tabs
    h = _norm_fwd(x, sm["attn_norm"], wb=x.shape[1], cb=0, nb=1, shared_gain=True, out_dtype=BF16, name="attn_norm_fwd")
    p = _matmul(h, wl["w_all"], mode="nt", out_dtype=F32, name="in_proj", tm=1024, tn=1280)
    cq_n = _norm_fwd(p, sm["a_q_norm"], wb=A_Q_RANK, cb=0, nb=1, shared_gain=True, out_dtype=BF16, name="a_q_norm_fwd")
    ckv_n = _norm_fwd(p, sm["a_kv_norm"], wb=A_KV_RANK, cb=1, nb=1, shared_gain=True, out_dtype=BF16, name="a_kv_norm_fwd")
    qa_raw = _matmul(cq_n, wl["uq"], mode="nn", out_dtype=F32, name="a_uq", tm=1024, tn=1024)
    kv = _matmul(ckv_n, wl["ukv"], mode="nn", out_dtype=BF16, name="a_ukv", tm=1024, tn=1024)
    qa =_rope(qa_raw, cos_a, sin_a, tw=A_QK, cb=0, nb=A_HEADS, half=A_ROPE // 2, sign=1, out_dtype=BF16, name="a_rope_q")
    ka = _latent_keys(kv, p, cos_a, sin_a, kr_cb=PB_KR, name="a_keys")
    oa, lse_a = _flash_fwd(qa, ka, kv, None, hkv=A_HEADS, g=1, dqk=A_QK, q_cb=0, k_cb=0, v_cb=1, v_step=2,
                           scale=(A_NOPE + A_ROPE) ** -0.5, tq=t, band=None, name="a_flash_fwd")
    table, band = bias
    qb = _rope(p, cos_b, sin_b, tw=LANE, cb=PB_BQ, nb=B_HEADS, half=HEAD_DIM // 2, sign=1, out_dtype=BF16, name="b_rope_q")
    kb = _rope(p, cos_b, sin_b, tw=LANE, cb=PB_BK, nb=B_HEADS, half=HEAD_DIM // 2, sign=1, out_dtype=BF16, name="b_rope_k")
    vb = _cast_cols(p, cb=PB_BV, nb=B_HEADS, name="b_cast_v")
    ob, lse_b = _flash_fwd(qb, kb, vb, table, hkv=B_HEADS, g=1, dqk=LANE, q_cb=0, k_cb=0, v_cb=0, v_step=1,
                           scale=HEAD_DIM ** -0.5, tq=t, band=band, name="b_flash_fwd")
    qn = _norm_fwd(p, sm["c_q_norm"], wb=LANE, cb=PB_CQH, nb=C_HEADS, shared_gain=True, out_dtype=F32, name="c_q_norm_fwd")
    kn = _norm_fwd(p, sm["c_k_norm"], wb=LANE, cb=PB_CKH, nb=C_KV_HEADS, shared_gain=True, out_dtype=F32, name="c_k_norm_fwd")
    qc = _rope(qn, cos_c, sin_c, tw=LANE, cb=0, nb=C_HEADS, half=HEAD_DIM // 4, sign=1, out_dtype=BF16, name="c_rope_q")
    kc = _rope(kn, cos_c, sin_c, tw=LANE, cb=0, nb=C_KV_HEADS, half=HEAD_DIM // 4, sign=1, out_dtype=BF16, name="c_rope_k")
    vc = _cast_cols(p, cb=PB_CVH, nb=C_KV_HEADS, name="c_cast_v")
    oc, lse_c = _flash_fwd(qc, kc, vc, None, hkv=C_KV_HEADS, g=C_GROUP, dqk=LANE, q_cb=0, k_cb=0, v_cb=0, v_step=1,
                           scale=HEAD_DIM ** -0.5, tq=t, band=None, name="c_flash_fwd")
    g_out = sm["out_norm"]
    ga, gb, gc = g_out[:, :A_WIDTH], g_out[:, A_WIDTH:A_WIDTH + B_WIDTH], g_out[:, A_WIDTH + B_WIDTH:]
    ya = _norm_fwd(oa, ga, wb=A_WIDTH, cb=0, nb=1, shared_gain=True, out_dtype=BF16, name="out_norm_a_fwd")
    yb = _norm_fwd(ob, gb, wb=B_WIDTH, cb=0, nb=1, shared_gain=True, out_dtype=BF16, name="out_norm_b_fwd")
    yc = _norm_fwd(oc, gc, wb=C_WIDTH, cb=0, nb=1, shared_gain=True, out_dtype=BF16, name="out_norm_c_fwd")
    y = jnp.concatenate([ya, yb, yc], axis=1)
    wl = {**wl, **ffn_weights(y)}
    x1 = _matmul(y, wl["w_out"], mode="nn", out_dtype=F32, name="out_proj", add=x, tm=1024, tn=1024)
    h2 = _norm_fwd(x1, sm["ffn_norm"], wb=x.shape[1], cb=0, nb=1, shared_gain=True, out_dtype=BF16, name="ffn_norm_fwd")
    gate, up, act = _ffn_up(h2, wl["w_gate"], wl["w_up"], name="ffn_up")
    x2 = _matmul(act, wl["w_down"], mode="nn", out_dtype=F32, name="ffn_down", add=x1, tm=1024, tn=512)
    saved = dict(x=x, h=h, p=p, cq_n=cq_n, ckv_n=ckv_n, kv=kv, qa=qa, ka=ka, oa=oa, lse_a=lse_a, qb=qb, kb=kb, vb=vb, ob=ob,
                 lse_b=lse_b, qc=qc, kc=kc, vc=vc, oc=oc, lse_c=lse_c, y=y, x1=x1, h2=h2, gate=gate, up=up, act=act)
    return x2, saved, wl


def _layer_bwd(dx2, dx2b, sv, wl, sm, tabs, bias, t, send_ffn, send_attn):
    s, d = dx2.shape
    (cos_a, sin_a), (cos_b, sin_b), (cos_c, sin_c) = tabs
    gw, gs = {}, {}
    dgate, dup = _ffn_down_dx(dx2b, wl["w_down"], sv["gate"], sv["up"], name="ffn_down_dx")
    gw["w_down"] = _matmul(sv["act"], dx2b, mode="tn", out_dtype=BF16, name="ffn_down_dw", tm=512, tn=2048)
    dh2 = _ffn_up_dx(dgate, dup, wl["w_gate"], wl["w_up"], name="ffn_up_dx")
    gw["w_gate"] = _matmul(sv["h2"], dgate, mode="tn", out_dtype=BF16, name="ffn_gate_dw", tm=1024, col_shards=True)
    gw["w_up"] = _matmul(sv["h2"], dup, mode="tn", out_dtype=BF16, name="ffn_up_dw", tm=1024, col_shards=True)
    dx1, gs["ffn_norm"], dx1b = _norm_bwd(sv["x1"], sm["ffn_norm"], dh2, wb=d, cb=0, nb=1, shared_gain=True,
                                          out_dtype=F32, name="ffn_norm_bwd", add=dx2, bf16_copy=True)
    w_out = wl["w_out"]
    dya = _matmul(dx1b, w_out[:A_WIDTH], mode="nt", out_dtype=F32, name="out_proj_dx_a", tm=1024, tn=1024)
    dyb = _matmul(dx1b, w_out[A_WIDTH:A_WIDTH + B_WIDTH], mode="nt", out_dtype=F32, name="out_proj_dx_b", tm=1024, tn=1024)
    dyc = _matmul(dx1b, w_out[A_WIDTH + B_WIDTH:], mode="nt", out_dtype=F32, name="out_proj_dx_c", tm=1024, tn=1024)
    gw["w_out"] = _matmul(sv["y"], dx1b, mode="tn", out_dtype=BF16, name="out_proj_dw", tm=512, tn=2048)
    token = send_ffn(gw)
    g_out = _tie(sm["out_norm"], token)
    ga, gb, gc = g_out[:, :A_WIDTH], g_out[:, A_WIDTH:A_WIDTH + B_WIDTH], g_out[:, A_WIDTH + B_WIDTH:]
    doa, dga = _norm_bwd(sv["oa"], ga, dya, wb=A_WIDTH, cb=0, nb=1, shared_gain=True, out_dtype=F32, name="out_norm_a_bwd")
    dob, dgb = _norm_bwd(sv["ob"], gb, dyb, wb=B_WIDTH, cb=0, nb=1, shared_gain=True, out_dtype=F32, name="out_norm_b_bwd")
    doc, dgc = _norm_bwd(sv["oc"], gc, dyc, wb=C_WIDTH, cb=0, nb=1, shared_gain=True, out_dtype=F32, name="out_norm_c_bwd")
    gs["out_norm"] = jnp.concatenate([dga, dgb, dgc], axis=1)
    p = sv["p"]
    dqc, dkc, dvc = _flash_bwd(sv["qc"], sv["kc"], sv["vc"], sv["oc"], doc, sv["lse_c"], None, hkv=C_KV_HEADS,
                               g=C_GROUP, dqk=LANE, q_cb=0, k_cb=0, v_cb=0, v_step=1, scale=HEAD_DIM ** -0.5,
                               tq=t, band=None, name="c_flash_bwd")
    dqn = _rope(dqc, cos_c, sin_c, tw=LANE, cb=0, nb=C_HEADS, half=HEAD_DIM // 4, sign=-1, out_dtype=F32, name="c_rope_q_bwd")
    dkn = _rope(dkc, cos_c, sin_c, tw=LANE, cb=0, nb=C_KV_HEADS, half=HEAD_DIM // 4, sign=-1, out_dtype=F32, name="c_rope_k_bwd")
    dpcq, gs["c_q_norm"] = _norm_bwd(p, sm["c_q_norm"], dqn, wb=LANE, cb=PB_CQH, nb=C_HEADS, shared_gain=True,
                                     out_dtype=BF16, name="c_q_norm_bwd")
    dpck, gs["c_k_norm"] = _norm_bwd(p, sm["c_k_norm"], dkn, wb=LANE, cb=PB_CKH, nb=C_KV_HEADS, shared_gain=True,
                                     out_dtype=BF16, name="c_k_norm_bwd")
    table, band = bias
    dqb, dkb, dvb = _flash_bwd(sv["qb"], sv["kb"], sv["vb"], sv["ob"], dob, sv["lse_b"], table, hkv=B_HEADS, g=1,
                               dqk=LANE, q_cb=0, k_cb=0, v_cb=0, v_step=1, scale=HEAD_DIM ** -0.5, tq=t, band=band,
                               name="b_flash_bwd")
    dpbq = _rope(dqb, cos_b, sin_b, tw=LANE, cb=0, nb=B_HEADS, half=HEAD_DIM // 2, sign=-1, out_dtype=BF16, name="b_rope_q_bwd")
    dpbk = _rope(dkb, cos_b, sin_b, tw=LANE, cb=0, nb=B_HEADS, half=HEAD_DIM // 2, sign=-1, out_dtype=BF16, name="b_rope_k_bwd")
    dqa, dka, dva = _flash_bwd(sv["qa"], sv["ka"], sv["kv"], sv["oa"], doa, sv["lse_a"], None, hkv=A_HEADS, g=1,
                               dqk=A_QK, q_cb=0, k_cb=0, v_cb=1, v_step=2, scale=(A_NOPE + A_ROPE) ** -0.5,
                               tq=t, band=None, name="a_flash_bwd")
    dqa_raw = _rope(dqa, cos_a, sin_a, tw=A_QK, cb=0, nb=A_HEADS, half=A_ROPE // 2, sign=-1, out_dtype=BF16, name="a_rope_q_bwd")
    dkv, dkr = _latent_keys_bwd(dka, dva, cos_a, sin_a, name="a_keys_bwd")
    dckv_n = _matmul(dkv, wl["ukv"], mode="nt", out_dtype=F32, name="a_ukv_dx", tm=1024, tn=512)
    gw["ukv"] = _matmul(sv["ckv_n"], dkv, mode="tn", out_dtype=BF16, name="a_ukv_dw", tm=512, tn=1024)
    dcq_n = _matmul(dqa_raw, wl["uq"], mode="nt", out_dtype=F32, name="a_uq_dx", tm=1024, tn=512)
    gw["uq"] = _matmul(sv["cq_n"], dqa_raw, mode="tn", out_dtype=BF16, name="a_uq_dw", tm=512, tn=1024)
    dcq, gs["a_q_norm"] = _norm_bwd(p, sm["a_q_norm"], dcq_n, wb=A_Q_RANK, cb=0, nb=1, shared_gain=True, out_dtype=BF16,
                                    name="a_q_norm_bwd")
    dckv, gs["a_kv_norm"] = _norm_bwd(p, sm["a_kv_norm"], dckv_n, wb=A_KV_RANK, cb=1, nb=1, shared_gain=True,
                                      out_dtype=BF16, name="a_kv_norm_bwd")
    dp = jnp.concatenate([dcq, dckv, dkr, jnp.zeros((s, A_PAD - (PB_KR + 1) * LANE), BF16), dpbq, dpbk,
                          dvb.astype(BF16), dpcq, dpck, dvc.astype(BF16)], axis=1)
    gw["w_all"] = _matmul(dp, sv["h"], mode="tn", out_dtype=BF16, name="in_proj_dw", tm=640, tn=2048)
    token = send_attn(gw)
    dh = _matmul(dp, wl["w_all"], mode="nn", out_dtype=F32, name="in_proj_dx", tm=1024, tn=512, after=token)
    dx, gs["attn_norm"], dxb = _norm_bwd(sv["x"], sm["attn_norm"], dh, wb=d, cb=0, nb=1, shared_gain=True,
                                         out_dtype=F32, name="attn_norm_bwd", add=dx1, bf16_copy=True)
    return dx, dxb, gs, token


def _pack_small(vals):
    flat = jnp.concatenate([vals[n].reshape(-1).astype(F32) for n in _SMALL])
    tile = SUBLANE * LANE
    padded = -(-flat.shape[0] // tile) * tile
    return jnp.pad(flat, (0, padded - flat.shape[0])).reshape(padded // LANE, LANE)


def _unpack_small(packed, like):
    flat = packed.reshape(-1)
    out, off = {}, 0
    for n in _SMALL:
        size = math.prod(like[n].shape)
        out[n] = flat[off:off + size].reshape(like[n].shape)
        off += size
    return out


def kernel(x, attn_norm, w_in, a_q_norm, a_w_uq, a_kv_norm, a_w_ukv, c_q_norm, c_k_norm, out_norm, w_out, ffn_norm, w_gate, w_up, w_down, final_norm, loss_target, m_attn_norm, m_w_in, m_a_q_norm, m_a_w_uq, m_a_kv_norm, m_a_w_ukv, m_c_q_norm, m_c_k_norm, m_out_norm, m_w_out, m_ffn_norm, m_w_gate, m_w_up, m_w_down, m_final_norm, v_attn_norm, v_w_in, v_a_q_norm, v_a_w_uq, v_a_kv_norm, v_a_w_ukv, v_c_q_norm, v_c_k_norm, v_out_norm, v_w_out, v_ffn_norm, v_w_gate, v_w_up, v_w_down, v_final_norm):
    w = dict(attn_norm=attn_norm, w_in=w_in, a_q_norm=a_q_norm, a_w_uq=a_w_uq, a_kv_norm=a_kv_norm, a_w_ukv=a_w_ukv,
             c_q_norm=c_q_norm, c_k_norm=c_k_norm, out_norm=out_norm, w_out=w_out, ffn_norm=ffn_norm, w_gate=w_gate,
             w_up=w_up, w_down=w_down, final_norm=final_norm)
    m = dict(attn_norm=m_attn_norm, w_in=m_w_in, a_q_norm=m_a_q_norm, a_w_uq=m_a_w_uq, a_kv_norm=m_a_kv_norm,
             a_w_ukv=m_a_w_ukv, c_q_norm=m_c_q_norm, c_k_norm=m_c_k_norm, out_norm=m_out_norm, w_out=m_w_out,
             ffn_norm=m_ffn_norm, w_gate=m_w_gate, w_up=m_w_up, w_down=m_w_down, final_norm=m_final_norm)
    v = dict(attn_norm=v_attn_norm, w_in=v_w_in, a_q_norm=v_a_q_norm, a_w_uq=v_a_w_uq, a_kv_norm=v_a_kv_norm,
             a_w_ukv=v_a_w_ukv, c_q_norm=v_c_q_norm, c_k_norm=v_c_k_norm, out_norm=v_out_norm, w_out=v_w_out,
             ffn_norm=v_ffn_norm, w_gate=v_w_gate, w_up=v_w_up, w_down=v_w_down, final_norm=v_final_norm)
    _, s, d = x.shape
    depth = attn_norm.shape[0]

    def as_stored(a, n):
        return jnp.swapaxes(a, 1, 2) if n == "w_in" else a
    t = _pick(s, 1024)

    me = (2 * lax.axis_index("x") + lax.axis_index("y")).astype(jnp.int32).reshape(1)

    gathers, after = {}, me
    for l in range(depth):
        for group, names in (("attn", _ATTN), ("ffn", _FFN)):
            bufs = [_cast_to_slot(as_stored(w[n], n), me, layer=l, name=f"cast_{n}")
                    for n in names]
            send_sems, recv_sems, bufs, _, after = _exchange_start(bufs, None, after, kind="gather",
                                                                   name=f"gather_start_{group}{l}")
            gathers[group, l] = (send_sems, recv_sems, bufs)
    all_started = after

    def gathered(group, l, after):
        send_sems, recv_sems, bufs = gathers[group, l]
        return _exchange_wait(send_sems, recv_sems, bufs, None, after, kind="gather", name=f"gather_wait_{group}{l}")

    tabs = _rope_tables(s)
    bias = _band_table(min(t, ATTN_ROW_CHUNK), s)

    xs = x.reshape(s, d)
    saved, wls, sms = [], [], []
    for l in range(depth):
        wl = _assemble_attn(gathered("attn", l, all_started if l == 0 else xs))
        sm = {n: w[n][l][None, :] for n in _SMALL if n != "final_norm"}
        xs, sv, wl = _layer_fwd(xs, wl, lambda after, l=l: _assemble_ffn(gathered("ffn", l, after)), sm, tabs, bias, t)
        saved.append(sv)
        wls.append(wl)
        sms.append(sm)
    dx, g_final, loss_row, dxb = _final_loss(xs, final_norm[None, :], loss_target.reshape(s, d), name="final_loss")
    loss = lax.psum(loss_row[0, 0], ("x", "y", "c"))

    sends = {}

    def send(group, l, srcs, after):
        lands = [lax.empty((3,) + a.shape[1:], BF16) for a in srcs]
        send_sems, recv_sems, srcs, lands, token = _exchange_start(srcs, lands, after, kind="scatter",
                                                                   name=f"scatter_start_{group}{l}")
        sends[group, l] = (send_sems, recv_sems, srcs, lands)
        return token

    gs_layers, token = [None] * depth, all_started
    for l in reversed(range(depth)):
        dx, dxb, gs_layers[l], token = _layer_bwd(
            dx, dxb, saved[l], wls[l], sms[l], tabs, bias, t,
            lambda gw, l=l, tk=token: send("ffn", l, _split_ffn_grads(gw), tk),
            lambda gw, l=l: send("attn", l, _split_attn_grads(gw), dx))
    grad_x = dx.reshape(x.shape)

    srcs, lands = {}, {}

    def arrive(key, after):
        send_sems, recv_sems, s_bufs, l_bufs = sends[key]
        got = _exchange_wait(send_sems, recv_sems, s_bufs, l_bufs, after, kind="scatter",
                             name=f"scatter_wait_{key[0]}{key[1]}")
        for k, n in enumerate(_ATTN if key[0] == "attn" else _FFN):
            srcs[n, key[1]], lands[n, key[1]] = got[k], got[len(s_bufs) + k]

    def summed(names):
        return [_sum_parts([srcs[n, l] for l in range(depth)], [lands[n, l] for l in range(depth)], me, name="sum_" + n)
                for n in names]

    last = ("attn", 0)
    for key in sends:
        if key != last:
            arrive(key, token)
    grads, deltas, new_m, new_v = {}, {}, {}, {}

    def update(n, mine, other):
        res = _adamw(mine, other, as_stored(w[n], n), as_stored(m[n], n), as_stored(v[n], n), name="adamw_" + n)
        grads[n], deltas[n], new_m[n], new_v[n] = [as_stored(r, n) for r in res]
        return res[0]

    sums_ffn = summed(_FFN)
    halves, after = [_FFN[:2], _FFN[2:]], token
    swaps = []
    for k, names in enumerate(halves):
        part = [sums_ffn[_FFN.index(n)] for n in names]
        swaps.append(_exchange_start(part, [lax.empty(a.shape, F32) for a in part], after, kind="swap",
                                     name=f"swap_start_ffn{k}"))
        after = swaps[-1][4]
    for k, names in enumerate(halves):
        sw = swaps[k]
        got = _exchange_wait(sw[0], sw[1], sw[2], sw[3], after, kind="swap", name=f"swap_wait_ffn{k}")
        for j, n in enumerate(names):
            after = update(n, got[j], got[len(names) + j])
    arrive(last, after)
    sums_attn = summed(_ATTN)
    sib_attn = list(_sibling_exchange(sums_attn, name="swap_core_sums_attn"))
    for n, mine, other in zip(_ATTN, sums_attn, sib_attn):
        update(n, mine, other)

    gsm = {n: jnp.stack([gs_layers[l][n][0] for l in range(depth)]) for n in _SMALL if n != "final_norm"}
    gsm["final_norm"] = g_final[0]
    packed = _pack_small(gsm)
    everyone = _all_gather_small(packed, name="gather_gain_grads").reshape(N_DEV, packed.shape[0], LANE)
    res = _small_adamw(everyone, _pack_small(w), _pack_small(m), _pack_small(v), name="adamw_gains")
    for dst, r in zip((grads, deltas, new_m, new_v), res):
        dst.update(_unpack_small(r, w))

    return (loss, grad_x, *[grads[n] for n in _WEIGHTS], *[deltas[n] for n in _WEIGHTS],
            *[new_m[n] for n in _WEIGHTS], *[new_v[n] for n in _WEIGHTS])
```
